```python
import math
import jax, jax.numpy as jnp
from jax import lax
import numpy as np

D_MODEL = 2048
BATCH = 8
SEQ = 2048
DEPTH = 1

ATTN_HEADS = 16
HEAD_DIM = 128
ATTN_W = ATTN_HEADS * HEAD_DIM
DILATED_GROUPS = ((128, 1), (512, 4), (2048, 16))
BLK = 128
LRU_W = D_MODEL
LRU_HEADS = 16
LRU_BLK = LRU_W // LRU_HEADS
CONV_W = 4
LRU_C = 8.0
D_FF = 4 * D_MODEL
EPS = 1e-6
IN_SPLITS = (ATTN_W, ATTN_W, ATTN_W, LRU_W, LRU_W, D_MODEL, D_MODEL)
IN_COLS = sum(IN_SPLITS)

kernel_name = "hybrid_dilated_attn_rglru_gated_block"


def rms_norm(x, g):
    xf = x.astype(jnp.float32)
    y = xf * lax.rsqrt(jnp.mean(xf * xf, axis=-1, keepdims=True) + EPS)
    return (y * g.astype(jnp.float32)).astype(x.dtype)


def alibi_slopes(n_heads):
    return 2.0 ** (-8.0 * jnp.arange(1, n_heads + 1, dtype=jnp.float32) / n_heads)


def dilated_group_attention(q, k, v, slopes, window, dilation):
    b, s, h, dh = q.shape
    d = dilation
    span = window // d
    u_len = s // d
    n_blk = -(-u_len // BLK)
    u_pad = n_blk * BLK
    qs = q.reshape(b, u_len, d, h, dh)
    ks = k.reshape(b, u_len, d, h, dh)
    vs = v.reshape(b, u_len, d, h, dh)
    qb = jnp.pad(qs, ((0, 0), (0, u_pad - u_len), (0, 0), (0, 0), (0, 0))).reshape(b, n_blk, BLK, d, h, dh)
    pad_kv = ((0, 0), (BLK, u_pad - u_len), (0, 0), (0, 0), (0, 0))
    kp = jnp.pad(ks, pad_kv).reshape(b, n_blk + 1, BLK, d, h, dh)
    vp = jnp.pad(vs, pad_kv).reshape(b, n_blk + 1, BLK, d, h, dh)
    kwin = jnp.concatenate([kp[:, :-1], kp[:, 1:]], axis=2)
    vwin = jnp.concatenate([vp[:, :-1], vp[:, 1:]], axis=2)
    scores = jnp.einsum('bnqrhe,bnkrhe->bnrhqk', qb, kwin).astype(jnp.float32) * (HEAD_DIM ** -0.5)
    i = jnp.arange(BLK)[:, None]
    j = jnp.arange(2 * BLK)[None, :]
    diff = BLK + i - j
    band = (diff >= 0) & (diff <= span)
    n_idx = jnp.arange(n_blk)[:, None, None]
    valid = band[None] & ((n_idx - 1) * BLK + j[None] >= 0)
    bias = -(slopes * d)[:, None, None] * diff.astype(jnp.float32)[None]
    scores = scores + bias[None, None, None]
    scores = jnp.where(valid[None, :, None, None], scores, -jnp.inf)
    m = jnp.max(scores, axis=-1, keepdims=True)
    p = jnp.exp(scores - m)
    l = jnp.sum(p, axis=-1)
    o = jnp.einsum('bnrhqk,bnkrhe->bnqrhe', p.astype(v.dtype), vwin).astype(jnp.float32)
    l_t = jnp.transpose(l, (0, 1, 4, 2, 3))
    o = o / l_t[..., None]
    lse = jnp.transpose(m[..., 0] + jnp.log(l), (0, 1, 4, 2, 3))
    o = o.reshape(b, u_pad, d, h, dh)[:, :u_len].reshape(b, s, h, dh)
    lse = lse.reshape(b, u_pad, d, h)[:, :u_len].reshape(b, s, h)
    return o, lse


def dilated_attention(q, k, v):
    b, s, _ = q.shape
    q = q.reshape(b, s, ATTN_HEADS, HEAD_DIM)
    k = k.reshape(b, s, ATTN_HEADS, HEAD_DIM)
    v = v.reshape(b, s, ATTN_HEADS, HEAD_DIM)
    slopes = alibi_slopes(ATTN_HEADS)
    outs, lses = [], []
    for window, dilation in DILATED_GROUPS:
        o, lse = dilated_group_attention(q, k, v, slopes, window, dilation)
        outs.append(o)
        lses.append(lse)
    w = jax.nn.softmax(jnp.stack(lses, axis=0), axis=0)
    out = jnp.sum(w[..., None] * jnp.stack(outs, axis=0), axis=0)
    return out.reshape(b, s, ATTN_W).astype(q.dtype)


def rg_lru_branch(xr, gate, conv_w, conv_b, wa, ba, wx, bx, lam):
    b, s, c = xr.shape
    xc = lax.conv_general_dilated(
        xr, conv_w.reshape(CONV_W, 1, c).astype(xr.dtype), window_strides=(1,),
        padding=[(CONV_W - 1, 0)], dimension_numbers=('NWC', 'WIO', 'NWC'),
        feature_group_count=c) + conv_b
    xh = xc.reshape(b, s, LRU_HEADS, LRU_BLK)
    r = jax.nn.sigmoid((jnp.einsum('bshi,hij->bshj', xh, wa).reshape(b, s, c) + ba).astype(jnp.float32))
    ig = jax.nn.sigmoid((jnp.einsum('bshi,hij->bshj', xh, wx).reshape(b, s, c) + bx).astype(jnp.float32))
    log_a = -LRU_C * r * jax.nn.softplus(-lam.astype(jnp.float32))
    a = jnp.exp(log_a)
    mult = jnp.sqrt(-jnp.expm1(2.0 * log_a))
    u = mult * (ig * xc.astype(jnp.float32))

    def combine(c1, c2):
        a1, b1 = c1
        a2, b2 = c2
        return a1 * a2, a2 * b1 + b2

    _, hseq = lax.associative_scan(combine, (a, u), axis=1)
    return (hseq * jax.nn.gelu(gate.astype(jnp.float32))).astype(xr.dtype)


def _fwd_setup_inputs(seed: int = 0) -> dict:
    key = jax.random.key(seed)
    ks = jax.random.split(key, 20)
    f32 = jnp.float32
    nrm = lambda k, shape, scale: jax.random.normal(k, shape, f32) * scale
    a8 = jax.random.uniform(ks[9], (DEPTH, LRU_W), f32, 0.9, 0.999)
    a_base = a8 ** (1.0 / LRU_C)
    lru_lambda = jnp.log(a_base) - jnp.log1p(-a_base)
    return {
        "x": jax.random.normal(ks[0], (BATCH, SEQ, D_MODEL), f32),
        "norm_mix_g": 1.0 + nrm(ks[1], (DEPTH, D_MODEL), 0.02),
        "w_in": nrm(ks[2], (DEPTH, D_MODEL, IN_COLS), D_MODEL ** -0.5),
        "conv_w": nrm(ks[3], (DEPTH, CONV_W, LRU_W), CONV_W ** -0.5),
        "conv_b": nrm(ks[4], (DEPTH, LRU_W), 0.02),
        "lru_wa": nrm(ks[5], (DEPTH, LRU_HEADS, LRU_BLK, LRU_BLK), LRU_BLK ** -0.5),
        "lru_ba": nrm(ks[6], (DEPTH, LRU_W), 0.02),
        "lru_wx": nrm(ks[7], (DEPTH, LRU_HEADS, LRU_BLK, LRU_BLK), LRU_BLK ** -0.5),
        "lru_bx": nrm(ks[8], (DEPTH, LRU_W), 0.02),
        "lru_lambda": lru_lambda,
        "w_proj_attn": nrm(ks[10], (DEPTH, ATTN_W, D_MODEL), ATTN_W ** -0.5),
        "w_proj_lru": nrm(ks[11], (DEPTH, LRU_W, D_MODEL), LRU_W ** -0.5),
        "w_out": nrm(ks[12], (DEPTH, D_MODEL, D_MODEL), D_MODEL ** -0.5),
        "norm_mlp_g": 1.0 + nrm(ks[13], (DEPTH, D_MODEL), 0.02),
        "w_up": nrm(ks[14], (DEPTH, D_MODEL, D_FF), D_MODEL ** -0.5),
        "w_down": nrm(ks[15], (DEPTH, D_FF, D_MODEL), D_FF ** -0.5),
        "norm_final_g": 1.0 + nrm(ks[16], (D_MODEL,), 0.02),
    }


def _fwd_reference(x, norm_mix_g, w_in, conv_w, conv_b, lru_wa, lru_ba, lru_wx, lru_bx, lru_lambda,
              w_proj_attn, w_proj_lru, w_out, norm_mlp_g, w_up, w_down, norm_final_g):
    h = x
    split_idx = list(np.cumsum(IN_SPLITS)[:-1])
    for l in range(DEPTH):
        xn = rms_norm(h, norm_mix_g[l])
        proj = jnp.einsum('bsd,dc->bsc', xn, w_in[l])
        q, k, v, xr, xg, g_attn, g_lru = jnp.split(proj, split_idx, axis=-1)
        y_attn = dilated_attention(q, k, v)
        y_lru = rg_lru_branch(xr, xg, conv_w[l], conv_b[l], lru_wa[l], lru_ba[l],
                              lru_wx[l], lru_bx[l], lru_lambda[l])
        merged = (jax.nn.sigmoid(g_attn) * jnp.einsum('bsc,cd->bsd', y_attn, w_proj_attn[l])
                  + jax.nn.sigmoid(g_lru) * jnp.einsum('bsc,cd->bsd', y_lru, w_proj_lru[l]))
        h = h + jnp.einsum('bsd,de->bse', merged, w_out[l])
        hn = rms_norm(h, norm_mlp_g[l])
        hid = jnp.square(jax.nn.relu(jnp.einsum('bsd,df->bsf', hn, w_up[l])))
        h = h + jnp.einsum('bsf,fd->bsd', hid, w_down[l])
    return rms_norm(h, norm_final_g)


import jax as _jax
import jax.numpy as _jnp

TWIN_FORMAT = 'train_step'
FWD_PARAMS = ['x', 'norm_mix_g', 'w_in', 'conv_w', 'conv_b', 'lru_wa', 'lru_ba', 'lru_wx', 'lru_bx', 'lru_lambda', 'w_proj_attn', 'w_proj_lru', 'w_out', 'norm_mlp_g', 'w_up', 'w_down', 'norm_final_g']
TWIN_WEIGHTS = ['norm_mix_g', 'w_in', 'conv_w', 'conv_b', 'lru_wa', 'lru_ba', 'lru_wx', 'lru_bx', 'lru_lambda', 'w_proj_attn', 'w_proj_lru', 'w_out', 'norm_mlp_g', 'w_up', 'w_down', 'norm_final_g']
TWIN_DIFF_INPUT = 'x'
TWIN_INPUTS = ['x', 'norm_mix_g', 'w_in', 'conv_w', 'conv_b', 'lru_wa', 'lru_ba', 'lru_wx', 'lru_bx', 'lru_lambda', 'w_proj_attn', 'w_proj_lru', 'w_out', 'norm_mlp_g', 'w_up', 'w_down', 'norm_final_g', 'loss_target', 'm_norm_mix_g', 'm_w_in', 'm_conv_w', 'm_conv_b', 'm_lru_wa', 'm_lru_ba', 'm_lru_wx', 'm_lru_bx', 'm_lru_lambda', 'm_w_proj_attn', 'm_w_proj_lru', 'm_w_out', 'm_norm_mlp_g', 'm_w_up', 'm_w_down', 'm_norm_final_g', 'v_norm_mix_g', 'v_w_in', 'v_conv_w', 'v_conv_b', 'v_lru_wa', 'v_lru_ba', 'v_lru_wx', 'v_lru_bx', 'v_lru_lambda', 'v_w_proj_attn', 'v_w_proj_lru', 'v_w_out', 'v_norm_mlp_g', 'v_w_up', 'v_w_down', 'v_norm_final_g']
TWIN_OUTPUTS = ['loss', 'grad_x', 'grad_norm_mix_g', 'grad_w_in', 'grad_conv_w', 'grad_conv_b', 'grad_lru_wa', 'grad_lru_ba', 'grad_lru_wx', 'grad_lru_bx', 'grad_lru_lambda', 'grad_w_proj_attn', 'grad_w_proj_lru', 'grad_w_out', 'grad_norm_mlp_g', 'grad_w_up', 'grad_w_down', 'grad_norm_final_g', 'delta_norm_mix_g', 'delta_w_in', 'delta_conv_w', 'delta_conv_b', 'delta_lru_wa', 'delta_lru_ba', 'delta_lru_wx', 'delta_lru_bx', 'delta_lru_lambda', 'delta_w_proj_attn', 'delta_w_proj_lru', 'delta_w_out', 'delta_norm_mlp_g', 'delta_w_up', 'delta_w_down', 'delta_norm_final_g', 'new_m_norm_mix_g', 'new_m_w_in', 'new_m_conv_w', 'new_m_conv_b', 'new_m_lru_wa', 'new_m_lru_ba', 'new_m_lru_wx', 'new_m_lru_bx', 'new_m_lru_lambda', 'new_m_w_proj_attn', 'new_m_w_proj_lru', 'new_m_w_out', 'new_m_norm_mlp_g', 'new_m_w_up', 'new_m_w_down', 'new_m_norm_final_g', 'new_v_norm_mix_g', 'new_v_w_in', 'new_v_conv_w', 'new_v_conv_b', 'new_v_lru_wa', 'new_v_lru_ba', 'new_v_lru_wx', 'new_v_lru_bx', 'new_v_lru_lambda', 'new_v_w_proj_attn', 'new_v_w_proj_lru', 'new_v_w_out', 'new_v_norm_mlp_g', 'new_v_w_up', 'new_v_w_down', 'new_v_norm_final_g']
TWIN_LEAF_KINDS = {'loss': 'loss', 'grad_x': 'grad_x', 'grad_norm_mix_g': 'grad_w', 'grad_w_in': 'grad_w', 'grad_conv_w': 'grad_w', 'grad_conv_b': 'grad_w', 'grad_lru_wa': 'grad_w', 'grad_lru_ba': 'grad_w', 'grad_lru_wx': 'grad_w', 'grad_lru_bx': 'grad_w', 'grad_lru_lambda': 'grad_w', 'grad_w_proj_attn': 'grad_w', 'grad_w_proj_lru': 'grad_w', 'grad_w_out': 'grad_w', 'grad_norm_mlp_g': 'grad_w', 'grad_w_up': 'grad_w', 'grad_w_down': 'grad_w', 'grad_norm_final_g': 'grad_w', 'delta_norm_mix_g': 'delta_w', 'delta_w_in': 'delta_w', 'delta_conv_w': 'delta_w', 'delta_conv_b': 'delta_w', 'delta_lru_wa': 'delta_w', 'delta_lru_ba': 'delta_w', 'delta_lru_wx': 'delta_w', 'delta_lru_bx': 'delta_w', 'delta_lru_lambda': 'delta_w', 'delta_w_proj_attn': 'delta_w', 'delta_w_proj_lru': 'delta_w', 'delta_w_out': 'delta_w', 'delta_norm_mlp_g': 'delta_w', 'delta_w_up': 'delta_w', 'delta_w_down': 'delta_w', 'delta_norm_final_g': 'delta_w', 'new_m_norm_mix_g': 'new_m', 'new_m_w_in': 'new_m', 'new_m_conv_w': 'new_m', 'new_m_conv_b': 'new_m', 'new_m_lru_wa': 'new_m', 'new_m_lru_ba': 'new_m', 'new_m_lru_wx': 'new_m', 'new_m_lru_bx': 'new_m', 'new_m_lru_lambda': 'new_m', 'new_m_w_proj_attn': 'new_m', 'new_m_w_proj_lru': 'new_m', 'new_m_w_out': 'new_m', 'new_m_norm_mlp_g': 'new_m', 'new_m_w_up': 'new_m', 'new_m_w_down': 'new_m', 'new_m_norm_final_g': 'new_m', 'new_v_norm_mix_g': 'new_v', 'new_v_w_in': 'new_v', 'new_v_conv_w': 'new_v', 'new_v_conv_b': 'new_v', 'new_v_lru_wa': 'new_v', 'new_v_lru_ba': 'new_v', 'new_v_lru_wx': 'new_v', 'new_v_lru_bx': 'new_v', 'new_v_lru_lambda': 'new_v', 'new_v_w_proj_attn': 'new_v', 'new_v_w_proj_lru': 'new_v', 'new_v_w_out': 'new_v', 'new_v_norm_mlp_g': 'new_v', 'new_v_w_up': 'new_v', 'new_v_w_down': 'new_v', 'new_v_norm_final_g': 'new_v'}


def _forward(args):
    return _fwd_reference(*[args[k] for k in FWD_PARAMS])


def _output_shape():
    out = _jax.eval_shape(lambda: _forward(_fwd_setup_inputs(0)))
    return out.shape, out.dtype

N_MICROBATCH = 1
ADAM_LR = 0.001
ADAM_B1 = 0.9
ADAM_B2 = 0.999
ADAM_EPS = 1e-08
ADAM_WD = 0.01
ADAM_STEP = 10
PER_EXAMPLE_BATCH_AXIS = {'x': 0, 'loss_target': 0}
SHARED_INPUTS = []
_WEIGHT_DTYPES = {'norm_mix_g': _jnp.float32, 'w_in': _jnp.float32, 'conv_w': _jnp.float32, 'conv_b': _jnp.float32, 'lru_wa': _jnp.float32, 'lru_ba': _jnp.float32, 'lru_wx': _jnp.float32, 'lru_bx': _jnp.float32, 'lru_lambda': _jnp.float32, 'w_proj_attn': _jnp.float32, 'w_proj_lru': _jnp.float32, 'w_out': _jnp.float32, 'norm_mlp_g': _jnp.float32, 'w_up': _jnp.float32, 'w_down': _jnp.float32, 'norm_final_g': _jnp.float32}
MOMENT_SCALE = {'norm_mix_g': 2.936898e-02, 'w_in': 1.076281e-02, 'conv_w': 1.513226e-02, 'conv_b': 1.661684e-01, 'lru_wa': 4.790886e-03, 'lru_ba': 4.181291e-03, 'lru_wx': 8.576943e-03, 'lru_bx': 5.432366e-03, 'lru_lambda': 8.584181e-03, 'w_proj_attn': 1.466021e-02, 'w_proj_lru': 1.506777e-02, 'w_out': 1.988178e-02, 'norm_mlp_g': 5.421030e-02, 'w_up': 2.760148e-02, 'w_down': 5.175438e-02, 'norm_final_g': 8.076260e+00}


def _to_microbatches(a, axis):
    t = _jnp.moveaxis(a, axis, 0)
    t = t.reshape((N_MICROBATCH, t.shape[0] // N_MICROBATCH) + t.shape[1:])
    return _jnp.moveaxis(t, 1, axis + 1)


def setup_inputs(seed: int = 0) -> dict:
    inp = _fwd_setup_inputs(seed)
    key = _jax.random.fold_in(_jax.random.key(seed), 7919)
    shape, _ = _output_shape()
    out = dict(inp)
    out["loss_target"] = _jax.random.normal(_jax.random.fold_in(key, 0), shape, _jnp.float32)
    for i, name in enumerate(TWIN_WEIGHTS):
        w = inp[name].astype(_jnp.float32)
        if MOMENT_SCALE is None:
            s = _jnp.sqrt(_jnp.mean(_jnp.square(w)) + 1e-30)
        else:
            s = MOMENT_SCALE[name]
        km, kv = _jax.random.split(_jax.random.fold_in(key, i + 1))
        out[name] = w
        out["m_" + name] = s * _jax.random.normal(km, w.shape, _jnp.float32)
        out["v_" + name] = (s * s) * _jax.random.uniform(kv, w.shape, _jnp.float32, 0.5, 1.5)
    if N_MICROBATCH > 1:
        for name, axis in PER_EXAMPLE_BATCH_AXIS.items():
            out[name] = _to_microbatches(out[name], axis)
    return {'x': out['x'], 'norm_mix_g': out['norm_mix_g'], 'w_in': out['w_in'], 'conv_w': out['conv_w'], 'conv_b': out['conv_b'], 'lru_wa': out['lru_wa'], 'lru_ba': out['lru_ba'], 'lru_wx': out['lru_wx'], 'lru_bx': out['lru_bx'], 'lru_lambda': out['lru_lambda'], 'w_proj_attn': out['w_proj_attn'], 'w_proj_lru': out['w_proj_lru'], 'w_out': out['w_out'], 'norm_mlp_g': out['norm_mlp_g'], 'w_up': out['w_up'], 'w_down': out['w_down'], 'norm_final_g': out['norm_final_g'], 'loss_target': out['loss_target'], 'm_norm_mix_g': out['m_norm_mix_g'], 'm_w_in': out['m_w_in'], 'm_conv_w': out['m_conv_w'], 'm_conv_b': out['m_conv_b'], 'm_lru_wa': out['m_lru_wa'], 'm_lru_ba': out['m_lru_ba'], 'm_lru_wx': out['m_lru_wx'], 'm_lru_bx': out['m_lru_bx'], 'm_lru_lambda': out['m_lru_lambda'], 'm_w_proj_attn': out['m_w_proj_attn'], 'm_w_proj_lru': out['m_w_proj_lru'], 'm_w_out': out['m_w_out'], 'm_norm_mlp_g': out['m_norm_mlp_g'], 'm_w_up': out['m_w_up'], 'm_w_down': out['m_w_down'], 'm_norm_final_g': out['m_norm_final_g'], 'v_norm_mix_g': out['v_norm_mix_g'], 'v_w_in': out['v_w_in'], 'v_conv_w': out['v_conv_w'], 'v_conv_b': out['v_conv_b'], 'v_lru_wa': out['v_lru_wa'], 'v_lru_ba': out['v_lru_ba'], 'v_lru_wx': out['v_lru_wx'], 'v_lru_bx': out['v_lru_bx'], 'v_lru_lambda': out['v_lru_lambda'], 'v_w_proj_attn': out['v_w_proj_attn'], 'v_w_proj_lru': out['v_w_proj_lru'], 'v_w_out': out['v_w_out'], 'v_norm_mlp_g': out['v_norm_mlp_g'], 'v_w_up': out['v_w_up'], 'v_w_down': out['v_w_down'], 'v_norm_final_g': out['v_norm_final_g']}


def _loss(weights, diff, rest, loss_target):
    with _jax.named_scope("forward"):
        args = {**rest, TWIN_DIFF_INPUT: diff, **{k: w.astype(_WEIGHT_DTYPES[k]) for k, w in weights.items()}}
        y = _forward(args)
    with _jax.named_scope("loss_head"):
        err = _jnp.square(y.astype(_jnp.float32) - loss_target)
        return 0.5 * _jnp.sum(_jnp.mean(err, axis=-1)) if err.ndim else 0.5 * err


def _adamw(w, g, m, v):
    m = ADAM_B1 * m + (1.0 - ADAM_B1) * g
    v = ADAM_B2 * v + (1.0 - ADAM_B2) * _jnp.square(g)
    m_hat = m / (1.0 - ADAM_B1 ** ADAM_STEP)
    v_hat = v / (1.0 - ADAM_B2 ** ADAM_STEP)
    delta = -ADAM_LR * (m_hat / (_jnp.sqrt(v_hat) + ADAM_EPS) + ADAM_WD * w)
    return delta, m, v


def reference(x, norm_mix_g, w_in, conv_w, conv_b, lru_wa, lru_ba, lru_wx, lru_bx, lru_lambda, w_proj_attn, w_proj_lru, w_out, norm_mlp_g, w_up, w_down, norm_final_g, loss_target, m_norm_mix_g, m_w_in, m_conv_w, m_conv_b, m_lru_wa, m_lru_ba, m_lru_wx, m_lru_bx, m_lru_lambda, m_w_proj_attn, m_w_proj_lru, m_w_out, m_norm_mlp_g, m_w_up, m_w_down, m_norm_final_g, v_norm_mix_g, v_w_in, v_conv_w, v_conv_b, v_lru_wa, v_lru_ba, v_lru_wx, v_lru_bx, v_lru_lambda, v_w_proj_attn, v_w_proj_lru, v_w_out, v_norm_mlp_g, v_w_up, v_w_down, v_norm_final_g):
    given = dict(x=x, norm_mix_g=norm_mix_g, w_in=w_in, conv_w=conv_w, conv_b=conv_b, lru_wa=lru_wa, lru_ba=lru_ba, lru_wx=lru_wx, lru_bx=lru_bx, lru_lambda=lru_lambda, w_proj_attn=w_proj_attn, w_proj_lru=w_proj_lru, w_out=w_out, norm_mlp_g=norm_mlp_g, w_up=w_up, w_down=w_down, norm_final_g=norm_final_g, loss_target=loss_target, m_norm_mix_g=m_norm_mix_g, m_w_in=m_w_in, m_conv_w=m_conv_w, m_conv_b=m_conv_b, m_lru_wa=m_lru_wa, m_lru_ba=m_lru_ba, m_lru_wx=m_lru_wx, m_lru_bx=m_lru_bx, m_lru_lambda=m_lru_lambda, m_w_proj_attn=m_w_proj_attn, m_w_proj_lru=m_w_proj_lru, m_w_out=m_w_out, m_norm_mlp_g=m_norm_mlp_g, m_w_up=m_w_up, m_w_down=m_w_down, m_norm_final_g=m_norm_final_g, v_norm_mix_g=v_norm_mix_g, v_w_in=v_w_in, v_conv_w=v_conv_w, v_conv_b=v_conv_b, v_lru_wa=v_lru_wa, v_lru_ba=v_lru_ba, v_lru_wx=v_lru_wx, v_lru_bx=v_lru_bx, v_lru_lambda=v_lru_lambda, v_w_proj_attn=v_w_proj_attn, v_w_proj_lru=v_w_proj_lru, v_w_out=v_w_out, v_norm_mlp_g=v_norm_mlp_g, v_w_up=v_w_up, v_w_down=v_w_down, v_norm_final_g=v_norm_final_g)
    weights = {n: given[n] for n in TWIN_WEIGHTS}
    shared = {n: given[n] for n in SHARED_INPUTS}
    per_example = {n: given[n] for n in ['x']}
    grad_fn = _jax.value_and_grad(_loss, argnums=(0, 1))

    def one_microbatch(ex, loss_target):
        ex = dict(ex)
        diff = ex.pop(TWIN_DIFF_INPUT)
        return grad_fn(weights, diff, {**shared, **ex}, loss_target)

    if N_MICROBATCH == 1:
        loss, (grad_w, grad_x) = one_microbatch(per_example, given["loss_target"])
    else:
        def body(carry, xs):
            loss_sum, grad_sum = carry
            l_k, (gw_k, gx_k) = one_microbatch(xs[0], xs[1])
            with _jax.named_scope("update"):
                return (loss_sum + l_k, _jax.tree.map(_jnp.add, grad_sum, gw_k)), gx_k

        init = (_jnp.zeros((), _jnp.float32), _jax.tree.map(_jnp.zeros_like, weights))
        (loss, grad_w), grad_x = _jax.lax.scan(body, init, (per_example, given["loss_target"]))
    with _jax.named_scope("update"):
        delta_w, new_m, new_v = {}, {}, {}
        for n in TWIN_WEIGHTS:
            delta_w[n], new_m[n], new_v[n] = _adamw(weights[n], grad_w[n], given["m_" + n], given["v_" + n])
    return (loss, grad_x, *[grad_w[n] for n in TWIN_WEIGHTS], *[delta_w[n] for n in TWIN_WEIGHTS],
            *[new_m[n] for n in TWIN_WEIGHTS], *[new_v[n] for n in TWIN_WEIGHTS])
```

```python
import functools

import jax
import jax.numpy as jnp
from jax import lax
from jax.experimental import pallas as pl
from jax.experimental.pallas import tpu as pltpu

F32 = jnp.float32
BF16 = jnp.bfloat16
MESH = pl.DeviceIdType.MESH
AXES = ("x", "y", "c")

N_CHIPS = 4
N_DEV = 8
HEAD_DIM = 128
ATTN_BLK = 128
DILATIONS = (1, 4, 16)
CONV_TAPS = 4
LRU_C = 8.0
EPS = 1e-6
N_SLOTS = 7
N_QKV = 3
VMEM_MIB = 2 ** 20
VMEM_V7X = 64 * VMEM_MIB

ADAM_LR = 0.001
ADAM_B1 = 0.9
ADAM_B2 = 0.999
ADAM_EPS = 1e-08
ADAM_WD = 0.01
ADAM_STEP = 10

NN = (((1,), (0,)), ((), ()))
NT = (((1,), (1,)), ((), ()))
TN = (((0,), (0,)), ((), ()))


def _params(vmem_mib=None, **kw):
    limit = None if vmem_mib is None else min(vmem_mib * VMEM_MIB, VMEM_V7X - 8 * VMEM_MIB)
    return pltpu.CompilerParams(vmem_limit_bytes=limit, **kw)


def _row_tile(rows, row_bytes, budget=VMEM_MIB):
    t = rows
    while t % 16 == 0 and t * row_bytes > budget:
        t //= 2
    return t


def _dot(a, b, dims):
    return lax.dot_general(a.astype(BF16), b.astype(BF16), dims, preferred_element_type=F32)


def _sigmoid(x):
    return jax.nn.sigmoid(x)


def _rms_fwd(x, g, name):
    s, d = x.shape
    tm = _row_tile(s, d * 4)

    def body(x_ref, g_ref, o_ref):
        xf = x_ref[...]
        r = lax.rsqrt(jnp.mean(xf * xf, axis=-1, keepdims=True) + EPS)
        o_ref[...] = (xf * r * g_ref[...]).astype(o_ref.dtype)

    return pl.pallas_call(
        body, name=name, grid=(s // tm,),
        in_specs=[pl.BlockSpec((tm, d), lambda i: (i, 0)), pl.BlockSpec((1, d), lambda i: (0, 0))],
        out_specs=pl.BlockSpec((tm, d), lambda i: (i, 0)),
        out_shape=jax.ShapeDtypeStruct((s, d), BF16), compiler_params=_params(32),
    )(x, g)


def _rms_bwd(x, g, dy, resid, name):
    s, d = x.shape
    tm = _row_tile(s, d * 4)

    def body(x_ref, g_ref, dy_ref, res_ref, dx_ref, dxb_ref, dg_ref):
        xf = x_ref[...]
        r = lax.rsqrt(jnp.mean(xf * xf, axis=-1, keepdims=True) + EPS)
        xh = xf * r
        dyv = dy_ref[...]
        dxh = dyv * g_ref[...]
        dx = r * (dxh - xh * jnp.mean(dxh * xh, axis=-1, keepdims=True)) + res_ref[...]
        dx_ref[...] = dx
        dxb_ref[...] = dx.astype(BF16)
        part = jnp.sum(dyv * xh, axis=0, keepdims=True)

        @pl.when(pl.program_id(0) == 0)
        def _():
            dg_ref[...] = part

        @pl.when(pl.program_id(0) > 0)
        def _():
            dg_ref[...] += part

    row = pl.BlockSpec((tm, d), lambda i: (i, 0))
    vec = pl.BlockSpec((1, d), lambda i: (0, 0))
    return pl.pallas_call(
        body, name=name, grid=(s // tm,),
        in_specs=[row, vec, row, row], out_specs=[row, row, vec],
        out_shape=[jax.ShapeDtypeStruct((s, d), F32), jax.ShapeDtypeStruct((s, d), BF16),
                   jax.ShapeDtypeStruct((1, d), F32)],
        compiler_params=_params(32),
    )(x, g, dy, resid)


def _loss_head(h2, g, target):
    s, d = h2.shape
    tm = _row_tile(s, d * 4)

    def body(x_ref, g_ref, t_ref, loss_ref, dx_ref, dxb_ref, dg_ref):
        xf = x_ref[...]
        gv = g_ref[...]
        r = lax.rsqrt(jnp.mean(xf * xf, axis=-1, keepdims=True) + EPS)
        xh = xf * r
        err = xh * gv - t_ref[...]
        part = jnp.sum(jnp.sum(err * err, axis=1, keepdims=True), axis=0, keepdims=True) * (0.5 / d)
        dyv = err * (1.0 / d)
        dxh = dyv * gv
        dx = r * (dxh - xh * jnp.mean(dxh * xh, axis=-1, keepdims=True))
        dx_ref[...] = dx
        dxb_ref[...] = dx.astype(BF16)
        dgp = jnp.sum(dyv * xh, axis=0, keepdims=True)

        @pl.when(pl.program_id(0) == 0)
        def _():
            dg_ref[...] = dgp
            loss_ref[...] = jnp.broadcast_to(part, loss_ref.shape)

        @pl.when(pl.program_id(0) > 0)
        def _():
            dg_ref[...] += dgp
            loss_ref[...] += jnp.broadcast_to(part, loss_ref.shape)

    row = pl.BlockSpec((tm, d), lambda i: (i, 0))
    vec = pl.BlockSpec((1, d), lambda i: (0, 0))
    return pl.pallas_call(
        body, name="loss_head", grid=(s // tm,),
        in_specs=[row, vec, row],
        out_specs=[pl.BlockSpec((8, 128), lambda i: (0, 0)), row, row, vec],
        out_shape=[jax.ShapeDtypeStruct((8, 128), F32), jax.ShapeDtypeStruct((s, d), F32),
                   jax.ShapeDtypeStruct((s, d), BF16), jax.ShapeDtypeStruct((1, d), F32)],
        compiler_params=_params(32),
    )(h2, g, target)


def _mm(name, operands, in_specs, out_shape, out_specs, grid, dims, epilogue, nk=1, acc_shape=None,
        vmem_mib=56, aliases=None):
    n_in = len(operands)
    n_out = len(out_shape)

    def body(*refs):
        a_ref, b_ref = refs[0], refs[1]
        extras = refs[2:n_in]
        outs = refs[n_in:n_in + n_out]

        def prod():
            return _dot(a_ref[...], b_ref[...], dims)

        if nk == 1:
            epilogue(prod(), extras, outs)
        else:
            acc = refs[n_in + n_out]
            k = pl.program_id(2)

            @pl.when(k == 0)
            def _():
                acc[...] = prod()

            @pl.when(k > 0)
            def _():
                acc[...] += prod()

            @pl.when(k == nk - 1)
            def _():
                epilogue(acc[...], extras, outs)

    scratch = [] if nk == 1 else [pltpu.VMEM(acc_shape, F32)]
    return pl.pallas_call(
        body, name=name, grid=grid, in_specs=in_specs, out_specs=out_specs, out_shape=out_shape,
        scratch_shapes=scratch, input_output_aliases=aliases or {},
        compiler_params=_params(vmem_mib),
    )(*operands)


def _store(acc, extras, outs):
    outs[0][...] = acc.astype(outs[0].dtype)


def _proj_in(xn, w_in_g, first_unit, n_units, n_slots):
    s, d = xn.shape
    u = d // 4
    per = N_SLOTS

    return _mm(
        "proj_in_%d" % first_unit, [xn, w_in_g],
        [pl.BlockSpec((s, d), lambda i, j, k: (0, 0)),
         pl.BlockSpec((None, d, u), lambda i, j, k: ((j + first_unit) // per, 0, (j + first_unit) % per))],
        [jax.ShapeDtypeStruct((n_slots, s, d), F32)],
        [pl.BlockSpec((None, s, u), lambda i, j, k: (j // 4, 0, j % 4))],
        (1, n_units, 1), NN, _store)[0]


def _mm_nn(name, a, b, extras, extra_specs, out_shape, out_specs, epilogue, tn, aliases=None):
    s, kdim = a.shape
    n = b.shape[1]
    return _mm(
        name, [a, b] + list(extras),
        [pl.BlockSpec((s, kdim), lambda i, j, k: (0, 0)), pl.BlockSpec((kdim, tn), lambda i, j, k: (0, j))]
        + list(extra_specs),
        out_shape, out_specs, (1, n // tn, 1), NN, epilogue, aliases=aliases)


def _mm_nt(name, a, b, extras, extra_specs, out_shape, out_specs, epilogue, tn, aliases=None):
    s, kdim = a.shape
    n = b.shape[0]
    return _mm(
        name, [a, b] + list(extras),
        [pl.BlockSpec((s, kdim), lambda i, j, k: (0, 0)), pl.BlockSpec((tn, kdim), lambda i, j, k: (j, 0))]
        + list(extra_specs),
        out_shape, out_specs, (1, n // tn, 1), NT, epilogue, aliases=aliases)


def _mm_tn(name, a, b, a_spec, b_spec, out_shape, out_spec, grid, m, tn, s, aliases=None, extra=None):
    ch = 256
    n_in = 2 if extra is None else 3

    def body(*refs):
        a_ref, b_ref = refs[0], refs[1]
        o_ref, at_ref = refs[n_in], refs[n_in + 1]

        @pl.when(pl.program_id(1) == 0)
        def _():
            for c0 in range(0, s, ch):
                at_ref[:, c0:c0 + ch] = a_ref[c0:c0 + ch, :].astype(F32).T.astype(BF16)

        o_ref[...] = _dot(at_ref[...], b_ref[...], NN).astype(o_ref.dtype)

    operands = [a, b] + ([] if extra is None else [extra])
    in_specs = [a_spec, b_spec] + ([] if extra is None else [pl.BlockSpec(memory_space=pl.ANY)])
    return pl.pallas_call(
        body, name=name, grid=grid, in_specs=in_specs, out_specs=out_spec, out_shape=out_shape,
        scratch_shapes=[pltpu.VMEM((m, s), BF16)], input_output_aliases=aliases or {},
        compiler_params=_params(56),
    )(*operands)


def _dxn(dproj_a, dproj_b, w_in_g, tn):
    n_a, s, d = dproj_a.shape
    u = d // 4
    ua = 4 * n_a
    nk = 4 * N_SLOTS
    per = N_SLOTS

    def body(a_ref, b_ref, w_ref, o_ref):
        k = pl.program_id(2)

        @pl.when(k == 0)
        def _():
            o_ref[...] = jnp.zeros_like(o_ref)

        @pl.when(k < ua)
        def _():
            o_ref[...] += _dot(a_ref[...], w_ref[...], NT)

        @pl.when(k >= ua)
        def _():
            o_ref[...] += _dot(b_ref[...], w_ref[...], NT)

    def a_map(i, j, k):
        kk = jnp.minimum(k, ua - 1)
        return (kk // 4, 0, kk % 4)

    def b_map(i, j, k):
        kk = jnp.maximum(k - ua, 0)
        return (kk // 4, 0, kk % 4)

    return pl.pallas_call(
        body, name="dxn", grid=(1, d // tn, nk),
        in_specs=[pl.BlockSpec((None, s, u), a_map), pl.BlockSpec((None, s, u), b_map),
                  pl.BlockSpec((None, tn, u), lambda i, j, k: (k // per, j, k % per))],
        out_specs=pl.BlockSpec((s, tn), lambda i, j, k: (0, j)),
        out_shape=jax.ShapeDtypeStruct((s, d), F32),
        compiler_params=_params(48),
    )(dproj_a, dproj_b, w_in_g)


def _attn_masks(nblk, slope, dil):
    nkeys = 2 * ATTN_BLK if nblk > 1 else ATTN_BLK
    ii = lax.broadcasted_iota(jnp.int32, (ATTN_BLK, nkeys), 0)
    jj = lax.broadcasted_iota(jnp.int32, (ATTN_BLK, nkeys), 1)
    diff = (ATTN_BLK + ii - jj) if nblk > 1 else (ii - jj)
    band = (diff >= 0) & (diff <= ATTN_BLK)
    bias = -(slope * float(dil)) * diff.astype(F32)
    return band, bias, jj


def _deinterleave(dst_ref, src_ref, dil, s):
    seg = s // dil
    if dil == 1:
        dst_ref[...] = src_ref[...].astype(dst_ref.dtype)
    else:
        for r in range(dil):
            dst_ref[r * seg:(r + 1) * seg, :] = src_ref[pl.ds(r, seg, stride=dil), :].astype(dst_ref.dtype)


def _attn_fwd(proj_a, slopes):
    _, s, d = proj_a.shape
    heads = d // HEAD_DIM
    scale = HEAD_DIM ** -0.5
    n_t = s // ATTN_BLK
    ng = len(DILATIONS)

    def body(q_ref, k_ref, v_ref, sl_ref, o_ref, lse_ref, qd, kd, vd, od, ld, og, lg):
        slope = sl_ref[...][:, :1]
        for g, dil in enumerate(DILATIONS):
            nblk = s // dil // ATTN_BLK
            _deinterleave(qd, q_ref, dil, s)
            _deinterleave(kd, k_ref, dil, s)
            _deinterleave(vd, v_ref, dil, s)
            band, bias, jj = _attn_masks(nblk, slope, dil)

            def blk(t, carry, nblk=nblk, band=band, bias=bias, jj=jj):
                cur = pl.multiple_of(t * ATTN_BLK, ATTN_BLK)
                q = qd[pl.ds(cur, ATTN_BLK), :]
                if nblk > 1:
                    prev = pl.multiple_of(jnp.maximum(t - 1, 0) * ATTN_BLK, ATTN_BLK)
                    kk = jnp.concatenate([kd[pl.ds(prev, ATTN_BLK), :], kd[pl.ds(cur, ATTN_BLK), :]], axis=0)
                    vv = jnp.concatenate([vd[pl.ds(prev, ATTN_BLK), :], vd[pl.ds(cur, ATTN_BLK), :]], axis=0)
                    valid = band & (jj >= jnp.where(t % nblk == 0, ATTN_BLK, 0))
                else:
                    kk = kd[pl.ds(cur, ATTN_BLK), :]
                    vv = vd[pl.ds(cur, ATTN_BLK), :]
                    valid = band
                sc = _dot(q, kk, NT) * scale + bias
                sc = jnp.where(valid, sc, -jnp.inf)
                m = jnp.max(sc, axis=1, keepdims=True)
                p = jnp.exp(sc - m)
                l = jnp.sum(p, axis=1, keepdims=True)
                od[pl.ds(cur, ATTN_BLK), :] = _dot(p, vv, NN) / l
                ld[pl.ds(cur, ATTN_BLK), :] = jnp.broadcast_to(m + jnp.log(l), (ATTN_BLK, HEAD_DIM))
                return carry

            lax.fori_loop(0, n_t, blk, 0)
            seg = s // dil
            if dil == 1:
                og[g] = od[...]
                lg[g] = ld[...]
            else:
                for r in range(dil):
                    og[g, pl.ds(r, seg, stride=dil), :] = od[r * seg:(r + 1) * seg, :]
                    lg[g, pl.ds(r, seg, stride=dil), :] = ld[r * seg:(r + 1) * seg, :]

        ch = 256

        def combine(c, carry):
            rows = pl.ds(pl.multiple_of(c * ch, ch), ch)
            ls = [lg[g, rows, :] for g in range(ng)]
            mx = functools.reduce(jnp.maximum, ls)
            es = [jnp.exp(x - mx) for x in ls]
            den = functools.reduce(jnp.add, es)
            num = functools.reduce(jnp.add, [es[g] * og[g, rows, :] for g in range(ng)])
            o_ref[rows, :] = (num / den).astype(o_ref.dtype)
            lse_ref[rows, :] = mx + jnp.log(den)
            return carry

        lax.fori_loop(0, s // ch, combine, 0)

    def col(slot):
        return pl.BlockSpec((None, s, HEAD_DIM), lambda h: (slot, 0, h))

    head = pl.BlockSpec((s, HEAD_DIM), lambda h: (0, h))
    return pl.pallas_call(
        body, name="attn_fwd", grid=(heads,),
        in_specs=[col(0), col(1), col(2), pl.BlockSpec((None, 1, HEAD_DIM), lambda h: (h, 0, 0))],
        out_specs=[head, head],
        out_shape=[jax.ShapeDtypeStruct((s, d), BF16), jax.ShapeDtypeStruct((s, d), F32)],
        scratch_shapes=[pltpu.VMEM((s, HEAD_DIM), BF16)] * 3 + [pltpu.VMEM((s, HEAD_DIM), F32)] * 2
        + [pltpu.VMEM((ng, s, HEAD_DIM), F32)] * 2,
        compiler_params=_params(40),
    )(proj_a, proj_a, proj_a, slopes)


def _attn_bwd(proj_a, slopes, y_attn, lse, dy):
    _, s, d = proj_a.shape
    heads = d // HEAD_DIM
    scale = HEAD_DIM ** -0.5
    n_t = s // ATTN_BLK

    def body(q_ref, k_ref, v_ref, sl_ref, o_ref, lse_ref, dy_ref, out_ref,
             qd, kd, vd, dod, lsd, dld, delta, dqd, dkd, dvd, dqa, dka, dva):
        slope = sl_ref[...][:, :1]
        dyv = dy_ref[...]
        delta[...] = jnp.broadcast_to(
            jnp.sum(dyv * o_ref[...].astype(F32), axis=1, keepdims=True), (s, HEAD_DIM))
        for g, dil in enumerate(DILATIONS):
            nblk = s // dil // ATTN_BLK
            seg = s // dil
            _deinterleave(qd, q_ref, dil, s)
            _deinterleave(kd, k_ref, dil, s)
            _deinterleave(vd, v_ref, dil, s)
            _deinterleave(dod, dy_ref, dil, s)
            _deinterleave(lsd, lse_ref, dil, s)
            _deinterleave(dld, delta, dil, s)
            dkd[...] = jnp.zeros_like(dkd)
            dvd[...] = jnp.zeros_like(dvd)
            band, bias, jj = _attn_masks(nblk, slope, dil)

            def blk(t, carry, nblk=nblk, band=band, bias=bias, jj=jj):
                cur = pl.multiple_of(t * ATTN_BLK, ATTN_BLK)
                rows = pl.ds(cur, ATTN_BLK)
                q = qd[rows, :]
                do = dod[rows, :]
                lse_b = lsd[rows, :]
                dl_b = dld[rows, :]
                if nblk > 1:
                    prev = pl.multiple_of(jnp.maximum(t - 1, 0) * ATTN_BLK, ATTN_BLK)
                    prows = pl.ds(prev, ATTN_BLK)
                    kk = jnp.concatenate([kd[prows, :], kd[rows, :]], axis=0)
                    vv = jnp.concatenate([vd[prows, :], vd[rows, :]], axis=0)
                    valid = band & (jj >= jnp.where(t % nblk == 0, ATTN_BLK, 0))
                    lse_b = jnp.concatenate([lse_b, lse_b], axis=1)
                    dl_b = jnp.concatenate([dl_b, dl_b], axis=1)
                else:
                    kk = kd[rows, :]
                    vv = vd[rows, :]
                    valid = band
                sc = _dot(q, kk, NT) * scale + bias
                p = jnp.where(valid, jnp.exp(sc - lse_b), 0.0)
                dp = _dot(do, vv, NT)
                ds = p * (dp - dl_b)
                dv_b = _dot(p, do, TN)
                dk_b = _dot(ds, q, TN) * scale
                dqd[rows, :] = _dot(ds, kk, NN) * scale
                if nblk > 1:
                    dkd[prows, :] += dk_b[:ATTN_BLK]
                    dvd[prows, :] += dv_b[:ATTN_BLK]
                    dkd[rows, :] += dk_b[ATTN_BLK:]
                    dvd[rows, :] += dv_b[ATTN_BLK:]
                else:
                    dkd[rows, :] += dk_b
                    dvd[rows, :] += dv_b
                return carry

            lax.fori_loop(0, n_t, blk, 0)
            for acc, part in ((dqa, dqd), (dka, dkd), (dva, dvd)):
                if dil == 1:
                    acc[...] = part[...]
                else:
                    for r in range(dil):
                        acc[pl.ds(r, seg, stride=dil), :] += part[r * seg:(r + 1) * seg, :]
        out_ref[0] = dqa[...].astype(out_ref.dtype)
        out_ref[1] = dka[...].astype(out_ref.dtype)
        out_ref[2] = dva[...].astype(out_ref.dtype)

    def col(slot):
        return pl.BlockSpec((None, s, HEAD_DIM), lambda h: (slot, 0, h))

    head = pl.BlockSpec((s, HEAD_DIM), lambda h: (0, h))
    return pl.pallas_call(
        body, name="attn_bwd", grid=(heads,),
        in_specs=[col(0), col(1), col(2), pl.BlockSpec((None, 1, HEAD_DIM), lambda h: (h, 0, 0)),
                  head, head, head],
        out_specs=pl.BlockSpec((N_QKV, s, HEAD_DIM), lambda h: (0, 0, h)),
        out_shape=jax.ShapeDtypeStruct((N_QKV, s, d), BF16),
        scratch_shapes=[pltpu.VMEM((s, HEAD_DIM), BF16)] * 4 + [pltpu.VMEM((s, HEAD_DIM), F32)] * 9,
        compiler_params=_params(48),
    )(proj_a, proj_a, proj_a, slopes, y_attn, lse, dy)


def _expm1(x):
    small = x * (1.0 + x * (0.5 + x * (1.0 / 6.0 + x * (1.0 / 24.0 + x * (1.0 / 120.0)))))
    return jnp.where(jnp.abs(x) < 0.1, small, jnp.exp(x) - 1.0)


def _softplus(x):
    return jnp.maximum(x, 0.0) + jnp.log1p(jnp.exp(-jnp.abs(x)))


GELU_K = 0.7978845608028654
GELU_C = 0.044715


def _gelu(x):
    t = jnp.tanh(GELU_K * (x + GELU_C * x * x * x))
    return 0.5 * x * (1.0 + t), t


def _gelu_grad(x, t):
    return 0.5 * (1.0 + t) + 0.5 * x * (1.0 - t * t) * GELU_K * (1.0 + 3.0 * GELU_C * x * x)


def _lru_gates(xc, wa, ba, wx, bx, sp):
    r = _sigmoid(_dot(xc, wa, NN) + ba)
    ig = _sigmoid(_dot(xc, wx, NN) + bx)
    log_a = -LRU_C * r * sp
    a = jnp.exp(log_a)
    mult = jnp.sqrt(-_expm1(2.0 * log_a))
    return r, ig, a, mult


def _scan_fwd(a, u, tt):
    row = lax.broadcasted_iota(jnp.int32, a.shape, 0)
    sh = 1
    while sh < tt:
        keep = row >= sh
        a_s = jnp.where(keep, pltpu.roll(a, sh, 0), 1.0)
        u_s = jnp.where(keep, pltpu.roll(u, sh, 0), 0.0)
        u = a * u_s + u
        a = a * a_s
        sh *= 2
    return a, u


def _scan_bwd(b, g, tt):
    row = lax.broadcasted_iota(jnp.int32, b.shape, 0)
    sh = 1
    while sh < tt:
        keep = row < tt - sh
        b_s = jnp.where(keep, pltpu.roll(b, tt - sh, 0), 1.0)
        g_s = jnp.where(keep, pltpu.roll(g, tt - sh, 0), 0.0)
        g = g + b * g_s
        b = b * b_s
        sh *= 2
    return b, g


def _conv_rows(xpad_ref, cw, cb, s):
    acc = cb
    for j in range(CONV_TAPS):
        off = 8 - (CONV_TAPS - 1) + j
        acc = acc + cw[j:j + 1, :] * xpad_ref[off:off + s, :]
    return acc


LRU_TILE = 128


def _lru_specs(s, d):
    heads = d // HEAD_DIM

    def col(slot):
        return pl.BlockSpec((None, s, HEAD_DIM), lambda h: (slot, 0, h))

    vec = pl.BlockSpec((1, HEAD_DIM), lambda h: (0, h))
    mat = pl.BlockSpec((None, HEAD_DIM, HEAD_DIM), lambda h: (h, 0, 0))
    cw = pl.BlockSpec((CONV_TAPS, HEAD_DIM), lambda h: (0, h))
    head = pl.BlockSpec((s, HEAD_DIM), lambda h: (0, h))
    return heads, col, vec, mat, cw, head


def _lru_fwd(proj_b, conv_w, conv_b, wa, ba, wx, bx, lam):
    _, s, d = proj_b.shape
    heads, col, vec, mat, cws, head = _lru_specs(s, d)
    tt = LRU_TILE

    def body(xr_ref, xg_ref, cw_ref, cb_ref, wa_ref, ba_ref, wx_ref, bx_ref, lam_ref, y_ref, h_ref, xpad, xc_s):
        xpad[0:8, :] = jnp.zeros((8, HEAD_DIM), F32)
        xpad[8:8 + s, :] = xr_ref[...]
        xc_s[...] = _conv_rows(xpad, cw_ref[...], cb_ref[...], s)
        sp = _softplus(-lam_ref[...])
        wav, wxv, bav, bxv = wa_ref[...], wx_ref[...], ba_ref[...], bx_ref[...]

        def tile(i, hc):
            rows = pl.ds(pl.multiple_of(i * tt, tt), tt)
            xc = xc_s[rows, :]
            _, ig, a, mult = _lru_gates(xc, wav, bav, wxv, bxv, sp)
            pa, hl = _scan_fwd(a, mult * (ig * xc), tt)
            h = hl + pa * hc
            h_ref[rows, :] = h
            gel, _ = _gelu(xg_ref[rows, :])
            y_ref[rows, :] = (h * gel).astype(y_ref.dtype)
            return h[tt - 1:tt, :]

        lax.fori_loop(0, s // tt, tile, jnp.zeros((1, HEAD_DIM), F32))

    return pl.pallas_call(
        body, name="lru_fwd", grid=(heads,),
        in_specs=[col(0), col(1), cws, vec, mat, vec, mat, vec, vec],
        out_specs=[head, head],
        out_shape=[jax.ShapeDtypeStruct((s, d), BF16), jax.ShapeDtypeStruct((s, d), F32)],
        scratch_shapes=[pltpu.VMEM((s + 8, HEAD_DIM), F32), pltpu.VMEM((s, HEAD_DIM), F32)],
        compiler_params=_params(32),
    )(proj_b, proj_b, conv_w, conv_b, wa, ba, wx, bx, lam)


def _lru_bwd(proj_b, h_lru, dy, conv_w, conv_b, wa, ba, wx, bx, lam, dproj_b):
    _, s, d = proj_b.shape
    heads, col, vec, mat, cws, head = _lru_specs(s, d)
    tt = LRU_TILE
    n_t = s // tt

    def body(xr_ref, xg_ref, h_ref, dy_ref, cw_ref, cb_ref, wa_ref, ba_ref, wx_ref, bx_ref, lam_ref, alias_ref,
             out_ref, dcw_ref, dcb_ref, dwa_ref, dba_ref, dwx_ref, dbx_ref, dlam_ref, xpad, xc_s, dxc_s):
        del alias_ref
        xpad[0:8, :] = jnp.zeros((8, HEAD_DIM), F32)
        xpad[8:8 + s, :] = xr_ref[...]
        cwv = cw_ref[...]
        xc_s[...] = _conv_rows(xpad, cwv, cb_ref[...], s)
        dxc_s[s:s + 8, :] = jnp.zeros((8, HEAD_DIM), F32)
        lamv = lam_ref[...]
        sp = _softplus(-lamv)
        wav, wxv, bav, bxv = wa_ref[...], wx_ref[...], ba_ref[...], bx_ref[...]
        dwa_ref[...] = jnp.zeros_like(dwa_ref)
        dwx_ref[...] = jnp.zeros_like(dwx_ref)
        zero = jnp.zeros((1, HEAD_DIM), F32)
        row = lax.broadcasted_iota(jnp.int32, (tt, HEAD_DIM), 0)

        def tile(it, carry):
            dh_next, a_next, dba, dbx, dsp, dcb = carry
            i = n_t - 1 - it
            t0 = pl.multiple_of(i * tt, tt)
            rows = pl.ds(t0, tt)
            xc = xc_s[rows, :]
            r, ig, a, mult = _lru_gates(xc, wav, bav, wxv, bxv, sp)
            h = h_ref[rows, :]
            before = h_ref[pl.ds(pl.multiple_of(jnp.maximum(t0 - 8, 0), 8), 8), :][7:8, :]
            before = before * (i > 0).astype(F32)
            h_prev = jnp.where(row == 0, before, pltpu.roll(h, 1, 0))
            xg = xg_ref[rows, :]
            dyv = dy_ref[rows, :]
            gel, th = _gelu(xg)
            out_ref[1, rows, :] = (dyv * h * _gelu_grad(xg, th)).astype(out_ref.dtype)
            b = jnp.where(row == tt - 1, a_next, pltpu.roll(a, tt - 1, 0))
            pb, z = _scan_bwd(b, dyv * gel, tt)
            dh = z + pb * dh_next
            da = dh * h_prev
            dmult = dh * (ig * xc)
            dig = dh * (mult * xc)
            dla = da * a - dmult * (a * a / mult)
            dzr = dla * (-LRU_C * sp) * (r * (1.0 - r))
            dzx = dig * (ig * (1.0 - ig))
            dxc = dh * (mult * ig) + _dot(dzr, wav, NT) + _dot(dzx, wxv, NT)
            dxc_s[rows, :] = dxc
            dwa_ref[...] += _dot(xc, dzr, TN)
            dwx_ref[...] += _dot(xc, dzx, TN)
            return (dh[0:1, :], a[0:1, :],
                    dba + jnp.sum(dzr, axis=0, keepdims=True),
                    dbx + jnp.sum(dzx, axis=0, keepdims=True),
                    dsp + jnp.sum(dla * (-LRU_C * r), axis=0, keepdims=True),
                    dcb + jnp.sum(dxc, axis=0, keepdims=True))

        _, _, dba, dbx, dsp, dcb = lax.fori_loop(0, n_t, tile, (zero, zero, zero, zero, zero, zero))
        dba_ref[...] = dba
        dbx_ref[...] = dbx
        dcb_ref[...] = dcb
        dlam_ref[...] = -dsp * _sigmoid(-lamv)
        dxr = jnp.zeros((s, HEAD_DIM), F32)
        for j in range(CONV_TAPS):
            back = CONV_TAPS - 1 - j
            off = 8 - back
            dcw_ref[j:j + 1, :] = jnp.sum(dxc_s[0:s, :] * xpad[off:off + s, :], axis=0, keepdims=True)
            dxr = dxr + cwv[j:j + 1, :] * dxc_s[back:back + s, :]
        out_ref[0] = dxr.astype(out_ref.dtype)

    return pl.pallas_call(
        body, name="lru_bwd", grid=(heads,),
        in_specs=[col(0), col(1), head, head, cws, vec, mat, vec, mat, vec, vec, pl.BlockSpec(memory_space=pl.ANY)],
        out_specs=[pl.BlockSpec((2, s, HEAD_DIM), lambda h: (0, 0, h)), cws, vec, mat, vec, mat, vec, vec],
        out_shape=[jax.ShapeDtypeStruct(dproj_b.shape, dproj_b.dtype),
                   jax.ShapeDtypeStruct((CONV_TAPS, d), F32), jax.ShapeDtypeStruct((1, d), F32),
                   jax.ShapeDtypeStruct(wa.shape, F32), jax.ShapeDtypeStruct((1, d), F32),
                   jax.ShapeDtypeStruct(wx.shape, F32), jax.ShapeDtypeStruct((1, d), F32),
                   jax.ShapeDtypeStruct((1, d), F32)],
        scratch_shapes=[pltpu.VMEM((s + 8, HEAD_DIM), F32), pltpu.VMEM((s, HEAD_DIM), F32),
                        pltpu.VMEM((s + 8, HEAD_DIM), F32)],
        input_output_aliases={11: 0},
        compiler_params=_params(32),
    )(proj_b, proj_b, h_lru, dy, conv_w, conv_b, wa, ba, wx, bx, lam, dproj_b)


def _place():
    x, y, c = (lax.axis_index(n) for n in AXES)
    return x, y, c


def _other_chips(x, y):
    return [(1 - x, y), (x, 1 - y), (1 - x, 1 - y)]


HBM = pl.BlockSpec(memory_space=pl.ANY)


def _gather_weights(shards):
    n = len(shards)

    def body(*refs):
        ins, outs = refs[:n], refs[n:2 * n]
        send_sems, recv_sems, local_sems = refs[2 * n:]
        x, y, c = _place()
        me = 2 * x + y
        chips = _other_chips(x, y)
        sibling = (x, y, 1 - c)

        def copy(w, k, src, dst, to):
            return pltpu.make_async_remote_copy(src_ref=src, dst_ref=dst, send_sem=send_sems.at[6 * w + k],
                                                recv_sem=recv_sems.at[6 * w + k], device_id=to, device_id_type=MESH)

        started = []
        for w in range(n):
            mine = pltpu.make_async_copy(ins[w], outs[w].at[me], local_sems.at[w])
            mine.start()
            started.append(mine)
        sends = []
        for w in range(n):
            for j, (cx, cy) in enumerate(chips):
                cp = copy(w, j, ins[w].at[c], outs[w].at[me, c], (cx, cy, c))
                cp.start()
                sends.append(cp)
        for j, (cx, cy) in enumerate(chips):
            cj = 2 * cx + cy
            for w in range(n):
                landed = outs[w].at[cj, c]
                copy(w, j, landed, landed, (cx, cy, c)).wait_recv()
                fwd = copy(w, 3 + j, landed, landed, sibling)
                fwd.start()
                sends.append(fwd)
        for j, (cx, cy) in enumerate(chips):
            cj = 2 * cx + cy
            for w in range(n):
                theirs = outs[w].at[cj, 1 - c]
                copy(w, 3 + j, theirs, theirs, sibling).wait_recv()
        for cp in sends:
            cp.wait_send()
        for cp in started:
            cp.wait()

    return pl.pallas_call(
        body, name="gather_weights", in_specs=[HBM] * n, out_specs=[HBM] * n,
        out_shape=[jax.ShapeDtypeStruct((N_CHIPS,) + a.shape, a.dtype) for a in shards],
        scratch_shapes=[pltpu.SemaphoreType.DMA((6 * n,)), pltpu.SemaphoreType.DMA((6 * n,)),
                        pltpu.SemaphoreType.DMA((n,))],
    )(*shards)


def _pair_exchange(grads):
    n = len(grads)

    def body(*refs):
        ins, outs = refs[:n], refs[n:2 * n]
        send_sems, recv_sems = refs[2 * n:]
        x, y, c = _place()
        sibling = (x, y, 1 - c)
        cps = []
        for w in range(n):
            for j in range(N_CHIPS):
                cp = pltpu.make_async_remote_copy(
                    src_ref=ins[w].at[j, 1 - c], dst_ref=outs[w].at[j], send_sem=send_sems.at[N_CHIPS * w + j],
                    recv_sem=recv_sems.at[N_CHIPS * w + j], device_id=sibling, device_id_type=MESH)
                cp.start()
                cps.append(cp)
        for cp in cps:
            cp.wait()

    return pl.pallas_call(
        body, name="reduce_pair_exchange", in_specs=[HBM] * n, out_specs=[HBM] * n,
        out_shape=[jax.ShapeDtypeStruct((N_CHIPS,) + a.shape[2:], a.dtype) for a in grads],
        scratch_shapes=[pltpu.SemaphoreType.DMA((N_CHIPS * n,)), pltpu.SemaphoreType.DMA((N_CHIPS * n,))],
    )(*grads)


def _chip_exchange(sums):
    n = len(sums)

    def body(*refs):
        ins, outs = refs[:n], refs[n:2 * n]
        send_sems, recv_sems, local_sems = refs[2 * n:]
        x, y, c = _place()
        me = 2 * x + y
        chips = _other_chips(x, y)
        local = []
        for w in range(n):
            cp = pltpu.make_async_copy(ins[w].at[me], outs[w].at[me], local_sems.at[w])
            cp.start()
            local.append(cp)
        cps = []
        for w in range(n):
            for j, (cx, cy) in enumerate(chips):
                cp = pltpu.make_async_remote_copy(
                    src_ref=ins[w].at[2 * cx + cy], dst_ref=outs[w].at[me], send_sem=send_sems.at[3 * w + j],
                    recv_sem=recv_sems.at[3 * w + j], device_id=(cx, cy, c), device_id_type=MESH)
                cp.start()
                cps.append(cp)
        for w in range(n):
            for j, (cx, cy) in enumerate(chips):
                theirs = outs[w].at[2 * cx + cy]
                pltpu.make_async_remote_copy(
                    src_ref=theirs, dst_ref=theirs, send_sem=send_sems.at[3 * w + j],
                    recv_sem=recv_sems.at[3 * w + j], device_id=(cx, cy, c), device_id_type=MESH).wait_recv()
        for cp in cps:
            cp.wait_send()
        for cp in local:
            cp.wait()

    return pl.pallas_call(
        body, name="reduce_chip_exchange", in_specs=[HBM] * n, out_specs=[HBM] * n,
        out_shape=[jax.ShapeDtypeStruct(a.shape, a.dtype) for a in sums],
        scratch_shapes=[pltpu.SemaphoreType.DMA((3 * n,)), pltpu.SemaphoreType.DMA((3 * n,)),
                        pltpu.SemaphoreType.DMA((n,))],
    )(*sums)


def _half_exchange(halves):
    n = len(halves)

    def body(*refs):
        ins, outs = refs[:n], refs[n:2 * n]
        send_sems, recv_sems, local_sems = refs[2 * n:]
        x, y, c = _place()
        sibling = (x, y, 1 - c)
        local, cps = [], []
        for w in range(n):
            cp = pltpu.make_async_copy(ins[w], outs[w].at[c], local_sems.at[w])
            cp.start()
            local.append(cp)
            rc = pltpu.make_async_remote_copy(
                src_ref=ins[w], dst_ref=outs[w].at[c], send_sem=send_sems.at[w], recv_sem=recv_sems.at[w],
                device_id=sibling, device_id_type=MESH)
            rc.start()
            cps.append(rc)
        for w in range(n):
            theirs = outs[w].at[1 - c]
            pltpu.make_async_remote_copy(
                src_ref=theirs, dst_ref=theirs, send_sem=send_sems.at[w], recv_sem=recv_sems.at[w],
                device_id=sibling, device_id_type=MESH).wait_recv()
        for cp in cps:
            cp.wait_send()
        for cp in local:
            cp.wait()

    return pl.pallas_call(
        body, name="reduce_half_exchange", in_specs=[HBM] * n, out_specs=[HBM] * n,
        out_shape=[jax.ShapeDtypeStruct((2,) + a.shape, a.dtype) for a in halves],
        scratch_shapes=[pltpu.SemaphoreType.DMA((n,)), pltpu.SemaphoreType.DMA((n,)),
                        pltpu.SemaphoreType.DMA((n,))],
    )(*halves)


def _all_gather8(block, name):
    def body(in_ref, out_ref, send_sems, recv_sems, local_sem):
        x, y, c = _place()
        me = 4 * x + 2 * y + c
        mine = pltpu.make_async_copy(in_ref, out_ref.at[me], local_sem)
        mine.start()
        flips = [(fx, fy, fc) for fx in (0, 1) for fy in (0, 1) for fc in (0, 1)][1:]
        cps = []
        for k, (fx, fy, fc) in enumerate(flips):
            cp = pltpu.make_async_remote_copy(
                src_ref=in_ref, dst_ref=out_ref.at[me], send_sem=send_sems.at[k], recv_sem=recv_sems.at[k],
                device_id=(x ^ fx, y ^ fy, c ^ fc), device_id_type=MESH)
            cp.start()
            cps.append(cp)
        for k, (fx, fy, fc) in enumerate(flips):
            px, py, pc = x ^ fx, y ^ fy, c ^ fc
            theirs = out_ref.at[4 * px + 2 * py + pc]
            pltpu.make_async_remote_copy(
                src_ref=theirs, dst_ref=theirs, send_sem=send_sems.at[k], recv_sem=recv_sems.at[k],
                device_id=(px, py, pc), device_id_type=MESH).wait_recv()
        for cp in cps:
            cp.wait_send()
        mine.wait()

    return pl.pallas_call(
        body, name=name, in_specs=[HBM], out_specs=HBM,
        out_shape=jax.ShapeDtypeStruct((N_DEV,) + block.shape, block.dtype),
        scratch_shapes=[pltpu.SemaphoreType.DMA((N_DEV - 1,)), pltpu.SemaphoreType.DMA((N_DEV - 1,)),
                        pltpu.SemaphoreType.DMA],
    )(block)


def _pair_sum(grad, recv, c_arr, name):
    _, _, rh, cols = grad.shape
    tr = _row_tile(rh, cols * 4)

    def body(c_ref, g_ref, r_ref, o_ref):
        del c_ref
        o_ref[...] = (g_ref[...].astype(F32) + r_ref[...].astype(F32)).astype(o_ref.dtype)

    spec = pl.BlockSpec((None, tr, cols), lambda j, i, c_ref: (j, i, 0))
    return pl.pallas_call(
        body, name=name,
        grid_spec=pltpu.PrefetchScalarGridSpec(
            num_scalar_prefetch=1, grid=(N_CHIPS, rh // tr),
            in_specs=[pl.BlockSpec((None, None, tr, cols), lambda j, i, c_ref: (j, c_ref[0], i, 0)), spec],
            out_specs=spec),
        out_shape=jax.ShapeDtypeStruct(recv.shape, BF16), compiler_params=_params(32),
    )(c_arr, grad, recv)


def _chip_sum(parts, name):
    _, rh, cols = parts.shape
    tr = _row_tile(rh, cols * 4)

    def body(p_ref, o_ref):
        acc = p_ref[0].astype(F32)
        for k in range(1, N_CHIPS):
            acc = acc + p_ref[k].astype(F32)
        o_ref[...] = acc

    return pl.pallas_call(
        body, name=name, grid=(rh // tr,),
        in_specs=[pl.BlockSpec((N_CHIPS, tr, cols), lambda i: (0, i, 0))],
        out_specs=pl.BlockSpec((tr, cols), lambda i: (i, 0)),
        out_shape=jax.ShapeDtypeStruct((rh, cols), F32), compiler_params=_params(32),
    )(parts)


def _adamw_math(w, g, m, v):
    m = ADAM_B1 * m + (1.0 - ADAM_B1) * g
    v = ADAM_B2 * v + (1.0 - ADAM_B2) * (g * g)
    m_hat = m / (1.0 - ADAM_B1 ** ADAM_STEP)
    v_hat = v / (1.0 - ADAM_B2 ** ADAM_STEP)
    delta = -ADAM_LR * (m_hat / (jnp.sqrt(v_hat) + ADAM_EPS) + ADAM_WD * w)
    return delta, m, v


def _adamw(w, g, m, v, name):
    rows, cols = w.shape
    tr = _row_tile(rows, cols * 4)

    def body(w_ref, g_ref, m_ref, v_ref, d_ref, nm_ref, nv_ref):
        d_ref[...], nm_ref[...], nv_ref[...] = _adamw_math(w_ref[...], g_ref[...], m_ref[...], v_ref[...])

    spec = pl.BlockSpec((tr, cols), lambda i: (i, 0))
    return pl.pallas_call(
        body, name=name, grid=(rows // tr,), in_specs=[spec] * 4, out_specs=[spec] * 3,
        out_shape=[jax.ShapeDtypeStruct(w.shape, F32)] * 3, compiler_params=_params(32),
    )(w, g, m, v)


def _sum8_adamw(parts, w, m, v):
    _, rows, cols = parts.shape
    tr = _row_tile(rows, cols * 4, VMEM_MIB // 4)

    def body(p_ref, w_ref, m_ref, v_ref, g_ref, d_ref, nm_ref, nv_ref):
        g = p_ref[0]
        for k in range(1, N_DEV):
            g = g + p_ref[k]
        g_ref[...] = g
        d_ref[...], nm_ref[...], nv_ref[...] = _adamw_math(w_ref[...], g, m_ref[...], v_ref[...])

    spec = pl.BlockSpec((tr, cols), lambda i: (i, 0))
    return pl.pallas_call(
        body, name="small_sum_adamw", grid=(rows // tr,),
        in_specs=[pl.BlockSpec((N_DEV, tr, cols), lambda i: (0, i, 0))] + [spec] * 3, out_specs=[spec] * 4,
        out_shape=[jax.ShapeDtypeStruct((rows, cols), F32)] * 4, compiler_params=_params(32),
    )(parts, w, m, v)


def _halves(a):
    r, c = a.shape
    return a.reshape(2, r // 2, c)


def kernel(x, norm_mix_g, w_in, conv_w, conv_b, lru_wa, lru_ba, lru_wx, lru_bx, lru_lambda, w_proj_attn, w_proj_lru, w_out, norm_mlp_g, w_up, w_down, norm_final_g, loss_target, m_norm_mix_g, m_w_in, m_conv_w, m_conv_b, m_lru_wa, m_lru_ba, m_lru_wx, m_lru_bx, m_lru_lambda, m_w_proj_attn, m_w_proj_lru, m_w_out, m_norm_mlp_g, m_w_up, m_w_down, m_norm_final_g, v_norm_mix_g, v_w_in, v_conv_w, v_conv_b, v_lru_wa, v_lru_ba, v_lru_wx, v_lru_bx, v_lru_lambda, v_w_proj_attn, v_w_proj_lru, v_w_out, v_norm_mlp_g, v_w_up, v_w_down, v_norm_final_g):
    s, d = x.shape[1], x.shape[2]
    ff = w_up.shape[2] * N_CHIPS
    heads = d // HEAD_DIM
    u = d // 4
    assert s % (max(DILATIONS) * ATTN_BLK) == 0 and d % (4 * HEAD_DIM) == 0 and ff == 4 * d
    xs = x[0]
    target = loss_target[0]
    gf = norm_final_g.reshape(1, d)
    wa, wx = lru_wa[0], lru_wx[0]
    cidx = lax.axis_index("c").astype(jnp.int32).reshape(1)
    chip = 2 * lax.axis_index("x") + lax.axis_index("y")
    slopes = jnp.broadcast_to(
        (2.0 ** (-8.0 * jnp.arange(1, heads + 1, dtype=F32) / heads))[:, None, None], (heads, 1, HEAD_DIM))

    big = [w_in[0], w_proj_attn[0], w_proj_lru[0], w_out[0], w_up[0], w_down[0]]
    gathered = _gather_weights([_halves(w.astype(BF16)) for w in big])
    w_in_g = gathered[0].reshape(N_CHIPS, d, N_SLOTS * u)
    wpa_g = gathered[1].reshape(d, d)
    wpl_g = gathered[2].reshape(d, d)
    wout_g = gathered[3].reshape(d, d)
    wup_g = gathered[4].reshape(N_CHIPS, d, d)
    wdown_g = gathered[5].reshape(ff, d)
    cw_pad = jnp.concatenate([conv_w[0], jnp.zeros((8 - CONV_TAPS, u), F32)], axis=0)
    cw_all = _all_gather8(cw_pad, "gather_conv_w")
    conv_w_full = jnp.concatenate([cw_all[2 * j, :CONV_TAPS] for j in range(N_CHIPS)], axis=1)

    xn = _rms_fwd(xs, norm_mix_g, "norm_mix")
    proj_a = _proj_in(xn, w_in_g, 0, 4 * N_QKV, N_QKV)
    proj_b = _proj_in(xn, w_in_g, 4 * N_QKV, 4 * (N_SLOTS - N_QKV), N_SLOTS - N_QKV)
    y_attn, lse = _attn_fwd(proj_a, slopes)
    y_lru, h_lru = _lru_fwd(proj_b, conv_w_full, conv_b, wa, lru_ba, wx, lru_bx, lru_lambda)

    tn = u
    sd_f32 = jax.ShapeDtypeStruct((s, d), F32)
    sd_bf16 = jax.ShapeDtypeStruct((s, d), BF16)
    col = pl.BlockSpec((s, tn), lambda i, j, k: (0, j))

    def slot(n):
        return pl.BlockSpec((None, s, tn), lambda i, j, k: (n, 0, j))

    p_attn = _mm_nn("proj_attn", y_attn, wpa_g, [], [], [sd_f32], [col], _store, tn)[0]

    def merge(acc, extras, outs):
        pa_ref, ga_ref, gl_ref = extras
        merged = _sigmoid(ga_ref[...]) * pa_ref[...] + _sigmoid(gl_ref[...]) * acc
        outs[0][...] = merged.astype(BF16)
        outs[1][...] = acc

    tn2 = max(HEAD_DIM, u // 2)
    col2 = pl.BlockSpec((s, tn2), lambda i, j, k: (0, j))

    def slot2(n):
        return pl.BlockSpec((None, s, tn2), lambda i, j, k: (n, 0, j))

    merged, p_lru = _mm_nn("proj_lru_merge", y_lru, wpl_g, [p_attn, proj_b, proj_b], [col2, slot2(2), slot2(3)],
                           [sd_bf16, sd_f32], [col2, col2], merge, tn2)

    def add_resid(acc, extras, outs):
        outs[0][...] = extras[0][...] + acc

    h1 = _mm_nn("w_out_resid", merged, wout_g, [xs], [col], [sd_f32], [col], add_resid, tn)[0]
    hn = _rms_fwd(h1, norm_mlp_g, "norm_mlp")

    def relu_sq(acc, extras, outs):
        r = jnp.maximum(acc, 0.0)
        outs[0][...] = (r * r).astype(BF16)
        outs[1][...] = r.astype(BF16)

    sf_bf16 = jax.ShapeDtypeStruct((s, ff), BF16)
    hid, relu_up = _mm(
        "w_up_relu2", [hn, wup_g],
        [pl.BlockSpec((s, d), lambda i, j, k: (0, 0)),
         pl.BlockSpec((None, d, tn), lambda i, j, k: (j // 4, 0, j % 4))],
        [sf_bf16, sf_bf16], [col, col], (1, ff // tn, 1), NN, relu_sq)
    h2 = _mm(
        "w_down_resid", [hid, wdown_g, h1],
        [pl.BlockSpec((s, d), lambda i, j, k: (0, k)), pl.BlockSpec((d, tn), lambda i, j, k: (k, j)), col],
        [sd_f32], [col], (1, d // tn, ff // d), NN, add_resid, nk=ff // d, acc_shape=(s, tn))[0]
    loss_part, dh2, dh2_b, d_gf = _loss_head(h2, gf, target)
    loss = lax.psum(loss_part[0, 0], AXES)

    def relu_sq_bwd(acc, extras, outs):
        outs[0][...] = (acc * (2.0 * extras[0][...].astype(F32))).astype(BF16)

    dup = _mm_nt("d_hid", dh2_b, wdown_g, [relu_up], [col], [sf_bf16], [col], relu_sq_bwd, tn)[0]
    tok_d = pl.BlockSpec((s, d), lambda i, j: (0, 0))
    g_wdown = _mm_tn(
        "g_w_down", hid, dh2_b, pl.BlockSpec((s, d), lambda i, j: (0, i)),
        pl.BlockSpec((s, tn), lambda i, j: (0, j)), jax.ShapeDtypeStruct((ff, d), BF16),
        pl.BlockSpec((d, tn), lambda i, j: (i, j)), (ff // d, d // tn), d, tn, s)
    dhn = _mm(
        "d_hn", [dup, wup_g],
        [pl.BlockSpec((s, d), lambda i, j, k: (0, k)), pl.BlockSpec((None, tn, d), lambda i, j, k: (k, j, 0))],
        [sd_f32], [col], (1, d // tn, ff // d), NT, _store, nk=ff // d, acc_shape=(s, tn))[0]
    g_wup = _mm_tn(
        "g_w_up", hn, dup, tok_d, pl.BlockSpec((s, tn), lambda i, j: (0, j)),
        jax.ShapeDtypeStruct((N_CHIPS, d, d), BF16), pl.BlockSpec((None, d, tn), lambda i, j: (j // 4, 0, j % 4)),
        (1, ff // tn), d, tn, s)
    dh1, dh1_b, d_gmlp = _rms_bwd(h1, norm_mlp_g, dhn, dh2, "norm_mlp_bwd")

    g_wout = _mm_tn(
        "g_w_out", merged, dh1_b, tok_d, pl.BlockSpec((s, tn), lambda i, j: (0, j)),
        jax.ShapeDtypeStruct((d, d), BF16), pl.BlockSpec((d, tn), lambda i, j: (0, j)), (1, d // tn), d, tn, s)

    def merge_bwd(acc, extras, outs):
        pa_ref, pl_ref, ga_ref, gl_ref = extras
        sa, sl = _sigmoid(ga_ref[...]), _sigmoid(gl_ref[...])
        outs[0][...] = (acc * sa).astype(BF16)
        outs[1][...] = (acc * sl).astype(BF16)
        outs[2][0] = (acc * pa_ref[...] * (sa * (1.0 - sa))).astype(BF16)
        outs[2][1] = (acc * pl_ref[...] * (sl * (1.0 - sl))).astype(BF16)

    nb = N_SLOTS - N_QKV
    d_pa, d_pl, dproj_b = _mm_nt(
        "d_merged", dh1_b, wout_g, [p_attn, p_lru, proj_b, proj_b], [col2, col2, slot2(2), slot2(3)],
        [sd_bf16, sd_bf16, jax.ShapeDtypeStruct((nb, s, d), BF16)],
        [col2, col2, pl.BlockSpec((2, s, tn2), lambda i, j, k: (1, 0, j))], merge_bwd, tn2)
    dy_attn = _mm_nt("d_y_attn", d_pa, wpa_g, [], [], [sd_f32], [col], _store, tn)[0]
    dy_lru = _mm_nt("d_y_lru", d_pl, wpl_g, [], [], [sd_f32], [col], _store, tn)[0]
    g_wpa = _mm_tn(
        "g_w_proj_attn", y_attn, d_pa, tok_d, pl.BlockSpec((s, tn), lambda i, j: (0, j)),
        jax.ShapeDtypeStruct((d, d), BF16), pl.BlockSpec((d, tn), lambda i, j: (0, j)), (1, d // tn), d, tn, s)
    g_wpl = _mm_tn(
        "g_w_proj_lru", y_lru, d_pl, tok_d, pl.BlockSpec((s, tn), lambda i, j: (0, j)),
        jax.ShapeDtypeStruct((d, d), BF16), pl.BlockSpec((d, tn), lambda i, j: (0, j)), (1, d // tn), d, tn, s)

    dproj_a = _attn_bwd(proj_a, slopes, y_attn, lse, dy_attn)
    dproj_b, d_cw, d_cb, d_wa, d_ba, d_wx, d_bx, d_lam = _lru_bwd(
        proj_b, h_lru, dy_lru, conv_w_full, conv_b, wa, lru_ba, wx, lru_bx, lru_lambda, dproj_b)
    dxn = _dxn(dproj_a, dproj_b, w_in_g, 2 * tn)
    per = N_SLOTS
    g_win_shape = jax.ShapeDtypeStruct((N_CHIPS, d, N_SLOTS * u), BF16)

    def g_win_part(name, dproj, first, prev):
        n_units = 4 * dproj.shape[0]
        return _mm_tn(
            name, xn, dproj, tok_d, pl.BlockSpec((None, s, u), lambda i, j: (j // 4, 0, j % 4)),
            g_win_shape, pl.BlockSpec((None, d, u), lambda i, j: ((j + first) // per, 0, (j + first) % per)),
            (1, n_units), d, u, s, aliases=None if prev is None else {2: 0}, extra=prev)

    g_win = g_win_part("g_w_in_qkv", dproj_a, 0, None)
    g_win = g_win_part("g_w_in_rest", dproj_b, 4 * N_QKV, g_win)
    grad_x, _, d_gmix = _rms_bwd(xs, norm_mix_g, dxn, dh1, "norm_mix_bwd")

    names = ["w_in", "w_proj_attn", "w_proj_lru", "w_out", "w_up", "w_down"]
    shard_shapes = [w.shape for w in big]
    grads = [g_win, g_wpa, g_wpl, g_wout, g_wup, g_wdown]
    grads4 = [g.reshape(N_CHIPS, 2, r // 2, c) for g, (r, c) in zip(grads, shard_shapes)]
    from_sibling = _pair_exchange(grads4)
    chip_sums = [_pair_sum(g, r, cidx, "pair_sum_" + nm) for g, r, nm in zip(grads4, from_sibling, names)]
    by_chip = _chip_exchange(chip_sums)
    halves = [_chip_sum(p, "chip_sum_" + nm) for p, nm in zip(by_chip, names)]
    full = _half_exchange(halves)
    big_m = [m_w_in[0], m_w_proj_attn[0], m_w_proj_lru[0], m_w_out[0], m_w_up[0], m_w_down[0]]
    big_v = [v_w_in[0], v_w_proj_attn[0], v_w_proj_lru[0], v_w_out[0], v_w_up[0], v_w_down[0]]
    big_out = {}
    for nm, w, g, m, v, shp in zip(names, big, full, big_m, big_v, shard_shapes):
        g = g.reshape(shp)
        dl, nm_, nv_ = _adamw(w, g, m, v, "adamw_" + nm)
        big_out[nm] = tuple(a[None] for a in (g, dl, nm_, nv_))

    mat_rows = heads * HEAD_DIM * HEAD_DIM // d
    vec_names = ["norm_mix_g", "conv_b", "lru_ba", "lru_bx", "lru_lambda", "norm_mlp_g", "norm_final_g"]

    def pack(wa_, wx_, cw_, vecs):
        rows = [wa_.reshape(mat_rows, d), wx_.reshape(mat_rows, d), cw_] + [a.reshape(1, d) for a in vecs]
        n = sum(a.shape[0] for a in rows)
        return jnp.concatenate(rows + [jnp.zeros((-n % 64, d), F32)], axis=0)

    zero_cw = jnp.zeros((CONV_TAPS, d), F32)
    small_g = pack(d_wa, d_wx, d_cw, [d_gmix, d_cb, d_ba, d_bx, d_lam, d_gmlp, d_gf])
    small_w = pack(wa, wx, zero_cw, [norm_mix_g, conv_b, lru_ba, lru_bx, lru_lambda, norm_mlp_g, norm_final_g])
    small_m = pack(m_lru_wa[0], m_lru_wx[0], zero_cw,
                   [m_norm_mix_g, m_conv_b, m_lru_ba, m_lru_bx, m_lru_lambda, m_norm_mlp_g, m_norm_final_g])
    small_v = pack(v_lru_wa[0], v_lru_wx[0], zero_cw,
                   [v_norm_mix_g, v_conv_b, v_lru_ba, v_lru_bx, v_lru_lambda, v_norm_mlp_g, v_norm_final_g])
    small_parts = _all_gather8(small_g, "gather_small_grads")
    small = _sum8_adamw(small_parts, small_w, small_m, small_v)
    g_cw = lax.dynamic_slice(small[0][2 * mat_rows:2 * mat_rows + CONV_TAPS], (0, chip * u), (CONV_TAPS, u))
    cw_out = (g_cw,) + tuple(_adamw(conv_w[0], g_cw, m_conv_w[0], v_conv_w[0], "adamw_conv_w"))

    def small_leaf(kind, name):
        a = small[kind]
        if name == "lru_wa":
            return a[0:mat_rows].reshape(lru_wa.shape)
        if name == "lru_wx":
            return a[mat_rows:2 * mat_rows].reshape(lru_wx.shape)
        if name == "conv_w":
            return cw_out[kind][None]
        row = a[2 * mat_rows + CONV_TAPS + vec_names.index(name)]
        return row if name == "norm_final_g" else row[None]

    order = ["norm_mix_g", "w_in", "conv_w", "conv_b", "lru_wa", "lru_ba", "lru_wx", "lru_bx", "lru_lambda",
             "w_proj_attn", "w_proj_lru", "w_out", "norm_mlp_g", "w_up", "w_down", "norm_final_g"]
    outs = [loss, grad_x[None]]
    for kind in range(4):
        for name in order:
            outs.append(big_out[name][kind] if name in big_out else small_leaf(kind, name))
    return tuple(outs)
```

```python
import functools

import jax
import jax.numpy as jnp
from jax import lax
from jax.experimental import pallas as pl
from jax.experimental.pallas import tpu as pltpu

F32 = jnp.float32
BF16 = jnp.bfloat16
MESH = pl.DeviceIdType.MESH
AXES = ("x", "y", "c")

N_CHIPS = 4
N_DEV = 8
HEAD_DIM = 128
ATTN_BLK = 128
DILATIONS = (1, 4, 16)
ATTN_UNROLL = 4
CONV_TAPS = 4
LRU_C = 8.0
EPS = 1e-6
N_SLOTS = 7
N_QKV = 3
VMEM_MIB = 2 ** 20
VMEM_V7X = 64 * VMEM_MIB

ADAM_LR = 0.001
ADAM_B1 = 0.9
ADAM_B2 = 0.999
ADAM_EPS = 1e-08
ADAM_WD = 0.01
ADAM_STEP = 10

NN = (((1,), (0,)), ((), ()))
NT = (((1,), (1,)), ((), ()))
TN = (((0,), (0,)), ((), ()))


def _params(vmem_mib=None, **kw):
    limit = None if vmem_mib is None else min(vmem_mib * VMEM_MIB, VMEM_V7X - 8 * VMEM_MIB)
    return pltpu.CompilerParams(vmem_limit_bytes=limit, **kw)


def _row_tile(rows, row_bytes, budget=VMEM_MIB):
    t = rows
    while t % 16 == 0 and t * row_bytes > budget:
        t //= 2
    return t


def _dot(a, b, dims):
    return lax.dot_general(a.astype(BF16), b.astype(BF16), dims, preferred_element_type=F32)


def _sigmoid(x):
    return jax.nn.sigmoid(x)


def _rms_fwd(x, g, name):
    s, d = x.shape
    tm = _row_tile(s, d * 4)

    def body(x_ref, g_ref, o_ref):
        xf = x_ref[...]
        r = lax.rsqrt(jnp.mean(xf * xf, axis=-1, keepdims=True) + EPS)
        o_ref[...] = (xf * r * g_ref[...]).astype(o_ref.dtype)

    return pl.pallas_call(
        body, name=name, grid=(s // tm,),
        in_specs=[pl.BlockSpec((tm, d), lambda i: (i, 0)), pl.BlockSpec((1, d), lambda i: (0, 0))],
        out_specs=pl.BlockSpec((tm, d), lambda i: (i, 0)),
        out_shape=jax.ShapeDtypeStruct((s, d), BF16), compiler_params=_params(32),
    )(x, g)


def _rms_bwd(x, g, dy, resid, name):
    s, d = x.shape
    tm = _row_tile(s, d * 4)

    def body(x_ref, g_ref, dy_ref, res_ref, dx_ref, dxb_ref, dg_ref):
        xf = x_ref[...]
        r = lax.rsqrt(jnp.mean(xf * xf, axis=-1, keepdims=True) + EPS)
        xh = xf * r
        dyv = dy_ref[...]
        dxh = dyv * g_ref[...]
        dx = r * (dxh - xh * jnp.mean(dxh * xh, axis=-1, keepdims=True)) + res_ref[...]
        dx_ref[...] = dx
        dxb_ref[...] = dx.astype(BF16)
        part = jnp.sum(dyv * xh, axis=0, keepdims=True)

        @pl.when(pl.program_id(0) == 0)
        def _():
            dg_ref[...] = part

        @pl.when(pl.program_id(0) > 0)
        def _():
            dg_ref[...] += part

    row = pl.BlockSpec((tm, d), lambda i: (i, 0))
    vec = pl.BlockSpec((1, d), lambda i: (0, 0))
    return pl.pallas_call(
        body, name=name, grid=(s // tm,),
        in_specs=[row, vec, row, row], out_specs=[row, row, vec],
        out_shape=[jax.ShapeDtypeStruct((s, d), F32), jax.ShapeDtypeStruct((s, d), BF16),
                   jax.ShapeDtypeStruct((1, d), F32)],
        compiler_params=_params(32),
    )(x, g, dy, resid)


def _loss_head(h2, g, target):
    s, d = h2.shape
    tm = _row_tile(s, d * 4)

    def body(x_ref, g_ref, t_ref, loss_ref, dx_ref, dxb_ref, dg_ref):
        xf = x_ref[...]
        gv = g_ref[...]
        r = lax.rsqrt(jnp.mean(xf * xf, axis=-1, keepdims=True) + EPS)
        xh = xf * r
        err = xh * gv - t_ref[...]
        part = jnp.sum(jnp.sum(err * err, axis=1, keepdims=True), axis=0, keepdims=True) * (0.5 / d)
        dyv = err * (1.0 / d)
        dxh = dyv * gv
        dx = r * (dxh - xh * jnp.mean(dxh * xh, axis=-1, keepdims=True))
        dx_ref[...] = dx
        dxb_ref[...] = dx.astype(BF16)
        dgp = jnp.sum(dyv * xh, axis=0, keepdims=True)

        @pl.when(pl.program_id(0) == 0)
        def _():
            dg_ref[...] = dgp
            loss_ref[...] = jnp.broadcast_to(part, loss_ref.shape)

        @pl.when(pl.program_id(0) > 0)
        def _():
            dg_ref[...] += dgp
            loss_ref[...] += jnp.broadcast_to(part, loss_ref.shape)

    row = pl.BlockSpec((tm, d), lambda i: (i, 0))
    vec = pl.BlockSpec((1, d), lambda i: (0, 0))
    return pl.pallas_call(
        body, name="loss_head", grid=(s // tm,),
        in_specs=[row, vec, row],
        out_specs=[pl.BlockSpec((8, 128), lambda i: (0, 0)), row, row, vec],
        out_shape=[jax.ShapeDtypeStruct((8, 128), F32), jax.ShapeDtypeStruct((s, d), F32),
                   jax.ShapeDtypeStruct((s, d), BF16), jax.ShapeDtypeStruct((1, d), F32)],
        compiler_params=_params(32),
    )(h2, g, target)


def _mm(name, operands, in_specs, out_shape, out_specs, grid, dims, epilogue, nk=1, acc_shape=None,
        vmem_mib=56, aliases=None):
    n_in = len(operands)
    n_out = len(out_shape)

    def body(*refs):
        a_ref, b_ref = refs[0], refs[1]
        extras = refs[2:n_in]
        outs = refs[n_in:n_in + n_out]

        def prod():
            return _dot(a_ref[...], b_ref[...], dims)

        if nk == 1:
            epilogue(prod(), extras, outs)
        else:
            acc = refs[n_in + n_out]
            k = pl.program_id(2)

            @pl.when(k == 0)
            def _():
                acc[...] = prod()

            @pl.when(k > 0)
            def _():
                acc[...] += prod()

            @pl.when(k == nk - 1)
            def _():
                epilogue(acc[...], extras, outs)

    scratch = [] if nk == 1 else [pltpu.VMEM(acc_shape, F32)]
    return pl.pallas_call(
        body, name=name, grid=grid, in_specs=in_specs, out_specs=out_specs, out_shape=out_shape,
        scratch_shapes=scratch, input_output_aliases=aliases or {},
        compiler_params=_params(vmem_mib),
    )(*operands)


def _store(acc, extras, outs):
    outs[0][...] = acc.astype(outs[0].dtype)


def _proj_in(xn, w_in_g, first_unit, n_units, n_slots):
    s, d = xn.shape
    u = d // 4
    per = N_SLOTS

    return _mm(
        "proj_in_%d" % first_unit, [xn, w_in_g],
        [pl.BlockSpec((s, d), lambda i, j, k: (0, 0)),
         pl.BlockSpec((None, d, u), lambda i, j, k: ((j + first_unit) // per, 0, (j + first_unit) % per))],
        [jax.ShapeDtypeStruct((n_slots, s, d), F32)],
        [pl.BlockSpec((None, s, u), lambda i, j, k: (j // 4, 0, j % 4))],
        (1, n_units, 1), NN, _store)[0]


def _mm_nn(name, a, b, extras, extra_specs, out_shape, out_specs, epilogue, tn, aliases=None):
    s, kdim = a.shape
    n = b.shape[1]
    return _mm(
        name, [a, b] + list(extras),
        [pl.BlockSpec((s, kdim), lambda i, j, k: (0, 0)), pl.BlockSpec((kdim, tn), lambda i, j, k: (0, j))]
        + list(extra_specs),
        out_shape, out_specs, (1, n // tn, 1), NN, epilogue, aliases=aliases)


def _mm_nt(name, a, b, extras, extra_specs, out_shape, out_specs, epilogue, tn, aliases=None):
    s, kdim = a.shape
    n = b.shape[0]
    return _mm(
        name, [a, b] + list(extras),
        [pl.BlockSpec((s, kdim), lambda i, j, k: (0, 0)), pl.BlockSpec((tn, kdim), lambda i, j, k: (j, 0))]
        + list(extra_specs),
        out_shape, out_specs, (1, n // tn, 1), NT, epilogue, aliases=aliases)


def _mm_tn(name, a, b, a_spec, b_spec, out_shape, out_spec, grid, m, tn, s, aliases=None, extra=None):
    ch = 256
    n_in = 2 if extra is None else 3

    def body(*refs):
        a_ref, b_ref = refs[0], refs[1]
        o_ref, at_ref = refs[n_in], refs[n_in + 1]

        @pl.when(pl.program_id(1) == 0)
        def _():
            for c0 in range(0, s, ch):
                at_ref[:, c0:c0 + ch] = a_ref[c0:c0 + ch, :].astype(F32).T.astype(BF16)

        o_ref[...] = _dot(at_ref[...], b_ref[...], NN).astype(o_ref.dtype)

    operands = [a, b] + ([] if extra is None else [extra])
    in_specs = [a_spec, b_spec] + ([] if extra is None else [pl.BlockSpec(memory_space=pl.ANY)])
    return pl.pallas_call(
        body, name=name, grid=grid, in_specs=in_specs, out_specs=out_spec, out_shape=out_shape,
        scratch_shapes=[pltpu.VMEM((m, s), BF16)], input_output_aliases=aliases or {},
        compiler_params=_params(56),
    )(*operands)


def _dxn(dproj_a, dproj_b, w_in_g, tn):
    n_a, s, d = dproj_a.shape
    u = d // 4
    ua = 4 * n_a
    nk = 4 * N_SLOTS
    per = N_SLOTS

    def body(a_ref, b_ref, w_ref, o_ref):
        k = pl.program_id(2)

        @pl.when(k == 0)
        def _():
            o_ref[...] = jnp.zeros_like(o_ref)

        @pl.when(k < ua)
        def _():
            o_ref[...] += _dot(a_ref[...], w_ref[...], NT)

        @pl.when(k >= ua)
        def _():
            o_ref[...] += _dot(b_ref[...], w_ref[...], NT)

    def a_map(i, j, k):
        kk = jnp.minimum(k, ua - 1)
        return (kk // 4, 0, kk % 4)

    def b_map(i, j, k):
        kk = jnp.maximum(k - ua, 0)
        return (kk // 4, 0, kk % 4)

    return pl.pallas_call(
        body, name="dxn", grid=(1, d // tn, nk),
        in_specs=[pl.BlockSpec((None, s, u), a_map), pl.BlockSpec((None, s, u), b_map),
                  pl.BlockSpec((None, tn, u), lambda i, j, k: (k // per, j, k % per))],
        out_specs=pl.BlockSpec((s, tn), lambda i, j, k: (0, j)),
        out_shape=jax.ShapeDtypeStruct((s, d), F32),
        compiler_params=_params(48),
    )(dproj_a, dproj_b, w_in_g)


def _attn_masks(nblk, slope, dil):
    nkeys = 2 * ATTN_BLK if nblk > 1 else ATTN_BLK
    ii = lax.broadcasted_iota(jnp.int32, (ATTN_BLK, nkeys), 0)
    jj = lax.broadcasted_iota(jnp.int32, (ATTN_BLK, nkeys), 1)
    diff = (ATTN_BLK + ii - jj) if nblk > 1 else (ii - jj)
    band = (diff >= 0) & (diff <= ATTN_BLK)
    bias = -(slope * float(dil)) * diff.astype(F32)
    return band, bias, jj


def _deinterleave(dst_ref, src_ref, dil, s):
    seg = s // dil
    if dil == 1:
        dst_ref[...] = src_ref[...].astype(dst_ref.dtype)
    else:
        for r in range(dil):
            dst_ref[r * seg:(r + 1) * seg, :] = src_ref[pl.ds(r, seg, stride=dil), :].astype(dst_ref.dtype)


def _attn_fwd(proj_a, slopes):
    _, s, d = proj_a.shape
    heads = d // HEAD_DIM
    scale = HEAD_DIM ** -0.5
    n_t = s // ATTN_BLK
    ng = len(DILATIONS)

    def body(q_ref, k_ref, v_ref, sl_ref, o_ref, lse_ref, qd, kd, vd, od, ld, og, lg):
        slope = sl_ref[...][:, :1]
        for g, dil in enumerate(DILATIONS):
            nblk = s // dil // ATTN_BLK
            _deinterleave(qd, q_ref, dil, s)
            _deinterleave(kd, k_ref, dil, s)
            _deinterleave(vd, v_ref, dil, s)
            band, bias, jj = _attn_masks(nblk, slope, dil)

            def blk(t, carry, nblk=nblk, band=band, bias=bias, jj=jj):
                cur = pl.multiple_of(t * ATTN_BLK, ATTN_BLK)
                q = qd[pl.ds(cur, ATTN_BLK), :]
                if nblk > 1:
                    prev = pl.multiple_of(jnp.maximum(t - 1, 0) * ATTN_BLK, ATTN_BLK)
                    kk = jnp.concatenate([kd[pl.ds(prev, ATTN_BLK), :], kd[pl.ds(cur, ATTN_BLK), :]], axis=0)
                    vv = jnp.concatenate([vd[pl.ds(prev, ATTN_BLK), :], vd[pl.ds(cur, ATTN_BLK), :]], axis=0)
                    valid = band & (jj >= jnp.where(t % nblk == 0, ATTN_BLK, 0))
                else:
                    kk = kd[pl.ds(cur, ATTN_BLK), :]
                    vv = vd[pl.ds(cur, ATTN_BLK), :]
                    valid = band
                sc = _dot(q, kk, NT) * scale + bias
                sc = jnp.where(valid, sc, -jnp.inf)
                m = jnp.max(sc, axis=1, keepdims=True)
                p = jnp.exp(sc - m)
                l = jnp.sum(p, axis=1, keepdims=True)
                od[pl.ds(cur, ATTN_BLK), :] = _dot(p, vv, NN) / l
                ld[pl.ds(cur, ATTN_BLK), :] = jnp.broadcast_to(m + jnp.log(l), (ATTN_BLK, HEAD_DIM))
                return carry

            lax.fori_loop(0, n_t, blk, 0, unroll=ATTN_UNROLL)
            seg = s // dil
            if dil == 1:
                og[g] = od[...]
                lg[g] = ld[...]
            else:
                for r in range(dil):
                    og[g, pl.ds(r, seg, stride=dil), :] = od[r * seg:(r + 1) * seg, :]
                    lg[g, pl.ds(r, seg, stride=dil), :] = ld[r * seg:(r + 1) * seg, :]

        ch = 256

        def combine(c, carry):
            rows = pl.ds(pl.multiple_of(c * ch, ch), ch)
            ls = [lg[g, rows, :] for g in range(ng)]
            mx = functools.reduce(jnp.maximum, ls)
            es = [jnp.exp(x - mx) for x in ls]
            den = functools.reduce(jnp.add, es)
            num = functools.reduce(jnp.add, [es[g] * og[g, rows, :] for g in range(ng)])
            o_ref[rows, :] = (num / den).astype(o_ref.dtype)
            lse_ref[rows, :] = mx + jnp.log(den)
            return carry

        lax.fori_loop(0, s // ch, combine, 0)

    def col(slot):
        return pl.BlockSpec((None, s, HEAD_DIM), lambda h: (slot, 0, h))

    head = pl.BlockSpec((s, HEAD_DIM), lambda h: (0, h))
    return pl.pallas_call(
        body, name="attn_fwd", grid=(heads,),
        in_specs=[col(0), col(1), col(2), pl.BlockSpec((None, 1, HEAD_DIM), lambda h: (h, 0, 0))],
        out_specs=[head, head],
        out_shape=[jax.ShapeDtypeStruct((s, d), BF16), jax.ShapeDtypeStruct((s, d), F32)],
        scratch_shapes=[pltpu.VMEM((s, HEAD_DIM), BF16)] * 3 + [pltpu.VMEM((s, HEAD_DIM), F32)] * 2
        + [pltpu.VMEM((ng, s, HEAD_DIM), F32)] * 2,
        compiler_params=_params(40),
    )(proj_a, proj_a, proj_a, slopes)


def _attn_bwd(proj_a, slopes, y_attn, lse, dy):
    _, s, d = proj_a.shape
    heads = d // HEAD_DIM
    scale = HEAD_DIM ** -0.5
    n_t = s // ATTN_BLK

    def body(q_ref, k_ref, v_ref, sl_ref, o_ref, lse_ref, dy_ref, out_ref,
             qd, kd, vd, dod, lsd, dld, delta, dqd, dkd, dvd, dqa, dka, dva):
        slope = sl_ref[...][:, :1]
        dyv = dy_ref[...]
        delta[...] = jnp.broadcast_to(
            jnp.sum(dyv * o_ref[...].astype(F32), axis=1, keepdims=True), (s, HEAD_DIM))
        for g, dil in enumerate(DILATIONS):
            nblk = s // dil // ATTN_BLK
            seg = s // dil
            _deinterleave(qd, q_ref, dil, s)
            _deinterleave(kd, k_ref, dil, s)
            _deinterleave(vd, v_ref, dil, s)
            _deinterleave(dod, dy_ref, dil, s)
            _deinterleave(lsd, lse_ref, dil, s)
            _deinterleave(dld, delta, dil, s)
            dkd[...] = jnp.zeros_like(dkd)
            dvd[...] = jnp.zeros_like(dvd)
            band, bias, jj = _attn_masks(nblk, slope, dil)

            def blk(t, carry, nblk=nblk, band=band, bias=bias, jj=jj):
                cur = pl.multiple_of(t * ATTN_BLK, ATTN_BLK)
                rows = pl.ds(cur, ATTN_BLK)
                q = qd[rows, :]
                do = dod[rows, :]
                lse_b = lsd[rows, :]
                dl_b = dld[rows, :]
                if nblk > 1:
                    prev = pl.multiple_of(jnp.maximum(t - 1, 0) * ATTN_BLK, ATTN_BLK)
                    prows = pl.ds(prev, ATTN_BLK)
                    kk = jnp.concatenate([kd[prows, :], kd[rows, :]], axis=0)
                    vv = jnp.concatenate([vd[prows, :], vd[rows, :]], axis=0)
                    valid = band & (jj >= jnp.where(t % nblk == 0, ATTN_BLK, 0))
                    lse_b = jnp.concatenate([lse_b, lse_b], axis=1)
                    dl_b = jnp.concatenate([dl_b, dl_b], axis=1)
                else:
                    kk = kd[rows, :]
                    vv = vd[rows, :]
                    valid = band
                sc = _dot(q, kk, NT) * scale + bias
                p = jnp.where(valid, jnp.exp(sc - lse_b), 0.0)
                dp = _dot(do, vv, NT)
                ds = p * (dp - dl_b)
                dv_b = _dot(p, do, TN)
                dk_b = _dot(ds, q, TN) * scale
                dqd[rows, :] = _dot(ds, kk, NN) * scale
                if nblk > 1:
                    dkd[prows, :] += dk_b[:ATTN_BLK]
                    dvd[prows, :] += dv_b[:ATTN_BLK]
                    dkd[rows, :] += dk_b[ATTN_BLK:]
                    dvd[rows, :] += dv_b[ATTN_BLK:]
                else:
                    dkd[rows, :] += dk_b
                    dvd[rows, :] += dv_b
                return carry

            lax.fori_loop(0, n_t, blk, 0, unroll=ATTN_UNROLL)
            for acc, part in ((dqa, dqd), (dka, dkd), (dva, dvd)):
                if dil == 1:
                    acc[...] = part[...]
                else:
                    for r in range(dil):
                        acc[pl.ds(r, seg, stride=dil), :] += part[r * seg:(r + 1) * seg, :]
        out_ref[0] = dqa[...].astype(out_ref.dtype)
        out_ref[1] = dka[...].astype(out_ref.dtype)
        out_ref[2] = dva[...].astype(out_ref.dtype)

    def col(slot):
        return pl.BlockSpec((None, s, HEAD_DIM), lambda h: (slot, 0, h))

    head = pl.BlockSpec((s, HEAD_DIM), lambda h: (0, h))
    return pl.pallas_call(
        body, name="attn_bwd", grid=(heads,),
        in_specs=[col(0), col(1), col(2), pl.BlockSpec((None, 1, HEAD_DIM), lambda h: (h, 0, 0)),
                  head, head, head],
        out_specs=pl.BlockSpec((N_QKV, s, HEAD_DIM), lambda h: (0, 0, h)),
        out_shape=jax.ShapeDtypeStruct((N_QKV, s, d), BF16),
        scratch_shapes=[pltpu.VMEM((s, HEAD_DIM), BF16)] * 4 + [pltpu.VMEM((s, HEAD_DIM), F32)] * 9,
        compiler_params=_params(48),
    )(proj_a, proj_a, proj_a, slopes, y_attn, lse, dy)


def _expm1(x):
    small = x * (1.0 + x * (0.5 + x * (1.0 / 6.0 + x * (1.0 / 24.0 + x * (1.0 / 120.0)))))
    return jnp.where(jnp.abs(x) < 0.1, small, jnp.exp(x) - 1.0)


def _softplus(x):
    return jnp.maximum(x, 0.0) + jnp.log1p(jnp.exp(-jnp.abs(x)))


GELU_K = 0.7978845608028654
GELU_C = 0.044715


def _gelu(x):
    t = jnp.tanh(GELU_K * (x + GELU_C * x * x * x))
    return 0.5 * x * (1.0 + t), t


def _gelu_grad(x, t):
    return 0.5 * (1.0 + t) + 0.5 * x * (1.0 - t * t) * GELU_K * (1.0 + 3.0 * GELU_C * x * x)


def _lru_gates(xc, wa, ba, wx, bx, sp):
    r = _sigmoid(_dot(xc, wa, NN) + ba)
    ig = _sigmoid(_dot(xc, wx, NN) + bx)
    log_a = -LRU_C * r * sp
    a = jnp.exp(log_a)
    mult = jnp.sqrt(-_expm1(2.0 * log_a))
    return r, ig, a, mult


def _scan_fwd(a, u, tt):
    row = lax.broadcasted_iota(jnp.int32, a.shape, 0)
    sh = 1
    while sh < tt:
        keep = row >= sh
        a_s = jnp.where(keep, pltpu.roll(a, sh, 0), 1.0)
        u_s = jnp.where(keep, pltpu.roll(u, sh, 0), 0.0)
        u = a * u_s + u
        a = a * a_s
        sh *= 2
    return a, u


def _scan_bwd(b, g, tt):
    row = lax.broadcasted_iota(jnp.int32, b.shape, 0)
    sh = 1
    while sh < tt:
        keep = row < tt - sh
        b_s = jnp.where(keep, pltpu.roll(b, tt - sh, 0), 1.0)
        g_s = jnp.where(keep, pltpu.roll(g, tt - sh, 0), 0.0)
        g = g + b * g_s
        b = b * b_s
        sh *= 2
    return b, g


def _conv_rows(xpad_ref, cw, cb, s):
    acc = cb
    for j in range(CONV_TAPS):
        off = 8 - (CONV_TAPS - 1) + j
        acc = acc + cw[j:j + 1, :] * xpad_ref[off:off + s, :]
    return acc


LRU_TILE = 128


def _lru_specs(s, d):
    heads = d // HEAD_DIM

    def col(slot):
        return pl.BlockSpec((None, s, HEAD_DIM), lambda h: (slot, 0, h))

    vec = pl.BlockSpec((1, HEAD_DIM), lambda h: (0, h))
    mat = pl.BlockSpec((None, HEAD_DIM, HEAD_DIM), lambda h: (h, 0, 0))
    cw = pl.BlockSpec((8, HEAD_DIM), lambda h: (0, h))
    head = pl.BlockSpec((s, HEAD_DIM), lambda h: (0, h))
    return heads, col, vec, mat, cw, head


def _lru_fwd(proj_b, conv_w, conv_b, wa, ba, wx, bx, lam):
    _, s, d = proj_b.shape
    heads, col, vec, mat, cws, head = _lru_specs(s, d)
    tt = LRU_TILE

    def body(xr_ref, xg_ref, cw_ref, cb_ref, wa_ref, ba_ref, wx_ref, bx_ref, lam_ref, y_ref, h_ref, xpad, xc_s):
        xpad[0:8, :] = jnp.zeros((8, HEAD_DIM), F32)
        xpad[8:8 + s, :] = xr_ref[...]
        xc_s[...] = _conv_rows(xpad, cw_ref[...], cb_ref[...], s)
        sp = _softplus(-lam_ref[...])
        wav, wxv, bav, bxv = wa_ref[...], wx_ref[...], ba_ref[...], bx_ref[...]

        def tile(i, hc):
            rows = pl.ds(pl.multiple_of(i * tt, tt), tt)
            xc = xc_s[rows, :]
            _, ig, a, mult = _lru_gates(xc, wav, bav, wxv, bxv, sp)
            pa, hl = _scan_fwd(a, mult * (ig * xc), tt)
            h = hl + pa * hc
            h_ref[rows, :] = h
            gel, _ = _gelu(xg_ref[rows, :])
            y_ref[rows, :] = (h * gel).astype(y_ref.dtype)
            return h[tt - 1:tt, :]

        lax.fori_loop(0, s // tt, tile, jnp.zeros((1, HEAD_DIM), F32))

    return pl.pallas_call(
        body, name="lru_fwd", grid=(heads,),
        in_specs=[col(0), col(1), cws, vec, mat, vec, mat, vec, vec],
        out_specs=[head, head],
        out_shape=[jax.ShapeDtypeStruct((s, d), BF16), jax.ShapeDtypeStruct((s, d), F32)],
        scratch_shapes=[pltpu.VMEM((s + 8, HEAD_DIM), F32), pltpu.VMEM((s, HEAD_DIM), F32)],
        compiler_params=_params(32),
    )(proj_b, proj_b, conv_w, conv_b, wa, ba, wx, bx, lam)


def _lru_bwd(proj_b, h_lru, dy, conv_w, conv_b, wa, ba, wx, bx, lam, dproj_b):
    _, s, d = proj_b.shape
    heads, col, vec, mat, cws, head = _lru_specs(s, d)
    tt = LRU_TILE
    n_t = s // tt

    def body(xr_ref, xg_ref, h_ref, dy_ref, cw_ref, cb_ref, wa_ref, ba_ref, wx_ref, bx_ref, lam_ref, alias_ref,
             out_ref, dcw_ref, dcb_ref, dwa_ref, dba_ref, dwx_ref, dbx_ref, dlam_ref, xpad, xc_s, dxc_s):
        del alias_ref
        xpad[0:8, :] = jnp.zeros((8, HEAD_DIM), F32)
        xpad[8:8 + s, :] = xr_ref[...]
        cwv = cw_ref[...]
        xc_s[...] = _conv_rows(xpad, cwv, cb_ref[...], s)
        dxc_s[s:s + 8, :] = jnp.zeros((8, HEAD_DIM), F32)
        lamv = lam_ref[...]
        sp = _softplus(-lamv)
        wav, wxv, bav, bxv = wa_ref[...], wx_ref[...], ba_ref[...], bx_ref[...]
        dwa_ref[...] = jnp.zeros_like(dwa_ref)
        dwx_ref[...] = jnp.zeros_like(dwx_ref)
        zero = jnp.zeros((1, HEAD_DIM), F32)
        row = lax.broadcasted_iota(jnp.int32, (tt, HEAD_DIM), 0)

        def tile(it, carry):
            dh_next, a_next, dba, dbx, dsp, dcb = carry
            i = n_t - 1 - it
            t0 = pl.multiple_of(i * tt, tt)
            rows = pl.ds(t0, tt)
            xc = xc_s[rows, :]
            r, ig, a, mult = _lru_gates(xc, wav, bav, wxv, bxv, sp)
            h = h_ref[rows, :]
            before = h_ref[pl.ds(pl.multiple_of(jnp.maximum(t0 - 8, 0), 8), 8), :][7:8, :]
            before = before * (i > 0).astype(F32)
            h_prev = jnp.where(row == 0, before, pltpu.roll(h, 1, 0))
            xg = xg_ref[rows, :]
            dyv = dy_ref[rows, :]
            gel, th = _gelu(xg)
            out_ref[1, rows, :] = (dyv * h * _gelu_grad(xg, th)).astype(out_ref.dtype)
            b = jnp.where(row == tt - 1, a_next, pltpu.roll(a, tt - 1, 0))
            pb, z = _scan_bwd(b, dyv * gel, tt)
            dh = z + pb * dh_next
            da = dh * h_prev
            dmult = dh * (ig * xc)
            dig = dh * (mult * xc)
            dla = da * a - dmult * (a * a / mult)
            dzr = dla * (-LRU_C * sp) * (r * (1.0 - r))
            dzx = dig * (ig * (1.0 - ig))
            dxc = dh * (mult * ig) + _dot(dzr, wav, NT) + _dot(dzx, wxv, NT)
            dxc_s[rows, :] = dxc
            dwa_ref[...] += _dot(xc, dzr, TN)
            dwx_ref[...] += _dot(xc, dzx, TN)
            return (dh[0:1, :], a[0:1, :],
                    dba + jnp.sum(dzr, axis=0, keepdims=True),
                    dbx + jnp.sum(dzx, axis=0, keepdims=True),
                    dsp + jnp.sum(dla * (-LRU_C * r), axis=0, keepdims=True),
                    dcb + jnp.sum(dxc, axis=0, keepdims=True))

        _, _, dba, dbx, dsp, dcb = lax.fori_loop(0, n_t, tile, (zero, zero, zero, zero, zero, zero))
        dba_ref[...] = dba
        dbx_ref[...] = dbx
        dcb_ref[...] = dcb
        dlam_ref[...] = -dsp * _sigmoid(-lamv)
        dxr = jnp.zeros((s, HEAD_DIM), F32)
        for j in range(CONV_TAPS):
            back = CONV_TAPS - 1 - j
            off = 8 - back
            dcw_ref[j:j + 1, :] = jnp.sum(dxc_s[0:s, :] * xpad[off:off + s, :], axis=0, keepdims=True)
            dxr = dxr + cwv[j:j + 1, :] * dxc_s[back:back + s, :]
        out_ref[0] = dxr.astype(out_ref.dtype)

    return pl.pallas_call(
        body, name="lru_bwd", grid=(heads,),
        in_specs=[col(0), col(1), head, head, cws, vec, mat, vec, mat, vec, vec, pl.BlockSpec(memory_space=pl.ANY)],
        out_specs=[pl.BlockSpec((2, s, HEAD_DIM), lambda h: (0, 0, h)),
                   pl.BlockSpec((CONV_TAPS, HEAD_DIM), lambda h: (0, h)), vec, mat, vec, mat, vec, vec],
        out_shape=[jax.ShapeDtypeStruct(dproj_b.shape, dproj_b.dtype),
                   jax.ShapeDtypeStruct((CONV_TAPS, d), F32), jax.ShapeDtypeStruct((1, d), F32),
                   jax.ShapeDtypeStruct(wa.shape, F32), jax.ShapeDtypeStruct((1, d), F32),
                   jax.ShapeDtypeStruct(wx.shape, F32), jax.ShapeDtypeStruct((1, d), F32),
                   jax.ShapeDtypeStruct((1, d), F32)],
        scratch_shapes=[pltpu.VMEM((s + 8, HEAD_DIM), F32), pltpu.VMEM((s, HEAD_DIM), F32),
                        pltpu.VMEM((s + 8, HEAD_DIM), F32)],
        input_output_aliases={11: 0},
        compiler_params=_params(32),
    )(proj_b, proj_b, h_lru, dy, conv_w, conv_b, wa, ba, wx, bx, lam, dproj_b)


def _place():
    x, y, c = (lax.axis_index(n) for n in AXES)
    return x, y, c


def _other_chips(x, y):
    return [(1 - x, y), (x, 1 - y), (1 - x, 1 - y)]


HBM = pl.BlockSpec(memory_space=pl.ANY)


def _cast_shard(w, chip_arr, name):
    r, cols = w.shape
    rh = r // 2
    tr = _row_tile(rh, cols * 4)
    nt = rh // tr

    def body(chip_ref, w_ref, o_ref):
        del chip_ref
        o_ref[...] = w_ref[...].astype(BF16)

    return pl.pallas_call(
        body, name=name,
        grid_spec=pltpu.PrefetchScalarGridSpec(
            num_scalar_prefetch=1, grid=(2, nt),
            in_specs=[pl.BlockSpec((tr, cols), lambda h, i, chip_ref: (h * nt + i, 0))],
            out_specs=pl.BlockSpec((None, None, tr, cols), lambda h, i, chip_ref: (chip_ref[0], h, i, 0))),
        out_shape=jax.ShapeDtypeStruct((N_CHIPS, 2, rh, cols), BF16), compiler_params=_params(32),
    )(chip_arr, w)


def _gather_weights(bufs):
    n = len(bufs)

    def body(*refs):
        outs = refs[n:2 * n]
        send_sems, recv_sems = refs[2 * n:]
        x, y, c = _place()
        me = 2 * x + y
        chips = _other_chips(x, y)
        sibling = (x, y, 1 - c)

        def copy(w, k, src, dst, to):
            return pltpu.make_async_remote_copy(src_ref=src, dst_ref=dst, send_sem=send_sems.at[6 * w + k],
                                                recv_sem=recv_sems.at[6 * w + k], device_id=to, device_id_type=MESH)

        sends = []
        for w in range(n):
            for j, (cx, cy) in enumerate(chips):
                cp = copy(w, j, outs[w].at[me, c], outs[w].at[me, c], (cx, cy, c))
                cp.start()
                sends.append(cp)
        for j, (cx, cy) in enumerate(chips):
            cj = 2 * cx + cy
            for w in range(n):
                landed = outs[w].at[cj, c]
                copy(w, j, landed, landed, (cx, cy, c)).wait_recv()
                fwd = copy(w, 3 + j, landed, landed, sibling)
                fwd.start()
                sends.append(fwd)
        for j, (cx, cy) in enumerate(chips):
            cj = 2 * cx + cy
            for w in range(n):
                theirs = outs[w].at[cj, 1 - c]
                copy(w, 3 + j, theirs, theirs, sibling).wait_recv()
        for cp in sends:
            cp.wait_send()

    return pl.pallas_call(
        body, name="gather_weights", in_specs=[HBM] * n, out_specs=[HBM] * n,
        out_shape=[jax.ShapeDtypeStruct(a.shape, a.dtype) for a in bufs],
        input_output_aliases={w: w for w in range(n)},
        scratch_shapes=[pltpu.SemaphoreType.DMA((6 * n,)), pltpu.SemaphoreType.DMA((6 * n,))],
    )(*bufs)


def _pair_exchange(grads):
    n = len(grads)

    def body(*refs):
        ins, outs = refs[:n], refs[n:2 * n]
        send_sems, recv_sems = refs[2 * n:]
        x, y, c = _place()
        sibling = (x, y, 1 - c)
        cps = []
        for w in range(n):
            for j in range(N_CHIPS):
                cp = pltpu.make_async_remote_copy(
                    src_ref=ins[w].at[j, 1 - c], dst_ref=outs[w].at[j], send_sem=send_sems.at[N_CHIPS * w + j],
                    recv_sem=recv_sems.at[N_CHIPS * w + j], device_id=sibling, device_id_type=MESH)
                cp.start()
                cps.append(cp)
        for cp in cps:
            cp.wait()

    return pl.pallas_call(
        body, name="reduce_pair_exchange", in_specs=[HBM] * n, out_specs=[HBM] * n,
        out_shape=[jax.ShapeDtypeStruct((N_CHIPS,) + a.shape[2:], a.dtype) for a in grads],
        scratch_shapes=[pltpu.SemaphoreType.DMA((N_CHIPS * n,)), pltpu.SemaphoreType.DMA((N_CHIPS * n,))],
    )(*grads)


def _chip_exchange(sums):
    n = len(sums)

    def body(*refs):
        ins, outs = refs[:n], refs[n:2 * n]
        send_sems, recv_sems = refs[2 * n:]
        x, y, c = _place()
        me = 2 * x + y
        chips = _other_chips(x, y)
        cps = []
        for w in range(n):
            for j, (cx, cy) in enumerate(chips):
                cp = pltpu.make_async_remote_copy(
                    src_ref=ins[w].at[2 * cx + cy], dst_ref=outs[w].at[me], send_sem=send_sems.at[3 * w + j],
                    recv_sem=recv_sems.at[3 * w + j], device_id=(cx, cy, c), device_id_type=MESH)
                cp.start()
                cps.append(cp)
        for w in range(n):
            for j, (cx, cy) in enumerate(chips):
                theirs = outs[w].at[2 * cx + cy]
                pltpu.make_async_remote_copy(
                    src_ref=theirs, dst_ref=theirs, send_sem=send_sems.at[3 * w + j],
                    recv_sem=recv_sems.at[3 * w + j], device_id=(cx, cy, c), device_id_type=MESH).wait_recv()
        for cp in cps:
            cp.wait_send()

    return pl.pallas_call(
        body, name="reduce_chip_exchange", in_specs=[HBM] * n, out_specs=[HBM] * n,
        out_shape=[jax.ShapeDtypeStruct(a.shape, a.dtype) for a in sums],
        scratch_shapes=[pltpu.SemaphoreType.DMA((3 * n,)), pltpu.SemaphoreType.DMA((3 * n,))],
    )(*sums)


def _half_exchange(bufs):
    n = len(bufs)

    def body(*refs):
        outs = refs[n:2 * n]
        send_sems, recv_sems = refs[2 * n:]
        x, y, c = _place()
        sibling = (x, y, 1 - c)
        cps = []
        for w in range(n):
            rc = pltpu.make_async_remote_copy(
                src_ref=outs[w].at[c], dst_ref=outs[w].at[c], send_sem=send_sems.at[w], recv_sem=recv_sems.at[w],
                device_id=sibling, device_id_type=MESH)
            rc.start()
            cps.append(rc)
        for w in range(n):
            theirs = outs[w].at[1 - c]
            pltpu.make_async_remote_copy(
                src_ref=theirs, dst_ref=theirs, send_sem=send_sems.at[w], recv_sem=recv_sems.at[w],
                device_id=sibling, device_id_type=MESH).wait_recv()
        for cp in cps:
            cp.wait_send()

    return pl.pallas_call(
        body, name="reduce_half_exchange", in_specs=[HBM] * n, out_specs=[HBM] * n,
        out_shape=[jax.ShapeDtypeStruct(a.shape, a.dtype) for a in bufs],
        input_output_aliases={w: w for w in range(n)},
        scratch_shapes=[pltpu.SemaphoreType.DMA((n,)), pltpu.SemaphoreType.DMA((n,))],
    )(*bufs)


def _all_gather8(block, name, own_row):
    def body(in_ref, out_ref, send_sems, recv_sems, local_sem):
        x, y, c = _place()
        me = 4 * x + 2 * y + c
        if own_row:
            mine = pltpu.make_async_copy(in_ref, out_ref.at[me], local_sem)
            mine.start()
        flips = [(fx, fy, fc) for fx in (0, 1) for fy in (0, 1) for fc in (0, 1)][1:]
        cps = []
        for k, (fx, fy, fc) in enumerate(flips):
            cp = pltpu.make_async_remote_copy(
                src_ref=in_ref, dst_ref=out_ref.at[me], send_sem=send_sems.at[k], recv_sem=recv_sems.at[k],
                device_id=(x ^ fx, y ^ fy, c ^ fc), device_id_type=MESH)
            cp.start()
            cps.append(cp)
        for k, (fx, fy, fc) in enumerate(flips):
            px, py, pc = x ^ fx, y ^ fy, c ^ fc
            theirs = out_ref.at[4 * px + 2 * py + pc]
            pltpu.make_async_remote_copy(
                src_ref=theirs, dst_ref=theirs, send_sem=send_sems.at[k], recv_sem=recv_sems.at[k],
                device_id=(px, py, pc), device_id_type=MESH).wait_recv()
        for cp in cps:
            cp.wait_send()
        if own_row:
            mine.wait()

    return pl.pallas_call(
        body, name=name, in_specs=[HBM], out_specs=HBM,
        out_shape=jax.ShapeDtypeStruct((N_DEV,) + block.shape, block.dtype),
        scratch_shapes=[pltpu.SemaphoreType.DMA((N_DEV - 1,)), pltpu.SemaphoreType.DMA((N_DEV - 1,)),
                        pltpu.SemaphoreType.DMA],
    )(block)


def _pair_sum(grad, recv, c_arr, name):
    _, _, rh, cols = grad.shape
    tr = _row_tile(rh, cols * 4)

    def body(c_ref, g_ref, r_ref, o_ref):
        del c_ref
        o_ref[...] = (g_ref[...].astype(F32) + r_ref[...].astype(F32)).astype(o_ref.dtype)

    spec = pl.BlockSpec((None, tr, cols), lambda j, i, c_ref: (j, i, 0))
    return pl.pallas_call(
        body, name=name,
        grid_spec=pltpu.PrefetchScalarGridSpec(
            num_scalar_prefetch=1, grid=(N_CHIPS, rh // tr),
            in_specs=[pl.BlockSpec((None, None, tr, cols), lambda j, i, c_ref: (j, c_ref[0], i, 0)), spec],
            out_specs=spec),
        out_shape=jax.ShapeDtypeStruct(recv.shape, BF16), compiler_params=_params(32),
    )(c_arr, grad, recv)


def _chip_sum(parts, own, place_arr, name):
    _, rh, cols = parts.shape
    tr = _row_tile(rh, cols * 4)

    def body(place_ref, p_ref, own_ref, o_ref):
        chip = place_ref[0]
        o_ref[...] = jnp.zeros_like(o_ref)
        for k in range(N_CHIPS):
            @pl.when(chip == k)
            def _():
                o_ref[...] += own_ref[...].astype(F32)

            @pl.when(chip != k)
            def _(k=k):
                o_ref[...] += p_ref[k].astype(F32)

    return pl.pallas_call(
        body, name=name,
        grid_spec=pltpu.PrefetchScalarGridSpec(
            num_scalar_prefetch=1, grid=(rh // tr,),
            in_specs=[pl.BlockSpec((N_CHIPS, tr, cols), lambda i, place_ref: (0, i, 0)),
                      pl.BlockSpec((None, tr, cols), lambda i, place_ref: (place_ref[0], i, 0))],
            out_specs=pl.BlockSpec((None, tr, cols), lambda i, place_ref: (place_ref[1], i, 0))),
        out_shape=jax.ShapeDtypeStruct((2, rh, cols), F32), compiler_params=_params(32),
    )(place_arr, parts, own)


def _adamw_math(w, g, m, v):
    m = ADAM_B1 * m + (1.0 - ADAM_B1) * g
    v = ADAM_B2 * v + (1.0 - ADAM_B2) * (g * g)
    m_hat = m / (1.0 - ADAM_B1 ** ADAM_STEP)
    v_hat = v / (1.0 - ADAM_B2 ** ADAM_STEP)
    delta = -ADAM_LR * (m_hat / (jnp.sqrt(v_hat) + ADAM_EPS) + ADAM_WD * w)
    return delta, m, v


def _adamw(w, g, m, v, name):
    rows, cols = w.shape
    tr = _row_tile(rows, cols * 4)

    def body(w_ref, g_ref, m_ref, v_ref, d_ref, nm_ref, nv_ref):
        d_ref[...], nm_ref[...], nv_ref[...] = _adamw_math(w_ref[...], g_ref[...], m_ref[...], v_ref[...])

    spec = pl.BlockSpec((tr, cols), lambda i: (i, 0))
    return pl.pallas_call(
        body, name=name, grid=(rows // tr,), in_specs=[spec] * 4, out_specs=[spec] * 3,
        out_shape=[jax.ShapeDtypeStruct(w.shape, F32)] * 3, compiler_params=_params(32),
    )(w, g, m, v)


def _sum8_adamw(parts, own, dev_arr, w, m, v):
    _, rows, cols = parts.shape
    tr = _row_tile(rows, cols * 4, VMEM_MIB // 4)

    def body(dev_ref, p_ref, own_ref, w_ref, m_ref, v_ref, g_ref, d_ref, nm_ref, nv_ref):
        dev = dev_ref[0]
        g_ref[...] = jnp.zeros_like(g_ref)
        for k in range(N_DEV):
            @pl.when(dev == k)
            def _():
                g_ref[...] += own_ref[...]

            @pl.when(dev != k)
            def _(k=k):
                g_ref[...] += p_ref[k]

        d_ref[...], nm_ref[...], nv_ref[...] = _adamw_math(w_ref[...], g_ref[...], m_ref[...], v_ref[...])

    spec = pl.BlockSpec((tr, cols), lambda i, dev_ref: (i, 0))
    return pl.pallas_call(
        body, name="small_sum_adamw",
        grid_spec=pltpu.PrefetchScalarGridSpec(
            num_scalar_prefetch=1, grid=(rows // tr,),
            in_specs=[pl.BlockSpec((N_DEV, tr, cols), lambda i, dev_ref: (0, i, 0))] + [spec] * 4,
            out_specs=[spec] * 4),
        out_shape=[jax.ShapeDtypeStruct((rows, cols), F32)] * 4, compiler_params=_params(32),
    )(dev_arr, parts, own, w, m, v)


def _pack_rows(pieces, rows, name):
    cols = pieces[0].shape[1]
    n = len(pieces)

    def body(*refs):
        o_ref = refs[n]
        o_ref[...] = jnp.zeros_like(o_ref)
        at = 0
        for p_ref in refs[:n]:
            r = p_ref.shape[0]
            o_ref[at:at + r, :] = p_ref[...]
            at += r

    return pl.pallas_call(
        body, name=name, out_shape=jax.ShapeDtypeStruct((rows, cols), F32), compiler_params=_params(32),
    )(*pieces)


def kernel(x, norm_mix_g, w_in, conv_w, conv_b, lru_wa, lru_ba, lru_wx, lru_bx, lru_lambda, w_proj_attn, w_proj_lru, w_out, norm_mlp_g, w_up, w_down, norm_final_g, loss_target, m_norm_mix_g, m_w_in, m_conv_w, m_conv_b, m_lru_wa, m_lru_ba, m_lru_wx, m_lru_bx, m_lru_lambda, m_w_proj_attn, m_w_proj_lru, m_w_out, m_norm_mlp_g, m_w_up, m_w_down, m_norm_final_g, v_norm_mix_g, v_w_in, v_conv_w, v_conv_b, v_lru_wa, v_lru_ba, v_lru_wx, v_lru_bx, v_lru_lambda, v_w_proj_attn, v_w_proj_lru, v_w_out, v_norm_mlp_g, v_w_up, v_w_down, v_norm_final_g):
    s, d = x.shape[1], x.shape[2]
    ff = w_up.shape[2] * N_CHIPS
    heads = d // HEAD_DIM
    u = d // 4
    assert s % (max(DILATIONS) * ATTN_BLK) == 0 and d % (4 * HEAD_DIM) == 0 and ff == 4 * d
    xs = x[0]
    target = loss_target[0]
    gf = norm_final_g.reshape(1, d)
    wa, wx = lru_wa[0], lru_wx[0]
    core = lax.axis_index("c").astype(jnp.int32)
    chip = (2 * lax.axis_index("x") + lax.axis_index("y")).astype(jnp.int32)
    cidx = core.reshape(1)
    chip_arr = chip.reshape(1)
    place_arr = jnp.stack([chip, core])
    dev_arr = (2 * chip + core).reshape(1)
    slopes = jnp.broadcast_to(
        (2.0 ** (-8.0 * jnp.arange(1, heads + 1, dtype=F32) / heads))[:, None, None], (heads, 1, HEAD_DIM))

    big = [w_in[0], w_proj_attn[0], w_proj_lru[0], w_out[0], w_up[0], w_down[0]]
    names = ["w_in", "w_proj_attn", "w_proj_lru", "w_out", "w_up", "w_down"]
    gathered = _gather_weights([_cast_shard(w, chip_arr, "cast_" + nm) for w, nm in zip(big, names)])
    w_in_g = gathered[0].reshape(N_CHIPS, d, N_SLOTS * u)
    wpa_g = gathered[1].reshape(d, d)
    wpl_g = gathered[2].reshape(d, d)
    wout_g = gathered[3].reshape(d, d)
    wup_g = gathered[4].reshape(N_CHIPS, d, d)
    wdown_g = gathered[5].reshape(ff, d)
    cw_pad = jnp.pad(conv_w[0], ((0, 8 - CONV_TAPS), (0, 0)))
    cw_all = _all_gather8(cw_pad, "gather_conv_w", True)
    conv_w_full = jnp.concatenate([cw_all[2 * j] for j in range(N_CHIPS)], axis=1)

    xn = _rms_fwd(xs, norm_mix_g, "norm_mix")
    proj_a = _proj_in(xn, w_in_g, 0, 4 * N_QKV, N_QKV)
    proj_b = _proj_in(xn, w_in_g, 4 * N_QKV, 4 * (N_SLOTS - N_QKV), N_SLOTS - N_QKV)
    y_attn, lse = _attn_fwd(proj_a, slopes)
    y_lru, h_lru = _lru_fwd(proj_b, conv_w_full, conv_b, wa, lru_ba, wx, lru_bx, lru_lambda)

    tn = u
    sd_f32 = jax.ShapeDtypeStruct((s, d), F32)
    sd_bf16 = jax.ShapeDtypeStruct((s, d), BF16)
    col = pl.BlockSpec((s, tn), lambda i, j, k: (0, j))

    def slot(n):
        return pl.BlockSpec((None, s, tn), lambda i, j, k: (n, 0, j))

    p_attn = _mm_nn("proj_attn", y_attn, wpa_g, [], [], [sd_f32], [col], _store, tn)[0]

    def merge(acc, extras, outs):
        pa_ref, ga_ref, gl_ref = extras
        merged = _sigmoid(ga_ref[...]) * pa_ref[...] + _sigmoid(gl_ref[...]) * acc
        outs[0][...] = merged.astype(BF16)
        outs[1][...] = acc

    tn2 = max(HEAD_DIM, u // 2)
    col2 = pl.BlockSpec((s, tn2), lambda i, j, k: (0, j))

    def slot2(n):
        return pl.BlockSpec((None, s, tn2), lambda i, j, k: (n, 0, j))

    merged, p_lru = _mm_nn("proj_lru_merge", y_lru, wpl_g, [p_attn, proj_b, proj_b], [col2, slot2(2), slot2(3)],
                           [sd_bf16, sd_f32], [col2, col2], merge, tn2)

    def add_resid(acc, extras, outs):
        outs[0][...] = extras[0][...] + acc

    h1 = _mm_nn("w_out_resid", merged, wout_g, [xs], [col], [sd_f32], [col], add_resid, tn)[0]
    hn = _rms_fwd(h1, norm_mlp_g, "norm_mlp")

    def relu_sq(acc, extras, outs):
        r = jnp.maximum(acc, 0.0)
        outs[0][...] = (r * r).astype(BF16)
        outs[1][...] = r.astype(BF16)

    sf_bf16 = jax.ShapeDtypeStruct((s, ff), BF16)
    hid, relu_up = _mm(
        "w_up_relu2", [hn, wup_g],
        [pl.BlockSpec((s, d), lambda i, j, k: (0, 0)),
         pl.BlockSpec((None, d, tn), lambda i, j, k: (j // 4, 0, j % 4))],
        [sf_bf16, sf_bf16], [col, col], (1, ff // tn, 1), NN, relu_sq)
    h2 = _mm(
        "w_down_resid", [hid, wdown_g, h1],
        [pl.BlockSpec((s, d), lambda i, j, k: (0, k)), pl.BlockSpec((d, tn), lambda i, j, k: (k, j)), col],
        [sd_f32], [col], (1, d // tn, ff // d), NN, add_resid, nk=ff // d, acc_shape=(s, tn))[0]
    loss_part, dh2, dh2_b, d_gf = _loss_head(h2, gf, target)
    loss = lax.psum(loss_part[0, 0], AXES)

    def relu_sq_bwd(acc, extras, outs):
        outs[0][...] = (acc * (2.0 * extras[0][...].astype(F32))).astype(BF16)

    dup = _mm_nt("d_hid", dh2_b, wdown_g, [relu_up], [col], [sf_bf16], [col], relu_sq_bwd, tn)[0]
    tok_d = pl.BlockSpec((s, d), lambda i, j: (0, 0))
    g_wdown = _mm_tn(
        "g_w_down", hid, dh2_b, pl.BlockSpec((s, d), lambda i, j: (0, i)),
        pl.BlockSpec((s, tn), lambda i, j: (0, j)), jax.ShapeDtypeStruct((ff, d), BF16),
        pl.BlockSpec((d, tn), lambda i, j: (i, j)), (ff // d, d // tn), d, tn, s)
    dhn = _mm(
        "d_hn", [dup, wup_g],
        [pl.BlockSpec((s, d), lambda i, j, k: (0, k)), pl.BlockSpec((None, tn, d), lambda i, j, k: (k, j, 0))],
        [sd_f32], [col], (1, d // tn, ff // d), NT, _store, nk=ff // d, acc_shape=(s, tn))[0]
    g_wup = _mm_tn(
        "g_w_up", hn, dup, tok_d, pl.BlockSpec((s, tn), lambda i, j: (0, j)),
        jax.ShapeDtypeStruct((N_CHIPS, d, d), BF16), pl.BlockSpec((None, d, tn), lambda i, j: (j // 4, 0, j % 4)),
        (1, ff // tn), d, tn, s)
    dh1, dh1_b, d_gmlp = _rms_bwd(h1, norm_mlp_g, dhn, dh2, "norm_mlp_bwd")

    g_wout = _mm_tn(
        "g_w_out", merged, dh1_b, tok_d, pl.BlockSpec((s, tn), lambda i, j: (0, j)),
        jax.ShapeDtypeStruct((d, d), BF16), pl.BlockSpec((d, tn), lambda i, j: (0, j)), (1, d // tn), d, tn, s)

    def merge_bwd(acc, extras, outs):
        pa_ref, pl_ref, ga_ref, gl_ref = extras
        sa, sl = _sigmoid(ga_ref[...]), _sigmoid(gl_ref[...])
        outs[0][...] = (acc * sa).astype(BF16)
        outs[1][...] = (acc * sl).astype(BF16)
        outs[2][0] = (acc * pa_ref[...] * (sa * (1.0 - sa))).astype(BF16)
        outs[2][1] = (acc * pl_ref[...] * (sl * (1.0 - sl))).astype(BF16)

    nb = N_SLOTS - N_QKV
    d_pa, d_pl, dproj_b = _mm_nt(
        "d_merged", dh1_b, wout_g, [p_attn, p_lru, proj_b, proj_b], [col2, col2, slot2(2), slot2(3)],
        [sd_bf16, sd_bf16, jax.ShapeDtypeStruct((nb, s, d), BF16)],
        [col2, col2, pl.BlockSpec((2, s, tn2), lambda i, j, k: (1, 0, j))], merge_bwd, tn2)
    dy_attn = _mm_nt("d_y_attn", d_pa, wpa_g, [], [], [sd_f32], [col], _store, tn)[0]
    dy_lru = _mm_nt("d_y_lru", d_pl, wpl_g, [], [], [sd_f32], [col], _store, tn)[0]
    g_wpa = _mm_tn(
        "g_w_proj_attn", y_attn, d_pa, tok_d, pl.BlockSpec((s, tn), lambda i, j: (0, j)),
        jax.ShapeDtypeStruct((d, d), BF16), pl.BlockSpec((d, tn), lambda i, j: (0, j)), (1, d // tn), d, tn, s)
    g_wpl = _mm_tn(
        "g_w_proj_lru", y_lru, d_pl, tok_d, pl.BlockSpec((s, tn), lambda i, j: (0, j)),
        jax.ShapeDtypeStruct((d, d), BF16), pl.BlockSpec((d, tn), lambda i, j: (0, j)), (1, d // tn), d, tn, s)

    dproj_a = _attn_bwd(proj_a, slopes, y_attn, lse, dy_attn)
    dproj_b, d_cw, d_cb, d_wa, d_ba, d_wx, d_bx, d_lam = _lru_bwd(
        proj_b, h_lru, dy_lru, conv_w_full, conv_b, wa, lru_ba, wx, lru_bx, lru_lambda, dproj_b)
    dxn = _dxn(dproj_a, dproj_b, w_in_g, 2 * tn)
    per = N_SLOTS
    g_win_shape = jax.ShapeDtypeStruct((N_CHIPS, d, N_SLOTS * u), BF16)

    def g_win_part(name, dproj, first, prev):
        n_units = 4 * dproj.shape[0]
        return _mm_tn(
            name, xn, dproj, tok_d, pl.BlockSpec((None, s, u), lambda i, j: (j // 4, 0, j % 4)),
            g_win_shape, pl.BlockSpec((None, d, u), lambda i, j: ((j + first) // per, 0, (j + first) % per)),
            (1, n_units), d, u, s, aliases=None if prev is None else {2: 0}, extra=prev)

    g_win = g_win_part("g_w_in_qkv", dproj_a, 0, None)
    g_win = g_win_part("g_w_in_rest", dproj_b, 4 * N_QKV, g_win)
    grad_x, _, d_gmix = _rms_bwd(xs, norm_mix_g, dxn, dh1, "norm_mix_bwd")

    shard_shapes = [w.shape for w in big]
    grads = [g_win, g_wpa, g_wpl, g_wout, g_wup, g_wdown]
    grads4 = [g.reshape(N_CHIPS, 2, r // 2, c) for g, (r, c) in zip(grads, shard_shapes)]
    from_sibling = _pair_exchange(grads4)
    chip_sums = [_pair_sum(g, r, cidx, "pair_sum_" + nm) for g, r, nm in zip(grads4, from_sibling, names)]
    by_chip = _chip_exchange(chip_sums)
    halves = [_chip_sum(p, own, place_arr, "chip_sum_" + nm) for p, own, nm in zip(by_chip, chip_sums, names)]
    full = _half_exchange(halves)
    big_m = [m_w_in[0], m_w_proj_attn[0], m_w_proj_lru[0], m_w_out[0], m_w_up[0], m_w_down[0]]
    big_v = [v_w_in[0], v_w_proj_attn[0], v_w_proj_lru[0], v_w_out[0], v_w_up[0], v_w_down[0]]
    big_out = {}
    for nm, w, g, m, v, shp in zip(names, big, full, big_m, big_v, shard_shapes):
        g = g.reshape(shp)
        dl, nm_, nv_ = _adamw(w, g, m, v, "adamw_" + nm)
        big_out[nm] = tuple(a[None] for a in (g, dl, nm_, nv_))

    mat_rows = heads * HEAD_DIM * HEAD_DIM // d
    vec_names = ["norm_mix_g", "conv_b", "lru_ba", "lru_bx", "lru_lambda", "norm_mlp_g", "norm_final_g"]

    def pack(wa_, wx_, cw_, vecs, name):
        rows = [wa_.reshape(mat_rows, d), wx_.reshape(mat_rows, d), cw_] + [a.reshape(1, d) for a in vecs]
        n = sum(a.shape[0] for a in rows)
        return _pack_rows(rows, n + (-n % 64), name)

    zero_cw = jnp.zeros((CONV_TAPS, d), F32)
    small_g = pack(d_wa, d_wx, d_cw, [d_gmix, d_cb, d_ba, d_bx, d_lam, d_gmlp, d_gf], "pack_small_g")
    small_w = pack(wa, wx, zero_cw, [norm_mix_g, conv_b, lru_ba, lru_bx, lru_lambda, norm_mlp_g, norm_final_g],
                   "pack_small_w")
    small_m = pack(m_lru_wa[0], m_lru_wx[0], zero_cw,
                   [m_norm_mix_g, m_conv_b, m_lru_ba, m_lru_bx, m_lru_lambda, m_norm_mlp_g, m_norm_final_g],
                   "pack_small_m")
    small_v = pack(v_lru_wa[0], v_lru_wx[0], zero_cw,
                   [v_norm_mix_g, v_conv_b, v_lru_ba, v_lru_bx, v_lru_lambda, v_norm_mlp_g, v_norm_final_g],
                   "pack_small_v")
    small_parts = _all_gather8(small_g, "gather_small_grads", False)
    small = _sum8_adamw(small_parts, small_g, dev_arr, small_w, small_m, small_v)
    g_cw = lax.dynamic_slice(small[0][2 * mat_rows:2 * mat_rows + CONV_TAPS], (0, chip * u), (CONV_TAPS, u))
    cw_out = (g_cw,) + tuple(_adamw(conv_w[0], g_cw, m_conv_w[0], v_conv_w[0], "adamw_conv_w"))

    def small_leaf(kind, name):
        a = small[kind]
        if name == "lru_wa":
            return a[0:mat_rows].reshape(lru_wa.shape)
        if name == "lru_wx":
            return a[mat_rows:2 * mat_rows].reshape(lru_wx.shape)
        if name == "conv_w":
            return cw_out[kind][None]
        row = a[2 * mat_rows + CONV_TAPS + vec_names.index(name)]
        return row if name == "norm_final_g" else row[None]

    order = ["norm_mix_g", "w_in", "conv_w", "conv_b", "lru_wa", "lru_ba", "lru_wx", "lru_bx", "lru_lambda",
             "w_proj_attn", "w_proj_lru", "w_out", "norm_mlp_g", "w_up", "w_down", "norm_final_g"]
    outs = [loss, grad_x[None]]
    for kind in range(4):
        for name in order:
            outs.append(big_out[name][kind] if name in big_out else small_leaf(kind, name))
    return tuple(outs)
```

```python
import functools

import jax
import jax.numpy as jnp
from jax import lax
from jax.experimental import pallas as pl
from jax.experimental.pallas import tpu as pltpu

F32 = jnp.float32
BF16 = jnp.bfloat16
MESH = pl.DeviceIdType.MESH
AXES = ("x", "y", "c")

N_CHIPS = 4
N_DEV = 8
HEAD_DIM = 128
ATTN_BLK = 128
DILATIONS = (1, 4, 16)
ATTN_UNROLL = 4
CONV_TAPS = 4
LRU_C = 8.0
EPS = 1e-6
N_SLOTS = 7
N_QKV = 3
VMEM_MIB = 2 ** 20
VMEM_V7X = 64 * VMEM_MIB

ADAM_LR = 0.001
ADAM_B1 = 0.9
ADAM_B2 = 0.999
ADAM_EPS = 1e-08
ADAM_WD = 0.01
ADAM_STEP = 10

NN = (((1,), (0,)), ((), ()))
NT = (((1,), (1,)), ((), ()))
TN = (((0,), (0,)), ((), ()))


def _params(vmem_mib=None, **kw):
    limit = None if vmem_mib is None else min(vmem_mib * VMEM_MIB, VMEM_V7X - 8 * VMEM_MIB)
    return pltpu.CompilerParams(vmem_limit_bytes=limit, **kw)


def _row_tile(rows, row_bytes, budget=VMEM_MIB):
    t = rows
    while t % 16 == 0 and t * row_bytes > budget:
        t //= 2
    return t


def _dot(a, b, dims):
    return lax.dot_general(a.astype(BF16), b.astype(BF16), dims, preferred_element_type=F32)


def _sigmoid(x):
    return jax.nn.sigmoid(x)


def _rms_fwd(x, g, name):
    s, d = x.shape
    tm = _row_tile(s, d * 4)

    def body(x_ref, g_ref, o_ref):
        xf = x_ref[...]
        r = lax.rsqrt(jnp.mean(xf * xf, axis=-1, keepdims=True) + EPS)
        o_ref[...] = (xf * r * g_ref[...]).astype(o_ref.dtype)

    return pl.pallas_call(
        body, name=name, grid=(s // tm,),
        in_specs=[pl.BlockSpec((tm, d), lambda i: (i, 0)), pl.BlockSpec((1, d), lambda i: (0, 0))],
        out_specs=pl.BlockSpec((tm, d), lambda i: (i, 0)),
        out_shape=jax.ShapeDtypeStruct((s, d), BF16), compiler_params=_params(32),
    )(x, g)


def _rms_bwd(x, g, dy, resid, name):
    s, d = x.shape
    tm = _row_tile(s, d * 4)

    def body(x_ref, g_ref, dy_ref, res_ref, dx_ref, dxb_ref, dg_ref):
        xf = x_ref[...]
        r = lax.rsqrt(jnp.mean(xf * xf, axis=-1, keepdims=True) + EPS)
        xh = xf * r
        dyv = dy_ref[...]
        dxh = dyv * g_ref[...]
        dx = r * (dxh - xh * jnp.mean(dxh * xh, axis=-1, keepdims=True)) + res_ref[...]
        dx_ref[...] = dx
        dxb_ref[...] = dx.astype(BF16)
        part = jnp.sum(dyv * xh, axis=0, keepdims=True)

        @pl.when(pl.program_id(0) == 0)
        def _():
            dg_ref[...] = part

        @pl.when(pl.program_id(0) > 0)
        def _():
            dg_ref[...] += part

    row = pl.BlockSpec((tm, d), lambda i: (i, 0))
    vec = pl.BlockSpec((1, d), lambda i: (0, 0))
    return pl.pallas_call(
        body, name=name, grid=(s // tm,),
        in_specs=[row, vec, row, row], out_specs=[row, row, vec],
        out_shape=[jax.ShapeDtypeStruct((s, d), F32), jax.ShapeDtypeStruct((s, d), BF16),
                   jax.ShapeDtypeStruct((1, d), F32)],
        compiler_params=_params(32),
    )(x, g, dy, resid)


def _loss_head(h2, g, target):
    s, d = h2.shape
    tm = _row_tile(s, d * 4)

    def body(x_ref, g_ref, t_ref, loss_ref, dx_ref, dxb_ref, dg_ref):
        xf = x_ref[...]
        gv = g_ref[...]
        r = lax.rsqrt(jnp.mean(xf * xf, axis=-1, keepdims=True) + EPS)
        xh = xf * r
        err = xh * gv - t_ref[...]
        part = jnp.sum(jnp.sum(err * err, axis=1, keepdims=True), axis=0, keepdims=True) * (0.5 / d)
        dyv = err * (1.0 / d)
        dxh = dyv * gv
        dx = r * (dxh - xh * jnp.mean(dxh * xh, axis=-1, keepdims=True))
        dx_ref[...] = dx
        dxb_ref[...] = dx.astype(BF16)
        dgp = jnp.sum(dyv * xh, axis=0, keepdims=True)

        @pl.when(pl.program_id(0) == 0)
        def _():
            dg_ref[...] = dgp
            loss_ref[...] = jnp.broadcast_to(part, loss_ref.shape)

        @pl.when(pl.program_id(0) > 0)
        def _():
            dg_ref[...] += dgp
            loss_ref[...] += jnp.broadcast_to(part, loss_ref.shape)

    row = pl.BlockSpec((tm, d), lambda i: (i, 0))
    vec = pl.BlockSpec((1, d), lambda i: (0, 0))
    return pl.pallas_call(
        body, name="loss_head", grid=(s // tm,),
        in_specs=[row, vec, row],
        out_specs=[pl.BlockSpec((8, 128), lambda i: (0, 0)), row, row, vec],
        out_shape=[jax.ShapeDtypeStruct((8, 128), F32), jax.ShapeDtypeStruct((s, d), F32),
                   jax.ShapeDtypeStruct((s, d), BF16), jax.ShapeDtypeStruct((1, d), F32)],
        compiler_params=_params(32),
    )(h2, g, target)


def _mm(name, operands, in_specs, out_shape, out_specs, grid, dims, epilogue, nk=1, acc_shape=None,
        vmem_mib=56, aliases=None):
    n_in = len(operands)
    n_out = len(out_shape)

    def body(*refs):
        a_ref, b_ref = refs[0], refs[1]
        extras = refs[2:n_in]
        outs = refs[n_in:n_in + n_out]

        def prod():
            return _dot(a_ref[...], b_ref[...], dims)

        if nk == 1:
            epilogue(prod(), extras, outs)
        else:
            acc = refs[n_in + n_out]
            k = pl.program_id(2)

            @pl.when(k == 0)
            def _():
                acc[...] = prod()

            @pl.when(k > 0)
            def _():
                acc[...] += prod()

            @pl.when(k == nk - 1)
            def _():
                epilogue(acc[...], extras, outs)

    scratch = [] if nk == 1 else [pltpu.VMEM(acc_shape, F32)]
    return pl.pallas_call(
        body, name=name, grid=grid, in_specs=in_specs, out_specs=out_specs, out_shape=out_shape,
        scratch_shapes=scratch, input_output_aliases=aliases or {},
        compiler_params=_params(vmem_mib),
    )(*operands)


def _store(acc, extras, outs):
    outs[0][...] = acc.astype(outs[0].dtype)


def _proj_in(xn, w_in_g, first_unit, n_units, n_slots):
    s, d = xn.shape
    u = d // 4
    per = N_SLOTS

    return _mm(
        "proj_in_%d" % first_unit, [xn, w_in_g],
        [pl.BlockSpec((s, d), lambda i, j, k: (0, 0)),
         pl.BlockSpec((None, d, u), lambda i, j, k: ((j + first_unit) // per, 0, (j + first_unit) % per))],
        [jax.ShapeDtypeStruct((n_slots, s, d), F32)],
        [pl.BlockSpec((None, s, u), lambda i, j, k: (j // 4, 0, j % 4))],
        (1, n_units, 1), NN, _store)[0]


def _mm_nn(name, a, b, extras, extra_specs, out_shape, out_specs, epilogue, tn, aliases=None):
    s, kdim = a.shape
    n = b.shape[1]
    return _mm(
        name, [a, b] + list(extras),
        [pl.BlockSpec((s, kdim), lambda i, j, k: (0, 0)), pl.BlockSpec((kdim, tn), lambda i, j, k: (0, j))]
        + list(extra_specs),
        out_shape, out_specs, (1, n // tn, 1), NN, epilogue, aliases=aliases)


def _mm_nt(name, a, b, extras, extra_specs, out_shape, out_specs, epilogue, tn, aliases=None):
    s, kdim = a.shape
    n = b.shape[0]
    return _mm(
        name, [a, b] + list(extras),
        [pl.BlockSpec((s, kdim), lambda i, j, k: (0, 0)), pl.BlockSpec((tn, kdim), lambda i, j, k: (j, 0))]
        + list(extra_specs),
        out_shape, out_specs, (1, n // tn, 1), NT, epilogue, aliases=aliases)


def _mm_tn(name, a, b, a_spec, b_spec, out_shape, out_spec, grid, m, tn, s, aliases=None, extra=None):
    ch = 256
    n_in = 2 if extra is None else 3

    def body(*refs):
        a_ref, b_ref = refs[0], refs[1]
        o_ref, at_ref = refs[n_in], refs[n_in + 1]

        @pl.when(pl.program_id(1) == 0)
        def _():
            for c0 in range(0, s, ch):
                at_ref[:, c0:c0 + ch] = a_ref[c0:c0 + ch, :].astype(F32).T.astype(BF16)

        o_ref[...] = _dot(at_ref[...], b_ref[...], NN).astype(o_ref.dtype)

    operands = [a, b] + ([] if extra is None else [extra])
    in_specs = [a_spec, b_spec] + ([] if extra is None else [pl.BlockSpec(memory_space=pl.ANY)])
    return pl.pallas_call(
        body, name=name, grid=grid, in_specs=in_specs, out_specs=out_spec, out_shape=out_shape,
        scratch_shapes=[pltpu.VMEM((m, s), BF16)], input_output_aliases=aliases or {},
        compiler_params=_params(56),
    )(*operands)


def _dxn(dproj_a, dproj_b, w_in_g, tn):
    n_a, s, d = dproj_a.shape
    u = d // 4
    ua = 4 * n_a
    nk = 4 * N_SLOTS
    per = N_SLOTS

    def body(a_ref, b_ref, w_ref, o_ref):
        k = pl.program_id(2)

        @pl.when(k == 0)
        def _():
            o_ref[...] = jnp.zeros_like(o_ref)

        @pl.when(k < ua)
        def _():
            o_ref[...] += _dot(a_ref[...], w_ref[...], NT)

        @pl.when(k >= ua)
        def _():
            o_ref[...] += _dot(b_ref[...], w_ref[...], NT)

    def a_map(i, j, k):
        kk = jnp.minimum(k, ua - 1)
        return (kk // 4, 0, kk % 4)

    def b_map(i, j, k):
        kk = jnp.maximum(k - ua, 0)
        return (kk // 4, 0, kk % 4)

    return pl.pallas_call(
        body, name="dxn", grid=(1, d // tn, nk),
        in_specs=[pl.BlockSpec((None, s, u), a_map), pl.BlockSpec((None, s, u), b_map),
                  pl.BlockSpec((None, tn, u), lambda i, j, k: (k // per, j, k % per))],
        out_specs=pl.BlockSpec((s, tn), lambda i, j, k: (0, j)),
        out_shape=jax.ShapeDtypeStruct((s, d), F32),
        compiler_params=_params(48),
    )(dproj_a, dproj_b, w_in_g)


def _attn_masks(nblk, slope, dil):
    nkeys = 2 * ATTN_BLK if nblk > 1 else ATTN_BLK
    ii = lax.broadcasted_iota(jnp.int32, (ATTN_BLK, nkeys), 0)
    jj = lax.broadcasted_iota(jnp.int32, (ATTN_BLK, nkeys), 1)
    diff = (ATTN_BLK + ii - jj) if nblk > 1 else (ii - jj)
    band = (diff >= 0) & (diff <= ATTN_BLK)
    bias = -(slope * float(dil)) * diff.astype(F32)
    return band, bias, jj


def _deinterleave(dst_ref, src_ref, dil, s):
    seg = s // dil
    if dil == 1:
        dst_ref[...] = src_ref[...].astype(dst_ref.dtype)
    else:
        for r in range(dil):
            dst_ref[r * seg:(r + 1) * seg, :] = src_ref[pl.ds(r, seg, stride=dil), :].astype(dst_ref.dtype)


def _attn_fwd(proj_a, slopes):
    _, s, d = proj_a.shape
    heads = d // HEAD_DIM
    scale = HEAD_DIM ** -0.5
    n_t = s // ATTN_BLK
    ng = len(DILATIONS)

    def body(q_ref, k_ref, v_ref, sl_ref, o_ref, lse_ref, qd, kd, vd, od, ld, og, lg):
        slope = sl_ref[...][:, :1]
        for g, dil in enumerate(DILATIONS):
            nblk = s // dil // ATTN_BLK
            _deinterleave(qd, q_ref, dil, s)
            _deinterleave(kd, k_ref, dil, s)
            _deinterleave(vd, v_ref, dil, s)
            band, bias, jj = _attn_masks(nblk, slope, dil)

            def blk(t, carry, nblk=nblk, band=band, bias=bias, jj=jj):
                cur = pl.multiple_of(t * ATTN_BLK, ATTN_BLK)
                q = qd[pl.ds(cur, ATTN_BLK), :]
                if nblk > 1:
                    prev = pl.multiple_of(jnp.maximum(t - 1, 0) * ATTN_BLK, ATTN_BLK)
                    kk = jnp.concatenate([kd[pl.ds(prev, ATTN_BLK), :], kd[pl.ds(cur, ATTN_BLK), :]], axis=0)
                    vv = jnp.concatenate([vd[pl.ds(prev, ATTN_BLK), :], vd[pl.ds(cur, ATTN_BLK), :]], axis=0)
                    valid = band & (jj >= jnp.where(t % nblk == 0, ATTN_BLK, 0))
                else:
                    kk = kd[pl.ds(cur, ATTN_BLK), :]
                    vv = vd[pl.ds(cur, ATTN_BLK), :]
                    valid = band
                sc = _dot(q, kk, NT) * scale + bias
                sc = jnp.where(valid, sc, -jnp.inf)
                m = jnp.max(sc, axis=1, keepdims=True)
                p = jnp.exp(sc - m)
                l = jnp.sum(p, axis=1, keepdims=True)
                od[pl.ds(cur, ATTN_BLK), :] = _dot(p, vv, NN) / l
                ld[pl.ds(cur, ATTN_BLK), :] = jnp.broadcast_to(m + jnp.log(l), (ATTN_BLK, HEAD_DIM))
                return carry

            lax.fori_loop(0, n_t, blk, 0, unroll=ATTN_UNROLL)
            seg = s // dil
            if dil == 1:
                og[g] = od[...]
                lg[g] = ld[...]
            else:
                for r in range(dil):
                    og[g, pl.ds(r, seg, stride=dil), :] = od[r * seg:(r + 1) * seg, :]
                    lg[g, pl.ds(r, seg, stride=dil), :] = ld[r * seg:(r + 1) * seg, :]

        ch = 256

        def combine(c, carry):
            rows = pl.ds(pl.multiple_of(c * ch, ch), ch)
            ls = [lg[g, rows, :] for g in range(ng)]
            mx = functools.reduce(jnp.maximum, ls)
            es = [jnp.exp(x - mx) for x in ls]
            den = functools.reduce(jnp.add, es)
            num = functools.reduce(jnp.add, [es[g] * og[g, rows, :] for g in range(ng)])
            o_ref[rows, :] = (num / den).astype(o_ref.dtype)
            lse_ref[rows, :] = mx + jnp.log(den)
            return carry

        lax.fori_loop(0, s // ch, combine, 0)

    def col(slot):
        return pl.BlockSpec((None, s, HEAD_DIM), lambda h: (slot, 0, h))

    head = pl.BlockSpec((s, HEAD_DIM), lambda h: (0, h))
    return pl.pallas_call(
        body, name="attn_fwd", grid=(heads,),
        in_specs=[col(0), col(1), col(2), pl.BlockSpec((None, 1, HEAD_DIM), lambda h: (h, 0, 0))],
        out_specs=[head, head],
        out_shape=[jax.ShapeDtypeStruct((s, d), BF16), jax.ShapeDtypeStruct((s, d), F32)],
        scratch_shapes=[pltpu.VMEM((s, HEAD_DIM), BF16)] * 3 + [pltpu.VMEM((s, HEAD_DIM), F32)] * 2
        + [pltpu.VMEM((ng, s, HEAD_DIM), F32)] * 2,
        compiler_params=_params(40),
    )(proj_a, proj_a, proj_a, slopes)


def _attn_bwd(proj_a, slopes, y_attn, lse, dy):
    _, s, d = proj_a.shape
    heads = d // HEAD_DIM
    scale = HEAD_DIM ** -0.5
    n_t = s // ATTN_BLK

    def body(q_ref, k_ref, v_ref, sl_ref, o_ref, lse_ref, dy_ref, out_ref,
             qd, kd, vd, dod, lsd, dld, delta, dqd, dkd, dvd, dqa, dka, dva):
        slope = sl_ref[...][:, :1]
        dyv = dy_ref[...]
        delta[...] = jnp.broadcast_to(
            jnp.sum(dyv * o_ref[...].astype(F32), axis=1, keepdims=True), (s, HEAD_DIM))
        for g, dil in enumerate(DILATIONS):
            nblk = s // dil // ATTN_BLK
            seg = s // dil
            _deinterleave(qd, q_ref, dil, s)
            _deinterleave(kd, k_ref, dil, s)
            _deinterleave(vd, v_ref, dil, s)
            _deinterleave(dod, dy_ref, dil, s)
            _deinterleave(lsd, lse_ref, dil, s)
            _deinterleave(dld, delta, dil, s)
            dkd[...] = jnp.zeros_like(dkd)
            dvd[...] = jnp.zeros_like(dvd)
            band, bias, jj = _attn_masks(nblk, slope, dil)

            def blk(t, carry, nblk=nblk, band=band, bias=bias, jj=jj):
                cur = pl.multiple_of(t * ATTN_BLK, ATTN_BLK)
                rows = pl.ds(cur, ATTN_BLK)
                q = qd[rows, :]
                do = dod[rows, :]
                lse_b = lsd[rows, :]
                dl_b = dld[rows, :]
                if nblk > 1:
                    prev = pl.multiple_of(jnp.maximum(t - 1, 0) * ATTN_BLK, ATTN_BLK)
                    prows = pl.ds(prev, ATTN_BLK)
                    kk = jnp.concatenate([kd[prows, :], kd[rows, :]], axis=0)
                    vv = jnp.concatenate([vd[prows, :], vd[rows, :]], axis=0)
                    valid = band & (jj >= jnp.where(t % nblk == 0, ATTN_BLK, 0))
                    lse_b = jnp.concatenate([lse_b, lse_b], axis=1)
                    dl_b = jnp.concatenate([dl_b, dl_b], axis=1)
                else:
                    kk = kd[rows, :]
                    vv = vd[rows, :]
                    valid = band
                sc = _dot(q, kk, NT) * scale + bias
                p = jnp.where(valid, jnp.exp(sc - lse_b), 0.0)
                dp = _dot(do, vv, NT)
                ds = p * (dp - dl_b)
                dv_b = _dot(p, do, TN)
                dk_b = _dot(ds, q, TN) * scale
                dqd[rows, :] = _dot(ds, kk, NN) * scale
                if nblk > 1:
                    dkd[prows, :] += dk_b[:ATTN_BLK]
                    dvd[prows, :] += dv_b[:ATTN_BLK]
                    dkd[rows, :] += dk_b[ATTN_BLK:]
                    dvd[rows, :] += dv_b[ATTN_BLK:]
                else:
                    dkd[rows, :] += dk_b
                    dvd[rows, :] += dv_b
                return carry

            lax.fori_loop(0, n_t, blk, 0, unroll=ATTN_UNROLL)
            for acc, part in ((dqa, dqd), (dka, dkd), (dva, dvd)):
                if dil == 1:
                    acc[...] = part[...]
                else:
                    for r in range(dil):
                        acc[pl.ds(r, seg, stride=dil), :] += part[r * seg:(r + 1) * seg, :]
        out_ref[0] = dqa[...].astype(out_ref.dtype)
        out_ref[1] = dka[...].astype(out_ref.dtype)
        out_ref[2] = dva[...].astype(out_ref.dtype)

    def col(slot):
        return pl.BlockSpec((None, s, HEAD_DIM), lambda h: (slot, 0, h))

    head = pl.BlockSpec((s, HEAD_DIM), lambda h: (0, h))
    return pl.pallas_call(
        body, name="attn_bwd", grid=(heads,),
        in_specs=[col(0), col(1), col(2), pl.BlockSpec((None, 1, HEAD_DIM), lambda h: (h, 0, 0)),
                  head, head, head],
        out_specs=pl.BlockSpec((N_QKV, s, HEAD_DIM), lambda h: (0, 0, h)),
        out_shape=jax.ShapeDtypeStruct((N_QKV, s, d), BF16),
        scratch_shapes=[pltpu.VMEM((s, HEAD_DIM), BF16)] * 4 + [pltpu.VMEM((s, HEAD_DIM), F32)] * 9,
        compiler_params=_params(48),
    )(proj_a, proj_a, proj_a, slopes, y_attn, lse, dy)


def _expm1(x):
    small = x * (1.0 + x * (0.5 + x * (1.0 / 6.0 + x * (1.0 / 24.0 + x * (1.0 / 120.0)))))
    return jnp.where(jnp.abs(x) < 0.1, small, jnp.exp(x) - 1.0)


def _softplus(x):
    return jnp.maximum(x, 0.0) + jnp.log1p(jnp.exp(-jnp.abs(x)))


GELU_K = 0.7978845608028654
GELU_C = 0.044715


def _gelu(x):
    t = jnp.tanh(GELU_K * (x + GELU_C * x * x * x))
    return 0.5 * x * (1.0 + t), t


def _gelu_grad(x, t):
    return 0.5 * (1.0 + t) + 0.5 * x * (1.0 - t * t) * GELU_K * (1.0 + 3.0 * GELU_C * x * x)


def _lru_gates(xc, wa, ba, wx, bx, sp):
    r = _sigmoid(_dot(xc, wa, NN) + ba)
    ig = _sigmoid(_dot(xc, wx, NN) + bx)
    log_a = -LRU_C * r * sp
    a = jnp.exp(log_a)
    mult = jnp.sqrt(-_expm1(2.0 * log_a))
    return r, ig, a, mult


def _scan_fwd(a, u, tt):
    row = lax.broadcasted_iota(jnp.int32, a.shape, 0)
    sh = 1
    while sh < tt:
        keep = row >= sh
        a_s = jnp.where(keep, pltpu.roll(a, sh, 0), 1.0)
        u_s = jnp.where(keep, pltpu.roll(u, sh, 0), 0.0)
        u = a * u_s + u
        a = a * a_s
        sh *= 2
    return a, u


def _scan_bwd(b, g, tt):
    row = lax.broadcasted_iota(jnp.int32, b.shape, 0)
    sh = 1
    while sh < tt:
        keep = row < tt - sh
        b_s = jnp.where(keep, pltpu.roll(b, tt - sh, 0), 1.0)
        g_s = jnp.where(keep, pltpu.roll(g, tt - sh, 0), 0.0)
        g = g + b * g_s
        b = b * b_s
        sh *= 2
    return b, g


def _conv_rows(xpad_ref, cw, cb, s):
    acc = cb
    for j in range(CONV_TAPS):
        off = 8 - (CONV_TAPS - 1) + j
        acc = acc + cw[j:j + 1, :] * xpad_ref[off:off + s, :]
    return acc


LRU_TILE = 128


def _lru_specs(s, d):
    heads = d // HEAD_DIM

    def col(slot):
        return pl.BlockSpec((None, s, HEAD_DIM), lambda h: (slot, 0, h))

    vec = pl.BlockSpec((1, HEAD_DIM), lambda h: (0, h))
    mat = pl.BlockSpec((None, HEAD_DIM, HEAD_DIM), lambda h: (h, 0, 0))
    cw = pl.BlockSpec((8, HEAD_DIM), lambda h: (0, h))
    head = pl.BlockSpec((s, HEAD_DIM), lambda h: (0, h))
    return heads, col, vec, mat, cw, head


def _lru_fwd(proj_b, conv_w, conv_b, wa, ba, wx, bx, lam):
    _, s, d = proj_b.shape
    heads, col, vec, mat, cws, head = _lru_specs(s, d)
    tt = LRU_TILE

    def body(xr_ref, xg_ref, cw_ref, cb_ref, wa_ref, ba_ref, wx_ref, bx_ref, lam_ref, y_ref, h_ref, xpad, xc_s):
        xpad[0:8, :] = jnp.zeros((8, HEAD_DIM), F32)
        xpad[8:8 + s, :] = xr_ref[...]
        xc_s[...] = _conv_rows(xpad, cw_ref[...], cb_ref[...], s)
        sp = _softplus(-lam_ref[...])
        wav, wxv, bav, bxv = wa_ref[...], wx_ref[...], ba_ref[...], bx_ref[...]

        def tile(i, hc):
            rows = pl.ds(pl.multiple_of(i * tt, tt), tt)
            xc = xc_s[rows, :]
            _, ig, a, mult = _lru_gates(xc, wav, bav, wxv, bxv, sp)
            pa, hl = _scan_fwd(a, mult * (ig * xc), tt)
            h = hl + pa * hc
            h_ref[rows, :] = h
            gel, _ = _gelu(xg_ref[rows, :])
            y_ref[rows, :] = (h * gel).astype(y_ref.dtype)
            return h[tt - 1:tt, :]

        lax.fori_loop(0, s // tt, tile, jnp.zeros((1, HEAD_DIM), F32))

    return pl.pallas_call(
        body, name="lru_fwd", grid=(heads,),
        in_specs=[col(0), col(1), cws, vec, mat, vec, mat, vec, vec],
        out_specs=[head, head],
        out_shape=[jax.ShapeDtypeStruct((s, d), BF16), jax.ShapeDtypeStruct((s, d), F32)],
        scratch_shapes=[pltpu.VMEM((s + 8, HEAD_DIM), F32), pltpu.VMEM((s, HEAD_DIM), F32)],
        compiler_params=_params(32),
    )(proj_b, proj_b, conv_w, conv_b, wa, ba, wx, bx, lam)


def _lru_bwd(proj_b, h_lru, dy, conv_w, conv_b, wa, ba, wx, bx, lam, dproj_b):
    _, s, d = proj_b.shape
    heads, col, vec, mat, cws, head = _lru_specs(s, d)
    tt = LRU_TILE
    n_t = s // tt

    def body(xr_ref, xg_ref, h_ref, dy_ref, cw_ref, cb_ref, wa_ref, ba_ref, wx_ref, bx_ref, lam_ref, alias_ref,
             out_ref, dcw_ref, dcb_ref, dwa_ref, dba_ref, dwx_ref, dbx_ref, dlam_ref, xpad, xc_s, dxc_s):
        del alias_ref
        xpad[0:8, :] = jnp.zeros((8, HEAD_DIM), F32)
        xpad[8:8 + s, :] = xr_ref[...]
        cwv = cw_ref[...]
        xc_s[...] = _conv_rows(xpad, cwv, cb_ref[...], s)
        dxc_s[s:s + 8, :] = jnp.zeros((8, HEAD_DIM), F32)
        lamv = lam_ref[...]
        sp = _softplus(-lamv)
        wav, wxv, bav, bxv = wa_ref[...], wx_ref[...], ba_ref[...], bx_ref[...]
        dwa_ref[...] = jnp.zeros_like(dwa_ref)
        dwx_ref[...] = jnp.zeros_like(dwx_ref)
        zero = jnp.zeros((1, HEAD_DIM), F32)
        row = lax.broadcasted_iota(jnp.int32, (tt, HEAD_DIM), 0)

        def tile(it, carry):
            dh_next, a_next, dba, dbx, dsp, dcb = carry
            i = n_t - 1 - it
            t0 = pl.multiple_of(i * tt, tt)
            rows = pl.ds(t0, tt)
            xc = xc_s[rows, :]
            r, ig, a, mult = _lru_gates(xc, wav, bav, wxv, bxv, sp)
            h = h_ref[rows, :]
            before = h_ref[pl.ds(pl.multiple_of(jnp.maximum(t0 - 8, 0), 8), 8), :][7:8, :]
            before = before * (i > 0).astype(F32)
            h_prev = jnp.where(row == 0, before, pltpu.roll(h, 1, 0))
            xg = xg_ref[rows, :]
            dyv = dy_ref[rows, :]
            gel, th = _gelu(xg)
            out_ref[1, rows, :] = (dyv * h * _gelu_grad(xg, th)).astype(out_ref.dtype)
            b = jnp.where(row == tt - 1, a_next, pltpu.roll(a, tt - 1, 0))
            pb, z = _scan_bwd(b, dyv * gel, tt)
            dh = z + pb * dh_next
            da = dh * h_prev
            dmult = dh * (ig * xc)
            dig = dh * (mult * xc)
            dla = da * a - dmult * (a * a / mult)
            dzr = dla * (-LRU_C * sp) * (r * (1.0 - r))
            dzx = dig * (ig * (1.0 - ig))
            dxc = dh * (mult * ig) + _dot(dzr, wav, NT) + _dot(dzx, wxv, NT)
            dxc_s[rows, :] = dxc
            dwa_ref[...] += _dot(xc, dzr, TN)
            dwx_ref[...] += _dot(xc, dzx, TN)
            return (dh[0:1, :], a[0:1, :],
                    dba + jnp.sum(dzr, axis=0, keepdims=True),
                    dbx + jnp.sum(dzx, axis=0, keepdims=True),
                    dsp + jnp.sum(dla * (-LRU_C * r), axis=0, keepdims=True),
                    dcb + jnp.sum(dxc, axis=0, keepdims=True))

        _, _, dba, dbx, dsp, dcb = lax.fori_loop(0, n_t, tile, (zero, zero, zero, zero, zero, zero))
        dba_ref[...] = dba
        dbx_ref[...] = dbx
        dcb_ref[...] = dcb
        dlam_ref[...] = -dsp * _sigmoid(-lamv)
        dxr = jnp.zeros((s, HEAD_DIM), F32)
        for j in range(CONV_TAPS):
            back = CONV_TAPS - 1 - j
            off = 8 - back
            dcw_ref[j:j + 1, :] = jnp.sum(dxc_s[0:s, :] * xpad[off:off + s, :], axis=0, keepdims=True)
            dxr = dxr + cwv[j:j + 1, :] * dxc_s[back:back + s, :]
        out_ref[0] = dxr.astype(out_ref.dtype)

    return pl.pallas_call(
        body, name="lru_bwd", grid=(heads,),
        in_specs=[col(0), col(1), head, head, cws, vec, mat, vec, mat, vec, vec, pl.BlockSpec(memory_space=pl.ANY)],
        out_specs=[pl.BlockSpec((2, s, HEAD_DIM), lambda h: (0, 0, h)),
                   pl.BlockSpec((CONV_TAPS, HEAD_DIM), lambda h: (0, h)), vec, mat, vec, mat, vec, vec],
        out_shape=[jax.ShapeDtypeStruct(dproj_b.shape, dproj_b.dtype),
                   jax.ShapeDtypeStruct((CONV_TAPS, d), F32), jax.ShapeDtypeStruct((1, d), F32),
                   jax.ShapeDtypeStruct(wa.shape, F32), jax.ShapeDtypeStruct((1, d), F32),
                   jax.ShapeDtypeStruct(wx.shape, F32), jax.ShapeDtypeStruct((1, d), F32),
                   jax.ShapeDtypeStruct((1, d), F32)],
        scratch_shapes=[pltpu.VMEM((s + 8, HEAD_DIM), F32), pltpu.VMEM((s, HEAD_DIM), F32),
                        pltpu.VMEM((s + 8, HEAD_DIM), F32)],
        input_output_aliases={11: 0},
        compiler_params=_params(32),
    )(proj_b, proj_b, h_lru, dy, conv_w, conv_b, wa, ba, wx, bx, lam, dproj_b)


def _place():
    x, y, c = (lax.axis_index(n) for n in AXES)
    return x, y, c


def _other_chips(x, y):
    return [(1 - x, y), (x, 1 - y), (1 - x, 1 - y)]


HBM = pl.BlockSpec(memory_space=pl.ANY)


def _cast_shard(w, chip_arr, name):
    r, cols = w.shape
    rh = r // 2
    tr = _row_tile(rh, cols * 4)
    nt = rh // tr

    def body(chip_ref, w_ref, o_ref):
        del chip_ref
        o_ref[...] = w_ref[...].astype(BF16)

    return pl.pallas_call(
        body, name=name,
        grid_spec=pltpu.PrefetchScalarGridSpec(
            num_scalar_prefetch=1, grid=(2, nt),
            in_specs=[pl.BlockSpec((tr, cols), lambda h, i, chip_ref: (h * nt + i, 0))],
            out_specs=pl.BlockSpec((None, None, tr, cols), lambda h, i, chip_ref: (chip_ref[0], h, i, 0))),
        out_shape=jax.ShapeDtypeStruct((N_CHIPS, 2, rh, cols), BF16), compiler_params=_params(32),
    )(chip_arr, w)


HBM_SPEC = pl.BlockSpec(memory_space=pltpu.HBM)
SEM_SPEC = pl.BlockSpec(memory_space=pltpu.SEMAPHORE)
EFFECT = pltpu.SideEffectType.DATAFLOW_SIDE_EFFECTING
TOKEN = jax.ShapeDtypeStruct((8, 128), F32)
TOKEN_SPEC = pl.BlockSpec(memory_space=pltpu.VMEM)


def _in_hbm(arrays):
    return [pltpu.with_memory_space_constraint(a, pltpu.HBM) for a in arrays]


def _hbm_like(arrays):
    return [pltpu.HBM(a.shape, a.dtype) for a in arrays]


def _sems(n):
    return pltpu.SemaphoreType.DMA((n,))


def _remote(src, dst, send_sem, recv_sem, to):
    return pltpu.make_async_remote_copy(src_ref=src, dst_ref=dst, send_sem=send_sem, recv_sem=recv_sem,
                                        device_id=to, device_id_type=MESH)


def _gather_start(bufs, groups):
    n = len(bufs)
    ng = len(groups)

    def body(*refs):
        ins = refs[:n]
        sems = refs[n:n + 2 * ng]
        x, y, c = _place()
        me = 2 * x + y
        for g, ws in enumerate(groups):
            for i, w in enumerate(ws):
                for j, (cx, cy) in enumerate(_other_chips(x, y)):
                    mine = ins[w].at[me, c]
                    _remote(mine, mine, sems[2 * g].at[3 * i + j], sems[2 * g + 1].at[3 * i + j], (cx, cy, c)).start()

    sem_shapes = []
    for ws in groups:
        sem_shapes += [_sems(3 * len(ws)), _sems(3 * len(ws))]
    res = pl.pallas_call(
        body, name="gather_start", in_specs=[HBM_SPEC] * n,
        out_specs=[SEM_SPEC] * (2 * ng) + [HBM_SPEC] * n, out_shape=sem_shapes + _hbm_like(bufs),
        input_output_aliases={w: 2 * ng + w for w in range(n)},
        compiler_params=pltpu.CompilerParams(has_side_effects=EFFECT),
    )(*_in_hbm(bufs))
    return [(res[2 * g], res[2 * g + 1]) for g in range(ng)], list(res[2 * ng:])


def _gather_forward(bufs, recv, after, name):
    m = len(bufs)

    def body(*refs):
        ins, recv_in = refs[:m], refs[m]
        fsend, frecv = refs[m + 1 + len(after)], refs[m + 2 + len(after)]
        x, y, c = _place()
        for j, (cx, cy) in enumerate(_other_chips(x, y)):
            for i in range(m):
                landed = ins[i].at[2 * cx + cy, c]
                k = 3 * i + j
                _remote(landed, landed, fsend.at[k], recv_in.at[k], (cx, cy, c)).wait_recv()
                _remote(landed, landed, fsend.at[k], frecv.at[k], (x, y, 1 - c)).start()

    res = pl.pallas_call(
        body, name=name, in_specs=[HBM_SPEC] * m + [SEM_SPEC] + [HBM] * len(after),
        out_specs=[SEM_SPEC, SEM_SPEC] + [HBM_SPEC] * m, out_shape=[_sems(3 * m), _sems(3 * m)] + _hbm_like(bufs),
        input_output_aliases={i: 2 + i for i in range(m)},
        compiler_params=pltpu.CompilerParams(has_side_effects=EFFECT),
    )(*bufs, recv, *after)
    return (res[0], res[1]), list(res[2:])


def _gather_finish(bufs, send, fsend, frecv, name):
    m = len(bufs)

    def body(*refs):
        ins = refs[:m]
        send_in, fsend_in, frecv_in = refs[m:m + 3]
        x, y, c = _place()
        me = 2 * x + y
        for j, (cx, cy) in enumerate(_other_chips(x, y)):
            cj = 2 * cx + cy
            for i in range(m):
                k = 3 * i + j
                mine = ins[i].at[me, c]
                _remote(mine, mine, send_in.at[k], frecv_in.at[k], (cx, cy, c)).wait_send()
                landed = ins[i].at[cj, c]
                _remote(landed, landed, fsend_in.at[k], frecv_in.at[k], (x, y, 1 - c)).wait_send()
                theirs = ins[i].at[cj, 1 - c]
                _remote(theirs, theirs, fsend_in.at[k], frecv_in.at[k], (x, y, 1 - c)).wait_recv()

    return list(pl.pallas_call(
        body, name=name, in_specs=[HBM_SPEC] * m + [SEM_SPEC] * 3,
        out_specs=[HBM_SPEC] * m, out_shape=_hbm_like(bufs),
        input_output_aliases={i: i for i in range(m)},
        compiler_params=pltpu.CompilerParams(has_side_effects=EFFECT),
    )(*bufs, send, fsend, frecv))


def _pair_exchange(grads, name):
    n = len(grads)

    def body(*refs):
        ins, outs = refs[:n], refs[n:2 * n]
        send_sems, recv_sems = refs[2 * n:]
        x, y, c = _place()
        sibling = (x, y, 1 - c)
        cps = []
        for w in range(n):
            for j in range(N_CHIPS):
                cp = pltpu.make_async_remote_copy(
                    src_ref=ins[w].at[j, 1 - c], dst_ref=outs[w].at[j], send_sem=send_sems.at[N_CHIPS * w + j],
                    recv_sem=recv_sems.at[N_CHIPS * w + j], device_id=sibling, device_id_type=MESH)
                cp.start()
                cps.append(cp)
        for cp in cps:
            cp.wait()

    return pl.pallas_call(
        body, name=name, in_specs=[HBM] * n, out_specs=[HBM] * n,
        out_shape=[jax.ShapeDtypeStruct((N_CHIPS,) + a.shape[2:], a.dtype) for a in grads],
        scratch_shapes=[pltpu.SemaphoreType.DMA((N_CHIPS * n,)), pltpu.SemaphoreType.DMA((N_CHIPS * n,))],
    )(*grads)


def _chip_start(sums, name):
    m = len(sums)
    lands = [lax.empty(a.shape, a.dtype) for a in sums]

    def body(*refs):
        ins, land_in = refs[:m], refs[m:2 * m]
        send, recv = refs[2 * m], refs[2 * m + 1]
        token = refs[4 * m + 2]
        x, y, c = _place()
        me = 2 * x + y
        for i in range(m):
            for j, (cx, cy) in enumerate(_other_chips(x, y)):
                _remote(ins[i].at[2 * cx + cy], land_in[i].at[me], send.at[3 * i + j], recv.at[3 * i + j],
                        (cx, cy, c)).start()
        token[...] = jnp.zeros_like(token)

    res = pl.pallas_call(
        body, name=name, in_specs=[HBM_SPEC] * (2 * m),
        out_specs=[SEM_SPEC, SEM_SPEC] + [HBM_SPEC] * (2 * m) + [TOKEN_SPEC],
        out_shape=[_sems(3 * m), _sems(3 * m)] + _hbm_like(sums) + _hbm_like(lands) + [TOKEN],
        input_output_aliases={i: 2 + i for i in range(2 * m)},
        compiler_params=pltpu.CompilerParams(has_side_effects=EFFECT),
    )(*_in_hbm(sums), *_in_hbm(lands))
    return (res[0], res[1]), list(res[2:2 + m]), list(res[2 + m:2 + 2 * m]), res[2 + 2 * m]


def _chip_wait(sems, sums, lands, after, name):
    m = len(sums)

    def body(*refs):
        ins, land_in = refs[:m], refs[m:2 * m]
        send_in, recv_in = refs[2 * m], refs[2 * m + 1]
        x, y, c = _place()
        for i in range(m):
            for j, (cx, cy) in enumerate(_other_chips(x, y)):
                cj = 2 * cx + cy
                cp = _remote(ins[i].at[cj], land_in[i].at[cj], send_in.at[3 * i + j], recv_in.at[3 * i + j], (cx, cy, c))
                cp.wait_send()
                cp.wait_recv()

    res = pl.pallas_call(
        body, name=name, in_specs=[HBM_SPEC] * (2 * m) + [SEM_SPEC, SEM_SPEC] + [HBM] * len(after),
        out_specs=[HBM_SPEC] * (2 * m), out_shape=_hbm_like(sums) + _hbm_like(lands),
        input_output_aliases={i: i for i in range(2 * m)},
        compiler_params=pltpu.CompilerParams(has_side_effects=EFFECT),
    )(*sums, *lands, sems[0], sems[1], *after)
    return list(res[:m]), list(res[m:])


def _half_exchange(bufs, name):
    n = len(bufs)

    def body(*refs):
        outs = refs[n:2 * n]
        send_sems, recv_sems = refs[2 * n:]
        x, y, c = _place()
        sibling = (x, y, 1 - c)
        cps = []
        for w in range(n):
            rc = pltpu.make_async_remote_copy(
                src_ref=outs[w].at[c], dst_ref=outs[w].at[c], send_sem=send_sems.at[w], recv_sem=recv_sems.at[w],
                device_id=sibling, device_id_type=MESH)
            rc.start()
            cps.append(rc)
        for w in range(n):
            theirs = outs[w].at[1 - c]
            pltpu.make_async_remote_copy(
                src_ref=theirs, dst_ref=theirs, send_sem=send_sems.at[w], recv_sem=recv_sems.at[w],
                device_id=sibling, device_id_type=MESH).wait_recv()
        for cp in cps:
            cp.wait_send()

    return pl.pallas_call(
        body, name=name, in_specs=[HBM] * n, out_specs=[HBM] * n,
        out_shape=[jax.ShapeDtypeStruct(a.shape, a.dtype) for a in bufs],
        input_output_aliases={w: w for w in range(n)},
        scratch_shapes=[pltpu.SemaphoreType.DMA((n,)), pltpu.SemaphoreType.DMA((n,))],
    )(*bufs)


def _all_gather8(block, name, own_row):
    def body(in_ref, out_ref, send_sems, recv_sems, local_sem):
        x, y, c = _place()
        me = 4 * x + 2 * y + c
        if own_row:
            mine = pltpu.make_async_copy(in_ref, out_ref.at[me], local_sem)
            mine.start()
        flips = [(fx, fy, fc) for fx in (0, 1) for fy in (0, 1) for fc in (0, 1)][1:]
        cps = []
        for k, (fx, fy, fc) in enumerate(flips):
            cp = pltpu.make_async_remote_copy(
                src_ref=in_ref, dst_ref=out_ref.at[me], send_sem=send_sems.at[k], recv_sem=recv_sems.at[k],
                device_id=(x ^ fx, y ^ fy, c ^ fc), device_id_type=MESH)
            cp.start()
            cps.append(cp)
        for k, (fx, fy, fc) in enumerate(flips):
            px, py, pc = x ^ fx, y ^ fy, c ^ fc
            theirs = out_ref.at[4 * px + 2 * py + pc]
            pltpu.make_async_remote_copy(
                src_ref=theirs, dst_ref=theirs, send_sem=send_sems.at[k], recv_sem=recv_sems.at[k],
                device_id=(px, py, pc), device_id_type=MESH).wait_recv()
        for cp in cps:
            cp.wait_send()
        if own_row:
            mine.wait()

    return pl.pallas_call(
        body, name=name, in_specs=[HBM], out_specs=HBM,
        out_shape=jax.ShapeDtypeStruct((N_DEV,) + block.shape, block.dtype),
        scratch_shapes=[pltpu.SemaphoreType.DMA((N_DEV - 1,)), pltpu.SemaphoreType.DMA((N_DEV - 1,)),
                        pltpu.SemaphoreType.DMA],
    )(block)


def _pair_sum(grad, recv, c_arr, name):
    _, _, rh, cols = grad.shape
    tr = _row_tile(rh, cols * 4)

    def body(c_ref, g_ref, r_ref, o_ref):
        del c_ref
        o_ref[...] = (g_ref[...].astype(F32) + r_ref[...].astype(F32)).astype(o_ref.dtype)

    spec = pl.BlockSpec((None, tr, cols), lambda j, i, c_ref: (j, i, 0))
    return pl.pallas_call(
        body, name=name,
        grid_spec=pltpu.PrefetchScalarGridSpec(
            num_scalar_prefetch=1, grid=(N_CHIPS, rh // tr),
            in_specs=[pl.BlockSpec((None, None, tr, cols), lambda j, i, c_ref: (j, c_ref[0], i, 0)), spec],
            out_specs=spec),
        out_shape=jax.ShapeDtypeStruct(recv.shape, BF16), compiler_params=_params(32),
    )(c_arr, grad, recv)


def _chip_sum(parts, own, place_arr, name):
    _, rh, cols = parts.shape
    tr = _row_tile(rh, cols * 4)

    def body(place_ref, p_ref, own_ref, o_ref):
        chip = place_ref[0]
        o_ref[...] = jnp.zeros_like(o_ref)
        for k in range(N_CHIPS):
            @pl.when(chip == k)
            def _():
                o_ref[...] += own_ref[...].astype(F32)

            @pl.when(chip != k)
            def _(k=k):
                o_ref[...] += p_ref[k].astype(F32)

    return pl.pallas_call(
        body, name=name,
        grid_spec=pltpu.PrefetchScalarGridSpec(
            num_scalar_prefetch=1, grid=(rh // tr,),
            in_specs=[pl.BlockSpec((N_CHIPS, tr, cols), lambda i, place_ref: (0, i, 0)),
                      pl.BlockSpec((None, tr, cols), lambda i, place_ref: (place_ref[0], i, 0))],
            out_specs=pl.BlockSpec((None, tr, cols), lambda i, place_ref: (place_ref[1], i, 0))),
        out_shape=jax.ShapeDtypeStruct((2, rh, cols), F32), compiler_params=_params(32),
    )(place_arr, parts, own)


def _adamw_math(w, g, m, v):
    m = ADAM_B1 * m + (1.0 - ADAM_B1) * g
    v = ADAM_B2 * v + (1.0 - ADAM_B2) * (g * g)
    m_hat = m / (1.0 - ADAM_B1 ** ADAM_STEP)
    v_hat = v / (1.0 - ADAM_B2 ** ADAM_STEP)
    delta = -ADAM_LR * (m_hat / (jnp.sqrt(v_hat) + ADAM_EPS) + ADAM_WD * w)
    return delta, m, v


def _adamw(w, g, m, v, name):
    rows, cols = w.shape
    tr = _row_tile(rows, cols * 4)

    def body(w_ref, g_ref, m_ref, v_ref, d_ref, nm_ref, nv_ref):
        d_ref[...], nm_ref[...], nv_ref[...] = _adamw_math(w_ref[...], g_ref[...], m_ref[...], v_ref[...])

    spec = pl.BlockSpec((tr, cols), lambda i: (i, 0))
    return pl.pallas_call(
        body, name=name, grid=(rows // tr,), in_specs=[spec] * 4, out_specs=[spec] * 3,
        out_shape=[jax.ShapeDtypeStruct(w.shape, F32)] * 3, compiler_params=_params(32),
    )(w, g, m, v)


def _sum8_adamw(parts, own, dev_arr, w, m, v):
    _, rows, cols = parts.shape
    tr = _row_tile(rows, cols * 4, VMEM_MIB // 4)

    def body(dev_ref, p_ref, own_ref, w_ref, m_ref, v_ref, g_ref, d_ref, nm_ref, nv_ref):
        dev = dev_ref[0]
        g_ref[...] = jnp.zeros_like(g_ref)
        for k in range(N_DEV):
            @pl.when(dev == k)
            def _():
                g_ref[...] += own_ref[...]

            @pl.when(dev != k)
            def _(k=k):
                g_ref[...] += p_ref[k]

        d_ref[...], nm_ref[...], nv_ref[...] = _adamw_math(w_ref[...], g_ref[...], m_ref[...], v_ref[...])

    spec = pl.BlockSpec((tr, cols), lambda i, dev_ref: (i, 0))
    return pl.pallas_call(
        body, name="small_sum_adamw",
        grid_spec=pltpu.PrefetchScalarGridSpec(
            num_scalar_prefetch=1, grid=(rows // tr,),
            in_specs=[pl.BlockSpec((N_DEV, tr, cols), lambda i, dev_ref: (0, i, 0))] + [spec] * 4,
            out_specs=[spec] * 4),
        out_shape=[jax.ShapeDtypeStruct((rows, cols), F32)] * 4, compiler_params=_params(32),
    )(dev_arr, parts, own, w, m, v)


def _pack_rows(pieces, rows, name):
    cols = pieces[0].shape[1]
    n = len(pieces)

    def body(*refs):
        o_ref = refs[n]
        o_ref[...] = jnp.zeros_like(o_ref)
        at = 0
        for p_ref in refs[:n]:
            r = p_ref.shape[0]
            o_ref[at:at + r, :] = p_ref[...]
            at += r

    return pl.pallas_call(
        body, name=name, out_shape=jax.ShapeDtypeStruct((rows, cols), F32), compiler_params=_params(32),
    )(*pieces)


def kernel(x, norm_mix_g, w_in, conv_w, conv_b, lru_wa, lru_ba, lru_wx, lru_bx, lru_lambda, w_proj_attn, w_proj_lru, w_out, norm_mlp_g, w_up, w_down, norm_final_g, loss_target, m_norm_mix_g, m_w_in, m_conv_w, m_conv_b, m_lru_wa, m_lru_ba, m_lru_wx, m_lru_bx, m_lru_lambda, m_w_proj_attn, m_w_proj_lru, m_w_out, m_norm_mlp_g, m_w_up, m_w_down, m_norm_final_g, v_norm_mix_g, v_w_in, v_conv_w, v_conv_b, v_lru_wa, v_lru_ba, v_lru_wx, v_lru_bx, v_lru_lambda, v_w_proj_attn, v_w_proj_lru, v_w_out, v_norm_mlp_g, v_w_up, v_w_down, v_norm_final_g):
    s, d = x.shape[1], x.shape[2]
    ff = w_up.shape[2] * N_CHIPS
    heads = d // HEAD_DIM
    u = d // 4
    assert s % (max(DILATIONS) * ATTN_BLK) == 0 and d % (4 * HEAD_DIM) == 0 and ff == 4 * d
    xs = x[0]
    target = loss_target[0]
    gf = norm_final_g.reshape(1, d)
    wa, wx = lru_wa[0], lru_wx[0]
    core = lax.axis_index("c").astype(jnp.int32)
    chip = (2 * lax.axis_index("x") + lax.axis_index("y")).astype(jnp.int32)
    cidx = core.reshape(1)
    chip_arr = chip.reshape(1)
    place_arr = jnp.stack([chip, core])
    dev_arr = (2 * chip + core).reshape(1)
    slopes = jnp.broadcast_to(
        (2.0 ** (-8.0 * jnp.arange(1, heads + 1, dtype=F32) / heads))[:, None, None], (heads, 1, HEAD_DIM))

    big = [w_in[0], w_proj_attn[0], w_proj_lru[0], w_out[0], w_up[0], w_down[0]]
    names = ["w_in", "w_proj_attn", "w_proj_lru", "w_out", "w_up", "w_down"]
    bufs = [_cast_shard(w, chip_arr, "cast_" + nm) for w, nm in zip(big, names)]
    (sem_a, sem_b), bufs = _gather_start(bufs, [[0], [1, 2, 3, 4, 5]])
    fsem_a, buf_a = _gather_forward(bufs[:1], sem_a[1], [], "gather_forward_w_in")
    buf_a = _gather_finish(buf_a, sem_a[0], fsem_a[0], fsem_a[1], "gather_finish_w_in")
    w_in_g = buf_a[0].reshape(N_CHIPS, d, N_SLOTS * u)
    cw_pad = jnp.pad(conv_w[0], ((0, 8 - CONV_TAPS), (0, 0)))
    cw_all = _all_gather8(cw_pad, "gather_conv_w", True)
    conv_w_full = jnp.concatenate([cw_all[2 * j] for j in range(N_CHIPS)], axis=1)

    xn = _rms_fwd(xs, norm_mix_g, "norm_mix")
    proj_a = _proj_in(xn, w_in_g, 0, 4 * N_QKV, N_QKV)
    proj_b = _proj_in(xn, w_in_g, 4 * N_QKV, 4 * (N_SLOTS - N_QKV), N_SLOTS - N_QKV)
    y_attn, lse = _attn_fwd(proj_a, slopes)
    y_lru, h_lru = _lru_fwd(proj_b, conv_w_full, conv_b, wa, lru_ba, wx, lru_bx, lru_lambda)
    fsem_b, buf_b = _gather_forward(bufs[1:], sem_b[1], [y_attn, y_lru], "gather_forward_rest")
    buf_b = _gather_finish(buf_b, sem_b[0], fsem_b[0], fsem_b[1], "gather_finish_rest")
    wpa_g = buf_b[0].reshape(d, d)
    wpl_g = buf_b[1].reshape(d, d)
    wout_g = buf_b[2].reshape(d, d)
    wup_g = buf_b[3].reshape(N_CHIPS, d, d)
    wdown_g = buf_b[4].reshape(ff, d)

    tn = u
    sd_f32 = jax.ShapeDtypeStruct((s, d), F32)
    sd_bf16 = jax.ShapeDtypeStruct((s, d), BF16)
    col = pl.BlockSpec((s, tn), lambda i, j, k: (0, j))

    def slot(n):
        return pl.BlockSpec((None, s, tn), lambda i, j, k: (n, 0, j))

    p_attn = _mm_nn("proj_attn", y_attn, wpa_g, [], [], [sd_f32], [col], _store, tn)[0]

    def merge(acc, extras, outs):
        pa_ref, ga_ref, gl_ref = extras
        merged = _sigmoid(ga_ref[...]) * pa_ref[...] + _sigmoid(gl_ref[...]) * acc
        outs[0][...] = merged.astype(BF16)
        outs[1][...] = acc

    tn2 = max(HEAD_DIM, u // 2)
    col2 = pl.BlockSpec((s, tn2), lambda i, j, k: (0, j))

    def slot2(n):
        return pl.BlockSpec((None, s, tn2), lambda i, j, k: (n, 0, j))

    merged, p_lru = _mm_nn("proj_lru_merge", y_lru, wpl_g, [p_attn, proj_b, proj_b], [col2, slot2(2), slot2(3)],
                           [sd_bf16, sd_f32], [col2, col2], merge, tn2)

    def add_resid(acc, extras, outs):
        outs[0][...] = extras[0][...] + acc

    h1 = _mm_nn("w_out_resid", merged, wout_g, [xs], [col], [sd_f32], [col], add_resid, tn)[0]
    hn = _rms_fwd(h1, norm_mlp_g, "norm_mlp")

    def relu_sq(acc, extras, outs):
        r = jnp.maximum(acc, 0.0)
        outs[0][...] = (r * r).astype(BF16)
        outs[1][...] = r.astype(BF16)

    sf_bf16 = jax.ShapeDtypeStruct((s, ff), BF16)
    hid, relu_up = _mm(
        "w_up_relu2", [hn, wup_g],
        [pl.BlockSpec((s, d), lambda i, j, k: (0, 0)),
         pl.BlockSpec((None, d, tn), lambda i, j, k: (j // 4, 0, j % 4))],
        [sf_bf16, sf_bf16], [col, col], (1, ff // tn, 1), NN, relu_sq)
    h2 = _mm(
        "w_down_resid", [hid, wdown_g, h1],
        [pl.BlockSpec((s, d), lambda i, j, k: (0, k)), pl.BlockSpec((d, tn), lambda i, j, k: (k, j)), col],
        [sd_f32], [col], (1, d // tn, ff // d), NN, add_resid, nk=ff // d, acc_shape=(s, tn))[0]
    loss_part, dh2, dh2_b, d_gf = _loss_head(h2, gf, target)
    loss = lax.psum(loss_part[0, 0], AXES)

    def relu_sq_bwd(acc, extras, outs):
        outs[0][...] = (acc * (2.0 * extras[0][...].astype(F32))).astype(BF16)

    dup = _mm_nt("d_hid", dh2_b, wdown_g, [relu_up], [col], [sf_bf16], [col], relu_sq_bwd, tn)[0]
    tok_d = pl.BlockSpec((s, d), lambda i, j: (0, 0))
    g_wdown = _mm_tn(
        "g_w_down", hid, dh2_b, pl.BlockSpec((s, d), lambda i, j: (0, i)),
        pl.BlockSpec((s, tn), lambda i, j: (0, j)), jax.ShapeDtypeStruct((ff, d), BF16),
        pl.BlockSpec((d, tn), lambda i, j: (i, j)), (ff // d, d // tn), d, tn, s)
    dhn = _mm(
        "d_hn", [dup, wup_g],
        [pl.BlockSpec((s, d), lambda i, j, k: (0, k)), pl.BlockSpec((None, tn, d), lambda i, j, k: (k, j, 0))],
        [sd_f32], [col], (1, d // tn, ff // d), NT, _store, nk=ff // d, acc_shape=(s, tn))[0]
    g_wup = _mm_tn(
        "g_w_up", hn, dup, tok_d, pl.BlockSpec((s, tn), lambda i, j: (0, j)),
        jax.ShapeDtypeStruct((N_CHIPS, d, d), BF16), pl.BlockSpec((None, d, tn), lambda i, j: (j // 4, 0, j % 4)),
        (1, ff // tn), d, tn, s)
    big_m = [m_w_in[0], m_w_proj_attn[0], m_w_proj_lru[0], m_w_out[0], m_w_up[0], m_w_down[0]]
    big_v = [v_w_in[0], v_w_proj_attn[0], v_w_proj_lru[0], v_w_out[0], v_w_up[0], v_w_down[0]]
    big_out = {}

    def reduce_begin(ids, gs, tag):
        g4 = [g.reshape(N_CHIPS, 2, big[i].shape[0] // 2, big[i].shape[1]) for i, g in zip(ids, gs)]
        from_sibling = _pair_exchange(g4, "pair_exchange_" + tag)
        sums = [_pair_sum(g, r, cidx, "pair_sum_" + names[i]) for i, g, r in zip(ids, g4, from_sibling)]
        return _chip_start(sums, "chip_start_" + tag)

    def reduce_end(ids, state, after, tag):
        sems, sums, lands, _ = state
        sums, lands = _chip_wait(sems, sums, lands, after, "chip_wait_" + tag)
        halves = [_chip_sum(p, own, place_arr, "chip_sum_" + names[i]) for i, p, own in zip(ids, lands, sums)]
        full = _half_exchange(halves, "half_exchange_" + tag)
        last = None
        for i, g in zip(ids, full):
            g = g.reshape(big[i].shape)
            dl, nm_, nv_ = _adamw(big[i], g, big_m[i], big_v[i], "adamw_" + names[i])
            big_out[names[i]] = tuple(a[None] for a in (g, dl, nm_, nv_))
            last = dl
        return last

    def after_token(a, state):
        return a + state[3][:1, :1]

    red_mlp = reduce_begin([4, 5], [g_wup, g_wdown], "mlp")
    dh1, dh1_b, d_gmlp = _rms_bwd(h1, after_token(norm_mlp_g, red_mlp), dhn, dh2, "norm_mlp_bwd")

    g_wout = _mm_tn(
        "g_w_out", merged, dh1_b, tok_d, pl.BlockSpec((s, tn), lambda i, j: (0, j)),
        jax.ShapeDtypeStruct((d, d), BF16), pl.BlockSpec((d, tn), lambda i, j: (0, j)), (1, d // tn), d, tn, s)

    def merge_bwd(acc, extras, outs):
        pa_ref, pl_ref, ga_ref, gl_ref = extras
        sa, sl = _sigmoid(ga_ref[...]), _sigmoid(gl_ref[...])
        outs[0][...] = (acc * sa).astype(BF16)
        outs[1][...] = (acc * sl).astype(BF16)
        outs[2][0] = (acc * pa_ref[...] * (sa * (1.0 - sa))).astype(BF16)
        outs[2][1] = (acc * pl_ref[...] * (sl * (1.0 - sl))).astype(BF16)

    nb = N_SLOTS - N_QKV
    d_pa, d_pl, dproj_b = _mm_nt(
        "d_merged", dh1_b, wout_g, [p_attn, p_lru, proj_b, proj_b], [col2, col2, slot2(2), slot2(3)],
        [sd_bf16, sd_bf16, jax.ShapeDtypeStruct((nb, s, d), BF16)],
        [col2, col2, pl.BlockSpec((2, s, tn2), lambda i, j, k: (1, 0, j))], merge_bwd, tn2)
    dy_attn = _mm_nt("d_y_attn", d_pa, wpa_g, [], [], [sd_f32], [col], _store, tn)[0]
    dy_lru = _mm_nt("d_y_lru", d_pl, wpl_g, [], [], [sd_f32], [col], _store, tn)[0]
    g_wpa = _mm_tn(
        "g_w_proj_attn", y_attn, d_pa, tok_d, pl.BlockSpec((s, tn), lambda i, j: (0, j)),
        jax.ShapeDtypeStruct((d, d), BF16), pl.BlockSpec((d, tn), lambda i, j: (0, j)), (1, d // tn), d, tn, s)
    g_wpl = _mm_tn(
        "g_w_proj_lru", y_lru, d_pl, tok_d, pl.BlockSpec((s, tn), lambda i, j: (0, j)),
        jax.ShapeDtypeStruct((d, d), BF16), pl.BlockSpec((d, tn), lambda i, j: (0, j)), (1, d // tn), d, tn, s)

    red_proj = reduce_begin([1, 2, 3], [g_wpa, g_wpl, g_wout], "proj")

    dproj_a = _attn_bwd(proj_a, after_token(slopes, red_proj), y_attn, lse, dy_attn)
    dproj_b, d_cw, d_cb, d_wa, d_ba, d_wx, d_bx, d_lam = _lru_bwd(
        proj_b, h_lru, dy_lru, conv_w_full, conv_b, wa, lru_ba, wx, lru_bx, after_token(lru_lambda, red_proj),
        dproj_b)
    dxn = _dxn(dproj_a, dproj_b, w_in_g, 2 * tn)
    per = N_SLOTS
    g_win_shape = jax.ShapeDtypeStruct((N_CHIPS, d, N_SLOTS * u), BF16)

    def g_win_part(name, dproj, first, prev):
        n_units = 4 * dproj.shape[0]
        return _mm_tn(
            name, xn, dproj, tok_d, pl.BlockSpec((None, s, u), lambda i, j: (j // 4, 0, j % 4)),
            g_win_shape, pl.BlockSpec((None, d, u), lambda i, j: ((j + first) // per, 0, (j + first) % per)),
            (1, n_units), d, u, s, aliases=None if prev is None else {2: 0}, extra=prev)

    g_win = g_win_part("g_w_in_qkv", dproj_a, 0, None)
    g_win = g_win_part("g_w_in_rest", dproj_b, 4 * N_QKV, g_win)
    red_in = reduce_begin([0], [g_win], "w_in")
    grad_x, _, d_gmix = _rms_bwd(xs, after_token(norm_mix_g, red_in), dxn, dh1, "norm_mix_bwd")

    done = reduce_end([4, 5], red_mlp, [grad_x], "mlp")
    done = reduce_end([1, 2, 3], red_proj, [done], "proj")

    mat_rows = heads * HEAD_DIM * HEAD_DIM // d
    vec_names = ["norm_mix_g", "conv_b", "lru_ba", "lru_bx", "lru_lambda", "norm_mlp_g", "norm_final_g"]

    def pack(wa_, wx_, cw_, vecs, name):
        rows = [wa_.reshape(mat_rows, d), wx_.reshape(mat_rows, d), cw_] + [a.reshape(1, d) for a in vecs]
        n = sum(a.shape[0] for a in rows)
        return _pack_rows(rows, n + (-n % 64), name)

    zero_cw = jnp.zeros((CONV_TAPS, d), F32)
    small_g = pack(d_wa, d_wx, d_cw, [d_gmix, d_cb, d_ba, d_bx, d_lam, d_gmlp, d_gf], "pack_small_g")
    small_w = pack(wa, wx, zero_cw, [norm_mix_g, conv_b, lru_ba, lru_bx, lru_lambda, norm_mlp_g, norm_final_g],
                   "pack_small_w")
    small_m = pack(m_lru_wa[0], m_lru_wx[0], zero_cw,
                   [m_norm_mix_g, m_conv_b, m_lru_ba, m_lru_bx, m_lru_lambda, m_norm_mlp_g, m_norm_final_g],
                   "pack_small_m")
    small_v = pack(v_lru_wa[0], v_lru_wx[0], zero_cw,
                   [v_norm_mix_g, v_conv_b, v_lru_ba, v_lru_bx, v_lru_lambda, v_norm_mlp_g, v_norm_final_g],
                   "pack_small_v")
    small_parts = _all_gather8(small_g, "gather_small_grads", False)
    small = _sum8_adamw(small_parts, small_g, dev_arr, small_w, small_m, small_v)
    g_cw = lax.dynamic_slice(small[0][2 * mat_rows:2 * mat_rows + CONV_TAPS], (0, chip * u), (CONV_TAPS, u))
    cw_out = (g_cw,) + tuple(_adamw(conv_w[0], g_cw, m_conv_w[0], v_conv_w[0], "adamw_conv_w"))
    reduce_end([0], red_in, [small[1]], "w_in")

    def small_leaf(kind, name):
        a = small[kind]
        if name == "lru_wa":
            return a[0:mat_rows].reshape(lru_wa.shape)
        if name == "lru_wx":
            return a[mat_rows:2 * mat_rows].reshape(lru_wx.shape)
        if name == "conv_w":
            return cw_out[kind][None]
        row = a[2 * mat_rows + CONV_TAPS + vec_names.index(name)]
        return row if name == "norm_final_g" else row[None]

    order = ["norm_mix_g", "w_in", "conv_w", "conv_b", "lru_wa", "lru_ba", "lru_wx", "lru_bx", "lru_lambda",
             "w_proj_attn", "w_proj_lru", "w_out", "norm_mlp_g", "w_up", "w_down", "norm_final_g"]
    outs = [loss, grad_x[None]]
    for kind in range(4):
        for name in order:
            outs.append(big_out[name][kind] if name in big_out else small_leaf(kind, name))
    return tuple(outs)
```

```python
import functools

import jax
import jax.numpy as jnp
from jax import lax
from jax.experimental import pallas as pl
from jax.experimental.pallas import tpu as pltpu

F32 = jnp.float32
BF16 = jnp.bfloat16
MESH = pl.DeviceIdType.MESH
AXES = ("x", "y", "c")

N_CHIPS = 4
N_DEV = 8
HEAD_DIM = 128
ATTN_BLK = 128
DILATIONS = (1, 4, 16)
ATTN_UNROLL = 4
CONV_TAPS = 4
LRU_C = 8.0
EPS = 1e-6
N_SLOTS = 7
N_QKV = 3
VMEM_MIB = 2 ** 20
VMEM_V7X = 64 * VMEM_MIB

ADAM_LR = 0.001
ADAM_B1 = 0.9
ADAM_B2 = 0.999
ADAM_EPS = 1e-08
ADAM_WD = 0.01
ADAM_STEP = 10

NN = (((1,), (0,)), ((), ()))
NT = (((1,), (1,)), ((), ()))
TN = (((0,), (0,)), ((), ()))


def _params(vmem_mib=None, **kw):
    limit = None if vmem_mib is None else min(vmem_mib * VMEM_MIB, VMEM_V7X - 8 * VMEM_MIB)
    return pltpu.CompilerParams(vmem_limit_bytes=limit, **kw)


def _row_tile(rows, row_bytes, budget=VMEM_MIB):
    t = rows
    while t % 16 == 0 and t * row_bytes > budget:
        t //= 2
    return t


def _dot(a, b, dims):
    return lax.dot_general(a.astype(BF16), b.astype(BF16), dims, preferred_element_type=F32)


def _sigmoid(x):
    return jax.nn.sigmoid(x)


def _rms_fwd(x, g, name):
    s, d = x.shape
    tm = _row_tile(s, d * 4)

    def body(x_ref, g_ref, o_ref):
        xf = x_ref[...]
        r = lax.rsqrt(jnp.mean(xf * xf, axis=-1, keepdims=True) + EPS)
        o_ref[...] = (xf * r * g_ref[...]).astype(o_ref.dtype)

    return pl.pallas_call(
        body, name=name, grid=(s // tm,),
        in_specs=[pl.BlockSpec((tm, d), lambda i: (i, 0)), pl.BlockSpec((1, d), lambda i: (0, 0))],
        out_specs=pl.BlockSpec((tm, d), lambda i: (i, 0)),
        out_shape=jax.ShapeDtypeStruct((s, d), BF16), compiler_params=_params(32),
    )(x, g)


def _rms_bwd(x, g, dy, resid, name):
    s, d = x.shape
    tm = _row_tile(s, d * 4)

    def body(x_ref, g_ref, dy_ref, res_ref, dx_ref, dxb_ref, dg_ref):
        xf = x_ref[...]
        r = lax.rsqrt(jnp.mean(xf * xf, axis=-1, keepdims=True) + EPS)
        xh = xf * r
        dyv = dy_ref[...]
        dxh = dyv * g_ref[...]
        dx = r * (dxh - xh * jnp.mean(dxh * xh, axis=-1, keepdims=True)) + res_ref[...]
        dx_ref[...] = dx
        dxb_ref[...] = dx.astype(BF16)
        part = jnp.sum(dyv * xh, axis=0, keepdims=True)

        @pl.when(pl.program_id(0) == 0)
        def _():
            dg_ref[...] = part

        @pl.when(pl.program_id(0) > 0)
        def _():
            dg_ref[...] += part

    row = pl.BlockSpec((tm, d), lambda i: (i, 0))
    vec = pl.BlockSpec((1, d), lambda i: (0, 0))
    return pl.pallas_call(
        body, name=name, grid=(s // tm,),
        in_specs=[row, vec, row, row], out_specs=[row, row, vec],
        out_shape=[jax.ShapeDtypeStruct((s, d), F32), jax.ShapeDtypeStruct((s, d), BF16),
                   jax.ShapeDtypeStruct((1, d), F32)],
        compiler_params=_params(32),
    )(x, g, dy, resid)


def _loss_head(h2, g, target):
    s, d = h2.shape
    tm = _row_tile(s, d * 4)

    def body(x_ref, g_ref, t_ref, loss_ref, dx_ref, dxb_ref, dg_ref):
        xf = x_ref[...]
        gv = g_ref[...]
        r = lax.rsqrt(jnp.mean(xf * xf, axis=-1, keepdims=True) + EPS)
        xh = xf * r
        err = xh * gv - t_ref[...]
        part = jnp.sum(jnp.sum(err * err, axis=1, keepdims=True), axis=0, keepdims=True) * (0.5 / d)
        dyv = err * (1.0 / d)
        dxh = dyv * gv
        dx = r * (dxh - xh * jnp.mean(dxh * xh, axis=-1, keepdims=True))
        dx_ref[...] = dx
        dxb_ref[...] = dx.astype(BF16)
        dgp = jnp.sum(dyv * xh, axis=0, keepdims=True)

        @pl.when(pl.program_id(0) == 0)
        def _():
            dg_ref[...] = dgp
            loss_ref[...] = jnp.broadcast_to(part, loss_ref.shape)

        @pl.when(pl.program_id(0) > 0)
        def _():
            dg_ref[...] += dgp
            loss_ref[...] += jnp.broadcast_to(part, loss_ref.shape)

    row = pl.BlockSpec((tm, d), lambda i: (i, 0))
    vec = pl.BlockSpec((1, d), lambda i: (0, 0))
    return pl.pallas_call(
        body, name="loss_head", grid=(s // tm,),
        in_specs=[row, vec, row],
        out_specs=[pl.BlockSpec((8, 128), lambda i: (0, 0)), row, row, vec],
        out_shape=[jax.ShapeDtypeStruct((8, 128), F32), jax.ShapeDtypeStruct((s, d), F32),
                   jax.ShapeDtypeStruct((s, d), BF16), jax.ShapeDtypeStruct((1, d), F32)],
        compiler_params=_params(32),
    )(h2, g, target)


def _mm(name, operands, in_specs, out_shape, out_specs, grid, dims, epilogue, nk=1, acc_shape=None,
        vmem_mib=56, aliases=None):
    n_in = len(operands)
    n_out = len(out_shape)

    def body(*refs):
        a_ref, b_ref = refs[0], refs[1]
        extras = refs[2:n_in]
        outs = refs[n_in:n_in + n_out]

        def prod():
            return _dot(a_ref[...], b_ref[...], dims)

        if nk == 1:
            epilogue(prod(), extras, outs)
        else:
            acc = refs[n_in + n_out]
            k = pl.program_id(2)

            @pl.when(k == 0)
            def _():
                acc[...] = prod()

            @pl.when(k > 0)
            def _():
                acc[...] += prod()

            @pl.when(k == nk - 1)
            def _():
                epilogue(acc[...], extras, outs)

    scratch = [] if nk == 1 else [pltpu.VMEM(acc_shape, F32)]
    return pl.pallas_call(
        body, name=name, grid=grid, in_specs=in_specs, out_specs=out_specs, out_shape=out_shape,
        scratch_shapes=scratch, input_output_aliases=aliases or {},
        compiler_params=_params(vmem_mib),
    )(*operands)


def _store(acc, extras, outs):
    outs[0][...] = acc.astype(outs[0].dtype)


def _proj_in(xn, w_in_g, first_unit, n_units, n_slots):
    s, d = xn.shape
    u = d // 4
    per = N_SLOTS

    return _mm(
        "proj_in_%d" % first_unit, [xn, w_in_g],
        [pl.BlockSpec((s, d), lambda i, j, k: (0, 0)),
         pl.BlockSpec((None, d, u), lambda i, j, k: ((j + first_unit) // per, 0, (j + first_unit) % per))],
        [jax.ShapeDtypeStruct((n_slots, s, d), F32)],
        [pl.BlockSpec((None, s, u), lambda i, j, k: (j // 4, 0, j % 4))],
        (1, n_units, 1), NN, _store)[0]


def _mm_nn(name, a, b, extras, extra_specs, out_shape, out_specs, epilogue, tn, aliases=None):
    s, kdim = a.shape
    n = b.shape[1]
    return _mm(
        name, [a, b] + list(extras),
        [pl.BlockSpec((s, kdim), lambda i, j, k: (0, 0)), pl.BlockSpec((kdim, tn), lambda i, j, k: (0, j))]
        + list(extra_specs),
        out_shape, out_specs, (1, n // tn, 1), NN, epilogue, aliases=aliases)


def _mm_nt(name, a, b, extras, extra_specs, out_shape, out_specs, epilogue, tn, aliases=None):
    s, kdim = a.shape
    n = b.shape[0]
    return _mm(
        name, [a, b] + list(extras),
        [pl.BlockSpec((s, kdim), lambda i, j, k: (0, 0)), pl.BlockSpec((tn, kdim), lambda i, j, k: (j, 0))]
        + list(extra_specs),
        out_shape, out_specs, (1, n // tn, 1), NT, epilogue, aliases=aliases)


def _mm_tn(name, a, b, a_spec, b_spec, out_shape, out_spec, grid, m, tn, s, aliases=None, extra=None):
    ch = 256
    n_in = 2 if extra is None else 3

    def body(*refs):
        a_ref, b_ref = refs[0], refs[1]
        o_ref, at_ref = refs[n_in], refs[n_in + 1]

        @pl.when(pl.program_id(1) == 0)
        def _():
            for c0 in range(0, s, ch):
                at_ref[:, c0:c0 + ch] = a_ref[c0:c0 + ch, :].astype(F32).T.astype(BF16)

        o_ref[...] = _dot(at_ref[...], b_ref[...], NN).astype(o_ref.dtype)

    operands = [a, b] + ([] if extra is None else [extra])
    in_specs = [a_spec, b_spec] + ([] if extra is None else [pl.BlockSpec(memory_space=pl.ANY)])
    return pl.pallas_call(
        body, name=name, grid=grid, in_specs=in_specs, out_specs=out_spec, out_shape=out_shape,
        scratch_shapes=[pltpu.VMEM((m, s), BF16)], input_output_aliases=aliases or {},
        compiler_params=_params(56),
    )(*operands)


def _dxn(dproj_a, dproj_b, w_in_g, tn):
    n_a, s, d = dproj_a.shape
    u = d // 4
    ua = 4 * n_a
    nk = 4 * N_SLOTS
    per = N_SLOTS

    def body(a_ref, b_ref, w_ref, o_ref):
        k = pl.program_id(2)

        @pl.when(k == 0)
        def _():
            o_ref[...] = jnp.zeros_like(o_ref)

        @pl.when(k < ua)
        def _():
            o_ref[...] += _dot(a_ref[...], w_ref[...], NT)

        @pl.when(k >= ua)
        def _():
            o_ref[...] += _dot(b_ref[...], w_ref[...], NT)

    def a_map(i, j, k):
        kk = jnp.minimum(k, ua - 1)
        return (kk // 4, 0, kk % 4)

    def b_map(i, j, k):
        kk = jnp.maximum(k - ua, 0)
        return (kk // 4, 0, kk % 4)

    return pl.pallas_call(
        body, name="dxn", grid=(1, d // tn, nk),
        in_specs=[pl.BlockSpec((None, s, u), a_map), pl.BlockSpec((None, s, u), b_map),
                  pl.BlockSpec((None, tn, u), lambda i, j, k: (k // per, j, k % per))],
        out_specs=pl.BlockSpec((s, tn), lambda i, j, k: (0, j)),
        out_shape=jax.ShapeDtypeStruct((s, d), F32),
        compiler_params=_params(48),
    )(dproj_a, dproj_b, w_in_g)


def _attn_masks(nblk, slope, dil):
    nkeys = 2 * ATTN_BLK if nblk > 1 else ATTN_BLK
    ii = lax.broadcasted_iota(jnp.int32, (ATTN_BLK, nkeys), 0)
    jj = lax.broadcasted_iota(jnp.int32, (ATTN_BLK, nkeys), 1)
    diff = (ATTN_BLK + ii - jj) if nblk > 1 else (ii - jj)
    band = (diff >= 0) & (diff <= ATTN_BLK)
    bias = -(slope * float(dil)) * diff.astype(F32)
    return band, bias, jj


def _deinterleave(dst_ref, src_ref, dil, s):
    seg = s // dil
    if dil == 1:
        dst_ref[...] = src_ref[...].astype(dst_ref.dtype)
    else:
        for r in range(dil):
            dst_ref[r * seg:(r + 1) * seg, :] = src_ref[pl.ds(r, seg, stride=dil), :].astype(dst_ref.dtype)


def _attn_fwd(proj_a, slopes):
    _, s, d = proj_a.shape
    heads = d // HEAD_DIM
    scale = HEAD_DIM ** -0.5
    n_t = s // ATTN_BLK
    ng = len(DILATIONS)

    def body(q_ref, k_ref, v_ref, sl_ref, o_ref, lse_ref, qd, kd, vd, od, ld, og, lg):
        slope = sl_ref[...][:, :1]
        for g, dil in enumerate(DILATIONS):
            nblk = s // dil // ATTN_BLK
            _deinterleave(qd, q_ref, dil, s)
            _deinterleave(kd, k_ref, dil, s)
            _deinterleave(vd, v_ref, dil, s)
            band, bias, jj = _attn_masks(nblk, slope, dil)

            def blk(t, carry, nblk=nblk, band=band, bias=bias, jj=jj):
                cur = pl.multiple_of(t * ATTN_BLK, ATTN_BLK)
                q = qd[pl.ds(cur, ATTN_BLK), :]
                if nblk > 1:
                    prev = pl.multiple_of(jnp.maximum(t - 1, 0) * ATTN_BLK, ATTN_BLK)
                    kk = jnp.concatenate([kd[pl.ds(prev, ATTN_BLK), :], kd[pl.ds(cur, ATTN_BLK), :]], axis=0)
                    vv = jnp.concatenate([vd[pl.ds(prev, ATTN_BLK), :], vd[pl.ds(cur, ATTN_BLK), :]], axis=0)
                    valid = band & (jj >= jnp.where(t % nblk == 0, ATTN_BLK, 0))
                else:
                    kk = kd[pl.ds(cur, ATTN_BLK), :]
                    vv = vd[pl.ds(cur, ATTN_BLK), :]
                    valid = band
                sc = _dot(q, kk, NT) * scale + bias
                sc = jnp.where(valid, sc, -jnp.inf)
                m = jnp.max(sc, axis=1, keepdims=True)
                p = jnp.exp(sc - m)
                l = jnp.sum(p, axis=1, keepdims=True)
                od[pl.ds(cur, ATTN_BLK), :] = _dot(p, vv, NN) / l
                ld[pl.ds(cur, ATTN_BLK), :] = jnp.broadcast_to(m + jnp.log(l), (ATTN_BLK, HEAD_DIM))
                return carry

            lax.fori_loop(0, n_t, blk, 0, unroll=ATTN_UNROLL)
            seg = s // dil
            if dil == 1:
                og[g] = od[...]
                lg[g] = ld[...]
            else:
                for r in range(dil):
                    og[g, pl.ds(r, seg, stride=dil), :] = od[r * seg:(r + 1) * seg, :]
                    lg[g, pl.ds(r, seg, stride=dil), :] = ld[r * seg:(r + 1) * seg, :]

        ch = 256

        def combine(c, carry):
            rows = pl.ds(pl.multiple_of(c * ch, ch), ch)
            ls = [lg[g, rows, :] for g in range(ng)]
            mx = functools.reduce(jnp.maximum, ls)
            es = [jnp.exp(x - mx) for x in ls]
            den = functools.reduce(jnp.add, es)
            num = functools.reduce(jnp.add, [es[g] * og[g, rows, :] for g in range(ng)])
            o_ref[rows, :] = (num / den).astype(o_ref.dtype)
            lse_ref[rows, :] = mx + jnp.log(den)
            return carry

        lax.fori_loop(0, s // ch, combine, 0)

    def col(slot):
        return pl.BlockSpec((None, s, HEAD_DIM), lambda h: (slot, 0, h))

    head = pl.BlockSpec((s, HEAD_DIM), lambda h: (0, h))
    return pl.pallas_call(
        body, name="attn_fwd", grid=(heads,),
        in_specs=[col(0), col(1), col(2), pl.BlockSpec((None, 1, HEAD_DIM), lambda h: (h, 0, 0))],
        out_specs=[head, head],
        out_shape=[jax.ShapeDtypeStruct((s, d), BF16), jax.ShapeDtypeStruct((s, d), F32)],
        scratch_shapes=[pltpu.VMEM((s, HEAD_DIM), BF16)] * 3 + [pltpu.VMEM((s, HEAD_DIM), F32)] * 2
        + [pltpu.VMEM((ng, s, HEAD_DIM), F32)] * 2,
        compiler_params=_params(40),
    )(proj_a, proj_a, proj_a, slopes)


def _attn_bwd(proj_a, slopes, y_attn, lse, dy):
    _, s, d = proj_a.shape
    heads = d // HEAD_DIM
    scale = HEAD_DIM ** -0.5
    n_t = s // ATTN_BLK

    def body(q_ref, k_ref, v_ref, sl_ref, o_ref, lse_ref, dy_ref, out_ref,
             qd, kd, vd, dod, lsd, dld, delta, dqd, dkd, dvd, dqa, dka, dva):
        slope = sl_ref[...][:, :1]
        dyv = dy_ref[...]
        delta[...] = jnp.broadcast_to(
            jnp.sum(dyv * o_ref[...].astype(F32), axis=1, keepdims=True), (s, HEAD_DIM))
        for g, dil in enumerate(DILATIONS):
            nblk = s // dil // ATTN_BLK
            seg = s // dil
            _deinterleave(qd, q_ref, dil, s)
            _deinterleave(kd, k_ref, dil, s)
            _deinterleave(vd, v_ref, dil, s)
            _deinterleave(dod, dy_ref, dil, s)
            _deinterleave(lsd, lse_ref, dil, s)
            _deinterleave(dld, delta, dil, s)
            dkd[...] = jnp.zeros_like(dkd)
            dvd[...] = jnp.zeros_like(dvd)
            band, bias, jj = _attn_masks(nblk, slope, dil)

            def blk(t, carry, nblk=nblk, band=band, bias=bias, jj=jj):
                cur = pl.multiple_of(t * ATTN_BLK, ATTN_BLK)
                rows = pl.ds(cur, ATTN_BLK)
                q = qd[rows, :]
                do = dod[rows, :]
                lse_b = lsd[rows, :]
                dl_b = dld[rows, :]
                if nblk > 1:
                    prev = pl.multiple_of(jnp.maximum(t - 1, 0) * ATTN_BLK, ATTN_BLK)
                    prows = pl.ds(prev, ATTN_BLK)
                    kk = jnp.concatenate([kd[prows, :], kd[rows, :]], axis=0)
                    vv = jnp.concatenate([vd[prows, :], vd[rows, :]], axis=0)
                    valid = band & (jj >= jnp.where(t % nblk == 0, ATTN_BLK, 0))
                    lse_b = jnp.concatenate([lse_b, lse_b], axis=1)
                    dl_b = jnp.concatenate([dl_b, dl_b], axis=1)
                else:
                    kk = kd[rows, :]
                    vv = vd[rows, :]
                    valid = band
                sc = _dot(q, kk, NT) * scale + bias
                p = jnp.where(valid, jnp.exp(sc - lse_b), 0.0)
                dp = _dot(do, vv, NT)
                ds = p * (dp - dl_b)
                dv_b = _dot(p, do, TN)
                dk_b = _dot(ds, q, TN) * scale
                dqd[rows, :] = _dot(ds, kk, NN) * scale
                if nblk > 1:
                    dkd[prows, :] += dk_b[:ATTN_BLK]
                    dvd[prows, :] += dv_b[:ATTN_BLK]
                    dkd[rows, :] += dk_b[ATTN_BLK:]
                    dvd[rows, :] += dv_b[ATTN_BLK:]
                else:
                    dkd[rows, :] += dk_b
                    dvd[rows, :] += dv_b
                return carry

            lax.fori_loop(0, n_t, blk, 0, unroll=ATTN_UNROLL)
            for acc, part in ((dqa, dqd), (dka, dkd), (dva, dvd)):
                if dil == 1:
                    acc[...] = part[...]
                else:
                    for r in range(dil):
                        acc[pl.ds(r, seg, stride=dil), :] += part[r * seg:(r + 1) * seg, :]
        out_ref[0] = dqa[...].astype(out_ref.dtype)
        out_ref[1] = dka[...].astype(out_ref.dtype)
        out_ref[2] = dva[...].astype(out_ref.dtype)

    def col(slot):
        return pl.BlockSpec((None, s, HEAD_DIM), lambda h: (slot, 0, h))

    head = pl.BlockSpec((s, HEAD_DIM), lambda h: (0, h))
    return pl.pallas_call(
        body, name="attn_bwd", grid=(heads,),
        in_specs=[col(0), col(1), col(2), pl.BlockSpec((None, 1, HEAD_DIM), lambda h: (h, 0, 0)),
                  head, head, head],
        out_specs=pl.BlockSpec((N_QKV, s, HEAD_DIM), lambda h: (0, 0, h)),
        out_shape=jax.ShapeDtypeStruct((N_QKV, s, d), BF16),
        scratch_shapes=[pltpu.VMEM((s, HEAD_DIM), BF16)] * 4 + [pltpu.VMEM((s, HEAD_DIM), F32)] * 9,
        compiler_params=_params(48),
    )(proj_a, proj_a, proj_a, slopes, y_attn, lse, dy)


def _expm1(x):
    small = x * (1.0 + x * (0.5 + x * (1.0 / 6.0 + x * (1.0 / 24.0 + x * (1.0 / 120.0)))))
    return jnp.where(jnp.abs(x) < 0.1, small, jnp.exp(x) - 1.0)


def _softplus(x):
    return jnp.maximum(x, 0.0) + jnp.log1p(jnp.exp(-jnp.abs(x)))


GELU_K = 0.7978845608028654
GELU_C = 0.044715


def _gelu(x):
    t = jnp.tanh(GELU_K * (x + GELU_C * x * x * x))
    return 0.5 * x * (1.0 + t), t


def _gelu_grad(x, t):
    return 0.5 * (1.0 + t) + 0.5 * x * (1.0 - t * t) * GELU_K * (1.0 + 3.0 * GELU_C * x * x)


def _lru_gates(xc, wa, ba, wx, bx, sp):
    r = _sigmoid(_dot(xc, wa, NN) + ba)
    ig = _sigmoid(_dot(xc, wx, NN) + bx)
    log_a = -LRU_C * r * sp
    a = jnp.exp(log_a)
    mult = jnp.sqrt(-_expm1(2.0 * log_a))
    return r, ig, a, mult


def _scan_fwd(a, u, tt):
    row = lax.broadcasted_iota(jnp.int32, a.shape, 0)
    sh = 1
    while sh < tt:
        keep = row >= sh
        a_s = jnp.where(keep, pltpu.roll(a, sh, 0), 1.0)
        u_s = jnp.where(keep, pltpu.roll(u, sh, 0), 0.0)
        u = a * u_s + u
        a = a * a_s
        sh *= 2
    return a, u


def _scan_bwd(b, g, tt):
    row = lax.broadcasted_iota(jnp.int32, b.shape, 0)
    sh = 1
    while sh < tt:
        keep = row < tt - sh
        b_s = jnp.where(keep, pltpu.roll(b, tt - sh, 0), 1.0)
        g_s = jnp.where(keep, pltpu.roll(g, tt - sh, 0), 0.0)
        g = g + b * g_s
        b = b * b_s
        sh *= 2
    return b, g


def _conv_rows(xpad_ref, cw, cb, s):
    acc = cb
    for j in range(CONV_TAPS):
        off = 8 - (CONV_TAPS - 1) + j
        acc = acc + cw[j:j + 1, :] * xpad_ref[off:off + s, :]
    return acc


LRU_TILE = 128


def _lru_specs(s, d):
    heads = d // HEAD_DIM

    def col(slot):
        return pl.BlockSpec((None, s, HEAD_DIM), lambda h: (slot, 0, h))

    vec = pl.BlockSpec((1, HEAD_DIM), lambda h: (0, h))
    mat = pl.BlockSpec((None, HEAD_DIM, HEAD_DIM), lambda h: (h, 0, 0))
    cw = pl.BlockSpec((8, HEAD_DIM), lambda h: (0, h))
    head = pl.BlockSpec((s, HEAD_DIM), lambda h: (0, h))
    return heads, col, vec, mat, cw, head


def _lru_fwd(proj_b, conv_w, conv_b, wa, ba, wx, bx, lam):
    _, s, d = proj_b.shape
    heads, col, vec, mat, cws, head = _lru_specs(s, d)
    tt = LRU_TILE

    def body(xr_ref, xg_ref, cw_ref, cb_ref, wa_ref, ba_ref, wx_ref, bx_ref, lam_ref, y_ref, h_ref, xpad, xc_s):
        xpad[0:8, :] = jnp.zeros((8, HEAD_DIM), F32)
        xpad[8:8 + s, :] = xr_ref[...]
        xc_s[...] = _conv_rows(xpad, cw_ref[...], cb_ref[...], s)
        sp = _softplus(-lam_ref[...])
        wav, wxv, bav, bxv = wa_ref[...], wx_ref[...], ba_ref[...], bx_ref[...]

        def tile(i, hc):
            rows = pl.ds(pl.multiple_of(i * tt, tt), tt)
            xc = xc_s[rows, :]
            _, ig, a, mult = _lru_gates(xc, wav, bav, wxv, bxv, sp)
            pa, hl = _scan_fwd(a, mult * (ig * xc), tt)
            h = hl + pa * hc
            h_ref[rows, :] = h
            gel, _ = _gelu(xg_ref[rows, :])
            y_ref[rows, :] = (h * gel).astype(y_ref.dtype)
            return h[tt - 1:tt, :]

        lax.fori_loop(0, s // tt, tile, jnp.zeros((1, HEAD_DIM), F32))

    return pl.pallas_call(
        body, name="lru_fwd", grid=(heads,),
        in_specs=[col(0), col(1), cws, vec, mat, vec, mat, vec, vec],
        out_specs=[head, head],
        out_shape=[jax.ShapeDtypeStruct((s, d), BF16), jax.ShapeDtypeStruct((s, d), F32)],
        scratch_shapes=[pltpu.VMEM((s + 8, HEAD_DIM), F32), pltpu.VMEM((s, HEAD_DIM), F32)],
        compiler_params=_params(32),
    )(proj_b, proj_b, conv_w, conv_b, wa, ba, wx, bx, lam)


def _lru_bwd(proj_b, h_lru, dy, conv_w, conv_b, wa, ba, wx, bx, lam, dproj_b):
    _, s, d = proj_b.shape
    heads, col, vec, mat, cws, head = _lru_specs(s, d)
    tt = LRU_TILE
    n_t = s // tt

    def body(xr_ref, xg_ref, h_ref, dy_ref, cw_ref, cb_ref, wa_ref, ba_ref, wx_ref, bx_ref, lam_ref, alias_ref,
             out_ref, dcw_ref, dcb_ref, dwa_ref, dba_ref, dwx_ref, dbx_ref, dlam_ref, xpad, xc_s, dxc_s):
        del alias_ref
        xpad[0:8, :] = jnp.zeros((8, HEAD_DIM), F32)
        xpad[8:8 + s, :] = xr_ref[...]
        cwv = cw_ref[...]
        xc_s[...] = _conv_rows(xpad, cwv, cb_ref[...], s)
        dxc_s[s:s + 8, :] = jnp.zeros((8, HEAD_DIM), F32)
        lamv = lam_ref[...]
        sp = _softplus(-lamv)
        wav, wxv, bav, bxv = wa_ref[...], wx_ref[...], ba_ref[...], bx_ref[...]
        dwa_ref[...] = jnp.zeros_like(dwa_ref)
        dwx_ref[...] = jnp.zeros_like(dwx_ref)
        zero = jnp.zeros((1, HEAD_DIM), F32)
        row = lax.broadcasted_iota(jnp.int32, (tt, HEAD_DIM), 0)

        def tile(it, carry):
            dh_next, a_next, dba, dbx, dsp, dcb = carry
            i = n_t - 1 - it
            t0 = pl.multiple_of(i * tt, tt)
            rows = pl.ds(t0, tt)
            xc = xc_s[rows, :]
            r, ig, a, mult = _lru_gates(xc, wav, bav, wxv, bxv, sp)
            h = h_ref[rows, :]
            before = h_ref[pl.ds(pl.multiple_of(jnp.maximum(t0 - 8, 0), 8), 8), :][7:8, :]
            before = before * (i > 0).astype(F32)
            h_prev = jnp.where(row == 0, before, pltpu.roll(h, 1, 0))
            xg = xg_ref[rows, :]
            dyv = dy_ref[rows, :]
            gel, th = _gelu(xg)
            out_ref[1, rows, :] = (dyv * h * _gelu_grad(xg, th)).astype(out_ref.dtype)
            b = jnp.where(row == tt - 1, a_next, pltpu.roll(a, tt - 1, 0))
            pb, z = _scan_bwd(b, dyv * gel, tt)
            dh = z + pb * dh_next
            da = dh * h_prev
            dmult = dh * (ig * xc)
            dig = dh * (mult * xc)
            dla = da * a - dmult * (a * a / mult)
            dzr = dla * (-LRU_C * sp) * (r * (1.0 - r))
            dzx = dig * (ig * (1.0 - ig))
            dxc = dh * (mult * ig) + _dot(dzr, wav, NT) + _dot(dzx, wxv, NT)
            dxc_s[rows, :] = dxc
            dwa_ref[...] += _dot(xc, dzr, TN)
            dwx_ref[...] += _dot(xc, dzx, TN)
            return (dh[0:1, :], a[0:1, :],
                    dba + jnp.sum(dzr, axis=0, keepdims=True),
                    dbx + jnp.sum(dzx, axis=0, keepdims=True),
                    dsp + jnp.sum(dla * (-LRU_C * r), axis=0, keepdims=True),
                    dcb + jnp.sum(dxc, axis=0, keepdims=True))

        _, _, dba, dbx, dsp, dcb = lax.fori_loop(0, n_t, tile, (zero, zero, zero, zero, zero, zero))
        dba_ref[...] = dba
        dbx_ref[...] = dbx
        dcb_ref[...] = dcb
        dlam_ref[...] = -dsp * _sigmoid(-lamv)
        dxr = jnp.zeros((s, HEAD_DIM), F32)
        for j in range(CONV_TAPS):
            back = CONV_TAPS - 1 - j
            off = 8 - back
            dcw_ref[j:j + 1, :] = jnp.sum(dxc_s[0:s, :] * xpad[off:off + s, :], axis=0, keepdims=True)
            dxr = dxr + cwv[j:j + 1, :] * dxc_s[back:back + s, :]
        out_ref[0] = dxr.astype(out_ref.dtype)

    return pl.pallas_call(
        body, name="lru_bwd", grid=(heads,),
        in_specs=[col(0), col(1), head, head, cws, vec, mat, vec, mat, vec, vec, pl.BlockSpec(memory_space=pl.ANY)],
        out_specs=[pl.BlockSpec((2, s, HEAD_DIM), lambda h: (0, 0, h)),
                   pl.BlockSpec((CONV_TAPS, HEAD_DIM), lambda h: (0, h)), vec, mat, vec, mat, vec, vec],
        out_shape=[jax.ShapeDtypeStruct(dproj_b.shape, dproj_b.dtype),
                   jax.ShapeDtypeStruct((CONV_TAPS, d), F32), jax.ShapeDtypeStruct((1, d), F32),
                   jax.ShapeDtypeStruct(wa.shape, F32), jax.ShapeDtypeStruct((1, d), F32),
                   jax.ShapeDtypeStruct(wx.shape, F32), jax.ShapeDtypeStruct((1, d), F32),
                   jax.ShapeDtypeStruct((1, d), F32)],
        scratch_shapes=[pltpu.VMEM((s + 8, HEAD_DIM), F32), pltpu.VMEM((s, HEAD_DIM), F32),
                        pltpu.VMEM((s + 8, HEAD_DIM), F32)],
        input_output_aliases={11: 0},
        compiler_params=_params(32),
    )(proj_b, proj_b, h_lru, dy, conv_w, conv_b, wa, ba, wx, bx, lam, dproj_b)


def _place():
    x, y, c = (lax.axis_index(n) for n in AXES)
    return x, y, c


def _other_chips(x, y):
    return [(1 - x, y), (x, 1 - y), (1 - x, 1 - y)]


HBM = pl.BlockSpec(memory_space=pl.ANY)


def _cast_shard(w, chip_arr, name):
    r, cols = w.shape
    rh = r // 2
    tr = _row_tile(rh, cols * 4)
    nt = rh // tr

    def body(chip_ref, w_ref, o_ref):
        del chip_ref
        o_ref[...] = w_ref[...].astype(BF16)

    return pl.pallas_call(
        body, name=name,
        grid_spec=pltpu.PrefetchScalarGridSpec(
            num_scalar_prefetch=1, grid=(2, nt),
            in_specs=[pl.BlockSpec((tr, cols), lambda h, i, chip_ref: (h * nt + i, 0))],
            out_specs=pl.BlockSpec((None, None, tr, cols), lambda h, i, chip_ref: (chip_ref[0], h, i, 0))),
        out_shape=jax.ShapeDtypeStruct((N_CHIPS, 2, rh, cols), BF16), compiler_params=_params(32),
    )(chip_arr, w)


HBM_SPEC = pl.BlockSpec(memory_space=pltpu.HBM)
SEM_SPEC = pl.BlockSpec(memory_space=pltpu.SEMAPHORE)
EFFECT = pltpu.SideEffectType.DATAFLOW_SIDE_EFFECTING
TOKEN = jax.ShapeDtypeStruct((8, 128), F32)
TOKEN_SPEC = pl.BlockSpec(memory_space=pltpu.VMEM)


def _in_hbm(arrays):
    return [pltpu.with_memory_space_constraint(a, pltpu.HBM) for a in arrays]


def _hbm_like(arrays):
    return [pltpu.HBM(a.shape, a.dtype) for a in arrays]


def _sems(n):
    return pltpu.SemaphoreType.DMA((n,))


def _remote(src, dst, send_sem, recv_sem, to):
    return pltpu.make_async_remote_copy(src_ref=src, dst_ref=dst, send_sem=send_sem, recv_sem=recv_sem,
                                        device_id=to, device_id_type=MESH)


def _gather_start(bufs, groups, after):
    n = len(bufs)
    ng = len(groups)

    def body(*refs):
        ins = refs[:n]
        sems = refs[n + len(after):n + len(after) + 2 * ng]
        x, y, c = _place()
        me = 2 * x + y
        for g, ws in enumerate(groups):
            for i, w in enumerate(ws):
                for j, (cx, cy) in enumerate(_other_chips(x, y)):
                    mine = ins[w].at[me, c]
                    _remote(mine, mine, sems[2 * g].at[3 * i + j], sems[2 * g + 1].at[3 * i + j], (cx, cy, c)).start()

    sem_shapes = []
    for ws in groups:
        sem_shapes += [_sems(3 * len(ws)), _sems(3 * len(ws))]
    res = pl.pallas_call(
        body, name="gather_start", in_specs=[HBM_SPEC] * n + [HBM] * len(after),
        out_specs=[SEM_SPEC] * (2 * ng) + [HBM_SPEC] * n, out_shape=sem_shapes + _hbm_like(bufs),
        input_output_aliases={w: 2 * ng + w for w in range(n)},
        compiler_params=pltpu.CompilerParams(has_side_effects=EFFECT),
    )(*_in_hbm(bufs), *after)
    return [(res[2 * g], res[2 * g + 1]) for g in range(ng)], list(res[2 * ng:])


def _gather_forward(bufs, recv, after, name):
    m = len(bufs)

    def body(*refs):
        ins, recv_in = refs[:m], refs[m]
        fsend, frecv = refs[m + 1 + len(after)], refs[m + 2 + len(after)]
        x, y, c = _place()
        for j, (cx, cy) in enumerate(_other_chips(x, y)):
            for i in range(m):
                landed = ins[i].at[2 * cx + cy, c]
                k = 3 * i + j
                _remote(landed, landed, fsend.at[k], recv_in.at[k], (cx, cy, c)).wait_recv()
                _remote(landed, landed, fsend.at[k], frecv.at[k], (x, y, 1 - c)).start()

    res = pl.pallas_call(
        body, name=name, in_specs=[HBM_SPEC] * m + [SEM_SPEC] + [HBM] * len(after),
        out_specs=[SEM_SPEC, SEM_SPEC] + [HBM_SPEC] * m, out_shape=[_sems(3 * m), _sems(3 * m)] + _hbm_like(bufs),
        input_output_aliases={i: 2 + i for i in range(m)},
        compiler_params=pltpu.CompilerParams(has_side_effects=EFFECT),
    )(*bufs, recv, *after)
    return (res[0], res[1]), list(res[2:])


def _gather_finish(bufs, send, fsend, frecv, name):
    m = len(bufs)

    def body(*refs):
        ins = refs[:m]
        send_in, fsend_in, frecv_in = refs[m:m + 3]
        x, y, c = _place()
        me = 2 * x + y
        for j, (cx, cy) in enumerate(_other_chips(x, y)):
            cj = 2 * cx + cy
            for i in range(m):
                k = 3 * i + j
                mine = ins[i].at[me, c]
                _remote(mine, mine, send_in.at[k], frecv_in.at[k], (cx, cy, c)).wait_send()
                landed = ins[i].at[cj, c]
                _remote(landed, landed, fsend_in.at[k], frecv_in.at[k], (x, y, 1 - c)).wait_send()
                theirs = ins[i].at[cj, 1 - c]
                _remote(theirs, theirs, fsend_in.at[k], frecv_in.at[k], (x, y, 1 - c)).wait_recv()

    return list(pl.pallas_call(
        body, name=name, in_specs=[HBM_SPEC] * m + [SEM_SPEC] * 3,
        out_specs=[HBM_SPEC] * m, out_shape=_hbm_like(bufs),
        input_output_aliases={i: i for i in range(m)},
        compiler_params=pltpu.CompilerParams(has_side_effects=EFFECT),
    )(*bufs, send, fsend, frecv))


def _pair_exchange(grads, name):
    n = len(grads)

    def body(*refs):
        ins, outs = refs[:n], refs[n:2 * n]
        send_sems, recv_sems = refs[2 * n:]
        x, y, c = _place()
        sibling = (x, y, 1 - c)
        cps = []
        for w in range(n):
            for j in range(N_CHIPS):
                cp = pltpu.make_async_remote_copy(
                    src_ref=ins[w].at[j, 1 - c], dst_ref=outs[w].at[j], send_sem=send_sems.at[N_CHIPS * w + j],
                    recv_sem=recv_sems.at[N_CHIPS * w + j], device_id=sibling, device_id_type=MESH)
                cp.start()
                cps.append(cp)
        for cp in cps:
            cp.wait()

    return pl.pallas_call(
        body, name=name, in_specs=[HBM] * n, out_specs=[HBM] * n,
        out_shape=[jax.ShapeDtypeStruct((N_CHIPS,) + a.shape[2:], a.dtype) for a in grads],
        scratch_shapes=[pltpu.SemaphoreType.DMA((N_CHIPS * n,)), pltpu.SemaphoreType.DMA((N_CHIPS * n,))],
    )(*grads)


def _chip_start(sums, name, to_all=()):
    m = len(sums)
    lands = [lax.empty(((N_CHIPS,) if i in to_all else ()) + a.shape, a.dtype) for i, a in enumerate(sums)]

    def body(*refs):
        ins, land_in = refs[:m], refs[m:2 * m]
        send, recv = refs[2 * m], refs[2 * m + 1]
        token = refs[4 * m + 2]
        x, y, c = _place()
        me = 2 * x + y
        for i in sorted(range(m), key=lambda i: i not in to_all):
            for j, (cx, cy) in enumerate(_other_chips(x, y)):
                src = ins[i] if i in to_all else ins[i].at[2 * cx + cy]
                _remote(src, land_in[i].at[me], send.at[3 * i + j], recv.at[3 * i + j], (cx, cy, c)).start()
        token[...] = jnp.zeros_like(token)

    res = pl.pallas_call(
        body, name=name, in_specs=[HBM_SPEC] * (2 * m),
        out_specs=[SEM_SPEC, SEM_SPEC] + [HBM_SPEC] * (2 * m) + [TOKEN_SPEC],
        out_shape=[_sems(3 * m), _sems(3 * m)] + _hbm_like(sums) + _hbm_like(lands) + [TOKEN],
        input_output_aliases={i: 2 + i for i in range(2 * m)},
        compiler_params=pltpu.CompilerParams(has_side_effects=EFFECT),
    )(*_in_hbm(sums), *_in_hbm(lands))
    return (res[0], res[1]), list(res[2:2 + m]), list(res[2 + m:2 + 2 * m]), res[2 + 2 * m]


def _chip_wait(sems, sums, lands, after, name, to_all=()):
    m = len(sums)

    def body(*refs):
        ins, land_in = refs[:m], refs[m:2 * m]
        send_in, recv_in = refs[2 * m], refs[2 * m + 1]
        x, y, c = _place()
        for i in range(m):
            for j, (cx, cy) in enumerate(_other_chips(x, y)):
                cj = 2 * cx + cy
                src = ins[i] if i in to_all else ins[i].at[cj]
                cp = _remote(src, land_in[i].at[cj], send_in.at[3 * i + j], recv_in.at[3 * i + j], (cx, cy, c))
                cp.wait_send()
                cp.wait_recv()

    res = pl.pallas_call(
        body, name=name, in_specs=[HBM_SPEC] * (2 * m) + [SEM_SPEC, SEM_SPEC] + [HBM] * len(after),
        out_specs=[HBM_SPEC] * (2 * m), out_shape=_hbm_like(sums) + _hbm_like(lands),
        input_output_aliases={i: i for i in range(2 * m)},
        compiler_params=pltpu.CompilerParams(has_side_effects=EFFECT),
    )(*sums, *lands, sems[0], sems[1], *after)
    return list(res[:m]), list(res[m:])


def _half_exchange(bufs, name):
    n = len(bufs)
    parts = []
    for w, a in enumerate(bufs):
        parts += [(w, None)] if a.ndim == 3 else [(w, j) for j in range(a.shape[0])]
    np_ = len(parts)

    def body(*refs):
        outs = refs[n:2 * n]
        send_sems, recv_sems = refs[2 * n:]
        x, y, c = _place()
        sibling = (x, y, 1 - c)

        def half(w, j, h):
            return outs[w].at[h] if j is None else outs[w].at[j, h]

        cps = []
        for k, (w, j) in enumerate(parts):
            rc = _remote(half(w, j, c), half(w, j, c), send_sems.at[k], recv_sems.at[k], sibling)
            rc.start()
            cps.append(rc)
        for k, (w, j) in enumerate(parts):
            theirs = half(w, j, 1 - c)
            _remote(theirs, theirs, send_sems.at[k], recv_sems.at[k], sibling).wait_recv()
        for cp in cps:
            cp.wait_send()

    return pl.pallas_call(
        body, name=name, in_specs=[HBM] * n, out_specs=[HBM] * n,
        out_shape=[jax.ShapeDtypeStruct(a.shape, a.dtype) for a in bufs],
        input_output_aliases={w: w for w in range(n)},
        scratch_shapes=[_sems(np_), _sems(np_)],
    )(*bufs)


def _all_gather8(block, name):
    def body(in_ref, out_ref, send_sems, recv_sems, local_sem):
        x, y, c = _place()
        me = 4 * x + 2 * y + c
        mine = pltpu.make_async_copy(in_ref, out_ref.at[me], local_sem)
        mine.start()
        flips = [(fx, fy, fc) for fx in (0, 1) for fy in (0, 1) for fc in (0, 1)][1:]
        cps = []
        for k, (fx, fy, fc) in enumerate(flips):
            cp = pltpu.make_async_remote_copy(
                src_ref=in_ref, dst_ref=out_ref.at[me], send_sem=send_sems.at[k], recv_sem=recv_sems.at[k],
                device_id=(x ^ fx, y ^ fy, c ^ fc), device_id_type=MESH)
            cp.start()
            cps.append(cp)
        for k, (fx, fy, fc) in enumerate(flips):
            px, py, pc = x ^ fx, y ^ fy, c ^ fc
            theirs = out_ref.at[4 * px + 2 * py + pc]
            pltpu.make_async_remote_copy(
                src_ref=theirs, dst_ref=theirs, send_sem=send_sems.at[k], recv_sem=recv_sems.at[k],
                device_id=(px, py, pc), device_id_type=MESH).wait_recv()
        for cp in cps:
            cp.wait_send()
        mine.wait()

    return pl.pallas_call(
        body, name=name, in_specs=[HBM], out_specs=HBM,
        out_shape=jax.ShapeDtypeStruct((N_DEV,) + block.shape, block.dtype),
        scratch_shapes=[pltpu.SemaphoreType.DMA((N_DEV - 1,)), pltpu.SemaphoreType.DMA((N_DEV - 1,)),
                        pltpu.SemaphoreType.DMA],
    )(block)


def _pair_sum(grad, recv, c_arr, name):
    _, _, rh, cols = grad.shape
    tr = _row_tile(rh, cols * 4)

    def body(c_ref, g_ref, r_ref, o_ref):
        del c_ref
        o_ref[...] = (g_ref[...].astype(F32) + r_ref[...].astype(F32)).astype(o_ref.dtype)

    spec = pl.BlockSpec((None, tr, cols), lambda j, i, c_ref: (j, i, 0))
    return pl.pallas_call(
        body, name=name,
        grid_spec=pltpu.PrefetchScalarGridSpec(
            num_scalar_prefetch=1, grid=(N_CHIPS, rh // tr),
            in_specs=[pl.BlockSpec((None, None, tr, cols), lambda j, i, c_ref: (j, c_ref[0], i, 0)), spec],
            out_specs=spec),
        out_shape=jax.ShapeDtypeStruct(recv.shape, grad.dtype), compiler_params=_params(32),
    )(c_arr, grad, recv)


def _sum_by_chip(chip, p_ref, own_ref, o_ref):
    o_ref[...] = jnp.zeros_like(o_ref)
    for k in range(N_CHIPS):
        @pl.when(chip == k)
        def _():
            o_ref[...] += own_ref[...].astype(F32)

        @pl.when(chip != k)
        def _(k=k):
            o_ref[...] += p_ref[k].astype(F32)


def _chip_sum_all(parts, own, place_arr, name):
    _, nj, rh, cols = parts.shape
    tr = _row_tile(rh, cols * 4)

    def body(place_ref, p_ref, own_ref, o_ref):
        _sum_by_chip(place_ref[0], p_ref, own_ref, o_ref)

    return pl.pallas_call(
        body, name=name,
        grid_spec=pltpu.PrefetchScalarGridSpec(
            num_scalar_prefetch=1, grid=(nj, rh // tr),
            in_specs=[pl.BlockSpec((N_CHIPS, None, tr, cols), lambda j, i, place_ref: (0, j, i, 0)),
                      pl.BlockSpec((None, tr, cols), lambda j, i, place_ref: (j, i, 0))],
            out_specs=pl.BlockSpec((None, None, tr, cols), lambda j, i, place_ref: (j, place_ref[1], i, 0))),
        out_shape=jax.ShapeDtypeStruct((nj, 2, rh, cols), F32), compiler_params=_params(32),
    )(place_arr, parts, own)


def _chip_sum(parts, own, place_arr, name):
    _, rh, cols = parts.shape
    tr = _row_tile(rh, cols * 4)

    def body(place_ref, p_ref, own_ref, o_ref):
        _sum_by_chip(place_ref[0], p_ref, own_ref, o_ref)

    return pl.pallas_call(
        body, name=name,
        grid_spec=pltpu.PrefetchScalarGridSpec(
            num_scalar_prefetch=1, grid=(rh // tr,),
            in_specs=[pl.BlockSpec((N_CHIPS, tr, cols), lambda i, place_ref: (0, i, 0)),
                      pl.BlockSpec((None, tr, cols), lambda i, place_ref: (place_ref[0], i, 0))],
            out_specs=pl.BlockSpec((None, tr, cols), lambda i, place_ref: (place_ref[1], i, 0))),
        out_shape=jax.ShapeDtypeStruct((2, rh, cols), F32), compiler_params=_params(32),
    )(place_arr, parts, own)


def _adamw_math(w, g, m, v):
    m = ADAM_B1 * m + (1.0 - ADAM_B1) * g
    v = ADAM_B2 * v + (1.0 - ADAM_B2) * (g * g)
    m_hat = m / (1.0 - ADAM_B1 ** ADAM_STEP)
    v_hat = v / (1.0 - ADAM_B2 ** ADAM_STEP)
    delta = -ADAM_LR * (m_hat / (jnp.sqrt(v_hat) + ADAM_EPS) + ADAM_WD * w)
    return delta, m, v


def _adamw(w, g, m, v, name):
    rows, cols = w.shape
    tr = _row_tile(rows, cols * 4)

    def body(w_ref, g_ref, m_ref, v_ref, d_ref, nm_ref, nv_ref):
        d_ref[...], nm_ref[...], nv_ref[...] = _adamw_math(w_ref[...], g_ref[...], m_ref[...], v_ref[...])

    spec = pl.BlockSpec((tr, cols), lambda i: (i, 0))
    return pl.pallas_call(
        body, name=name, grid=(rows // tr,), in_specs=[spec] * 4, out_specs=[spec] * 3,
        out_shape=[jax.ShapeDtypeStruct(w.shape, F32)] * 3, compiler_params=_params(32),
    )(w, g, m, v)


def _pack_rows(pieces, rows, name):
    cols = pieces[0].shape[1]
    n = len(pieces)

    def body(*refs):
        o_ref = refs[n]
        o_ref[...] = jnp.zeros_like(o_ref)
        at = 0
        for p_ref in refs[:n]:
            r = p_ref.shape[0]
            o_ref[at:at + r, :] = p_ref[...]
            at += r

    return pl.pallas_call(
        body, name=name, out_shape=jax.ShapeDtypeStruct((rows, cols), F32), compiler_params=_params(32),
    )(*pieces)


def kernel(x, norm_mix_g, w_in, conv_w, conv_b, lru_wa, lru_ba, lru_wx, lru_bx, lru_lambda, w_proj_attn, w_proj_lru, w_out, norm_mlp_g, w_up, w_down, norm_final_g, loss_target, m_norm_mix_g, m_w_in, m_conv_w, m_conv_b, m_lru_wa, m_lru_ba, m_lru_wx, m_lru_bx, m_lru_lambda, m_w_proj_attn, m_w_proj_lru, m_w_out, m_norm_mlp_g, m_w_up, m_w_down, m_norm_final_g, v_norm_mix_g, v_w_in, v_conv_w, v_conv_b, v_lru_wa, v_lru_ba, v_lru_wx, v_lru_bx, v_lru_lambda, v_w_proj_attn, v_w_proj_lru, v_w_out, v_norm_mlp_g, v_w_up, v_w_down, v_norm_final_g):
    s, d = x.shape[1], x.shape[2]
    ff = w_up.shape[2] * N_CHIPS
    heads = d // HEAD_DIM
    u = d // 4
    assert s % (max(DILATIONS) * ATTN_BLK) == 0 and d % (4 * HEAD_DIM) == 0 and ff == 4 * d
    xs = x[0]
    target = loss_target[0]
    gf = norm_final_g.reshape(1, d)
    wa, wx = lru_wa[0], lru_wx[0]
    core = lax.axis_index("c").astype(jnp.int32)
    chip = (2 * lax.axis_index("x") + lax.axis_index("y")).astype(jnp.int32)
    cidx = core.reshape(1)
    chip_arr = chip.reshape(1)
    place_arr = jnp.stack([chip, core])
    dev_arr = (2 * chip + core).reshape(1)
    slopes = jnp.broadcast_to(
        (2.0 ** (-8.0 * jnp.arange(1, heads + 1, dtype=F32) / heads))[:, None, None], (heads, 1, HEAD_DIM))

    big = [w_in[0], w_proj_attn[0], w_proj_lru[0], w_out[0], w_up[0], w_down[0]]
    names = ["w_in", "w_proj_attn", "w_proj_lru", "w_out", "w_up", "w_down"]
    cw_pad = jnp.pad(conv_w[0], ((0, 8 - CONV_TAPS), (0, 0)))
    cw_all = _all_gather8(cw_pad, "gather_conv_w")
    conv_w_full = jnp.concatenate([cw_all[2 * j] for j in range(N_CHIPS)], axis=1)
    bufs = [_cast_shard(w, chip_arr, "cast_" + nm) for w, nm in zip(big, names)]
    (sem_a, sem_b), bufs = _gather_start(bufs, [[0], [1, 2, 3, 4, 5]], [cw_all])
    fsem_a, buf_a = _gather_forward(bufs[:1], sem_a[1], [], "gather_forward_w_in")
    buf_a = _gather_finish(buf_a, sem_a[0], fsem_a[0], fsem_a[1], "gather_finish_w_in")
    w_in_g = buf_a[0].reshape(N_CHIPS, d, N_SLOTS * u)

    xn = _rms_fwd(xs, norm_mix_g, "norm_mix")
    proj_a = _proj_in(xn, w_in_g, 0, 4 * N_QKV, N_QKV)
    proj_b = _proj_in(xn, w_in_g, 4 * N_QKV, 4 * (N_SLOTS - N_QKV), N_SLOTS - N_QKV)
    y_attn, lse = _attn_fwd(proj_a, slopes)
    y_lru, h_lru = _lru_fwd(proj_b, conv_w_full, conv_b, wa, lru_ba, wx, lru_bx, lru_lambda)
    fsem_b, buf_b = _gather_forward(bufs[1:], sem_b[1], [y_attn, y_lru], "gather_forward_rest")
    buf_b = _gather_finish(buf_b, sem_b[0], fsem_b[0], fsem_b[1], "gather_finish_rest")
    wpa_g = buf_b[0].reshape(d, d)
    wpl_g = buf_b[1].reshape(d, d)
    wout_g = buf_b[2].reshape(d, d)
    wup_g = buf_b[3].reshape(N_CHIPS, d, d)
    wdown_g = buf_b[4].reshape(ff, d)

    tn = u
    sd_f32 = jax.ShapeDtypeStruct((s, d), F32)
    sd_bf16 = jax.ShapeDtypeStruct((s, d), BF16)
    col = pl.BlockSpec((s, tn), lambda i, j, k: (0, j))

    def slot(n):
        return pl.BlockSpec((None, s, tn), lambda i, j, k: (n, 0, j))

    p_attn = _mm_nn("proj_attn", y_attn, wpa_g, [], [], [sd_f32], [col], _store, tn)[0]

    def merge(acc, extras, outs):
        pa_ref, ga_ref, gl_ref = extras
        merged = _sigmoid(ga_ref[...]) * pa_ref[...] + _sigmoid(gl_ref[...]) * acc
        outs[0][...] = merged.astype(BF16)
        outs[1][...] = acc

    tn2 = max(HEAD_DIM, u // 2)
    col2 = pl.BlockSpec((s, tn2), lambda i, j, k: (0, j))

    def slot2(n):
        return pl.BlockSpec((None, s, tn2), lambda i, j, k: (n, 0, j))

    merged, p_lru = _mm_nn("proj_lru_merge", y_lru, wpl_g, [p_attn, proj_b, proj_b], [col2, slot2(2), slot2(3)],
                           [sd_bf16, sd_f32], [col2, col2], merge, tn2)

    def add_resid(acc, extras, outs):
        outs[0][...] = extras[0][...] + acc

    h1 = _mm_nn("w_out_resid", merged, wout_g, [xs], [col], [sd_f32], [col], add_resid, tn)[0]
    hn = _rms_fwd(h1, norm_mlp_g, "norm_mlp")

    def relu_sq(acc, extras, outs):
        r = jnp.maximum(acc, 0.0)
        outs[0][...] = (r * r).astype(BF16)
        outs[1][...] = r.astype(BF16)

    sf_bf16 = jax.ShapeDtypeStruct((s, ff), BF16)
    hid, relu_up = _mm(
        "w_up_relu2", [hn, wup_g],
        [pl.BlockSpec((s, d), lambda i, j, k: (0, 0)),
         pl.BlockSpec((None, d, tn), lambda i, j, k: (j // 4, 0, j % 4))],
        [sf_bf16, sf_bf16], [col, col], (1, ff // tn, 1), NN, relu_sq)
    h2 = _mm(
        "w_down_resid", [hid, wdown_g, h1],
        [pl.BlockSpec((s, d), lambda i, j, k: (0, k)), pl.BlockSpec((d, tn), lambda i, j, k: (k, j)), col],
        [sd_f32], [col], (1, d // tn, ff // d), NN, add_resid, nk=ff // d, acc_shape=(s, tn))[0]
    loss_part, dh2, dh2_b, d_gf = _loss_head(h2, gf, target)
    loss = lax.psum(loss_part[0, 0], AXES)

    def relu_sq_bwd(acc, extras, outs):
        outs[0][...] = (acc * (2.0 * extras[0][...].astype(F32))).astype(BF16)

    dup = _mm_nt("d_hid", dh2_b, wdown_g, [relu_up], [col], [sf_bf16], [col], relu_sq_bwd, tn)[0]
    tok_d = pl.BlockSpec((s, d), lambda i, j: (0, 0))
    g_wdown = _mm_tn(
        "g_w_down", hid, dh2_b, pl.BlockSpec((s, d), lambda i, j: (0, i)),
        pl.BlockSpec((s, tn), lambda i, j: (0, j)), jax.ShapeDtypeStruct((ff, d), BF16),
        pl.BlockSpec((d, tn), lambda i, j: (i, j)), (ff // d, d // tn), d, tn, s)
    dhn = _mm(
        "d_hn", [dup, wup_g],
        [pl.BlockSpec((s, d), lambda i, j, k: (0, k)), pl.BlockSpec((None, tn, d), lambda i, j, k: (k, j, 0))],
        [sd_f32], [col], (1, d // tn, ff // d), NT, _store, nk=ff // d, acc_shape=(s, tn))[0]
    g_wup = _mm_tn(
        "g_w_up", hn, dup, tok_d, pl.BlockSpec((s, tn), lambda i, j: (0, j)),
        jax.ShapeDtypeStruct((N_CHIPS, d, d), BF16), pl.BlockSpec((None, d, tn), lambda i, j: (j // 4, 0, j % 4)),
        (1, ff // tn), d, tn, s)
    big_m = [m_w_in[0], m_w_proj_attn[0], m_w_proj_lru[0], m_w_out[0], m_w_up[0], m_w_down[0]]
    big_v = [v_w_in[0], v_w_proj_attn[0], v_w_proj_lru[0], v_w_out[0], v_w_up[0], v_w_down[0]]
    big_out = {}

    def reduce_begin(ids, gs, tag, everywhere=None):
        g4 = [g.reshape(N_CHIPS, 2, big[i].shape[0] // 2, big[i].shape[1]) for i, g in zip(ids, gs)]
        tags = [names[i] for i in ids]
        if everywhere is not None:
            g4.append(everywhere.reshape(N_CHIPS, 2, everywhere.shape[0] // (2 * N_CHIPS), everywhere.shape[1]))
            tags.append("small_" + tag)
        from_sibling = _pair_exchange(g4, "pair_exchange_" + tag)
        sums = [_pair_sum(g, r, cidx, "pair_sum_" + t) for t, g, r in zip(tags, g4, from_sibling)]
        to_all = () if everywhere is None else (len(ids),)
        return _chip_start(sums, "chip_start_" + tag, to_all), to_all

    def reduce_end(ids, begun, after, tag):
        (sems, sums, lands, _), to_all = begun
        sums, lands = _chip_wait(sems, sums, lands, after, "chip_wait_" + tag, to_all)
        halves = [_chip_sum(p, own, place_arr, "chip_sum_" + names[i]) for i, p, own in zip(ids, lands, sums)]
        if to_all:
            halves.append(_chip_sum_all(lands[-1], sums[-1], place_arr, "chip_sum_small_" + tag))
        full = _half_exchange(halves, "half_exchange_" + tag)
        last = None
        for i, g in zip(ids, full):
            g = g.reshape(big[i].shape)
            dl, nm_, nv_ = _adamw(big[i], g, big_m[i], big_v[i], "adamw_" + names[i])
            big_out[names[i]] = tuple(a[None] for a in (g, dl, nm_, nv_))
            last = dl
        everywhere = full[-1].reshape(-1, full[-1].shape[-1]) if to_all else None
        return everywhere, last

    def after_token(a, begun):
        return a + begun[0][3][:1, :1]

    red_mlp = reduce_begin([4, 5], [g_wup, g_wdown], "mlp")
    dh1, dh1_b, d_gmlp = _rms_bwd(h1, after_token(norm_mlp_g, red_mlp), dhn, dh2, "norm_mlp_bwd")

    g_wout = _mm_tn(
        "g_w_out", merged, dh1_b, tok_d, pl.BlockSpec((s, tn), lambda i, j: (0, j)),
        jax.ShapeDtypeStruct((d, d), BF16), pl.BlockSpec((d, tn), lambda i, j: (0, j)), (1, d // tn), d, tn, s)

    def merge_bwd(acc, extras, outs):
        pa_ref, pl_ref, ga_ref, gl_ref = extras
        sa, sl = _sigmoid(ga_ref[...]), _sigmoid(gl_ref[...])
        outs[0][...] = (acc * sa).astype(BF16)
        outs[1][...] = (acc * sl).astype(BF16)
        outs[2][0] = (acc * pa_ref[...] * (sa * (1.0 - sa))).astype(BF16)
        outs[2][1] = (acc * pl_ref[...] * (sl * (1.0 - sl))).astype(BF16)

    nb = N_SLOTS - N_QKV
    d_pa, d_pl, dproj_b = _mm_nt(
        "d_merged", dh1_b, wout_g, [p_attn, p_lru, proj_b, proj_b], [col2, col2, slot2(2), slot2(3)],
        [sd_bf16, sd_bf16, jax.ShapeDtypeStruct((nb, s, d), BF16)],
        [col2, col2, pl.BlockSpec((2, s, tn2), lambda i, j, k: (1, 0, j))], merge_bwd, tn2)
    dy_attn = _mm_nt("d_y_attn", d_pa, wpa_g, [], [], [sd_f32], [col], _store, tn)[0]
    dy_lru = _mm_nt("d_y_lru", d_pl, wpl_g, [], [], [sd_f32], [col], _store, tn)[0]
    g_wpa = _mm_tn(
        "g_w_proj_attn", y_attn, d_pa, tok_d, pl.BlockSpec((s, tn), lambda i, j: (0, j)),
        jax.ShapeDtypeStruct((d, d), BF16), pl.BlockSpec((d, tn), lambda i, j: (0, j)), (1, d // tn), d, tn, s)
    g_wpl = _mm_tn(
        "g_w_proj_lru", y_lru, d_pl, tok_d, pl.BlockSpec((s, tn), lambda i, j: (0, j)),
        jax.ShapeDtypeStruct((d, d), BF16), pl.BlockSpec((d, tn), lambda i, j: (0, j)), (1, d // tn), d, tn, s)

    red_proj = reduce_begin([1, 2, 3], [g_wpa, g_wpl, g_wout], "proj")

    dproj_a = _attn_bwd(proj_a, after_token(slopes, red_proj), y_attn, lse, dy_attn)
    dproj_b, d_cw, d_cb, d_wa, d_ba, d_wx, d_bx, d_lam = _lru_bwd(
        proj_b, h_lru, dy_lru, conv_w_full, conv_b, wa, lru_ba, wx, lru_bx, after_token(lru_lambda, red_proj),
        dproj_b)
    dxn = _dxn(dproj_a, dproj_b, w_in_g, 2 * tn)
    per = N_SLOTS
    g_win_shape = jax.ShapeDtypeStruct((N_CHIPS, d, N_SLOTS * u), BF16)

    def g_win_part(name, dproj, first, prev):
        n_units = 4 * dproj.shape[0]
        return _mm_tn(
            name, xn, dproj, tok_d, pl.BlockSpec((None, s, u), lambda i, j: (j // 4, 0, j % 4)),
            g_win_shape, pl.BlockSpec((None, d, u), lambda i, j: ((j + first) // per, 0, (j + first) % per)),
            (1, n_units), d, u, s, aliases=None if prev is None else {2: 0}, extra=prev)

    grad_x, _, d_gmix = _rms_bwd(xs, norm_mix_g, dxn, dh1, "norm_mix_bwd")

    mat_rows = heads * HEAD_DIM * HEAD_DIM // d
    vec_names = ["norm_mix_g", "conv_b", "lru_ba", "lru_bx", "lru_lambda", "norm_mlp_g", "norm_final_g"]

    def pack(wa_, wx_, cw_, vecs, name):
        rows = [wa_.reshape(mat_rows, d), wx_.reshape(mat_rows, d), cw_] + [a.reshape(1, d) for a in vecs]
        n = sum(a.shape[0] for a in rows)
        return _pack_rows(rows, n + (-n % 64), name)

    zero_cw = jnp.zeros((CONV_TAPS, d), F32)
    small_g = pack(d_wa, d_wx, d_cw, [d_gmix, d_cb, d_ba, d_bx, d_lam, d_gmlp, d_gf], "pack_small_g")
    g_win = g_win_part("g_w_in_qkv", dproj_a, 0, None)
    g_win = g_win_part("g_w_in_rest", dproj_b, 4 * N_QKV, g_win)
    red_in = reduce_begin([0], [g_win], "w_in", everywhere=small_g)

    _, done = reduce_end([4, 5], red_mlp, [grad_x, after_token(norm_mix_g, red_in)], "mlp")
    _, done = reduce_end([1, 2, 3], red_proj, [done], "proj")
    small_w = pack(wa, wx, zero_cw, [norm_mix_g, conv_b, lru_ba, lru_bx, lru_lambda, norm_mlp_g, norm_final_g],
                   "pack_small_w")
    small_m = pack(m_lru_wa[0], m_lru_wx[0], zero_cw,
                   [m_norm_mix_g, m_conv_b, m_lru_ba, m_lru_bx, m_lru_lambda, m_norm_mlp_g, m_norm_final_g],
                   "pack_small_m")
    small_v = pack(v_lru_wa[0], v_lru_wx[0], zero_cw,
                   [v_norm_mix_g, v_conv_b, v_lru_ba, v_lru_bx, v_lru_lambda, v_norm_mlp_g, v_norm_final_g],
                   "pack_small_v")
    small_sum, _ = reduce_end([0], red_in, [done, small_w, small_m, small_v], "w_in")
    small = (small_sum,) + tuple(_adamw(small_w, small_sum, small_m, small_v, "adamw_small"))
    g_cw = lax.dynamic_slice(small_sum[2 * mat_rows:2 * mat_rows + CONV_TAPS], (0, chip * u), (CONV_TAPS, u))
    cw_out = (g_cw,) + tuple(_adamw(conv_w[0], g_cw, m_conv_w[0], v_conv_w[0], "adamw_conv_w"))

    def small_leaf(kind, name):
        a = small[kind]
        if name == "lru_wa":
            return a[0:mat_rows].reshape(lru_wa.shape)
        if name == "lru_wx":
            return a[mat_rows:2 * mat_rows].reshape(lru_wx.shape)
        if name == "conv_w":
            return cw_out[kind][None]
        row = a[2 * mat_rows + CONV_TAPS + vec_names.index(name)]
        return row if name == "norm_final_g" else row[None]

    order = ["norm_mix_g", "w_in", "conv_w", "conv_b", "lru_wa", "lru_ba", "lru_wx", "lru_bx", "lru_lambda",
             "w_proj_attn", "w_proj_lru", "w_out", "norm_mlp_g", "w_up", "w_down", "norm_final_g"]
    outs = [loss, grad_x[None]]
    for kind in range(4):
        for name in order:
            outs.append(big_out[name][kind] if name in big_out else small_leaf(kind, name))
    return tuple(outs)
```

```python
import functools

import jax
import jax.numpy as jnp
from jax import lax
from jax.experimental import pallas as pl
from jax.experimental.pallas import tpu as pltpu

F32 = jnp.float32
BF16 = jnp.bfloat16
MESH = pl.DeviceIdType.MESH
AXES = ("x", "y", "c")

N_CHIPS = 4
N_DEV = 8
HEAD_DIM = 128
ATTN_BLK = 128
DILATIONS = (1, 4, 16)
ATTN_UNROLL = 8
CONV_TAPS = 4
LRU_C = 8.0
EPS = 1e-6
N_SLOTS = 7
N_QKV = 3
VMEM_MIB = 2 ** 20
VMEM_V7X = 64 * VMEM_MIB

ADAM_LR = 0.001
ADAM_B1 = 0.9
ADAM_B2 = 0.999
ADAM_EPS = 1e-08
ADAM_WD = 0.01
ADAM_STEP = 10

NN = (((1,), (0,)), ((), ()))
NT = (((1,), (1,)), ((), ()))
TN = (((0,), (0,)), ((), ()))


def _params(vmem_mib=None, **kw):
    limit = None if vmem_mib is None else min(vmem_mib * VMEM_MIB, VMEM_V7X - 8 * VMEM_MIB)
    return pltpu.CompilerParams(vmem_limit_bytes=limit, **kw)


def _row_tile(rows, row_bytes, budget=VMEM_MIB):
    t = rows
    while t % 16 == 0 and t * row_bytes > budget:
        t //= 2
    return t


def _dot(a, b, dims):
    return lax.dot_general(a.astype(BF16), b.astype(BF16), dims, preferred_element_type=F32)


def _sigmoid(x):
    return jax.nn.sigmoid(x)


def _rms_fwd(x, g, name):
    s, d = x.shape
    tm = _row_tile(s, d * 4)

    def body(x_ref, g_ref, o_ref):
        xf = x_ref[...]
        r = lax.rsqrt(jnp.mean(xf * xf, axis=-1, keepdims=True) + EPS)
        o_ref[...] = (xf * r * g_ref[...]).astype(o_ref.dtype)

    return pl.pallas_call(
        body, name=name, grid=(s // tm,),
        in_specs=[pl.BlockSpec((tm, d), lambda i: (i, 0)), pl.BlockSpec((1, d), lambda i: (0, 0))],
        out_specs=pl.BlockSpec((tm, d), lambda i: (i, 0)),
        out_shape=jax.ShapeDtypeStruct((s, d), BF16), compiler_params=_params(32),
    )(x, g)


def _rms_bwd(x, g, dy, resid, name):
    s, d = x.shape
    tm = _row_tile(s, d * 4)

    def body(x_ref, g_ref, dy_ref, res_ref, dx_ref, dxb_ref, dg_ref):
        xf = x_ref[...]
        r = lax.rsqrt(jnp.mean(xf * xf, axis=-1, keepdims=True) + EPS)
        xh = xf * r
        dyv = dy_ref[...]
        dxh = dyv * g_ref[...]
        dx = r * (dxh - xh * jnp.mean(dxh * xh, axis=-1, keepdims=True)) + res_ref[...]
        dx_ref[...] = dx
        dxb_ref[...] = dx.astype(BF16)
        part = jnp.sum(dyv * xh, axis=0, keepdims=True)

        @pl.when(pl.program_id(0) == 0)
        def _():
            dg_ref[...] = part

        @pl.when(pl.program_id(0) > 0)
        def _():
            dg_ref[...] += part

    row = pl.BlockSpec((tm, d), lambda i: (i, 0))
    vec = pl.BlockSpec((1, d), lambda i: (0, 0))
    return pl.pallas_call(
        body, name=name, grid=(s // tm,),
        in_specs=[row, vec, row, row], out_specs=[row, row, vec],
        out_shape=[jax.ShapeDtypeStruct((s, d), F32), jax.ShapeDtypeStruct((s, d), BF16),
                   jax.ShapeDtypeStruct((1, d), F32)],
        compiler_params=_params(32),
    )(x, g, dy, resid)


def _loss_head(h2, g, target):
    s, d = h2.shape
    tm = _row_tile(s, d * 4)

    def body(x_ref, g_ref, t_ref, loss_ref, dx_ref, dxb_ref, dg_ref):
        xf = x_ref[...]
        gv = g_ref[...]
        r = lax.rsqrt(jnp.mean(xf * xf, axis=-1, keepdims=True) + EPS)
        xh = xf * r
        err = xh * gv - t_ref[...]
        part = jnp.sum(jnp.sum(err * err, axis=1, keepdims=True), axis=0, keepdims=True) * (0.5 / d)
        dyv = err * (1.0 / d)
        dxh = dyv * gv
        dx = r * (dxh - xh * jnp.mean(dxh * xh, axis=-1, keepdims=True))
        dx_ref[...] = dx
        dxb_ref[...] = dx.astype(BF16)
        dgp = jnp.sum(dyv * xh, axis=0, keepdims=True)

        @pl.when(pl.program_id(0) == 0)
        def _():
            dg_ref[...] = dgp
            loss_ref[...] = jnp.broadcast_to(part, loss_ref.shape)

        @pl.when(pl.program_id(0) > 0)
        def _():
            dg_ref[...] += dgp
            loss_ref[...] += jnp.broadcast_to(part, loss_ref.shape)

    row = pl.BlockSpec((tm, d), lambda i: (i, 0))
    vec = pl.BlockSpec((1, d), lambda i: (0, 0))
    return pl.pallas_call(
        body, name="loss_head", grid=(s // tm,),
        in_specs=[row, vec, row],
        out_specs=[pl.BlockSpec((8, 128), lambda i: (0, 0)), row, row, vec],
        out_shape=[jax.ShapeDtypeStruct((8, 128), F32), jax.ShapeDtypeStruct((s, d), F32),
                   jax.ShapeDtypeStruct((s, d), BF16), jax.ShapeDtypeStruct((1, d), F32)],
        compiler_params=_params(32),
    )(h2, g, target)


def _mm(name, operands, in_specs, out_shape, out_specs, grid, dims, epilogue, nk=1, acc_shape=None,
        vmem_mib=56, aliases=None):
    n_in = len(operands)
    n_out = len(out_shape)

    def body(*refs):
        a_ref, b_ref = refs[0], refs[1]
        extras = refs[2:n_in]
        outs = refs[n_in:n_in + n_out]

        def prod():
            return _dot(a_ref[...], b_ref[...], dims)

        if nk == 1:
            epilogue(prod(), extras, outs)
        else:
            acc = refs[n_in + n_out]
            k = pl.program_id(2)

            @pl.when(k == 0)
            def _():
                acc[...] = prod()

            @pl.when(k > 0)
            def _():
                acc[...] += prod()

            @pl.when(k == nk - 1)
            def _():
                epilogue(acc[...], extras, outs)

    scratch = [] if nk == 1 else [pltpu.VMEM(acc_shape, F32)]
    return pl.pallas_call(
        body, name=name, grid=grid, in_specs=in_specs, out_specs=out_specs, out_shape=out_shape,
        scratch_shapes=scratch, input_output_aliases=aliases or {},
        compiler_params=_params(vmem_mib),
    )(*operands)


def _store(acc, extras, outs):
    outs[0][...] = acc.astype(outs[0].dtype)


def _proj_in(xn, w_in_g, first_unit, n_units, n_slots):
    s, d = xn.shape
    u = d // 4
    per = N_SLOTS

    return _mm(
        "proj_in_%d" % first_unit, [xn, w_in_g],
        [pl.BlockSpec((s, d), lambda i, j, k: (0, 0)),
         pl.BlockSpec((None, d, u), lambda i, j, k: ((j + first_unit) // per, 0, (j + first_unit) % per))],
        [jax.ShapeDtypeStruct((n_slots, s, d), F32)],
        [pl.BlockSpec((None, s, u), lambda i, j, k: (j // 4, 0, j % 4))],
        (1, n_units, 1), NN, _store)[0]


def _mm_nn(name, a, b, extras, extra_specs, out_shape, out_specs, epilogue, tn, aliases=None):
    s, kdim = a.shape
    n = b.shape[1]
    return _mm(
        name, [a, b] + list(extras),
        [pl.BlockSpec((s, kdim), lambda i, j, k: (0, 0)), pl.BlockSpec((kdim, tn), lambda i, j, k: (0, j))]
        + list(extra_specs),
        out_shape, out_specs, (1, n // tn, 1), NN, epilogue, aliases=aliases)


def _mm_nt(name, a, b, extras, extra_specs, out_shape, out_specs, epilogue, tn, aliases=None):
    s, kdim = a.shape
    n = b.shape[0]
    return _mm(
        name, [a, b] + list(extras),
        [pl.BlockSpec((s, kdim), lambda i, j, k: (0, 0)), pl.BlockSpec((tn, kdim), lambda i, j, k: (j, 0))]
        + list(extra_specs),
        out_shape, out_specs, (1, n // tn, 1), NT, epilogue, aliases=aliases)


def _mm_tn(name, a, b, a_spec, b_spec, out_shape, out_spec, grid, m, tn, s, aliases=None, extra=None):
    ch = 256
    n_in = 2 if extra is None else 3

    def body(*refs):
        a_ref, b_ref = refs[0], refs[1]
        o_ref, at_ref = refs[n_in], refs[n_in + 1]

        @pl.when(pl.program_id(1) == 0)
        def _():
            for c0 in range(0, s, ch):
                at_ref[:, c0:c0 + ch] = a_ref[c0:c0 + ch, :].astype(F32).T.astype(BF16)

        o_ref[...] = _dot(at_ref[...], b_ref[...], NN).astype(o_ref.dtype)

    operands = [a, b] + ([] if extra is None else [extra])
    in_specs = [a_spec, b_spec] + ([] if extra is None else [pl.BlockSpec(memory_space=pl.ANY)])
    return pl.pallas_call(
        body, name=name, grid=grid, in_specs=in_specs, out_specs=out_spec, out_shape=out_shape,
        scratch_shapes=[pltpu.VMEM((m, s), BF16)], input_output_aliases=aliases or {},
        compiler_params=_params(56),
    )(*operands)


def _dxn(dproj_a, dproj_b, w_in_g, tn, after):
    n_a, s, d = dproj_a.shape
    u = d // 4
    ua = 4 * n_a
    nk = 4 * N_SLOTS
    per = N_SLOTS

    def body(a_ref, b_ref, w_ref, *rest):
        o_ref = rest[len(after)]
        k = pl.program_id(2)

        @pl.when(k == 0)
        def _():
            o_ref[...] = jnp.zeros_like(o_ref)

        @pl.when(k < ua)
        def _():
            o_ref[...] += _dot(a_ref[...], w_ref[...], NT)

        @pl.when(k >= ua)
        def _():
            o_ref[...] += _dot(b_ref[...], w_ref[...], NT)

    def a_map(i, j, k):
        kk = jnp.minimum(k, ua - 1)
        return (kk // 4, 0, kk % 4)

    def b_map(i, j, k):
        kk = jnp.maximum(k - ua, 0)
        return (kk // 4, 0, kk % 4)

    return pl.pallas_call(
        body, name="dxn", grid=(1, d // tn, nk),
        in_specs=[pl.BlockSpec((None, s, u), a_map), pl.BlockSpec((None, s, u), b_map),
                  pl.BlockSpec((None, tn, u), lambda i, j, k: (k // per, j, k % per))] + [HBM] * len(after),
        out_specs=pl.BlockSpec((s, tn), lambda i, j, k: (0, j)),
        out_shape=jax.ShapeDtypeStruct((s, d), F32),
        compiler_params=_params(48),
    )(dproj_a, dproj_b, w_in_g, *after)


def _attn_masks(nblk, slope, dil):
    nkeys = 2 * ATTN_BLK if nblk > 1 else ATTN_BLK
    ii = lax.broadcasted_iota(jnp.int32, (ATTN_BLK, nkeys), 0)
    jj = lax.broadcasted_iota(jnp.int32, (ATTN_BLK, nkeys), 1)
    diff = (ATTN_BLK + ii - jj) if nblk > 1 else (ii - jj)
    band = (diff >= 0) & (diff <= ATTN_BLK)
    bias = -(slope * float(dil)) * diff.astype(F32)
    return band, bias, jj


def _streams(pairs, dil, s):
    if dil == 1:
        return [src for _, src in pairs]
    seg = s // dil
    for dst, src in pairs:
        for r in range(dil):
            dst[r * seg:(r + 1) * seg, :] = src[pl.ds(r, seg, stride=dil), :].astype(dst.dtype)
    return [dst for dst, _ in pairs]


def _attn_fwd(proj_a, slopes):
    _, s, d = proj_a.shape
    heads = d // HEAD_DIM
    scale = HEAD_DIM ** -0.5
    n_t = s // ATTN_BLK
    ng = len(DILATIONS)

    def body(q_ref, k_ref, v_ref, sl_ref, o_ref, lse_ref, qd, kd, vd, od, ld, og, lg):
        slope = sl_ref[...][:, :1]
        for g, dil in enumerate(DILATIONS):
            nblk = s // dil // ATTN_BLK
            qs, ks, vs = _streams([(qd, q_ref), (kd, k_ref), (vd, v_ref)], dil, s)
            o_t, l_t = (og.at[g], lg.at[g]) if dil == 1 else (od, ld)
            band, bias, jj = _attn_masks(nblk, slope, dil)

            def blk(t, carry, nblk=nblk, band=band, bias=bias, jj=jj, qs=qs, ks=ks, vs=vs, o_t=o_t, l_t=l_t):
                cur = pl.multiple_of(t * ATTN_BLK, ATTN_BLK)
                q = qs[pl.ds(cur, ATTN_BLK), :]
                if nblk > 1:
                    prev = pl.multiple_of(jnp.maximum(t - 1, 0) * ATTN_BLK, ATTN_BLK)
                    kk = jnp.concatenate([ks[pl.ds(prev, ATTN_BLK), :], ks[pl.ds(cur, ATTN_BLK), :]], axis=0)
                    vv = jnp.concatenate([vs[pl.ds(prev, ATTN_BLK), :], vs[pl.ds(cur, ATTN_BLK), :]], axis=0)
                    valid = band & (jj >= jnp.where(t % nblk == 0, ATTN_BLK, 0))
                else:
                    kk = ks[pl.ds(cur, ATTN_BLK), :]
                    vv = vs[pl.ds(cur, ATTN_BLK), :]
                    valid = band
                sc = _dot(q, kk, NT) * scale + bias
                sc = jnp.where(valid, sc, -jnp.inf)
                m = jnp.max(sc, axis=1, keepdims=True)
                p = jnp.exp(sc - m)
                l = jnp.sum(p, axis=1, keepdims=True)
                o_t[pl.ds(cur, ATTN_BLK), :] = _dot(p, vv, NN) / l
                l_t[pl.ds(cur, ATTN_BLK), :] = jnp.broadcast_to(m + jnp.log(l), (ATTN_BLK, HEAD_DIM))
                return carry

            lax.fori_loop(0, n_t, blk, 0, unroll=ATTN_UNROLL)
            seg = s // dil
            if dil > 1:
                for r in range(dil):
                    og[g, pl.ds(r, seg, stride=dil), :] = od[r * seg:(r + 1) * seg, :]
                    lg[g, pl.ds(r, seg, stride=dil), :] = ld[r * seg:(r + 1) * seg, :]

        ch = 256

        def combine(c, carry):
            rows = pl.ds(pl.multiple_of(c * ch, ch), ch)
            ls = [lg[g, rows, :] for g in range(ng)]
            mx = functools.reduce(jnp.maximum, ls)
            es = [jnp.exp(x - mx) for x in ls]
            den = functools.reduce(jnp.add, es)
            num = functools.reduce(jnp.add, [es[g] * og[g, rows, :] for g in range(ng)])
            o_ref[rows, :] = (num / den).astype(o_ref.dtype)
            lse_ref[rows, :] = mx + jnp.log(den)
            return carry

        lax.fori_loop(0, s // ch, combine, 0)

    def col(slot):
        return pl.BlockSpec((None, s, HEAD_DIM), lambda h: (slot, 0, h))

    head = pl.BlockSpec((s, HEAD_DIM), lambda h: (0, h))
    return pl.pallas_call(
        body, name="attn_fwd", grid=(heads,),
        in_specs=[col(0), col(1), col(2), pl.BlockSpec((None, 1, HEAD_DIM), lambda h: (h, 0, 0))],
        out_specs=[head, head],
        out_shape=[jax.ShapeDtypeStruct((s, d), BF16), jax.ShapeDtypeStruct((s, d), F32)],
        scratch_shapes=[pltpu.VMEM((s, HEAD_DIM), BF16)] * 3 + [pltpu.VMEM((s, HEAD_DIM), F32)] * 2
        + [pltpu.VMEM((ng, s, HEAD_DIM), F32)] * 2,
        compiler_params=_params(40),
    )(proj_a, proj_a, proj_a, slopes)


def _attn_bwd(proj_a, slopes, y_attn, lse, dy):
    _, s, d = proj_a.shape
    heads = d // HEAD_DIM
    scale = HEAD_DIM ** -0.5
    n_t = s // ATTN_BLK

    def body(q_ref, k_ref, v_ref, sl_ref, o_ref, lse_ref, dy_ref, out_ref,
             qd, kd, vd, dod, lsd, dld, delta, dqd, dkd, dvd, dqa, dka, dva):
        slope = sl_ref[...][:, :1]
        dyv = dy_ref[...]
        delta[...] = jnp.broadcast_to(
            jnp.sum(dyv * o_ref[...].astype(F32), axis=1, keepdims=True), (s, HEAD_DIM))
        for g, dil in enumerate(DILATIONS):
            nblk = s // dil // ATTN_BLK
            seg = s // dil
            qs, ks, vs, dos, lss, dls = _streams(
                [(qd, q_ref), (kd, k_ref), (vd, v_ref), (dod, dy_ref), (lsd, lse_ref), (dld, delta)], dil, s)
            dq_t, dk_t, dv_t = (dqa, dka, dva) if dil == 1 else (dqd, dkd, dvd)
            dk_t[...] = jnp.zeros_like(dk_t)
            dv_t[...] = jnp.zeros_like(dv_t)
            band, bias, jj = _attn_masks(nblk, slope, dil)

            def blk(t, carry, nblk=nblk, band=band, bias=bias, jj=jj, qs=qs, ks=ks, vs=vs, dos=dos, lss=lss,
                    dls=dls, dq_t=dq_t, dk_t=dk_t, dv_t=dv_t):
                cur = pl.multiple_of(t * ATTN_BLK, ATTN_BLK)
                rows = pl.ds(cur, ATTN_BLK)
                q = qs[rows, :]
                do = dos[rows, :]
                lse_b = lss[rows, :]
                dl_b = dls[rows, :]
                if nblk > 1:
                    prev = pl.multiple_of(jnp.maximum(t - 1, 0) * ATTN_BLK, ATTN_BLK)
                    prows = pl.ds(prev, ATTN_BLK)
                    kk = jnp.concatenate([ks[prows, :], ks[rows, :]], axis=0)
                    vv = jnp.concatenate([vs[prows, :], vs[rows, :]], axis=0)
                    valid = band & (jj >= jnp.where(t % nblk == 0, ATTN_BLK, 0))
                    lse_b = jnp.concatenate([lse_b, lse_b], axis=1)
                    dl_b = jnp.concatenate([dl_b, dl_b], axis=1)
                else:
                    kk = ks[rows, :]
                    vv = vs[rows, :]
                    valid = band
                sc = _dot(q, kk, NT) * scale + bias
                p = jnp.where(valid, jnp.exp(sc - lse_b), 0.0)
                dp = _dot(do, vv, NT)
                ds = p * (dp - dl_b)
                dv_b = _dot(p, do, TN)
                dk_b = _dot(ds, q, TN) * scale
                dq_t[rows, :] = _dot(ds, kk, NN) * scale
                if nblk > 1:
                    dk_t[prows, :] += dk_b[:ATTN_BLK]
                    dv_t[prows, :] += dv_b[:ATTN_BLK]
                    dk_t[rows, :] += dk_b[ATTN_BLK:]
                    dv_t[rows, :] += dv_b[ATTN_BLK:]
                else:
                    dk_t[rows, :] += dk_b
                    dv_t[rows, :] += dv_b
                return carry

            lax.fori_loop(0, n_t, blk, 0, unroll=ATTN_UNROLL)
            if dil > 1:
                for acc, part in ((dqa, dqd), (dka, dkd), (dva, dvd)):
                    for r in range(dil):
                        acc[pl.ds(r, seg, stride=dil), :] += part[r * seg:(r + 1) * seg, :]
        out_ref[0] = dqa[...].astype(out_ref.dtype)
        out_ref[1] = dka[...].astype(out_ref.dtype)
        out_ref[2] = dva[...].astype(out_ref.dtype)

    def col(slot):
        return pl.BlockSpec((None, s, HEAD_DIM), lambda h: (slot, 0, h))

    head = pl.BlockSpec((s, HEAD_DIM), lambda h: (0, h))
    return pl.pallas_call(
        body, name="attn_bwd", grid=(heads,),
        in_specs=[col(0), col(1), col(2), pl.BlockSpec((None, 1, HEAD_DIM), lambda h: (h, 0, 0)),
                  head, head, head],
        out_specs=pl.BlockSpec((N_QKV, s, HEAD_DIM), lambda h: (0, 0, h)),
        out_shape=jax.ShapeDtypeStruct((N_QKV, s, d), BF16),
        scratch_shapes=[pltpu.VMEM((s, HEAD_DIM), BF16)] * 4 + [pltpu.VMEM((s, HEAD_DIM), F32)] * 9,
        compiler_params=_params(48),
    )(proj_a, proj_a, proj_a, slopes, y_attn, lse, dy)


def _expm1(x):
    small = x * (1.0 + x * (0.5 + x * (1.0 / 6.0 + x * (1.0 / 24.0 + x * (1.0 / 120.0)))))
    return jnp.where(jnp.abs(x) < 0.1, small, jnp.exp(x) - 1.0)


def _softplus(x):
    return jnp.maximum(x, 0.0) + jnp.log1p(jnp.exp(-jnp.abs(x)))


GELU_K = 0.7978845608028654
GELU_C = 0.044715


def _gelu(x):
    t = jnp.tanh(GELU_K * (x + GELU_C * x * x * x))
    return 0.5 * x * (1.0 + t), t


def _gelu_grad(x, t):
    return 0.5 * (1.0 + t) + 0.5 * x * (1.0 - t * t) * GELU_K * (1.0 + 3.0 * GELU_C * x * x)


def _lru_gates(xc, wa, ba, wx, bx, sp):
    r = _sigmoid(_dot(xc, wa, NN) + ba)
    ig = _sigmoid(_dot(xc, wx, NN) + bx)
    log_a = -LRU_C * r * sp
    a = jnp.exp(log_a)
    mult = jnp.sqrt(-_expm1(2.0 * log_a))
    return r, ig, a, mult


def _scan_fwd(a, u, tt):
    row = lax.broadcasted_iota(jnp.int32, a.shape, 0)
    sh = 1
    while sh < tt:
        keep = row >= sh
        a_s = jnp.where(keep, pltpu.roll(a, sh, 0), 1.0)
        u_s = jnp.where(keep, pltpu.roll(u, sh, 0), 0.0)
        u = a * u_s + u
        a = a * a_s
        sh *= 2
    return a, u


def _scan_bwd(b, g, tt):
    row = lax.broadcasted_iota(jnp.int32, b.shape, 0)
    sh = 1
    while sh < tt:
        keep = row < tt - sh
        b_s = jnp.where(keep, pltpu.roll(b, tt - sh, 0), 1.0)
        g_s = jnp.where(keep, pltpu.roll(g, tt - sh, 0), 0.0)
        g = g + b * g_s
        b = b * b_s
        sh *= 2
    return b, g


def _conv_rows(xpad_ref, cw, cb, s):
    acc = cb
    for j in range(CONV_TAPS):
        off = 8 - (CONV_TAPS - 1) + j
        acc = acc + cw[j:j + 1, :] * xpad_ref[off:off + s, :]
    return acc


LRU_TILE = 128


def _lru_specs(s, d):
    heads = d // HEAD_DIM

    def col(slot):
        return pl.BlockSpec((None, s, HEAD_DIM), lambda h: (slot, 0, h))

    vec = pl.BlockSpec((1, HEAD_DIM), lambda h: (0, h))
    mat = pl.BlockSpec((None, HEAD_DIM, HEAD_DIM), lambda h: (h, 0, 0))
    cw = pl.BlockSpec((8, HEAD_DIM), lambda h: (0, h))
    head = pl.BlockSpec((s, HEAD_DIM), lambda h: (0, h))
    return heads, col, vec, mat, cw, head


def _lru_fwd(proj_b, conv_w, conv_b, wa, ba, wx, bx, lam):
    _, s, d = proj_b.shape
    heads, col, vec, mat, cws, head = _lru_specs(s, d)
    tt = LRU_TILE

    def body(xr_ref, xg_ref, cw_ref, cb_ref, wa_ref, ba_ref, wx_ref, bx_ref, lam_ref, y_ref, h_ref, xpad, xc_s):
        xpad[0:8, :] = jnp.zeros((8, HEAD_DIM), F32)
        xpad[8:8 + s, :] = xr_ref[...]
        xc_s[...] = _conv_rows(xpad, cw_ref[...], cb_ref[...], s)
        sp = _softplus(-lam_ref[...])
        wav, wxv, bav, bxv = wa_ref[...], wx_ref[...], ba_ref[...], bx_ref[...]

        def tile(i, hc):
            rows = pl.ds(pl.multiple_of(i * tt, tt), tt)
            xc = xc_s[rows, :]
            _, ig, a, mult = _lru_gates(xc, wav, bav, wxv, bxv, sp)
            pa, hl = _scan_fwd(a, mult * (ig * xc), tt)
            h = hl + pa * hc
            h_ref[rows, :] = h
            gel, _ = _gelu(xg_ref[rows, :])
            y_ref[rows, :] = (h * gel).astype(y_ref.dtype)
            return h[tt - 1:tt, :]

        lax.fori_loop(0, s // tt, tile, jnp.zeros((1, HEAD_DIM), F32))

    return pl.pallas_call(
        body, name="lru_fwd", grid=(heads,),
        in_specs=[col(0), col(1), cws, vec, mat, vec, mat, vec, vec],
        out_specs=[head, head],
        out_shape=[jax.ShapeDtypeStruct((s, d), BF16), jax.ShapeDtypeStruct((s, d), F32)],
        scratch_shapes=[pltpu.VMEM((s + 8, HEAD_DIM), F32), pltpu.VMEM((s, HEAD_DIM), F32)],
        compiler_params=_params(32),
    )(proj_b, proj_b, conv_w, conv_b, wa, ba, wx, bx, lam)


def _lru_bwd(proj_b, h_lru, dy, conv_w, conv_b, wa, ba, wx, bx, lam, dproj_b):
    _, s, d = proj_b.shape
    heads, col, vec, mat, cws, head = _lru_specs(s, d)
    tt = LRU_TILE
    n_t = s // tt

    def body(xr_ref, xg_ref, h_ref, dy_ref, cw_ref, cb_ref, wa_ref, ba_ref, wx_ref, bx_ref, lam_ref, alias_ref,
             out_ref, dcw_ref, dcb_ref, dwa_ref, dba_ref, dwx_ref, dbx_ref, dlam_ref, xpad, xc_s, dxc_s):
        del alias_ref
        xpad[0:8, :] = jnp.zeros((8, HEAD_DIM), F32)
        xpad[8:8 + s, :] = xr_ref[...]
        cwv = cw_ref[...]
        xc_s[...] = _conv_rows(xpad, cwv, cb_ref[...], s)
        dxc_s[s:s + 8, :] = jnp.zeros((8, HEAD_DIM), F32)
        lamv = lam_ref[...]
        sp = _softplus(-lamv)
        wav, wxv, bav, bxv = wa_ref[...], wx_ref[...], ba_ref[...], bx_ref[...]
        dwa_ref[...] = jnp.zeros_like(dwa_ref)
        dwx_ref[...] = jnp.zeros_like(dwx_ref)
        zero = jnp.zeros((1, HEAD_DIM), F32)
        row = lax.broadcasted_iota(jnp.int32, (tt, HEAD_DIM), 0)

        def tile(it, carry):
            dh_next, a_next, dba, dbx, dsp, dcb = carry
            i = n_t - 1 - it
            t0 = pl.multiple_of(i * tt, tt)
            rows = pl.ds(t0, tt)
            xc = xc_s[rows, :]
            r, ig, a, mult = _lru_gates(xc, wav, bav, wxv, bxv, sp)
            h = h_ref[rows, :]
            before = h_ref[pl.ds(pl.multiple_of(jnp.maximum(t0 - 8, 0), 8), 8), :][7:8, :]
            before = before * (i > 0).astype(F32)
            h_prev = jnp.where(row == 0, before, pltpu.roll(h, 1, 0))
            xg = xg_ref[rows, :]
            dyv = dy_ref[rows, :]
            gel, th = _gelu(xg)
            out_ref[1, rows, :] = (dyv * h * _gelu_grad(xg, th)).astype(out_ref.dtype)
            b = jnp.where(row == tt - 1, a_next, pltpu.roll(a, tt - 1, 0))
            pb, z = _scan_bwd(b, dyv * gel, tt)
            dh = z + pb * dh_next
            da = dh * h_prev
            dmult = dh * (ig * xc)
            dig = dh * (mult * xc)
            dla = da * a - dmult * (a * a / mult)
            dzr = dla * (-LRU_C * sp) * (r * (1.0 - r))
            dzx = dig * (ig * (1.0 - ig))
            dxc = dh * (mult * ig) + _dot(dzr, wav, NT) + _dot(dzx, wxv, NT)
            dxc_s[rows, :] = dxc
            dwa_ref[...] += _dot(xc, dzr, TN)
            dwx_ref[...] += _dot(xc, dzx, TN)
            return (dh[0:1, :], a[0:1, :],
                    dba + jnp.sum(dzr, axis=0, keepdims=True),
                    dbx + jnp.sum(dzx, axis=0, keepdims=True),
                    dsp + jnp.sum(dla * (-LRU_C * r), axis=0, keepdims=True),
                    dcb + jnp.sum(dxc, axis=0, keepdims=True))

        _, _, dba, dbx, dsp, dcb = lax.fori_loop(0, n_t, tile, (zero, zero, zero, zero, zero, zero))
        dba_ref[...] = dba
        dbx_ref[...] = dbx
        dcb_ref[...] = dcb
        dlam_ref[...] = -dsp * _sigmoid(-lamv)
        dxr = jnp.zeros((s, HEAD_DIM), F32)
        for j in range(CONV_TAPS):
            back = CONV_TAPS - 1 - j
            off = 8 - back
            dcw_ref[j:j + 1, :] = jnp.sum(dxc_s[0:s, :] * xpad[off:off + s, :], axis=0, keepdims=True)
            dxr = dxr + cwv[j:j + 1, :] * dxc_s[back:back + s, :]
        out_ref[0] = dxr.astype(out_ref.dtype)

    return pl.pallas_call(
        body, name="lru_bwd", grid=(heads,),
        in_specs=[col(0), col(1), head, head, cws, vec, mat, vec, mat, vec, vec, pl.BlockSpec(memory_space=pl.ANY)],
        out_specs=[pl.BlockSpec((2, s, HEAD_DIM), lambda h: (0, 0, h)),
                   pl.BlockSpec((CONV_TAPS, HEAD_DIM), lambda h: (0, h)), vec, mat, vec, mat, vec, vec],
        out_shape=[jax.ShapeDtypeStruct(dproj_b.shape, dproj_b.dtype),
                   jax.ShapeDtypeStruct((CONV_TAPS, d), F32), jax.ShapeDtypeStruct((1, d), F32),
                   jax.ShapeDtypeStruct(wa.shape, F32), jax.ShapeDtypeStruct((1, d), F32),
                   jax.ShapeDtypeStruct(wx.shape, F32), jax.ShapeDtypeStruct((1, d), F32),
                   jax.ShapeDtypeStruct((1, d), F32)],
        scratch_shapes=[pltpu.VMEM((s + 8, HEAD_DIM), F32), pltpu.VMEM((s, HEAD_DIM), F32),
                        pltpu.VMEM((s + 8, HEAD_DIM), F32)],
        input_output_aliases={11: 0},
        compiler_params=_params(32),
    )(proj_b, proj_b, h_lru, dy, conv_w, conv_b, wa, ba, wx, bx, lam, dproj_b)


def _place():
    x, y, c = (lax.axis_index(n) for n in AXES)
    return x, y, c


def _other_chips(x, y):
    return [(1 - x, y), (x, 1 - y), (1 - x, 1 - y)]


HBM = pl.BlockSpec(memory_space=pl.ANY)


def _cast_shard(w, chip_arr, name):
    r, cols = w.shape
    rh = r // 2
    tr = _row_tile(rh, cols * 4)
    nt = rh // tr

    def body(chip_ref, w_ref, o_ref):
        del chip_ref
        o_ref[...] = w_ref[...].astype(BF16)

    return pl.pallas_call(
        body, name=name,
        grid_spec=pltpu.PrefetchScalarGridSpec(
            num_scalar_prefetch=1, grid=(2, nt),
            in_specs=[pl.BlockSpec((tr, cols), lambda h, i, chip_ref: (h * nt + i, 0))],
            out_specs=pl.BlockSpec((None, None, tr, cols), lambda h, i, chip_ref: (chip_ref[0], h, i, 0))),
        out_shape=jax.ShapeDtypeStruct((N_CHIPS, 2, rh, cols), BF16), compiler_params=_params(32),
    )(chip_arr, w)


HBM_SPEC = pl.BlockSpec(memory_space=pltpu.HBM)
SEM_SPEC = pl.BlockSpec(memory_space=pltpu.SEMAPHORE)
EFFECT = pltpu.SideEffectType.DATAFLOW_SIDE_EFFECTING
TOKEN = jax.ShapeDtypeStruct((8, 128), F32)
TOKEN_SPEC = pl.BlockSpec(memory_space=pltpu.VMEM)


def _in_hbm(arrays):
    return [pltpu.with_memory_space_constraint(a, pltpu.HBM) for a in arrays]


def _hbm_like(arrays):
    return [pltpu.HBM(a.shape, a.dtype) for a in arrays]


def _sems(n):
    return pltpu.SemaphoreType.DMA((n,))


def _remote(src, dst, send_sem, recv_sem, to):
    return pltpu.make_async_remote_copy(src_ref=src, dst_ref=dst, send_sem=send_sem, recv_sem=recv_sem,
                                        device_id=to, device_id_type=MESH)


def _gather_start(bufs, groups, after):
    n = len(bufs)
    ng = len(groups)

    def body(*refs):
        ins = refs[:n]
        sems = refs[n + len(after):n + len(after) + 2 * ng]
        x, y, c = _place()
        me = 2 * x + y
        for g, ws in enumerate(groups):
            for i, w in enumerate(ws):
                for j, (cx, cy) in enumerate(_other_chips(x, y)):
                    mine = ins[w].at[me, c]
                    _remote(mine, mine, sems[2 * g].at[3 * i + j], sems[2 * g + 1].at[3 * i + j], (cx, cy, c)).start()

    sem_shapes = []
    for ws in groups:
        sem_shapes += [_sems(3 * len(ws)), _sems(3 * len(ws))]
    res = pl.pallas_call(
        body, name="gather_start", in_specs=[HBM_SPEC] * n + [HBM] * len(after),
        out_specs=[SEM_SPEC] * (2 * ng) + [HBM_SPEC] * n, out_shape=sem_shapes + _hbm_like(bufs),
        input_output_aliases={w: 2 * ng + w for w in range(n)},
        compiler_params=pltpu.CompilerParams(has_side_effects=EFFECT),
    )(*_in_hbm(bufs), *after)
    return [(res[2 * g], res[2 * g + 1]) for g in range(ng)], list(res[2 * ng:])


def _gather_forward(bufs, recv, after, name):
    m = len(bufs)

    def body(*refs):
        ins, recv_in = refs[:m], refs[m]
        fsend, frecv = refs[m + 1 + len(after)], refs[m + 2 + len(after)]
        x, y, c = _place()
        for j, (cx, cy) in enumerate(_other_chips(x, y)):
            for i in range(m):
                landed = ins[i].at[2 * cx + cy, c]
                k = 3 * i + j
                _remote(landed, landed, fsend.at[k], recv_in.at[k], (cx, cy, c)).wait_recv()
                _remote(landed, landed, fsend.at[k], frecv.at[k], (x, y, 1 - c)).start()

    res = pl.pallas_call(
        body, name=name, in_specs=[HBM_SPEC] * m + [SEM_SPEC] + [HBM] * len(after),
        out_specs=[SEM_SPEC, SEM_SPEC] + [HBM_SPEC] * m, out_shape=[_sems(3 * m), _sems(3 * m)] + _hbm_like(bufs),
        input_output_aliases={i: 2 + i for i in range(m)},
        compiler_params=pltpu.CompilerParams(has_side_effects=EFFECT),
    )(*bufs, recv, *after)
    return (res[0], res[1]), list(res[2:])


def _gather_finish(bufs, send, fsend, frecv, name):
    m = len(bufs)

    def body(*refs):
        ins = refs[:m]
        send_in, fsend_in, frecv_in = refs[m:m + 3]
        x, y, c = _place()
        me = 2 * x + y
        for j, (cx, cy) in enumerate(_other_chips(x, y)):
            cj = 2 * cx + cy
            for i in range(m):
                k = 3 * i + j
                mine = ins[i].at[me, c]
                _remote(mine, mine, send_in.at[k], frecv_in.at[k], (cx, cy, c)).wait_send()
                landed = ins[i].at[cj, c]
                _remote(landed, landed, fsend_in.at[k], frecv_in.at[k], (x, y, 1 - c)).wait_send()
                theirs = ins[i].at[cj, 1 - c]
                _remote(theirs, theirs, fsend_in.at[k], frecv_in.at[k], (x, y, 1 - c)).wait_recv()

    return list(pl.pallas_call(
        body, name=name, in_specs=[HBM_SPEC] * m + [SEM_SPEC] * 3,
        out_specs=[HBM_SPEC] * m, out_shape=_hbm_like(bufs),
        input_output_aliases={i: i for i in range(m)},
        compiler_params=pltpu.CompilerParams(has_side_effects=EFFECT),
    )(*bufs, send, fsend, frecv))


def _pair_exchange(grads, name):
    n = len(grads)

    def body(*refs):
        ins, outs = refs[:n], refs[n:2 * n]
        send_sems, recv_sems = refs[2 * n:]
        x, y, c = _place()
        sibling = (x, y, 1 - c)
        cps = []
        for w in range(n):
            for j in range(N_CHIPS):
                cp = pltpu.make_async_remote_copy(
                    src_ref=ins[w].at[j, 1 - c], dst_ref=outs[w].at[j], send_sem=send_sems.at[N_CHIPS * w + j],
                    recv_sem=recv_sems.at[N_CHIPS * w + j], device_id=sibling, device_id_type=MESH)
                cp.start()
                cps.append(cp)
        for cp in cps:
            cp.wait()

    return pl.pallas_call(
        body, name=name, in_specs=[HBM] * n, out_specs=[HBM] * n,
        out_shape=[jax.ShapeDtypeStruct((N_CHIPS,) + a.shape[2:], a.dtype) for a in grads],
        scratch_shapes=[pltpu.SemaphoreType.DMA((N_CHIPS * n,)), pltpu.SemaphoreType.DMA((N_CHIPS * n,))],
    )(*grads)


def _chip_start(sums, name, to_all=()):
    m = len(sums)
    lands = [lax.empty(((N_CHIPS,) if i in to_all else ()) + a.shape, a.dtype) for i, a in enumerate(sums)]

    def body(*refs):
        ins, land_in = refs[:m], refs[m:2 * m]
        send, recv = refs[2 * m], refs[2 * m + 1]
        token = refs[4 * m + 2]
        x, y, c = _place()
        me = 2 * x + y
        for i in sorted(range(m), key=lambda i: i not in to_all):
            for j, (cx, cy) in enumerate(_other_chips(x, y)):
                src = ins[i] if i in to_all else ins[i].at[2 * cx + cy]
                _remote(src, land_in[i].at[me], send.at[3 * i + j], recv.at[3 * i + j], (cx, cy, c)).start()
        token[...] = jnp.zeros_like(token)

    res = pl.pallas_call(
        body, name=name, in_specs=[HBM_SPEC] * (2 * m),
        out_specs=[SEM_SPEC, SEM_SPEC] + [HBM_SPEC] * (2 * m) + [TOKEN_SPEC],
        out_shape=[_sems(3 * m), _sems(3 * m)] + _hbm_like(sums) + _hbm_like(lands) + [TOKEN],
        input_output_aliases={i: 2 + i for i in range(2 * m)},
        compiler_params=pltpu.CompilerParams(has_side_effects=EFFECT),
    )(*_in_hbm(sums), *_in_hbm(lands))
    return (res[0], res[1]), list(res[2:2 + m]), list(res[2 + m:2 + 2 * m]), res[2 + 2 * m]


def _chip_wait(sems, sums, lands, after, name, to_all=()):
    m = len(sums)

    def body(*refs):
        ins, land_in = refs[:m], refs[m:2 * m]
        send_in, recv_in = refs[2 * m], refs[2 * m + 1]
        x, y, c = _place()
        for i in range(m):
            for j, (cx, cy) in enumerate(_other_chips(x, y)):
                cj = 2 * cx + cy
                src = ins[i] if i in to_all else ins[i].at[cj]
                cp = _remote(src, land_in[i].at[cj], send_in.at[3 * i + j], recv_in.at[3 * i + j], (cx, cy, c))
                cp.wait_send()
                cp.wait_recv()

    res = pl.pallas_call(
        body, name=name, in_specs=[HBM_SPEC] * (2 * m) + [SEM_SPEC, SEM_SPEC] + [HBM] * len(after),
        out_specs=[HBM_SPEC] * (2 * m), out_shape=_hbm_like(sums) + _hbm_like(lands),
        input_output_aliases={i: i for i in range(2 * m)},
        compiler_params=pltpu.CompilerParams(has_side_effects=EFFECT),
    )(*sums, *lands, sems[0], sems[1], *after)
    return list(res[:m]), list(res[m:])


def _half_exchange(bufs, name):
    n = len(bufs)
    parts = []
    for w, a in enumerate(bufs):
        parts += [(w, None)] if a.ndim == 3 else [(w, j) for j in range(a.shape[0])]
    np_ = len(parts)

    def body(*refs):
        outs = refs[n:2 * n]
        send_sems, recv_sems = refs[2 * n:]
        x, y, c = _place()
        sibling = (x, y, 1 - c)

        def half(w, j, h):
            return outs[w].at[h] if j is None else outs[w].at[j, h]

        cps = []
        for k, (w, j) in enumerate(parts):
            rc = _remote(half(w, j, c), half(w, j, c), send_sems.at[k], recv_sems.at[k], sibling)
            rc.start()
            cps.append(rc)
        for k, (w, j) in enumerate(parts):
            theirs = half(w, j, 1 - c)
            _remote(theirs, theirs, send_sems.at[k], recv_sems.at[k], sibling).wait_recv()
        for cp in cps:
            cp.wait_send()

    return pl.pallas_call(
        body, name=name, in_specs=[HBM] * n, out_specs=[HBM] * n,
        out_shape=[jax.ShapeDtypeStruct(a.shape, a.dtype) for a in bufs],
        input_output_aliases={w: w for w in range(n)},
        scratch_shapes=[_sems(np_), _sems(np_)],
    )(*bufs)


def _all_gather8(block, name, after=()):
    def body(in_ref, *rest):
        out_ref, send_sems, recv_sems, local_sem = rest[len(after):]
        x, y, c = _place()
        me = 4 * x + 2 * y + c
        mine = pltpu.make_async_copy(in_ref, out_ref.at[me], local_sem)
        mine.start()
        flips = [(fx, fy, fc) for fx in (0, 1) for fy in (0, 1) for fc in (0, 1)][1:]
        cps = []
        for k, (fx, fy, fc) in enumerate(flips):
            cp = pltpu.make_async_remote_copy(
                src_ref=in_ref, dst_ref=out_ref.at[me], send_sem=send_sems.at[k], recv_sem=recv_sems.at[k],
                device_id=(x ^ fx, y ^ fy, c ^ fc), device_id_type=MESH)
            cp.start()
            cps.append(cp)
        for k, (fx, fy, fc) in enumerate(flips):
            px, py, pc = x ^ fx, y ^ fy, c ^ fc
            theirs = out_ref.at[4 * px + 2 * py + pc]
            pltpu.make_async_remote_copy(
                src_ref=theirs, dst_ref=theirs, send_sem=send_sems.at[k], recv_sem=recv_sems.at[k],
                device_id=(px, py, pc), device_id_type=MESH).wait_recv()
        for cp in cps:
            cp.wait_send()
        mine.wait()

    return pl.pallas_call(
        body, name=name, in_specs=[HBM] * (1 + len(after)), out_specs=HBM,
        out_shape=jax.ShapeDtypeStruct((N_DEV,) + block.shape, block.dtype),
        scratch_shapes=[pltpu.SemaphoreType.DMA((N_DEV - 1,)), pltpu.SemaphoreType.DMA((N_DEV - 1,)),
                        pltpu.SemaphoreType.DMA],
    )(block, *after)


def _pair_sum(grad, recv, c_arr, name):
    _, _, rh, cols = grad.shape
    tr = _row_tile(rh, cols * 4)

    def body(c_ref, g_ref, r_ref, o_ref):
        del c_ref
        o_ref[...] = (g_ref[...].astype(F32) + r_ref[...].astype(F32)).astype(o_ref.dtype)

    spec = pl.BlockSpec((None, tr, cols), lambda j, i, c_ref: (j, i, 0))
    return pl.pallas_call(
        body, name=name,
        grid_spec=pltpu.PrefetchScalarGridSpec(
            num_scalar_prefetch=1, grid=(N_CHIPS, rh // tr),
            in_specs=[pl.BlockSpec((None, None, tr, cols), lambda j, i, c_ref: (j, c_ref[0], i, 0)), spec],
            out_specs=spec),
        out_shape=jax.ShapeDtypeStruct(recv.shape, grad.dtype), compiler_params=_params(32),
    )(c_arr, grad, recv)


def _sum_by_chip(chip, p_ref, own_ref, o_ref):
    o_ref[...] = jnp.zeros_like(o_ref)
    for k in range(N_CHIPS):
        @pl.when(chip == k)
        def _():
            o_ref[...] += own_ref[...].astype(F32)

        @pl.when(chip != k)
        def _(k=k):
            o_ref[...] += p_ref[k].astype(F32)


def _chip_sum_all(parts, own, place_arr, name):
    _, nj, rh, cols = parts.shape
    tr = _row_tile(rh, cols * 4)

    def body(place_ref, p_ref, own_ref, o_ref):
        _sum_by_chip(place_ref[0], p_ref, own_ref, o_ref)

    return pl.pallas_call(
        body, name=name,
        grid_spec=pltpu.PrefetchScalarGridSpec(
            num_scalar_prefetch=1, grid=(nj, rh // tr),
            in_specs=[pl.BlockSpec((N_CHIPS, None, tr, cols), lambda j, i, place_ref: (0, j, i, 0)),
                      pl.BlockSpec((None, tr, cols), lambda j, i, place_ref: (j, i, 0))],
            out_specs=pl.BlockSpec((None, None, tr, cols), lambda j, i, place_ref: (j, place_ref[1], i, 0))),
        out_shape=jax.ShapeDtypeStruct((nj, 2, rh, cols), F32), compiler_params=_params(32),
    )(place_arr, parts, own)


def _chip_sum(parts, own, place_arr, name):
    _, rh, cols = parts.shape
    tr = _row_tile(rh, cols * 4)

    def body(place_ref, p_ref, own_ref, o_ref):
        _sum_by_chip(place_ref[0], p_ref, own_ref, o_ref)

    return pl.pallas_call(
        body, name=name,
        grid_spec=pltpu.PrefetchScalarGridSpec(
            num_scalar_prefetch=1, grid=(rh // tr,),
            in_specs=[pl.BlockSpec((N_CHIPS, tr, cols), lambda i, place_ref: (0, i, 0)),
                      pl.BlockSpec((None, tr, cols), lambda i, place_ref: (place_ref[0], i, 0))],
            out_specs=pl.BlockSpec((None, tr, cols), lambda i, place_ref: (place_ref[1], i, 0))),
        out_shape=jax.ShapeDtypeStruct((2, rh, cols), F32), compiler_params=_params(32),
    )(place_arr, parts, own)


def _adamw_math(w, g, m, v):
    m = ADAM_B1 * m + (1.0 - ADAM_B1) * g
    v = ADAM_B2 * v + (1.0 - ADAM_B2) * (g * g)
    m_hat = m / (1.0 - ADAM_B1 ** ADAM_STEP)
    v_hat = v / (1.0 - ADAM_B2 ** ADAM_STEP)
    delta = -ADAM_LR * (m_hat / (jnp.sqrt(v_hat) + ADAM_EPS) + ADAM_WD * w)
    return delta, m, v


def _adamw(w, g, m, v, name):
    rows, cols = w.shape
    tr = _row_tile(rows, cols * 4)

    def body(w_ref, g_ref, m_ref, v_ref, d_ref, nm_ref, nv_ref):
        d_ref[...], nm_ref[...], nv_ref[...] = _adamw_math(w_ref[...], g_ref[...], m_ref[...], v_ref[...])

    spec = pl.BlockSpec((tr, cols), lambda i: (i, 0))
    return pl.pallas_call(
        body, name=name, grid=(rows // tr,), in_specs=[spec] * 4, out_specs=[spec] * 3,
        out_shape=[jax.ShapeDtypeStruct(w.shape, F32)] * 3, compiler_params=_params(32),
    )(w, g, m, v)


def _sum8_adamw_row(parts, w, m, v, name):
    cols = parts.shape[2]

    def body(p_ref, w_ref, m_ref, v_ref, g_ref, d_ref, nm_ref, nv_ref):
        g = p_ref[0, 0:1, :]
        for k in range(1, N_DEV):
            g = g + p_ref[k, 0:1, :]
        g_ref[...] = g
        d_ref[...], nm_ref[...], nv_ref[...] = _adamw_math(w_ref[...], g, m_ref[...], v_ref[...])

    return pl.pallas_call(
        body, name=name, out_shape=[jax.ShapeDtypeStruct((1, cols), F32)] * 4, compiler_params=_params(32),
    )(parts, w, m, v)


def _pack_rows(pieces, rows, name):
    cols = pieces[0].shape[1]
    n = len(pieces)

    def body(*refs):
        o_ref = refs[n]
        o_ref[...] = jnp.zeros_like(o_ref)
        at = 0
        for p_ref in refs[:n]:
            r = p_ref.shape[0]
            o_ref[at:at + r, :] = p_ref[...]
            at += r

    return pl.pallas_call(
        body, name=name, out_shape=jax.ShapeDtypeStruct((rows, cols), F32), compiler_params=_params(32),
    )(*pieces)


def kernel(x, norm_mix_g, w_in, conv_w, conv_b, lru_wa, lru_ba, lru_wx, lru_bx, lru_lambda, w_proj_attn, w_proj_lru, w_out, norm_mlp_g, w_up, w_down, norm_final_g, loss_target, m_norm_mix_g, m_w_in, m_conv_w, m_conv_b, m_lru_wa, m_lru_ba, m_lru_wx, m_lru_bx, m_lru_lambda, m_w_proj_attn, m_w_proj_lru, m_w_out, m_norm_mlp_g, m_w_up, m_w_down, m_norm_final_g, v_norm_mix_g, v_w_in, v_conv_w, v_conv_b, v_lru_wa, v_lru_ba, v_lru_wx, v_lru_bx, v_lru_lambda, v_w_proj_attn, v_w_proj_lru, v_w_out, v_norm_mlp_g, v_w_up, v_w_down, v_norm_final_g):
    s, d = x.shape[1], x.shape[2]
    ff = w_up.shape[2] * N_CHIPS
    heads = d // HEAD_DIM
    u = d // 4
    assert s % (max(DILATIONS) * ATTN_BLK) == 0 and d % (4 * HEAD_DIM) == 0 and ff == 4 * d and DILATIONS[0] == 1
    xs, target = _in_hbm([x[0], loss_target[0]])
    gf = norm_final_g.reshape(1, d)
    wa, wx = lru_wa[0], lru_wx[0]
    core = lax.axis_index("c").astype(jnp.int32)
    chip = (2 * lax.axis_index("x") + lax.axis_index("y")).astype(jnp.int32)
    cidx = core.reshape(1)
    chip_arr = chip.reshape(1)
    place_arr = jnp.stack([chip, core])
    slopes = jnp.broadcast_to(
        (2.0 ** (-8.0 * jnp.arange(1, heads + 1, dtype=F32) / heads))[:, None, None], (heads, 1, HEAD_DIM))

    big = _in_hbm([w_in[0], w_proj_attn[0], w_proj_lru[0], w_out[0], w_up[0], w_down[0]])
    names = ["w_in", "w_proj_attn", "w_proj_lru", "w_out", "w_up", "w_down"]
    cw_pad = jnp.pad(conv_w[0], ((0, 8 - CONV_TAPS), (0, 0)))
    cw_all = _all_gather8(cw_pad, "gather_conv_w")
    conv_w_full = jnp.concatenate([cw_all[2 * j] for j in range(N_CHIPS)], axis=1)
    bufs = [_cast_shard(w, chip_arr, "cast_" + nm) for w, nm in zip(big, names)]
    (sem_a, sem_b), bufs = _gather_start(bufs, [[0], [1, 2, 3, 4, 5]], [cw_all])
    fsem_a, buf_a = _gather_forward(bufs[:1], sem_a[1], [], "gather_forward_w_in")
    buf_a = _gather_finish(buf_a, sem_a[0], fsem_a[0], fsem_a[1], "gather_finish_w_in")
    w_in_g = buf_a[0].reshape(N_CHIPS, d, N_SLOTS * u)

    xn = _rms_fwd(xs, norm_mix_g, "norm_mix")
    proj_a = _proj_in(xn, w_in_g, 0, 4 * N_QKV, N_QKV)
    proj_b = _proj_in(xn, w_in_g, 4 * N_QKV, 4 * (N_SLOTS - N_QKV), N_SLOTS - N_QKV)
    y_attn, lse = _attn_fwd(proj_a, slopes)
    y_lru, h_lru = _lru_fwd(proj_b, conv_w_full, conv_b, wa, lru_ba, wx, lru_bx, lru_lambda)
    fsem_b, buf_b = _gather_forward(bufs[1:], sem_b[1], [y_attn, y_lru], "gather_forward_rest")
    buf_b = _gather_finish(buf_b, sem_b[0], fsem_b[0], fsem_b[1], "gather_finish_rest")
    wpa_g = buf_b[0].reshape(d, d)
    wpl_g = buf_b[1].reshape(d, d)
    wout_g = buf_b[2].reshape(d, d)
    wup_g = buf_b[3].reshape(N_CHIPS, d, d)
    wdown_g = buf_b[4].reshape(ff, d)

    tn = u
    sd_f32 = jax.ShapeDtypeStruct((s, d), F32)
    sd_bf16 = jax.ShapeDtypeStruct((s, d), BF16)
    col = pl.BlockSpec((s, tn), lambda i, j, k: (0, j))

    def slot(n):
        return pl.BlockSpec((None, s, tn), lambda i, j, k: (n, 0, j))

    p_attn = _mm_nn("proj_attn", y_attn, wpa_g, [], [], [sd_f32], [col], _store, tn)[0]

    def merge(acc, extras, outs):
        pa_ref, ga_ref, gl_ref = extras
        merged = _sigmoid(ga_ref[...]) * pa_ref[...] + _sigmoid(gl_ref[...]) * acc
        outs[0][...] = merged.astype(BF16)
        outs[1][...] = acc

    tn2 = max(HEAD_DIM, u // 2)
    col2 = pl.BlockSpec((s, tn2), lambda i, j, k: (0, j))

    def slot2(n):
        return pl.BlockSpec((None, s, tn2), lambda i, j, k: (n, 0, j))

    merged, p_lru = _mm_nn("proj_lru_merge", y_lru, wpl_g, [p_attn, proj_b, proj_b], [col2, slot2(2), slot2(3)],
                           [sd_bf16, sd_f32], [col2, col2], merge, tn2)

    def add_resid(acc, extras, outs):
        outs[0][...] = extras[0][...] + acc

    h1 = _mm_nn("w_out_resid", merged, wout_g, [xs], [col], [sd_f32], [col], add_resid, tn)[0]
    hn = _rms_fwd(h1, norm_mlp_g, "norm_mlp")

    def relu_sq(acc, extras, outs):
        r = jnp.maximum(acc, 0.0)
        outs[0][...] = (r * r).astype(BF16)
        outs[1][...] = r.astype(BF16)

    sf_bf16 = jax.ShapeDtypeStruct((s, ff), BF16)
    hid, relu_up = _mm(
        "w_up_relu2", [hn, wup_g],
        [pl.BlockSpec((s, d), lambda i, j, k: (0, 0)),
         pl.BlockSpec((None, d, tn), lambda i, j, k: (j // 4, 0, j % 4))],
        [sf_bf16, sf_bf16], [col, col], (1, ff // tn, 1), NN, relu_sq)
    h2 = _mm(
        "w_down_resid", [hid, wdown_g, h1],
        [pl.BlockSpec((s, d), lambda i, j, k: (0, k)), pl.BlockSpec((d, tn), lambda i, j, k: (k, j)), col],
        [sd_f32], [col], (1, d // tn, ff // d), NN, add_resid, nk=ff // d, acc_shape=(s, tn))[0]
    loss_part, dh2, dh2_b, d_gf = _loss_head(h2, gf, target)
    loss = lax.psum(loss_part[0, 0], AXES)

    def relu_sq_bwd(acc, extras, outs):
        outs[0][...] = (acc * (2.0 * extras[0][...].astype(F32))).astype(BF16)

    dup = _mm_nt("d_hid", dh2_b, wdown_g, [relu_up], [col], [sf_bf16], [col], relu_sq_bwd, tn)[0]
    tok_d = pl.BlockSpec((s, d), lambda i, j: (0, 0))
    g_wdown = _mm_tn(
        "g_w_down", hid, dh2_b, pl.BlockSpec((s, d), lambda i, j: (0, i)),
        pl.BlockSpec((s, tn), lambda i, j: (0, j)), jax.ShapeDtypeStruct((ff, d), BF16),
        pl.BlockSpec((d, tn), lambda i, j: (i, j)), (ff // d, d // tn), d, tn, s)
    dhn = _mm(
        "d_hn", [dup, wup_g],
        [pl.BlockSpec((s, d), lambda i, j, k: (0, k)), pl.BlockSpec((None, tn, d), lambda i, j, k: (k, j, 0))],
        [sd_f32], [col], (1, d // tn, ff // d), NT, _store, nk=ff // d, acc_shape=(s, tn))[0]
    g_wup = _mm_tn(
        "g_w_up", hn, dup, tok_d, pl.BlockSpec((s, tn), lambda i, j: (0, j)),
        jax.ShapeDtypeStruct((N_CHIPS, d, d), BF16), pl.BlockSpec((None, d, tn), lambda i, j: (j // 4, 0, j % 4)),
        (1, ff // tn), d, tn, s)
    big_m = _in_hbm([m_w_in[0], m_w_proj_attn[0], m_w_proj_lru[0], m_w_out[0], m_w_up[0], m_w_down[0]])
    big_v = _in_hbm([v_w_in[0], v_w_proj_attn[0], v_w_proj_lru[0], v_w_out[0], v_w_up[0], v_w_down[0]])
    big_out = {}

    def reduce_begin(ids, gs, tag, everywhere=None):
        g4 = [g.reshape(N_CHIPS, 2, big[i].shape[0] // 2, big[i].shape[1]) for i, g in zip(ids, gs)]
        tags = [names[i] for i in ids]
        if everywhere is not None:
            g4.append(everywhere.reshape(N_CHIPS, 2, everywhere.shape[0] // (2 * N_CHIPS), everywhere.shape[1]))
            tags.append("small_" + tag)
        from_sibling = _pair_exchange(g4, "pair_exchange_" + tag)
        sums = [_pair_sum(g, r, cidx, "pair_sum_" + t) for t, g, r in zip(tags, g4, from_sibling)]
        to_all = () if everywhere is None else (len(ids),)
        return _chip_start(sums, "chip_start_" + tag, to_all), to_all

    def reduce_end(ids, begun, after, tag):
        (sems, sums, lands, _), to_all = begun
        sums, lands = _chip_wait(sems, sums, lands, after, "chip_wait_" + tag, to_all)
        halves = [_chip_sum(p, own, place_arr, "chip_sum_" + names[i]) for i, p, own in zip(ids, lands, sums)]
        if to_all:
            halves.append(_chip_sum_all(lands[-1], sums[-1], place_arr, "chip_sum_small_" + tag))
        full = _half_exchange(halves, "half_exchange_" + tag)
        last = None
        for i, g in zip(ids, full):
            g = g.reshape(big[i].shape)
            dl, nm_, nv_ = _adamw(big[i], g, big_m[i], big_v[i], "adamw_" + names[i])
            big_out[names[i]] = tuple(a[None] for a in (g, dl, nm_, nv_))
            last = dl
        everywhere = full[-1].reshape(-1, full[-1].shape[-1]) if to_all else None
        return everywhere, last

    def after_token(a, begun):
        return a + begun[0][3][:1, :1]

    red_mlp = reduce_begin([4, 5], [g_wup, g_wdown], "mlp")
    dh1, dh1_b, d_gmlp = _rms_bwd(h1, after_token(norm_mlp_g, red_mlp), dhn, dh2, "norm_mlp_bwd")

    g_wout = _mm_tn(
        "g_w_out", merged, dh1_b, tok_d, pl.BlockSpec((s, tn), lambda i, j: (0, j)),
        jax.ShapeDtypeStruct((d, d), BF16), pl.BlockSpec((d, tn), lambda i, j: (0, j)), (1, d // tn), d, tn, s)

    def merge_bwd(acc, extras, outs):
        pa_ref, pl_ref, ga_ref, gl_ref = extras
        sa, sl = _sigmoid(ga_ref[...]), _sigmoid(gl_ref[...])
        outs[0][...] = (acc * sa).astype(BF16)
        outs[1][...] = (acc * sl).astype(BF16)
        outs[2][0] = (acc * pa_ref[...] * (sa * (1.0 - sa))).astype(BF16)
        outs[2][1] = (acc * pl_ref[...] * (sl * (1.0 - sl))).astype(BF16)

    nb = N_SLOTS - N_QKV
    d_pa, d_pl, dproj_b = _mm_nt(
        "d_merged", dh1_b, wout_g, [p_attn, p_lru, proj_b, proj_b], [col2, col2, slot2(2), slot2(3)],
        [sd_bf16, sd_bf16, jax.ShapeDtypeStruct((nb, s, d), BF16)],
        [col2, col2, pl.BlockSpec((2, s, tn2), lambda i, j, k: (1, 0, j))], merge_bwd, tn2)
    dy_attn = _mm_nt("d_y_attn", d_pa, wpa_g, [], [], [sd_f32], [col], _store, tn)[0]
    dy_lru = _mm_nt("d_y_lru", d_pl, wpl_g, [], [], [sd_f32], [col], _store, tn)[0]
    g_wpa = _mm_tn(
        "g_w_proj_attn", y_attn, d_pa, tok_d, pl.BlockSpec((s, tn), lambda i, j: (0, j)),
        jax.ShapeDtypeStruct((d, d), BF16), pl.BlockSpec((d, tn), lambda i, j: (0, j)), (1, d // tn), d, tn, s)
    g_wpl = _mm_tn(
        "g_w_proj_lru", y_lru, d_pl, tok_d, pl.BlockSpec((s, tn), lambda i, j: (0, j)),
        jax.ShapeDtypeStruct((d, d), BF16), pl.BlockSpec((d, tn), lambda i, j: (0, j)), (1, d // tn), d, tn, s)

    red_proj = reduce_begin([1, 2, 3], [g_wpa, g_wpl, g_wout], "proj")

    dproj_a = _attn_bwd(proj_a, after_token(slopes, red_proj), y_attn, lse, dy_attn)
    dproj_b, d_cw, d_cb, d_wa, d_ba, d_wx, d_bx, d_lam = _lru_bwd(
        proj_b, h_lru, dy_lru, conv_w_full, conv_b, wa, lru_ba, wx, lru_bx, after_token(lru_lambda, red_proj),
        dproj_b)
    per = N_SLOTS
    g_win_shape = jax.ShapeDtypeStruct((N_CHIPS, d, N_SLOTS * u), BF16)

    def g_win_part(name, dproj, first, prev):
        n_units = 4 * dproj.shape[0]
        return _mm_tn(
            name, xn, dproj, tok_d, pl.BlockSpec((None, s, u), lambda i, j: (j // 4, 0, j % 4)),
            g_win_shape, pl.BlockSpec((None, d, u), lambda i, j: ((j + first) // per, 0, (j + first) % per)),
            (1, n_units), d, u, s, aliases=None if prev is None else {2: 0}, extra=prev)

    mat_rows = heads * HEAD_DIM * HEAD_DIM // d
    vec_names = ["norm_mix_g", "conv_b", "lru_ba", "lru_bx", "lru_lambda", "norm_mlp_g", "norm_final_g"]

    def pack(wa_, wx_, cw_, vecs, name):
        rows = [wa_.reshape(mat_rows, d), wx_.reshape(mat_rows, d), cw_] + [a.reshape(1, d) for a in vecs]
        n = sum(a.shape[0] for a in rows)
        return _pack_rows(rows, n + (-n % 64), name)

    zero_cw = jnp.zeros((CONV_TAPS, d), F32)
    small_g = pack(d_wa, d_wx, d_cw, [jnp.zeros((1, d), F32), d_cb, d_ba, d_bx, d_lam, d_gmlp, d_gf], "pack_small_g")
    g_win = g_win_part("g_w_in_qkv", dproj_a, 0, None)
    g_win = g_win_part("g_w_in_rest", dproj_b, 4 * N_QKV, g_win)
    red_in = reduce_begin([0], [g_win], "w_in", everywhere=small_g)
    dxn = _dxn(dproj_a, dproj_b, w_in_g, 2 * tn, [red_in[0][3]])
    grad_x, _, d_gmix = _rms_bwd(xs, norm_mix_g, dxn, dh1, "norm_mix_bwd")

    _, done = reduce_end([4, 5], red_mlp, [grad_x], "mlp")
    _, done = reduce_end([1, 2, 3], red_proj, [done], "proj")
    small_w = pack(wa, wx, zero_cw, [norm_mix_g, conv_b, lru_ba, lru_bx, lru_lambda, norm_mlp_g, norm_final_g],
                   "pack_small_w")
    small_m = pack(m_lru_wa[0], m_lru_wx[0], zero_cw,
                   [m_norm_mix_g, m_conv_b, m_lru_ba, m_lru_bx, m_lru_lambda, m_norm_mlp_g, m_norm_final_g],
                   "pack_small_m")
    small_v = pack(v_lru_wa[0], v_lru_wx[0], zero_cw,
                   [v_norm_mix_g, v_conv_b, v_lru_ba, v_lru_bx, v_lru_lambda, v_norm_mlp_g, v_norm_final_g],
                   "pack_small_v")
    small_sum, _ = reduce_end([0], red_in, [done, small_w, small_m, small_v], "w_in")
    small = (small_sum,) + tuple(_adamw(small_w, small_sum, small_m, small_v, "adamw_small"))
    g_cw = lax.dynamic_slice(small_sum[2 * mat_rows:2 * mat_rows + CONV_TAPS], (0, chip * u), (CONV_TAPS, u))
    cw_out = (g_cw,) + tuple(_adamw(conv_w[0], g_cw, m_conv_w[0], v_conv_w[0], "adamw_conv_w"))
    gmix_parts = _all_gather8(jnp.pad(d_gmix, ((0, 7), (0, 0))), "gather_gain_grad", [small[1]])
    gmix_out = _sum8_adamw_row(gmix_parts, norm_mix_g, m_norm_mix_g, v_norm_mix_g, "sum_adamw_norm_mix_g")

    def small_leaf(kind, name):
        a = small[kind]
        if name == "norm_mix_g":
            return gmix_out[kind]
        if name == "lru_wa":
            return a[0:mat_rows].reshape(lru_wa.shape)
        if name == "lru_wx":
            return a[mat_rows:2 * mat_rows].reshape(lru_wx.shape)
        if name == "conv_w":
            return cw_out[kind][None]
        row = a[2 * mat_rows + CONV_TAPS + vec_names.index(name)]
        return row if name == "norm_final_g" else row[None]

    order = ["norm_mix_g", "w_in", "conv_w", "conv_b", "lru_wa", "lru_ba", "lru_wx", "lru_bx", "lru_lambda",
             "w_proj_attn", "w_proj_lru", "w_out", "norm_mlp_g", "w_up", "w_down", "norm_final_g"]
    outs = [loss, grad_x[None]]
    for kind in range(4):
        for name in order:
            outs.append(big_out[name][kind] if name in big_out else small_leaf(kind, name))
    return tuple(outs)
```

```python
import functools

import jax
import jax.numpy as jnp
from jax import lax
from jax.experimental import pallas as pl
from jax.experimental.pallas import tpu as pltpu

F32 = jnp.float32
BF16 = jnp.bfloat16
MESH = pl.DeviceIdType.MESH
AXES = ("x", "y", "c")

N_CHIPS = 4
N_DEV = 8
HEAD_DIM = 128
ATTN_BLK = 128
DILATIONS = (1, 4, 16)
ATTN_UNROLL = 8
CONV_TAPS = 4
LRU_C = 8.0
EPS = 1e-6
N_SLOTS = 7
N_QKV = 3
VMEM_MIB = 2 ** 20
VMEM_V7X = 64 * VMEM_MIB

ADAM_LR = 0.001
ADAM_B1 = 0.9
ADAM_B2 = 0.999
ADAM_EPS = 1e-08
ADAM_WD = 0.01
ADAM_STEP = 10

NN = (((1,), (0,)), ((), ()))
NT = (((1,), (1,)), ((), ()))
TN = (((0,), (0,)), ((), ()))


def _params(vmem_mib=None, **kw):
    limit = None if vmem_mib is None else min(vmem_mib * VMEM_MIB, VMEM_V7X - 8 * VMEM_MIB)
    return pltpu.CompilerParams(vmem_limit_bytes=limit, **kw)


def _row_tile(rows, row_bytes, budget=VMEM_MIB):
    t = rows
    while t % 16 == 0 and t * row_bytes > budget:
        t //= 2
    return t


def _dot(a, b, dims):
    return lax.dot_general(a.astype(BF16), b.astype(BF16), dims, preferred_element_type=F32)


def _sigmoid(x):
    return jax.nn.sigmoid(x)


def _rms_fwd(x, g, name):
    s, d = x.shape
    tm = _row_tile(s, d * 4)

    def body(x_ref, g_ref, o_ref):
        xf = x_ref[...]
        r = lax.rsqrt(jnp.mean(xf * xf, axis=-1, keepdims=True) + EPS)
        o_ref[...] = (xf * r * g_ref[...]).astype(o_ref.dtype)

    return pl.pallas_call(
        body, name=name, grid=(s // tm,),
        in_specs=[pl.BlockSpec((tm, d), lambda i: (i, 0)), pl.BlockSpec((1, d), lambda i: (0, 0))],
        out_specs=pl.BlockSpec((tm, d), lambda i: (i, 0)),
        out_shape=jax.ShapeDtypeStruct((s, d), BF16), compiler_params=_params(32),
    )(x, g)


def _rms_bwd(x, g, dy, resid, name):
    s, d = x.shape
    tm = _row_tile(s, d * 4)

    def body(x_ref, g_ref, dy_ref, res_ref, dx_ref, dxb_ref, dg_ref):
        xf = x_ref[...]
        r = lax.rsqrt(jnp.mean(xf * xf, axis=-1, keepdims=True) + EPS)
        xh = xf * r
        dyv = dy_ref[...]
        dxh = dyv * g_ref[...]
        dx = r * (dxh - xh * jnp.mean(dxh * xh, axis=-1, keepdims=True)) + res_ref[...]
        dx_ref[...] = dx
        dxb_ref[...] = dx.astype(BF16)
        part = jnp.sum(dyv * xh, axis=0, keepdims=True)

        @pl.when(pl.program_id(0) == 0)
        def _():
            dg_ref[...] = part

        @pl.when(pl.program_id(0) > 0)
        def _():
            dg_ref[...] += part

    row = pl.BlockSpec((tm, d), lambda i: (i, 0))
    vec = pl.BlockSpec((1, d), lambda i: (0, 0))
    return pl.pallas_call(
        body, name=name, grid=(s // tm,),
        in_specs=[row, vec, row, row], out_specs=[row, row, vec],
        out_shape=[jax.ShapeDtypeStruct((s, d), F32), jax.ShapeDtypeStruct((s, d), BF16),
                   jax.ShapeDtypeStruct((1, d), F32)],
        compiler_params=_params(32),
    )(x, g, dy, resid)


def _loss_head(h2, g, target):
    s, d = h2.shape
    tm = _row_tile(s, d * 4)

    def body(x_ref, g_ref, t_ref, loss_ref, dx_ref, dxb_ref, dg_ref):
        xf = x_ref[...]
        gv = g_ref[...]
        r = lax.rsqrt(jnp.mean(xf * xf, axis=-1, keepdims=True) + EPS)
        xh = xf * r
        err = xh * gv - t_ref[...]
        part = jnp.sum(jnp.sum(err * err, axis=1, keepdims=True), axis=0, keepdims=True) * (0.5 / d)
        dyv = err * (1.0 / d)
        dxh = dyv * gv
        dx = r * (dxh - xh * jnp.mean(dxh * xh, axis=-1, keepdims=True))
        dx_ref[...] = dx
        dxb_ref[...] = dx.astype(BF16)
        dgp = jnp.sum(dyv * xh, axis=0, keepdims=True)

        @pl.when(pl.program_id(0) == 0)
        def _():
            dg_ref[...] = dgp
            loss_ref[...] = jnp.broadcast_to(part, loss_ref.shape)

        @pl.when(pl.program_id(0) > 0)
        def _():
            dg_ref[...] += dgp
            loss_ref[...] += jnp.broadcast_to(part, loss_ref.shape)

    row = pl.BlockSpec((tm, d), lambda i: (i, 0))
    vec = pl.BlockSpec((1, d), lambda i: (0, 0))
    return pl.pallas_call(
        body, name="loss_head", grid=(s // tm,),
        in_specs=[row, vec, row],
        out_specs=[pl.BlockSpec((8, 128), lambda i: (0, 0)), row, row, vec],
        out_shape=[jax.ShapeDtypeStruct((8, 128), F32), jax.ShapeDtypeStruct((s, d), F32),
                   jax.ShapeDtypeStruct((s, d), BF16), jax.ShapeDtypeStruct((1, d), F32)],
        compiler_params=_params(32),
    )(h2, g, target)


def _mm(name, operands, in_specs, out_shape, out_specs, grid, dims, epilogue, nk=1, acc_shape=None,
        vmem_mib=56, aliases=None):
    n_in = len(operands)
    n_out = len(out_shape)

    def body(*refs):
        a_ref, b_ref = refs[0], refs[1]
        extras = refs[2:n_in]
        outs = refs[n_in:n_in + n_out]

        def prod():
            return _dot(a_ref[...], b_ref[...], dims)

        if nk == 1:
            epilogue(prod(), extras, outs)
        else:
            acc = refs[n_in + n_out]
            k = pl.program_id(2)

            @pl.when(k == 0)
            def _():
                acc[...] = prod()

            @pl.when(k > 0)
            def _():
                acc[...] += prod()

            @pl.when(k == nk - 1)
            def _():
                epilogue(acc[...], extras, outs)

    scratch = [] if nk == 1 else [pltpu.VMEM(acc_shape, F32)]
    return pl.pallas_call(
        body, name=name, grid=grid, in_specs=in_specs, out_specs=out_specs, out_shape=out_shape,
        scratch_shapes=scratch, input_output_aliases=aliases or {},
        compiler_params=_params(vmem_mib),
    )(*operands)


def _store(acc, extras, outs):
    outs[0][...] = acc.astype(outs[0].dtype)


def _proj_in(xn, w_in_g, first_unit, n_units, n_slots):
    s, d = xn.shape
    u = d // 4
    per = N_SLOTS

    return _mm(
        "proj_in_%d" % first_unit, [xn, w_in_g],
        [pl.BlockSpec((s, d), lambda i, j, k: (0, 0)),
         pl.BlockSpec((None, d, u), lambda i, j, k: ((j + first_unit) // per, 0, (j + first_unit) % per))],
        [jax.ShapeDtypeStruct((n_slots, s, d), F32)],
        [pl.BlockSpec((None, s, u), lambda i, j, k: (j // 4, 0, j % 4))],
        (1, n_units, 1), NN, _store)[0]


def _mm_nn(name, a, b, extras, extra_specs, out_shape, out_specs, epilogue, tn, aliases=None):
    s, kdim = a.shape
    n = b.shape[1]
    return _mm(
        name, [a, b] + list(extras),
        [pl.BlockSpec((s, kdim), lambda i, j, k: (0, 0)), pl.BlockSpec((kdim, tn), lambda i, j, k: (0, j))]
        + list(extra_specs),
        out_shape, out_specs, (1, n // tn, 1), NN, epilogue, aliases=aliases)


def _mm_nt(name, a, b, extras, extra_specs, out_shape, out_specs, epilogue, tn, aliases=None):
    s, kdim = a.shape
    n = b.shape[0]
    return _mm(
        name, [a, b] + list(extras),
        [pl.BlockSpec((s, kdim), lambda i, j, k: (0, 0)), pl.BlockSpec((tn, kdim), lambda i, j, k: (j, 0))]
        + list(extra_specs),
        out_shape, out_specs, (1, n // tn, 1), NT, epilogue, aliases=aliases)


def _mm_tn(name, a, b, a_spec, b_spec, out_shape, out_spec, grid, m, tn, s, aliases=None, extra=None):
    ch = 256
    n_in = 2 if extra is None else 3

    def body(*refs):
        a_ref, b_ref = refs[0], refs[1]
        o_ref, at_ref = refs[n_in], refs[n_in + 1]

        @pl.when(pl.program_id(1) == 0)
        def _():
            for c0 in range(0, s, ch):
                at_ref[:, c0:c0 + ch] = a_ref[c0:c0 + ch, :].astype(F32).T.astype(BF16)

        o_ref[...] = _dot(at_ref[...], b_ref[...], NN).astype(o_ref.dtype)

    operands = [a, b] + ([] if extra is None else [extra])
    in_specs = [a_spec, b_spec] + ([] if extra is None else [pl.BlockSpec(memory_space=pl.ANY)])
    return pl.pallas_call(
        body, name=name, grid=grid, in_specs=in_specs, out_specs=out_spec, out_shape=out_shape,
        scratch_shapes=[pltpu.VMEM((m, s), BF16)], input_output_aliases=aliases or {},
        compiler_params=_params(56),
    )(*operands)


def _dxn(dproj_a, dproj_b, w_in_g, tn, after):
    n_a, s, d = dproj_a.shape
    u = d // 4
    ua = 4 * n_a
    nk = 4 * N_SLOTS
    per = N_SLOTS

    def body(a_ref, b_ref, w_ref, *rest):
        o_ref = rest[len(after)]
        k = pl.program_id(2)

        @pl.when(k == 0)
        def _():
            o_ref[...] = jnp.zeros_like(o_ref)

        @pl.when(k < ua)
        def _():
            o_ref[...] += _dot(a_ref[...], w_ref[...], NT)

        @pl.when(k >= ua)
        def _():
            o_ref[...] += _dot(b_ref[...], w_ref[...], NT)

    def a_map(i, j, k):
        kk = jnp.minimum(k, ua - 1)
        return (kk // 4, 0, kk % 4)

    def b_map(i, j, k):
        kk = jnp.maximum(k - ua, 0)
        return (kk // 4, 0, kk % 4)

    return pl.pallas_call(
        body, name="dxn", grid=(1, d // tn, nk),
        in_specs=[pl.BlockSpec((None, s, u), a_map), pl.BlockSpec((None, s, u), b_map),
                  pl.BlockSpec((None, tn, u), lambda i, j, k: (k // per, j, k % per))] + [HBM] * len(after),
        out_specs=pl.BlockSpec((s, tn), lambda i, j, k: (0, j)),
        out_shape=jax.ShapeDtypeStruct((s, d), F32),
        compiler_params=_params(48),
    )(dproj_a, dproj_b, w_in_g, *after)


def _attn_masks(slope, dil):
    ii = lax.broadcasted_iota(jnp.int32, (ATTN_BLK, 2 * ATTN_BLK), 0)
    jj = lax.broadcasted_iota(jnp.int32, (ATTN_BLK, 2 * ATTN_BLK), 1)
    diff = ATTN_BLK + ii - jj
    band = (diff >= 0) & (diff <= ATTN_BLK)
    bias = -(slope * float(dil)) * diff.astype(F32)
    return band, bias, jj


def _attn_window(t, nblk):
    cur = pl.ds(pl.multiple_of(t * ATTN_BLK, ATTN_BLK), ATTN_BLK)
    prev = pl.ds(pl.multiple_of(jnp.maximum(t - 1, 0) * ATTN_BLK, ATTN_BLK), ATTN_BLK)
    first = jnp.where(t % nblk == 0, ATTN_BLK, 0)
    return prev, cur, first


def _streams(pairs, dil, s):
    if dil == 1:
        return [src for _, src in pairs]
    seg = s // dil
    for dst, src in pairs:
        for r in range(dil):
            dst[r * seg:(r + 1) * seg, :] = src[pl.ds(r, seg, stride=dil), :].astype(dst.dtype)
    return [dst for dst, _ in pairs]


def _attn_fwd(proj_a, slopes):
    _, s, d = proj_a.shape
    heads = d // HEAD_DIM
    scale = HEAD_DIM ** -0.5
    n_t = s // ATTN_BLK
    ng = len(DILATIONS)

    def body(q_ref, k_ref, v_ref, sl_ref, o_ref, lse_ref, qd, kd, vd, od, ld, og, lg):
        slope = sl_ref[...][:, :1]
        for g, dil in enumerate(DILATIONS):
            nblk = s // dil // ATTN_BLK
            qs, ks, vs = _streams([(qd, q_ref), (kd, k_ref), (vd, v_ref)], dil, s)
            o_t, l_t = (og.at[g], lg.at[g]) if dil == 1 else (od, ld)
            band, bias, jj = _attn_masks(slope, dil)

            def blk(t, carry, nblk=nblk, band=band, bias=bias, jj=jj, qs=qs, ks=ks, vs=vs, o_t=o_t, l_t=l_t):
                prev, cur, first = _attn_window(t, nblk)
                kk = jnp.concatenate([ks[prev, :], ks[cur, :]], axis=0)
                vv = jnp.concatenate([vs[prev, :], vs[cur, :]], axis=0)
                sc = _dot(qs[cur, :], kk, NT) * scale + bias
                sc = jnp.where(band & (jj >= first), sc, -jnp.inf)
                m = jnp.max(sc, axis=1, keepdims=True)
                p = jnp.exp(sc - m)
                l = jnp.sum(p, axis=1, keepdims=True)
                o_t[cur, :] = _dot(p, vv, NN) / l
                l_t[cur, :] = jnp.broadcast_to(m + jnp.log(l), (ATTN_BLK, HEAD_DIM))
                return carry

            lax.fori_loop(0, n_t, blk, 0, unroll=ATTN_UNROLL)
            seg = s // dil
            if dil > 1:
                for r in range(dil):
                    og[g, pl.ds(r, seg, stride=dil), :] = od[r * seg:(r + 1) * seg, :]
                    lg[g, pl.ds(r, seg, stride=dil), :] = ld[r * seg:(r + 1) * seg, :]

        ch = 256

        def combine(c, carry):
            rows = pl.ds(pl.multiple_of(c * ch, ch), ch)
            ls = [lg[g, rows, :] for g in range(ng)]
            mx = functools.reduce(jnp.maximum, ls)
            es = [jnp.exp(x - mx) for x in ls]
            den = functools.reduce(jnp.add, es)
            num = functools.reduce(jnp.add, [es[g] * og[g, rows, :] for g in range(ng)])
            o_ref[rows, :] = (num / den).astype(o_ref.dtype)
            lse_ref[rows, :] = mx + jnp.log(den)
            return carry

        lax.fori_loop(0, s // ch, combine, 0)

    def col(slot):
        return pl.BlockSpec((None, s, HEAD_DIM), lambda h: (slot, 0, h))

    head = pl.BlockSpec((s, HEAD_DIM), lambda h: (0, h))
    return pl.pallas_call(
        body, name="attn_fwd", grid=(heads,),
        in_specs=[col(0), col(1), col(2), pl.BlockSpec((None, 1, HEAD_DIM), lambda h: (h, 0, 0))],
        out_specs=[head, head],
        out_shape=[jax.ShapeDtypeStruct((s, d), BF16), jax.ShapeDtypeStruct((s, d), F32)],
        scratch_shapes=[pltpu.VMEM((s, HEAD_DIM), BF16)] * 3 + [pltpu.VMEM((s, HEAD_DIM), F32)] * 2
        + [pltpu.VMEM((ng, s, HEAD_DIM), F32)] * 2,
        compiler_params=_params(40),
    )(proj_a, proj_a, proj_a, slopes)


def _attn_bwd(proj_a, slopes, y_attn, lse, dy):
    _, s, d = proj_a.shape
    heads = d // HEAD_DIM
    scale = HEAD_DIM ** -0.5
    n_t = s // ATTN_BLK

    def body(q_ref, k_ref, v_ref, sl_ref, o_ref, lse_ref, dy_ref, out_ref,
             qd, kd, vd, dod, lsd, dld, delta, dqd, dkd, dvd, dqa, dka, dva):
        slope = sl_ref[...][:, :1]
        dyv = dy_ref[...]
        delta[...] = jnp.broadcast_to(
            jnp.sum(dyv * o_ref[...].astype(F32), axis=1, keepdims=True), (s, HEAD_DIM))
        for g, dil in enumerate(DILATIONS):
            nblk = s // dil // ATTN_BLK
            seg = s // dil
            qs, ks, vs, dos, lss, dls = _streams(
                [(qd, q_ref), (kd, k_ref), (vd, v_ref), (dod, dy_ref), (lsd, lse_ref), (dld, delta)], dil, s)
            dq_t, dk_t, dv_t = (dqa, dka, dva) if dil == 1 else (dqd, dkd, dvd)
            band, bias, jj = _attn_masks(slope, dil)

            def blk(t, carry, nblk=nblk, band=band, bias=bias, jj=jj, qs=qs, ks=ks, vs=vs, dos=dos, lss=lss,
                    dls=dls, dq_t=dq_t, dk_t=dk_t, dv_t=dv_t):
                ck, cv = carry
                prev, cur, first = _attn_window(t, nblk)
                q = qs[cur, :]
                do = dos[cur, :]
                lse_b = lss[cur, :]
                dl_b = dls[cur, :]
                kk = jnp.concatenate([ks[prev, :], ks[cur, :]], axis=0)
                vv = jnp.concatenate([vs[prev, :], vs[cur, :]], axis=0)
                sc = _dot(q, kk, NT) * scale + bias
                p = jnp.where(band & (jj >= first), jnp.exp(sc - jnp.concatenate([lse_b, lse_b], axis=1)), 0.0)
                dp = _dot(do, vv, NT)
                ds = p * (dp - jnp.concatenate([dl_b, dl_b], axis=1))
                dv_b = _dot(p, do, TN)
                dk_b = _dot(ds, q, TN) * scale
                dq_t[cur, :] = _dot(ds, kk, NN) * scale
                dk_t[prev, :] = ck + dk_b[:ATTN_BLK]
                dv_t[prev, :] = cv + dv_b[:ATTN_BLK]
                return dk_b[ATTN_BLK:], dv_b[ATTN_BLK:]

            zero = jnp.zeros((ATTN_BLK, HEAD_DIM), F32)
            ck, cv = lax.fori_loop(0, n_t, blk, (zero, zero), unroll=ATTN_UNROLL)
            dk_t[(n_t - 1) * ATTN_BLK:, :] = ck
            dv_t[(n_t - 1) * ATTN_BLK:, :] = cv
            if dil > 1:
                for acc, part in ((dqa, dqd), (dka, dkd), (dva, dvd)):
                    for r in range(dil):
                        acc[pl.ds(r, seg, stride=dil), :] += part[r * seg:(r + 1) * seg, :]
        out_ref[0] = dqa[...].astype(out_ref.dtype)
        out_ref[1] = dka[...].astype(out_ref.dtype)
        out_ref[2] = dva[...].astype(out_ref.dtype)

    def col(slot):
        return pl.BlockSpec((None, s, HEAD_DIM), lambda h: (slot, 0, h))

    head = pl.BlockSpec((s, HEAD_DIM), lambda h: (0, h))
    return pl.pallas_call(
        body, name="attn_bwd", grid=(heads,),
        in_specs=[col(0), col(1), col(2), pl.BlockSpec((None, 1, HEAD_DIM), lambda h: (h, 0, 0)),
                  head, head, head],
        out_specs=pl.BlockSpec((N_QKV, s, HEAD_DIM), lambda h: (0, 0, h)),
        out_shape=jax.ShapeDtypeStruct((N_QKV, s, d), BF16),
        scratch_shapes=[pltpu.VMEM((s, HEAD_DIM), BF16)] * 4 + [pltpu.VMEM((s, HEAD_DIM), F32)] * 9,
        compiler_params=_params(48),
    )(proj_a, proj_a, proj_a, slopes, y_attn, lse, dy)


def _expm1(x):
    small = x * (1.0 + x * (0.5 + x * (1.0 / 6.0 + x * (1.0 / 24.0 + x * (1.0 / 120.0)))))
    return jnp.where(jnp.abs(x) < 0.1, small, jnp.exp(x) - 1.0)


def _softplus(x):
    return jnp.maximum(x, 0.0) + jnp.log1p(jnp.exp(-jnp.abs(x)))


GELU_K = 0.7978845608028654
GELU_C = 0.044715


def _gelu(x):
    t = jnp.tanh(GELU_K * (x + GELU_C * x * x * x))
    return 0.5 * x * (1.0 + t), t


def _gelu_grad(x, t):
    return 0.5 * (1.0 + t) + 0.5 * x * (1.0 - t * t) * GELU_K * (1.0 + 3.0 * GELU_C * x * x)


def _lru_gates(xc, wa, ba, wx, bx, sp):
    r = _sigmoid(_dot(xc, wa, NN) + ba)
    ig = _sigmoid(_dot(xc, wx, NN) + bx)
    log_a = -LRU_C * r * sp
    a = jnp.exp(log_a)
    mult = jnp.sqrt(-_expm1(2.0 * log_a))
    return r, ig, a, mult


def _scan_fwd(a, u, tt):
    row = lax.broadcasted_iota(jnp.int32, a.shape, 0)
    sh = 1
    while sh < tt:
        keep = row >= sh
        a_s = jnp.where(keep, pltpu.roll(a, sh, 0), 1.0)
        u_s = jnp.where(keep, pltpu.roll(u, sh, 0), 0.0)
        u = a * u_s + u
        a = a * a_s
        sh *= 2
    return a, u


def _scan_bwd(b, g, tt):
    row = lax.broadcasted_iota(jnp.int32, b.shape, 0)
    sh = 1
    while sh < tt:
        keep = row < tt - sh
        b_s = jnp.where(keep, pltpu.roll(b, tt - sh, 0), 1.0)
        g_s = jnp.where(keep, pltpu.roll(g, tt - sh, 0), 0.0)
        g = g + b * g_s
        b = b * b_s
        sh *= 2
    return b, g


def _conv_rows(xpad_ref, cw, cb, s):
    acc = cb
    for j in range(CONV_TAPS):
        off = 8 - (CONV_TAPS - 1) + j
        acc = acc + cw[j:j + 1, :] * xpad_ref[off:off + s, :]
    return acc


LRU_TILE = 128


def _lru_specs(s, d):
    heads = d // HEAD_DIM

    def col(slot):
        return pl.BlockSpec((None, s, HEAD_DIM), lambda h: (slot, 0, h))

    vec = pl.BlockSpec((1, HEAD_DIM), lambda h: (0, h))
    mat = pl.BlockSpec((None, HEAD_DIM, HEAD_DIM), lambda h: (h, 0, 0))
    cw = pl.BlockSpec((8, HEAD_DIM), lambda h: (0, h))
    head = pl.BlockSpec((s, HEAD_DIM), lambda h: (0, h))
    return heads, col, vec, mat, cw, head


def _lru_fwd(proj_b, conv_w, conv_b, wa, ba, wx, bx, lam):
    _, s, d = proj_b.shape
    heads, col, vec, mat, cws, head = _lru_specs(s, d)
    tt = LRU_TILE

    def body(xr_ref, xg_ref, cw_ref, cb_ref, wa_ref, ba_ref, wx_ref, bx_ref, lam_ref, y_ref, h_ref, xpad, xc_s):
        xpad[0:8, :] = jnp.zeros((8, HEAD_DIM), F32)
        xpad[8:8 + s, :] = xr_ref[...]
        xc_s[...] = _conv_rows(xpad, cw_ref[...], cb_ref[...], s)
        sp = _softplus(-lam_ref[...])
        wav, wxv, bav, bxv = wa_ref[...], wx_ref[...], ba_ref[...], bx_ref[...]

        def tile(i, hc):
            rows = pl.ds(pl.multiple_of(i * tt, tt), tt)
            xc = xc_s[rows, :]
            _, ig, a, mult = _lru_gates(xc, wav, bav, wxv, bxv, sp)
            pa, hl = _scan_fwd(a, mult * (ig * xc), tt)
            h = hl + pa * hc
            h_ref[rows, :] = h
            gel, _ = _gelu(xg_ref[rows, :])
            y_ref[rows, :] = (h * gel).astype(y_ref.dtype)
            return h[tt - 1:tt, :]

        lax.fori_loop(0, s // tt, tile, jnp.zeros((1, HEAD_DIM), F32))

    return pl.pallas_call(
        body, name="lru_fwd", grid=(heads,),
        in_specs=[col(0), col(1), cws, vec, mat, vec, mat, vec, vec],
        out_specs=[head, head],
        out_shape=[jax.ShapeDtypeStruct((s, d), BF16), jax.ShapeDtypeStruct((s, d), F32)],
        scratch_shapes=[pltpu.VMEM((s + 8, HEAD_DIM), F32), pltpu.VMEM((s, HEAD_DIM), F32)],
        compiler_params=_params(32),
    )(proj_b, proj_b, conv_w, conv_b, wa, ba, wx, bx, lam)


def _lru_bwd(proj_b, h_lru, dy, conv_w, conv_b, wa, ba, wx, bx, lam, dproj_b):
    _, s, d = proj_b.shape
    heads, col, vec, mat, cws, head = _lru_specs(s, d)
    tt = LRU_TILE
    n_t = s // tt

    def body(xr_ref, xg_ref, h_ref, dy_ref, cw_ref, cb_ref, wa_ref, ba_ref, wx_ref, bx_ref, lam_ref, alias_ref,
             out_ref, dcw_ref, dcb_ref, dwa_ref, dba_ref, dwx_ref, dbx_ref, dlam_ref, xpad, xc_s, dxc_s):
        del alias_ref
        xpad[0:8, :] = jnp.zeros((8, HEAD_DIM), F32)
        xpad[8:8 + s, :] = xr_ref[...]
        cwv = cw_ref[...]
        xc_s[...] = _conv_rows(xpad, cwv, cb_ref[...], s)
        dxc_s[s:s + 8, :] = jnp.zeros((8, HEAD_DIM), F32)
        lamv = lam_ref[...]
        sp = _softplus(-lamv)
        wav, wxv, bav, bxv = wa_ref[...], wx_ref[...], ba_ref[...], bx_ref[...]
        dwa_ref[...] = jnp.zeros_like(dwa_ref)
        dwx_ref[...] = jnp.zeros_like(dwx_ref)
        zero = jnp.zeros((1, HEAD_DIM), F32)
        row = lax.broadcasted_iota(jnp.int32, (tt, HEAD_DIM), 0)

        def tile(it, carry):
            dh_next, a_next, dba, dbx, dsp, dcb = carry
            i = n_t - 1 - it
            t0 = pl.multiple_of(i * tt, tt)
            rows = pl.ds(t0, tt)
            xc = xc_s[rows, :]
            r, ig, a, mult = _lru_gates(xc, wav, bav, wxv, bxv, sp)
            h = h_ref[rows, :]
            before = h_ref[pl.ds(pl.multiple_of(jnp.maximum(t0 - 8, 0), 8), 8), :][7:8, :]
            before = before * (i > 0).astype(F32)
            h_prev = jnp.where(row == 0, before, pltpu.roll(h, 1, 0))
            xg = xg_ref[rows, :]
            dyv = dy_ref[rows, :]
            gel, th = _gelu(xg)
            out_ref[1, rows, :] = (dyv * h * _gelu_grad(xg, th)).astype(out_ref.dtype)
            b = jnp.where(row == tt - 1, a_next, pltpu.roll(a, tt - 1, 0))
            pb, z = _scan_bwd(b, dyv * gel, tt)
            dh = z + pb * dh_next
            da = dh * h_prev
            dmult = dh * (ig * xc)
            dig = dh * (mult * xc)
            dla = da * a - dmult * (a * a / mult)
            dzr = dla * (-LRU_C * sp) * (r * (1.0 - r))
            dzx = dig * (ig * (1.0 - ig))
            dxc = dh * (mult * ig) + _dot(dzr, wav, NT) + _dot(dzx, wxv, NT)
            dxc_s[rows, :] = dxc
            dwa_ref[...] += _dot(xc, dzr, TN)
            dwx_ref[...] += _dot(xc, dzx, TN)
            return (dh[0:1, :], a[0:1, :],
                    dba + jnp.sum(dzr, axis=0, keepdims=True),
                    dbx + jnp.sum(dzx, axis=0, keepdims=True),
                    dsp + jnp.sum(dla * (-LRU_C * r), axis=0, keepdims=True),
                    dcb + jnp.sum(dxc, axis=0, keepdims=True))

        _, _, dba, dbx, dsp, dcb = lax.fori_loop(0, n_t, tile, (zero, zero, zero, zero, zero, zero))
        dba_ref[...] = dba
        dbx_ref[...] = dbx
        dcb_ref[...] = dcb
        dlam_ref[...] = -dsp * _sigmoid(-lamv)
        dxr = jnp.zeros((s, HEAD_DIM), F32)
        for j in range(CONV_TAPS):
            back = CONV_TAPS - 1 - j
            off = 8 - back
            dcw_ref[j:j + 1, :] = jnp.sum(dxc_s[0:s, :] * xpad[off:off + s, :], axis=0, keepdims=True)
            dxr = dxr + cwv[j:j + 1, :] * dxc_s[back:back + s, :]
        out_ref[0] = dxr.astype(out_ref.dtype)

    return pl.pallas_call(
        body, name="lru_bwd", grid=(heads,),
        in_specs=[col(0), col(1), head, head, cws, vec, mat, vec, mat, vec, vec, pl.BlockSpec(memory_space=pl.ANY)],
        out_specs=[pl.BlockSpec((2, s, HEAD_DIM), lambda h: (0, 0, h)),
                   pl.BlockSpec((CONV_TAPS, HEAD_DIM), lambda h: (0, h)), vec, mat, vec, mat, vec, vec],
        out_shape=[jax.ShapeDtypeStruct(dproj_b.shape, dproj_b.dtype),
                   jax.ShapeDtypeStruct((CONV_TAPS, d), F32), jax.ShapeDtypeStruct((1, d), F32),
                   jax.ShapeDtypeStruct(wa.shape, F32), jax.ShapeDtypeStruct((1, d), F32),
                   jax.ShapeDtypeStruct(wx.shape, F32), jax.ShapeDtypeStruct((1, d), F32),
                   jax.ShapeDtypeStruct((1, d), F32)],
        scratch_shapes=[pltpu.VMEM((s + 8, HEAD_DIM), F32), pltpu.VMEM((s, HEAD_DIM), F32),
                        pltpu.VMEM((s + 8, HEAD_DIM), F32)],
        input_output_aliases={11: 0},
        compiler_params=_params(32),
    )(proj_b, proj_b, h_lru, dy, conv_w, conv_b, wa, ba, wx, bx, lam, dproj_b)


def _place():
    x, y, c = (lax.axis_index(n) for n in AXES)
    return x, y, c


def _other_chips(x, y):
    return [(1 - x, y), (x, 1 - y), (1 - x, 1 - y)]


HBM = pl.BlockSpec(memory_space=pl.ANY)


def _cast_shard(w, chip_arr, name):
    r, cols = w.shape
    rh = r // 2
    tr = _row_tile(rh, cols * 4)
    nt = rh // tr

    def body(chip_ref, w_ref, o_ref):
        del chip_ref
        o_ref[...] = w_ref[...].astype(BF16)

    return pl.pallas_call(
        body, name=name,
        grid_spec=pltpu.PrefetchScalarGridSpec(
            num_scalar_prefetch=1, grid=(2, nt),
            in_specs=[pl.BlockSpec((tr, cols), lambda h, i, chip_ref: (h * nt + i, 0))],
            out_specs=pl.BlockSpec((None, None, tr, cols), lambda h, i, chip_ref: (chip_ref[0], h, i, 0))),
        out_shape=jax.ShapeDtypeStruct((N_CHIPS, 2, rh, cols), BF16), compiler_params=_params(32),
    )(chip_arr, w)


HBM_SPEC = pl.BlockSpec(memory_space=pltpu.HBM)
SEM_SPEC = pl.BlockSpec(memory_space=pltpu.SEMAPHORE)
EFFECT = pltpu.SideEffectType.DATAFLOW_SIDE_EFFECTING
TOKEN = jax.ShapeDtypeStruct((8, 128), F32)
TOKEN_SPEC = pl.BlockSpec(memory_space=pltpu.VMEM)


def _in_hbm(arrays):
    return [pltpu.with_memory_space_constraint(a, pltpu.HBM) for a in arrays]


def _hbm_like(arrays):
    return [pltpu.HBM(a.shape, a.dtype) for a in arrays]


def _sems(n):
    return pltpu.SemaphoreType.DMA((n,))


def _remote(src, dst, send_sem, recv_sem, to):
    return pltpu.make_async_remote_copy(src_ref=src, dst_ref=dst, send_sem=send_sem, recv_sem=recv_sem,
                                        device_id=to, device_id_type=MESH)


def _gather_start(bufs, groups, after, name):
    n = len(bufs)
    ng = len(groups)

    def body(*refs):
        ins = refs[:n]
        sems = refs[n + len(after):n + len(after) + 2 * ng]
        x, y, c = _place()
        me = 2 * x + y
        for g, ws in enumerate(groups):
            for i, w in enumerate(ws):
                for j, (cx, cy) in enumerate(_other_chips(x, y)):
                    mine = ins[w].at[me, c]
                    _remote(mine, mine, sems[2 * g].at[3 * i + j], sems[2 * g + 1].at[3 * i + j], (cx, cy, c)).start()

    sem_shapes = []
    for ws in groups:
        sem_shapes += [_sems(3 * len(ws)), _sems(3 * len(ws))]
    res = pl.pallas_call(
        body, name=name, in_specs=[HBM_SPEC] * n + [HBM] * len(after),
        out_specs=[SEM_SPEC] * (2 * ng) + [HBM_SPEC] * n, out_shape=sem_shapes + _hbm_like(bufs),
        input_output_aliases={w: 2 * ng + w for w in range(n)},
        compiler_params=pltpu.CompilerParams(has_side_effects=EFFECT),
    )(*_in_hbm(bufs), *after)
    return [(res[2 * g], res[2 * g + 1]) for g in range(ng)], list(res[2 * ng:])


def _gather_forward(bufs, recv, after, name):
    m = len(bufs)

    def body(*refs):
        ins, recv_in = refs[:m], refs[m]
        fsend, frecv = refs[m + 1 + len(after)], refs[m + 2 + len(after)]
        x, y, c = _place()
        for j, (cx, cy) in enumerate(_other_chips(x, y)):
            for i in range(m):
                landed = ins[i].at[2 * cx + cy, c]
                k = 3 * i + j
                _remote(landed, landed, fsend.at[k], recv_in.at[k], (cx, cy, c)).wait_recv()
                _remote(landed, landed, fsend.at[k], frecv.at[k], (x, y, 1 - c)).start()

    res = pl.pallas_call(
        body, name=name, in_specs=[HBM_SPEC] * m + [SEM_SPEC] + [HBM] * len(after),
        out_specs=[SEM_SPEC, SEM_SPEC] + [HBM_SPEC] * m, out_shape=[_sems(3 * m), _sems(3 * m)] + _hbm_like(bufs),
        input_output_aliases={i: 2 + i for i in range(m)},
        compiler_params=pltpu.CompilerParams(has_side_effects=EFFECT),
    )(*bufs, recv, *after)
    return (res[0], res[1]), list(res[2:])


def _gather_finish(bufs, send, fsend, frecv, after, name):
    m = len(bufs)

    def body(*refs):
        ins = refs[:m]
        send_in, fsend_in, frecv_in = refs[m:m + 3]
        x, y, c = _place()
        me = 2 * x + y
        for j, (cx, cy) in enumerate(_other_chips(x, y)):
            cj = 2 * cx + cy
            for i in range(m):
                k = 3 * i + j
                mine = ins[i].at[me, c]
                _remote(mine, mine, send_in.at[k], frecv_in.at[k], (cx, cy, c)).wait_send()
                landed = ins[i].at[cj, c]
                _remote(landed, landed, fsend_in.at[k], frecv_in.at[k], (x, y, 1 - c)).wait_send()
                theirs = ins[i].at[cj, 1 - c]
                _remote(theirs, theirs, fsend_in.at[k], frecv_in.at[k], (x, y, 1 - c)).wait_recv()

    return list(pl.pallas_call(
        body, name=name, in_specs=[HBM_SPEC] * m + [SEM_SPEC] * 3 + [HBM] * len(after),
        out_specs=[HBM_SPEC] * m, out_shape=_hbm_like(bufs),
        input_output_aliases={i: i for i in range(m)},
        compiler_params=pltpu.CompilerParams(has_side_effects=EFFECT),
    )(*bufs, send, fsend, frecv, *after))


def _pair_exchange(grads, name):
    n = len(grads)

    def body(*refs):
        ins, outs = refs[:n], refs[n:2 * n]
        send_sems, recv_sems = refs[2 * n:]
        x, y, c = _place()
        sibling = (x, y, 1 - c)
        cps = []
        for w in range(n):
            for j in range(N_CHIPS):
                cp = pltpu.make_async_remote_copy(
                    src_ref=ins[w].at[j, 1 - c], dst_ref=outs[w].at[j], send_sem=send_sems.at[N_CHIPS * w + j],
                    recv_sem=recv_sems.at[N_CHIPS * w + j], device_id=sibling, device_id_type=MESH)
                cp.start()
                cps.append(cp)
        for cp in cps:
            cp.wait()

    return pl.pallas_call(
        body, name=name, in_specs=[HBM] * n, out_specs=[HBM] * n,
        out_shape=[jax.ShapeDtypeStruct((N_CHIPS,) + a.shape[2:], a.dtype) for a in grads],
        scratch_shapes=[pltpu.SemaphoreType.DMA((N_CHIPS * n,)), pltpu.SemaphoreType.DMA((N_CHIPS * n,))],
    )(*grads)


def _chip_start(sums, name, to_all=()):
    m = len(sums)
    lands = [lax.empty(((N_CHIPS,) if i in to_all else ()) + a.shape, a.dtype) for i, a in enumerate(sums)]

    def body(*refs):
        ins, land_in = refs[:m], refs[m:2 * m]
        send, recv = refs[2 * m], refs[2 * m + 1]
        token = refs[4 * m + 2]
        x, y, c = _place()
        me = 2 * x + y
        for i in sorted(range(m), key=lambda i: i not in to_all):
            for j, (cx, cy) in enumerate(_other_chips(x, y)):
                src = ins[i] if i in to_all else ins[i].at[2 * cx + cy]
                _remote(src, land_in[i].at[me], send.at[3 * i + j], recv.at[3 * i + j], (cx, cy, c)).start()
        token[...] = jnp.zeros_like(token)

    res = pl.pallas_call(
        body, name=name, in_specs=[HBM_SPEC] * (2 * m),
        out_specs=[SEM_SPEC, SEM_SPEC] + [HBM_SPEC] * (2 * m) + [TOKEN_SPEC],
        out_shape=[_sems(3 * m), _sems(3 * m)] + _hbm_like(sums) + _hbm_like(lands) + [TOKEN],
        input_output_aliases={i: 2 + i for i in range(2 * m)},
        compiler_params=pltpu.CompilerParams(has_side_effects=EFFECT),
    )(*_in_hbm(sums), *_in_hbm(lands))
    return (res[0], res[1]), list(res[2:2 + m]), list(res[2 + m:2 + 2 * m]), res[2 + 2 * m]


def _chip_wait(sems, sums, lands, after, name, to_all=()):
    m = len(sums)

    def body(*refs):
        ins, land_in = refs[:m], refs[m:2 * m]
        send_in, recv_in = refs[2 * m], refs[2 * m + 1]
        x, y, c = _place()
        for i in range(m):
            for j, (cx, cy) in enumerate(_other_chips(x, y)):
                cj = 2 * cx + cy
                src = ins[i] if i in to_all else ins[i].at[cj]
                cp = _remote(src, land_in[i].at[cj], send_in.at[3 * i + j], recv_in.at[3 * i + j], (cx, cy, c))
                cp.wait_send()
                cp.wait_recv()

    res = pl.pallas_call(
        body, name=name, in_specs=[HBM_SPEC] * (2 * m) + [SEM_SPEC, SEM_SPEC] + [HBM] * len(after),
        out_specs=[HBM_SPEC] * (2 * m), out_shape=_hbm_like(sums) + _hbm_like(lands),
        input_output_aliases={i: i for i in range(2 * m)},
        compiler_params=pltpu.CompilerParams(has_side_effects=EFFECT),
    )(*sums, *lands, sems[0], sems[1], *after)
    return list(res[:m]), list(res[m:])


def _half_exchange(bufs, name):
    n = len(bufs)
    parts = []
    for w, a in enumerate(bufs):
        parts += [(w, None)] if a.ndim == 3 else [(w, j) for j in range(a.shape[0])]
    np_ = len(parts)

    def body(*refs):
        outs = refs[n:2 * n]
        send_sems, recv_sems = refs[2 * n:]
        x, y, c = _place()
        sibling = (x, y, 1 - c)

        def half(w, j, h):
            return outs[w].at[h] if j is None else outs[w].at[j, h]

        cps = []
        for k, (w, j) in enumerate(parts):
            rc = _remote(half(w, j, c), half(w, j, c), send_sems.at[k], recv_sems.at[k], sibling)
            rc.start()
            cps.append(rc)
        for k, (w, j) in enumerate(parts):
            theirs = half(w, j, 1 - c)
            _remote(theirs, theirs, send_sems.at[k], recv_sems.at[k], sibling).wait_recv()
        for cp in cps:
            cp.wait_send()

    return pl.pallas_call(
        body, name=name, in_specs=[HBM] * n, out_specs=[HBM] * n,
        out_shape=[jax.ShapeDtypeStruct(a.shape, a.dtype) for a in bufs],
        input_output_aliases={w: w for w in range(n)},
        scratch_shapes=[_sems(np_), _sems(np_)],
    )(*bufs)


def _all_gather8(block, name, after=()):
    def body(in_ref, *rest):
        out_ref, send_sems, recv_sems, local_sem = rest[len(after):]
        x, y, c = _place()
        me = 4 * x + 2 * y + c
        mine = pltpu.make_async_copy(in_ref, out_ref.at[me], local_sem)
        mine.start()
        flips = [(fx, fy, fc) for fx in (0, 1) for fy in (0, 1) for fc in (0, 1)][1:]
        cps = []
        for k, (fx, fy, fc) in enumerate(flips):
            cp = pltpu.make_async_remote_copy(
                src_ref=in_ref, dst_ref=out_ref.at[me], send_sem=send_sems.at[k], recv_sem=recv_sems.at[k],
                device_id=(x ^ fx, y ^ fy, c ^ fc), device_id_type=MESH)
            cp.start()
            cps.append(cp)
        for k, (fx, fy, fc) in enumerate(flips):
            px, py, pc = x ^ fx, y ^ fy, c ^ fc
            theirs = out_ref.at[4 * px + 2 * py + pc]
            pltpu.make_async_remote_copy(
                src_ref=theirs, dst_ref=theirs, send_sem=send_sems.at[k], recv_sem=recv_sems.at[k],
                device_id=(px, py, pc), device_id_type=MESH).wait_recv()
        for cp in cps:
            cp.wait_send()
        mine.wait()

    return pl.pallas_call(
        body, name=name, in_specs=[HBM] * (1 + len(after)), out_specs=HBM,
        out_shape=jax.ShapeDtypeStruct((N_DEV,) + block.shape, block.dtype),
        scratch_shapes=[pltpu.SemaphoreType.DMA((N_DEV - 1,)), pltpu.SemaphoreType.DMA((N_DEV - 1,)),
                        pltpu.SemaphoreType.DMA],
    )(block, *after)


def _pair_sum(grad, recv, c_arr, name):
    _, _, rh, cols = grad.shape
    tr = _row_tile(rh, cols * 4)

    def body(c_ref, g_ref, r_ref, o_ref):
        del c_ref
        o_ref[...] = (g_ref[...].astype(F32) + r_ref[...].astype(F32)).astype(o_ref.dtype)

    spec = pl.BlockSpec((None, tr, cols), lambda j, i, c_ref: (j, i, 0))
    return pl.pallas_call(
        body, name=name,
        grid_spec=pltpu.PrefetchScalarGridSpec(
            num_scalar_prefetch=1, grid=(N_CHIPS, rh // tr),
            in_specs=[pl.BlockSpec((None, None, tr, cols), lambda j, i, c_ref: (j, c_ref[0], i, 0)), spec],
            out_specs=spec),
        out_shape=jax.ShapeDtypeStruct(recv.shape, grad.dtype), compiler_params=_params(32),
    )(c_arr, grad, recv)


def _sum_by_chip(chip, p_ref, own_ref, o_ref):
    o_ref[...] = jnp.zeros_like(o_ref)
    for k in range(N_CHIPS):
        @pl.when(chip == k)
        def _():
            o_ref[...] += own_ref[...].astype(F32)

        @pl.when(chip != k)
        def _(k=k):
            o_ref[...] += p_ref[k].astype(F32)


def _chip_sum_all(parts, own, place_arr, name):
    _, nj, rh, cols = parts.shape
    tr = _row_tile(rh, cols * 4)

    def body(place_ref, p_ref, own_ref, o_ref):
        _sum_by_chip(place_ref[0], p_ref, own_ref, o_ref)

    return pl.pallas_call(
        body, name=name,
        grid_spec=pltpu.PrefetchScalarGridSpec(
            num_scalar_prefetch=1, grid=(nj, rh // tr),
            in_specs=[pl.BlockSpec((N_CHIPS, None, tr, cols), lambda j, i, place_ref: (0, j, i, 0)),
                      pl.BlockSpec((None, tr, cols), lambda j, i, place_ref: (j, i, 0))],
            out_specs=pl.BlockSpec((None, None, tr, cols), lambda j, i, place_ref: (j, place_ref[1], i, 0))),
        out_shape=jax.ShapeDtypeStruct((nj, 2, rh, cols), F32), compiler_params=_params(32),
    )(place_arr, parts, own)


def _chip_sum(parts, own, place_arr, name):
    _, rh, cols = parts.shape
    tr = _row_tile(rh, cols * 4)

    def body(place_ref, p_ref, own_ref, o_ref):
        _sum_by_chip(place_ref[0], p_ref, own_ref, o_ref)

    return pl.pallas_call(
        body, name=name,
        grid_spec=pltpu.PrefetchScalarGridSpec(
            num_scalar_prefetch=1, grid=(rh // tr,),
            in_specs=[pl.BlockSpec((N_CHIPS, tr, cols), lambda i, place_ref: (0, i, 0)),
                      pl.BlockSpec((None, tr, cols), lambda i, place_ref: (place_ref[0], i, 0))],
            out_specs=pl.BlockSpec((None, tr, cols), lambda i, place_ref: (place_ref[1], i, 0))),
        out_shape=jax.ShapeDtypeStruct((2, rh, cols), F32), compiler_params=_params(32),
    )(place_arr, parts, own)


def _adamw_math(w, g, m, v):
    m = ADAM_B1 * m + (1.0 - ADAM_B1) * g
    v = ADAM_B2 * v + (1.0 - ADAM_B2) * (g * g)
    m_hat = m / (1.0 - ADAM_B1 ** ADAM_STEP)
    v_hat = v / (1.0 - ADAM_B2 ** ADAM_STEP)
    delta = -ADAM_LR * (m_hat / (jnp.sqrt(v_hat) + ADAM_EPS) + ADAM_WD * w)
    return delta, m, v


def _adamw(w, g, m, v, name):
    rows, cols = w.shape
    tr = _row_tile(rows, cols * 4)

    def body(w_ref, g_ref, m_ref, v_ref, go_ref, d_ref, nm_ref, nv_ref):
        gv = g_ref[...]
        go_ref[...] = gv
        d_ref[...], nm_ref[...], nv_ref[...] = _adamw_math(w_ref[...], gv, m_ref[...], v_ref[...])

    spec = pl.BlockSpec((tr, cols), lambda i: (i, 0))
    return pl.pallas_call(
        body, name=name, grid=(rows // tr,), in_specs=[spec] * 4, out_specs=[spec] * 4,
        out_shape=[jax.ShapeDtypeStruct(w.shape, F32)] * 4, compiler_params=_params(32),
    )(w, g, m, v)


def _sum8_adamw_row(parts, w, m, v, name):
    cols = parts.shape[2]

    def body(p_ref, w_ref, m_ref, v_ref, g_ref, d_ref, nm_ref, nv_ref):
        g = p_ref[0, 0:1, :]
        for k in range(1, N_DEV):
            g = g + p_ref[k, 0:1, :]
        g_ref[...] = g
        d_ref[...], nm_ref[...], nv_ref[...] = _adamw_math(w_ref[...], g, m_ref[...], v_ref[...])

    return pl.pallas_call(
        body, name=name, out_shape=[jax.ShapeDtypeStruct((1, cols), F32)] * 4, compiler_params=_params(32),
    )(parts, w, m, v)


def _pack_rows(pieces, rows, name):
    cols = pieces[0].shape[1]
    n = len(pieces)

    def body(*refs):
        o_ref = refs[n]
        o_ref[...] = jnp.zeros_like(o_ref)
        at = 0
        for p_ref in refs[:n]:
            r = p_ref.shape[0]
            o_ref[at:at + r, :] = p_ref[...]
            at += r

    return pl.pallas_call(
        body, name=name, out_shape=jax.ShapeDtypeStruct((rows, cols), F32), compiler_params=_params(32),
    )(*pieces)


def kernel(x, norm_mix_g, w_in, conv_w, conv_b, lru_wa, lru_ba, lru_wx, lru_bx, lru_lambda, w_proj_attn, w_proj_lru, w_out, norm_mlp_g, w_up, w_down, norm_final_g, loss_target, m_norm_mix_g, m_w_in, m_conv_w, m_conv_b, m_lru_wa, m_lru_ba, m_lru_wx, m_lru_bx, m_lru_lambda, m_w_proj_attn, m_w_proj_lru, m_w_out, m_norm_mlp_g, m_w_up, m_w_down, m_norm_final_g, v_norm_mix_g, v_w_in, v_conv_w, v_conv_b, v_lru_wa, v_lru_ba, v_lru_wx, v_lru_bx, v_lru_lambda, v_w_proj_attn, v_w_proj_lru, v_w_out, v_norm_mlp_g, v_w_up, v_w_down, v_norm_final_g):
    s, d = x.shape[1], x.shape[2]
    ff = w_up.shape[2] * N_CHIPS
    heads = d // HEAD_DIM
    u = d // 4
    assert s % (max(DILATIONS) * ATTN_BLK) == 0 and d % (4 * HEAD_DIM) == 0 and ff == 4 * d and DILATIONS[0] == 1
    xs, target = _in_hbm([x[0], loss_target[0]])
    gf = norm_final_g.reshape(1, d)
    wa, wx = lru_wa[0], lru_wx[0]
    core = lax.axis_index("c").astype(jnp.int32)
    chip = (2 * lax.axis_index("x") + lax.axis_index("y")).astype(jnp.int32)
    cidx = core.reshape(1)
    chip_arr = chip.reshape(1)
    place_arr = jnp.stack([chip, core])
    slopes = jnp.broadcast_to(
        (2.0 ** (-8.0 * jnp.arange(1, heads + 1, dtype=F32) / heads))[:, None, None], (heads, 1, HEAD_DIM))

    big = _in_hbm([w_in[0], w_proj_attn[0], w_proj_lru[0], w_out[0], w_up[0], w_down[0]])
    names = ["w_in", "w_proj_attn", "w_proj_lru", "w_out", "w_up", "w_down"]
    cw_pad = jnp.pad(conv_w[0], ((0, 8 - CONV_TAPS), (0, 0)))
    cw_all = _all_gather8(cw_pad, "gather_conv_w")
    conv_w_full = jnp.concatenate([cw_all[2 * j] for j in range(N_CHIPS)], axis=1)
    (sem_a,), buf_a = _gather_start([_cast_shard(big[0], chip_arr, "cast_w_in")], [[0]], [cw_all], "gather_start_w_in")
    bufs = [_cast_shard(w, chip_arr, "cast_" + nm) for w, nm in zip(big[1:], names[1:])]
    (sem_b, sem_c), bufs = _gather_start(bufs, [[0, 1, 2], [3, 4]], buf_a, "gather_start_rest")
    fsem_a, buf_a = _gather_forward(buf_a, sem_a[1], bufs[:1], "gather_forward_w_in")
    buf_a = _gather_finish(buf_a, sem_a[0], fsem_a[0], fsem_a[1], [], "gather_finish_w_in")
    w_in_g = buf_a[0].reshape(N_CHIPS, d, N_SLOTS * u)

    xn = _rms_fwd(xs, norm_mix_g, "norm_mix")
    proj_a = _proj_in(xn, w_in_g, 0, 4 * N_QKV, N_QKV)
    proj_b = _proj_in(xn, w_in_g, 4 * N_QKV, 4 * (N_SLOTS - N_QKV), N_SLOTS - N_QKV)
    y_attn, lse = _attn_fwd(proj_a, slopes)
    y_lru, h_lru = _lru_fwd(proj_b, conv_w_full, conv_b, wa, lru_ba, wx, lru_bx, lru_lambda)
    fsem_b, buf_b = _gather_forward(bufs[:3], sem_b[1], [y_attn, y_lru], "gather_forward_proj")
    fsem_c, buf_c = _gather_forward(bufs[3:], sem_c[1], [buf_b[0]], "gather_forward_mlp")
    buf_b = _gather_finish(buf_b, sem_b[0], fsem_b[0], fsem_b[1], [buf_c[0]], "gather_finish_proj")
    wpa_g = buf_b[0].reshape(d, d)
    wpl_g = buf_b[1].reshape(d, d)
    wout_g = buf_b[2].reshape(d, d)

    tn = u
    sd_f32 = jax.ShapeDtypeStruct((s, d), F32)
    sd_bf16 = jax.ShapeDtypeStruct((s, d), BF16)
    col = pl.BlockSpec((s, tn), lambda i, j, k: (0, j))

    def slot(n):
        return pl.BlockSpec((None, s, tn), lambda i, j, k: (n, 0, j))

    p_attn = _mm_nn("proj_attn", y_attn, wpa_g, [], [], [sd_f32], [col], _store, tn)[0]

    def merge(acc, extras, outs):
        pa_ref, ga_ref, gl_ref = extras
        merged = _sigmoid(ga_ref[...]) * pa_ref[...] + _sigmoid(gl_ref[...]) * acc
        outs[0][...] = merged.astype(BF16)
        outs[1][...] = acc

    tn2 = max(HEAD_DIM, u // 2)
    col2 = pl.BlockSpec((s, tn2), lambda i, j, k: (0, j))

    def slot2(n):
        return pl.BlockSpec((None, s, tn2), lambda i, j, k: (n, 0, j))

    merged, p_lru = _mm_nn("proj_lru_merge", y_lru, wpl_g, [p_attn, proj_b, proj_b], [col2, slot2(2), slot2(3)],
                           [sd_bf16, sd_f32], [col2, col2], merge, tn2)

    def add_resid(acc, extras, outs):
        outs[0][...] = extras[0][...] + acc

    h1 = _mm_nn("w_out_resid", merged, wout_g, [xs], [col], [sd_f32], [col], add_resid, tn)[0]
    hn = _rms_fwd(h1, norm_mlp_g, "norm_mlp")
    buf_c = _gather_finish(buf_c, sem_c[0], fsem_c[0], fsem_c[1], [hn], "gather_finish_mlp")
    wup_g = buf_c[0].reshape(N_CHIPS, d, d)
    wdown_g = buf_c[1].reshape(ff, d)

    def relu_sq(acc, extras, outs):
        r = jnp.maximum(acc, 0.0)
        outs[0][...] = (r * r).astype(BF16)
        outs[1][...] = r.astype(BF16)

    sf_bf16 = jax.ShapeDtypeStruct((s, ff), BF16)
    hid, relu_up = _mm(
        "w_up_relu2", [hn, wup_g],
        [pl.BlockSpec((s, d), lambda i, j, k: (0, 0)),
         pl.BlockSpec((None, d, tn), lambda i, j, k: (j // 4, 0, j % 4))],
        [sf_bf16, sf_bf16], [col, col], (1, ff // tn, 1), NN, relu_sq)
    h2 = _mm(
        "w_down_resid", [hid, wdown_g, h1],
        [pl.BlockSpec((s, d), lambda i, j, k: (0, k)), pl.BlockSpec((d, tn), lambda i, j, k: (k, j)), col],
        [sd_f32], [col], (1, d // tn, ff // d), NN, add_resid, nk=ff // d, acc_shape=(s, tn))[0]
    loss_part, dh2, dh2_b, d_gf = _loss_head(h2, gf, target)
    loss = lax.psum(loss_part[0, 0], AXES)

    def relu_sq_bwd(acc, extras, outs):
        outs[0][...] = (acc * (2.0 * extras[0][...].astype(F32))).astype(BF16)

    dup = _mm_nt("d_hid", dh2_b, wdown_g, [relu_up], [col], [sf_bf16], [col], relu_sq_bwd, tn)[0]
    tok_d = pl.BlockSpec((s, d), lambda i, j: (0, 0))
    g_wdown = _mm_tn(
        "g_w_down", hid, dh2_b, pl.BlockSpec((s, d), lambda i, j: (0, i)),
        pl.BlockSpec((s, tn), lambda i, j: (0, j)), jax.ShapeDtypeStruct((ff, d), BF16),
        pl.BlockSpec((d, tn), lambda i, j: (i, j)), (ff // d, d // tn), d, tn, s)
    dhn = _mm(
        "d_hn", [dup, wup_g],
        [pl.BlockSpec((s, d), lambda i, j, k: (0, k)), pl.BlockSpec((None, tn, d), lambda i, j, k: (k, j, 0))],
        [sd_f32], [col], (1, d // tn, ff // d), NT, _store, nk=ff // d, acc_shape=(s, tn))[0]
    g_wup = _mm_tn(
        "g_w_up", hn, dup, tok_d, pl.BlockSpec((s, tn), lambda i, j: (0, j)),
        jax.ShapeDtypeStruct((N_CHIPS, d, d), BF16), pl.BlockSpec((None, d, tn), lambda i, j: (j // 4, 0, j % 4)),
        (1, ff // tn), d, tn, s)
    big_m = _in_hbm([m_w_in[0], m_w_proj_attn[0], m_w_proj_lru[0], m_w_out[0], m_w_up[0], m_w_down[0]])
    big_v = _in_hbm([v_w_in[0], v_w_proj_attn[0], v_w_proj_lru[0], v_w_out[0], v_w_up[0], v_w_down[0]])
    big_out = {}

    def reduce_begin(ids, gs, tag, everywhere=None):
        g4 = [g.reshape(N_CHIPS, 2, big[i].shape[0] // 2, big[i].shape[1]) for i, g in zip(ids, gs)]
        tags = [names[i] for i in ids]
        if everywhere is not None:
            g4.append(everywhere.reshape(N_CHIPS, 2, everywhere.shape[0] // (2 * N_CHIPS), everywhere.shape[1]))
            tags.append("small_" + tag)
        from_sibling = _pair_exchange(g4, "pair_exchange_" + tag)
        sums = [_pair_sum(g, r, cidx, "pair_sum_" + t) for t, g, r in zip(tags, g4, from_sibling)]
        to_all = () if everywhere is None else (len(ids),)
        return _chip_start(sums, "chip_start_" + tag, to_all), to_all

    def reduce_end(ids, begun, after, tag):
        (sems, sums, lands, _), to_all = begun
        sums, lands = _chip_wait(sems, sums, lands, after, "chip_wait_" + tag, to_all)
        halves = [_chip_sum(p, own, place_arr, "chip_sum_" + names[i]) for i, p, own in zip(ids, lands, sums)]
        if to_all:
            halves.append(_chip_sum_all(lands[-1], sums[-1], place_arr, "chip_sum_small_" + tag))
        full = _half_exchange(halves, "half_exchange_" + tag)
        last = None
        for i, g in zip(ids, full):
            res = _adamw(big[i], g.reshape(big[i].shape), big_m[i], big_v[i], "adamw_" + names[i])
            big_out[names[i]] = tuple(a[None] for a in res)
            last = res[1]
        everywhere = full[-1].reshape(-1, full[-1].shape[-1]) if to_all else None
        return everywhere, last

    def after_token(a, begun):
        return a + begun[0][3][:1, :1]

    red_mlp = reduce_begin([4, 5], [g_wup, g_wdown], "mlp")
    dh1, dh1_b, d_gmlp = _rms_bwd(h1, after_token(norm_mlp_g, red_mlp), dhn, dh2, "norm_mlp_bwd")

    g_wout = _mm_tn(
        "g_w_out", merged, dh1_b, tok_d, pl.BlockSpec((s, tn), lambda i, j: (0, j)),
        jax.ShapeDtypeStruct((d, d), BF16), pl.BlockSpec((d, tn), lambda i, j: (0, j)), (1, d // tn), d, tn, s)

    def merge_bwd(acc, extras, outs):
        pa_ref, pl_ref, ga_ref, gl_ref = extras
        sa, sl = _sigmoid(ga_ref[...]), _sigmoid(gl_ref[...])
        outs[0][...] = (acc * sa).astype(BF16)
        outs[1][...] = (acc * sl).astype(BF16)
        outs[2][0] = (acc * pa_ref[...] * (sa * (1.0 - sa))).astype(BF16)
        outs[2][1] = (acc * pl_ref[...] * (sl * (1.0 - sl))).astype(BF16)

    nb = N_SLOTS - N_QKV
    d_pa, d_pl, dproj_b = _mm_nt(
        "d_merged", dh1_b, wout_g, [p_attn, p_lru, proj_b, proj_b], [col2, col2, slot2(2), slot2(3)],
        [sd_bf16, sd_bf16, jax.ShapeDtypeStruct((nb, s, d), BF16)],
        [col2, col2, pl.BlockSpec((2, s, tn2), lambda i, j, k: (1, 0, j))], merge_bwd, tn2)
    dy_attn = _mm_nt("d_y_attn", d_pa, wpa_g, [], [], [sd_f32], [col], _store, tn)[0]
    dy_lru = _mm_nt("d_y_lru", d_pl, wpl_g, [], [], [sd_f32], [col], _store, tn)[0]
    g_wpa = _mm_tn(
        "g_w_proj_attn", y_attn, d_pa, tok_d, pl.BlockSpec((s, tn), lambda i, j: (0, j)),
        jax.ShapeDtypeStruct((d, d), BF16), pl.BlockSpec((d, tn), lambda i, j: (0, j)), (1, d // tn), d, tn, s)
    g_wpl = _mm_tn(
        "g_w_proj_lru", y_lru, d_pl, tok_d, pl.BlockSpec((s, tn), lambda i, j: (0, j)),
        jax.ShapeDtypeStruct((d, d), BF16), pl.BlockSpec((d, tn), lambda i, j: (0, j)), (1, d // tn), d, tn, s)

    red_proj = reduce_begin([1, 2, 3], [g_wpa, g_wpl, g_wout], "proj")

    dproj_a = _attn_bwd(proj_a, after_token(slopes, red_proj), y_attn, lse, dy_attn)
    dproj_b, d_cw, d_cb, d_wa, d_ba, d_wx, d_bx, d_lam = _lru_bwd(
        proj_b, h_lru, dy_lru, conv_w_full, conv_b, wa, lru_ba, wx, lru_bx, after_token(lru_lambda, red_proj),
        dproj_b)
    per = N_SLOTS
    g_win_shape = jax.ShapeDtypeStruct((N_CHIPS, d, N_SLOTS * u), BF16)

    def g_win_part(name, dproj, first, prev):
        n_units = 4 * dproj.shape[0]
        return _mm_tn(
            name, xn, dproj, tok_d, pl.BlockSpec((None, s, u), lambda i, j: (j // 4, 0, j % 4)),
            g_win_shape, pl.BlockSpec((None, d, u), lambda i, j: ((j + first) // per, 0, (j + first) % per)),
            (1, n_units), d, u, s, aliases=None if prev is None else {2: 0}, extra=prev)

    mat_rows = heads * HEAD_DIM * HEAD_DIM // d
    vec_names = ["norm_mix_g", "conv_b", "lru_ba", "lru_bx", "lru_lambda", "norm_mlp_g", "norm_final_g"]

    def pack(wa_, wx_, cw_, vecs, name):
        rows = [wa_.reshape(mat_rows, d), wx_.reshape(mat_rows, d), cw_] + [a.reshape(1, d) for a in vecs]
        n = sum(a.shape[0] for a in rows)
        return _pack_rows(rows, n + (-n % 64), name)

    zero_cw = jnp.zeros((CONV_TAPS, d), F32)
    small_g = pack(d_wa, d_wx, d_cw, [jnp.zeros((1, d), F32), d_cb, d_ba, d_bx, d_lam, d_gmlp, d_gf], "pack_small_g")
    g_win = g_win_part("g_w_in_qkv", dproj_a, 0, None)
    g_win = g_win_part("g_w_in_rest", dproj_b, 4 * N_QKV, g_win)
    red_in = reduce_begin([0], [g_win], "w_in", everywhere=small_g)
    dxn = _dxn(dproj_a, dproj_b, w_in_g, 2 * tn, [red_in[0][3]])
    grad_x, _, d_gmix = _rms_bwd(xs, norm_mix_g, dxn, dh1, "norm_mix_bwd")

    _, done = reduce_end([4, 5], red_mlp, [grad_x], "mlp")
    _, done = reduce_end([1, 2, 3], red_proj, [done], "proj")
    small_w = pack(wa, wx, zero_cw, [norm_mix_g, conv_b, lru_ba, lru_bx, lru_lambda, norm_mlp_g, norm_final_g],
                   "pack_small_w")
    small_m = pack(m_lru_wa[0], m_lru_wx[0], zero_cw,
                   [m_norm_mix_g, m_conv_b, m_lru_ba, m_lru_bx, m_lru_lambda, m_norm_mlp_g, m_norm_final_g],
                   "pack_small_m")
    small_v = pack(v_lru_wa[0], v_lru_wx[0], zero_cw,
                   [v_norm_mix_g, v_conv_b, v_lru_ba, v_lru_bx, v_lru_lambda, v_norm_mlp_g, v_norm_final_g],
                   "pack_small_v")
    small_sum, _ = reduce_end([0], red_in, [done, small_w, small_m, small_v], "w_in")
    small = _adamw(small_w, small_sum, small_m, small_v, "adamw_small")
    g_cw = lax.dynamic_slice(small_sum[2 * mat_rows:2 * mat_rows + CONV_TAPS], (0, chip * u), (CONV_TAPS, u))
    cw_out = _adamw(conv_w[0], g_cw, m_conv_w[0], v_conv_w[0], "adamw_conv_w")
    gmix_parts = _all_gather8(jnp.pad(d_gmix, ((0, 7), (0, 0))), "gather_gain_grad", [small[1]])
    gmix_out = _sum8_adamw_row(gmix_parts, norm_mix_g, m_norm_mix_g, v_norm_mix_g, "sum_adamw_norm_mix_g")

    def small_leaf(kind, name):
        a = small[kind]
        if name == "norm_mix_g":
            return gmix_out[kind]
        if name == "lru_wa":
            return a[0:mat_rows].reshape(lru_wa.shape)
        if name == "lru_wx":
            return a[mat_rows:2 * mat_rows].reshape(lru_wx.shape)
        if name == "conv_w":
            return cw_out[kind][None]
        row = a[2 * mat_rows + CONV_TAPS + vec_names.index(name)]
        return row if name == "norm_final_g" else row[None]

    order = ["norm_mix_g", "w_in", "conv_w", "conv_b", "lru_wa", "lru_ba", "lru_wx", "lru_bx", "lru_lambda",
             "w_proj_attn", "w_proj_lru", "w_out", "norm_mlp_g", "w_up", "w_down", "norm_final_g"]
    outs = [loss, grad_x[None]]
    for kind in range(4):
        for name in order:
            outs.append(big_out[name][kind] if name in big_out else small_leaf(kind, name))
    return tuple(outs)
```

```python
import functools

import jax
import jax.numpy as jnp
from jax import lax
from jax.experimental import pallas as pl
from jax.experimental.pallas import tpu as pltpu

F32 = jnp.float32
BF16 = jnp.bfloat16
MESH = pl.DeviceIdType.MESH
AXES = ("x", "y", "c")

N_CHIPS = 4
N_DEV = 8
HEAD_DIM = 128
ATTN_BLK = 128
DILATIONS = (1, 4, 16)
ATTN_UNROLL = 8
CONV_TAPS = 4
LRU_C = 8.0
EPS = 1e-6
N_SLOTS = 7
N_QKV = 3
VMEM_MIB = 2 ** 20
VMEM_V7X = 64 * VMEM_MIB

ADAM_LR = 0.001
ADAM_B1 = 0.9
ADAM_B2 = 0.999
ADAM_EPS = 1e-08
ADAM_WD = 0.01
ADAM_STEP = 10

NN = (((1,), (0,)), ((), ()))
NT = (((1,), (1,)), ((), ()))
TN = (((0,), (0,)), ((), ()))


def _params(vmem_mib=None, **kw):
    limit = None if vmem_mib is None else min(vmem_mib * VMEM_MIB, VMEM_V7X - 8 * VMEM_MIB)
    return pltpu.CompilerParams(vmem_limit_bytes=limit, **kw)


def _row_tile(rows, row_bytes, budget=VMEM_MIB):
    t = rows
    while t % 16 == 0 and t * row_bytes > budget:
        t //= 2
    return t


def _dot(a, b, dims):
    return lax.dot_general(a.astype(BF16), b.astype(BF16), dims, preferred_element_type=F32)


def _sigmoid(x):
    return jax.nn.sigmoid(x)


def _rms_fwd(x, g, name):
    s, d = x.shape
    tm = _row_tile(s, d * 4)

    def body(x_ref, g_ref, o_ref):
        xf = x_ref[...]
        r = lax.rsqrt(jnp.mean(xf * xf, axis=-1, keepdims=True) + EPS)
        o_ref[...] = (xf * r * g_ref[...]).astype(o_ref.dtype)

    return pl.pallas_call(
        body, name=name, grid=(s // tm,),
        in_specs=[pl.BlockSpec((tm, d), lambda i: (i, 0)), pl.BlockSpec((1, d), lambda i: (0, 0))],
        out_specs=pl.BlockSpec((tm, d), lambda i: (i, 0)),
        out_shape=jax.ShapeDtypeStruct((s, d), BF16), compiler_params=_params(32),
    )(x, g)


def _rms_bwd(x, g, dy, resid, name):
    s, d = x.shape
    tm = _row_tile(s, d * 4)

    def body(x_ref, g_ref, dy_ref, res_ref, dx_ref, dxb_ref, dg_ref):
        xf = x_ref[...]
        r = lax.rsqrt(jnp.mean(xf * xf, axis=-1, keepdims=True) + EPS)
        xh = xf * r
        dyv = dy_ref[...]
        dxh = dyv * g_ref[...]
        dx = r * (dxh - xh * jnp.mean(dxh * xh, axis=-1, keepdims=True)) + res_ref[...]
        dx_ref[...] = dx
        dxb_ref[...] = dx.astype(BF16)
        part = jnp.sum(dyv * xh, axis=0, keepdims=True)

        @pl.when(pl.program_id(0) == 0)
        def _():
            dg_ref[...] = part

        @pl.when(pl.program_id(0) > 0)
        def _():
            dg_ref[...] += part

    row = pl.BlockSpec((tm, d), lambda i: (i, 0))
    vec = pl.BlockSpec((1, d), lambda i: (0, 0))
    return pl.pallas_call(
        body, name=name, grid=(s // tm,),
        in_specs=[row, vec, row, row], out_specs=[row, row, vec],
        out_shape=[jax.ShapeDtypeStruct((s, d), F32), jax.ShapeDtypeStruct((s, d), BF16),
                   jax.ShapeDtypeStruct((1, d), F32)],
        compiler_params=_params(32),
    )(x, g, dy, resid)


def _loss_head(h2, g, target):
    s, d = h2.shape
    tm = _row_tile(s, d * 4)

    def body(x_ref, g_ref, t_ref, loss_ref, dx_ref, dxb_ref, dg_ref):
        xf = x_ref[...]
        gv = g_ref[...]
        r = lax.rsqrt(jnp.mean(xf * xf, axis=-1, keepdims=True) + EPS)
        xh = xf * r
        err = xh * gv - t_ref[...]
        part = jnp.sum(jnp.sum(err * err, axis=1, keepdims=True), axis=0, keepdims=True) * (0.5 / d)
        dyv = err * (1.0 / d)
        dxh = dyv * gv
        dx = r * (dxh - xh * jnp.mean(dxh * xh, axis=-1, keepdims=True))
        dx_ref[...] = dx
        dxb_ref[...] = dx.astype(BF16)
        dgp = jnp.sum(dyv * xh, axis=0, keepdims=True)

        @pl.when(pl.program_id(0) == 0)
        def _():
            dg_ref[...] = dgp
            loss_ref[...] = jnp.broadcast_to(part, loss_ref.shape)

        @pl.when(pl.program_id(0) > 0)
        def _():
            dg_ref[...] += dgp
            loss_ref[...] += jnp.broadcast_to(part, loss_ref.shape)

    row = pl.BlockSpec((tm, d), lambda i: (i, 0))
    vec = pl.BlockSpec((1, d), lambda i: (0, 0))
    return pl.pallas_call(
        body, name="loss_head", grid=(s // tm,),
        in_specs=[row, vec, row],
        out_specs=[pl.BlockSpec((8, 128), lambda i: (0, 0)), row, row, vec],
        out_shape=[jax.ShapeDtypeStruct((8, 128), F32), jax.ShapeDtypeStruct((s, d), F32),
                   jax.ShapeDtypeStruct((s, d), BF16), jax.ShapeDtypeStruct((1, d), F32)],
        compiler_params=_params(32),
    )(h2, g, target)


def _mm(name, operands, in_specs, out_shape, out_specs, grid, dims, epilogue, nk=1, acc_shape=None,
        vmem_mib=56, aliases=None):
    n_in = len(operands)
    n_out = len(out_shape)

    def body(*refs):
        a_ref, b_ref = refs[0], refs[1]
        extras = refs[2:n_in]
        outs = refs[n_in:n_in + n_out]

        def prod():
            return _dot(a_ref[...], b_ref[...], dims)

        if nk == 1:
            epilogue(prod(), extras, outs)
        else:
            acc = refs[n_in + n_out]
            k = pl.program_id(2)

            @pl.when(k == 0)
            def _():
                acc[...] = prod()

            @pl.when(k > 0)
            def _():
                acc[...] += prod()

            @pl.when(k == nk - 1)
            def _():
                epilogue(acc[...], extras, outs)

    scratch = [] if nk == 1 else [pltpu.VMEM(acc_shape, F32)]
    return pl.pallas_call(
        body, name=name, grid=grid, in_specs=in_specs, out_specs=out_specs, out_shape=out_shape,
        scratch_shapes=scratch, input_output_aliases=aliases or {},
        compiler_params=_params(vmem_mib),
    )(*operands)


def _store(acc, extras, outs):
    outs[0][...] = acc.astype(outs[0].dtype)


def _proj_in(xn, w_in_g, first_unit, n_units, n_slots):
    s, d = xn.shape
    u = d // 4
    per = N_SLOTS

    return _mm(
        "proj_in_%d" % first_unit, [xn, w_in_g],
        [pl.BlockSpec((s, d), lambda i, j, k: (0, 0)),
         pl.BlockSpec((None, d, u), lambda i, j, k: ((j + first_unit) // per, 0, (j + first_unit) % per))],
        [jax.ShapeDtypeStruct((n_slots, s, d), F32)],
        [pl.BlockSpec((None, s, u), lambda i, j, k: (j // 4, 0, j % 4))],
        (1, n_units, 1), NN, _store)[0]


def _mm_nn(name, a, b, extras, extra_specs, out_shape, out_specs, epilogue, tn, aliases=None):
    s, kdim = a.shape
    n = b.shape[1]
    return _mm(
        name, [a, b] + list(extras),
        [pl.BlockSpec((s, kdim), lambda i, j, k: (0, 0)), pl.BlockSpec((kdim, tn), lambda i, j, k: (0, j))]
        + list(extra_specs),
        out_shape, out_specs, (1, n // tn, 1), NN, epilogue, aliases=aliases)


def _mm_nt(name, a, b, extras, extra_specs, out_shape, out_specs, epilogue, tn, aliases=None):
    s, kdim = a.shape
    n = b.shape[0]
    return _mm(
        name, [a, b] + list(extras),
        [pl.BlockSpec((s, kdim), lambda i, j, k: (0, 0)), pl.BlockSpec((tn, kdim), lambda i, j, k: (j, 0))]
        + list(extra_specs),
        out_shape, out_specs, (1, n // tn, 1), NT, epilogue, aliases=aliases)


def _mm_tn(name, a, b, a_spec, b_spec, out_shape, out_spec, grid, m, tn, s, aliases=None, extra=None):
    ch = 256
    n_in = 2 if extra is None else 3

    def body(*refs):
        a_ref, b_ref = refs[0], refs[1]
        o_ref, at_ref = refs[n_in], refs[n_in + 1]

        @pl.when(pl.program_id(1) == 0)
        def _():
            for c0 in range(0, s, ch):
                at_ref[:, c0:c0 + ch] = a_ref[c0:c0 + ch, :].astype(F32).T.astype(BF16)

        o_ref[...] = _dot(at_ref[...], b_ref[...], NN).astype(o_ref.dtype)

    operands = [a, b] + ([] if extra is None else [extra])
    in_specs = [a_spec, b_spec] + ([] if extra is None else [pl.BlockSpec(memory_space=pl.ANY)])
    return pl.pallas_call(
        body, name=name, grid=grid, in_specs=in_specs, out_specs=out_spec, out_shape=out_shape,
        scratch_shapes=[pltpu.VMEM((m, s), BF16)], input_output_aliases=aliases or {},
        compiler_params=_params(56),
    )(*operands)


def _dxn(dproj_a, dproj_b, w_in_g, tn, after):
    n_a, s, d = dproj_a.shape
    u = d // 4
    ua = 4 * n_a
    nk = 4 * N_SLOTS
    per = N_SLOTS

    def body(a_ref, b_ref, w_ref, *rest):
        o_ref = rest[len(after)]
        k = pl.program_id(2)

        @pl.when(k == 0)
        def _():
            o_ref[...] = jnp.zeros_like(o_ref)

        @pl.when(k < ua)
        def _():
            o_ref[...] += _dot(a_ref[...], w_ref[...], NT)

        @pl.when(k >= ua)
        def _():
            o_ref[...] += _dot(b_ref[...], w_ref[...], NT)

    def a_map(i, j, k):
        kk = jnp.minimum(k, ua - 1)
        return (kk // 4, 0, kk % 4)

    def b_map(i, j, k):
        kk = jnp.maximum(k - ua, 0)
        return (kk // 4, 0, kk % 4)

    return pl.pallas_call(
        body, name="dxn", grid=(1, d // tn, nk),
        in_specs=[pl.BlockSpec((None, s, u), a_map), pl.BlockSpec((None, s, u), b_map),
                  pl.BlockSpec((None, tn, u), lambda i, j, k: (k // per, j, k % per))] + [HBM] * len(after),
        out_specs=pl.BlockSpec((s, tn), lambda i, j, k: (0, j)),
        out_shape=jax.ShapeDtypeStruct((s, d), F32),
        compiler_params=_params(48),
    )(dproj_a, dproj_b, w_in_g, *after)


def _attn_masks(slope, dil):
    ii = lax.broadcasted_iota(jnp.int32, (ATTN_BLK, 2 * ATTN_BLK), 0)
    jj = lax.broadcasted_iota(jnp.int32, (ATTN_BLK, 2 * ATTN_BLK), 1)
    diff = ATTN_BLK + ii - jj
    band = (diff >= 0) & (diff <= ATTN_BLK)
    bias = -(slope * float(dil)) * diff.astype(F32)
    return band, bias, jj


def _attn_window(t, nblk):
    cur = pl.ds(pl.multiple_of(t * ATTN_BLK, ATTN_BLK), ATTN_BLK)
    prev = pl.ds(pl.multiple_of(jnp.maximum(t - 1, 0) * ATTN_BLK, ATTN_BLK), ATTN_BLK)
    first = jnp.where(t % nblk == 0, ATTN_BLK, 0)
    return prev, cur, first


def _unrolled_loop(n, step, init):
    def trip(i, carry):
        for k in range(ATTN_UNROLL):
            carry = step(i * ATTN_UNROLL + k, carry)
        return carry

    return lax.fori_loop(0, n // ATTN_UNROLL, trip, init)


def _streams(pairs, dil, s):
    if dil == 1:
        return [src for _, src in pairs]
    seg = s // dil
    for dst, src in pairs:
        for r in range(dil):
            dst[r * seg:(r + 1) * seg, :] = src[pl.ds(r, seg, stride=dil), :].astype(dst.dtype)
    return [dst for dst, _ in pairs]


def _attn_fwd(proj_a, slopes):
    _, s, d = proj_a.shape
    heads = d // HEAD_DIM
    scale = HEAD_DIM ** -0.5
    n_t = s // ATTN_BLK
    ng = len(DILATIONS)

    def body(q_ref, k_ref, v_ref, sl_ref, o_ref, lse_ref, qd, kd, vd, od, ld, og, lg):
        slope = sl_ref[...][:, :1]
        for g, dil in enumerate(DILATIONS):
            nblk = s // dil // ATTN_BLK
            qs, ks, vs = _streams([(qd, q_ref), (kd, k_ref), (vd, v_ref)], dil, s)
            o_t, l_t = (og.at[g], lg.at[g]) if dil == 1 else (od, ld)
            band, bias, jj = _attn_masks(slope, dil)

            def blk(t, carry, nblk=nblk, band=band, bias=bias, jj=jj, qs=qs, ks=ks, vs=vs, o_t=o_t, l_t=l_t):
                prev, cur, first = _attn_window(t, nblk)
                kk = jnp.concatenate([ks[prev, :], ks[cur, :]], axis=0)
                vv = jnp.concatenate([vs[prev, :], vs[cur, :]], axis=0)
                sc = _dot(qs[cur, :], kk, NT) * scale + bias
                sc = jnp.where(band & (jj >= first), sc, -jnp.inf)
                m = jnp.max(sc, axis=1, keepdims=True)
                p = jnp.exp(sc - m)
                l = jnp.sum(p, axis=1, keepdims=True)
                o_t[cur, :] = _dot(p, vv, NN) / l
                l_t[cur, :] = jnp.broadcast_to(m + jnp.log(l), (ATTN_BLK, HEAD_DIM))
                return carry

            _unrolled_loop(n_t, blk, 0)
            seg = s // dil
            if dil > 1:
                for r in range(dil):
                    og[g, pl.ds(r, seg, stride=dil), :] = od[r * seg:(r + 1) * seg, :]
                    lg[g, pl.ds(r, seg, stride=dil), :] = ld[r * seg:(r + 1) * seg, :]

        ch = 256

        def combine(c, carry):
            rows = pl.ds(pl.multiple_of(c * ch, ch), ch)
            ls = [lg[g, rows, :] for g in range(ng)]
            mx = functools.reduce(jnp.maximum, ls)
            es = [jnp.exp(x - mx) for x in ls]
            den = functools.reduce(jnp.add, es)
            num = functools.reduce(jnp.add, [es[g] * og[g, rows, :] for g in range(ng)])
            o_ref[rows, :] = (num / den).astype(o_ref.dtype)
            lse_ref[rows, :] = mx + jnp.log(den)
            return carry

        lax.fori_loop(0, s // ch, combine, 0)

    def col(slot):
        return pl.BlockSpec((None, s, HEAD_DIM), lambda h: (slot, 0, h))

    head = pl.BlockSpec((s, HEAD_DIM), lambda h: (0, h))
    return pl.pallas_call(
        body, name="attn_fwd", grid=(heads,),
        in_specs=[col(0), col(1), col(2), pl.BlockSpec((None, 1, HEAD_DIM), lambda h: (h, 0, 0))],
        out_specs=[head, head],
        out_shape=[jax.ShapeDtypeStruct((s, d), BF16), jax.ShapeDtypeStruct((s, d), F32)],
        scratch_shapes=[pltpu.VMEM((s, HEAD_DIM), BF16)] * 3 + [pltpu.VMEM((s, HEAD_DIM), F32)] * 2
        + [pltpu.VMEM((ng, s, HEAD_DIM), F32)] * 2,
        compiler_params=_params(40),
    )(proj_a, proj_a, proj_a, slopes)


def _attn_bwd(proj_a, slopes, y_attn, lse, dy):
    _, s, d = proj_a.shape
    heads = d // HEAD_DIM
    scale = HEAD_DIM ** -0.5
    n_t = s // ATTN_BLK

    def body(q_ref, k_ref, v_ref, sl_ref, o_ref, lse_ref, dy_ref, out_ref,
             qd, kd, vd, dod, lsd, dld, delta, dqd, dkd, dvd, dqa, dka, dva):
        slope = sl_ref[...][:, :1]
        dyv = dy_ref[...]
        delta[...] = jnp.broadcast_to(
            jnp.sum(dyv * o_ref[...].astype(F32), axis=1, keepdims=True), (s, HEAD_DIM))
        for g, dil in enumerate(DILATIONS):
            nblk = s // dil // ATTN_BLK
            seg = s // dil
            qs, ks, vs, dos, lss, dls = _streams(
                [(qd, q_ref), (kd, k_ref), (vd, v_ref), (dod, dy_ref), (lsd, lse_ref), (dld, delta)], dil, s)
            dq_t, dk_t, dv_t = (dqa, dka, dva) if dil == 1 else (dqd, dkd, dvd)
            band, bias, jj = _attn_masks(slope, dil)

            def blk(t, carry, nblk=nblk, band=band, bias=bias, jj=jj, qs=qs, ks=ks, vs=vs, dos=dos, lss=lss,
                    dls=dls, dq_t=dq_t, dk_t=dk_t, dv_t=dv_t):
                ck, cv = carry
                prev, cur, first = _attn_window(t, nblk)
                q = qs[cur, :]
                do = dos[cur, :]
                lse_b = lss[cur, :]
                dl_b = dls[cur, :]
                kk = jnp.concatenate([ks[prev, :], ks[cur, :]], axis=0)
                vv = jnp.concatenate([vs[prev, :], vs[cur, :]], axis=0)
                sc = _dot(q, kk, NT) * scale + bias
                p = jnp.where(band & (jj >= first), jnp.exp(sc - jnp.concatenate([lse_b, lse_b], axis=1)), 0.0)
                dp = _dot(do, vv, NT)
                ds = p * (dp - jnp.concatenate([dl_b, dl_b], axis=1))
                dv_b = _dot(p, do, TN)
                dk_b = _dot(ds, q, TN) * scale
                dq_t[cur, :] = _dot(ds, kk, NN) * scale
                @pl.when(t > 0)
                def _():
                    dk_t[prev, :] = ck + dk_b[:ATTN_BLK]
                    dv_t[prev, :] = cv + dv_b[:ATTN_BLK]

                return dk_b[ATTN_BLK:], dv_b[ATTN_BLK:]

            zero = jnp.zeros((ATTN_BLK, HEAD_DIM), F32)
            ck, cv = _unrolled_loop(n_t, blk, (zero, zero))
            dk_t[(n_t - 1) * ATTN_BLK:, :] = ck
            dv_t[(n_t - 1) * ATTN_BLK:, :] = cv
            if dil > 1:
                for acc, part in ((dqa, dqd), (dka, dkd), (dva, dvd)):
                    for r in range(dil):
                        acc[pl.ds(r, seg, stride=dil), :] += part[r * seg:(r + 1) * seg, :]
        out_ref[0] = dqa[...].astype(out_ref.dtype)
        out_ref[1] = dka[...].astype(out_ref.dtype)
        out_ref[2] = dva[...].astype(out_ref.dtype)

    def col(slot):
        return pl.BlockSpec((None, s, HEAD_DIM), lambda h: (slot, 0, h))

    head = pl.BlockSpec((s, HEAD_DIM), lambda h: (0, h))
    return pl.pallas_call(
        body, name="attn_bwd", grid=(heads,),
        in_specs=[col(0), col(1), col(2), pl.BlockSpec((None, 1, HEAD_DIM), lambda h: (h, 0, 0)),
                  head, head, head],
        out_specs=pl.BlockSpec((N_QKV, s, HEAD_DIM), lambda h: (0, 0, h)),
        out_shape=jax.ShapeDtypeStruct((N_QKV, s, d), BF16),
        scratch_shapes=[pltpu.VMEM((s, HEAD_DIM), BF16)] * 4 + [pltpu.VMEM((s, HEAD_DIM), F32)] * 9,
        compiler_params=_params(48),
    )(proj_a, proj_a, proj_a, slopes, y_attn, lse, dy)


def _expm1(x):
    small = x * (1.0 + x * (0.5 + x * (1.0 / 6.0 + x * (1.0 / 24.0 + x * (1.0 / 120.0)))))
    return jnp.where(jnp.abs(x) < 0.1, small, jnp.exp(x) - 1.0)


def _softplus(x):
    return jnp.maximum(x, 0.0) + jnp.log1p(jnp.exp(-jnp.abs(x)))


GELU_K = 0.7978845608028654
GELU_C = 0.044715


def _gelu(x):
    t = jnp.tanh(GELU_K * (x + GELU_C * x * x * x))
    return 0.5 * x * (1.0 + t), t


def _gelu_grad(x, t):
    return 0.5 * (1.0 + t) + 0.5 * x * (1.0 - t * t) * GELU_K * (1.0 + 3.0 * GELU_C * x * x)


def _lru_gates(xc, wa, ba, wx, bx, sp):
    r = _sigmoid(_dot(xc, wa, NN) + ba)
    ig = _sigmoid(_dot(xc, wx, NN) + bx)
    log_a = -LRU_C * r * sp
    a = jnp.exp(log_a)
    mult = jnp.sqrt(-_expm1(2.0 * log_a))
    return r, ig, a, mult


def _scan_fwd(a, u, tt):
    row = lax.broadcasted_iota(jnp.int32, a.shape, 0)
    sh = 1
    while sh < tt:
        keep = row >= sh
        a_s = jnp.where(keep, pltpu.roll(a, sh, 0), 1.0)
        u_s = jnp.where(keep, pltpu.roll(u, sh, 0), 0.0)
        u = a * u_s + u
        a = a * a_s
        sh *= 2
    return a, u


def _scan_bwd(b, g, tt):
    row = lax.broadcasted_iota(jnp.int32, b.shape, 0)
    sh = 1
    while sh < tt:
        keep = row < tt - sh
        b_s = jnp.where(keep, pltpu.roll(b, tt - sh, 0), 1.0)
        g_s = jnp.where(keep, pltpu.roll(g, tt - sh, 0), 0.0)
        g = g + b * g_s
        b = b * b_s
        sh *= 2
    return b, g


def _conv_rows(xpad_ref, cw, cb, s):
    acc = cb
    for j in range(CONV_TAPS):
        off = 8 - (CONV_TAPS - 1) + j
        acc = acc + cw[j:j + 1, :] * xpad_ref[off:off + s, :]
    return acc


LRU_TILE = 128


def _lru_specs(s, d):
    heads = d // HEAD_DIM

    def col(slot):
        return pl.BlockSpec((None, s, HEAD_DIM), lambda h: (slot, 0, h))

    vec = pl.BlockSpec((1, HEAD_DIM), lambda h: (0, h))
    mat = pl.BlockSpec((None, HEAD_DIM, HEAD_DIM), lambda h: (h, 0, 0))
    cw = pl.BlockSpec((8, HEAD_DIM), lambda h: (0, h))
    head = pl.BlockSpec((s, HEAD_DIM), lambda h: (0, h))
    return heads, col, vec, mat, cw, head


def _lru_fwd(proj_b, conv_w, conv_b, wa, ba, wx, bx, lam):
    _, s, d = proj_b.shape
    heads, col, vec, mat, cws, head = _lru_specs(s, d)
    tt = LRU_TILE

    def body(xr_ref, xg_ref, cw_ref, cb_ref, wa_ref, ba_ref, wx_ref, bx_ref, lam_ref, y_ref, h_ref, xpad, xc_s):
        xpad[0:8, :] = jnp.zeros((8, HEAD_DIM), F32)
        xpad[8:8 + s, :] = xr_ref[...]
        xc_s[...] = _conv_rows(xpad, cw_ref[...], cb_ref[...], s)
        sp = _softplus(-lam_ref[...])
        wav, wxv, bav, bxv = wa_ref[...], wx_ref[...], ba_ref[...], bx_ref[...]

        def tile(i, hc):
            rows = pl.ds(pl.multiple_of(i * tt, tt), tt)
            xc = xc_s[rows, :]
            _, ig, a, mult = _lru_gates(xc, wav, bav, wxv, bxv, sp)
            pa, hl = _scan_fwd(a, mult * (ig * xc), tt)
            h = hl + pa * hc
            h_ref[rows, :] = h
            gel, _ = _gelu(xg_ref[rows, :])
            y_ref[rows, :] = (h * gel).astype(y_ref.dtype)
            return h[tt - 1:tt, :]

        lax.fori_loop(0, s // tt, tile, jnp.zeros((1, HEAD_DIM), F32))

    return pl.pallas_call(
        body, name="lru_fwd", grid=(heads,),
        in_specs=[col(0), col(1), cws, vec, mat, vec, mat, vec, vec],
        out_specs=[head, head],
        out_shape=[jax.ShapeDtypeStruct((s, d), BF16), jax.ShapeDtypeStruct((s, d), F32)],
        scratch_shapes=[pltpu.VMEM((s + 8, HEAD_DIM), F32), pltpu.VMEM((s, HEAD_DIM), F32)],
        compiler_params=_params(32),
    )(proj_b, proj_b, conv_w, conv_b, wa, ba, wx, bx, lam)


def _lru_bwd(proj_b, h_lru, dy, conv_w, conv_b, wa, ba, wx, bx, lam, dproj_b):
    _, s, d = proj_b.shape
    heads, col, vec, mat, cws, head = _lru_specs(s, d)
    tt = LRU_TILE
    n_t = s // tt

    def body(xr_ref, xg_ref, h_ref, dy_ref, cw_ref, cb_ref, wa_ref, ba_ref, wx_ref, bx_ref, lam_ref, alias_ref,
             out_ref, dcw_ref, dcb_ref, dwa_ref, dba_ref, dwx_ref, dbx_ref, dlam_ref, xpad, xc_s, dxc_s):
        del alias_ref
        xpad[0:8, :] = jnp.zeros((8, HEAD_DIM), F32)
        xpad[8:8 + s, :] = xr_ref[...]
        cwv = cw_ref[...]
        xc_s[...] = _conv_rows(xpad, cwv, cb_ref[...], s)
        dxc_s[s:s + 8, :] = jnp.zeros((8, HEAD_DIM), F32)
        lamv = lam_ref[...]
        sp = _softplus(-lamv)
        wav, wxv, bav, bxv = wa_ref[...], wx_ref[...], ba_ref[...], bx_ref[...]
        dwa_ref[...] = jnp.zeros_like(dwa_ref)
        dwx_ref[...] = jnp.zeros_like(dwx_ref)
        zero = jnp.zeros((1, HEAD_DIM), F32)
        row = lax.broadcasted_iota(jnp.int32, (tt, HEAD_DIM), 0)

        def tile(it, carry):
            dh_next, a_next, dba, dbx, dsp, dcb = carry
            i = n_t - 1 - it
            t0 = pl.multiple_of(i * tt, tt)
            rows = pl.ds(t0, tt)
            xc = xc_s[rows, :]
            r, ig, a, mult = _lru_gates(xc, wav, bav, wxv, bxv, sp)
            h = h_ref[rows, :]
            before = h_ref[pl.ds(pl.multiple_of(jnp.maximum(t0 - 8, 0), 8), 8), :][7:8, :]
            before = before * (i > 0).astype(F32)
            h_prev = jnp.where(row == 0, before, pltpu.roll(h, 1, 0))
            xg = xg_ref[rows, :]
            dyv = dy_ref[rows, :]
            gel, th = _gelu(xg)
            out_ref[1, rows, :] = (dyv * h * _gelu_grad(xg, th)).astype(out_ref.dtype)
            b = jnp.where(row == tt - 1, a_next, pltpu.roll(a, tt - 1, 0))
            pb, z = _scan_bwd(b, dyv * gel, tt)
            dh = z + pb * dh_next
            da = dh * h_prev
            dmult = dh * (ig * xc)
            dig = dh * (mult * xc)
            dla = da * a - dmult * (a * a / mult)
            dzr = dla * (-LRU_C * sp) * (r * (1.0 - r))
            dzx = dig * (ig * (1.0 - ig))
            dxc = dh * (mult * ig) + _dot(dzr, wav, NT) + _dot(dzx, wxv, NT)
            dxc_s[rows, :] = dxc
            dwa_ref[...] += _dot(xc, dzr, TN)
            dwx_ref[...] += _dot(xc, dzx, TN)
            return (dh[0:1, :], a[0:1, :],
                    dba + jnp.sum(dzr, axis=0, keepdims=True),
                    dbx + jnp.sum(dzx, axis=0, keepdims=True),
                    dsp + jnp.sum(dla * (-LRU_C * r), axis=0, keepdims=True),
                    dcb + jnp.sum(dxc, axis=0, keepdims=True))

        _, _, dba, dbx, dsp, dcb = lax.fori_loop(0, n_t, tile, (zero, zero, zero, zero, zero, zero))
        dba_ref[...] = dba
        dbx_ref[...] = dbx
        dcb_ref[...] = dcb
        dlam_ref[...] = -dsp * _sigmoid(-lamv)
        dxr = jnp.zeros((s, HEAD_DIM), F32)
        for j in range(CONV_TAPS):
            back = CONV_TAPS - 1 - j
            off = 8 - back
            dcw_ref[j:j + 1, :] = jnp.sum(dxc_s[0:s, :] * xpad[off:off + s, :], axis=0, keepdims=True)
            dxr = dxr + cwv[j:j + 1, :] * dxc_s[back:back + s, :]
        out_ref[0] = dxr.astype(out_ref.dtype)

    return pl.pallas_call(
        body, name="lru_bwd", grid=(heads,),
        in_specs=[col(0), col(1), head, head, cws, vec, mat, vec, mat, vec, vec, pl.BlockSpec(memory_space=pl.ANY)],
        out_specs=[pl.BlockSpec((2, s, HEAD_DIM), lambda h: (0, 0, h)),
                   pl.BlockSpec((CONV_TAPS, HEAD_DIM), lambda h: (0, h)), vec, mat, vec, mat, vec, vec],
        out_shape=[jax.ShapeDtypeStruct(dproj_b.shape, dproj_b.dtype),
                   jax.ShapeDtypeStruct((CONV_TAPS, d), F32), jax.ShapeDtypeStruct((1, d), F32),
                   jax.ShapeDtypeStruct(wa.shape, F32), jax.ShapeDtypeStruct((1, d), F32),
                   jax.ShapeDtypeStruct(wx.shape, F32), jax.ShapeDtypeStruct((1, d), F32),
                   jax.ShapeDtypeStruct((1, d), F32)],
        scratch_shapes=[pltpu.VMEM((s + 8, HEAD_DIM), F32), pltpu.VMEM((s, HEAD_DIM), F32),
                        pltpu.VMEM((s + 8, HEAD_DIM), F32)],
        input_output_aliases={11: 0},
        compiler_params=_params(32),
    )(proj_b, proj_b, h_lru, dy, conv_w, conv_b, wa, ba, wx, bx, lam, dproj_b)


def _place():
    x, y, c = (lax.axis_index(n) for n in AXES)
    return x, y, c


def _other_chips(x, y):
    return [(1 - x, y), (x, 1 - y), (1 - x, 1 - y)]


HBM = pl.BlockSpec(memory_space=pl.ANY)


def _cast_shard(w, chip_arr, name):
    r, cols = w.shape
    rh = r // 2
    tr = _row_tile(rh, cols * 4)
    nt = rh // tr

    def body(chip_ref, w_ref, o_ref):
        del chip_ref
        o_ref[...] = w_ref[...].astype(BF16)

    return pl.pallas_call(
        body, name=name,
        grid_spec=pltpu.PrefetchScalarGridSpec(
            num_scalar_prefetch=1, grid=(2, nt),
            in_specs=[pl.BlockSpec((tr, cols), lambda h, i, chip_ref: (h * nt + i, 0))],
            out_specs=pl.BlockSpec((None, None, tr, cols), lambda h, i, chip_ref: (chip_ref[0], h, i, 0))),
        out_shape=jax.ShapeDtypeStruct((N_CHIPS, 2, rh, cols), BF16), compiler_params=_params(32),
    )(chip_arr, w)


HBM_SPEC = pl.BlockSpec(memory_space=pltpu.HBM)
SEM_SPEC = pl.BlockSpec(memory_space=pltpu.SEMAPHORE)
EFFECT = pltpu.SideEffectType.DATAFLOW_SIDE_EFFECTING
TOKEN = jax.ShapeDtypeStruct((8, 128), F32)
TOKEN_SPEC = pl.BlockSpec(memory_space=pltpu.VMEM)


def _in_hbm(arrays):
    return [pltpu.with_memory_space_constraint(a, pltpu.HBM) for a in arrays]


def _hbm_like(arrays):
    return [pltpu.HBM(a.shape, a.dtype) for a in arrays]


def _sems(n):
    return pltpu.SemaphoreType.DMA((n,))


def _remote(src, dst, send_sem, recv_sem, to):
    return pltpu.make_async_remote_copy(src_ref=src, dst_ref=dst, send_sem=send_sem, recv_sem=recv_sem,
                                        device_id=to, device_id_type=MESH)


def _gather_start(bufs, groups, after, name):
    n = len(bufs)
    ng = len(groups)

    def body(*refs):
        ins = refs[:n]
        sems = refs[n + len(after):n + len(after) + 2 * ng]
        x, y, c = _place()
        me = 2 * x + y
        for g, ws in enumerate(groups):
            for i, w in enumerate(ws):
                for j, (cx, cy) in enumerate(_other_chips(x, y)):
                    mine = ins[w].at[me, c]
                    _remote(mine, mine, sems[2 * g].at[3 * i + j], sems[2 * g + 1].at[3 * i + j], (cx, cy, c)).start()

    sem_shapes = []
    for ws in groups:
        sem_shapes += [_sems(3 * len(ws)), _sems(3 * len(ws))]
    res = pl.pallas_call(
        body, name=name, in_specs=[HBM_SPEC] * n + [HBM] * len(after),
        out_specs=[SEM_SPEC] * (2 * ng) + [HBM_SPEC] * n, out_shape=sem_shapes + _hbm_like(bufs),
        input_output_aliases={w: 2 * ng + w for w in range(n)},
        compiler_params=pltpu.CompilerParams(has_side_effects=EFFECT),
    )(*_in_hbm(bufs), *after)
    return [(res[2 * g], res[2 * g + 1]) for g in range(ng)], list(res[2 * ng:])


def _gather_forward(bufs, recv, after, name):
    m = len(bufs)

    def body(*refs):
        ins, recv_in = refs[:m], refs[m]
        fsend, frecv = refs[m + 1 + len(after)], refs[m + 2 + len(after)]
        x, y, c = _place()
        for j, (cx, cy) in enumerate(_other_chips(x, y)):
            for i in range(m):
                landed = ins[i].at[2 * cx + cy, c]
                k = 3 * i + j
                _remote(landed, landed, fsend.at[k], recv_in.at[k], (cx, cy, c)).wait_recv()
                _remote(landed, landed, fsend.at[k], frecv.at[k], (x, y, 1 - c)).start()

    res = pl.pallas_call(
        body, name=name, in_specs=[HBM_SPEC] * m + [SEM_SPEC] + [HBM] * len(after),
        out_specs=[SEM_SPEC, SEM_SPEC] + [HBM_SPEC] * m, out_shape=[_sems(3 * m), _sems(3 * m)] + _hbm_like(bufs),
        input_output_aliases={i: 2 + i for i in range(m)},
        compiler_params=pltpu.CompilerParams(has_side_effects=EFFECT),
    )(*bufs, recv, *after)
    return (res[0], res[1]), list(res[2:])


def _gather_finish(bufs, send, fsend, frecv, after, name):
    m = len(bufs)

    def body(*refs):
        ins = refs[:m]
        send_in, fsend_in, frecv_in = refs[m:m + 3]
        x, y, c = _place()
        me = 2 * x + y
        for j, (cx, cy) in enumerate(_other_chips(x, y)):
            cj = 2 * cx + cy
            for i in range(m):
                k = 3 * i + j
                mine = ins[i].at[me, c]
                _remote(mine, mine, send_in.at[k], frecv_in.at[k], (cx, cy, c)).wait_send()
                landed = ins[i].at[cj, c]
                _remote(landed, landed, fsend_in.at[k], frecv_in.at[k], (x, y, 1 - c)).wait_send()
                theirs = ins[i].at[cj, 1 - c]
                _remote(theirs, theirs, fsend_in.at[k], frecv_in.at[k], (x, y, 1 - c)).wait_recv()

    return list(pl.pallas_call(
        body, name=name, in_specs=[HBM_SPEC] * m + [SEM_SPEC] * 3 + [HBM] * len(after),
        out_specs=[HBM_SPEC] * m, out_shape=_hbm_like(bufs),
        input_output_aliases={i: i for i in range(m)},
        compiler_params=pltpu.CompilerParams(has_side_effects=EFFECT),
    )(*bufs, send, fsend, frecv, *after))


def _pair_exchange(grads, name):
    n = len(grads)

    def body(*refs):
        ins, outs = refs[:n], refs[n:2 * n]
        send_sems, recv_sems = refs[2 * n:]
        x, y, c = _place()
        sibling = (x, y, 1 - c)
        cps = []
        for w in range(n):
            for j in range(N_CHIPS):
                cp = pltpu.make_async_remote_copy(
                    src_ref=ins[w].at[j, 1 - c], dst_ref=outs[w].at[j], send_sem=send_sems.at[N_CHIPS * w + j],
                    recv_sem=recv_sems.at[N_CHIPS * w + j], device_id=sibling, device_id_type=MESH)
                cp.start()
                cps.append(cp)
        for cp in cps:
            cp.wait()

    return pl.pallas_call(
        body, name=name, in_specs=[HBM] * n, out_specs=[HBM] * n,
        out_shape=[jax.ShapeDtypeStruct((N_CHIPS,) + a.shape[2:], a.dtype) for a in grads],
        scratch_shapes=[pltpu.SemaphoreType.DMA((N_CHIPS * n,)), pltpu.SemaphoreType.DMA((N_CHIPS * n,))],
    )(*grads)


def _chip_start(sums, name, to_all=()):
    m = len(sums)
    lands = [lax.empty(((N_CHIPS,) if i in to_all else ()) + a.shape, a.dtype) for i, a in enumerate(sums)]

    def body(*refs):
        ins, land_in = refs[:m], refs[m:2 * m]
        send, recv = refs[2 * m], refs[2 * m + 1]
        token = refs[4 * m + 2]
        x, y, c = _place()
        me = 2 * x + y
        for i in sorted(range(m), key=lambda i: i not in to_all):
            for j, (cx, cy) in enumerate(_other_chips(x, y)):
                src = ins[i] if i in to_all else ins[i].at[2 * cx + cy]
                _remote(src, land_in[i].at[me], send.at[3 * i + j], recv.at[3 * i + j], (cx, cy, c)).start()
        token[...] = jnp.zeros_like(token)

    res = pl.pallas_call(
        body, name=name, in_specs=[HBM_SPEC] * (2 * m),
        out_specs=[SEM_SPEC, SEM_SPEC] + [HBM_SPEC] * (2 * m) + [TOKEN_SPEC],
        out_shape=[_sems(3 * m), _sems(3 * m)] + _hbm_like(sums) + _hbm_like(lands) + [TOKEN],
        input_output_aliases={i: 2 + i for i in range(2 * m)},
        compiler_params=pltpu.CompilerParams(has_side_effects=EFFECT),
    )(*_in_hbm(sums), *_in_hbm(lands))
    return (res[0], res[1]), list(res[2:2 + m]), list(res[2 + m:2 + 2 * m]), res[2 + 2 * m]


def _chip_wait(sems, sums, lands, after, name, to_all=()):
    m = len(sums)

    def body(*refs):
        ins, land_in = refs[:m], refs[m:2 * m]
        send_in, recv_in = refs[2 * m], refs[2 * m + 1]
        x, y, c = _place()
        for i in range(m):
            for j, (cx, cy) in enumerate(_other_chips(x, y)):
                cj = 2 * cx + cy
                src = ins[i] if i in to_all else ins[i].at[cj]
                cp = _remote(src, land_in[i].at[cj], send_in.at[3 * i + j], recv_in.at[3 * i + j], (cx, cy, c))
                cp.wait_send()
                cp.wait_recv()

    res = pl.pallas_call(
        body, name=name, in_specs=[HBM_SPEC] * (2 * m) + [SEM_SPEC, SEM_SPEC] + [HBM] * len(after),
        out_specs=[HBM_SPEC] * (2 * m), out_shape=_hbm_like(sums) + _hbm_like(lands),
        input_output_aliases={i: i for i in range(2 * m)},
        compiler_params=pltpu.CompilerParams(has_side_effects=EFFECT),
    )(*sums, *lands, sems[0], sems[1], *after)
    return list(res[:m]), list(res[m:])


def _half_exchange(bufs, name):
    n = len(bufs)
    parts = []
    for w, a in enumerate(bufs):
        parts += [(w, None)] if a.ndim == 3 else [(w, j) for j in range(a.shape[0])]
    np_ = len(parts)

    def body(*refs):
        outs = refs[n:2 * n]
        send_sems, recv_sems = refs[2 * n:]
        x, y, c = _place()
        sibling = (x, y, 1 - c)

        def half(w, j, h):
            return outs[w].at[h] if j is None else outs[w].at[j, h]

        cps = []
        for k, (w, j) in enumerate(parts):
            rc = _remote(half(w, j, c), half(w, j, c), send_sems.at[k], recv_sems.at[k], sibling)
            rc.start()
            cps.append(rc)
        for k, (w, j) in enumerate(parts):
            theirs = half(w, j, 1 - c)
            _remote(theirs, theirs, send_sems.at[k], recv_sems.at[k], sibling).wait_recv()
        for cp in cps:
            cp.wait_send()

    return pl.pallas_call(
        body, name=name, in_specs=[HBM] * n, out_specs=[HBM] * n,
        out_shape=[jax.ShapeDtypeStruct(a.shape, a.dtype) for a in bufs],
        input_output_aliases={w: w for w in range(n)},
        scratch_shapes=[_sems(np_), _sems(np_)],
    )(*bufs)


def _all_gather8(block, name, after=()):
    def body(in_ref, *rest):
        out_ref, send_sems, recv_sems, local_sem = rest[len(after):]
        x, y, c = _place()
        me = 4 * x + 2 * y + c
        mine = pltpu.make_async_copy(in_ref, out_ref.at[me], local_sem)
        mine.start()
        flips = [(fx, fy, fc) for fx in (0, 1) for fy in (0, 1) for fc in (0, 1)][1:]
        cps = []
        for k, (fx, fy, fc) in enumerate(flips):
            cp = pltpu.make_async_remote_copy(
                src_ref=in_ref, dst_ref=out_ref.at[me], send_sem=send_sems.at[k], recv_sem=recv_sems.at[k],
                device_id=(x ^ fx, y ^ fy, c ^ fc), device_id_type=MESH)
            cp.start()
            cps.append(cp)
        for k, (fx, fy, fc) in enumerate(flips):
            px, py, pc = x ^ fx, y ^ fy, c ^ fc
            theirs = out_ref.at[4 * px + 2 * py + pc]
            pltpu.make_async_remote_copy(
                src_ref=theirs, dst_ref=theirs, send_sem=send_sems.at[k], recv_sem=recv_sems.at[k],
                device_id=(px, py, pc), device_id_type=MESH).wait_recv()
        for cp in cps:
            cp.wait_send()
        mine.wait()

    return pl.pallas_call(
        body, name=name, in_specs=[HBM] * (1 + len(after)), out_specs=HBM,
        out_shape=jax.ShapeDtypeStruct((N_DEV,) + block.shape, block.dtype),
        scratch_shapes=[pltpu.SemaphoreType.DMA((N_DEV - 1,)), pltpu.SemaphoreType.DMA((N_DEV - 1,)),
                        pltpu.SemaphoreType.DMA],
    )(block, *after)


def _pair_sum(grad, recv, c_arr, name):
    _, _, rh, cols = grad.shape
    tr = _row_tile(rh, cols * 4)

    def body(c_ref, g_ref, r_ref, o_ref):
        del c_ref
        o_ref[...] = (g_ref[...].astype(F32) + r_ref[...].astype(F32)).astype(o_ref.dtype)

    spec = pl.BlockSpec((None, tr, cols), lambda j, i, c_ref: (j, i, 0))
    return pl.pallas_call(
        body, name=name,
        grid_spec=pltpu.PrefetchScalarGridSpec(
            num_scalar_prefetch=1, grid=(N_CHIPS, rh // tr),
            in_specs=[pl.BlockSpec((None, None, tr, cols), lambda j, i, c_ref: (j, c_ref[0], i, 0)), spec],
            out_specs=spec),
        out_shape=jax.ShapeDtypeStruct(recv.shape, grad.dtype), compiler_params=_params(32),
    )(c_arr, grad, recv)


def _sum_by_chip(chip, p_ref, own_ref, o_ref):
    o_ref[...] = jnp.zeros_like(o_ref)
    for k in range(N_CHIPS):
        @pl.when(chip == k)
        def _():
            o_ref[...] += own_ref[...].astype(F32)

        @pl.when(chip != k)
        def _(k=k):
            o_ref[...] += p_ref[k].astype(F32)


def _chip_sum_all(parts, own, place_arr, name):
    _, nj, rh, cols = parts.shape
    tr = _row_tile(rh, cols * 4)

    def body(place_ref, p_ref, own_ref, o_ref):
        _sum_by_chip(place_ref[0], p_ref, own_ref, o_ref)

    return pl.pallas_call(
        body, name=name,
        grid_spec=pltpu.PrefetchScalarGridSpec(
            num_scalar_prefetch=1, grid=(nj, rh // tr),
            in_specs=[pl.BlockSpec((N_CHIPS, None, tr, cols), lambda j, i, place_ref: (0, j, i, 0)),
                      pl.BlockSpec((None, tr, cols), lambda j, i, place_ref: (j, i, 0))],
            out_specs=pl.BlockSpec((None, None, tr, cols), lambda j, i, place_ref: (j, place_ref[1], i, 0))),
        out_shape=jax.ShapeDtypeStruct((nj, 2, rh, cols), F32), compiler_params=_params(32),
    )(place_arr, parts, own)


def _chip_sum(parts, own, place_arr, name):
    _, rh, cols = parts.shape
    tr = _row_tile(rh, cols * 4)

    def body(place_ref, p_ref, own_ref, o_ref):
        _sum_by_chip(place_ref[0], p_ref, own_ref, o_ref)

    return pl.pallas_call(
        body, name=name,
        grid_spec=pltpu.PrefetchScalarGridSpec(
            num_scalar_prefetch=1, grid=(rh // tr,),
            in_specs=[pl.BlockSpec((N_CHIPS, tr, cols), lambda i, place_ref: (0, i, 0)),
                      pl.BlockSpec((None, tr, cols), lambda i, place_ref: (place_ref[0], i, 0))],
            out_specs=pl.BlockSpec((None, tr, cols), lambda i, place_ref: (place_ref[1], i, 0))),
        out_shape=jax.ShapeDtypeStruct((2, rh, cols), F32), compiler_params=_params(32),
    )(place_arr, parts, own)


def _adamw_math(w, g, m, v):
    m = ADAM_B1 * m + (1.0 - ADAM_B1) * g
    v = ADAM_B2 * v + (1.0 - ADAM_B2) * (g * g)
    m_hat = m / (1.0 - ADAM_B1 ** ADAM_STEP)
    v_hat = v / (1.0 - ADAM_B2 ** ADAM_STEP)
    delta = -ADAM_LR * (m_hat / (jnp.sqrt(v_hat) + ADAM_EPS) + ADAM_WD * w)
    return delta, m, v


def _adamw(w, g, m, v, name):
    rows, cols = w.shape
    tr = _row_tile(rows, cols * 4)

    def body(w_ref, g_ref, m_ref, v_ref, go_ref, d_ref, nm_ref, nv_ref):
        gv = g_ref[...]
        go_ref[...] = gv
        d_ref[...], nm_ref[...], nv_ref[...] = _adamw_math(w_ref[...], gv, m_ref[...], v_ref[...])

    spec = pl.BlockSpec((tr, cols), lambda i: (i, 0))
    return pl.pallas_call(
        body, name=name, grid=(rows // tr,), in_specs=[spec] * 4, out_specs=[spec] * 4,
        out_shape=[jax.ShapeDtypeStruct(w.shape, F32)] * 4, compiler_params=_params(32),
    )(w, g, m, v)


def _sum8_adamw_row(parts, w, m, v, name):
    cols = parts.shape[2]

    def body(p_ref, w_ref, m_ref, v_ref, g_ref, d_ref, nm_ref, nv_ref):
        g = p_ref[0, 0:1, :]
        for k in range(1, N_DEV):
            g = g + p_ref[k, 0:1, :]
        g_ref[...] = g
        d_ref[...], nm_ref[...], nv_ref[...] = _adamw_math(w_ref[...], g, m_ref[...], v_ref[...])

    return pl.pallas_call(
        body, name=name, out_shape=[jax.ShapeDtypeStruct((1, cols), F32)] * 4, compiler_params=_params(32),
    )(parts, w, m, v)


def _pack_rows(pieces, rows, name):
    cols = pieces[0].shape[1]
    n = len(pieces)

    def body(*refs):
        o_ref = refs[n]
        o_ref[...] = jnp.zeros_like(o_ref)
        at = 0
        for p_ref in refs[:n]:
            r = p_ref.shape[0]
            o_ref[at:at + r, :] = p_ref[...]
            at += r

    return pl.pallas_call(
        body, name=name, out_shape=jax.ShapeDtypeStruct((rows, cols), F32), compiler_params=_params(32),
    )(*pieces)


def kernel(x, norm_mix_g, w_in, conv_w, conv_b, lru_wa, lru_ba, lru_wx, lru_bx, lru_lambda, w_proj_attn, w_proj_lru, w_out, norm_mlp_g, w_up, w_down, norm_final_g, loss_target, m_norm_mix_g, m_w_in, m_conv_w, m_conv_b, m_lru_wa, m_lru_ba, m_lru_wx, m_lru_bx, m_lru_lambda, m_w_proj_attn, m_w_proj_lru, m_w_out, m_norm_mlp_g, m_w_up, m_w_down, m_norm_final_g, v_norm_mix_g, v_w_in, v_conv_w, v_conv_b, v_lru_wa, v_lru_ba, v_lru_wx, v_lru_bx, v_lru_lambda, v_w_proj_attn, v_w_proj_lru, v_w_out, v_norm_mlp_g, v_w_up, v_w_down, v_norm_final_g):
    s, d = x.shape[1], x.shape[2]
    ff = w_up.shape[2] * N_CHIPS
    heads = d // HEAD_DIM
    u = d // 4
    assert s % (max(DILATIONS) * ATTN_BLK) == 0 and d % (4 * HEAD_DIM) == 0 and ff == 4 * d and DILATIONS[0] == 1
    xs, target = _in_hbm([x[0], loss_target[0]])
    gf = norm_final_g.reshape(1, d)
    wa, wx = lru_wa[0], lru_wx[0]
    core = lax.axis_index("c").astype(jnp.int32)
    chip = (2 * lax.axis_index("x") + lax.axis_index("y")).astype(jnp.int32)
    cidx = core.reshape(1)
    chip_arr = chip.reshape(1)
    place_arr = jnp.stack([chip, core])
    slopes = jnp.broadcast_to(
        (2.0 ** (-8.0 * jnp.arange(1, heads + 1, dtype=F32) / heads))[:, None, None], (heads, 1, HEAD_DIM))

    big = _in_hbm([w_in[0], w_proj_attn[0], w_proj_lru[0], w_out[0], w_up[0], w_down[0]])
    names = ["w_in", "w_proj_attn", "w_proj_lru", "w_out", "w_up", "w_down"]
    cw_pad = jnp.pad(conv_w[0], ((0, 8 - CONV_TAPS), (0, 0)))
    cw_all = _all_gather8(cw_pad, "gather_conv_w")
    conv_w_full = jnp.concatenate([cw_all[2 * j] for j in range(N_CHIPS)], axis=1)
    (sem_a,), buf_a = _gather_start([_cast_shard(big[0], chip_arr, "cast_w_in")], [[0]], [cw_all], "gather_start_w_in")
    bufs = [_cast_shard(w, chip_arr, "cast_" + nm) for w, nm in zip(big[1:], names[1:])]
    (sem_b, sem_c), bufs = _gather_start(bufs, [[0, 1, 2], [3, 4]], buf_a, "gather_start_rest")
    fsem_a, buf_a = _gather_forward(buf_a, sem_a[1], bufs[:1], "gather_forward_w_in")
    buf_a = _gather_finish(buf_a, sem_a[0], fsem_a[0], fsem_a[1], [], "gather_finish_w_in")
    w_in_g = buf_a[0].reshape(N_CHIPS, d, N_SLOTS * u)

    xn = _rms_fwd(xs, norm_mix_g, "norm_mix")
    proj_a = _proj_in(xn, w_in_g, 0, 4 * N_QKV, N_QKV)
    proj_b = _proj_in(xn, w_in_g, 4 * N_QKV, 4 * (N_SLOTS - N_QKV), N_SLOTS - N_QKV)
    y_attn, lse = _attn_fwd(proj_a, slopes)
    y_lru, h_lru = _lru_fwd(proj_b, conv_w_full, conv_b, wa, lru_ba, wx, lru_bx, lru_lambda)
    fsem_b, buf_b = _gather_forward(bufs[:3], sem_b[1], [y_attn, y_lru], "gather_forward_proj")
    fsem_c, buf_c = _gather_forward(bufs[3:], sem_c[1], [buf_b[0]], "gather_forward_mlp")
    buf_b = _gather_finish(buf_b, sem_b[0], fsem_b[0], fsem_b[1], [buf_c[0]], "gather_finish_proj")
    wpa_g = buf_b[0].reshape(d, d)
    wpl_g = buf_b[1].reshape(d, d)
    wout_g = buf_b[2].reshape(d, d)

    tn = u
    sd_f32 = jax.ShapeDtypeStruct((s, d), F32)
    sd_bf16 = jax.ShapeDtypeStruct((s, d), BF16)
    col = pl.BlockSpec((s, tn), lambda i, j, k: (0, j))

    def slot(n):
        return pl.BlockSpec((None, s, tn), lambda i, j, k: (n, 0, j))

    p_attn = _mm_nn("proj_attn", y_attn, wpa_g, [], [], [sd_f32], [col], _store, tn)[0]

    def merge(acc, extras, outs):
        pa_ref, ga_ref, gl_ref = extras
        merged = _sigmoid(ga_ref[...]) * pa_ref[...] + _sigmoid(gl_ref[...]) * acc
        outs[0][...] = merged.astype(BF16)
        outs[1][...] = acc

    tn2 = max(HEAD_DIM, u // 2)
    col2 = pl.BlockSpec((s, tn2), lambda i, j, k: (0, j))

    def slot2(n):
        return pl.BlockSpec((None, s, tn2), lambda i, j, k: (n, 0, j))

    merged, p_lru = _mm_nn("proj_lru_merge", y_lru, wpl_g, [p_attn, proj_b, proj_b], [col2, slot2(2), slot2(3)],
                           [sd_bf16, sd_f32], [col2, col2], merge, tn2)

    def add_resid(acc, extras, outs):
        outs[0][...] = extras[0][...] + acc

    h1 = _mm_nn("w_out_resid", merged, wout_g, [xs], [col], [sd_f32], [col], add_resid, tn)[0]
    hn = _rms_fwd(h1, norm_mlp_g, "norm_mlp")
    buf_c = _gather_finish(buf_c, sem_c[0], fsem_c[0], fsem_c[1], [hn], "gather_finish_mlp")
    wup_g = buf_c[0].reshape(N_CHIPS, d, d)
    wdown_g = buf_c[1].reshape(ff, d)

    def relu_sq(acc, extras, outs):
        r = jnp.maximum(acc, 0.0)
        outs[0][...] = (r * r).astype(BF16)
        outs[1][...] = r.astype(BF16)

    sf_bf16 = jax.ShapeDtypeStruct((s, ff), BF16)
    hid, relu_up = _mm(
        "w_up_relu2", [hn, wup_g],
        [pl.BlockSpec((s, d), lambda i, j, k: (0, 0)),
         pl.BlockSpec((None, d, tn), lambda i, j, k: (j // 4, 0, j % 4))],
        [sf_bf16, sf_bf16], [col, col], (1, ff // tn, 1), NN, relu_sq)
    h2 = _mm(
        "w_down_resid", [hid, wdown_g, h1],
        [pl.BlockSpec((s, d), lambda i, j, k: (0, k)), pl.BlockSpec((d, tn), lambda i, j, k: (k, j)), col],
        [sd_f32], [col], (1, d // tn, ff // d), NN, add_resid, nk=ff // d, acc_shape=(s, tn))[0]
    loss_part, dh2, dh2_b, d_gf = _loss_head(h2, gf, target)
    loss = lax.psum(loss_part[0, 0], AXES)

    def relu_sq_bwd(acc, extras, outs):
        outs[0][...] = (acc * (2.0 * extras[0][...].astype(F32))).astype(BF16)

    dup = _mm_nt("d_hid", dh2_b, wdown_g, [relu_up], [col], [sf_bf16], [col], relu_sq_bwd, tn)[0]
    tok_d = pl.BlockSpec((s, d), lambda i, j: (0, 0))
    g_wdown = _mm_tn(
        "g_w_down", hid, dh2_b, pl.BlockSpec((s, d), lambda i, j: (0, i)),
        pl.BlockSpec((s, tn), lambda i, j: (0, j)), jax.ShapeDtypeStruct((ff, d), BF16),
        pl.BlockSpec((d, tn), lambda i, j: (i, j)), (ff // d, d // tn), d, tn, s)
    dhn = _mm(
        "d_hn", [dup, wup_g],
        [pl.BlockSpec((s, d), lambda i, j, k: (0, k)), pl.BlockSpec((None, tn, d), lambda i, j, k: (k, j, 0))],
        [sd_f32], [col], (1, d // tn, ff // d), NT, _store, nk=ff // d, acc_shape=(s, tn))[0]
    g_wup = _mm_tn(
        "g_w_up", hn, dup, tok_d, pl.BlockSpec((s, tn), lambda i, j: (0, j)),
        jax.ShapeDtypeStruct((N_CHIPS, d, d), BF16), pl.BlockSpec((None, d, tn), lambda i, j: (j // 4, 0, j % 4)),
        (1, ff // tn), d, tn, s)
    big_m = _in_hbm([m_w_in[0], m_w_proj_attn[0], m_w_proj_lru[0], m_w_out[0], m_w_up[0], m_w_down[0]])
    big_v = _in_hbm([v_w_in[0], v_w_proj_attn[0], v_w_proj_lru[0], v_w_out[0], v_w_up[0], v_w_down[0]])
    big_out = {}

    def reduce_begin(ids, gs, tag, everywhere=None):
        g4 = [g.reshape(N_CHIPS, 2, big[i].shape[0] // 2, big[i].shape[1]) for i, g in zip(ids, gs)]
        tags = [names[i] for i in ids]
        if everywhere is not None:
            g4.append(everywhere.reshape(N_CHIPS, 2, everywhere.shape[0] // (2 * N_CHIPS), everywhere.shape[1]))
            tags.append("small_" + tag)
        from_sibling = _pair_exchange(g4, "pair_exchange_" + tag)
        sums = [_pair_sum(g, r, cidx, "pair_sum_" + t) for t, g, r in zip(tags, g4, from_sibling)]
        to_all = () if everywhere is None else (len(ids),)
        return _chip_start(sums, "chip_start_" + tag, to_all), to_all

    def reduce_end(ids, begun, after, tag):
        (sems, sums, lands, _), to_all = begun
        sums, lands = _chip_wait(sems, sums, lands, after, "chip_wait_" + tag, to_all)
        halves = [_chip_sum(p, own, place_arr, "chip_sum_" + names[i]) for i, p, own in zip(ids, lands, sums)]
        if to_all:
            halves.append(_chip_sum_all(lands[-1], sums[-1], place_arr, "chip_sum_small_" + tag))
        full = _half_exchange(halves, "half_exchange_" + tag)
        last = None
        for i, g in zip(ids, full):
            res = _adamw(big[i], g.reshape(big[i].shape), big_m[i], big_v[i], "adamw_" + names[i])
            big_out[names[i]] = tuple(a[None] for a in res)
            last = res[1]
        everywhere = full[-1].reshape(-1, full[-1].shape[-1]) if to_all else None
        return everywhere, last

    def after_token(a, begun):
        return a + begun[0][3][:1, :1]

    red_mlp = reduce_begin([4, 5], [g_wup, g_wdown], "mlp")
    dh1, dh1_b, d_gmlp = _rms_bwd(h1, after_token(norm_mlp_g, red_mlp), dhn, dh2, "norm_mlp_bwd")

    g_wout = _mm_tn(
        "g_w_out", merged, dh1_b, tok_d, pl.BlockSpec((s, tn), lambda i, j: (0, j)),
        jax.ShapeDtypeStruct((d, d), BF16), pl.BlockSpec((d, tn), lambda i, j: (0, j)), (1, d // tn), d, tn, s)

    def merge_bwd(acc, extras, outs):
        pa_ref, pl_ref, ga_ref, gl_ref = extras
        sa, sl = _sigmoid(ga_ref[...]), _sigmoid(gl_ref[...])
        outs[0][...] = (acc * sa).astype(BF16)
        outs[1][...] = (acc * sl).astype(BF16)
        outs[2][0] = (acc * pa_ref[...] * (sa * (1.0 - sa))).astype(BF16)
        outs[2][1] = (acc * pl_ref[...] * (sl * (1.0 - sl))).astype(BF16)

    nb = N_SLOTS - N_QKV
    d_pa, d_pl, dproj_b = _mm_nt(
        "d_merged", dh1_b, wout_g, [p_attn, p_lru, proj_b, proj_b], [col2, col2, slot2(2), slot2(3)],
        [sd_bf16, sd_bf16, jax.ShapeDtypeStruct((nb, s, d), BF16)],
        [col2, col2, pl.BlockSpec((2, s, tn2), lambda i, j, k: (1, 0, j))], merge_bwd, tn2)
    dy_attn = _mm_nt("d_y_attn", d_pa, wpa_g, [], [], [sd_f32], [col], _store, tn)[0]
    dy_lru = _mm_nt("d_y_lru", d_pl, wpl_g, [], [], [sd_f32], [col], _store, tn)[0]
    g_wpa = _mm_tn(
        "g_w_proj_attn", y_attn, d_pa, tok_d, pl.BlockSpec((s, tn), lambda i, j: (0, j)),
        jax.ShapeDtypeStruct((d, d), BF16), pl.BlockSpec((d, tn), lambda i, j: (0, j)), (1, d // tn), d, tn, s)
    g_wpl = _mm_tn(
        "g_w_proj_lru", y_lru, d_pl, tok_d, pl.BlockSpec((s, tn), lambda i, j: (0, j)),
        jax.ShapeDtypeStruct((d, d), BF16), pl.BlockSpec((d, tn), lambda i, j: (0, j)), (1, d // tn), d, tn, s)

    red_proj = reduce_begin([1, 2, 3], [g_wpa, g_wpl, g_wout], "proj")

    dproj_a = _attn_bwd(proj_a, after_token(slopes, red_proj), y_attn, lse, dy_attn)
    dproj_b, d_cw, d_cb, d_wa, d_ba, d_wx, d_bx, d_lam = _lru_bwd(
        proj_b, h_lru, dy_lru, conv_w_full, conv_b, wa, lru_ba, wx, lru_bx, after_token(lru_lambda, red_proj),
        dproj_b)
    per = N_SLOTS
    g_win_shape = jax.ShapeDtypeStruct((N_CHIPS, d, N_SLOTS * u), BF16)

    def g_win_part(name, dproj, first, prev):
        n_units = 4 * dproj.shape[0]
        return _mm_tn(
            name, xn, dproj, tok_d, pl.BlockSpec((None, s, u), lambda i, j: (j // 4, 0, j % 4)),
            g_win_shape, pl.BlockSpec((None, d, u), lambda i, j: ((j + first) // per, 0, (j + first) % per)),
            (1, n_units), d, u, s, aliases=None if prev is None else {2: 0}, extra=prev)

    mat_rows = heads * HEAD_DIM * HEAD_DIM // d
    vec_names = ["norm_mix_g", "conv_b", "lru_ba", "lru_bx", "lru_lambda", "norm_mlp_g", "norm_final_g"]

    def pack(wa_, wx_, cw_, vecs, name):
        rows = [wa_.reshape(mat_rows, d), wx_.reshape(mat_rows, d), cw_] + [a.reshape(1, d) for a in vecs]
        n = sum(a.shape[0] for a in rows)
        return _pack_rows(rows, n + (-n % 64), name)

    zero_cw = jnp.zeros((CONV_TAPS, d), F32)
    small_g = pack(d_wa, d_wx, d_cw, [jnp.zeros((1, d), F32), d_cb, d_ba, d_bx, d_lam, d_gmlp, d_gf], "pack_small_g")
    g_win = g_win_part("g_w_in_qkv", dproj_a, 0, None)
    g_win = g_win_part("g_w_in_rest", dproj_b, 4 * N_QKV, g_win)
    red_in = reduce_begin([0], [g_win], "w_in", everywhere=small_g)
    dxn = _dxn(dproj_a, dproj_b, w_in_g, 2 * tn, [red_in[0][3]])
    grad_x, _, d_gmix = _rms_bwd(xs, norm_mix_g, dxn, dh1, "norm_mix_bwd")

    _, done = reduce_end([4, 5], red_mlp, [grad_x], "mlp")
    _, done = reduce_end([1, 2, 3], red_proj, [done], "proj")
    small_w = pack(wa, wx, zero_cw, [norm_mix_g, conv_b, lru_ba, lru_bx, lru_lambda, norm_mlp_g, norm_final_g],
                   "pack_small_w")
    small_m = pack(m_lru_wa[0], m_lru_wx[0], zero_cw,
                   [m_norm_mix_g, m_conv_b, m_lru_ba, m_lru_bx, m_lru_lambda, m_norm_mlp_g, m_norm_final_g],
                   "pack_small_m")
    small_v = pack(v_lru_wa[0], v_lru_wx[0], zero_cw,
                   [v_norm_mix_g, v_conv_b, v_lru_ba, v_lru_bx, v_lru_lambda, v_norm_mlp_g, v_norm_final_g],
                   "pack_small_v")
    small_sum, _ = reduce_end([0], red_in, [done, small_w, small_m, small_v], "w_in")
    small = _adamw(small_w, small_sum, small_m, small_v, "adamw_small")
    g_cw = lax.dynamic_slice(small_sum[2 * mat_rows:2 * mat_rows + CONV_TAPS], (0, chip * u), (CONV_TAPS, u))
    cw_out = _adamw(conv_w[0], g_cw, m_conv_w[0], v_conv_w[0], "adamw_conv_w")
    gmix_parts = _all_gather8(jnp.pad(d_gmix, ((0, 7), (0, 0))), "gather_gain_grad", [small[1]])
    gmix_out = _sum8_adamw_row(gmix_parts, norm_mix_g, m_norm_mix_g, v_norm_mix_g, "sum_adamw_norm_mix_g")

    def small_leaf(kind, name):
        a = small[kind]
        if name == "norm_mix_g":
            return gmix_out[kind]
        if name == "lru_wa":
            return a[0:mat_rows].reshape(lru_wa.shape)
        if name == "lru_wx":
            return a[mat_rows:2 * mat_rows].reshape(lru_wx.shape)
        if name == "conv_w":
            return cw_out[kind][None]
        row = a[2 * mat_rows + CONV_TAPS + vec_names.index(name)]
        return row if name == "norm_final_g" else row[None]

    order = ["norm_mix_g", "w_in", "conv_w", "conv_b", "lru_wa", "lru_ba", "lru_wx", "lru_bx", "lru_lambda",
             "w_proj_attn", "w_proj_lru", "w_out", "norm_mlp_g", "w_up", "w_down", "norm_final_g"]
    outs = [loss, grad_x[None]]
    for kind in range(4):
        for name in order:
            outs.append(big_out[name][kind] if name in big_out else small_leaf(kind, name))
    return tuple(outs)
```

```python
import functools

import jax
import jax.numpy as jnp
from jax import lax
from jax.experimental import pallas as pl
from jax.experimental.pallas import tpu as pltpu

F32 = jnp.float32
BF16 = jnp.bfloat16
MESH = pl.DeviceIdType.MESH
AXES = ("x", "y", "c")

N_CHIPS = 4
N_DEV = 8
HEAD_DIM = 128
ATTN_BLK = 128
DILATIONS = (1, 4, 16)
ATTN_UNROLL = 8
CONV_TAPS = 4
LRU_C = 8.0
EPS = 1e-6
N_SLOTS = 7
N_QKV = 3
VMEM_MIB = 2 ** 20
VMEM_V7X = 64 * VMEM_MIB
STREAM_TILE = 4 * VMEM_MIB

ADAM_LR = 0.001
ADAM_B1 = 0.9
ADAM_B2 = 0.999
ADAM_EPS = 1e-08
ADAM_WD = 0.01
ADAM_STEP = 10

NN = (((1,), (0,)), ((), ()))
NT = (((1,), (1,)), ((), ()))
TN = (((0,), (0,)), ((), ()))


def _params(vmem_mib=None, **kw):
    limit = None if vmem_mib is None else min(vmem_mib * VMEM_MIB, VMEM_V7X - 8 * VMEM_MIB)
    return pltpu.CompilerParams(vmem_limit_bytes=limit, **kw)


def _row_tile(rows, row_bytes, budget=VMEM_MIB):
    t = rows
    while t % 16 == 0 and t * row_bytes > budget:
        t //= 2
    return t


def _dot(a, b, dims):
    return lax.dot_general(a.astype(BF16), b.astype(BF16), dims, preferred_element_type=F32)


def _sigmoid(x):
    return jax.nn.sigmoid(x)


def _rms_fwd(x, g, name):
    s, d = x.shape
    tm = _row_tile(s, d * 4)

    def body(x_ref, g_ref, o_ref):
        xf = x_ref[...]
        r = lax.rsqrt(jnp.mean(xf * xf, axis=-1, keepdims=True) + EPS)
        o_ref[...] = (xf * r * g_ref[...]).astype(o_ref.dtype)

    return pl.pallas_call(
        body, name=name, grid=(s // tm,),
        in_specs=[pl.BlockSpec((tm, d), lambda i: (i, 0)), pl.BlockSpec((1, d), lambda i: (0, 0))],
        out_specs=pl.BlockSpec((tm, d), lambda i: (i, 0)),
        out_shape=jax.ShapeDtypeStruct((s, d), BF16), compiler_params=_params(32),
    )(x, g)


def _rms_bwd(x, g, dy, resid, name):
    s, d = x.shape
    tm = _row_tile(s, d * 4)

    def body(x_ref, g_ref, dy_ref, res_ref, dx_ref, dxb_ref, dg_ref):
        xf = x_ref[...]
        r = lax.rsqrt(jnp.mean(xf * xf, axis=-1, keepdims=True) + EPS)
        xh = xf * r
        dyv = dy_ref[...]
        dxh = dyv * g_ref[...]
        dx = r * (dxh - xh * jnp.mean(dxh * xh, axis=-1, keepdims=True)) + res_ref[...]
        dx_ref[...] = dx
        dxb_ref[...] = dx.astype(BF16)
        part = jnp.sum(dyv * xh, axis=0, keepdims=True)

        @pl.when(pl.program_id(0) == 0)
        def _():
            dg_ref[...] = part

        @pl.when(pl.program_id(0) > 0)
        def _():
            dg_ref[...] += part

    row = pl.BlockSpec((tm, d), lambda i: (i, 0))
    vec = pl.BlockSpec((1, d), lambda i: (0, 0))
    return pl.pallas_call(
        body, name=name, grid=(s // tm,),
        in_specs=[row, vec, row, row], out_specs=[row, row, vec],
        out_shape=[jax.ShapeDtypeStruct((s, d), F32), jax.ShapeDtypeStruct((s, d), BF16),
                   jax.ShapeDtypeStruct((1, d), F32)],
        compiler_params=_params(32),
    )(x, g, dy, resid)


def _loss_head(h2, g, target):
    s, d = h2.shape
    tm = _row_tile(s, d * 4)

    def body(x_ref, g_ref, t_ref, loss_ref, dx_ref, dxb_ref, dg_ref):
        xf = x_ref[...]
        gv = g_ref[...]
        r = lax.rsqrt(jnp.mean(xf * xf, axis=-1, keepdims=True) + EPS)
        xh = xf * r
        err = xh * gv - t_ref[...]
        part = jnp.sum(jnp.sum(err * err, axis=1, keepdims=True), axis=0, keepdims=True) * (0.5 / d)
        dyv = err * (1.0 / d)
        dxh = dyv * gv
        dx = r * (dxh - xh * jnp.mean(dxh * xh, axis=-1, keepdims=True))
        dx_ref[...] = dx
        dxb_ref[...] = dx.astype(BF16)
        dgp = jnp.sum(dyv * xh, axis=0, keepdims=True)

        @pl.when(pl.program_id(0) == 0)
        def _():
            dg_ref[...] = dgp
            loss_ref[...] = jnp.broadcast_to(part, loss_ref.shape)

        @pl.when(pl.program_id(0) > 0)
        def _():
            dg_ref[...] += dgp
            loss_ref[...] += jnp.broadcast_to(part, loss_ref.shape)

    row = pl.BlockSpec((tm, d), lambda i: (i, 0))
    vec = pl.BlockSpec((1, d), lambda i: (0, 0))
    return pl.pallas_call(
        body, name="loss_head", grid=(s // tm,),
        in_specs=[row, vec, row],
        out_specs=[pl.BlockSpec((8, 128), lambda i: (0, 0)), row, row, vec],
        out_shape=[jax.ShapeDtypeStruct((8, 128), F32), jax.ShapeDtypeStruct((s, d), F32),
                   jax.ShapeDtypeStruct((s, d), BF16), jax.ShapeDtypeStruct((1, d), F32)],
        compiler_params=_params(32),
    )(h2, g, target)


def _mm(name, operands, in_specs, out_shape, out_specs, grid, dims, epilogue, nk=1, acc_shape=None,
        vmem_mib=56, aliases=None):
    n_in = len(operands)
    n_out = len(out_shape)

    def body(*refs):
        a_ref, b_ref = refs[0], refs[1]
        extras = refs[2:n_in]
        outs = refs[n_in:n_in + n_out]

        def prod():
            return _dot(a_ref[...], b_ref[...], dims)

        if nk == 1:
            epilogue(prod(), extras, outs)
        else:
            acc = refs[n_in + n_out]
            k = pl.program_id(2)

            @pl.when(k == 0)
            def _():
                acc[...] = prod()

            @pl.when(k > 0)
            def _():
                acc[...] += prod()

            @pl.when(k == nk - 1)
            def _():
                epilogue(acc[...], extras, outs)

    scratch = [] if nk == 1 else [pltpu.VMEM(acc_shape, F32)]
    return pl.pallas_call(
        body, name=name, grid=grid, in_specs=in_specs, out_specs=out_specs, out_shape=out_shape,
        scratch_shapes=scratch, input_output_aliases=aliases or {},
        compiler_params=_params(vmem_mib),
    )(*operands)


def _store(acc, extras, outs):
    outs[0][...] = acc.astype(outs[0].dtype)


def _proj_in(xn, w_in_g, first_unit, n_units, n_slots):
    s, d = xn.shape
    u = d // 4
    per = N_SLOTS

    return _mm(
        "proj_in_%d" % first_unit, [xn, w_in_g],
        [pl.BlockSpec((s, d), lambda i, j, k: (0, 0)),
         pl.BlockSpec((None, d, u), lambda i, j, k: ((j + first_unit) // per, 0, (j + first_unit) % per))],
        [jax.ShapeDtypeStruct((n_slots, s, d), F32)],
        [pl.BlockSpec((None, s, u), lambda i, j, k: (j // 4, 0, j % 4))],
        (1, n_units, 1), NN, _store)[0]


def _mm_nn(name, a, b, extras, extra_specs, out_shape, out_specs, epilogue, tn, aliases=None):
    s, kdim = a.shape
    n = b.shape[1]
    return _mm(
        name, [a, b] + list(extras),
        [pl.BlockSpec((s, kdim), lambda i, j, k: (0, 0)), pl.BlockSpec((kdim, tn), lambda i, j, k: (0, j))]
        + list(extra_specs),
        out_shape, out_specs, (1, n // tn, 1), NN, epilogue, aliases=aliases)


def _mm_nt(name, a, b, extras, extra_specs, out_shape, out_specs, epilogue, tn, aliases=None):
    s, kdim = a.shape
    n = b.shape[0]
    return _mm(
        name, [a, b] + list(extras),
        [pl.BlockSpec((s, kdim), lambda i, j, k: (0, 0)), pl.BlockSpec((tn, kdim), lambda i, j, k: (j, 0))]
        + list(extra_specs),
        out_shape, out_specs, (1, n // tn, 1), NT, epilogue, aliases=aliases)


def _mm_tn(name, a, b, a_spec, b_spec, out_shape, out_spec, grid, m, tn, s, aliases=None, extra=None):
    ch = 256
    n_in = 2 if extra is None else 3

    def body(*refs):
        a_ref, b_ref = refs[0], refs[1]
        o_ref, at_ref = refs[n_in], refs[n_in + 1]

        @pl.when(pl.program_id(1) == 0)
        def _():
            for c0 in range(0, s, ch):
                at_ref[:, c0:c0 + ch] = a_ref[c0:c0 + ch, :].astype(F32).T.astype(BF16)

        o_ref[...] = _dot(at_ref[...], b_ref[...], NN).astype(o_ref.dtype)

    operands = [a, b] + ([] if extra is None else [extra])
    in_specs = [a_spec, b_spec] + ([] if extra is None else [pl.BlockSpec(memory_space=pl.ANY)])
    return pl.pallas_call(
        body, name=name, grid=grid, in_specs=in_specs, out_specs=out_spec, out_shape=out_shape,
        scratch_shapes=[pltpu.VMEM((m, s), BF16)], input_output_aliases=aliases or {},
        compiler_params=_params(56),
    )(*operands)


def _dxn(dproj_a, dproj_b, w_in_g, tn, after):
    n_a, s, d = dproj_a.shape
    u = d // 4
    ua = 4 * n_a
    nk = 4 * N_SLOTS
    per = N_SLOTS

    def body(a_ref, b_ref, w_ref, *rest):
        o_ref = rest[len(after)]
        k = pl.program_id(2)

        @pl.when(k == 0)
        def _():
            o_ref[...] = jnp.zeros_like(o_ref)

        @pl.when(k < ua)
        def _():
            o_ref[...] += _dot(a_ref[...], w_ref[...], NT)

        @pl.when(k >= ua)
        def _():
            o_ref[...] += _dot(b_ref[...], w_ref[...], NT)

    def a_map(i, j, k):
        kk = jnp.minimum(k, ua - 1)
        return (kk // 4, 0, kk % 4)

    def b_map(i, j, k):
        kk = jnp.maximum(k - ua, 0)
        return (kk // 4, 0, kk % 4)

    return pl.pallas_call(
        body, name="dxn", grid=(1, d // tn, nk),
        in_specs=[pl.BlockSpec((None, s, u), a_map), pl.BlockSpec((None, s, u), b_map),
                  pl.BlockSpec((None, tn, u), lambda i, j, k: (k // per, j, k % per))] + [HBM] * len(after),
        out_specs=pl.BlockSpec((s, tn), lambda i, j, k: (0, j)),
        out_shape=jax.ShapeDtypeStruct((s, d), F32),
        compiler_params=_params(48),
    )(dproj_a, dproj_b, w_in_g, *after)


def _attn_masks(slope, dil):
    ii = lax.broadcasted_iota(jnp.int32, (ATTN_BLK, 2 * ATTN_BLK), 0)
    jj = lax.broadcasted_iota(jnp.int32, (ATTN_BLK, 2 * ATTN_BLK), 1)
    diff = ATTN_BLK + ii - jj
    band = (diff >= 0) & (diff <= ATTN_BLK)
    bias = -(slope * float(dil)) * diff.astype(F32)
    return band, bias, jj


def _attn_window(t, nblk):
    cur = pl.ds(pl.multiple_of(t * ATTN_BLK, ATTN_BLK), ATTN_BLK)
    prev = pl.ds(pl.multiple_of(jnp.maximum(t - 1, 0) * ATTN_BLK, ATTN_BLK), ATTN_BLK)
    first = jnp.where(t % nblk == 0, ATTN_BLK, 0)
    return prev, cur, first


def _unrolled_loop(n, step, init):
    def trip(i, carry):
        for k in range(ATTN_UNROLL):
            carry = step(i * ATTN_UNROLL + k, carry)
        return carry

    return lax.fori_loop(0, n // ATTN_UNROLL, trip, init)


def _streams(pairs, dil, s):
    if dil == 1:
        return [src for _, src in pairs]
    seg = s // dil
    for dst, src in pairs:
        for r in range(dil):
            dst[r * seg:(r + 1) * seg, :] = src[pl.ds(r, seg, stride=dil), :].astype(dst.dtype)
    return [dst for dst, _ in pairs]


def _attn_fwd(proj_a, slopes):
    _, s, d = proj_a.shape
    heads = d // HEAD_DIM
    scale = HEAD_DIM ** -0.5
    n_t = s // ATTN_BLK
    ng = len(DILATIONS)

    def body(q_ref, k_ref, v_ref, sl_ref, o_ref, lse_ref, qd, kd, vd, od, ld, og, lg):
        slope = sl_ref[...][:, :1]
        for g, dil in enumerate(DILATIONS):
            nblk = s // dil // ATTN_BLK
            qs, ks, vs = _streams([(qd, q_ref), (kd, k_ref), (vd, v_ref)], dil, s)
            o_t, l_t = (og.at[g], lg.at[g]) if dil == 1 else (od, ld)
            band, bias, jj = _attn_masks(slope, dil)

            def blk(t, carry, nblk=nblk, band=band, bias=bias, jj=jj, qs=qs, ks=ks, vs=vs, o_t=o_t, l_t=l_t):
                prev, cur, first = _attn_window(t, nblk)
                kk = jnp.concatenate([ks[prev, :], ks[cur, :]], axis=0)
                vv = jnp.concatenate([vs[prev, :], vs[cur, :]], axis=0)
                sc = _dot(qs[cur, :], kk, NT) * scale + bias
                sc = jnp.where(band & (jj >= first), sc, -jnp.inf)
                m = jnp.max(sc, axis=1, keepdims=True)
                p = jnp.exp(sc - m)
                l = jnp.sum(p, axis=1, keepdims=True)
                o_t[cur, :] = _dot(p, vv, NN) / l
                l_t[cur, :] = jnp.broadcast_to(m + jnp.log(l), (ATTN_BLK, HEAD_DIM))
                return carry

            _unrolled_loop(n_t, blk, 0)
            seg = s // dil
            if dil > 1:
                for r in range(dil):
                    og[g, pl.ds(r, seg, stride=dil), :] = od[r * seg:(r + 1) * seg, :]
                    lg[g, pl.ds(r, seg, stride=dil), :] = ld[r * seg:(r + 1) * seg, :]

        ch = 256

        def combine(c, carry):
            rows = pl.ds(pl.multiple_of(c * ch, ch), ch)
            ls = [lg[g, rows, :] for g in range(ng)]
            mx = functools.reduce(jnp.maximum, ls)
            es = [jnp.exp(x - mx) for x in ls]
            den = functools.reduce(jnp.add, es)
            num = functools.reduce(jnp.add, [es[g] * og[g, rows, :] for g in range(ng)])
            o_ref[rows, :] = (num / den).astype(o_ref.dtype)
            lse_ref[rows, :] = mx + jnp.log(den)
            return carry

        lax.fori_loop(0, s // ch, combine, 0)

    def col(slot):
        return pl.BlockSpec((None, s, HEAD_DIM), lambda h: (slot, 0, h))

    head = pl.BlockSpec((s, HEAD_DIM), lambda h: (0, h))
    return pl.pallas_call(
        body, name="attn_fwd", grid=(heads,),
        in_specs=[col(0), col(1), col(2), pl.BlockSpec((None, 1, HEAD_DIM), lambda h: (h, 0, 0))],
        out_specs=[head, head],
        out_shape=[jax.ShapeDtypeStruct((s, d), BF16), jax.ShapeDtypeStruct((s, d), F32)],
        scratch_shapes=[pltpu.VMEM((s, HEAD_DIM), BF16)] * 3 + [pltpu.VMEM((s, HEAD_DIM), F32)] * 2
        + [pltpu.VMEM((ng, s, HEAD_DIM), F32)] * 2,
        compiler_params=_params(40),
    )(proj_a, proj_a, proj_a, slopes)


def _attn_bwd(proj_a, slopes, y_attn, lse, dy):
    _, s, d = proj_a.shape
    heads = d // HEAD_DIM
    scale = HEAD_DIM ** -0.5
    n_t = s // ATTN_BLK

    def body(q_ref, k_ref, v_ref, sl_ref, o_ref, lse_ref, dy_ref, out_ref,
             qd, kd, vd, dod, lsd, dld, delta, dqd, dkd, dvd, dqa, dka, dva):
        slope = sl_ref[...][:, :1]
        dyv = dy_ref[...]
        delta[...] = jnp.broadcast_to(
            jnp.sum(dyv * o_ref[...].astype(F32), axis=1, keepdims=True), (s, HEAD_DIM))
        for g, dil in enumerate(DILATIONS):
            nblk = s // dil // ATTN_BLK
            seg = s // dil
            qs, ks, vs, dos, lss, dls = _streams(
                [(qd, q_ref), (kd, k_ref), (vd, v_ref), (dod, dy_ref), (lsd, lse_ref), (dld, delta)], dil, s)
            dq_t, dk_t, dv_t = (dqa, dka, dva) if dil == 1 else (dqd, dkd, dvd)
            band, bias, jj = _attn_masks(slope, dil)

            def blk(t, carry, nblk=nblk, band=band, bias=bias, jj=jj, qs=qs, ks=ks, vs=vs, dos=dos, lss=lss,
                    dls=dls, dq_t=dq_t, dk_t=dk_t, dv_t=dv_t):
                ck, cv = carry
                prev, cur, first = _attn_window(t, nblk)
                q = qs[cur, :]
                do = dos[cur, :]
                lse_b = lss[cur, :]
                dl_b = dls[cur, :]
                kk = jnp.concatenate([ks[prev, :], ks[cur, :]], axis=0)
                vv = jnp.concatenate([vs[prev, :], vs[cur, :]], axis=0)
                sc = _dot(q, kk, NT) * scale + bias
                p = jnp.where(band & (jj >= first), jnp.exp(sc - jnp.concatenate([lse_b, lse_b], axis=1)), 0.0)
                dp = _dot(do, vv, NT)
                ds = p * (dp - jnp.concatenate([dl_b, dl_b], axis=1))
                dv_b = _dot(p, do, TN)
                dk_b = _dot(ds, q, TN) * scale
                dq_t[cur, :] = _dot(ds, kk, NN) * scale
                done = pl.ds(pl.multiple_of(jnp.where(t == 0, n_t, t - 1) * ATTN_BLK, ATTN_BLK), ATTN_BLK)
                dk_t[done, :] = ck + dk_b[:ATTN_BLK]
                dv_t[done, :] = cv + dv_b[:ATTN_BLK]
                return dk_b[ATTN_BLK:], dv_b[ATTN_BLK:]

            zero = jnp.zeros((ATTN_BLK, HEAD_DIM), F32)
            ck, cv = _unrolled_loop(n_t, blk, (zero, zero))
            dk_t[(n_t - 1) * ATTN_BLK:n_t * ATTN_BLK, :] = ck
            dv_t[(n_t - 1) * ATTN_BLK:n_t * ATTN_BLK, :] = cv
            if dil > 1:
                for acc, part in ((dqa, dqd), (dka, dkd), (dva, dvd)):
                    for r in range(dil):
                        acc[pl.ds(r, seg, stride=dil), :] += part[r * seg:(r + 1) * seg, :]
        out_ref[0] = dqa[...].astype(out_ref.dtype)
        out_ref[1] = dka[0:s, :].astype(out_ref.dtype)
        out_ref[2] = dva[0:s, :].astype(out_ref.dtype)

    def col(slot):
        return pl.BlockSpec((None, s, HEAD_DIM), lambda h: (slot, 0, h))

    head = pl.BlockSpec((s, HEAD_DIM), lambda h: (0, h))
    return pl.pallas_call(
        body, name="attn_bwd", grid=(heads,),
        in_specs=[col(0), col(1), col(2), pl.BlockSpec((None, 1, HEAD_DIM), lambda h: (h, 0, 0)),
                  head, head, head],
        out_specs=pl.BlockSpec((N_QKV, s, HEAD_DIM), lambda h: (0, 0, h)),
        out_shape=jax.ShapeDtypeStruct((N_QKV, s, d), BF16),
        scratch_shapes=[pltpu.VMEM((s, HEAD_DIM), BF16)] * 4 + [pltpu.VMEM((s, HEAD_DIM), F32)] * 4
        + [pltpu.VMEM((s + ATTN_BLK, HEAD_DIM), F32)] * 2 + [pltpu.VMEM((s, HEAD_DIM), F32)]
        + [pltpu.VMEM((s + ATTN_BLK, HEAD_DIM), F32)] * 2,
        compiler_params=_params(48),
    )(proj_a, proj_a, proj_a, slopes, y_attn, lse, dy)


def _expm1(x):
    small = x * (1.0 + x * (0.5 + x * (1.0 / 6.0 + x * (1.0 / 24.0 + x * (1.0 / 120.0)))))
    return jnp.where(jnp.abs(x) < 0.1, small, jnp.exp(x) - 1.0)


def _softplus(x):
    return jnp.maximum(x, 0.0) + jnp.log1p(jnp.exp(-jnp.abs(x)))


GELU_K = 0.7978845608028654
GELU_C = 0.044715


def _gelu(x):
    t = jnp.tanh(GELU_K * (x + GELU_C * x * x * x))
    return 0.5 * x * (1.0 + t), t


def _gelu_grad(x, t):
    return 0.5 * (1.0 + t) + 0.5 * x * (1.0 - t * t) * GELU_K * (1.0 + 3.0 * GELU_C * x * x)


def _lru_gates(xc, wa, ba, wx, bx, sp):
    r = _sigmoid(_dot(xc, wa, NN) + ba)
    ig = _sigmoid(_dot(xc, wx, NN) + bx)
    log_a = -LRU_C * r * sp
    a = jnp.exp(log_a)
    mult = jnp.sqrt(-_expm1(2.0 * log_a))
    return r, ig, a, mult


def _scan_fwd(a, u, tt):
    row = lax.broadcasted_iota(jnp.int32, a.shape, 0)
    sh = 1
    while sh < tt:
        keep = row >= sh
        a_s = jnp.where(keep, pltpu.roll(a, sh, 0), 1.0)
        u_s = jnp.where(keep, pltpu.roll(u, sh, 0), 0.0)
        u = a * u_s + u
        a = a * a_s
        sh *= 2
    return a, u


def _scan_bwd(b, g, tt):
    row = lax.broadcasted_iota(jnp.int32, b.shape, 0)
    sh = 1
    while sh < tt:
        keep = row < tt - sh
        b_s = jnp.where(keep, pltpu.roll(b, tt - sh, 0), 1.0)
        g_s = jnp.where(keep, pltpu.roll(g, tt - sh, 0), 0.0)
        g = g + b * g_s
        b = b * b_s
        sh *= 2
    return b, g


def _conv_rows(xpad_ref, cw, cb, s):
    acc = cb
    for j in range(CONV_TAPS):
        off = 8 - (CONV_TAPS - 1) + j
        acc = acc + cw[j:j + 1, :] * xpad_ref[off:off + s, :]
    return acc


LRU_TILE = 128


def _lru_specs(s, d):
    heads = d // HEAD_DIM

    def col(slot):
        return pl.BlockSpec((None, s, HEAD_DIM), lambda h: (slot, 0, h))

    vec = pl.BlockSpec((1, HEAD_DIM), lambda h: (0, h))
    mat = pl.BlockSpec((None, HEAD_DIM, HEAD_DIM), lambda h: (h, 0, 0))
    cw = pl.BlockSpec((8, HEAD_DIM), lambda h: (0, h))
    head = pl.BlockSpec((s, HEAD_DIM), lambda h: (0, h))
    return heads, col, vec, mat, cw, head


def _lru_fwd(proj_b, conv_w, conv_b, wa, ba, wx, bx, lam):
    _, s, d = proj_b.shape
    heads, col, vec, mat, cws, head = _lru_specs(s, d)
    tt = LRU_TILE

    def body(xr_ref, xg_ref, cw_ref, cb_ref, wa_ref, ba_ref, wx_ref, bx_ref, lam_ref, y_ref, h_ref, xpad, xc_s):
        xpad[0:8, :] = jnp.zeros((8, HEAD_DIM), F32)
        xpad[8:8 + s, :] = xr_ref[...]
        xc_s[...] = _conv_rows(xpad, cw_ref[...], cb_ref[...], s)
        sp = _softplus(-lam_ref[...])
        wav, wxv, bav, bxv = wa_ref[...], wx_ref[...], ba_ref[...], bx_ref[...]

        def tile(i, hc):
            rows = pl.ds(pl.multiple_of(i * tt, tt), tt)
            xc = xc_s[rows, :]
            _, ig, a, mult = _lru_gates(xc, wav, bav, wxv, bxv, sp)
            pa, hl = _scan_fwd(a, mult * (ig * xc), tt)
            h = hl + pa * hc
            h_ref[rows, :] = h
            gel, _ = _gelu(xg_ref[rows, :])
            y_ref[rows, :] = (h * gel).astype(y_ref.dtype)
            return h[tt - 1:tt, :]

        lax.fori_loop(0, s // tt, tile, jnp.zeros((1, HEAD_DIM), F32))

    return pl.pallas_call(
        body, name="lru_fwd", grid=(heads,),
        in_specs=[col(0), col(1), cws, vec, mat, vec, mat, vec, vec],
        out_specs=[head, head],
        out_shape=[jax.ShapeDtypeStruct((s, d), BF16), jax.ShapeDtypeStruct((s, d), F32)],
        scratch_shapes=[pltpu.VMEM((s + 8, HEAD_DIM), F32), pltpu.VMEM((s, HEAD_DIM), F32)],
        compiler_params=_params(32),
    )(proj_b, proj_b, conv_w, conv_b, wa, ba, wx, bx, lam)


def _lru_bwd(proj_b, h_lru, dy, conv_w, conv_b, wa, ba, wx, bx, lam, dproj_b):
    _, s, d = proj_b.shape
    heads, col, vec, mat, cws, head = _lru_specs(s, d)
    tt = LRU_TILE
    n_t = s // tt

    def body(xr_ref, xg_ref, h_ref, dy_ref, cw_ref, cb_ref, wa_ref, ba_ref, wx_ref, bx_ref, lam_ref, alias_ref,
             out_ref, dcw_ref, dcb_ref, dwa_ref, dba_ref, dwx_ref, dbx_ref, dlam_ref, xpad, xc_s, dxc_s):
        del alias_ref
        xpad[0:8, :] = jnp.zeros((8, HEAD_DIM), F32)
        xpad[8:8 + s, :] = xr_ref[...]
        cwv = cw_ref[...]
        xc_s[...] = _conv_rows(xpad, cwv, cb_ref[...], s)
        dxc_s[s:s + 8, :] = jnp.zeros((8, HEAD_DIM), F32)
        lamv = lam_ref[...]
        sp = _softplus(-lamv)
        wav, wxv, bav, bxv = wa_ref[...], wx_ref[...], ba_ref[...], bx_ref[...]
        dwa_ref[...] = jnp.zeros_like(dwa_ref)
        dwx_ref[...] = jnp.zeros_like(dwx_ref)
        zero = jnp.zeros((1, HEAD_DIM), F32)
        row = lax.broadcasted_iota(jnp.int32, (tt, HEAD_DIM), 0)

        def tile(it, carry):
            dh_next, a_next, dba, dbx, dsp, dcb = carry
            i = n_t - 1 - it
            t0 = pl.multiple_of(i * tt, tt)
            rows = pl.ds(t0, tt)
            xc = xc_s[rows, :]
            r, ig, a, mult = _lru_gates(xc, wav, bav, wxv, bxv, sp)
            h = h_ref[rows, :]
            before = h_ref[pl.ds(pl.multiple_of(jnp.maximum(t0 - 8, 0), 8), 8), :][7:8, :]
            before = before * (i > 0).astype(F32)
            h_prev = jnp.where(row == 0, before, pltpu.roll(h, 1, 0))
            xg = xg_ref[rows, :]
            dyv = dy_ref[rows, :]
            gel, th = _gelu(xg)
            out_ref[1, rows, :] = (dyv * h * _gelu_grad(xg, th)).astype(out_ref.dtype)
            b = jnp.where(row == tt - 1, a_next, pltpu.roll(a, tt - 1, 0))
            pb, z = _scan_bwd(b, dyv * gel, tt)
            dh = z + pb * dh_next
            da = dh * h_prev
            dmult = dh * (ig * xc)
            dig = dh * (mult * xc)
            dla = da * a - dmult * (a * a / mult)
            dzr = dla * (-LRU_C * sp) * (r * (1.0 - r))
            dzx = dig * (ig * (1.0 - ig))
            dxc = dh * (mult * ig) + _dot(dzr, wav, NT) + _dot(dzx, wxv, NT)
            dxc_s[rows, :] = dxc
            dwa_ref[...] += _dot(xc, dzr, TN)
            dwx_ref[...] += _dot(xc, dzx, TN)
            return (dh[0:1, :], a[0:1, :],
                    dba + jnp.sum(dzr, axis=0, keepdims=True),
                    dbx + jnp.sum(dzx, axis=0, keepdims=True),
                    dsp + jnp.sum(dla * (-LRU_C * r), axis=0, keepdims=True),
                    dcb + jnp.sum(dxc, axis=0, keepdims=True))

        _, _, dba, dbx, dsp, dcb = lax.fori_loop(0, n_t, tile, (zero, zero, zero, zero, zero, zero))
        dba_ref[...] = dba
        dbx_ref[...] = dbx
        dcb_ref[...] = dcb
        dlam_ref[...] = -dsp * _sigmoid(-lamv)
        dxr = jnp.zeros((s, HEAD_DIM), F32)
        for j in range(CONV_TAPS):
            back = CONV_TAPS - 1 - j
            off = 8 - back
            dcw_ref[j:j + 1, :] = jnp.sum(dxc_s[0:s, :] * xpad[off:off + s, :], axis=0, keepdims=True)
            dxr = dxr + cwv[j:j + 1, :] * dxc_s[back:back + s, :]
        out_ref[0] = dxr.astype(out_ref.dtype)

    return pl.pallas_call(
        body, name="lru_bwd", grid=(heads,),
        in_specs=[col(0), col(1), head, head, cws, vec, mat, vec, mat, vec, vec, pl.BlockSpec(memory_space=pl.ANY)],
        out_specs=[pl.BlockSpec((2, s, HEAD_DIM), lambda h: (0, 0, h)),
                   pl.BlockSpec((CONV_TAPS, HEAD_DIM), lambda h: (0, h)), vec, mat, vec, mat, vec, vec],
        out_shape=[jax.ShapeDtypeStruct(dproj_b.shape, dproj_b.dtype),
                   jax.ShapeDtypeStruct((CONV_TAPS, d), F32), jax.ShapeDtypeStruct((1, d), F32),
                   jax.ShapeDtypeStruct(wa.shape, F32), jax.ShapeDtypeStruct((1, d), F32),
                   jax.ShapeDtypeStruct(wx.shape, F32), jax.ShapeDtypeStruct((1, d), F32),
                   jax.ShapeDtypeStruct((1, d), F32)],
        scratch_shapes=[pltpu.VMEM((s + 8, HEAD_DIM), F32), pltpu.VMEM((s, HEAD_DIM), F32),
                        pltpu.VMEM((s + 8, HEAD_DIM), F32)],
        input_output_aliases={11: 0},
        compiler_params=_params(32),
    )(proj_b, proj_b, h_lru, dy, conv_w, conv_b, wa, ba, wx, bx, lam, dproj_b)


def _place():
    x, y, c = (lax.axis_index(n) for n in AXES)
    return x, y, c


def _other_chips(x, y):
    return [(1 - x, y), (x, 1 - y), (1 - x, 1 - y)]


HBM = pl.BlockSpec(memory_space=pl.ANY)


def _cast_shard(w, chip_arr, name):
    r, cols = w.shape
    rh = r // 2
    tr = _row_tile(rh, cols * 4, STREAM_TILE)
    nt = rh // tr

    def body(chip_ref, w_ref, o_ref):
        del chip_ref
        o_ref[...] = w_ref[...].astype(BF16)

    return pl.pallas_call(
        body, name=name,
        grid_spec=pltpu.PrefetchScalarGridSpec(
            num_scalar_prefetch=1, grid=(2, nt),
            in_specs=[pl.BlockSpec((tr, cols), lambda h, i, chip_ref: (h * nt + i, 0))],
            out_specs=pl.BlockSpec((None, None, tr, cols), lambda h, i, chip_ref: (chip_ref[0], h, i, 0))),
        out_shape=jax.ShapeDtypeStruct((N_CHIPS, 2, rh, cols), BF16), compiler_params=_params(32),
    )(chip_arr, w)


HBM_SPEC = pl.BlockSpec(memory_space=pltpu.HBM)
SEM_SPEC = pl.BlockSpec(memory_space=pltpu.SEMAPHORE)
EFFECT = pltpu.SideEffectType.DATAFLOW_SIDE_EFFECTING
TOKEN = jax.ShapeDtypeStruct((8, 128), F32)
TOKEN_SPEC = pl.BlockSpec(memory_space=pltpu.VMEM)


def _in_hbm(arrays):
    return [pltpu.with_memory_space_constraint(a, pltpu.HBM) for a in arrays]


def _hbm_like(arrays):
    return [pltpu.HBM(a.shape, a.dtype) for a in arrays]


def _sems(n):
    return pltpu.SemaphoreType.DMA((n,))


def _remote(src, dst, send_sem, recv_sem, to):
    return pltpu.make_async_remote_copy(src_ref=src, dst_ref=dst, send_sem=send_sem, recv_sem=recv_sem,
                                        device_id=to, device_id_type=MESH)


def _gather_start(bufs, groups, after, name):
    n = len(bufs)
    ng = len(groups)

    def body(*refs):
        ins = refs[:n]
        sems = refs[n + len(after):n + len(after) + 2 * ng]
        x, y, c = _place()
        me = 2 * x + y
        for g, ws in enumerate(groups):
            for i, w in enumerate(ws):
                for j, (cx, cy) in enumerate(_other_chips(x, y)):
                    mine = ins[w].at[me, c]
                    _remote(mine, mine, sems[2 * g].at[3 * i + j], sems[2 * g + 1].at[3 * i + j], (cx, cy, c)).start()

    sem_shapes = []
    for ws in groups:
        sem_shapes += [_sems(3 * len(ws)), _sems(3 * len(ws))]
    res = pl.pallas_call(
        body, name=name, in_specs=[HBM_SPEC] * n + [HBM] * len(after),
        out_specs=[SEM_SPEC] * (2 * ng) + [HBM_SPEC] * n, out_shape=sem_shapes + _hbm_like(bufs),
        input_output_aliases={w: 2 * ng + w for w in range(n)},
        compiler_params=pltpu.CompilerParams(has_side_effects=EFFECT),
    )(*_in_hbm(bufs), *after)
    return [(res[2 * g], res[2 * g + 1]) for g in range(ng)], list(res[2 * ng:])


def _gather_forward(bufs, recv, after, name):
    m = len(bufs)

    def body(*refs):
        ins, recv_in = refs[:m], refs[m]
        fsend, frecv = refs[m + 1 + len(after)], refs[m + 2 + len(after)]
        x, y, c = _place()
        for j, (cx, cy) in enumerate(_other_chips(x, y)):
            for i in range(m):
                landed = ins[i].at[2 * cx + cy, c]
                k = 3 * i + j
                _remote(landed, landed, fsend.at[k], recv_in.at[k], (cx, cy, c)).wait_recv()
                _remote(landed, landed, fsend.at[k], frecv.at[k], (x, y, 1 - c)).start()

    res = pl.pallas_call(
        body, name=name, in_specs=[HBM_SPEC] * m + [SEM_SPEC] + [HBM] * len(after),
        out_specs=[SEM_SPEC, SEM_SPEC] + [HBM_SPEC] * m, out_shape=[_sems(3 * m), _sems(3 * m)] + _hbm_like(bufs),
        input_output_aliases={i: 2 + i for i in range(m)},
        compiler_params=pltpu.CompilerParams(has_side_effects=EFFECT),
    )(*bufs, recv, *after)
    return (res[0], res[1]), list(res[2:])


def _gather_finish(bufs, send, fsend, frecv, after, name):
    m = len(bufs)

    def body(*refs):
        ins = refs[:m]
        send_in, fsend_in, frecv_in = refs[m:m + 3]
        x, y, c = _place()
        me = 2 * x + y
        for j, (cx, cy) in enumerate(_other_chips(x, y)):
            cj = 2 * cx + cy
            for i in range(m):
                k = 3 * i + j
                mine = ins[i].at[me, c]
                _remote(mine, mine, send_in.at[k], frecv_in.at[k], (cx, cy, c)).wait_send()
                landed = ins[i].at[cj, c]
                _remote(landed, landed, fsend_in.at[k], frecv_in.at[k], (x, y, 1 - c)).wait_send()
                theirs = ins[i].at[cj, 1 - c]
                _remote(theirs, theirs, fsend_in.at[k], frecv_in.at[k], (x, y, 1 - c)).wait_recv()

    return list(pl.pallas_call(
        body, name=name, in_specs=[HBM_SPEC] * m + [SEM_SPEC] * 3 + [HBM] * len(after),
        out_specs=[HBM_SPEC] * m, out_shape=_hbm_like(bufs),
        input_output_aliases={i: i for i in range(m)},
        compiler_params=pltpu.CompilerParams(has_side_effects=EFFECT),
    )(*bufs, send, fsend, frecv, *after))


def _pair_exchange(grads, name):
    n = len(grads)

    def body(*refs):
        ins, outs = refs[:n], refs[n:2 * n]
        send_sems, recv_sems = refs[2 * n:]
        x, y, c = _place()
        sibling = (x, y, 1 - c)
        cps = []
        for w in range(n):
            for j in range(N_CHIPS):
                cp = pltpu.make_async_remote_copy(
                    src_ref=ins[w].at[j, 1 - c], dst_ref=outs[w].at[j], send_sem=send_sems.at[N_CHIPS * w + j],
                    recv_sem=recv_sems.at[N_CHIPS * w + j], device_id=sibling, device_id_type=MESH)
                cp.start()
                cps.append(cp)
        for cp in cps:
            cp.wait()

    return pl.pallas_call(
        body, name=name, in_specs=[HBM] * n, out_specs=[HBM] * n,
        out_shape=[jax.ShapeDtypeStruct((N_CHIPS,) + a.shape[2:], a.dtype) for a in grads],
        scratch_shapes=[pltpu.SemaphoreType.DMA((N_CHIPS * n,)), pltpu.SemaphoreType.DMA((N_CHIPS * n,))],
    )(*grads)


def _chip_start(sums, name, to_all=()):
    m = len(sums)
    lands = [lax.empty(((N_CHIPS,) if i in to_all else ()) + a.shape, a.dtype) for i, a in enumerate(sums)]

    def body(*refs):
        ins, land_in = refs[:m], refs[m:2 * m]
        send, recv = refs[2 * m], refs[2 * m + 1]
        token = refs[4 * m + 2]
        x, y, c = _place()
        me = 2 * x + y
        for i in sorted(range(m), key=lambda i: i not in to_all):
            for j, (cx, cy) in enumerate(_other_chips(x, y)):
                src = ins[i] if i in to_all else ins[i].at[2 * cx + cy]
                _remote(src, land_in[i].at[me], send.at[3 * i + j], recv.at[3 * i + j], (cx, cy, c)).start()
        token[...] = jnp.zeros_like(token)

    res = pl.pallas_call(
        body, name=name, in_specs=[HBM_SPEC] * (2 * m),
        out_specs=[SEM_SPEC, SEM_SPEC] + [HBM_SPEC] * (2 * m) + [TOKEN_SPEC],
        out_shape=[_sems(3 * m), _sems(3 * m)] + _hbm_like(sums) + _hbm_like(lands) + [TOKEN],
        input_output_aliases={i: 2 + i for i in range(2 * m)},
        compiler_params=pltpu.CompilerParams(has_side_effects=EFFECT),
    )(*_in_hbm(sums), *_in_hbm(lands))
    return (res[0], res[1]), list(res[2:2 + m]), list(res[2 + m:2 + 2 * m]), res[2 + 2 * m]


def _chip_wait(sems, sums, lands, after, name, to_all=()):
    m = len(sums)

    def body(*refs):
        ins, land_in = refs[:m], refs[m:2 * m]
        send_in, recv_in = refs[2 * m], refs[2 * m + 1]
        x, y, c = _place()
        for i in range(m):
            for j, (cx, cy) in enumerate(_other_chips(x, y)):
                cj = 2 * cx + cy
                src = ins[i] if i in to_all else ins[i].at[cj]
                cp = _remote(src, land_in[i].at[cj], send_in.at[3 * i + j], recv_in.at[3 * i + j], (cx, cy, c))
                cp.wait_send()
                cp.wait_recv()

    res = pl.pallas_call(
        body, name=name, in_specs=[HBM_SPEC] * (2 * m) + [SEM_SPEC, SEM_SPEC] + [HBM] * len(after),
        out_specs=[HBM_SPEC] * (2 * m), out_shape=_hbm_like(sums) + _hbm_like(lands),
        input_output_aliases={i: i for i in range(2 * m)},
        compiler_params=pltpu.CompilerParams(has_side_effects=EFFECT),
    )(*sums, *lands, sems[0], sems[1], *after)
    return list(res[:m]), list(res[m:])


def _half_exchange(bufs, name):
    n = len(bufs)
    parts = []
    for w, a in enumerate(bufs):
        parts += [(w, None)] if a.ndim == 3 else [(w, j) for j in range(a.shape[0])]
    np_ = len(parts)

    def body(*refs):
        outs = refs[n:2 * n]
        send_sems, recv_sems = refs[2 * n:]
        x, y, c = _place()
        sibling = (x, y, 1 - c)

        def half(w, j, h):
            return outs[w].at[h] if j is None else outs[w].at[j, h]

        cps = []
        for k, (w, j) in enumerate(parts):
            rc = _remote(half(w, j, c), half(w, j, c), send_sems.at[k], recv_sems.at[k], sibling)
            rc.start()
            cps.append(rc)
        for k, (w, j) in enumerate(parts):
            theirs = half(w, j, 1 - c)
            _remote(theirs, theirs, send_sems.at[k], recv_sems.at[k], sibling).wait_recv()
        for cp in cps:
            cp.wait_send()

    return pl.pallas_call(
        body, name=name, in_specs=[HBM] * n, out_specs=[HBM] * n,
        out_shape=[jax.ShapeDtypeStruct(a.shape, a.dtype) for a in bufs],
        input_output_aliases={w: w for w in range(n)},
        scratch_shapes=[_sems(np_), _sems(np_)],
    )(*bufs)


def _all_gather8(block, name, after=()):
    def body(in_ref, *rest):
        out_ref, send_sems, recv_sems, local_sem = rest[len(after):]
        x, y, c = _place()
        me = 4 * x + 2 * y + c
        mine = pltpu.make_async_copy(in_ref, out_ref.at[me], local_sem)
        mine.start()
        flips = [(fx, fy, fc) for fx in (0, 1) for fy in (0, 1) for fc in (0, 1)][1:]
        cps = []
        for k, (fx, fy, fc) in enumerate(flips):
            cp = pltpu.make_async_remote_copy(
                src_ref=in_ref, dst_ref=out_ref.at[me], send_sem=send_sems.at[k], recv_sem=recv_sems.at[k],
                device_id=(x ^ fx, y ^ fy, c ^ fc), device_id_type=MESH)
            cp.start()
            cps.append(cp)
        for k, (fx, fy, fc) in enumerate(flips):
            px, py, pc = x ^ fx, y ^ fy, c ^ fc
            theirs = out_ref.at[4 * px + 2 * py + pc]
            pltpu.make_async_remote_copy(
                src_ref=theirs, dst_ref=theirs, send_sem=send_sems.at[k], recv_sem=recv_sems.at[k],
                device_id=(px, py, pc), device_id_type=MESH).wait_recv()
        for cp in cps:
            cp.wait_send()
        mine.wait()

    return pl.pallas_call(
        body, name=name, in_specs=[HBM] * (1 + len(after)), out_specs=HBM,
        out_shape=jax.ShapeDtypeStruct((N_DEV,) + block.shape, block.dtype),
        scratch_shapes=[pltpu.SemaphoreType.DMA((N_DEV - 1,)), pltpu.SemaphoreType.DMA((N_DEV - 1,)),
                        pltpu.SemaphoreType.DMA],
    )(block, *after)


def _pair_sum(grad, recv, c_arr, name):
    _, _, rh, cols = grad.shape
    tr = _row_tile(rh, cols * grad.dtype.itemsize, STREAM_TILE // 2)

    def body(c_ref, g_ref, r_ref, o_ref):
        del c_ref
        o_ref[...] = (g_ref[...].astype(F32) + r_ref[...].astype(F32)).astype(o_ref.dtype)

    spec = pl.BlockSpec((None, tr, cols), lambda j, i, c_ref: (j, i, 0))
    return pl.pallas_call(
        body, name=name,
        grid_spec=pltpu.PrefetchScalarGridSpec(
            num_scalar_prefetch=1, grid=(N_CHIPS, rh // tr),
            in_specs=[pl.BlockSpec((None, None, tr, cols), lambda j, i, c_ref: (j, c_ref[0], i, 0)), spec],
            out_specs=spec),
        out_shape=jax.ShapeDtypeStruct(recv.shape, grad.dtype), compiler_params=_params(32),
    )(c_arr, grad, recv)


def _sum_by_chip(chip, p_ref, own_ref, o_ref):
    o_ref[...] = jnp.zeros_like(o_ref)
    for k in range(N_CHIPS):
        @pl.when(chip == k)
        def _():
            o_ref[...] += own_ref[...].astype(F32)

        @pl.when(chip != k)
        def _(k=k):
            o_ref[...] += p_ref[k].astype(F32)


def _chip_sum_all(parts, own, place_arr, name):
    _, nj, rh, cols = parts.shape
    tr = _row_tile(rh, cols * 4, STREAM_TILE // 2)

    def body(place_ref, p_ref, own_ref, o_ref):
        _sum_by_chip(place_ref[0], p_ref, own_ref, o_ref)

    return pl.pallas_call(
        body, name=name,
        grid_spec=pltpu.PrefetchScalarGridSpec(
            num_scalar_prefetch=1, grid=(nj, rh // tr),
            in_specs=[pl.BlockSpec((N_CHIPS, None, tr, cols), lambda j, i, place_ref: (0, j, i, 0)),
                      pl.BlockSpec((None, tr, cols), lambda j, i, place_ref: (j, i, 0))],
            out_specs=pl.BlockSpec((None, None, tr, cols), lambda j, i, place_ref: (j, place_ref[1], i, 0))),
        out_shape=jax.ShapeDtypeStruct((nj, 2, rh, cols), F32), compiler_params=_params(32),
    )(place_arr, parts, own)


def _chip_sum(parts, own, place_arr, name):
    _, rh, cols = parts.shape
    tr = _row_tile(rh, cols * 4, STREAM_TILE // 2)

    def body(place_ref, p_ref, own_ref, o_ref):
        _sum_by_chip(place_ref[0], p_ref, own_ref, o_ref)

    return pl.pallas_call(
        body, name=name,
        grid_spec=pltpu.PrefetchScalarGridSpec(
            num_scalar_prefetch=1, grid=(rh // tr,),
            in_specs=[pl.BlockSpec((N_CHIPS, tr, cols), lambda i, place_ref: (0, i, 0)),
                      pl.BlockSpec((None, tr, cols), lambda i, place_ref: (place_ref[0], i, 0))],
            out_specs=pl.BlockSpec((None, tr, cols), lambda i, place_ref: (place_ref[1], i, 0))),
        out_shape=jax.ShapeDtypeStruct((2, rh, cols), F32), compiler_params=_params(32),
    )(place_arr, parts, own)


def _adamw_math(w, g, m, v):
    m = ADAM_B1 * m + (1.0 - ADAM_B1) * g
    v = ADAM_B2 * v + (1.0 - ADAM_B2) * (g * g)
    m_hat = m / (1.0 - ADAM_B1 ** ADAM_STEP)
    v_hat = v / (1.0 - ADAM_B2 ** ADAM_STEP)
    delta = -ADAM_LR * (m_hat / (jnp.sqrt(v_hat) + ADAM_EPS) + ADAM_WD * w)
    return delta, m, v


def _adamw(w, g, m, v, name):
    rows, cols = w.shape
    tr = _row_tile(rows, cols * 4)

    def body(w_ref, g_ref, m_ref, v_ref, go_ref, d_ref, nm_ref, nv_ref):
        gv = g_ref[...]
        go_ref[...] = gv
        d_ref[...], nm_ref[...], nv_ref[...] = _adamw_math(w_ref[...], gv, m_ref[...], v_ref[...])

    spec = pl.BlockSpec((tr, cols), lambda i: (i, 0))
    return pl.pallas_call(
        body, name=name, grid=(rows // tr,), in_specs=[spec] * 4, out_specs=[spec] * 4,
        out_shape=[jax.ShapeDtypeStruct(w.shape, F32)] * 4, compiler_params=_params(32),
    )(w, g, m, v)


def _sum8_adamw_row(parts, w, m, v, name):
    cols = parts.shape[2]

    def body(p_ref, w_ref, m_ref, v_ref, g_ref, d_ref, nm_ref, nv_ref):
        g = p_ref[0, 0:1, :]
        for k in range(1, N_DEV):
            g = g + p_ref[k, 0:1, :]
        g_ref[...] = g
        d_ref[...], nm_ref[...], nv_ref[...] = _adamw_math(w_ref[...], g, m_ref[...], v_ref[...])

    return pl.pallas_call(
        body, name=name, out_shape=[jax.ShapeDtypeStruct((1, cols), F32)] * 4, compiler_params=_params(32),
    )(parts, w, m, v)


def _pack_rows(pieces, rows, name):
    cols = pieces[0].shape[1]
    n = len(pieces)

    def body(*refs):
        o_ref = refs[n]
        o_ref[...] = jnp.zeros_like(o_ref)
        at = 0
        for p_ref in refs[:n]:
            r = p_ref.shape[0]
            o_ref[at:at + r, :] = p_ref[...]
            at += r

    return pl.pallas_call(
        body, name=name, out_shape=jax.ShapeDtypeStruct((rows, cols), F32), compiler_params=_params(32),
    )(*pieces)


def kernel(x, norm_mix_g, w_in, conv_w, conv_b, lru_wa, lru_ba, lru_wx, lru_bx, lru_lambda, w_proj_attn, w_proj_lru, w_out, norm_mlp_g, w_up, w_down, norm_final_g, loss_target, m_norm_mix_g, m_w_in, m_conv_w, m_conv_b, m_lru_wa, m_lru_ba, m_lru_wx, m_lru_bx, m_lru_lambda, m_w_proj_attn, m_w_proj_lru, m_w_out, m_norm_mlp_g, m_w_up, m_w_down, m_norm_final_g, v_norm_mix_g, v_w_in, v_conv_w, v_conv_b, v_lru_wa, v_lru_ba, v_lru_wx, v_lru_bx, v_lru_lambda, v_w_proj_attn, v_w_proj_lru, v_w_out, v_norm_mlp_g, v_w_up, v_w_down, v_norm_final_g):
    s, d = x.shape[1], x.shape[2]
    ff = w_up.shape[2] * N_CHIPS
    heads = d // HEAD_DIM
    u = d // 4
    assert s % (max(DILATIONS) * ATTN_BLK) == 0 and d % (4 * HEAD_DIM) == 0 and ff == 4 * d and DILATIONS[0] == 1
    xs, target = _in_hbm([x[0], loss_target[0]])
    gf = norm_final_g.reshape(1, d)
    wa, wx = lru_wa[0], lru_wx[0]
    core = lax.axis_index("c").astype(jnp.int32)
    chip = (2 * lax.axis_index("x") + lax.axis_index("y")).astype(jnp.int32)
    cidx = core.reshape(1)
    chip_arr = chip.reshape(1)
    place_arr = jnp.stack([chip, core])
    slopes = jnp.broadcast_to(
        (2.0 ** (-8.0 * jnp.arange(1, heads + 1, dtype=F32) / heads))[:, None, None], (heads, 1, HEAD_DIM))

    big = _in_hbm([w_in[0], w_proj_attn[0], w_proj_lru[0], w_out[0], w_up[0], w_down[0]])
    names = ["w_in", "w_proj_attn", "w_proj_lru", "w_out", "w_up", "w_down"]
    cw_pad = jnp.pad(conv_w[0], ((0, 8 - CONV_TAPS), (0, 0)))
    cw_all = _all_gather8(cw_pad, "gather_conv_w")
    conv_w_full = jnp.concatenate([cw_all[2 * j] for j in range(N_CHIPS)], axis=1)
    (sem_a,), buf_a = _gather_start([_cast_shard(big[0], chip_arr, "cast_w_in")], [[0]], [cw_all], "gather_start_w_in")
    bufs = [_cast_shard(w, chip_arr, "cast_" + nm) for w, nm in zip(big[1:], names[1:])]
    (sem_b, sem_c), bufs = _gather_start(bufs, [[0, 1, 2], [3, 4]], buf_a, "gather_start_rest")
    fsem_a, buf_a = _gather_forward(buf_a, sem_a[1], bufs[:1], "gather_forward_w_in")
    buf_a = _gather_finish(buf_a, sem_a[0], fsem_a[0], fsem_a[1], [], "gather_finish_w_in")
    w_in_g = buf_a[0].reshape(N_CHIPS, d, N_SLOTS * u)

    xn = _rms_fwd(xs, norm_mix_g, "norm_mix")
    proj_a = _proj_in(xn, w_in_g, 0, 4 * N_QKV, N_QKV)
    proj_b = _proj_in(xn, w_in_g, 4 * N_QKV, 4 * (N_SLOTS - N_QKV), N_SLOTS - N_QKV)
    y_attn, lse = _attn_fwd(proj_a, slopes)
    y_lru, h_lru = _lru_fwd(proj_b, conv_w_full, conv_b, wa, lru_ba, wx, lru_bx, lru_lambda)
    fsem_b, buf_b = _gather_forward(bufs[:3], sem_b[1], [y_attn, y_lru], "gather_forward_proj")
    fsem_c, buf_c = _gather_forward(bufs[3:], sem_c[1], [buf_b[0]], "gather_forward_mlp")
    buf_b = _gather_finish(buf_b, sem_b[0], fsem_b[0], fsem_b[1], [buf_c[0]], "gather_finish_proj")
    wpa_g = buf_b[0].reshape(d, d)
    wpl_g = buf_b[1].reshape(d, d)
    wout_g = buf_b[2].reshape(d, d)

    tn = u
    sd_f32 = jax.ShapeDtypeStruct((s, d), F32)
    sd_bf16 = jax.ShapeDtypeStruct((s, d), BF16)
    col = pl.BlockSpec((s, tn), lambda i, j, k: (0, j))

    def slot(n):
        return pl.BlockSpec((None, s, tn), lambda i, j, k: (n, 0, j))

    p_attn = _mm_nn("proj_attn", y_attn, wpa_g, [], [], [sd_f32], [col], _store, tn)[0]

    def merge(acc, extras, outs):
        pa_ref, ga_ref, gl_ref = extras
        merged = _sigmoid(ga_ref[...]) * pa_ref[...] + _sigmoid(gl_ref[...]) * acc
        outs[0][...] = merged.astype(BF16)
        outs[1][...] = acc

    tn2 = max(HEAD_DIM, u // 2)
    col2 = pl.BlockSpec((s, tn2), lambda i, j, k: (0, j))

    def slot2(n):
        return pl.BlockSpec((None, s, tn2), lambda i, j, k: (n, 0, j))

    merged, p_lru = _mm_nn("proj_lru_merge", y_lru, wpl_g, [p_attn, proj_b, proj_b], [col2, slot2(2), slot2(3)],
                           [sd_bf16, sd_f32], [col2, col2], merge, tn2)

    def add_resid(acc, extras, outs):
        outs[0][...] = extras[0][...] + acc

    h1 = _mm_nn("w_out_resid", merged, wout_g, [xs], [col], [sd_f32], [col], add_resid, tn)[0]
    hn = _rms_fwd(h1, norm_mlp_g, "norm_mlp")
    buf_c = _gather_finish(buf_c, sem_c[0], fsem_c[0], fsem_c[1], [hn], "gather_finish_mlp")
    wup_g = buf_c[0].reshape(N_CHIPS, d, d)
    wdown_g = buf_c[1].reshape(ff, d)

    def relu_sq(acc, extras, outs):
        r = jnp.maximum(acc, 0.0)
        outs[0][...] = (r * r).astype(BF16)
        outs[1][...] = r.astype(BF16)

    sf_bf16 = jax.ShapeDtypeStruct((s, ff), BF16)
    hid, relu_up = _mm(
        "w_up_relu2", [hn, wup_g],
        [pl.BlockSpec((s, d), lambda i, j, k: (0, 0)),
         pl.BlockSpec((None, d, tn), lambda i, j, k: (j // 4, 0, j % 4))],
        [sf_bf16, sf_bf16], [col, col], (1, ff // tn, 1), NN, relu_sq)
    h2 = _mm(
        "w_down_resid", [hid, wdown_g, h1],
        [pl.BlockSpec((s, d), lambda i, j, k: (0, k)), pl.BlockSpec((d, tn), lambda i, j, k: (k, j)), col],
        [sd_f32], [col], (1, d // tn, ff // d), NN, add_resid, nk=ff // d, acc_shape=(s, tn))[0]
    loss_part, dh2, dh2_b, d_gf = _loss_head(h2, gf, target)
    loss = lax.psum(loss_part[0, 0], AXES)

    def relu_sq_bwd(acc, extras, outs):
        outs[0][...] = (acc * (2.0 * extras[0][...].astype(F32))).astype(BF16)

    dup = _mm_nt("d_hid", dh2_b, wdown_g, [relu_up], [col], [sf_bf16], [col], relu_sq_bwd, tn)[0]
    tok_d = pl.BlockSpec((s, d), lambda i, j: (0, 0))
    g_wdown = _mm_tn(
        "g_w_down", hid, dh2_b, pl.BlockSpec((s, d), lambda i, j: (0, i)),
        pl.BlockSpec((s, tn), lambda i, j: (0, j)), jax.ShapeDtypeStruct((ff, d), BF16),
        pl.BlockSpec((d, tn), lambda i, j: (i, j)), (ff // d, d // tn), d, tn, s)
    dhn = _mm(
        "d_hn", [dup, wup_g],
        [pl.BlockSpec((s, d), lambda i, j, k: (0, k)), pl.BlockSpec((None, tn, d), lambda i, j, k: (k, j, 0))],
        [sd_f32], [col], (1, d // tn, ff // d), NT, _store, nk=ff // d, acc_shape=(s, tn))[0]
    g_wup = _mm_tn(
        "g_w_up", hn, dup, tok_d, pl.BlockSpec((s, tn), lambda i, j: (0, j)),
        jax.ShapeDtypeStruct((N_CHIPS, d, d), BF16), pl.BlockSpec((None, d, tn), lambda i, j: (j // 4, 0, j % 4)),
        (1, ff // tn), d, tn, s)
    big_m = _in_hbm([m_w_in[0], m_w_proj_attn[0], m_w_proj_lru[0], m_w_out[0], m_w_up[0], m_w_down[0]])
    big_v = _in_hbm([v_w_in[0], v_w_proj_attn[0], v_w_proj_lru[0], v_w_out[0], v_w_up[0], v_w_down[0]])
    big_out = {}

    def reduce_begin(ids, gs, tag, everywhere=None):
        g4 = [g.reshape(N_CHIPS, 2, big[i].shape[0] // 2, big[i].shape[1]) for i, g in zip(ids, gs)]
        tags = [names[i] for i in ids]
        if everywhere is not None:
            g4.append(everywhere.reshape(N_CHIPS, 2, everywhere.shape[0] // (2 * N_CHIPS), everywhere.shape[1]))
            tags.append("small_" + tag)
        from_sibling = _pair_exchange(g4, "pair_exchange_" + tag)
        sums = [_pair_sum(g, r, cidx, "pair_sum_" + t) for t, g, r in zip(tags, g4, from_sibling)]
        to_all = () if everywhere is None else (len(ids),)
        return _chip_start(sums, "chip_start_" + tag, to_all), to_all

    def reduce_end(ids, begun, after, tag):
        (sems, sums, lands, _), to_all = begun
        sums, lands = _chip_wait(sems, sums, lands, after, "chip_wait_" + tag, to_all)
        halves = [_chip_sum(p, own, place_arr, "chip_sum_" + names[i]) for i, p, own in zip(ids, lands, sums)]
        if to_all:
            halves.append(_chip_sum_all(lands[-1], sums[-1], place_arr, "chip_sum_small_" + tag))
        full = _half_exchange(halves, "half_exchange_" + tag)
        last = None
        for i, g in zip(ids, full):
            res = _adamw(big[i], g.reshape(big[i].shape), big_m[i], big_v[i], "adamw_" + names[i])
            big_out[names[i]] = tuple(a[None] for a in res)
            last = res[1]
        everywhere = full[-1].reshape(-1, full[-1].shape[-1]) if to_all else None
        return everywhere, last

    def after_token(a, begun):
        return a + begun[0][3][:1, :1]

    red_mlp = reduce_begin([4, 5], [g_wup, g_wdown], "mlp")
    dh1, dh1_b, d_gmlp = _rms_bwd(h1, after_token(norm_mlp_g, red_mlp), dhn, dh2, "norm_mlp_bwd")

    g_wout = _mm_tn(
        "g_w_out", merged, dh1_b, tok_d, pl.BlockSpec((s, tn), lambda i, j: (0, j)),
        jax.ShapeDtypeStruct((d, d), BF16), pl.BlockSpec((d, tn), lambda i, j: (0, j)), (1, d // tn), d, tn, s)

    def merge_bwd(acc, extras, outs):
        pa_ref, pl_ref, ga_ref, gl_ref = extras
        sa, sl = _sigmoid(ga_ref[...]), _sigmoid(gl_ref[...])
        outs[0][...] = (acc * sa).astype(BF16)
        outs[1][...] = (acc * sl).astype(BF16)
        outs[2][0] = (acc * pa_ref[...] * (sa * (1.0 - sa))).astype(BF16)
        outs[2][1] = (acc * pl_ref[...] * (sl * (1.0 - sl))).astype(BF16)

    nb = N_SLOTS - N_QKV
    d_pa, d_pl, dproj_b = _mm_nt(
        "d_merged", dh1_b, wout_g, [p_attn, p_lru, proj_b, proj_b], [col2, col2, slot2(2), slot2(3)],
        [sd_bf16, sd_bf16, jax.ShapeDtypeStruct((nb, s, d), BF16)],
        [col2, col2, pl.BlockSpec((2, s, tn2), lambda i, j, k: (1, 0, j))], merge_bwd, tn2)
    dy_attn = _mm_nt("d_y_attn", d_pa, wpa_g, [], [], [sd_f32], [col], _store, tn)[0]
    dy_lru = _mm_nt("d_y_lru", d_pl, wpl_g, [], [], [sd_f32], [col], _store, tn)[0]
    g_wpa = _mm_tn(
        "g_w_proj_attn", y_attn, d_pa, tok_d, pl.BlockSpec((s, tn), lambda i, j: (0, j)),
        jax.ShapeDtypeStruct((d, d), BF16), pl.BlockSpec((d, tn), lambda i, j: (0, j)), (1, d // tn), d, tn, s)
    g_wpl = _mm_tn(
        "g_w_proj_lru", y_lru, d_pl, tok_d, pl.BlockSpec((s, tn), lambda i, j: (0, j)),
        jax.ShapeDtypeStruct((d, d), BF16), pl.BlockSpec((d, tn), lambda i, j: (0, j)), (1, d // tn), d, tn, s)

    red_proj = reduce_begin([1, 2, 3], [g_wpa, g_wpl, g_wout], "proj")

    dproj_a = _attn_bwd(proj_a, after_token(slopes, red_proj), y_attn, lse, dy_attn)
    dproj_b, d_cw, d_cb, d_wa, d_ba, d_wx, d_bx, d_lam = _lru_bwd(
        proj_b, h_lru, dy_lru, conv_w_full, conv_b, wa, lru_ba, wx, lru_bx, after_token(lru_lambda, red_proj),
        dproj_b)
    per = N_SLOTS
    g_win_shape = jax.ShapeDtypeStruct((N_CHIPS, d, N_SLOTS * u), BF16)

    def g_win_part(name, dproj, first, prev):
        n_units = 4 * dproj.shape[0]
        return _mm_tn(
            name, xn, dproj, tok_d, pl.BlockSpec((None, s, u), lambda i, j: (j // 4, 0, j % 4)),
            g_win_shape, pl.BlockSpec((None, d, u), lambda i, j: ((j + first) // per, 0, (j + first) % per)),
            (1, n_units), d, u, s, aliases=None if prev is None else {2: 0}, extra=prev)

    mat_rows = heads * HEAD_DIM * HEAD_DIM // d
    vec_names = ["norm_mix_g", "conv_b", "lru_ba", "lru_bx", "lru_lambda", "norm_mlp_g", "norm_final_g"]

    def pack(wa_, wx_, cw_, vecs, name):
        rows = [wa_.reshape(mat_rows, d), wx_.reshape(mat_rows, d), cw_] + [a.reshape(1, d) for a in vecs]
        n = sum(a.shape[0] for a in rows)
        return _pack_rows(rows, n + (-n % 64), name)

    zero_cw = jnp.zeros((CONV_TAPS, d), F32)
    small_g = pack(d_wa, d_wx, d_cw, [jnp.zeros((1, d), F32), d_cb, d_ba, d_bx, d_lam, d_gmlp, d_gf], "pack_small_g")
    g_win = g_win_part("g_w_in_qkv", dproj_a, 0, None)
    g_win = g_win_part("g_w_in_rest", dproj_b, 4 * N_QKV, g_win)
    red_in = reduce_begin([0], [g_win], "w_in", everywhere=small_g)
    dxn = _dxn(dproj_a, dproj_b, w_in_g, 2 * tn, [red_in[0][3]])
    grad_x, _, d_gmix = _rms_bwd(xs, norm_mix_g, dxn, dh1, "norm_mix_bwd")

    _, done = reduce_end([4, 5], red_mlp, [grad_x], "mlp")
    _, done = reduce_end([1, 2, 3], red_proj, [done], "proj")
    small_w = pack(wa, wx, zero_cw, [norm_mix_g, conv_b, lru_ba, lru_bx, lru_lambda, norm_mlp_g, norm_final_g],
                   "pack_small_w")
    small_m = pack(m_lru_wa[0], m_lru_wx[0], zero_cw,
                   [m_norm_mix_g, m_conv_b, m_lru_ba, m_lru_bx, m_lru_lambda, m_norm_mlp_g, m_norm_final_g],
                   "pack_small_m")
    small_v = pack(v_lru_wa[0], v_lru_wx[0], zero_cw,
                   [v_norm_mix_g, v_conv_b, v_lru_ba, v_lru_bx, v_lru_lambda, v_norm_mlp_g, v_norm_final_g],
                   "pack_small_v")
    small_sum, _ = reduce_end([0], red_in, [done, small_w, small_m, small_v], "w_in")
    small = _adamw(small_w, small_sum, small_m, small_v, "adamw_small")
    g_cw = lax.dynamic_slice(small_sum[2 * mat_rows:2 * mat_rows + CONV_TAPS], (0, chip * u), (CONV_TAPS, u))
    cw_out = _adamw(conv_w[0], g_cw, m_conv_w[0], v_conv_w[0], "adamw_conv_w")
    gmix_parts = _all_gather8(jnp.pad(d_gmix, ((0, 7), (0, 0))), "gather_gain_grad", [small[1]])
    gmix_out = _sum8_adamw_row(gmix_parts, norm_mix_g, m_norm_mix_g, v_norm_mix_g, "sum_adamw_norm_mix_g")

    def small_leaf(kind, name):
        a = small[kind]
        if name == "norm_mix_g":
            return gmix_out[kind]
        if name == "lru_wa":
            return a[0:mat_rows].reshape(lru_wa.shape)
        if name == "lru_wx":
            return a[mat_rows:2 * mat_rows].reshape(lru_wx.shape)
        if name == "conv_w":
            return cw_out[kind][None]
        row = a[2 * mat_rows + CONV_TAPS + vec_names.index(name)]
        return row if name == "norm_final_g" else row[None]

    order = ["norm_mix_g", "w_in", "conv_w", "conv_b", "lru_wa", "lru_ba", "lru_wx", "lru_bx", "lru_lambda",
             "w_proj_attn", "w_proj_lru", "w_out", "norm_mlp_g", "w_up", "w_down", "norm_final_g"]
    outs = [loss, grad_x[None]]
    for kind in range(4):
        for name in order:
            outs.append(big_out[name][kind] if name in big_out else small_leaf(kind, name))
    return tuple(outs)
```

```python
import functools

import jax
import jax.numpy as jnp
from jax import lax
from jax.experimental import pallas as pl
from jax.experimental.pallas import tpu as pltpu

F32 = jnp.float32
BF16 = jnp.bfloat16
MESH = pl.DeviceIdType.MESH
AXES = ("x", "y", "c")

N_CHIPS = 4
N_DEV = 8
HEAD_DIM = 128
ATTN_BLK = 128
DILATIONS = (1, 4, 16)
ATTN_UNROLL = 8
CONV_TAPS = 4
LRU_C = 8.0
EPS = 1e-6
N_SLOTS = 7
N_QKV = 3
VMEM_MIB = 2 ** 20
VMEM_V7X = 64 * VMEM_MIB
STREAM_TILE = 4 * VMEM_MIB

ADAM_LR = 0.001
ADAM_B1 = 0.9
ADAM_B2 = 0.999
ADAM_EPS = 1e-08
ADAM_WD = 0.01
ADAM_STEP = 10

NN = (((1,), (0,)), ((), ()))
NT = (((1,), (1,)), ((), ()))
TN = (((0,), (0,)), ((), ()))


def _params(vmem_mib=None, **kw):
    limit = None if vmem_mib is None else min(vmem_mib * VMEM_MIB, VMEM_V7X - 8 * VMEM_MIB)
    return pltpu.CompilerParams(vmem_limit_bytes=limit, **kw)


def _row_tile(rows, row_bytes, budget=VMEM_MIB):
    t = rows
    while t % 16 == 0 and t * row_bytes > budget:
        t //= 2
    return t


def _dot(a, b, dims):
    return lax.dot_general(a.astype(BF16), b.astype(BF16), dims, preferred_element_type=F32)


def _sigmoid(x):
    return jax.nn.sigmoid(x)


def _rms_fwd(x, g, name):
    s, d = x.shape
    tm = _row_tile(s, d * 4)

    def body(x_ref, g_ref, o_ref):
        xf = x_ref[...]
        r = lax.rsqrt(jnp.mean(xf * xf, axis=-1, keepdims=True) + EPS)
        o_ref[...] = (xf * r * g_ref[...]).astype(o_ref.dtype)

    return pl.pallas_call(
        body, name=name, grid=(s // tm,),
        in_specs=[pl.BlockSpec((tm, d), lambda i: (i, 0)), pl.BlockSpec((1, d), lambda i: (0, 0))],
        out_specs=pl.BlockSpec((tm, d), lambda i: (i, 0)),
        out_shape=jax.ShapeDtypeStruct((s, d), BF16), compiler_params=_params(32),
    )(x, g)


def _rms_bwd(x, g, dy, resid, name):
    s, d = x.shape
    tm = _row_tile(s, d * 4)

    def body(x_ref, g_ref, dy_ref, res_ref, dx_ref, dxb_ref, dg_ref):
        xf = x_ref[...]
        r = lax.rsqrt(jnp.mean(xf * xf, axis=-1, keepdims=True) + EPS)
        xh = xf * r
        dyv = dy_ref[...]
        dxh = dyv * g_ref[...]
        dx = r * (dxh - xh * jnp.mean(dxh * xh, axis=-1, keepdims=True)) + res_ref[...]
        dx_ref[...] = dx
        dxb_ref[...] = dx.astype(BF16)
        part = jnp.sum(dyv * xh, axis=0, keepdims=True)

        @pl.when(pl.program_id(0) == 0)
        def _():
            dg_ref[...] = part

        @pl.when(pl.program_id(0) > 0)
        def _():
            dg_ref[...] += part

    row = pl.BlockSpec((tm, d), lambda i: (i, 0))
    vec = pl.BlockSpec((1, d), lambda i: (0, 0))
    return pl.pallas_call(
        body, name=name, grid=(s // tm,),
        in_specs=[row, vec, row, row], out_specs=[row, row, vec],
        out_shape=[jax.ShapeDtypeStruct((s, d), F32), jax.ShapeDtypeStruct((s, d), BF16),
                   jax.ShapeDtypeStruct((1, d), F32)],
        compiler_params=_params(32),
    )(x, g, dy, resid)


def _loss_head(h2, g, target):
    s, d = h2.shape
    tm = _row_tile(s, d * 4)

    def body(x_ref, g_ref, t_ref, loss_ref, dx_ref, dxb_ref, dg_ref):
        xf = x_ref[...]
        gv = g_ref[...]
        r = lax.rsqrt(jnp.mean(xf * xf, axis=-1, keepdims=True) + EPS)
        xh = xf * r
        err = xh * gv - t_ref[...]
        part = jnp.sum(jnp.sum(err * err, axis=1, keepdims=True), axis=0, keepdims=True) * (0.5 / d)
        dyv = err * (1.0 / d)
        dxh = dyv * gv
        dx = r * (dxh - xh * jnp.mean(dxh * xh, axis=-1, keepdims=True))
        dx_ref[...] = dx
        dxb_ref[...] = dx.astype(BF16)
        dgp = jnp.sum(dyv * xh, axis=0, keepdims=True)

        @pl.when(pl.program_id(0) == 0)
        def _():
            dg_ref[...] = dgp
            loss_ref[...] = jnp.broadcast_to(part, loss_ref.shape)

        @pl.when(pl.program_id(0) > 0)
        def _():
            dg_ref[...] += dgp
            loss_ref[...] += jnp.broadcast_to(part, loss_ref.shape)

    row = pl.BlockSpec((tm, d), lambda i: (i, 0))
    vec = pl.BlockSpec((1, d), lambda i: (0, 0))
    return pl.pallas_call(
        body, name="loss_head", grid=(s // tm,),
        in_specs=[row, vec, row],
        out_specs=[pl.BlockSpec((8, 128), lambda i: (0, 0)), row, row, vec],
        out_shape=[jax.ShapeDtypeStruct((8, 128), F32), jax.ShapeDtypeStruct((s, d), F32),
                   jax.ShapeDtypeStruct((s, d), BF16), jax.ShapeDtypeStruct((1, d), F32)],
        compiler_params=_params(32),
    )(h2, g, target)


def _mm(name, operands, in_specs, out_shape, out_specs, grid, dims, epilogue, nk=1, acc_shape=None,
        vmem_mib=56, aliases=None):
    n_in = len(operands)
    n_out = len(out_shape)

    def body(*refs):
        a_ref, b_ref = refs[0], refs[1]
        extras = refs[2:n_in]
        outs = refs[n_in:n_in + n_out]

        def prod():
            return _dot(a_ref[...], b_ref[...], dims)

        if nk == 1:
            epilogue(prod(), extras, outs)
        else:
            acc = refs[n_in + n_out]
            k = pl.program_id(2)

            @pl.when(k == 0)
            def _():
                acc[...] = prod()

            @pl.when(k > 0)
            def _():
                acc[...] += prod()

            @pl.when(k == nk - 1)
            def _():
                epilogue(acc[...], extras, outs)

    scratch = [] if nk == 1 else [pltpu.VMEM(acc_shape, F32)]
    return pl.pallas_call(
        body, name=name, grid=grid, in_specs=in_specs, out_specs=out_specs, out_shape=out_shape,
        scratch_shapes=scratch, input_output_aliases=aliases or {},
        compiler_params=_params(vmem_mib),
    )(*operands)


def _store(acc, extras, outs):
    outs[0][...] = acc.astype(outs[0].dtype)


def _proj_in_shard(xn, w_in_g, shard_arr, prev, name):
    s, d = xn.shape
    u = d // 4
    per = N_SLOTS
    n_prev = 0 if prev is None else 1

    def body(sh_ref, x_ref, w_ref, *rest):
        del sh_ref
        rest[n_prev][...] = _dot(x_ref[...], w_ref[...], NN)

    def out_map(j, sh_ref):
        unit = per * sh_ref[0] + j
        return (unit // 4, 0, unit % 4)

    return pl.pallas_call(
        body, name=name,
        grid_spec=pltpu.PrefetchScalarGridSpec(
            num_scalar_prefetch=1, grid=(per,),
            in_specs=[pl.BlockSpec((s, d), lambda j, sh_ref: (0, 0)),
                      pl.BlockSpec((None, d, u), lambda j, sh_ref: (sh_ref[0], 0, j))] + [HBM] * n_prev,
            out_specs=pl.BlockSpec((None, s, u), out_map)),
        out_shape=jax.ShapeDtypeStruct((N_SLOTS, s, d), F32),
        input_output_aliases={3: 0} if n_prev else {},
        compiler_params=_params(56),
    )(shard_arr, xn, w_in_g, *([] if prev is None else [prev]))


def _mm_nn(name, a, b, extras, extra_specs, out_shape, out_specs, epilogue, tn, aliases=None):
    s, kdim = a.shape
    n = b.shape[1]
    return _mm(
        name, [a, b] + list(extras),
        [pl.BlockSpec((s, kdim), lambda i, j, k: (0, 0)), pl.BlockSpec((kdim, tn), lambda i, j, k: (0, j))]
        + list(extra_specs),
        out_shape, out_specs, (1, n // tn, 1), NN, epilogue, aliases=aliases)


def _mm_nt(name, a, b, extras, extra_specs, out_shape, out_specs, epilogue, tn, aliases=None):
    s, kdim = a.shape
    n = b.shape[0]
    return _mm(
        name, [a, b] + list(extras),
        [pl.BlockSpec((s, kdim), lambda i, j, k: (0, 0)), pl.BlockSpec((tn, kdim), lambda i, j, k: (j, 0))]
        + list(extra_specs),
        out_shape, out_specs, (1, n // tn, 1), NT, epilogue, aliases=aliases)


def _mm_tn(name, a, b, a_spec, b_spec, out_shape, out_spec, grid, m, tn, s, aliases=None, extra=None):
    ch = 256
    n_in = 2 if extra is None else 3

    def body(*refs):
        a_ref, b_ref = refs[0], refs[1]
        o_ref, at_ref = refs[n_in], refs[n_in + 1]

        @pl.when(pl.program_id(1) == 0)
        def _():
            for c0 in range(0, s, ch):
                at_ref[:, c0:c0 + ch] = a_ref[c0:c0 + ch, :].astype(F32).T.astype(BF16)

        o_ref[...] = _dot(at_ref[...], b_ref[...], NN).astype(o_ref.dtype)

    operands = [a, b] + ([] if extra is None else [extra])
    in_specs = [a_spec, b_spec] + ([] if extra is None else [pl.BlockSpec(memory_space=pl.ANY)])
    return pl.pallas_call(
        body, name=name, grid=grid, in_specs=in_specs, out_specs=out_spec, out_shape=out_shape,
        scratch_shapes=[pltpu.VMEM((m, s), BF16)], input_output_aliases=aliases or {},
        compiler_params=_params(56),
    )(*operands)


def _dxn(dproj_a, dproj_b, w_in_g, tn, after):
    n_a, s, d = dproj_a.shape
    u = d // 4
    ua = 4 * n_a
    nk = 4 * N_SLOTS
    per = N_SLOTS

    def body(a_ref, b_ref, w_ref, *rest):
        o_ref = rest[len(after)]
        k = pl.program_id(2)

        @pl.when(k == 0)
        def _():
            o_ref[...] = jnp.zeros_like(o_ref)

        @pl.when(k < ua)
        def _():
            o_ref[...] += _dot(a_ref[...], w_ref[...], NT)

        @pl.when(k >= ua)
        def _():
            o_ref[...] += _dot(b_ref[...], w_ref[...], NT)

    def a_map(i, j, k):
        kk = jnp.minimum(k, ua - 1)
        return (kk // 4, 0, kk % 4)

    def b_map(i, j, k):
        kk = jnp.maximum(k - ua, 0)
        return (kk // 4, 0, kk % 4)

    return pl.pallas_call(
        body, name="dxn", grid=(1, d // tn, nk),
        in_specs=[pl.BlockSpec((None, s, u), a_map), pl.BlockSpec((None, s, u), b_map),
                  pl.BlockSpec((None, tn, u), lambda i, j, k: (k // per, j, k % per))] + [HBM] * len(after),
        out_specs=pl.BlockSpec((s, tn), lambda i, j, k: (0, j)),
        out_shape=jax.ShapeDtypeStruct((s, d), F32),
        compiler_params=_params(48),
    )(dproj_a, dproj_b, w_in_g, *after)


def _attn_masks(slope, dil):
    ii = lax.broadcasted_iota(jnp.int32, (ATTN_BLK, 2 * ATTN_BLK), 0)
    jj = lax.broadcasted_iota(jnp.int32, (ATTN_BLK, 2 * ATTN_BLK), 1)
    diff = ATTN_BLK + ii - jj
    band = (diff >= 0) & (diff <= ATTN_BLK)
    bias = -(slope * float(dil)) * diff.astype(F32)
    return band, bias, jj


def _attn_window(t, nblk):
    cur = pl.ds(pl.multiple_of(t * ATTN_BLK, ATTN_BLK), ATTN_BLK)
    prev = pl.ds(pl.multiple_of(jnp.maximum(t - 1, 0) * ATTN_BLK, ATTN_BLK), ATTN_BLK)
    first = jnp.where(t % nblk == 0, ATTN_BLK, 0)
    return prev, cur, first


def _unrolled_loop(n, step, init):
    def trip(i, carry):
        for k in range(ATTN_UNROLL):
            carry = step(i * ATTN_UNROLL + k, carry)
        return carry

    return lax.fori_loop(0, n // ATTN_UNROLL, trip, init)


def _streams(pairs, dil, s):
    if dil == 1:
        return [src for _, src in pairs]
    seg = s // dil
    for dst, src in pairs:
        for r in range(dil):
            dst[r * seg:(r + 1) * seg, :] = src[pl.ds(r, seg, stride=dil), :].astype(dst.dtype)
    return [dst for dst, _ in pairs]


def _attn_fwd(proj_a, slopes):
    _, s, d = proj_a.shape
    heads = d // HEAD_DIM
    scale = HEAD_DIM ** -0.5
    n_t = s // ATTN_BLK
    ng = len(DILATIONS)

    def body(q_ref, k_ref, v_ref, sl_ref, o_ref, lse_ref, qd, kd, vd, od, ld, og, lg):
        slope = sl_ref[...][:, :1]
        for g, dil in enumerate(DILATIONS):
            nblk = s // dil // ATTN_BLK
            qs, ks, vs = _streams([(qd, q_ref), (kd, k_ref), (vd, v_ref)], dil, s)
            o_t, l_t = (og.at[g], lg.at[g]) if dil == 1 else (od, ld)
            band, bias, jj = _attn_masks(slope, dil)

            def blk(t, carry, nblk=nblk, band=band, bias=bias, jj=jj, qs=qs, ks=ks, vs=vs, o_t=o_t, l_t=l_t):
                prev, cur, first = _attn_window(t, nblk)
                kk = jnp.concatenate([ks[prev, :], ks[cur, :]], axis=0)
                vv = jnp.concatenate([vs[prev, :], vs[cur, :]], axis=0)
                sc = _dot(qs[cur, :], kk, NT) * scale + bias
                sc = jnp.where(band & (jj >= first), sc, -jnp.inf)
                m = jnp.max(sc, axis=1, keepdims=True)
                p = jnp.exp(sc - m)
                l = jnp.sum(p, axis=1, keepdims=True)
                o_t[cur, :] = _dot(p, vv, NN) / l
                l_t[cur, :] = jnp.broadcast_to(m + jnp.log(l), (ATTN_BLK, HEAD_DIM))
                return carry

            _unrolled_loop(n_t, blk, 0)
            seg = s // dil
            if dil > 1:
                for r in range(dil):
                    og[g, pl.ds(r, seg, stride=dil), :] = od[r * seg:(r + 1) * seg, :]
                    lg[g, pl.ds(r, seg, stride=dil), :] = ld[r * seg:(r + 1) * seg, :]

        ch = 256

        def combine(c, carry):
            rows = pl.ds(pl.multiple_of(c * ch, ch), ch)
            ls = [lg[g, rows, :] for g in range(ng)]
            mx = functools.reduce(jnp.maximum, ls)
            es = [jnp.exp(x - mx) for x in ls]
            den = functools.reduce(jnp.add, es)
            num = functools.reduce(jnp.add, [es[g] * og[g, rows, :] for g in range(ng)])
            o_ref[rows, :] = (num / den).astype(o_ref.dtype)
            lse_ref[rows, :] = mx + jnp.log(den)
            return carry

        lax.fori_loop(0, s // ch, combine, 0)

    def col(slot):
        return pl.BlockSpec((None, s, HEAD_DIM), lambda h: (slot, 0, h))

    head = pl.BlockSpec((s, HEAD_DIM), lambda h: (0, h))
    return pl.pallas_call(
        body, name="attn_fwd", grid=(heads,),
        in_specs=[col(0), col(1), col(2), pl.BlockSpec((None, 1, HEAD_DIM), lambda h: (h, 0, 0))],
        out_specs=[head, head],
        out_shape=[jax.ShapeDtypeStruct((s, d), BF16), jax.ShapeDtypeStruct((s, d), F32)],
        scratch_shapes=[pltpu.VMEM((s, HEAD_DIM), BF16)] * 3 + [pltpu.VMEM((s, HEAD_DIM), F32)] * 2
        + [pltpu.VMEM((ng, s, HEAD_DIM), F32)] * 2,
        compiler_params=_params(40),
    )(proj_a, proj_a, proj_a, slopes)


def _attn_bwd(proj_a, slopes, y_attn, lse, dy):
    _, s, d = proj_a.shape
    heads = d // HEAD_DIM
    scale = HEAD_DIM ** -0.5
    n_t = s // ATTN_BLK

    def body(q_ref, k_ref, v_ref, sl_ref, o_ref, lse_ref, dy_ref, out_ref,
             qd, kd, vd, dod, lsd, dld, delta, dqd, dkd, dvd, dqa, dka, dva):
        slope = sl_ref[...][:, :1]
        dyv = dy_ref[...]
        delta[...] = jnp.broadcast_to(
            jnp.sum(dyv * o_ref[...].astype(F32), axis=1, keepdims=True), (s, HEAD_DIM))
        for g, dil in enumerate(DILATIONS):
            nblk = s // dil // ATTN_BLK
            seg = s // dil
            qs, ks, vs, dos, lss, dls = _streams(
                [(qd, q_ref), (kd, k_ref), (vd, v_ref), (dod, dy_ref), (lsd, lse_ref), (dld, delta)], dil, s)
            dq_t, dk_t, dv_t = (dqa, dka, dva) if dil == 1 else (dqd, dkd, dvd)
            band, bias, jj = _attn_masks(slope, dil)

            def blk(t, carry, nblk=nblk, band=band, bias=bias, jj=jj, qs=qs, ks=ks, vs=vs, dos=dos, lss=lss,
                    dls=dls, dq_t=dq_t, dk_t=dk_t, dv_t=dv_t):
                ck, cv = carry
                prev, cur, first = _attn_window(t, nblk)
                q = qs[cur, :]
                do = dos[cur, :]
                lse_b = lss[cur, :]
                dl_b = dls[cur, :]
                kk = jnp.concatenate([ks[prev, :], ks[cur, :]], axis=0)
                vv = jnp.concatenate([vs[prev, :], vs[cur, :]], axis=0)
                sc = _dot(q, kk, NT) * scale + bias
                p = jnp.where(band & (jj >= first), jnp.exp(sc - jnp.concatenate([lse_b, lse_b], axis=1)), 0.0)
                dp = _dot(do, vv, NT)
                ds = p * (dp - jnp.concatenate([dl_b, dl_b], axis=1))
                dv_b = _dot(p, do, TN)
                dk_b = _dot(ds, q, TN) * scale
                dq_t[cur, :] = _dot(ds, kk, NN) * scale
                done = pl.ds(pl.multiple_of(jnp.where(t == 0, n_t, t - 1) * ATTN_BLK, ATTN_BLK), ATTN_BLK)
                dk_t[done, :] = ck + dk_b[:ATTN_BLK]
                dv_t[done, :] = cv + dv_b[:ATTN_BLK]
                return dk_b[ATTN_BLK:], dv_b[ATTN_BLK:]

            zero = jnp.zeros((ATTN_BLK, HEAD_DIM), F32)
            ck, cv = _unrolled_loop(n_t, blk, (zero, zero))
            dk_t[(n_t - 1) * ATTN_BLK:n_t * ATTN_BLK, :] = ck
            dv_t[(n_t - 1) * ATTN_BLK:n_t * ATTN_BLK, :] = cv
            if dil > 1:
                for acc, part in ((dqa, dqd), (dka, dkd), (dva, dvd)):
                    for r in range(dil):
                        acc[pl.ds(r, seg, stride=dil), :] += part[r * seg:(r + 1) * seg, :]
        out_ref[0] = dqa[...].astype(out_ref.dtype)
        out_ref[1] = dka[0:s, :].astype(out_ref.dtype)
        out_ref[2] = dva[0:s, :].astype(out_ref.dtype)

    def col(slot):
        return pl.BlockSpec((None, s, HEAD_DIM), lambda h: (slot, 0, h))

    head = pl.BlockSpec((s, HEAD_DIM), lambda h: (0, h))
    return pl.pallas_call(
        body, name="attn_bwd", grid=(heads,),
        in_specs=[col(0), col(1), col(2), pl.BlockSpec((None, 1, HEAD_DIM), lambda h: (h, 0, 0)),
                  head, head, head],
        out_specs=pl.BlockSpec((N_QKV, s, HEAD_DIM), lambda h: (0, 0, h)),
        out_shape=jax.ShapeDtypeStruct((N_QKV, s, d), BF16),
        scratch_shapes=[pltpu.VMEM((s, HEAD_DIM), BF16)] * 4 + [pltpu.VMEM((s, HEAD_DIM), F32)] * 4
        + [pltpu.VMEM((s + ATTN_BLK, HEAD_DIM), F32)] * 2 + [pltpu.VMEM((s, HEAD_DIM), F32)]
        + [pltpu.VMEM((s + ATTN_BLK, HEAD_DIM), F32)] * 2,
        compiler_params=_params(48),
    )(proj_a, proj_a, proj_a, slopes, y_attn, lse, dy)


def _expm1(x):
    small = x * (1.0 + x * (0.5 + x * (1.0 / 6.0 + x * (1.0 / 24.0 + x * (1.0 / 120.0)))))
    return jnp.where(jnp.abs(x) < 0.1, small, jnp.exp(x) - 1.0)


def _softplus(x):
    return jnp.maximum(x, 0.0) + jnp.log1p(jnp.exp(-jnp.abs(x)))


GELU_K = 0.7978845608028654
GELU_C = 0.044715


def _gelu(x):
    t = jnp.tanh(GELU_K * (x + GELU_C * x * x * x))
    return 0.5 * x * (1.0 + t), t


def _gelu_grad(x, t):
    return 0.5 * (1.0 + t) + 0.5 * x * (1.0 - t * t) * GELU_K * (1.0 + 3.0 * GELU_C * x * x)


def _lru_gates(xc, wa, ba, wx, bx, sp):
    r = _sigmoid(_dot(xc, wa, NN) + ba)
    ig = _sigmoid(_dot(xc, wx, NN) + bx)
    log_a = -LRU_C * r * sp
    a = jnp.exp(log_a)
    mult = jnp.sqrt(-_expm1(2.0 * log_a))
    return r, ig, a, mult


def _scan_fwd(a, u, tt):
    row = lax.broadcasted_iota(jnp.int32, a.shape, 0)
    sh = 1
    while sh < tt:
        keep = row >= sh
        a_s = jnp.where(keep, pltpu.roll(a, sh, 0), 1.0)
        u_s = jnp.where(keep, pltpu.roll(u, sh, 0), 0.0)
        u = a * u_s + u
        a = a * a_s
        sh *= 2
    return a, u


def _scan_bwd(b, g, tt):
    row = lax.broadcasted_iota(jnp.int32, b.shape, 0)
    sh = 1
    while sh < tt:
        keep = row < tt - sh
        b_s = jnp.where(keep, pltpu.roll(b, tt - sh, 0), 1.0)
        g_s = jnp.where(keep, pltpu.roll(g, tt - sh, 0), 0.0)
        g = g + b * g_s
        b = b * b_s
        sh *= 2
    return b, g


def _conv_rows(xpad_ref, cw, cb, s):
    acc = cb
    for j in range(CONV_TAPS):
        off = 8 - (CONV_TAPS - 1) + j
        acc = acc + cw[j:j + 1, :] * xpad_ref[off:off + s, :]
    return acc


LRU_TILE = 128


def _lru_specs(s, d):
    heads = d // HEAD_DIM

    def col(slot):
        return pl.BlockSpec((None, s, HEAD_DIM), lambda h: (slot, 0, h))

    vec = pl.BlockSpec((1, HEAD_DIM), lambda h: (0, h))
    mat = pl.BlockSpec((None, HEAD_DIM, HEAD_DIM), lambda h: (h, 0, 0))
    cw = pl.BlockSpec((8, HEAD_DIM), lambda h: (0, h))
    head = pl.BlockSpec((s, HEAD_DIM), lambda h: (0, h))
    return heads, col, vec, mat, cw, head


def _lru_fwd(proj_b, conv_w, conv_b, wa, ba, wx, bx, lam):
    _, s, d = proj_b.shape
    heads, col, vec, mat, cws, head = _lru_specs(s, d)
    tt = LRU_TILE

    def body(xr_ref, xg_ref, cw_ref, cb_ref, wa_ref, ba_ref, wx_ref, bx_ref, lam_ref, y_ref, h_ref, xpad, xc_s):
        xpad[0:8, :] = jnp.zeros((8, HEAD_DIM), F32)
        xpad[8:8 + s, :] = xr_ref[...]
        xc_s[...] = _conv_rows(xpad, cw_ref[...], cb_ref[...], s)
        sp = _softplus(-lam_ref[...])
        wav, wxv, bav, bxv = wa_ref[...], wx_ref[...], ba_ref[...], bx_ref[...]

        def tile(i, hc):
            rows = pl.ds(pl.multiple_of(i * tt, tt), tt)
            xc = xc_s[rows, :]
            _, ig, a, mult = _lru_gates(xc, wav, bav, wxv, bxv, sp)
            pa, hl = _scan_fwd(a, mult * (ig * xc), tt)
            h = hl + pa * hc
            h_ref[rows, :] = h
            gel, _ = _gelu(xg_ref[rows, :])
            y_ref[rows, :] = (h * gel).astype(y_ref.dtype)
            return h[tt - 1:tt, :]

        lax.fori_loop(0, s // tt, tile, jnp.zeros((1, HEAD_DIM), F32))

    return pl.pallas_call(
        body, name="lru_fwd", grid=(heads,),
        in_specs=[col(N_QKV), col(N_QKV + 1), cws, vec, mat, vec, mat, vec, vec],
        out_specs=[head, head],
        out_shape=[jax.ShapeDtypeStruct((s, d), BF16), jax.ShapeDtypeStruct((s, d), F32)],
        scratch_shapes=[pltpu.VMEM((s + 8, HEAD_DIM), F32), pltpu.VMEM((s, HEAD_DIM), F32)],
        compiler_params=_params(32),
    )(proj_b, proj_b, conv_w, conv_b, wa, ba, wx, bx, lam)


def _lru_bwd(proj_b, h_lru, dy, conv_w, conv_b, wa, ba, wx, bx, lam, dproj_b):
    _, s, d = proj_b.shape
    heads, col, vec, mat, cws, head = _lru_specs(s, d)
    tt = LRU_TILE
    n_t = s // tt

    def body(xr_ref, xg_ref, h_ref, dy_ref, cw_ref, cb_ref, wa_ref, ba_ref, wx_ref, bx_ref, lam_ref, alias_ref,
             out_ref, dcw_ref, dcb_ref, dwa_ref, dba_ref, dwx_ref, dbx_ref, dlam_ref, xpad, xc_s, dxc_s):
        del alias_ref
        xpad[0:8, :] = jnp.zeros((8, HEAD_DIM), F32)
        xpad[8:8 + s, :] = xr_ref[...]
        cwv = cw_ref[...]
        xc_s[...] = _conv_rows(xpad, cwv, cb_ref[...], s)
        dxc_s[s:s + 8, :] = jnp.zeros((8, HEAD_DIM), F32)
        lamv = lam_ref[...]
        sp = _softplus(-lamv)
        wav, wxv, bav, bxv = wa_ref[...], wx_ref[...], ba_ref[...], bx_ref[...]
        dwa_ref[...] = jnp.zeros_like(dwa_ref)
        dwx_ref[...] = jnp.zeros_like(dwx_ref)
        zero = jnp.zeros((1, HEAD_DIM), F32)
        row = lax.broadcasted_iota(jnp.int32, (tt, HEAD_DIM), 0)

        def tile(it, carry):
            dh_next, a_next, dba, dbx, dsp, dcb = carry
            i = n_t - 1 - it
            t0 = pl.multiple_of(i * tt, tt)
            rows = pl.ds(t0, tt)
            xc = xc_s[rows, :]
            r, ig, a, mult = _lru_gates(xc, wav, bav, wxv, bxv, sp)
            h = h_ref[rows, :]
            before = h_ref[pl.ds(pl.multiple_of(jnp.maximum(t0 - 8, 0), 8), 8), :][7:8, :]
            before = before * (i > 0).astype(F32)
            h_prev = jnp.where(row == 0, before, pltpu.roll(h, 1, 0))
            xg = xg_ref[rows, :]
            dyv = dy_ref[rows, :]
            gel, th = _gelu(xg)
            out_ref[1, rows, :] = (dyv * h * _gelu_grad(xg, th)).astype(out_ref.dtype)
            b = jnp.where(row == tt - 1, a_next, pltpu.roll(a, tt - 1, 0))
            pb, z = _scan_bwd(b, dyv * gel, tt)
            dh = z + pb * dh_next
            da = dh * h_prev
            dmult = dh * (ig * xc)
            dig = dh * (mult * xc)
            dla = da * a - dmult * (a * a / mult)
            dzr = dla * (-LRU_C * sp) * (r * (1.0 - r))
            dzx = dig * (ig * (1.0 - ig))
            dxc = dh * (mult * ig) + _dot(dzr, wav, NT) + _dot(dzx, wxv, NT)
            dxc_s[rows, :] = dxc
            dwa_ref[...] += _dot(xc, dzr, TN)
            dwx_ref[...] += _dot(xc, dzx, TN)
            return (dh[0:1, :], a[0:1, :],
                    dba + jnp.sum(dzr, axis=0, keepdims=True),
                    dbx + jnp.sum(dzx, axis=0, keepdims=True),
                    dsp + jnp.sum(dla * (-LRU_C * r), axis=0, keepdims=True),
                    dcb + jnp.sum(dxc, axis=0, keepdims=True))

        _, _, dba, dbx, dsp, dcb = lax.fori_loop(0, n_t, tile, (zero, zero, zero, zero, zero, zero))
        dba_ref[...] = dba
        dbx_ref[...] = dbx
        dcb_ref[...] = dcb
        dlam_ref[...] = -dsp * _sigmoid(-lamv)
        dxr = jnp.zeros((s, HEAD_DIM), F32)
        for j in range(CONV_TAPS):
            back = CONV_TAPS - 1 - j
            off = 8 - back
            dcw_ref[j:j + 1, :] = jnp.sum(dxc_s[0:s, :] * xpad[off:off + s, :], axis=0, keepdims=True)
            dxr = dxr + cwv[j:j + 1, :] * dxc_s[back:back + s, :]
        out_ref[0] = dxr.astype(out_ref.dtype)

    return pl.pallas_call(
        body, name="lru_bwd", grid=(heads,),
        in_specs=[col(N_QKV), col(N_QKV + 1), head, head, cws, vec, mat, vec, mat, vec, vec,
                  pl.BlockSpec(memory_space=pl.ANY)],
        out_specs=[pl.BlockSpec((2, s, HEAD_DIM), lambda h: (0, 0, h)),
                   pl.BlockSpec((CONV_TAPS, HEAD_DIM), lambda h: (0, h)), vec, mat, vec, mat, vec, vec],
        out_shape=[jax.ShapeDtypeStruct(dproj_b.shape, dproj_b.dtype),
                   jax.ShapeDtypeStruct((CONV_TAPS, d), F32), jax.ShapeDtypeStruct((1, d), F32),
                   jax.ShapeDtypeStruct(wa.shape, F32), jax.ShapeDtypeStruct((1, d), F32),
                   jax.ShapeDtypeStruct(wx.shape, F32), jax.ShapeDtypeStruct((1, d), F32),
                   jax.ShapeDtypeStruct((1, d), F32)],
        scratch_shapes=[pltpu.VMEM((s + 8, HEAD_DIM), F32), pltpu.VMEM((s, HEAD_DIM), F32),
                        pltpu.VMEM((s + 8, HEAD_DIM), F32)],
        input_output_aliases={11: 0},
        compiler_params=_params(32),
    )(proj_b, proj_b, h_lru, dy, conv_w, conv_b, wa, ba, wx, bx, lam, dproj_b)


def _place():
    x, y, c = (lax.axis_index(n) for n in AXES)
    return x, y, c


def _other_chips(x, y):
    return [(1 - x, y), (x, 1 - y), (1 - x, 1 - y)]


HBM = pl.BlockSpec(memory_space=pl.ANY)


def _cast_shard(w, chip_arr, name):
    r, cols = w.shape
    rh = r // 2
    tr = _row_tile(rh, cols * 4, STREAM_TILE)
    nt = rh // tr

    def body(chip_ref, w_ref, o_ref):
        del chip_ref
        o_ref[...] = w_ref[...].astype(BF16)

    return pl.pallas_call(
        body, name=name,
        grid_spec=pltpu.PrefetchScalarGridSpec(
            num_scalar_prefetch=1, grid=(2, nt),
            in_specs=[pl.BlockSpec((tr, cols), lambda h, i, chip_ref: (h * nt + i, 0))],
            out_specs=pl.BlockSpec((None, None, tr, cols), lambda h, i, chip_ref: (chip_ref[0], h, i, 0))),
        out_shape=jax.ShapeDtypeStruct((N_CHIPS, 2, rh, cols), BF16), compiler_params=_params(32),
    )(chip_arr, w)


HBM_SPEC = pl.BlockSpec(memory_space=pltpu.HBM)
SEM_SPEC = pl.BlockSpec(memory_space=pltpu.SEMAPHORE)
EFFECT = pltpu.SideEffectType.DATAFLOW_SIDE_EFFECTING
TOKEN = jax.ShapeDtypeStruct((8, 128), F32)
TOKEN_SPEC = pl.BlockSpec(memory_space=pltpu.VMEM)


def _in_hbm(arrays):
    return [pltpu.with_memory_space_constraint(a, pltpu.HBM) for a in arrays]


def _hbm_like(arrays):
    return [pltpu.HBM(a.shape, a.dtype) for a in arrays]


def _sems(n):
    return pltpu.SemaphoreType.DMA((n,))


def _remote(src, dst, send_sem, recv_sem, to):
    return pltpu.make_async_remote_copy(src_ref=src, dst_ref=dst, send_sem=send_sem, recv_sem=recv_sem,
                                        device_id=to, device_id_type=MESH)


ALL_FLIPS = (0, 1, 2)


def _gather_start(bufs, groups, after, name):
    n = len(bufs)
    ng = len(groups)

    def body(*refs):
        ins = refs[:n]
        sems = refs[n + len(after):n + len(after) + 2 * ng]
        x, y, c = _place()
        me = 2 * x + y
        chips = _other_chips(x, y)
        for g, (ws, flips) in enumerate(groups):
            for i, w in enumerate(ws):
                for jj, j in enumerate(flips):
                    k = len(flips) * i + jj
                    mine = ins[w].at[me, c]
                    _remote(mine, mine, sems[2 * g].at[k], sems[2 * g + 1].at[k], (*chips[j], c)).start()

    sem_shapes = []
    for ws, flips in groups:
        sem_shapes += [_sems(len(flips) * len(ws))] * 2
    res = pl.pallas_call(
        body, name=name, in_specs=[HBM_SPEC] * n + [HBM] * len(after),
        out_specs=[SEM_SPEC] * (2 * ng) + [HBM_SPEC] * n, out_shape=sem_shapes + _hbm_like(bufs),
        input_output_aliases={w: 2 * ng + w for w in range(n)},
        compiler_params=pltpu.CompilerParams(has_side_effects=EFFECT),
    )(*_in_hbm(bufs), *after)
    return [(res[2 * g], res[2 * g + 1]) for g in range(ng)], list(res[2 * ng:])


def _gather_forward(bufs, recv, after, name, flips=ALL_FLIPS):
    m = len(bufs)
    nf = len(flips)

    def body(*refs):
        ins, recv_in = refs[:m], refs[m]
        fsend, frecv = refs[m + 1 + len(after)], refs[m + 2 + len(after)]
        x, y, c = _place()
        chips = _other_chips(x, y)
        for jj, j in enumerate(flips):
            cx, cy = chips[j]
            for i in range(m):
                landed = ins[i].at[2 * cx + cy, c]
                k = nf * i + jj
                _remote(landed, landed, fsend.at[k], recv_in.at[k], (cx, cy, c)).wait_recv()
                _remote(landed, landed, fsend.at[k], frecv.at[k], (x, y, 1 - c)).start()

    res = pl.pallas_call(
        body, name=name, in_specs=[HBM_SPEC] * m + [SEM_SPEC] + [HBM] * len(after),
        out_specs=[SEM_SPEC, SEM_SPEC] + [HBM_SPEC] * m, out_shape=[_sems(nf * m), _sems(nf * m)] + _hbm_like(bufs),
        input_output_aliases={i: 2 + i for i in range(m)},
        compiler_params=pltpu.CompilerParams(has_side_effects=EFFECT),
    )(*bufs, recv, *after)
    return (res[0], res[1]), list(res[2:])


def _gather_finish(bufs, send, fsend, frecv, after, name, flips=ALL_FLIPS):
    m = len(bufs)
    nf = len(flips)

    def body(*refs):
        ins = refs[:m]
        send_in, fsend_in, frecv_in = refs[m:m + 3]
        x, y, c = _place()
        me = 2 * x + y
        chips = _other_chips(x, y)
        for jj, j in enumerate(flips):
            cx, cy = chips[j]
            cj = 2 * cx + cy
            for i in range(m):
                k = nf * i + jj
                mine = ins[i].at[me, c]
                _remote(mine, mine, send_in.at[k], frecv_in.at[k], (cx, cy, c)).wait_send()
                landed = ins[i].at[cj, c]
                _remote(landed, landed, fsend_in.at[k], frecv_in.at[k], (x, y, 1 - c)).wait_send()
                theirs = ins[i].at[cj, 1 - c]
                _remote(theirs, theirs, fsend_in.at[k], frecv_in.at[k], (x, y, 1 - c)).wait_recv()

    return list(pl.pallas_call(
        body, name=name, in_specs=[HBM_SPEC] * m + [SEM_SPEC] * 3 + [HBM] * len(after),
        out_specs=[HBM_SPEC] * m, out_shape=_hbm_like(bufs),
        input_output_aliases={i: i for i in range(m)},
        compiler_params=pltpu.CompilerParams(has_side_effects=EFFECT),
    )(*bufs, send, fsend, frecv, *after))


def _pair_exchange(grads, name):
    n = len(grads)

    def body(*refs):
        ins, outs = refs[:n], refs[n:2 * n]
        send_sems, recv_sems = refs[2 * n:]
        x, y, c = _place()
        sibling = (x, y, 1 - c)
        cps = []
        for w in range(n):
            for j in range(N_CHIPS):
                cp = pltpu.make_async_remote_copy(
                    src_ref=ins[w].at[j, 1 - c], dst_ref=outs[w].at[j], send_sem=send_sems.at[N_CHIPS * w + j],
                    recv_sem=recv_sems.at[N_CHIPS * w + j], device_id=sibling, device_id_type=MESH)
                cp.start()
                cps.append(cp)
        for cp in cps:
            cp.wait()

    return pl.pallas_call(
        body, name=name, in_specs=[HBM] * n, out_specs=[HBM] * n,
        out_shape=[jax.ShapeDtypeStruct((N_CHIPS,) + a.shape[2:], a.dtype) for a in grads],
        scratch_shapes=[pltpu.SemaphoreType.DMA((N_CHIPS * n,)), pltpu.SemaphoreType.DMA((N_CHIPS * n,))],
    )(*grads)


def _chip_start(sums, name, to_all=()):
    m = len(sums)
    lands = [lax.empty(((N_CHIPS,) if i in to_all else ()) + a.shape, a.dtype) for i, a in enumerate(sums)]

    def body(*refs):
        ins, land_in = refs[:m], refs[m:2 * m]
        send, recv = refs[2 * m], refs[2 * m + 1]
        token = refs[4 * m + 2]
        x, y, c = _place()
        me = 2 * x + y
        for i in sorted(range(m), key=lambda i: i not in to_all):
            for j, (cx, cy) in enumerate(_other_chips(x, y)):
                src = ins[i] if i in to_all else ins[i].at[2 * cx + cy]
                _remote(src, land_in[i].at[me], send.at[3 * i + j], recv.at[3 * i + j], (cx, cy, c)).start()
        token[...] = jnp.zeros_like(token)

    res = pl.pallas_call(
        body, name=name, in_specs=[HBM_SPEC] * (2 * m),
        out_specs=[SEM_SPEC, SEM_SPEC] + [HBM_SPEC] * (2 * m) + [TOKEN_SPEC],
        out_shape=[_sems(3 * m), _sems(3 * m)] + _hbm_like(sums) + _hbm_like(lands) + [TOKEN],
        input_output_aliases={i: 2 + i for i in range(2 * m)},
        compiler_params=pltpu.CompilerParams(has_side_effects=EFFECT),
    )(*_in_hbm(sums), *_in_hbm(lands))
    return (res[0], res[1]), list(res[2:2 + m]), list(res[2 + m:2 + 2 * m]), res[2 + 2 * m]


def _chip_wait(sems, sums, lands, after, name, to_all=()):
    m = len(sums)

    def body(*refs):
        ins, land_in = refs[:m], refs[m:2 * m]
        send_in, recv_in = refs[2 * m], refs[2 * m + 1]
        x, y, c = _place()
        for i in range(m):
            for j, (cx, cy) in enumerate(_other_chips(x, y)):
                cj = 2 * cx + cy
                src = ins[i] if i in to_all else ins[i].at[cj]
                cp = _remote(src, land_in[i].at[cj], send_in.at[3 * i + j], recv_in.at[3 * i + j], (cx, cy, c))
                cp.wait_send()
                cp.wait_recv()

    res = pl.pallas_call(
        body, name=name, in_specs=[HBM_SPEC] * (2 * m) + [SEM_SPEC, SEM_SPEC] + [HBM] * len(after),
        out_specs=[HBM_SPEC] * (2 * m), out_shape=_hbm_like(sums) + _hbm_like(lands),
        input_output_aliases={i: i for i in range(2 * m)},
        compiler_params=pltpu.CompilerParams(has_side_effects=EFFECT),
    )(*sums, *lands, sems[0], sems[1], *after)
    return list(res[:m]), list(res[m:])


def _half_exchange(bufs, name):
    n = len(bufs)
    parts = []
    for w, a in enumerate(bufs):
        parts += [(w, None)] if a.ndim == 3 else [(w, j) for j in range(a.shape[0])]
    np_ = len(parts)

    def body(*refs):
        outs = refs[n:2 * n]
        send_sems, recv_sems = refs[2 * n:]
        x, y, c = _place()
        sibling = (x, y, 1 - c)

        def half(w, j, h):
            return outs[w].at[h] if j is None else outs[w].at[j, h]

        cps = []
        for k, (w, j) in enumerate(parts):
            rc = _remote(half(w, j, c), half(w, j, c), send_sems.at[k], recv_sems.at[k], sibling)
            rc.start()
            cps.append(rc)
        for k, (w, j) in enumerate(parts):
            theirs = half(w, j, 1 - c)
            _remote(theirs, theirs, send_sems.at[k], recv_sems.at[k], sibling).wait_recv()
        for cp in cps:
            cp.wait_send()

    return pl.pallas_call(
        body, name=name, in_specs=[HBM] * n, out_specs=[HBM] * n,
        out_shape=[jax.ShapeDtypeStruct(a.shape, a.dtype) for a in bufs],
        input_output_aliases={w: w for w in range(n)},
        scratch_shapes=[_sems(np_), _sems(np_)],
    )(*bufs)


def _all_gather8(block, name, after=()):
    def body(in_ref, *rest):
        out_ref, send_sems, recv_sems, local_sem = rest[len(after):]
        x, y, c = _place()
        me = 4 * x + 2 * y + c
        mine = pltpu.make_async_copy(in_ref, out_ref.at[me], local_sem)
        mine.start()
        flips = [(fx, fy, fc) for fx in (0, 1) for fy in (0, 1) for fc in (0, 1)][1:]
        cps = []
        for k, (fx, fy, fc) in enumerate(flips):
            cp = pltpu.make_async_remote_copy(
                src_ref=in_ref, dst_ref=out_ref.at[me], send_sem=send_sems.at[k], recv_sem=recv_sems.at[k],
                device_id=(x ^ fx, y ^ fy, c ^ fc), device_id_type=MESH)
            cp.start()
            cps.append(cp)
        for k, (fx, fy, fc) in enumerate(flips):
            px, py, pc = x ^ fx, y ^ fy, c ^ fc
            theirs = out_ref.at[4 * px + 2 * py + pc]
            pltpu.make_async_remote_copy(
                src_ref=theirs, dst_ref=theirs, send_sem=send_sems.at[k], recv_sem=recv_sems.at[k],
                device_id=(px, py, pc), device_id_type=MESH).wait_recv()
        for cp in cps:
            cp.wait_send()
        mine.wait()

    return pl.pallas_call(
        body, name=name, in_specs=[HBM] * (1 + len(after)), out_specs=HBM,
        out_shape=jax.ShapeDtypeStruct((N_DEV,) + block.shape, block.dtype),
        scratch_shapes=[pltpu.SemaphoreType.DMA((N_DEV - 1,)), pltpu.SemaphoreType.DMA((N_DEV - 1,)),
                        pltpu.SemaphoreType.DMA],
    )(block, *after)


def _pair_sum(grad, recv, c_arr, name):
    _, _, rh, cols = grad.shape
    tr = _row_tile(rh, cols * grad.dtype.itemsize, STREAM_TILE // 2)

    def body(c_ref, g_ref, r_ref, o_ref):
        del c_ref
        o_ref[...] = (g_ref[...].astype(F32) + r_ref[...].astype(F32)).astype(o_ref.dtype)

    spec = pl.BlockSpec((None, tr, cols), lambda j, i, c_ref: (j, i, 0))
    return pl.pallas_call(
        body, name=name,
        grid_spec=pltpu.PrefetchScalarGridSpec(
            num_scalar_prefetch=1, grid=(N_CHIPS, rh // tr),
            in_specs=[pl.BlockSpec((None, None, tr, cols), lambda j, i, c_ref: (j, c_ref[0], i, 0)), spec],
            out_specs=spec),
        out_shape=jax.ShapeDtypeStruct(recv.shape, grad.dtype), compiler_params=_params(32),
    )(c_arr, grad, recv)


def _sum_by_chip(chip, p_ref, own_ref, o_ref):
    o_ref[...] = jnp.zeros_like(o_ref)
    for k in range(N_CHIPS):
        @pl.when(chip == k)
        def _():
            o_ref[...] += own_ref[...].astype(F32)

        @pl.when(chip != k)
        def _(k=k):
            o_ref[...] += p_ref[k].astype(F32)


def _chip_sum_all(parts, own, place_arr, name):
    _, nj, rh, cols = parts.shape
    tr = _row_tile(rh, cols * 4, STREAM_TILE // 2)

    def body(place_ref, p_ref, own_ref, o_ref):
        _sum_by_chip(place_ref[0], p_ref, own_ref, o_ref)

    return pl.pallas_call(
        body, name=name,
        grid_spec=pltpu.PrefetchScalarGridSpec(
            num_scalar_prefetch=1, grid=(nj, rh // tr),
            in_specs=[pl.BlockSpec((N_CHIPS, None, tr, cols), lambda j, i, place_ref: (0, j, i, 0)),
                      pl.BlockSpec((None, tr, cols), lambda j, i, place_ref: (j, i, 0))],
            out_specs=pl.BlockSpec((None, None, tr, cols), lambda j, i, place_ref: (j, place_ref[1], i, 0))),
        out_shape=jax.ShapeDtypeStruct((nj, 2, rh, cols), F32), compiler_params=_params(32),
    )(place_arr, parts, own)


def _chip_sum(parts, own, place_arr, name):
    _, rh, cols = parts.shape
    tr = _row_tile(rh, cols * 4, STREAM_TILE // 2)

    def body(place_ref, p_ref, own_ref, o_ref):
        _sum_by_chip(place_ref[0], p_ref, own_ref, o_ref)

    return pl.pallas_call(
        body, name=name,
        grid_spec=pltpu.PrefetchScalarGridSpec(
            num_scalar_prefetch=1, grid=(rh // tr,),
            in_specs=[pl.BlockSpec((N_CHIPS, tr, cols), lambda i, place_ref: (0, i, 0)),
                      pl.BlockSpec((None, tr, cols), lambda i, place_ref: (place_ref[0], i, 0))],
            out_specs=pl.BlockSpec((None, tr, cols), lambda i, place_ref: (place_ref[1], i, 0))),
        out_shape=jax.ShapeDtypeStruct((2, rh, cols), F32), compiler_params=_params(32),
    )(place_arr, parts, own)


def _adamw_math(w, g, m, v):
    m = ADAM_B1 * m + (1.0 - ADAM_B1) * g
    v = ADAM_B2 * v + (1.0 - ADAM_B2) * (g * g)
    m_hat = m / (1.0 - ADAM_B1 ** ADAM_STEP)
    v_hat = v / (1.0 - ADAM_B2 ** ADAM_STEP)
    delta = -ADAM_LR * (m_hat / (jnp.sqrt(v_hat) + ADAM_EPS) + ADAM_WD * w)
    return delta, m, v


def _adamw(w, g, m, v, name):
    rows, cols = w.shape
    tr = _row_tile(rows, cols * 4)

    def body(w_ref, g_ref, m_ref, v_ref, go_ref, d_ref, nm_ref, nv_ref):
        gv = g_ref[...]
        go_ref[...] = gv
        d_ref[...], nm_ref[...], nv_ref[...] = _adamw_math(w_ref[...], gv, m_ref[...], v_ref[...])

    spec = pl.BlockSpec((tr, cols), lambda i: (i, 0))
    return pl.pallas_call(
        body, name=name, grid=(rows // tr,), in_specs=[spec] * 4, out_specs=[spec] * 4,
        out_shape=[jax.ShapeDtypeStruct(w.shape, F32)] * 4, compiler_params=_params(32),
    )(w, g, m, v)


def _sum8_adamw_row(parts, w, m, v, name):
    cols = parts.shape[2]

    def body(p_ref, w_ref, m_ref, v_ref, g_ref, d_ref, nm_ref, nv_ref):
        g = p_ref[0, 0:1, :]
        for k in range(1, N_DEV):
            g = g + p_ref[k, 0:1, :]
        g_ref[...] = g
        d_ref[...], nm_ref[...], nv_ref[...] = _adamw_math(w_ref[...], g, m_ref[...], v_ref[...])

    return pl.pallas_call(
        body, name=name, out_shape=[jax.ShapeDtypeStruct((1, cols), F32)] * 4, compiler_params=_params(32),
    )(parts, w, m, v)


def _pack_rows(pieces, rows, name):
    cols = pieces[0].shape[1]
    n = len(pieces)

    def body(*refs):
        o_ref = refs[n]
        o_ref[...] = jnp.zeros_like(o_ref)
        at = 0
        for p_ref in refs[:n]:
            r = p_ref.shape[0]
            o_ref[at:at + r, :] = p_ref[...]
            at += r

    return pl.pallas_call(
        body, name=name, out_shape=jax.ShapeDtypeStruct((rows, cols), F32), compiler_params=_params(32),
    )(*pieces)


def kernel(x, norm_mix_g, w_in, conv_w, conv_b, lru_wa, lru_ba, lru_wx, lru_bx, lru_lambda, w_proj_attn, w_proj_lru, w_out, norm_mlp_g, w_up, w_down, norm_final_g, loss_target, m_norm_mix_g, m_w_in, m_conv_w, m_conv_b, m_lru_wa, m_lru_ba, m_lru_wx, m_lru_bx, m_lru_lambda, m_w_proj_attn, m_w_proj_lru, m_w_out, m_norm_mlp_g, m_w_up, m_w_down, m_norm_final_g, v_norm_mix_g, v_w_in, v_conv_w, v_conv_b, v_lru_wa, v_lru_ba, v_lru_wx, v_lru_bx, v_lru_lambda, v_w_proj_attn, v_w_proj_lru, v_w_out, v_norm_mlp_g, v_w_up, v_w_down, v_norm_final_g):
    s, d = x.shape[1], x.shape[2]
    ff = w_up.shape[2] * N_CHIPS
    heads = d // HEAD_DIM
    u = d // 4
    assert s % (max(DILATIONS) * ATTN_BLK) == 0 and d % (4 * HEAD_DIM) == 0 and ff == 4 * d and DILATIONS[0] == 1
    xs, target = _in_hbm([x[0], loss_target[0]])
    gf = norm_final_g.reshape(1, d)
    wa, wx = lru_wa[0], lru_wx[0]
    core = lax.axis_index("c").astype(jnp.int32)
    chip = (2 * lax.axis_index("x") + lax.axis_index("y")).astype(jnp.int32)
    cidx = core.reshape(1)
    chip_arr = chip.reshape(1)
    place_arr = jnp.stack([chip, core])
    slopes = jnp.broadcast_to(
        (2.0 ** (-8.0 * jnp.arange(1, heads + 1, dtype=F32) / heads))[:, None, None], (heads, 1, HEAD_DIM))

    big = _in_hbm([w_in[0], w_proj_attn[0], w_proj_lru[0], w_out[0], w_up[0], w_down[0]])
    names = ["w_in", "w_proj_attn", "w_proj_lru", "w_out", "w_up", "w_down"]
    cw_pad = jnp.pad(conv_w[0], ((0, 8 - CONV_TAPS), (0, 0)))
    cw_all = _all_gather8(cw_pad, "gather_conv_w")
    conv_w_full = jnp.concatenate([cw_all[2 * j] for j in range(N_CHIPS)], axis=1)
    w_in_groups = [([0], (j,)) for j in ALL_FLIPS]
    sem_a, buf_a = _gather_start([_cast_shard(big[0], chip_arr, "cast_w_in")], w_in_groups, [cw_all], "gather_start_w_in")
    bufs = [_cast_shard(w, chip_arr, "cast_" + nm) for w, nm in zip(big[1:], names[1:])]
    (sem_b, sem_c), bufs = _gather_start(bufs, [([0, 1, 2], ALL_FLIPS), ([3, 4], ALL_FLIPS)], buf_a,
                                         "gather_start_rest")

    xn = _rms_fwd(xs, norm_mix_g, "norm_mix")
    flip_bits = (2, 1, 3)

    def w_in_view():
        return buf_a[0].reshape(N_CHIPS, d, N_SLOTS * u)

    proj = _proj_in_shard(xn, w_in_view(), chip_arr, None, "proj_in_own")
    fsem = {}
    for j in (0, 1):
        fsem[j], buf_a = _gather_forward(buf_a, sem_a[j][1], [proj, bufs[0]] if j == 0 else [], "gather_forward_w_in_%d" % j,
                                         flips=(j,))
    for j in ALL_FLIPS:
        if j == 2:
            fsem[j], buf_a = _gather_forward(buf_a, sem_a[j][1], [proj], "gather_forward_w_in_2", flips=(j,))
        buf_a = _gather_finish(buf_a, sem_a[j][0], fsem[j][0], fsem[j][1], [proj], "gather_finish_w_in_%d" % j, flips=(j,))
        proj = _proj_in_shard(xn, w_in_view(), chip_arr ^ flip_bits[j], proj, "proj_in_from_%d" % j)
    w_in_g = w_in_view()
    proj_a = proj_b = proj
    y_attn, lse = _attn_fwd(proj_a, slopes)
    y_lru, h_lru = _lru_fwd(proj_b, conv_w_full, conv_b, wa, lru_ba, wx, lru_bx, lru_lambda)
    fsem_b, buf_b = _gather_forward(bufs[:3], sem_b[1], [y_attn, y_lru], "gather_forward_proj")
    fsem_c, buf_c = _gather_forward(bufs[3:], sem_c[1], [buf_b[0]], "gather_forward_mlp")
    buf_b = _gather_finish(buf_b, sem_b[0], fsem_b[0], fsem_b[1], [buf_c[0]], "gather_finish_proj")
    wpa_g = buf_b[0].reshape(d, d)
    wpl_g = buf_b[1].reshape(d, d)
    wout_g = buf_b[2].reshape(d, d)

    tn = u
    sd_f32 = jax.ShapeDtypeStruct((s, d), F32)
    sd_bf16 = jax.ShapeDtypeStruct((s, d), BF16)
    col = pl.BlockSpec((s, tn), lambda i, j, k: (0, j))

    def slot(n):
        return pl.BlockSpec((None, s, tn), lambda i, j, k: (n, 0, j))

    p_attn = _mm_nn("proj_attn", y_attn, wpa_g, [], [], [sd_f32], [col], _store, tn)[0]

    def merge(acc, extras, outs):
        pa_ref, ga_ref, gl_ref = extras
        merged = _sigmoid(ga_ref[...]) * pa_ref[...] + _sigmoid(gl_ref[...]) * acc
        outs[0][...] = merged.astype(BF16)
        outs[1][...] = acc

    tn2 = max(HEAD_DIM, u // 2)
    col2 = pl.BlockSpec((s, tn2), lambda i, j, k: (0, j))

    def slot2(n):
        return pl.BlockSpec((None, s, tn2), lambda i, j, k: (n, 0, j))

    merged, p_lru = _mm_nn("proj_lru_merge", y_lru, wpl_g, [p_attn, proj, proj], [col2, slot2(5), slot2(6)],
                           [sd_bf16, sd_f32], [col2, col2], merge, tn2)

    def add_resid(acc, extras, outs):
        outs[0][...] = extras[0][...] + acc

    h1 = _mm_nn("w_out_resid", merged, wout_g, [xs], [col], [sd_f32], [col], add_resid, tn)[0]
    hn = _rms_fwd(h1, norm_mlp_g, "norm_mlp")
    buf_c = _gather_finish(buf_c, sem_c[0], fsem_c[0], fsem_c[1], [hn], "gather_finish_mlp")
    wup_g = buf_c[0].reshape(N_CHIPS, d, d)
    wdown_g = buf_c[1].reshape(ff, d)

    def relu_sq(acc, extras, outs):
        r = jnp.maximum(acc, 0.0)
        outs[0][...] = (r * r).astype(BF16)
        outs[1][...] = r.astype(BF16)

    sf_bf16 = jax.ShapeDtypeStruct((s, ff), BF16)
    hid, relu_up = _mm(
        "w_up_relu2", [hn, wup_g],
        [pl.BlockSpec((s, d), lambda i, j, k: (0, 0)),
         pl.BlockSpec((None, d, tn), lambda i, j, k: (j // 4, 0, j % 4))],
        [sf_bf16, sf_bf16], [col, col], (1, ff // tn, 1), NN, relu_sq)
    h2 = _mm(
        "w_down_resid", [hid, wdown_g, h1],
        [pl.BlockSpec((s, d), lambda i, j, k: (0, k)), pl.BlockSpec((d, tn), lambda i, j, k: (k, j)), col],
        [sd_f32], [col], (1, d // tn, ff // d), NN, add_resid, nk=ff // d, acc_shape=(s, tn))[0]
    loss_part, dh2, dh2_b, d_gf = _loss_head(h2, gf, target)
    loss = lax.psum(loss_part[0, 0], AXES)

    def relu_sq_bwd(acc, extras, outs):
        outs[0][...] = (acc * (2.0 * extras[0][...].astype(F32))).astype(BF16)

    dup = _mm_nt("d_hid", dh2_b, wdown_g, [relu_up], [col], [sf_bf16], [col], relu_sq_bwd, tn)[0]
    tok_d = pl.BlockSpec((s, d), lambda i, j: (0, 0))
    g_wdown = _mm_tn(
        "g_w_down", hid, dh2_b, pl.BlockSpec((s, d), lambda i, j: (0, i)),
        pl.BlockSpec((s, tn), lambda i, j: (0, j)), jax.ShapeDtypeStruct((ff, d), BF16),
        pl.BlockSpec((d, tn), lambda i, j: (i, j)), (ff // d, d // tn), d, tn, s)
    dhn = _mm(
        "d_hn", [dup, wup_g],
        [pl.BlockSpec((s, d), lambda i, j, k: (0, k)), pl.BlockSpec((None, tn, d), lambda i, j, k: (k, j, 0))],
        [sd_f32], [col], (1, d // tn, ff // d), NT, _store, nk=ff // d, acc_shape=(s, tn))[0]
    g_wup = _mm_tn(
        "g_w_up", hn, dup, tok_d, pl.BlockSpec((s, tn), lambda i, j: (0, j)),
        jax.ShapeDtypeStruct((N_CHIPS, d, d), BF16), pl.BlockSpec((None, d, tn), lambda i, j: (j // 4, 0, j % 4)),
        (1, ff // tn), d, tn, s)
    big_m = _in_hbm([m_w_in[0], m_w_proj_attn[0], m_w_proj_lru[0], m_w_out[0], m_w_up[0], m_w_down[0]])
    big_v = _in_hbm([v_w_in[0], v_w_proj_attn[0], v_w_proj_lru[0], v_w_out[0], v_w_up[0], v_w_down[0]])
    big_out = {}

    def reduce_begin(ids, gs, tag, everywhere=None):
        g4 = [g.reshape(N_CHIPS, 2, big[i].shape[0] // 2, big[i].shape[1]) for i, g in zip(ids, gs)]
        tags = [names[i] for i in ids]
        if everywhere is not None:
            g4.append(everywhere.reshape(N_CHIPS, 2, everywhere.shape[0] // (2 * N_CHIPS), everywhere.shape[1]))
            tags.append("small_" + tag)
        from_sibling = _pair_exchange(g4, "pair_exchange_" + tag)
        sums = [_pair_sum(g, r, cidx, "pair_sum_" + t) for t, g, r in zip(tags, g4, from_sibling)]
        to_all = () if everywhere is None else (len(ids),)
        return _chip_start(sums, "chip_start_" + tag, to_all), to_all

    def reduce_end(ids, begun, after, tag):
        (sems, sums, lands, _), to_all = begun
        sums, lands = _chip_wait(sems, sums, lands, after, "chip_wait_" + tag, to_all)
        halves = [_chip_sum(p, own, place_arr, "chip_sum_" + names[i]) for i, p, own in zip(ids, lands, sums)]
        if to_all:
            halves.append(_chip_sum_all(lands[-1], sums[-1], place_arr, "chip_sum_small_" + tag))
        full = _half_exchange(halves, "half_exchange_" + tag)
        last = None
        for i, g in zip(ids, full):
            res = _adamw(big[i], g.reshape(big[i].shape), big_m[i], big_v[i], "adamw_" + names[i])
            big_out[names[i]] = tuple(a[None] for a in res)
            last = res[1]
        everywhere = full[-1].reshape(-1, full[-1].shape[-1]) if to_all else None
        return everywhere, last

    def after_token(a, begun):
        return a + begun[0][3][:1, :1]

    red_mlp = reduce_begin([4, 5], [g_wup, g_wdown], "mlp")
    dh1, dh1_b, d_gmlp = _rms_bwd(h1, after_token(norm_mlp_g, red_mlp), dhn, dh2, "norm_mlp_bwd")

    g_wout = _mm_tn(
        "g_w_out", merged, dh1_b, tok_d, pl.BlockSpec((s, tn), lambda i, j: (0, j)),
        jax.ShapeDtypeStruct((d, d), BF16), pl.BlockSpec((d, tn), lambda i, j: (0, j)), (1, d // tn), d, tn, s)

    def merge_bwd(acc, extras, outs):
        pa_ref, pl_ref, ga_ref, gl_ref = extras
        sa, sl = _sigmoid(ga_ref[...]), _sigmoid(gl_ref[...])
        outs[0][...] = (acc * sa).astype(BF16)
        outs[1][...] = (acc * sl).astype(BF16)
        outs[2][0] = (acc * pa_ref[...] * (sa * (1.0 - sa))).astype(BF16)
        outs[2][1] = (acc * pl_ref[...] * (sl * (1.0 - sl))).astype(BF16)

    nb = N_SLOTS - N_QKV
    d_pa, d_pl, dproj_b = _mm_nt(
        "d_merged", dh1_b, wout_g, [p_attn, p_lru, proj, proj], [col2, col2, slot2(5), slot2(6)],
        [sd_bf16, sd_bf16, jax.ShapeDtypeStruct((nb, s, d), BF16)],
        [col2, col2, pl.BlockSpec((2, s, tn2), lambda i, j, k: (1, 0, j))], merge_bwd, tn2)
    dy_attn = _mm_nt("d_y_attn", d_pa, wpa_g, [], [], [sd_f32], [col], _store, tn)[0]
    dy_lru = _mm_nt("d_y_lru", d_pl, wpl_g, [], [], [sd_f32], [col], _store, tn)[0]
    g_wpa = _mm_tn(
        "g_w_proj_attn", y_attn, d_pa, tok_d, pl.BlockSpec((s, tn), lambda i, j: (0, j)),
        jax.ShapeDtypeStruct((d, d), BF16), pl.BlockSpec((d, tn), lambda i, j: (0, j)), (1, d // tn), d, tn, s)
    g_wpl = _mm_tn(
        "g_w_proj_lru", y_lru, d_pl, tok_d, pl.BlockSpec((s, tn), lambda i, j: (0, j)),
        jax.ShapeDtypeStruct((d, d), BF16), pl.BlockSpec((d, tn), lambda i, j: (0, j)), (1, d // tn), d, tn, s)

    red_proj = reduce_begin([1, 2, 3], [g_wpa, g_wpl, g_wout], "proj")

    dproj_a = _attn_bwd(proj_a, after_token(slopes, red_proj), y_attn, lse, dy_attn)
    dproj_b, d_cw, d_cb, d_wa, d_ba, d_wx, d_bx, d_lam = _lru_bwd(
        proj_b, h_lru, dy_lru, conv_w_full, conv_b, wa, lru_ba, wx, lru_bx, after_token(lru_lambda, red_proj),
        dproj_b)
    per = N_SLOTS
    g_win_shape = jax.ShapeDtypeStruct((N_CHIPS, d, N_SLOTS * u), BF16)

    def g_win_part(name, dproj, first, prev):
        n_units = 4 * dproj.shape[0]
        return _mm_tn(
            name, xn, dproj, tok_d, pl.BlockSpec((None, s, u), lambda i, j: (j // 4, 0, j % 4)),
            g_win_shape, pl.BlockSpec((None, d, u), lambda i, j: ((j + first) // per, 0, (j + first) % per)),
            (1, n_units), d, u, s, aliases=None if prev is None else {2: 0}, extra=prev)

    mat_rows = heads * HEAD_DIM * HEAD_DIM // d
    vec_names = ["norm_mix_g", "conv_b", "lru_ba", "lru_bx", "lru_lambda", "norm_mlp_g", "norm_final_g"]

    def pack(wa_, wx_, cw_, vecs, name):
        rows = [wa_.reshape(mat_rows, d), wx_.reshape(mat_rows, d), cw_] + [a.reshape(1, d) for a in vecs]
        n = sum(a.shape[0] for a in rows)
        return _pack_rows(rows, n + (-n % 64), name)

    zero_cw = jnp.zeros((CONV_TAPS, d), F32)
    small_g = pack(d_wa, d_wx, d_cw, [jnp.zeros((1, d), F32), d_cb, d_ba, d_bx, d_lam, d_gmlp, d_gf], "pack_small_g")
    g_win = g_win_part("g_w_in_qkv", dproj_a, 0, None)
    g_win = g_win_part("g_w_in_rest", dproj_b, 4 * N_QKV, g_win)
    red_in = reduce_begin([0], [g_win], "w_in", everywhere=small_g)
    dxn = _dxn(dproj_a, dproj_b, w_in_g, 2 * tn, [red_in[0][3]])
    grad_x, _, d_gmix = _rms_bwd(xs, norm_mix_g, dxn, dh1, "norm_mix_bwd")

    _, done = reduce_end([4, 5], red_mlp, [grad_x], "mlp")
    _, done = reduce_end([1, 2, 3], red_proj, [done], "proj")
    small_w = pack(wa, wx, zero_cw, [norm_mix_g, conv_b, lru_ba, lru_bx, lru_lambda, norm_mlp_g, norm_final_g],
                   "pack_small_w")
    small_m = pack(m_lru_wa[0], m_lru_wx[0], zero_cw,
                   [m_norm_mix_g, m_conv_b, m_lru_ba, m_lru_bx, m_lru_lambda, m_norm_mlp_g, m_norm_final_g],
                   "pack_small_m")
    small_v = pack(v_lru_wa[0], v_lru_wx[0], zero_cw,
                   [v_norm_mix_g, v_conv_b, v_lru_ba, v_lru_bx, v_lru_lambda, v_norm_mlp_g, v_norm_final_g],
                   "pack_small_v")
    small_sum, _ = reduce_end([0], red_in, [done, small_w, small_m, small_v], "w_in")
    small = _adamw(small_w, small_sum, small_m, small_v, "adamw_small")
    g_cw = lax.dynamic_slice(small_sum[2 * mat_rows:2 * mat_rows + CONV_TAPS], (0, chip * u), (CONV_TAPS, u))
    cw_out = _adamw(conv_w[0], g_cw, m_conv_w[0], v_conv_w[0], "adamw_conv_w")
    gmix_parts = _all_gather8(jnp.pad(d_gmix, ((0, 7), (0, 0))), "gather_gain_grad", [small[1]])
    gmix_out = _sum8_adamw_row(gmix_parts, norm_mix_g, m_norm_mix_g, v_norm_mix_g, "sum_adamw_norm_mix_g")

    def small_leaf(kind, name):
        a = small[kind]
        if name == "norm_mix_g":
            return gmix_out[kind]
        if name == "lru_wa":
            return a[0:mat_rows].reshape(lru_wa.shape)
        if name == "lru_wx":
            return a[mat_rows:2 * mat_rows].reshape(lru_wx.shape)
        if name == "conv_w":
            return cw_out[kind][None]
        row = a[2 * mat_rows + CONV_TAPS + vec_names.index(name)]
        return row if name == "norm_final_g" else row[None]

    order = ["norm_mix_g", "w_in", "conv_w", "conv_b", "lru_wa", "lru_ba", "lru_wx", "lru_bx", "lru_lambda",
             "w_proj_attn", "w_proj_lru", "w_out", "norm_mlp_g", "w_up", "w_down", "norm_final_g"]
    outs = [loss, grad_x[None]]
    for kind in range(4):
        for name in order:
            outs.append(big_out[name][kind] if name in big_out else small_leaf(kind, name))
    return tuple(outs)
```

```python
import functools

import jax
import jax.numpy as jnp
from jax import lax
from jax.experimental import pallas as pl
from jax.experimental.pallas import tpu as pltpu

F32 = jnp.float32
BF16 = jnp.bfloat16
MESH = pl.DeviceIdType.MESH
AXES = ("x", "y", "c")

N_CHIPS = 4
N_DEV = 8
HEAD_DIM = 128
ATTN_BLK = 128
DILATIONS = (1, 4, 16)
ATTN_UNROLL = 8
CONV_TAPS = 4
LRU_C = 8.0
EPS = 1e-6
N_SLOTS = 7
N_QKV = 3
VMEM_MIB = 2 ** 20
VMEM_V7X = 64 * VMEM_MIB
STREAM_TILE = 4 * VMEM_MIB

ADAM_LR = 0.001
ADAM_B1 = 0.9
ADAM_B2 = 0.999
ADAM_EPS = 1e-08
ADAM_WD = 0.01
ADAM_STEP = 10

NN = (((1,), (0,)), ((), ()))
NT = (((1,), (1,)), ((), ()))
TN = (((0,), (0,)), ((), ()))


def _params(vmem_mib=None, **kw):
    limit = None if vmem_mib is None else min(vmem_mib * VMEM_MIB, VMEM_V7X - 8 * VMEM_MIB)
    return pltpu.CompilerParams(vmem_limit_bytes=limit, **kw)


def _row_tile(rows, row_bytes, budget=VMEM_MIB):
    t = rows
    while t % 16 == 0 and t * row_bytes > budget:
        t //= 2
    return t


def _dot(a, b, dims):
    return lax.dot_general(a.astype(BF16), b.astype(BF16), dims, preferred_element_type=F32)


def _sigmoid(x):
    return jax.nn.sigmoid(x)


def _rms_fwd(x, g, name):
    s, d = x.shape
    tm = _row_tile(s, d * 4)

    def body(x_ref, g_ref, o_ref):
        xf = x_ref[...]
        r = lax.rsqrt(jnp.mean(xf * xf, axis=-1, keepdims=True) + EPS)
        o_ref[...] = (xf * r * g_ref[...]).astype(o_ref.dtype)

    return pl.pallas_call(
        body, name=name, grid=(s // tm,),
        in_specs=[pl.BlockSpec((tm, d), lambda i: (i, 0)), pl.BlockSpec((1, d), lambda i: (0, 0))],
        out_specs=pl.BlockSpec((tm, d), lambda i: (i, 0)),
        out_shape=jax.ShapeDtypeStruct((s, d), BF16), compiler_params=_params(32),
    )(x, g)


def _rms_bwd(x, g, dy, resid, name):
    s, d = x.shape
    tm = _row_tile(s, d * 4)

    def body(x_ref, g_ref, dy_ref, res_ref, dx_ref, dxb_ref, dg_ref):
        xf = x_ref[...]
        r = lax.rsqrt(jnp.mean(xf * xf, axis=-1, keepdims=True) + EPS)
        xh = xf * r
        dyv = dy_ref[...]
        dxh = dyv * g_ref[...]
        dx = r * (dxh - xh * jnp.mean(dxh * xh, axis=-1, keepdims=True)) + res_ref[...]
        dx_ref[...] = dx
        dxb_ref[...] = dx.astype(BF16)
        part = jnp.sum(dyv * xh, axis=0, keepdims=True)

        @pl.when(pl.program_id(0) == 0)
        def _():
            dg_ref[...] = part

        @pl.when(pl.program_id(0) > 0)
        def _():
            dg_ref[...] += part

    row = pl.BlockSpec((tm, d), lambda i: (i, 0))
    vec = pl.BlockSpec((1, d), lambda i: (0, 0))
    return pl.pallas_call(
        body, name=name, grid=(s // tm,),
        in_specs=[row, vec, row, row], out_specs=[row, row, vec],
        out_shape=[jax.ShapeDtypeStruct((s, d), F32), jax.ShapeDtypeStruct((s, d), BF16),
                   jax.ShapeDtypeStruct((1, d), F32)],
        compiler_params=_params(32),
    )(x, g, dy, resid)


def _loss_head(h2, g, target):
    s, d = h2.shape
    tm = _row_tile(s, d * 4)

    def body(x_ref, g_ref, t_ref, loss_ref, dx_ref, dxb_ref, dg_ref):
        xf = x_ref[...]
        gv = g_ref[...]
        r = lax.rsqrt(jnp.mean(xf * xf, axis=-1, keepdims=True) + EPS)
        xh = xf * r
        err = xh * gv - t_ref[...]
        part = jnp.sum(jnp.sum(err * err, axis=1, keepdims=True), axis=0, keepdims=True) * (0.5 / d)
        dyv = err * (1.0 / d)
        dxh = dyv * gv
        dx = r * (dxh - xh * jnp.mean(dxh * xh, axis=-1, keepdims=True))
        dx_ref[...] = dx
        dxb_ref[...] = dx.astype(BF16)
        dgp = jnp.sum(dyv * xh, axis=0, keepdims=True)

        @pl.when(pl.program_id(0) == 0)
        def _():
            dg_ref[...] = dgp
            loss_ref[...] = jnp.broadcast_to(part, loss_ref.shape)

        @pl.when(pl.program_id(0) > 0)
        def _():
            dg_ref[...] += dgp
            loss_ref[...] += jnp.broadcast_to(part, loss_ref.shape)

    row = pl.BlockSpec((tm, d), lambda i: (i, 0))
    vec = pl.BlockSpec((1, d), lambda i: (0, 0))
    return pl.pallas_call(
        body, name="loss_head", grid=(s // tm,),
        in_specs=[row, vec, row],
        out_specs=[pl.BlockSpec((8, 128), lambda i: (0, 0)), row, row, vec],
        out_shape=[jax.ShapeDtypeStruct((8, 128), F32), jax.ShapeDtypeStruct((s, d), F32),
                   jax.ShapeDtypeStruct((s, d), BF16), jax.ShapeDtypeStruct((1, d), F32)],
        compiler_params=_params(32),
    )(h2, g, target)


def _mm(name, operands, in_specs, out_shape, out_specs, grid, dims, epilogue, nk=1, acc_shape=None,
        vmem_mib=56, aliases=None):
    n_in = len(operands)
    n_out = len(out_shape)

    def body(*refs):
        a_ref, b_ref = refs[0], refs[1]
        extras = refs[2:n_in]
        outs = refs[n_in:n_in + n_out]

        def prod():
            return _dot(a_ref[...], b_ref[...], dims)

        if nk == 1:
            epilogue(prod(), extras, outs)
        else:
            acc = refs[n_in + n_out]
            k = pl.program_id(2)

            @pl.when(k == 0)
            def _():
                acc[...] = prod()

            @pl.when(k > 0)
            def _():
                acc[...] += prod()

            @pl.when(k == nk - 1)
            def _():
                epilogue(acc[...], extras, outs)

    scratch = [] if nk == 1 else [pltpu.VMEM(acc_shape, F32)]
    return pl.pallas_call(
        body, name=name, grid=grid, in_specs=in_specs, out_specs=out_specs, out_shape=out_shape,
        scratch_shapes=scratch, input_output_aliases=aliases or {},
        compiler_params=_params(vmem_mib),
    )(*operands)


def _store(acc, extras, outs):
    outs[0][...] = acc.astype(outs[0].dtype)


def _proj_in_shard(xn, w_in_g, shard_arr, prev, name):
    s, d = xn.shape
    u = d // 4
    per = N_SLOTS
    n_prev = 0 if prev is None else 1

    def body(sh_ref, x_ref, w_ref, *rest):
        del sh_ref
        rest[n_prev][...] = _dot(x_ref[...], w_ref[...], NN)

    def out_map(j, sh_ref):
        unit = per * sh_ref[0] + j
        return (unit // 4, 0, unit % 4)

    return pl.pallas_call(
        body, name=name,
        grid_spec=pltpu.PrefetchScalarGridSpec(
            num_scalar_prefetch=1, grid=(per,),
            in_specs=[pl.BlockSpec((s, d), lambda j, sh_ref: (0, 0)),
                      pl.BlockSpec((None, d, u), lambda j, sh_ref: (sh_ref[0], 0, j))] + [HBM] * n_prev,
            out_specs=pl.BlockSpec((None, s, u), out_map)),
        out_shape=jax.ShapeDtypeStruct((N_SLOTS, s, d), F32),
        input_output_aliases={3: 0} if n_prev else {},
        compiler_params=_params(56),
    )(shard_arr, xn, w_in_g, *([] if prev is None else [prev]))


def _mm_nn(name, a, b, extras, extra_specs, out_shape, out_specs, epilogue, tn, aliases=None):
    s, kdim = a.shape
    n = b.shape[1]
    return _mm(
        name, [a, b] + list(extras),
        [pl.BlockSpec((s, kdim), lambda i, j, k: (0, 0)), pl.BlockSpec((kdim, tn), lambda i, j, k: (0, j))]
        + list(extra_specs),
        out_shape, out_specs, (1, n // tn, 1), NN, epilogue, aliases=aliases)


def _mm_nt(name, a, b, extras, extra_specs, out_shape, out_specs, epilogue, tn, aliases=None):
    s, kdim = a.shape
    n = b.shape[0]
    return _mm(
        name, [a, b] + list(extras),
        [pl.BlockSpec((s, kdim), lambda i, j, k: (0, 0)), pl.BlockSpec((tn, kdim), lambda i, j, k: (j, 0))]
        + list(extra_specs),
        out_shape, out_specs, (1, n // tn, 1), NT, epilogue, aliases=aliases)


def _mm_tn(name, a, b, a_spec, b_spec, out_shape, out_spec, grid, m, tn, s, aliases=None, extra=None):
    ch = 256
    n_in = 2 if extra is None else 3

    def body(*refs):
        a_ref, b_ref = refs[0], refs[1]
        o_ref, at_ref = refs[n_in], refs[n_in + 1]

        @pl.when(pl.program_id(1) == 0)
        def _():
            for c0 in range(0, s, ch):
                at_ref[:, c0:c0 + ch] = a_ref[c0:c0 + ch, :].astype(F32).T.astype(BF16)

        o_ref[...] = _dot(at_ref[...], b_ref[...], NN).astype(o_ref.dtype)

    operands = [a, b] + ([] if extra is None else [extra])
    in_specs = [a_spec, b_spec] + ([] if extra is None else [pl.BlockSpec(memory_space=pl.ANY)])
    return pl.pallas_call(
        body, name=name, grid=grid, in_specs=in_specs, out_specs=out_spec, out_shape=out_shape,
        scratch_shapes=[pltpu.VMEM((m, s), BF16)], input_output_aliases=aliases or {},
        compiler_params=_params(56),
    )(*operands)


def _dxn(dproj_a, dproj_b, w_in_g, tn, after):
    n_a, s, d = dproj_a.shape
    u = d // 4
    ua = 4 * n_a
    nk = 4 * N_SLOTS
    per = N_SLOTS

    def body(a_ref, b_ref, w_ref, *rest):
        o_ref = rest[len(after)]
        k = pl.program_id(2)

        @pl.when(k == 0)
        def _():
            o_ref[...] = jnp.zeros_like(o_ref)

        @pl.when(k < ua)
        def _():
            o_ref[...] += _dot(a_ref[...], w_ref[...], NT)

        @pl.when(k >= ua)
        def _():
            o_ref[...] += _dot(b_ref[...], w_ref[...], NT)

    def a_map(i, j, k):
        kk = jnp.minimum(k, ua - 1)
        return (kk // 4, 0, kk % 4)

    def b_map(i, j, k):
        kk = jnp.maximum(k - ua, 0)
        return (kk // 4, 0, kk % 4)

    return pl.pallas_call(
        body, name="dxn", grid=(1, d // tn, nk),
        in_specs=[pl.BlockSpec((None, s, u), a_map), pl.BlockSpec((None, s, u), b_map),
                  pl.BlockSpec((None, tn, u), lambda i, j, k: (k // per, j, k % per))] + [HBM] * len(after),
        out_specs=pl.BlockSpec((s, tn), lambda i, j, k: (0, j)),
        out_shape=jax.ShapeDtypeStruct((s, d), F32),
        compiler_params=_params(48),
    )(dproj_a, dproj_b, w_in_g, *after)


def _attn_masks(slope, dil):
    ii = lax.broadcasted_iota(jnp.int32, (ATTN_BLK, 2 * ATTN_BLK), 0)
    jj = lax.broadcasted_iota(jnp.int32, (ATTN_BLK, 2 * ATTN_BLK), 1)
    diff = ATTN_BLK + ii - jj
    band = (diff >= 0) & (diff <= ATTN_BLK)
    bias = -(slope * float(dil)) * diff.astype(F32)
    return band, bias, jj


def _attn_window(t, nblk):
    cur = pl.ds(pl.multiple_of(t * ATTN_BLK, ATTN_BLK), ATTN_BLK)
    prev = pl.ds(pl.multiple_of(jnp.maximum(t - 1, 0) * ATTN_BLK, ATTN_BLK), ATTN_BLK)
    first = jnp.where(t % nblk == 0, ATTN_BLK, 0)
    return prev, cur, first


def _unrolled_loop(n, step, init):
    def trip(i, carry):
        for k in range(ATTN_UNROLL):
            carry = step(i * ATTN_UNROLL + k, carry)
        return carry

    return lax.fori_loop(0, n // ATTN_UNROLL, trip, init)


def _streams(pairs, dil, s):
    if dil == 1:
        return [src for _, src in pairs]
    seg = s // dil
    for dst, src in pairs:
        for r in range(dil):
            dst[r * seg:(r + 1) * seg, :] = src[pl.ds(r, seg, stride=dil), :].astype(dst.dtype)
    return [dst for dst, _ in pairs]


def _attn_fwd(proj_a, slopes):
    _, s, d = proj_a.shape
    heads = d // HEAD_DIM
    scale = HEAD_DIM ** -0.5
    n_t = s // ATTN_BLK
    ng = len(DILATIONS)

    def body(q_ref, k_ref, v_ref, sl_ref, o_ref, lse_ref, qd, kd, vd, od, ld, og, lg):
        slope = sl_ref[...][:, :1]
        for g, dil in enumerate(DILATIONS):
            nblk = s // dil // ATTN_BLK
            qs, ks, vs = _streams([(qd, q_ref), (kd, k_ref), (vd, v_ref)], dil, s)
            o_t, l_t = (og.at[g], lg.at[g]) if dil == 1 else (od, ld)
            band, bias, jj = _attn_masks(slope, dil)

            def blk(t, carry, nblk=nblk, band=band, bias=bias, jj=jj, qs=qs, ks=ks, vs=vs, o_t=o_t, l_t=l_t):
                prev, cur, first = _attn_window(t, nblk)
                kk = jnp.concatenate([ks[prev, :], ks[cur, :]], axis=0)
                vv = jnp.concatenate([vs[prev, :], vs[cur, :]], axis=0)
                sc = _dot(qs[cur, :], kk, NT) * scale + bias
                sc = jnp.where(band & (jj >= first), sc, -jnp.inf)
                m = jnp.max(sc, axis=1, keepdims=True)
                p = jnp.exp(sc - m)
                l = jnp.sum(p, axis=1, keepdims=True)
                o_t[cur, :] = _dot(p, vv, NN) / l
                l_t[cur, :] = jnp.broadcast_to(m + jnp.log(l), (ATTN_BLK, HEAD_DIM))
                return carry

            _unrolled_loop(n_t, blk, 0)
            seg = s // dil
            if dil > 1:
                for r in range(dil):
                    og[g, pl.ds(r, seg, stride=dil), :] = od[r * seg:(r + 1) * seg, :]
                    lg[g, pl.ds(r, seg, stride=dil), :] = ld[r * seg:(r + 1) * seg, :]

        ch = 256

        def combine(c, carry):
            rows = pl.ds(pl.multiple_of(c * ch, ch), ch)
            ls = [lg[g, rows, :] for g in range(ng)]
            mx = functools.reduce(jnp.maximum, ls)
            es = [jnp.exp(x - mx) for x in ls]
            den = functools.reduce(jnp.add, es)
            num = functools.reduce(jnp.add, [es[g] * og[g, rows, :] for g in range(ng)])
            o_ref[rows, :] = (num / den).astype(o_ref.dtype)
            lse_ref[rows, :] = mx + jnp.log(den)
            return carry

        lax.fori_loop(0, s // ch, combine, 0)

    def col(slot):
        return pl.BlockSpec((None, s, HEAD_DIM), lambda h: (slot, 0, h))

    head = pl.BlockSpec((s, HEAD_DIM), lambda h: (0, h))
    return pl.pallas_call(
        body, name="attn_fwd", grid=(heads,),
        in_specs=[col(0), col(1), col(2), pl.BlockSpec((None, 1, HEAD_DIM), lambda h: (h, 0, 0))],
        out_specs=[head, head],
        out_shape=[jax.ShapeDtypeStruct((s, d), BF16), jax.ShapeDtypeStruct((s, d), F32)],
        scratch_shapes=[pltpu.VMEM((s, HEAD_DIM), BF16)] * 3 + [pltpu.VMEM((s, HEAD_DIM), F32)] * 2
        + [pltpu.VMEM((ng, s, HEAD_DIM), F32)] * 2,
        compiler_params=_params(40),
    )(proj_a, proj_a, proj_a, slopes)


def _attn_bwd(proj_a, slopes, y_attn, lse, dy):
    _, s, d = proj_a.shape
    heads = d // HEAD_DIM
    scale = HEAD_DIM ** -0.5
    n_t = s // ATTN_BLK

    def body(q_ref, k_ref, v_ref, sl_ref, o_ref, lse_ref, dy_ref, out_ref,
             qd, kd, vd, dod, lsd, dld, delta, dqd, dkd, dvd, dqa, dka, dva):
        slope = sl_ref[...][:, :1]
        dyv = dy_ref[...]
        delta[...] = jnp.broadcast_to(
            jnp.sum(dyv * o_ref[...].astype(F32), axis=1, keepdims=True), (s, HEAD_DIM))
        for g, dil in enumerate(DILATIONS):
            nblk = s // dil // ATTN_BLK
            seg = s // dil
            qs, ks, vs, dos, lss, dls = _streams(
                [(qd, q_ref), (kd, k_ref), (vd, v_ref), (dod, dy_ref), (lsd, lse_ref), (dld, delta)], dil, s)
            dq_t, dk_t, dv_t = (dqa, dka, dva) if dil == 1 else (dqd, dkd, dvd)
            band, bias, jj = _attn_masks(slope, dil)

            def blk(t, carry, nblk=nblk, band=band, bias=bias, jj=jj, qs=qs, ks=ks, vs=vs, dos=dos, lss=lss,
                    dls=dls, dq_t=dq_t, dk_t=dk_t, dv_t=dv_t):
                ck, cv = carry
                prev, cur, first = _attn_window(t, nblk)
                q = qs[cur, :]
                do = dos[cur, :]
                lse_b = lss[cur, :]
                dl_b = dls[cur, :]
                kk = jnp.concatenate([ks[prev, :], ks[cur, :]], axis=0)
                vv = jnp.concatenate([vs[prev, :], vs[cur, :]], axis=0)
                sc = _dot(q, kk, NT) * scale + bias
                p = jnp.where(band & (jj >= first), jnp.exp(sc - jnp.concatenate([lse_b, lse_b], axis=1)), 0.0)
                dp = _dot(do, vv, NT)
                ds = p * (dp - jnp.concatenate([dl_b, dl_b], axis=1))
                dv_b = _dot(p, do, TN)
                dk_b = _dot(ds, q, TN) * scale
                dq_t[cur, :] = _dot(ds, kk, NN) * scale
                done = pl.ds(pl.multiple_of(jnp.where(t == 0, n_t, t - 1) * ATTN_BLK, ATTN_BLK), ATTN_BLK)
                dk_t[done, :] = ck + dk_b[:ATTN_BLK]
                dv_t[done, :] = cv + dv_b[:ATTN_BLK]
                return dk_b[ATTN_BLK:], dv_b[ATTN_BLK:]

            zero = jnp.zeros((ATTN_BLK, HEAD_DIM), F32)
            ck, cv = _unrolled_loop(n_t, blk, (zero, zero))
            dk_t[(n_t - 1) * ATTN_BLK:n_t * ATTN_BLK, :] = ck
            dv_t[(n_t - 1) * ATTN_BLK:n_t * ATTN_BLK, :] = cv
            if dil > 1:
                for acc, part in ((dqa, dqd), (dka, dkd), (dva, dvd)):
                    for r in range(dil):
                        acc[pl.ds(r, seg, stride=dil), :] += part[r * seg:(r + 1) * seg, :]
        out_ref[0] = dqa[...].astype(out_ref.dtype)
        out_ref[1] = dka[0:s, :].astype(out_ref.dtype)
        out_ref[2] = dva[0:s, :].astype(out_ref.dtype)

    def col(slot):
        return pl.BlockSpec((None, s, HEAD_DIM), lambda h: (slot, 0, h))

    head = pl.BlockSpec((s, HEAD_DIM), lambda h: (0, h))
    return pl.pallas_call(
        body, name="attn_bwd", grid=(heads,),
        in_specs=[col(0), col(1), col(2), pl.BlockSpec((None, 1, HEAD_DIM), lambda h: (h, 0, 0)),
                  head, head, head],
        out_specs=pl.BlockSpec((N_QKV, s, HEAD_DIM), lambda h: (0, 0, h)),
        out_shape=jax.ShapeDtypeStruct((N_QKV, s, d), BF16),
        scratch_shapes=[pltpu.VMEM((s, HEAD_DIM), BF16)] * 4 + [pltpu.VMEM((s, HEAD_DIM), F32)] * 4
        + [pltpu.VMEM((s + ATTN_BLK, HEAD_DIM), F32)] * 2 + [pltpu.VMEM((s, HEAD_DIM), F32)]
        + [pltpu.VMEM((s + ATTN_BLK, HEAD_DIM), F32)] * 2,
        compiler_params=_params(48),
    )(proj_a, proj_a, proj_a, slopes, y_attn, lse, dy)


def _expm1(x):
    small = x * (1.0 + x * (0.5 + x * (1.0 / 6.0 + x * (1.0 / 24.0 + x * (1.0 / 120.0)))))
    return jnp.where(jnp.abs(x) < 0.1, small, jnp.exp(x) - 1.0)


def _softplus(x):
    return jnp.maximum(x, 0.0) + jnp.log1p(jnp.exp(-jnp.abs(x)))


GELU_K = 0.7978845608028654
GELU_C = 0.044715


def _gelu(x):
    t = jnp.tanh(GELU_K * (x + GELU_C * x * x * x))
    return 0.5 * x * (1.0 + t), t


def _gelu_grad(x, t):
    return 0.5 * (1.0 + t) + 0.5 * x * (1.0 - t * t) * GELU_K * (1.0 + 3.0 * GELU_C * x * x)


def _lru_gates(xc, wa, ba, wx, bx, sp):
    r = _sigmoid(_dot(xc, wa, NN) + ba)
    ig = _sigmoid(_dot(xc, wx, NN) + bx)
    log_a = -LRU_C * r * sp
    a = jnp.exp(log_a)
    mult = jnp.sqrt(-_expm1(2.0 * log_a))
    return r, ig, a, mult


def _scan_fwd(a, u, tt):
    row = lax.broadcasted_iota(jnp.int32, a.shape, 0)
    sh = 1
    while sh < tt:
        keep = row >= sh
        a_s = jnp.where(keep, pltpu.roll(a, sh, 0), 1.0)
        u_s = jnp.where(keep, pltpu.roll(u, sh, 0), 0.0)
        u = a * u_s + u
        a = a * a_s
        sh *= 2
    return a, u


def _scan_bwd(b, g, tt):
    row = lax.broadcasted_iota(jnp.int32, b.shape, 0)
    sh = 1
    while sh < tt:
        keep = row < tt - sh
        b_s = jnp.where(keep, pltpu.roll(b, tt - sh, 0), 1.0)
        g_s = jnp.where(keep, pltpu.roll(g, tt - sh, 0), 0.0)
        g = g + b * g_s
        b = b * b_s
        sh *= 2
    return b, g


def _conv_rows(xpad_ref, cw, cb, s):
    acc = cb
    for j in range(CONV_TAPS):
        off = 8 - (CONV_TAPS - 1) + j
        acc = acc + cw[j:j + 1, :] * xpad_ref[off:off + s, :]
    return acc


LRU_TILE = 128


def _lru_specs(s, d):
    heads = d // HEAD_DIM

    def col(slot):
        return pl.BlockSpec((None, s, HEAD_DIM), lambda h: (slot, 0, h))

    vec = pl.BlockSpec((1, HEAD_DIM), lambda h: (0, h))
    mat = pl.BlockSpec((None, HEAD_DIM, HEAD_DIM), lambda h: (h, 0, 0))
    cw = pl.BlockSpec((8, HEAD_DIM), lambda h: (0, h))
    head = pl.BlockSpec((s, HEAD_DIM), lambda h: (0, h))
    return heads, col, vec, mat, cw, head


def _lru_fwd(proj_b, conv_w, conv_b, wa, ba, wx, bx, lam):
    _, s, d = proj_b.shape
    heads, col, vec, mat, cws, head = _lru_specs(s, d)
    tt = LRU_TILE

    def body(xr_ref, xg_ref, cw_ref, cb_ref, wa_ref, ba_ref, wx_ref, bx_ref, lam_ref, y_ref, h_ref, xpad, xc_s):
        xpad[0:8, :] = jnp.zeros((8, HEAD_DIM), F32)
        xpad[8:8 + s, :] = xr_ref[...]
        xc_s[...] = _conv_rows(xpad, cw_ref[...], cb_ref[...], s)
        sp = _softplus(-lam_ref[...])
        wav, wxv, bav, bxv = wa_ref[...], wx_ref[...], ba_ref[...], bx_ref[...]

        def tile(i, hc):
            rows = pl.ds(pl.multiple_of(i * tt, tt), tt)
            xc = xc_s[rows, :]
            _, ig, a, mult = _lru_gates(xc, wav, bav, wxv, bxv, sp)
            pa, hl = _scan_fwd(a, mult * (ig * xc), tt)
            h = hl + pa * hc
            h_ref[rows, :] = h
            gel, _ = _gelu(xg_ref[rows, :])
            y_ref[rows, :] = (h * gel).astype(y_ref.dtype)
            return h[tt - 1:tt, :]

        lax.fori_loop(0, s // tt, tile, jnp.zeros((1, HEAD_DIM), F32))

    return pl.pallas_call(
        body, name="lru_fwd", grid=(heads,),
        in_specs=[col(N_QKV), col(N_QKV + 1), cws, vec, mat, vec, mat, vec, vec],
        out_specs=[head, head],
        out_shape=[jax.ShapeDtypeStruct((s, d), BF16), jax.ShapeDtypeStruct((s, d), F32)],
        scratch_shapes=[pltpu.VMEM((s + 8, HEAD_DIM), F32), pltpu.VMEM((s, HEAD_DIM), F32)],
        compiler_params=_params(32),
    )(proj_b, proj_b, conv_w, conv_b, wa, ba, wx, bx, lam)


def _lru_bwd(proj_b, h_lru, dy, conv_w, conv_b, wa, ba, wx, bx, lam, dproj_b):
    _, s, d = proj_b.shape
    heads, col, vec, mat, cws, head = _lru_specs(s, d)
    tt = LRU_TILE
    n_t = s // tt

    def body(xr_ref, xg_ref, h_ref, dy_ref, cw_ref, cb_ref, wa_ref, ba_ref, wx_ref, bx_ref, lam_ref, alias_ref,
             out_ref, dcw_ref, dcb_ref, dwa_ref, dba_ref, dwx_ref, dbx_ref, dlam_ref, xpad, xc_s, dxc_s):
        del alias_ref
        xpad[0:8, :] = jnp.zeros((8, HEAD_DIM), F32)
        xpad[8:8 + s, :] = xr_ref[...]
        cwv = cw_ref[...]
        xc_s[...] = _conv_rows(xpad, cwv, cb_ref[...], s)
        dxc_s[s:s + 8, :] = jnp.zeros((8, HEAD_DIM), F32)
        lamv = lam_ref[...]
        sp = _softplus(-lamv)
        wav, wxv, bav, bxv = wa_ref[...], wx_ref[...], ba_ref[...], bx_ref[...]
        dwa_ref[...] = jnp.zeros_like(dwa_ref)
        dwx_ref[...] = jnp.zeros_like(dwx_ref)
        zero = jnp.zeros((1, HEAD_DIM), F32)
        row = lax.broadcasted_iota(jnp.int32, (tt, HEAD_DIM), 0)

        def tile(it, carry):
            dh_next, a_next, dba, dbx, dsp, dcb = carry
            i = n_t - 1 - it
            t0 = pl.multiple_of(i * tt, tt)
            rows = pl.ds(t0, tt)
            xc = xc_s[rows, :]
            r, ig, a, mult = _lru_gates(xc, wav, bav, wxv, bxv, sp)
            h = h_ref[rows, :]
            before = h_ref[pl.ds(pl.multiple_of(jnp.maximum(t0 - 8, 0), 8), 8), :][7:8, :]
            before = before * (i > 0).astype(F32)
            h_prev = jnp.where(row == 0, before, pltpu.roll(h, 1, 0))
            xg = xg_ref[rows, :]
            dyv = dy_ref[rows, :]
            gel, th = _gelu(xg)
            out_ref[1, rows, :] = (dyv * h * _gelu_grad(xg, th)).astype(out_ref.dtype)
            b = jnp.where(row == tt - 1, a_next, pltpu.roll(a, tt - 1, 0))
            pb, z = _scan_bwd(b, dyv * gel, tt)
            dh = z + pb * dh_next
            da = dh * h_prev
            dmult = dh * (ig * xc)
            dig = dh * (mult * xc)
            dla = da * a - dmult * (a * a / mult)
            dzr = dla * (-LRU_C * sp) * (r * (1.0 - r))
            dzx = dig * (ig * (1.0 - ig))
            dxc = dh * (mult * ig) + _dot(dzr, wav, NT) + _dot(dzx, wxv, NT)
            dxc_s[rows, :] = dxc
            dwa_ref[...] += _dot(xc, dzr, TN)
            dwx_ref[...] += _dot(xc, dzx, TN)
            return (dh[0:1, :], a[0:1, :],
                    dba + jnp.sum(dzr, axis=0, keepdims=True),
                    dbx + jnp.sum(dzx, axis=0, keepdims=True),
                    dsp + jnp.sum(dla * (-LRU_C * r), axis=0, keepdims=True),
                    dcb + jnp.sum(dxc, axis=0, keepdims=True))

        _, _, dba, dbx, dsp, dcb = lax.fori_loop(0, n_t, tile, (zero, zero, zero, zero, zero, zero))
        dba_ref[...] = dba
        dbx_ref[...] = dbx
        dcb_ref[...] = dcb
        dlam_ref[...] = -dsp * _sigmoid(-lamv)
        dxr = jnp.zeros((s, HEAD_DIM), F32)
        for j in range(CONV_TAPS):
            back = CONV_TAPS - 1 - j
            off = 8 - back
            dcw_ref[j:j + 1, :] = jnp.sum(dxc_s[0:s, :] * xpad[off:off + s, :], axis=0, keepdims=True)
            dxr = dxr + cwv[j:j + 1, :] * dxc_s[back:back + s, :]
        out_ref[0] = dxr.astype(out_ref.dtype)

    return pl.pallas_call(
        body, name="lru_bwd", grid=(heads,),
        in_specs=[col(N_QKV), col(N_QKV + 1), head, head, cws, vec, mat, vec, mat, vec, vec,
                  pl.BlockSpec(memory_space=pl.ANY)],
        out_specs=[pl.BlockSpec((2, s, HEAD_DIM), lambda h: (0, 0, h)),
                   pl.BlockSpec((CONV_TAPS, HEAD_DIM), lambda h: (0, h)), vec, mat, vec, mat, vec, vec],
        out_shape=[jax.ShapeDtypeStruct(dproj_b.shape, dproj_b.dtype),
                   jax.ShapeDtypeStruct((CONV_TAPS, d), F32), jax.ShapeDtypeStruct((1, d), F32),
                   jax.ShapeDtypeStruct(wa.shape, F32), jax.ShapeDtypeStruct((1, d), F32),
                   jax.ShapeDtypeStruct(wx.shape, F32), jax.ShapeDtypeStruct((1, d), F32),
                   jax.ShapeDtypeStruct((1, d), F32)],
        scratch_shapes=[pltpu.VMEM((s + 8, HEAD_DIM), F32), pltpu.VMEM((s, HEAD_DIM), F32),
                        pltpu.VMEM((s + 8, HEAD_DIM), F32)],
        input_output_aliases={11: 0},
        compiler_params=_params(32),
    )(proj_b, proj_b, h_lru, dy, conv_w, conv_b, wa, ba, wx, bx, lam, dproj_b)


def _place():
    x, y, c = (lax.axis_index(n) for n in AXES)
    return x, y, c


def _other_chips(x, y):
    return [(1 - x, y), (x, 1 - y), (1 - x, 1 - y)]


HBM = pl.BlockSpec(memory_space=pl.ANY)


def _cast_shard(w, chip_arr, name):
    r, cols = w.shape
    rh = r // 2
    tr = _row_tile(rh, cols * 4, STREAM_TILE)
    nt = rh // tr

    def body(chip_ref, w_ref, o_ref):
        del chip_ref
        o_ref[...] = w_ref[...].astype(BF16)

    return pl.pallas_call(
        body, name=name,
        grid_spec=pltpu.PrefetchScalarGridSpec(
            num_scalar_prefetch=1, grid=(2, nt),
            in_specs=[pl.BlockSpec((tr, cols), lambda h, i, chip_ref: (h * nt + i, 0))],
            out_specs=pl.BlockSpec((None, None, tr, cols), lambda h, i, chip_ref: (chip_ref[0], h, i, 0))),
        out_shape=jax.ShapeDtypeStruct((N_CHIPS, 2, rh, cols), BF16), compiler_params=_params(32),
    )(chip_arr, w)


HBM_SPEC = pl.BlockSpec(memory_space=pltpu.HBM)
SEM_SPEC = pl.BlockSpec(memory_space=pltpu.SEMAPHORE)
EFFECT = pltpu.SideEffectType.DATAFLOW_SIDE_EFFECTING
TOKEN = jax.ShapeDtypeStruct((8, 128), F32)
TOKEN_SPEC = pl.BlockSpec(memory_space=pltpu.VMEM)


def _in_hbm(arrays):
    return [pltpu.with_memory_space_constraint(a, pltpu.HBM) for a in arrays]


def _hbm_like(arrays):
    return [pltpu.HBM(a.shape, a.dtype) for a in arrays]


def _sems(n):
    return pltpu.SemaphoreType.DMA((n,))


def _remote(src, dst, send_sem, recv_sem, to):
    return pltpu.make_async_remote_copy(src_ref=src, dst_ref=dst, send_sem=send_sem, recv_sem=recv_sem,
                                        device_id=to, device_id_type=MESH)


ALL_FLIPS = (0, 1, 2)


def _gather_start(bufs, groups, after, name):
    n = len(bufs)
    ng = len(groups)

    def body(*refs):
        ins = refs[:n]
        sems = refs[n + len(after):n + len(after) + 2 * ng]
        x, y, c = _place()
        me = 2 * x + y
        chips = _other_chips(x, y)
        for g, (ws, flips) in enumerate(groups):
            for i, w in enumerate(ws):
                for jj, j in enumerate(flips):
                    k = len(flips) * i + jj
                    mine = ins[w].at[me, c]
                    _remote(mine, mine, sems[2 * g].at[k], sems[2 * g + 1].at[k], (*chips[j], c)).start()

    sem_shapes = []
    for ws, flips in groups:
        sem_shapes += [_sems(len(flips) * len(ws))] * 2
    res = pl.pallas_call(
        body, name=name, in_specs=[HBM_SPEC] * n + [HBM] * len(after),
        out_specs=[SEM_SPEC] * (2 * ng) + [HBM_SPEC] * n, out_shape=sem_shapes + _hbm_like(bufs),
        input_output_aliases={w: 2 * ng + w for w in range(n)},
        compiler_params=pltpu.CompilerParams(has_side_effects=EFFECT),
    )(*_in_hbm(bufs), *after)
    return [(res[2 * g], res[2 * g + 1]) for g in range(ng)], list(res[2 * ng:])


def _gather_forward(bufs, recv, after, name, flips=ALL_FLIPS):
    m = len(bufs)
    nf = len(flips)

    def body(*refs):
        ins, recv_in = refs[:m], refs[m]
        fsend, frecv = refs[m + 1 + len(after)], refs[m + 2 + len(after)]
        x, y, c = _place()
        chips = _other_chips(x, y)
        for jj, j in enumerate(flips):
            cx, cy = chips[j]
            for i in range(m):
                landed = ins[i].at[2 * cx + cy, c]
                k = nf * i + jj
                _remote(landed, landed, fsend.at[k], recv_in.at[k], (cx, cy, c)).wait_recv()
                _remote(landed, landed, fsend.at[k], frecv.at[k], (x, y, 1 - c)).start()

    res = pl.pallas_call(
        body, name=name, in_specs=[HBM_SPEC] * m + [SEM_SPEC] + [HBM] * len(after),
        out_specs=[SEM_SPEC, SEM_SPEC] + [HBM_SPEC] * m, out_shape=[_sems(nf * m), _sems(nf * m)] + _hbm_like(bufs),
        input_output_aliases={i: 2 + i for i in range(m)},
        compiler_params=pltpu.CompilerParams(has_side_effects=EFFECT),
    )(*bufs, recv, *after)
    return (res[0], res[1]), list(res[2:])


NEIGHBOURS = (0, 1)


def _relay_partner(x, y, c):
    return 2 * (x ^ (1 - c)) + (y ^ c), (x ^ c, y ^ (1 - c))


def _gather_forward_relay(bufs, recv, after, name):
    m = len(bufs)
    nf = len(NEIGHBOURS)

    def body(*refs):
        ins, recv_in = refs[:m], refs[m]
        fsend, frecv, rsend, rrecv = refs[m + 1 + len(after):m + 5 + len(after)]
        x, y, c = _place()
        chips = _other_chips(x, y)
        for jj, j in enumerate(NEIGHBOURS):
            cx, cy = chips[j]
            for i in range(m):
                landed = ins[i].at[2 * cx + cy, c]
                k = nf * i + jj
                _remote(landed, landed, fsend.at[k], recv_in.at[k], (cx, cy, c)).wait_recv()
        row, (px, py) = _relay_partner(x, y, c)
        for i in range(m):
            relayed = ins[i].at[row, c]
            _remote(relayed, relayed, rsend.at[i], rrecv.at[i], (px, py, c)).start()
        for jj, j in enumerate(NEIGHBOURS):
            cx, cy = chips[j]
            for i in range(m):
                landed = ins[i].at[2 * cx + cy, c]
                k = nf * i + jj
                _remote(landed, landed, fsend.at[k], frecv.at[k], (x, y, 1 - c)).start()

    res = pl.pallas_call(
        body, name=name, in_specs=[HBM_SPEC] * m + [SEM_SPEC] + [HBM] * len(after),
        out_specs=[SEM_SPEC] * 4 + [HBM_SPEC] * m,
        out_shape=[_sems(nf * m), _sems(nf * m), _sems(m), _sems(m)] + _hbm_like(bufs),
        input_output_aliases={i: 4 + i for i in range(m)},
        compiler_params=pltpu.CompilerParams(has_side_effects=EFFECT),
    )(*bufs, recv, *after)
    return tuple(res[:4]), list(res[4:])


def _gather_forward_diag(bufs, rrecv, after, name):
    m = len(bufs)

    def body(*refs):
        ins, rrecv_in = refs[:m], refs[m]
        dsend, drecv = refs[m + 1 + len(after)], refs[m + 2 + len(after)]
        x, y, c = _place()
        diag = (2 * x + y) ^ 3
        _, (px, py) = _relay_partner(x, y, c)
        for i in range(m):
            landed = ins[i].at[diag, c]
            _remote(landed, landed, dsend.at[i], rrecv_in.at[i], (px, py, c)).wait_recv()
            _remote(landed, landed, dsend.at[i], drecv.at[i], (x, y, 1 - c)).start()

    res = pl.pallas_call(
        body, name=name, in_specs=[HBM_SPEC] * m + [SEM_SPEC] + [HBM] * len(after),
        out_specs=[SEM_SPEC, SEM_SPEC] + [HBM_SPEC] * m, out_shape=[_sems(m), _sems(m)] + _hbm_like(bufs),
        input_output_aliases={i: 2 + i for i in range(m)},
        compiler_params=pltpu.CompilerParams(has_side_effects=EFFECT),
    )(*bufs, rrecv, *after)
    return (res[0], res[1]), list(res[2:])


def _gather_finish_diag(bufs, rsend, dsend, drecv, after, name):
    m = len(bufs)

    def body(*refs):
        ins = refs[:m]
        rsend_in, dsend_in, drecv_in = refs[m:m + 3]
        x, y, c = _place()
        diag = (2 * x + y) ^ 3
        row, (px, py) = _relay_partner(x, y, c)
        for i in range(m):
            relayed = ins[i].at[row, c]
            _remote(relayed, relayed, rsend_in.at[i], drecv_in.at[i], (px, py, c)).wait_send()
            landed = ins[i].at[diag, c]
            _remote(landed, landed, dsend_in.at[i], drecv_in.at[i], (x, y, 1 - c)).wait_send()
            theirs = ins[i].at[diag, 1 - c]
            _remote(theirs, theirs, dsend_in.at[i], drecv_in.at[i], (x, y, 1 - c)).wait_recv()

    return list(pl.pallas_call(
        body, name=name, in_specs=[HBM_SPEC] * m + [SEM_SPEC] * 3 + [HBM] * len(after),
        out_specs=[HBM_SPEC] * m, out_shape=_hbm_like(bufs),
        input_output_aliases={i: i for i in range(m)},
        compiler_params=pltpu.CompilerParams(has_side_effects=EFFECT),
    )(*bufs, rsend, dsend, drecv, *after))


def _gather_finish(bufs, send, fsend, frecv, after, name, flips=ALL_FLIPS):
    m = len(bufs)
    nf = len(flips)

    def body(*refs):
        ins = refs[:m]
        send_in, fsend_in, frecv_in = refs[m:m + 3]
        x, y, c = _place()
        me = 2 * x + y
        chips = _other_chips(x, y)
        for jj, j in enumerate(flips):
            cx, cy = chips[j]
            cj = 2 * cx + cy
            for i in range(m):
                k = nf * i + jj
                mine = ins[i].at[me, c]
                _remote(mine, mine, send_in.at[k], frecv_in.at[k], (cx, cy, c)).wait_send()
                landed = ins[i].at[cj, c]
                _remote(landed, landed, fsend_in.at[k], frecv_in.at[k], (x, y, 1 - c)).wait_send()
                theirs = ins[i].at[cj, 1 - c]
                _remote(theirs, theirs, fsend_in.at[k], frecv_in.at[k], (x, y, 1 - c)).wait_recv()

    return list(pl.pallas_call(
        body, name=name, in_specs=[HBM_SPEC] * m + [SEM_SPEC] * 3 + [HBM] * len(after),
        out_specs=[HBM_SPEC] * m, out_shape=_hbm_like(bufs),
        input_output_aliases={i: i for i in range(m)},
        compiler_params=pltpu.CompilerParams(has_side_effects=EFFECT),
    )(*bufs, send, fsend, frecv, *after))


def _pair_exchange(grads, name):
    n = len(grads)

    def body(*refs):
        ins, outs = refs[:n], refs[n:2 * n]
        send_sems, recv_sems = refs[2 * n:]
        x, y, c = _place()
        sibling = (x, y, 1 - c)
        cps = []
        for w in range(n):
            for j in range(N_CHIPS):
                cp = pltpu.make_async_remote_copy(
                    src_ref=ins[w].at[j, 1 - c], dst_ref=outs[w].at[j], send_sem=send_sems.at[N_CHIPS * w + j],
                    recv_sem=recv_sems.at[N_CHIPS * w + j], device_id=sibling, device_id_type=MESH)
                cp.start()
                cps.append(cp)
        for cp in cps:
            cp.wait()

    return pl.pallas_call(
        body, name=name, in_specs=[HBM] * n, out_specs=[HBM] * n,
        out_shape=[jax.ShapeDtypeStruct((N_CHIPS,) + a.shape[2:], a.dtype) for a in grads],
        scratch_shapes=[pltpu.SemaphoreType.DMA((N_CHIPS * n,)), pltpu.SemaphoreType.DMA((N_CHIPS * n,))],
    )(*grads)


def _chip_start(sums, name, to_all=()):
    m = len(sums)
    lands = [lax.empty(((N_CHIPS,) if i in to_all else ()) + a.shape, a.dtype) for i, a in enumerate(sums)]

    def body(*refs):
        ins, land_in = refs[:m], refs[m:2 * m]
        send, recv = refs[2 * m], refs[2 * m + 1]
        token = refs[4 * m + 2]
        x, y, c = _place()
        me = 2 * x + y
        for i in sorted(range(m), key=lambda i: i not in to_all):
            for j, (cx, cy) in enumerate(_other_chips(x, y)):
                src = ins[i] if i in to_all else ins[i].at[2 * cx + cy]
                _remote(src, land_in[i].at[me], send.at[3 * i + j], recv.at[3 * i + j], (cx, cy, c)).start()
        token[...] = jnp.zeros_like(token)

    res = pl.pallas_call(
        body, name=name, in_specs=[HBM_SPEC] * (2 * m),
        out_specs=[SEM_SPEC, SEM_SPEC] + [HBM_SPEC] * (2 * m) + [TOKEN_SPEC],
        out_shape=[_sems(3 * m), _sems(3 * m)] + _hbm_like(sums) + _hbm_like(lands) + [TOKEN],
        input_output_aliases={i: 2 + i for i in range(2 * m)},
        compiler_params=pltpu.CompilerParams(has_side_effects=EFFECT),
    )(*_in_hbm(sums), *_in_hbm(lands))
    return (res[0], res[1]), list(res[2:2 + m]), list(res[2 + m:2 + 2 * m]), res[2 + 2 * m]


def _chip_wait(sems, sums, lands, after, name, to_all=()):
    m = len(sums)

    def body(*refs):
        ins, land_in = refs[:m], refs[m:2 * m]
        send_in, recv_in = refs[2 * m], refs[2 * m + 1]
        x, y, c = _place()
        for i in range(m):
            for j, (cx, cy) in enumerate(_other_chips(x, y)):
                cj = 2 * cx + cy
                src = ins[i] if i in to_all else ins[i].at[cj]
                cp = _remote(src, land_in[i].at[cj], send_in.at[3 * i + j], recv_in.at[3 * i + j], (cx, cy, c))
                cp.wait_send()
                cp.wait_recv()

    res = pl.pallas_call(
        body, name=name, in_specs=[HBM_SPEC] * (2 * m) + [SEM_SPEC, SEM_SPEC] + [HBM] * len(after),
        out_specs=[HBM_SPEC] * (2 * m), out_shape=_hbm_like(sums) + _hbm_like(lands),
        input_output_aliases={i: i for i in range(2 * m)},
        compiler_params=pltpu.CompilerParams(has_side_effects=EFFECT),
    )(*sums, *lands, sems[0], sems[1], *after)
    return list(res[:m]), list(res[m:])


def _half_exchange(bufs, name):
    n = len(bufs)
    parts = []
    for w, a in enumerate(bufs):
        parts += [(w, None)] if a.ndim == 3 else [(w, j) for j in range(a.shape[0])]
    np_ = len(parts)

    def body(*refs):
        outs = refs[n:2 * n]
        send_sems, recv_sems = refs[2 * n:]
        x, y, c = _place()
        sibling = (x, y, 1 - c)

        def half(w, j, h):
            return outs[w].at[h] if j is None else outs[w].at[j, h]

        cps = []
        for k, (w, j) in enumerate(parts):
            rc = _remote(half(w, j, c), half(w, j, c), send_sems.at[k], recv_sems.at[k], sibling)
            rc.start()
            cps.append(rc)
        for k, (w, j) in enumerate(parts):
            theirs = half(w, j, 1 - c)
            _remote(theirs, theirs, send_sems.at[k], recv_sems.at[k], sibling).wait_recv()
        for cp in cps:
            cp.wait_send()

    return pl.pallas_call(
        body, name=name, in_specs=[HBM] * n, out_specs=[HBM] * n,
        out_shape=[jax.ShapeDtypeStruct(a.shape, a.dtype) for a in bufs],
        input_output_aliases={w: w for w in range(n)},
        scratch_shapes=[_sems(np_), _sems(np_)],
    )(*bufs)


def _all_gather8(block, name, after=()):
    def body(in_ref, *rest):
        out_ref, send_sems, recv_sems, local_sem = rest[len(after):]
        x, y, c = _place()
        me = 4 * x + 2 * y + c
        mine = pltpu.make_async_copy(in_ref, out_ref.at[me], local_sem)
        mine.start()
        flips = [(fx, fy, fc) for fx in (0, 1) for fy in (0, 1) for fc in (0, 1)][1:]
        cps = []
        for k, (fx, fy, fc) in enumerate(flips):
            cp = pltpu.make_async_remote_copy(
                src_ref=in_ref, dst_ref=out_ref.at[me], send_sem=send_sems.at[k], recv_sem=recv_sems.at[k],
                device_id=(x ^ fx, y ^ fy, c ^ fc), device_id_type=MESH)
            cp.start()
            cps.append(cp)
        for k, (fx, fy, fc) in enumerate(flips):
            px, py, pc = x ^ fx, y ^ fy, c ^ fc
            theirs = out_ref.at[4 * px + 2 * py + pc]
            pltpu.make_async_remote_copy(
                src_ref=theirs, dst_ref=theirs, send_sem=send_sems.at[k], recv_sem=recv_sems.at[k],
                device_id=(px, py, pc), device_id_type=MESH).wait_recv()
        for cp in cps:
            cp.wait_send()
        mine.wait()

    return pl.pallas_call(
        body, name=name, in_specs=[HBM] * (1 + len(after)), out_specs=HBM,
        out_shape=jax.ShapeDtypeStruct((N_DEV,) + block.shape, block.dtype),
        scratch_shapes=[pltpu.SemaphoreType.DMA((N_DEV - 1,)), pltpu.SemaphoreType.DMA((N_DEV - 1,)),
                        pltpu.SemaphoreType.DMA],
    )(block, *after)


def _pair_sum(grad, recv, c_arr, name):
    _, _, rh, cols = grad.shape
    tr = _row_tile(rh, cols * grad.dtype.itemsize, STREAM_TILE // 2)

    def body(c_ref, g_ref, r_ref, o_ref):
        del c_ref
        o_ref[...] = (g_ref[...].astype(F32) + r_ref[...].astype(F32)).astype(o_ref.dtype)

    spec = pl.BlockSpec((None, tr, cols), lambda j, i, c_ref: (j, i, 0))
    return pl.pallas_call(
        body, name=name,
        grid_spec=pltpu.PrefetchScalarGridSpec(
            num_scalar_prefetch=1, grid=(N_CHIPS, rh // tr),
            in_specs=[pl.BlockSpec((None, None, tr, cols), lambda j, i, c_ref: (j, c_ref[0], i, 0)), spec],
            out_specs=spec),
        out_shape=jax.ShapeDtypeStruct(recv.shape, grad.dtype), compiler_params=_params(32),
    )(c_arr, grad, recv)


def _sum_by_chip(chip, p_ref, own_ref, o_ref):
    o_ref[...] = jnp.zeros_like(o_ref)
    for k in range(N_CHIPS):
        @pl.when(chip == k)
        def _():
            o_ref[...] += own_ref[...].astype(F32)

        @pl.when(chip != k)
        def _(k=k):
            o_ref[...] += p_ref[k].astype(F32)


def _chip_sum_all(parts, own, place_arr, name):
    _, nj, rh, cols = parts.shape
    tr = _row_tile(rh, cols * 4, STREAM_TILE // 2)

    def body(place_ref, p_ref, own_ref, o_ref):
        _sum_by_chip(place_ref[0], p_ref, own_ref, o_ref)

    return pl.pallas_call(
        body, name=name,
        grid_spec=pltpu.PrefetchScalarGridSpec(
            num_scalar_prefetch=1, grid=(nj, rh // tr),
            in_specs=[pl.BlockSpec((N_CHIPS, None, tr, cols), lambda j, i, place_ref: (0, j, i, 0)),
                      pl.BlockSpec((None, tr, cols), lambda j, i, place_ref: (j, i, 0))],
            out_specs=pl.BlockSpec((None, None, tr, cols), lambda j, i, place_ref: (j, place_ref[1], i, 0))),
        out_shape=jax.ShapeDtypeStruct((nj, 2, rh, cols), F32), compiler_params=_params(32),
    )(place_arr, parts, own)


def _chip_sum(parts, own, place_arr, name):
    _, rh, cols = parts.shape
    tr = _row_tile(rh, cols * 4, STREAM_TILE // 2)

    def body(place_ref, p_ref, own_ref, o_ref):
        _sum_by_chip(place_ref[0], p_ref, own_ref, o_ref)

    return pl.pallas_call(
        body, name=name,
        grid_spec=pltpu.PrefetchScalarGridSpec(
            num_scalar_prefetch=1, grid=(rh // tr,),
            in_specs=[pl.BlockSpec((N_CHIPS, tr, cols), lambda i, place_ref: (0, i, 0)),
                      pl.BlockSpec((None, tr, cols), lambda i, place_ref: (place_ref[0], i, 0))],
            out_specs=pl.BlockSpec((None, tr, cols), lambda i, place_ref: (place_ref[1], i, 0))),
        out_shape=jax.ShapeDtypeStruct((2, rh, cols), F32), compiler_params=_params(32),
    )(place_arr, parts, own)


def _adamw_math(w, g, m, v):
    m = ADAM_B1 * m + (1.0 - ADAM_B1) * g
    v = ADAM_B2 * v + (1.0 - ADAM_B2) * (g * g)
    m_hat = m / (1.0 - ADAM_B1 ** ADAM_STEP)
    v_hat = v / (1.0 - ADAM_B2 ** ADAM_STEP)
    delta = -ADAM_LR * (m_hat / (jnp.sqrt(v_hat) + ADAM_EPS) + ADAM_WD * w)
    return delta, m, v


def _adamw(w, g, m, v, name):
    rows, cols = w.shape
    tr = _row_tile(rows, cols * 4)

    def body(w_ref, g_ref, m_ref, v_ref, go_ref, d_ref, nm_ref, nv_ref):
        gv = g_ref[...]
        go_ref[...] = gv
        d_ref[...], nm_ref[...], nv_ref[...] = _adamw_math(w_ref[...], gv, m_ref[...], v_ref[...])

    spec = pl.BlockSpec((tr, cols), lambda i: (i, 0))
    return pl.pallas_call(
        body, name=name, grid=(rows // tr,), in_specs=[spec] * 4, out_specs=[spec] * 4,
        out_shape=[jax.ShapeDtypeStruct(w.shape, F32)] * 4, compiler_params=_params(32),
    )(w, g, m, v)


def _sum8_adamw_row(parts, w, m, v, name):
    cols = parts.shape[2]

    def body(p_ref, w_ref, m_ref, v_ref, g_ref, d_ref, nm_ref, nv_ref):
        g = p_ref[0, 0:1, :]
        for k in range(1, N_DEV):
            g = g + p_ref[k, 0:1, :]
        g_ref[...] = g
        d_ref[...], nm_ref[...], nv_ref[...] = _adamw_math(w_ref[...], g, m_ref[...], v_ref[...])

    return pl.pallas_call(
        body, name=name, out_shape=[jax.ShapeDtypeStruct((1, cols), F32)] * 4, compiler_params=_params(32),
    )(parts, w, m, v)


def _pack_rows(pieces, rows, name):
    cols = pieces[0].shape[1]
    n = len(pieces)

    def body(*refs):
        o_ref = refs[n]
        o_ref[...] = jnp.zeros_like(o_ref)
        at = 0
        for p_ref in refs[:n]:
            r = p_ref.shape[0]
            o_ref[at:at + r, :] = p_ref[...]
            at += r

    return pl.pallas_call(
        body, name=name, out_shape=jax.ShapeDtypeStruct((rows, cols), F32), compiler_params=_params(32),
    )(*pieces)


def kernel(x, norm_mix_g, w_in, conv_w, conv_b, lru_wa, lru_ba, lru_wx, lru_bx, lru_lambda, w_proj_attn, w_proj_lru, w_out, norm_mlp_g, w_up, w_down, norm_final_g, loss_target, m_norm_mix_g, m_w_in, m_conv_w, m_conv_b, m_lru_wa, m_lru_ba, m_lru_wx, m_lru_bx, m_lru_lambda, m_w_proj_attn, m_w_proj_lru, m_w_out, m_norm_mlp_g, m_w_up, m_w_down, m_norm_final_g, v_norm_mix_g, v_w_in, v_conv_w, v_conv_b, v_lru_wa, v_lru_ba, v_lru_wx, v_lru_bx, v_lru_lambda, v_w_proj_attn, v_w_proj_lru, v_w_out, v_norm_mlp_g, v_w_up, v_w_down, v_norm_final_g):
    s, d = x.shape[1], x.shape[2]
    ff = w_up.shape[2] * N_CHIPS
    heads = d // HEAD_DIM
    u = d // 4
    assert s % (max(DILATIONS) * ATTN_BLK) == 0 and d % (4 * HEAD_DIM) == 0 and ff == 4 * d and DILATIONS[0] == 1
    xs, target = _in_hbm([x[0], loss_target[0]])
    gf = norm_final_g.reshape(1, d)
    wa, wx = lru_wa[0], lru_wx[0]
    core = lax.axis_index("c").astype(jnp.int32)
    chip = (2 * lax.axis_index("x") + lax.axis_index("y")).astype(jnp.int32)
    cidx = core.reshape(1)
    chip_arr = chip.reshape(1)
    place_arr = jnp.stack([chip, core])
    slopes = jnp.broadcast_to(
        (2.0 ** (-8.0 * jnp.arange(1, heads + 1, dtype=F32) / heads))[:, None, None], (heads, 1, HEAD_DIM))

    big = _in_hbm([w_in[0], w_proj_attn[0], w_proj_lru[0], w_out[0], w_up[0], w_down[0]])
    names = ["w_in", "w_proj_attn", "w_proj_lru", "w_out", "w_up", "w_down"]
    cw_pad = jnp.pad(conv_w[0], ((0, 8 - CONV_TAPS), (0, 0)))
    cw_all = _all_gather8(cw_pad, "gather_conv_w")
    conv_w_full = jnp.concatenate([cw_all[2 * j] for j in range(N_CHIPS)], axis=1)
    (sem_a,), buf_a = _gather_start([_cast_shard(big[0], chip_arr, "cast_w_in")], [([0], NEIGHBOURS)], [cw_all],
                                    "gather_start_w_in")

    xn = _rms_fwd(xs, norm_mix_g, "norm_mix")
    flip_bits = (2, 1, 3)

    def w_in_view():
        return buf_a[0].reshape(N_CHIPS, d, N_SLOTS * u)

    proj = _proj_in_shard(xn, w_in_view(), chip_arr, None, "proj_in_own")
    (fs, fr, rs, rr), buf_a = _gather_forward_relay(buf_a, sem_a[1], [proj], "gather_forward_w_in")
    bufs = [_cast_shard(w, chip_arr, "cast_" + nm) for w, nm in zip(big[1:], names[1:])]
    (sem_b, sem_c, sem_d), bufs = _gather_start(
        bufs, [([0, 1, 2], ALL_FLIPS), ([3], ALL_FLIPS), ([4], ALL_FLIPS)], buf_a, "gather_start_rest")
    buf_a = _gather_finish(buf_a, sem_a[0], fs, fr, [bufs[0]], "gather_finish_w_in", flips=NEIGHBOURS)
    for j in NEIGHBOURS:
        proj = _proj_in_shard(xn, w_in_view(), chip_arr ^ flip_bits[j], proj, "proj_in_from_%d" % j)
    (ds, dr), buf_a = _gather_forward_diag(buf_a, rr, [proj], "gather_forward_w_in_diag")
    buf_a = _gather_finish_diag(buf_a, rs, ds, dr, [], "gather_finish_w_in_diag")
    proj = _proj_in_shard(xn, w_in_view(), chip_arr ^ flip_bits[2], proj, "proj_in_from_2")
    w_in_g = w_in_view()
    proj_a = proj_b = proj
    y_attn, lse = _attn_fwd(proj_a, slopes)
    y_lru, h_lru = _lru_fwd(proj_b, conv_w_full, conv_b, wa, lru_ba, wx, lru_bx, lru_lambda)
    fsem_b, buf_b = _gather_forward(bufs[:3], sem_b[1], [y_attn, y_lru], "gather_forward_proj")
    fsem_c, buf_c = _gather_forward(bufs[3:4], sem_c[1], [buf_b[0]], "gather_forward_w_up")
    buf_b = _gather_finish(buf_b, sem_b[0], fsem_b[0], fsem_b[1], [buf_c[0]], "gather_finish_proj")
    wpa_g = buf_b[0].reshape(d, d)
    wpl_g = buf_b[1].reshape(d, d)
    wout_g = buf_b[2].reshape(d, d)

    tn = u
    sd_f32 = jax.ShapeDtypeStruct((s, d), F32)
    sd_bf16 = jax.ShapeDtypeStruct((s, d), BF16)
    col = pl.BlockSpec((s, tn), lambda i, j, k: (0, j))

    def slot(n):
        return pl.BlockSpec((None, s, tn), lambda i, j, k: (n, 0, j))

    p_attn = _mm_nn("proj_attn", y_attn, wpa_g, [], [], [sd_f32], [col], _store, tn)[0]

    def merge(acc, extras, outs):
        pa_ref, ga_ref, gl_ref = extras
        merged = _sigmoid(ga_ref[...]) * pa_ref[...] + _sigmoid(gl_ref[...]) * acc
        outs[0][...] = merged.astype(BF16)
        outs[1][...] = acc

    tn2 = max(HEAD_DIM, u // 2)
    col2 = pl.BlockSpec((s, tn2), lambda i, j, k: (0, j))

    def slot2(n):
        return pl.BlockSpec((None, s, tn2), lambda i, j, k: (n, 0, j))

    merged, p_lru = _mm_nn("proj_lru_merge", y_lru, wpl_g, [p_attn, proj, proj], [col2, slot2(5), slot2(6)],
                           [sd_bf16, sd_f32], [col2, col2], merge, tn2)

    def add_resid(acc, extras, outs):
        outs[0][...] = extras[0][...] + acc

    h1 = _mm_nn("w_out_resid", merged, wout_g, [xs], [col], [sd_f32], [col], add_resid, tn)[0]
    hn = _rms_fwd(h1, norm_mlp_g, "norm_mlp")
    fsem_d, buf_d = _gather_forward(bufs[4:], sem_d[1], [hn], "gather_forward_w_down")
    buf_c = _gather_finish(buf_c, sem_c[0], fsem_c[0], fsem_c[1], [buf_d[0]], "gather_finish_w_up")
    wup_g = buf_c[0].reshape(N_CHIPS, d, d)

    def relu_sq(acc, extras, outs):
        r = jnp.maximum(acc, 0.0)
        outs[0][...] = (r * r).astype(BF16)
        outs[1][...] = r.astype(BF16)

    sf_bf16 = jax.ShapeDtypeStruct((s, ff), BF16)
    hid, relu_up = _mm(
        "w_up_relu2", [hn, wup_g],
        [pl.BlockSpec((s, d), lambda i, j, k: (0, 0)),
         pl.BlockSpec((None, d, tn), lambda i, j, k: (j // 4, 0, j % 4))],
        [sf_bf16, sf_bf16], [col, col], (1, ff // tn, 1), NN, relu_sq)
    wdown_g = _gather_finish(buf_d, sem_d[0], fsem_d[0], fsem_d[1], [hid], "gather_finish_w_down")[0].reshape(ff, d)
    h2 = _mm(
        "w_down_resid", [hid, wdown_g, h1],
        [pl.BlockSpec((s, d), lambda i, j, k: (0, k)), pl.BlockSpec((d, tn), lambda i, j, k: (k, j)), col],
        [sd_f32], [col], (1, d // tn, ff // d), NN, add_resid, nk=ff // d, acc_shape=(s, tn))[0]
    loss_part, dh2, dh2_b, d_gf = _loss_head(h2, gf, target)
    loss = lax.psum(loss_part[0, 0], AXES)

    def relu_sq_bwd(acc, extras, outs):
        outs[0][...] = (acc * (2.0 * extras[0][...].astype(F32))).astype(BF16)

    dup = _mm_nt("d_hid", dh2_b, wdown_g, [relu_up], [col], [sf_bf16], [col], relu_sq_bwd, tn)[0]
    tok_d = pl.BlockSpec((s, d), lambda i, j: (0, 0))
    g_wdown = _mm_tn(
        "g_w_down", hid, dh2_b, pl.BlockSpec((s, d), lambda i, j: (0, i)),
        pl.BlockSpec((s, tn), lambda i, j: (0, j)), jax.ShapeDtypeStruct((ff, d), BF16),
        pl.BlockSpec((d, tn), lambda i, j: (i, j)), (ff // d, d // tn), d, tn, s)
    dhn = _mm(
        "d_hn", [dup, wup_g],
        [pl.BlockSpec((s, d), lambda i, j, k: (0, k)), pl.BlockSpec((None, tn, d), lambda i, j, k: (k, j, 0))],
        [sd_f32], [col], (1, d // tn, ff // d), NT, _store, nk=ff // d, acc_shape=(s, tn))[0]
    g_wup = _mm_tn(
        "g_w_up", hn, dup, tok_d, pl.BlockSpec((s, tn), lambda i, j: (0, j)),
        jax.ShapeDtypeStruct((N_CHIPS, d, d), BF16), pl.BlockSpec((None, d, tn), lambda i, j: (j // 4, 0, j % 4)),
        (1, ff // tn), d, tn, s)
    big_m = _in_hbm([m_w_in[0], m_w_proj_attn[0], m_w_proj_lru[0], m_w_out[0], m_w_up[0], m_w_down[0]])
    big_v = _in_hbm([v_w_in[0], v_w_proj_attn[0], v_w_proj_lru[0], v_w_out[0], v_w_up[0], v_w_down[0]])
    big_out = {}

    def reduce_begin(ids, gs, tag, everywhere=None):
        g4 = [g.reshape(N_CHIPS, 2, big[i].shape[0] // 2, big[i].shape[1]) for i, g in zip(ids, gs)]
        tags = [names[i] for i in ids]
        if everywhere is not None:
            g4.append(everywhere.reshape(N_CHIPS, 2, everywhere.shape[0] // (2 * N_CHIPS), everywhere.shape[1]))
            tags.append("small_" + tag)
        from_sibling = _pair_exchange(g4, "pair_exchange_" + tag)
        sums = [_pair_sum(g, r, cidx, "pair_sum_" + t) for t, g, r in zip(tags, g4, from_sibling)]
        to_all = () if everywhere is None else (len(ids),)
        return _chip_start(sums, "chip_start_" + tag, to_all), to_all

    def reduce_end(ids, begun, after, tag):
        (sems, sums, lands, _), to_all = begun
        sums, lands = _chip_wait(sems, sums, lands, after, "chip_wait_" + tag, to_all)
        halves = [_chip_sum(p, own, place_arr, "chip_sum_" + names[i]) for i, p, own in zip(ids, lands, sums)]
        if to_all:
            halves.append(_chip_sum_all(lands[-1], sums[-1], place_arr, "chip_sum_small_" + tag))
        full = _half_exchange(halves, "half_exchange_" + tag)
        last = None
        for i, g in zip(ids, full):
            res = _adamw(big[i], g.reshape(big[i].shape), big_m[i], big_v[i], "adamw_" + names[i])
            big_out[names[i]] = tuple(a[None] for a in res)
            last = res[1]
        everywhere = full[-1].reshape(-1, full[-1].shape[-1]) if to_all else None
        return everywhere, last

    def after_token(a, begun):
        return a + begun[0][3][:1, :1]

    red_mlp = reduce_begin([4, 5], [g_wup, g_wdown], "mlp")
    dh1, dh1_b, d_gmlp = _rms_bwd(h1, after_token(norm_mlp_g, red_mlp), dhn, dh2, "norm_mlp_bwd")

    g_wout = _mm_tn(
        "g_w_out", merged, dh1_b, tok_d, pl.BlockSpec((s, tn), lambda i, j: (0, j)),
        jax.ShapeDtypeStruct((d, d), BF16), pl.BlockSpec((d, tn), lambda i, j: (0, j)), (1, d // tn), d, tn, s)

    def merge_bwd(acc, extras, outs):
        pa_ref, pl_ref, ga_ref, gl_ref = extras
        sa, sl = _sigmoid(ga_ref[...]), _sigmoid(gl_ref[...])
        outs[0][...] = (acc * sa).astype(BF16)
        outs[1][...] = (acc * sl).astype(BF16)
        outs[2][0] = (acc * pa_ref[...] * (sa * (1.0 - sa))).astype(BF16)
        outs[2][1] = (acc * pl_ref[...] * (sl * (1.0 - sl))).astype(BF16)

    nb = N_SLOTS - N_QKV
    d_pa, d_pl, dproj_b = _mm_nt(
        "d_merged", dh1_b, wout_g, [p_attn, p_lru, proj, proj], [col2, col2, slot2(5), slot2(6)],
        [sd_bf16, sd_bf16, jax.ShapeDtypeStruct((nb, s, d), BF16)],
        [col2, col2, pl.BlockSpec((2, s, tn2), lambda i, j, k: (1, 0, j))], merge_bwd, tn2)
    dy_attn = _mm_nt("d_y_attn", d_pa, wpa_g, [], [], [sd_f32], [col], _store, tn)[0]
    dy_lru = _mm_nt("d_y_lru", d_pl, wpl_g, [], [], [sd_f32], [col], _store, tn)[0]
    g_wpa = _mm_tn(
        "g_w_proj_attn", y_attn, d_pa, tok_d, pl.BlockSpec((s, tn), lambda i, j: (0, j)),
        jax.ShapeDtypeStruct((d, d), BF16), pl.BlockSpec((d, tn), lambda i, j: (0, j)), (1, d // tn), d, tn, s)
    g_wpl = _mm_tn(
        "g_w_proj_lru", y_lru, d_pl, tok_d, pl.BlockSpec((s, tn), lambda i, j: (0, j)),
        jax.ShapeDtypeStruct((d, d), BF16), pl.BlockSpec((d, tn), lambda i, j: (0, j)), (1, d // tn), d, tn, s)

    red_proj = reduce_begin([1, 2, 3], [g_wpa, g_wpl, g_wout], "proj")

    dproj_a = _attn_bwd(proj_a, after_token(slopes, red_proj), y_attn, lse, dy_attn)
    dproj_b, d_cw, d_cb, d_wa, d_ba, d_wx, d_bx, d_lam = _lru_bwd(
        proj_b, h_lru, dy_lru, conv_w_full, conv_b, wa, lru_ba, wx, lru_bx, after_token(lru_lambda, red_proj),
        dproj_b)
    per = N_SLOTS
    g_win_shape = jax.ShapeDtypeStruct((N_CHIPS, d, N_SLOTS * u), BF16)

    def g_win_part(name, dproj, first, prev):
        n_units = 4 * dproj.shape[0]
        return _mm_tn(
            name, xn, dproj, tok_d, pl.BlockSpec((None, s, u), lambda i, j: (j // 4, 0, j % 4)),
            g_win_shape, pl.BlockSpec((None, d, u), lambda i, j: ((j + first) // per, 0, (j + first) % per)),
            (1, n_units), d, u, s, aliases=None if prev is None else {2: 0}, extra=prev)

    mat_rows = heads * HEAD_DIM * HEAD_DIM // d
    vec_names = ["norm_mix_g", "conv_b", "lru_ba", "lru_bx", "lru_lambda", "norm_mlp_g", "norm_final_g"]

    def pack(wa_, wx_, cw_, vecs, name):
        rows = [wa_.reshape(mat_rows, d), wx_.reshape(mat_rows, d), cw_] + [a.reshape(1, d) for a in vecs]
        n = sum(a.shape[0] for a in rows)
        return _pack_rows(rows, n + (-n % 64), name)

    zero_cw = jnp.zeros((CONV_TAPS, d), F32)
    small_g = pack(d_wa, d_wx, d_cw, [jnp.zeros((1, d), F32), d_cb, d_ba, d_bx, d_lam, d_gmlp, d_gf], "pack_small_g")
    g_win = g_win_part("g_w_in_qkv", dproj_a, 0, None)
    g_win = g_win_part("g_w_in_rest", dproj_b, 4 * N_QKV, g_win)
    red_in = reduce_begin([0], [g_win], "w_in", everywhere=small_g)
    dxn = _dxn(dproj_a, dproj_b, w_in_g, 2 * tn, [red_in[0][3]])
    grad_x, _, d_gmix = _rms_bwd(xs, norm_mix_g, dxn, dh1, "norm_mix_bwd")

    _, done = reduce_end([4, 5], red_mlp, [grad_x], "mlp")
    _, done = reduce_end([1, 2, 3], red_proj, [done], "proj")
    small_w = pack(wa, wx, zero_cw, [norm_mix_g, conv_b, lru_ba, lru_bx, lru_lambda, norm_mlp_g, norm_final_g],
                   "pack_small_w")
    small_m = pack(m_lru_wa[0], m_lru_wx[0], zero_cw,
                   [m_norm_mix_g, m_conv_b, m_lru_ba, m_lru_bx, m_lru_lambda, m_norm_mlp_g, m_norm_final_g],
                   "pack_small_m")
    small_v = pack(v_lru_wa[0], v_lru_wx[0], zero_cw,
                   [v_norm_mix_g, v_conv_b, v_lru_ba, v_lru_bx, v_lru_lambda, v_norm_mlp_g, v_norm_final_g],
                   "pack_small_v")
    small_sum, _ = reduce_end([0], red_in, [done, small_w, small_m, small_v], "w_in")
    small = _adamw(small_w, small_sum, small_m, small_v, "adamw_small")
    g_cw = lax.dynamic_slice(small_sum[2 * mat_rows:2 * mat_rows + CONV_TAPS], (0, chip * u), (CONV_TAPS, u))
    cw_out = _adamw(conv_w[0], g_cw, m_conv_w[0], v_conv_w[0], "adamw_conv_w")
    gmix_parts = _all_gather8(jnp.pad(d_gmix, ((0, 7), (0, 0))), "gather_gain_grad", [small[1]])
    gmix_out = _sum8_adamw_row(gmix_parts, norm_mix_g, m_norm_mix_g, v_norm_mix_g, "sum_adamw_norm_mix_g")

    def small_leaf(kind, name):
        a = small[kind]
        if name == "norm_mix_g":
            return gmix_out[kind]
        if name == "lru_wa":
            return a[0:mat_rows].reshape(lru_wa.shape)
        if name == "lru_wx":
            return a[mat_rows:2 * mat_rows].reshape(lru_wx.shape)
        if name == "conv_w":
            return cw_out[kind][None]
        row = a[2 * mat_rows + CONV_TAPS + vec_names.index(name)]
        return row if name == "norm_final_g" else row[None]

    order = ["norm_mix_g", "w_in", "conv_w", "conv_b", "lru_wa", "lru_ba", "lru_wx", "lru_bx", "lru_lambda",
             "w_proj_attn", "w_proj_lru", "w_out", "norm_mlp_g", "w_up", "w_down", "norm_final_g"]
    outs = [loss, grad_x[None]]
    for kind in range(4):
        for name in order:
            outs.append(big_out[name][kind] if name in big_out else small_leaf(kind, name))
    return tuple(outs)
```

```python
import functools

import jax
import jax.numpy as jnp
from jax import lax
from jax.experimental import pallas as pl
from jax.experimental.pallas import tpu as pltpu

F32 = jnp.float32
BF16 = jnp.bfloat16
MESH = pl.DeviceIdType.MESH
AXES = ("x", "y", "c")

N_CHIPS = 4
N_DEV = 8
HEAD_DIM = 128
ATTN_BLK = 128
DILATIONS = (1, 4, 16)
ATTN_UNROLL = 8
CONV_TAPS = 4
LRU_C = 8.0
EPS = 1e-6
N_SLOTS = 7
N_QKV = 3
VMEM_MIB = 2 ** 20
VMEM_V7X = 64 * VMEM_MIB
STREAM_TILE = 4 * VMEM_MIB

ADAM_LR = 0.001
ADAM_B1 = 0.9
ADAM_B2 = 0.999
ADAM_EPS = 1e-08
ADAM_WD = 0.01
ADAM_STEP = 10

NN = (((1,), (0,)), ((), ()))
NT = (((1,), (1,)), ((), ()))
TN = (((0,), (0,)), ((), ()))


def _params(vmem_mib=None, **kw):
    limit = None if vmem_mib is None else min(vmem_mib * VMEM_MIB, VMEM_V7X - 8 * VMEM_MIB)
    return pltpu.CompilerParams(vmem_limit_bytes=limit, **kw)


def _row_tile(rows, row_bytes, budget=VMEM_MIB):
    t = rows
    while t % 16 == 0 and t * row_bytes > budget:
        t //= 2
    return t


def _dot(a, b, dims):
    return lax.dot_general(a.astype(BF16), b.astype(BF16), dims, preferred_element_type=F32)


def _sigmoid(x):
    return jax.nn.sigmoid(x)


def _rms_fwd(x, g, name):
    s, d = x.shape
    tm = _row_tile(s, d * 4)

    def body(x_ref, g_ref, o_ref):
        xf = x_ref[...]
        r = lax.rsqrt(jnp.mean(xf * xf, axis=-1, keepdims=True) + EPS)
        o_ref[...] = (xf * r * g_ref[...]).astype(o_ref.dtype)

    return pl.pallas_call(
        body, name=name, grid=(s // tm,),
        in_specs=[pl.BlockSpec((tm, d), lambda i: (i, 0)), pl.BlockSpec((1, d), lambda i: (0, 0))],
        out_specs=pl.BlockSpec((tm, d), lambda i: (i, 0)),
        out_shape=jax.ShapeDtypeStruct((s, d), BF16), compiler_params=_params(32),
    )(x, g)


def _rms_bwd(x, g, dy, resid, name):
    s, d = x.shape
    tm = _row_tile(s, d * 4)

    def body(x_ref, g_ref, dy_ref, res_ref, dx_ref, dxb_ref, dg_ref):
        xf = x_ref[...]
        r = lax.rsqrt(jnp.mean(xf * xf, axis=-1, keepdims=True) + EPS)
        xh = xf * r
        dyv = dy_ref[...]
        dxh = dyv * g_ref[...]
        dx = r * (dxh - xh * jnp.mean(dxh * xh, axis=-1, keepdims=True)) + res_ref[...]
        dx_ref[...] = dx
        dxb_ref[...] = dx.astype(BF16)
        part = jnp.sum(dyv * xh, axis=0, keepdims=True)

        @pl.when(pl.program_id(0) == 0)
        def _():
            dg_ref[...] = part

        @pl.when(pl.program_id(0) > 0)
        def _():
            dg_ref[...] += part

    row = pl.BlockSpec((tm, d), lambda i: (i, 0))
    vec = pl.BlockSpec((1, d), lambda i: (0, 0))
    return pl.pallas_call(
        body, name=name, grid=(s // tm,),
        in_specs=[row, vec, row, row], out_specs=[row, row, vec],
        out_shape=[jax.ShapeDtypeStruct((s, d), F32), jax.ShapeDtypeStruct((s, d), BF16),
                   jax.ShapeDtypeStruct((1, d), F32)],
        compiler_params=_params(32),
    )(x, g, dy, resid)


def _loss_head(h2, g, target):
    s, d = h2.shape
    tm = _row_tile(s, d * 4)

    def body(x_ref, g_ref, t_ref, loss_ref, dx_ref, dxb_ref, dg_ref):
        xf = x_ref[...]
        gv = g_ref[...]
        r = lax.rsqrt(jnp.mean(xf * xf, axis=-1, keepdims=True) + EPS)
        xh = xf * r
        err = xh * gv - t_ref[...]
        part = jnp.sum(jnp.sum(err * err, axis=1, keepdims=True), axis=0, keepdims=True) * (0.5 / d)
        dyv = err * (1.0 / d)
        dxh = dyv * gv
        dx = r * (dxh - xh * jnp.mean(dxh * xh, axis=-1, keepdims=True))
        dx_ref[...] = dx
        dxb_ref[...] = dx.astype(BF16)
        dgp = jnp.sum(dyv * xh, axis=0, keepdims=True)

        @pl.when(pl.program_id(0) == 0)
        def _():
            dg_ref[...] = dgp
            loss_ref[...] = jnp.broadcast_to(part, loss_ref.shape)

        @pl.when(pl.program_id(0) > 0)
        def _():
            dg_ref[...] += dgp
            loss_ref[...] += jnp.broadcast_to(part, loss_ref.shape)

    row = pl.BlockSpec((tm, d), lambda i: (i, 0))
    vec = pl.BlockSpec((1, d), lambda i: (0, 0))
    return pl.pallas_call(
        body, name="loss_head", grid=(s // tm,),
        in_specs=[row, vec, row],
        out_specs=[pl.BlockSpec((8, 128), lambda i: (0, 0)), row, row, vec],
        out_shape=[jax.ShapeDtypeStruct((8, 128), F32), jax.ShapeDtypeStruct((s, d), F32),
                   jax.ShapeDtypeStruct((s, d), BF16), jax.ShapeDtypeStruct((1, d), F32)],
        compiler_params=_params(32),
    )(h2, g, target)


def _mm(name, operands, in_specs, out_shape, out_specs, grid, dims, epilogue, nk=1, acc_shape=None,
        vmem_mib=56, aliases=None):
    n_in = len(operands)
    n_out = len(out_shape)

    def body(*refs):
        a_ref, b_ref = refs[0], refs[1]
        extras = refs[2:n_in]
        outs = refs[n_in:n_in + n_out]

        def prod():
            return _dot(a_ref[...], b_ref[...], dims)

        if nk == 1:
            epilogue(prod(), extras, outs)
        else:
            acc = refs[n_in + n_out]
            k = pl.program_id(2)

            @pl.when(k == 0)
            def _():
                acc[...] = prod()

            @pl.when(k > 0)
            def _():
                acc[...] += prod()

            @pl.when(k == nk - 1)
            def _():
                epilogue(acc[...], extras, outs)

    scratch = [] if nk == 1 else [pltpu.VMEM(acc_shape, F32)]
    return pl.pallas_call(
        body, name=name, grid=grid, in_specs=in_specs, out_specs=out_specs, out_shape=out_shape,
        scratch_shapes=scratch, input_output_aliases=aliases or {},
        compiler_params=_params(vmem_mib),
    )(*operands)


def _store(acc, extras, outs):
    outs[0][...] = acc.astype(outs[0].dtype)


def _proj_in_shard(xn, w_in_g, shard_arr, prev, name):
    s, d = xn.shape
    u = d // 4
    per = N_SLOTS
    n_prev = 0 if prev is None else 1

    def body(sh_ref, x_ref, w_ref, *rest):
        del sh_ref
        rest[n_prev][...] = _dot(x_ref[...], w_ref[...], NN)

    def out_map(j, sh_ref):
        unit = per * sh_ref[0] + j
        return (unit // 4, 0, unit % 4)

    return pl.pallas_call(
        body, name=name,
        grid_spec=pltpu.PrefetchScalarGridSpec(
            num_scalar_prefetch=1, grid=(per,),
            in_specs=[pl.BlockSpec((s, d), lambda j, sh_ref: (0, 0)),
                      pl.BlockSpec((None, d, u), lambda j, sh_ref: (sh_ref[0], 0, j))] + [HBM] * n_prev,
            out_specs=pl.BlockSpec((None, s, u), out_map)),
        out_shape=jax.ShapeDtypeStruct((N_SLOTS, s, d), F32),
        input_output_aliases={3: 0} if n_prev else {},
        compiler_params=_params(56),
    )(shard_arr, xn, w_in_g, *([] if prev is None else [prev]))


def _mm_nn(name, a, b, extras, extra_specs, out_shape, out_specs, epilogue, tn, aliases=None):
    s, kdim = a.shape
    n = b.shape[1]
    return _mm(
        name, [a, b] + list(extras),
        [pl.BlockSpec((s, kdim), lambda i, j, k: (0, 0)), pl.BlockSpec((kdim, tn), lambda i, j, k: (0, j))]
        + list(extra_specs),
        out_shape, out_specs, (1, n // tn, 1), NN, epilogue, aliases=aliases)


def _mm_nt(name, a, b, extras, extra_specs, out_shape, out_specs, epilogue, tn, aliases=None):
    s, kdim = a.shape
    n = b.shape[0]
    return _mm(
        name, [a, b] + list(extras),
        [pl.BlockSpec((s, kdim), lambda i, j, k: (0, 0)), pl.BlockSpec((tn, kdim), lambda i, j, k: (j, 0))]
        + list(extra_specs),
        out_shape, out_specs, (1, n // tn, 1), NT, epilogue, aliases=aliases)


def _mm_tn(name, a, b, a_spec, b_spec, out_shape, out_spec, grid, m, tn, s, aliases=None, extra=None):
    ch = 256
    n_in = 2 if extra is None else 3

    def body(*refs):
        a_ref, b_ref = refs[0], refs[1]
        o_ref, at_ref = refs[n_in], refs[n_in + 1]

        @pl.when(pl.program_id(1) == 0)
        def _():
            for c0 in range(0, s, ch):
                at_ref[:, c0:c0 + ch] = a_ref[c0:c0 + ch, :].astype(F32).T.astype(BF16)

        o_ref[...] = _dot(at_ref[...], b_ref[...], NN).astype(o_ref.dtype)

    operands = [a, b] + ([] if extra is None else [extra])
    in_specs = [a_spec, b_spec] + ([] if extra is None else [pl.BlockSpec(memory_space=pl.ANY)])
    return pl.pallas_call(
        body, name=name, grid=grid, in_specs=in_specs, out_specs=out_spec, out_shape=out_shape,
        scratch_shapes=[pltpu.VMEM((m, s), BF16)], input_output_aliases=aliases or {},
        compiler_params=_params(56),
    )(*operands)


def _dxn(dproj_a, dproj_b, w_in_g, tn, after):
    n_a, s, d = dproj_a.shape
    u = d // 4
    ua = 4 * n_a
    nk = 4 * N_SLOTS
    per = N_SLOTS

    def body(a_ref, b_ref, w_ref, *rest):
        o_ref = rest[len(after)]
        k = pl.program_id(2)

        @pl.when(k == 0)
        def _():
            o_ref[...] = jnp.zeros_like(o_ref)

        @pl.when(k < ua)
        def _():
            o_ref[...] += _dot(a_ref[...], w_ref[...], NT)

        @pl.when(k >= ua)
        def _():
            o_ref[...] += _dot(b_ref[...], w_ref[...], NT)

    def a_map(i, j, k):
        kk = jnp.minimum(k, ua - 1)
        return (kk // 4, 0, kk % 4)

    def b_map(i, j, k):
        kk = jnp.maximum(k - ua, 0)
        return (kk // 4, 0, kk % 4)

    return pl.pallas_call(
        body, name="dxn", grid=(1, d // tn, nk),
        in_specs=[pl.BlockSpec((None, s, u), a_map), pl.BlockSpec((None, s, u), b_map),
                  pl.BlockSpec((None, tn, u), lambda i, j, k: (k // per, j, k % per))] + [HBM] * len(after),
        out_specs=pl.BlockSpec((s, tn), lambda i, j, k: (0, j)),
        out_shape=jax.ShapeDtypeStruct((s, d), F32),
        compiler_params=_params(48),
    )(dproj_a, dproj_b, w_in_g, *after)


def _attn_masks(slope, dil):
    ii = lax.broadcasted_iota(jnp.int32, (ATTN_BLK, 2 * ATTN_BLK), 0)
    jj = lax.broadcasted_iota(jnp.int32, (ATTN_BLK, 2 * ATTN_BLK), 1)
    diff = ATTN_BLK + ii - jj
    band = (diff >= 0) & (diff <= ATTN_BLK)
    bias = -(slope * float(dil)) * diff.astype(F32)
    return band, bias, jj


def _attn_window(t, nblk):
    cur = pl.ds(pl.multiple_of(t * ATTN_BLK, ATTN_BLK), ATTN_BLK)
    prev = pl.ds(pl.multiple_of(jnp.maximum(t - 1, 0) * ATTN_BLK, ATTN_BLK), ATTN_BLK)
    first = jnp.where(t % nblk == 0, ATTN_BLK, 0)
    return prev, cur, first


def _unrolled_loop(n, step, init, unroll=ATTN_UNROLL):
    def trip(i, carry):
        for k in range(unroll):
            carry = step(i * unroll + k, carry)
        return carry

    return lax.fori_loop(0, n // unroll, trip, init)


def _streams(pairs, dil, s):
    if dil == 1:
        return [src for _, src in pairs]
    seg = s // dil
    for dst, src in pairs:
        for r in range(dil):
            dst[r * seg:(r + 1) * seg, :] = src[pl.ds(r, seg, stride=dil), :].astype(dst.dtype)
    return [dst for dst, _ in pairs]


def _attn_fwd(proj_a, slopes):
    _, s, d = proj_a.shape
    heads = d // HEAD_DIM
    scale = HEAD_DIM ** -0.5
    n_t = s // ATTN_BLK
    ng = len(DILATIONS)

    def body(q_ref, k_ref, v_ref, sl_ref, o_ref, lse_ref, qd, kd, vd, od, ld, og, lg):
        slope = sl_ref[...][:, :1]
        for g, dil in enumerate(DILATIONS):
            nblk = s // dil // ATTN_BLK
            qs, ks, vs = _streams([(qd, q_ref), (kd, k_ref), (vd, v_ref)], dil, s)
            o_t, l_t = (og.at[g], lg.at[g]) if dil == 1 else (od, ld)
            band, bias, jj = _attn_masks(slope, dil)

            def blk(t, carry, nblk=nblk, band=band, bias=bias, jj=jj, qs=qs, ks=ks, vs=vs, o_t=o_t, l_t=l_t):
                prev, cur, first = _attn_window(t, nblk)
                kk = jnp.concatenate([ks[prev, :], ks[cur, :]], axis=0)
                vv = jnp.concatenate([vs[prev, :], vs[cur, :]], axis=0)
                sc = _dot(qs[cur, :], kk, NT) * scale + bias
                sc = jnp.where(band & (jj >= first), sc, -jnp.inf)
                m = jnp.max(sc, axis=1, keepdims=True)
                p = jnp.exp(sc - m)
                l = jnp.sum(p, axis=1, keepdims=True)
                o_t[cur, :] = _dot(p, vv, NN) / l
                l_t[cur, :] = jnp.broadcast_to(m + jnp.log(l), (ATTN_BLK, HEAD_DIM))
                return carry

            _unrolled_loop(n_t, blk, 0)
            seg = s // dil
            if dil > 1:
                for r in range(dil):
                    og[g, pl.ds(r, seg, stride=dil), :] = od[r * seg:(r + 1) * seg, :]
                    lg[g, pl.ds(r, seg, stride=dil), :] = ld[r * seg:(r + 1) * seg, :]

        ch = 256

        def combine(c, carry):
            rows = pl.ds(pl.multiple_of(c * ch, ch), ch)
            ls = [lg[g, rows, :] for g in range(ng)]
            mx = functools.reduce(jnp.maximum, ls)
            es = [jnp.exp(x - mx) for x in ls]
            den = functools.reduce(jnp.add, es)
            num = functools.reduce(jnp.add, [es[g] * og[g, rows, :] for g in range(ng)])
            o_ref[rows, :] = (num / den).astype(o_ref.dtype)
            lse_ref[rows, :] = mx + jnp.log(den)
            return carry

        lax.fori_loop(0, s // ch, combine, 0)

    def col(slot):
        return pl.BlockSpec((None, s, HEAD_DIM), lambda h: (slot, 0, h))

    head = pl.BlockSpec((s, HEAD_DIM), lambda h: (0, h))
    return pl.pallas_call(
        body, name="attn_fwd", grid=(heads,),
        in_specs=[col(0), col(1), col(2), pl.BlockSpec((None, 1, HEAD_DIM), lambda h: (h, 0, 0))],
        out_specs=[head, head],
        out_shape=[jax.ShapeDtypeStruct((s, d), BF16), jax.ShapeDtypeStruct((s, d), F32)],
        scratch_shapes=[pltpu.VMEM((s, HEAD_DIM), BF16)] * 3 + [pltpu.VMEM((s, HEAD_DIM), F32)] * 2
        + [pltpu.VMEM((ng, s, HEAD_DIM), F32)] * 2,
        compiler_params=_params(40),
    )(proj_a, proj_a, proj_a, slopes)


def _attn_bwd(proj_a, slopes, y_attn, lse, dy):
    _, s, d = proj_a.shape
    heads = d // HEAD_DIM
    scale = HEAD_DIM ** -0.5
    n_t = s // ATTN_BLK

    def body(q_ref, k_ref, v_ref, sl_ref, o_ref, lse_ref, dy_ref, out_ref,
             qd, kd, vd, dod, lsd, dld, delta, dqd, dkd, dvd, dqa, dka, dva):
        slope = sl_ref[...][:, :1]
        dyv = dy_ref[...]
        delta[...] = jnp.broadcast_to(
            jnp.sum(dyv * o_ref[...].astype(F32), axis=1, keepdims=True), (s, HEAD_DIM))
        for g, dil in enumerate(DILATIONS):
            nblk = s // dil // ATTN_BLK
            seg = s // dil
            qs, ks, vs, dos, lss, dls = _streams(
                [(qd, q_ref), (kd, k_ref), (vd, v_ref), (dod, dy_ref), (lsd, lse_ref), (dld, delta)], dil, s)
            dq_t, dk_t, dv_t = (dqa, dka, dva) if dil == 1 else (dqd, dkd, dvd)
            band, bias, jj = _attn_masks(slope, dil)

            def blk(t, carry, nblk=nblk, band=band, bias=bias, jj=jj, qs=qs, ks=ks, vs=vs, dos=dos, lss=lss,
                    dls=dls, dq_t=dq_t, dk_t=dk_t, dv_t=dv_t):
                ck, cv = carry
                prev, cur, first = _attn_window(t, nblk)
                q = qs[cur, :]
                do = dos[cur, :]
                lse_b = lss[cur, :]
                dl_b = dls[cur, :]
                kk = jnp.concatenate([ks[prev, :], ks[cur, :]], axis=0)
                vv = jnp.concatenate([vs[prev, :], vs[cur, :]], axis=0)
                sc = _dot(q, kk, NT) * scale + bias
                p = jnp.where(band & (jj >= first), jnp.exp(sc - jnp.concatenate([lse_b, lse_b], axis=1)), 0.0)
                dp = _dot(do, vv, NT)
                ds = p * (dp - jnp.concatenate([dl_b, dl_b], axis=1))
                dv_b = _dot(p, do, TN)
                dk_b = _dot(ds, q, TN) * scale
                dq_t[cur, :] = _dot(ds, kk, NN) * scale
                done = pl.ds(pl.multiple_of(jnp.where(t == 0, n_t, t - 1) * ATTN_BLK, ATTN_BLK), ATTN_BLK)
                dk_t[done, :] = ck + dk_b[:ATTN_BLK]
                dv_t[done, :] = cv + dv_b[:ATTN_BLK]
                return dk_b[ATTN_BLK:], dv_b[ATTN_BLK:]

            zero = jnp.zeros((ATTN_BLK, HEAD_DIM), F32)
            ck, cv = _unrolled_loop(n_t, blk, (zero, zero))
            dk_t[(n_t - 1) * ATTN_BLK:n_t * ATTN_BLK, :] = ck
            dv_t[(n_t - 1) * ATTN_BLK:n_t * ATTN_BLK, :] = cv
            if dil > 1:
                for acc, part in ((dqa, dqd), (dka, dkd), (dva, dvd)):
                    for r in range(dil):
                        acc[pl.ds(r, seg, stride=dil), :] += part[r * seg:(r + 1) * seg, :]
        out_ref[0] = dqa[...].astype(out_ref.dtype)
        out_ref[1] = dka[0:s, :].astype(out_ref.dtype)
        out_ref[2] = dva[0:s, :].astype(out_ref.dtype)

    def col(slot):
        return pl.BlockSpec((None, s, HEAD_DIM), lambda h: (slot, 0, h))

    head = pl.BlockSpec((s, HEAD_DIM), lambda h: (0, h))
    return pl.pallas_call(
        body, name="attn_bwd", grid=(heads,),
        in_specs=[col(0), col(1), col(2), pl.BlockSpec((None, 1, HEAD_DIM), lambda h: (h, 0, 0)),
                  head, head, head],
        out_specs=pl.BlockSpec((N_QKV, s, HEAD_DIM), lambda h: (0, 0, h)),
        out_shape=jax.ShapeDtypeStruct((N_QKV, s, d), BF16),
        scratch_shapes=[pltpu.VMEM((s, HEAD_DIM), BF16)] * 4 + [pltpu.VMEM((s, HEAD_DIM), F32)] * 4
        + [pltpu.VMEM((s + ATTN_BLK, HEAD_DIM), F32)] * 2 + [pltpu.VMEM((s, HEAD_DIM), F32)]
        + [pltpu.VMEM((s + ATTN_BLK, HEAD_DIM), F32)] * 2,
        compiler_params=_params(48),
    )(proj_a, proj_a, proj_a, slopes, y_attn, lse, dy)


def _expm1(x):
    small = x * (1.0 + x * (0.5 + x * (1.0 / 6.0 + x * (1.0 / 24.0 + x * (1.0 / 120.0)))))
    return jnp.where(jnp.abs(x) < 0.1, small, jnp.exp(x) - 1.0)


def _softplus(x):
    return jnp.maximum(x, 0.0) + jnp.log1p(jnp.exp(-jnp.abs(x)))


GELU_K = 0.7978845608028654
GELU_C = 0.044715


def _gelu(x):
    t = jnp.tanh(GELU_K * (x + GELU_C * x * x * x))
    return 0.5 * x * (1.0 + t), t


def _gelu_grad(x, t):
    return 0.5 * (1.0 + t) + 0.5 * x * (1.0 - t * t) * GELU_K * (1.0 + 3.0 * GELU_C * x * x)


def _lru_gates(xc, wa, ba, wx, bx, sp):
    r = _sigmoid(_dot(xc, wa, NN) + ba)
    ig = _sigmoid(_dot(xc, wx, NN) + bx)
    log_a = -LRU_C * r * sp
    a = jnp.exp(log_a)
    mult = jnp.sqrt(-_expm1(2.0 * log_a))
    return r, ig, a, mult


def _scan_fwd(a, u, tt):
    row = lax.broadcasted_iota(jnp.int32, a.shape, 0)
    sh = 1
    while sh < tt:
        keep = row >= sh
        a_s = jnp.where(keep, pltpu.roll(a, sh, 0), 1.0)
        u_s = jnp.where(keep, pltpu.roll(u, sh, 0), 0.0)
        u = a * u_s + u
        a = a * a_s
        sh *= 2
    return a, u


def _scan_bwd(b, g, tt):
    row = lax.broadcasted_iota(jnp.int32, b.shape, 0)
    sh = 1
    while sh < tt:
        keep = row < tt - sh
        b_s = jnp.where(keep, pltpu.roll(b, tt - sh, 0), 1.0)
        g_s = jnp.where(keep, pltpu.roll(g, tt - sh, 0), 0.0)
        g = g + b * g_s
        b = b * b_s
        sh *= 2
    return b, g


def _conv_rows(xpad_ref, cw, cb, s):
    acc = cb
    for j in range(CONV_TAPS):
        off = 8 - (CONV_TAPS - 1) + j
        acc = acc + cw[j:j + 1, :] * xpad_ref[off:off + s, :]
    return acc


LRU_TILE = 128
LRU_UNROLL = 4


def _lru_specs(s, d):
    heads = d // HEAD_DIM

    def col(slot):
        return pl.BlockSpec((None, s, HEAD_DIM), lambda h: (slot, 0, h))

    vec = pl.BlockSpec((1, HEAD_DIM), lambda h: (0, h))
    mat = pl.BlockSpec((None, HEAD_DIM, HEAD_DIM), lambda h: (h, 0, 0))
    cw = pl.BlockSpec((8, HEAD_DIM), lambda h: (0, h))
    head = pl.BlockSpec((s, HEAD_DIM), lambda h: (0, h))
    return heads, col, vec, mat, cw, head


def _lru_fwd(proj_b, conv_w, conv_b, wa, ba, wx, bx, lam):
    _, s, d = proj_b.shape
    heads, col, vec, mat, cws, head = _lru_specs(s, d)
    tt = LRU_TILE

    def body(xr_ref, xg_ref, cw_ref, cb_ref, wa_ref, ba_ref, wx_ref, bx_ref, lam_ref, y_ref, h_ref, xpad, xc_s):
        xpad[0:8, :] = jnp.zeros((8, HEAD_DIM), F32)
        xpad[8:8 + s, :] = xr_ref[...]
        xc_s[...] = _conv_rows(xpad, cw_ref[...], cb_ref[...], s)
        sp = _softplus(-lam_ref[...])
        wav, wxv, bav, bxv = wa_ref[...], wx_ref[...], ba_ref[...], bx_ref[...]

        def tile(i, hc):
            rows = pl.ds(pl.multiple_of(i * tt, tt), tt)
            xc = xc_s[rows, :]
            _, ig, a, mult = _lru_gates(xc, wav, bav, wxv, bxv, sp)
            pa, hl = _scan_fwd(a, mult * (ig * xc), tt)
            h = hl + pa * hc
            h_ref[rows, :] = h
            gel, _ = _gelu(xg_ref[rows, :])
            y_ref[rows, :] = (h * gel).astype(y_ref.dtype)
            return h[tt - 1:tt, :]

        _unrolled_loop(s // tt, tile, jnp.zeros((1, HEAD_DIM), F32), LRU_UNROLL)

    return pl.pallas_call(
        body, name="lru_fwd", grid=(heads,),
        in_specs=[col(N_QKV), col(N_QKV + 1), cws, vec, mat, vec, mat, vec, vec],
        out_specs=[head, head],
        out_shape=[jax.ShapeDtypeStruct((s, d), BF16), jax.ShapeDtypeStruct((s, d), F32)],
        scratch_shapes=[pltpu.VMEM((s + 8, HEAD_DIM), F32), pltpu.VMEM((s, HEAD_DIM), F32)],
        compiler_params=_params(32),
    )(proj_b, proj_b, conv_w, conv_b, wa, ba, wx, bx, lam)


def _lru_bwd(proj_b, h_lru, dy, conv_w, conv_b, wa, ba, wx, bx, lam, dproj_b):
    _, s, d = proj_b.shape
    heads, col, vec, mat, cws, head = _lru_specs(s, d)
    tt = LRU_TILE
    n_t = s // tt

    def body(xr_ref, xg_ref, h_ref, dy_ref, cw_ref, cb_ref, wa_ref, ba_ref, wx_ref, bx_ref, lam_ref, alias_ref,
             out_ref, dcw_ref, dcb_ref, dwa_ref, dba_ref, dwx_ref, dbx_ref, dlam_ref, xpad, xc_s, dxc_s):
        del alias_ref
        xpad[0:8, :] = jnp.zeros((8, HEAD_DIM), F32)
        xpad[8:8 + s, :] = xr_ref[...]
        cwv = cw_ref[...]
        xc_s[...] = _conv_rows(xpad, cwv, cb_ref[...], s)
        dxc_s[s:s + 8, :] = jnp.zeros((8, HEAD_DIM), F32)
        lamv = lam_ref[...]
        sp = _softplus(-lamv)
        wav, wxv, bav, bxv = wa_ref[...], wx_ref[...], ba_ref[...], bx_ref[...]
        dwa_ref[...] = jnp.zeros_like(dwa_ref)
        dwx_ref[...] = jnp.zeros_like(dwx_ref)
        zero = jnp.zeros((1, HEAD_DIM), F32)
        row = lax.broadcasted_iota(jnp.int32, (tt, HEAD_DIM), 0)

        def tile(it, carry):
            dh_next, a_next, dba, dbx, dsp, dcb = carry
            i = n_t - 1 - it
            t0 = pl.multiple_of(i * tt, tt)
            rows = pl.ds(t0, tt)
            xc = xc_s[rows, :]
            r, ig, a, mult = _lru_gates(xc, wav, bav, wxv, bxv, sp)
            h = h_ref[rows, :]
            before = h_ref[pl.ds(pl.multiple_of(jnp.maximum(t0 - 8, 0), 8), 8), :][7:8, :]
            before = before * (i > 0).astype(F32)
            h_prev = jnp.where(row == 0, before, pltpu.roll(h, 1, 0))
            xg = xg_ref[rows, :]
            dyv = dy_ref[rows, :]
            gel, th = _gelu(xg)
            out_ref[1, rows, :] = (dyv * h * _gelu_grad(xg, th)).astype(out_ref.dtype)
            b = jnp.where(row == tt - 1, a_next, pltpu.roll(a, tt - 1, 0))
            pb, z = _scan_bwd(b, dyv * gel, tt)
            dh = z + pb * dh_next
            da = dh * h_prev
            dmult = dh * (ig * xc)
            dig = dh * (mult * xc)
            dla = da * a - dmult * (a * a / mult)
            dzr = dla * (-LRU_C * sp) * (r * (1.0 - r))
            dzx = dig * (ig * (1.0 - ig))
            dxc = dh * (mult * ig) + _dot(dzr, wav, NT) + _dot(dzx, wxv, NT)
            dxc_s[rows, :] = dxc
            dwa_ref[...] += _dot(xc, dzr, TN)
            dwx_ref[...] += _dot(xc, dzx, TN)
            return (dh[0:1, :], a[0:1, :],
                    dba + jnp.sum(dzr, axis=0, keepdims=True),
                    dbx + jnp.sum(dzx, axis=0, keepdims=True),
                    dsp + jnp.sum(dla * (-LRU_C * r), axis=0, keepdims=True),
                    dcb + jnp.sum(dxc, axis=0, keepdims=True))

        _, _, dba, dbx, dsp, dcb = _unrolled_loop(n_t, tile, (zero, zero, zero, zero, zero, zero), LRU_UNROLL)
        dba_ref[...] = dba
        dbx_ref[...] = dbx
        dcb_ref[...] = dcb
        dlam_ref[...] = -dsp * _sigmoid(-lamv)
        dxr = jnp.zeros((s, HEAD_DIM), F32)
        for j in range(CONV_TAPS):
            back = CONV_TAPS - 1 - j
            off = 8 - back
            dcw_ref[j:j + 1, :] = jnp.sum(dxc_s[0:s, :] * xpad[off:off + s, :], axis=0, keepdims=True)
            dxr = dxr + cwv[j:j + 1, :] * dxc_s[back:back + s, :]
        out_ref[0] = dxr.astype(out_ref.dtype)

    return pl.pallas_call(
        body, name="lru_bwd", grid=(heads,),
        in_specs=[col(N_QKV), col(N_QKV + 1), head, head, cws, vec, mat, vec, mat, vec, vec,
                  pl.BlockSpec(memory_space=pl.ANY)],
        out_specs=[pl.BlockSpec((2, s, HEAD_DIM), lambda h: (0, 0, h)),
                   pl.BlockSpec((CONV_TAPS, HEAD_DIM), lambda h: (0, h)), vec, mat, vec, mat, vec, vec],
        out_shape=[jax.ShapeDtypeStruct(dproj_b.shape, dproj_b.dtype),
                   jax.ShapeDtypeStruct((CONV_TAPS, d), F32), jax.ShapeDtypeStruct((1, d), F32),
                   jax.ShapeDtypeStruct(wa.shape, F32), jax.ShapeDtypeStruct((1, d), F32),
                   jax.ShapeDtypeStruct(wx.shape, F32), jax.ShapeDtypeStruct((1, d), F32),
                   jax.ShapeDtypeStruct((1, d), F32)],
        scratch_shapes=[pltpu.VMEM((s + 8, HEAD_DIM), F32), pltpu.VMEM((s, HEAD_DIM), F32),
                        pltpu.VMEM((s + 8, HEAD_DIM), F32)],
        input_output_aliases={11: 0},
        compiler_params=_params(32),
    )(proj_b, proj_b, h_lru, dy, conv_w, conv_b, wa, ba, wx, bx, lam, dproj_b)


def _place():
    x, y, c = (lax.axis_index(n) for n in AXES)
    return x, y, c


def _other_chips(x, y):
    return [(1 - x, y), (x, 1 - y), (1 - x, 1 - y)]


HBM = pl.BlockSpec(memory_space=pl.ANY)


def _cast_shard(w, chip_arr, name):
    r, cols = w.shape
    rh = r // 2
    tr = _row_tile(rh, cols * 4, STREAM_TILE)
    nt = rh // tr

    def body(chip_ref, w_ref, o_ref):
        del chip_ref
        o_ref[...] = w_ref[...].astype(BF16)

    return pl.pallas_call(
        body, name=name,
        grid_spec=pltpu.PrefetchScalarGridSpec(
            num_scalar_prefetch=1, grid=(2, nt),
            in_specs=[pl.BlockSpec((tr, cols), lambda h, i, chip_ref: (h * nt + i, 0))],
            out_specs=pl.BlockSpec((None, None, tr, cols), lambda h, i, chip_ref: (chip_ref[0], h, i, 0))),
        out_shape=jax.ShapeDtypeStruct((N_CHIPS, 2, rh, cols), BF16), compiler_params=_params(32),
    )(chip_arr, w)


HBM_SPEC = pl.BlockSpec(memory_space=pltpu.HBM)
SEM_SPEC = pl.BlockSpec(memory_space=pltpu.SEMAPHORE)
EFFECT = pltpu.SideEffectType.DATAFLOW_SIDE_EFFECTING
TOKEN = jax.ShapeDtypeStruct((8, 128), F32)
TOKEN_SPEC = pl.BlockSpec(memory_space=pltpu.VMEM)


def _in_hbm(arrays):
    return [pltpu.with_memory_space_constraint(a, pltpu.HBM) for a in arrays]


def _hbm_like(arrays):
    return [pltpu.HBM(a.shape, a.dtype) for a in arrays]


def _sems(n):
    return pltpu.SemaphoreType.DMA((n,))


def _remote(src, dst, send_sem, recv_sem, to):
    return pltpu.make_async_remote_copy(src_ref=src, dst_ref=dst, send_sem=send_sem, recv_sem=recv_sem,
                                        device_id=to, device_id_type=MESH)


ALL_FLIPS = (0, 1, 2)


def _gather_start(bufs, groups, after, name):
    n = len(bufs)
    ng = len(groups)

    def body(*refs):
        ins = refs[:n]
        sems = refs[n + len(after):n + len(after) + 2 * ng]
        x, y, c = _place()
        me = 2 * x + y
        chips = _other_chips(x, y)
        for g, (ws, flips) in enumerate(groups):
            for i, w in enumerate(ws):
                for jj, j in enumerate(flips):
                    k = len(flips) * i + jj
                    mine = ins[w].at[me, c]
                    _remote(mine, mine, sems[2 * g].at[k], sems[2 * g + 1].at[k], (*chips[j], c)).start()

    sem_shapes = []
    for ws, flips in groups:
        sem_shapes += [_sems(len(flips) * len(ws))] * 2
    res = pl.pallas_call(
        body, name=name, in_specs=[HBM_SPEC] * n + [HBM] * len(after),
        out_specs=[SEM_SPEC] * (2 * ng) + [HBM_SPEC] * n, out_shape=sem_shapes + _hbm_like(bufs),
        input_output_aliases={w: 2 * ng + w for w in range(n)},
        compiler_params=pltpu.CompilerParams(has_side_effects=EFFECT),
    )(*_in_hbm(bufs), *after)
    return [(res[2 * g], res[2 * g + 1]) for g in range(ng)], list(res[2 * ng:])


def _gather_forward(bufs, recv, after, name, flips=ALL_FLIPS):
    m = len(bufs)
    nf = len(flips)

    def body(*refs):
        ins, recv_in = refs[:m], refs[m]
        fsend, frecv = refs[m + 1 + len(after)], refs[m + 2 + len(after)]
        x, y, c = _place()
        chips = _other_chips(x, y)
        for jj, j in enumerate(flips):
            cx, cy = chips[j]
            for i in range(m):
                landed = ins[i].at[2 * cx + cy, c]
                k = nf * i + jj
                _remote(landed, landed, fsend.at[k], recv_in.at[k], (cx, cy, c)).wait_recv()
                _remote(landed, landed, fsend.at[k], frecv.at[k], (x, y, 1 - c)).start()

    res = pl.pallas_call(
        body, name=name, in_specs=[HBM_SPEC] * m + [SEM_SPEC] + [HBM] * len(after),
        out_specs=[SEM_SPEC, SEM_SPEC] + [HBM_SPEC] * m, out_shape=[_sems(nf * m), _sems(nf * m)] + _hbm_like(bufs),
        input_output_aliases={i: 2 + i for i in range(m)},
        compiler_params=pltpu.CompilerParams(has_side_effects=EFFECT),
    )(*bufs, recv, *after)
    return (res[0], res[1]), list(res[2:])


NEIGHBOURS = (0, 1)


def _relay_partner(x, y, c):
    return 2 * (x ^ (1 - c)) + (y ^ c), (x ^ c, y ^ (1 - c))


def _gather_forward_relay(bufs, recv, after, name):
    m = len(bufs)
    nf = len(NEIGHBOURS)

    def body(*refs):
        ins, recv_in = refs[:m], refs[m]
        fsend, frecv, rsend, rrecv = refs[m + 1 + len(after):m + 5 + len(after)]
        x, y, c = _place()
        chips = _other_chips(x, y)
        for jj, j in enumerate(NEIGHBOURS):
            cx, cy = chips[j]
            for i in range(m):
                landed = ins[i].at[2 * cx + cy, c]
                k = nf * i + jj
                _remote(landed, landed, fsend.at[k], recv_in.at[k], (cx, cy, c)).wait_recv()
        row, (px, py) = _relay_partner(x, y, c)
        for i in range(m):
            relayed = ins[i].at[row, c]
            _remote(relayed, relayed, rsend.at[i], rrecv.at[i], (px, py, c)).start()
        for jj, j in enumerate(NEIGHBOURS):
            cx, cy = chips[j]
            for i in range(m):
                landed = ins[i].at[2 * cx + cy, c]
                k = nf * i + jj
                _remote(landed, landed, fsend.at[k], frecv.at[k], (x, y, 1 - c)).start()

    res = pl.pallas_call(
        body, name=name, in_specs=[HBM_SPEC] * m + [SEM_SPEC] + [HBM] * len(after),
        out_specs=[SEM_SPEC] * 4 + [HBM_SPEC] * m,
        out_shape=[_sems(nf * m), _sems(nf * m), _sems(m), _sems(m)] + _hbm_like(bufs),
        input_output_aliases={i: 4 + i for i in range(m)},
        compiler_params=pltpu.CompilerParams(has_side_effects=EFFECT),
    )(*bufs, recv, *after)
    return tuple(res[:4]), list(res[4:])


def _gather_forward_diag(bufs, rrecv, after, name):
    m = len(bufs)

    def body(*refs):
        ins, rrecv_in = refs[:m], refs[m]
        dsend, drecv = refs[m + 1 + len(after)], refs[m + 2 + len(after)]
        x, y, c = _place()
        diag = (2 * x + y) ^ 3
        _, (px, py) = _relay_partner(x, y, c)
        for i in range(m):
            landed = ins[i].at[diag, c]
            _remote(landed, landed, dsend.at[i], rrecv_in.at[i], (px, py, c)).wait_recv()
            _remote(landed, landed, dsend.at[i], drecv.at[i], (x, y, 1 - c)).start()

    res = pl.pallas_call(
        body, name=name, in_specs=[HBM_SPEC] * m + [SEM_SPEC] + [HBM] * len(after),
        out_specs=[SEM_SPEC, SEM_SPEC] + [HBM_SPEC] * m, out_shape=[_sems(m), _sems(m)] + _hbm_like(bufs),
        input_output_aliases={i: 2 + i for i in range(m)},
        compiler_params=pltpu.CompilerParams(has_side_effects=EFFECT),
    )(*bufs, rrecv, *after)
    return (res[0], res[1]), list(res[2:])


def _gather_finish_diag(bufs, rsend, dsend, drecv, after, name):
    m = len(bufs)

    def body(*refs):
        ins = refs[:m]
        rsend_in, dsend_in, drecv_in = refs[m:m + 3]
        x, y, c = _place()
        diag = (2 * x + y) ^ 3
        row, (px, py) = _relay_partner(x, y, c)
        for i in range(m):
            relayed = ins[i].at[row, c]
            _remote(relayed, relayed, rsend_in.at[i], drecv_in.at[i], (px, py, c)).wait_send()
            landed = ins[i].at[diag, c]
            _remote(landed, landed, dsend_in.at[i], drecv_in.at[i], (x, y, 1 - c)).wait_send()
            theirs = ins[i].at[diag, 1 - c]
            _remote(theirs, theirs, dsend_in.at[i], drecv_in.at[i], (x, y, 1 - c)).wait_recv()

    return list(pl.pallas_call(
        body, name=name, in_specs=[HBM_SPEC] * m + [SEM_SPEC] * 3 + [HBM] * len(after),
        out_specs=[HBM_SPEC] * m, out_shape=_hbm_like(bufs),
        input_output_aliases={i: i for i in range(m)},
        compiler_params=pltpu.CompilerParams(has_side_effects=EFFECT),
    )(*bufs, rsend, dsend, drecv, *after))


def _gather_finish(bufs, send, fsend, frecv, after, name, flips=ALL_FLIPS):
    m = len(bufs)
    nf = len(flips)

    def body(*refs):
        ins = refs[:m]
        send_in, fsend_in, frecv_in = refs[m:m + 3]
        x, y, c = _place()
        me = 2 * x + y
        chips = _other_chips(x, y)
        for jj, j in enumerate(flips):
            cx, cy = chips[j]
            cj = 2 * cx + cy
            for i in range(m):
                k = nf * i + jj
                mine = ins[i].at[me, c]
                _remote(mine, mine, send_in.at[k], frecv_in.at[k], (cx, cy, c)).wait_send()
                landed = ins[i].at[cj, c]
                _remote(landed, landed, fsend_in.at[k], frecv_in.at[k], (x, y, 1 - c)).wait_send()
                theirs = ins[i].at[cj, 1 - c]
                _remote(theirs, theirs, fsend_in.at[k], frecv_in.at[k], (x, y, 1 - c)).wait_recv()

    return list(pl.pallas_call(
        body, name=name, in_specs=[HBM_SPEC] * m + [SEM_SPEC] * 3 + [HBM] * len(after),
        out_specs=[HBM_SPEC] * m, out_shape=_hbm_like(bufs),
        input_output_aliases={i: i for i in range(m)},
        compiler_params=pltpu.CompilerParams(has_side_effects=EFFECT),
    )(*bufs, send, fsend, frecv, *after))


def _pair_exchange(grads, name):
    n = len(grads)

    def body(*refs):
        ins, outs = refs[:n], refs[n:2 * n]
        send_sems, recv_sems = refs[2 * n:]
        x, y, c = _place()
        sibling = (x, y, 1 - c)
        cps = []
        for w in range(n):
            for j in range(N_CHIPS):
                cp = pltpu.make_async_remote_copy(
                    src_ref=ins[w].at[j, 1 - c], dst_ref=outs[w].at[j], send_sem=send_sems.at[N_CHIPS * w + j],
                    recv_sem=recv_sems.at[N_CHIPS * w + j], device_id=sibling, device_id_type=MESH)
                cp.start()
                cps.append(cp)
        for cp in cps:
            cp.wait()

    return pl.pallas_call(
        body, name=name, in_specs=[HBM] * n, out_specs=[HBM] * n,
        out_shape=[jax.ShapeDtypeStruct((N_CHIPS,) + a.shape[2:], a.dtype) for a in grads],
        scratch_shapes=[pltpu.SemaphoreType.DMA((N_CHIPS * n,)), pltpu.SemaphoreType.DMA((N_CHIPS * n,))],
    )(*grads)


def _pair_start(grads, name):
    n = len(grads)
    lands = [lax.empty((N_CHIPS,) + a.shape[2:], a.dtype) for a in grads]

    def body(*refs):
        ins, land_in = refs[:n], refs[n:2 * n]
        send, recv = refs[2 * n], refs[2 * n + 1]
        token = refs[4 * n + 2]
        x, y, c = _place()
        for w in range(n):
            for j in range(N_CHIPS):
                k = N_CHIPS * w + j
                _remote(ins[w].at[j, 1 - c], land_in[w].at[j], send.at[k], recv.at[k], (x, y, 1 - c)).start()
        token[...] = jnp.zeros_like(token)

    res = pl.pallas_call(
        body, name=name, in_specs=[HBM_SPEC] * (2 * n),
        out_specs=[SEM_SPEC, SEM_SPEC] + [HBM_SPEC] * (2 * n) + [TOKEN_SPEC],
        out_shape=[_sems(N_CHIPS * n), _sems(N_CHIPS * n)] + _hbm_like(grads) + _hbm_like(lands) + [TOKEN],
        input_output_aliases={i: 2 + i for i in range(2 * n)},
        compiler_params=pltpu.CompilerParams(has_side_effects=EFFECT),
    )(*_in_hbm(grads), *_in_hbm(lands))
    return (res[0], res[1]), list(res[2:2 + n]), list(res[2 + n:2 + 2 * n]), res[2 + 2 * n]


def _pair_wait(sems, grads, lands, after, name):
    n = len(grads)

    def body(*refs):
        ins, land_in = refs[:n], refs[n:2 * n]
        send_in, recv_in = refs[2 * n], refs[2 * n + 1]
        x, y, c = _place()
        for w in range(n):
            for j in range(N_CHIPS):
                k = N_CHIPS * w + j
                cp = _remote(ins[w].at[j, 1 - c], land_in[w].at[j], send_in.at[k], recv_in.at[k], (x, y, 1 - c))
                cp.wait_send()
                cp.wait_recv()

    res = pl.pallas_call(
        body, name=name, in_specs=[HBM_SPEC] * (2 * n) + [SEM_SPEC, SEM_SPEC] + [HBM] * len(after),
        out_specs=[HBM_SPEC] * (2 * n), out_shape=_hbm_like(grads) + _hbm_like(lands),
        input_output_aliases={i: i for i in range(2 * n)},
        compiler_params=pltpu.CompilerParams(has_side_effects=EFFECT),
    )(*grads, *lands, sems[0], sems[1], *after)
    return list(res[:n]), list(res[n:])


def _chip_start(sums, name, to_all=()):
    m = len(sums)
    lands = [lax.empty(((N_CHIPS,) if i in to_all else ()) + a.shape, a.dtype) for i, a in enumerate(sums)]

    def body(*refs):
        ins, land_in = refs[:m], refs[m:2 * m]
        send, recv = refs[2 * m], refs[2 * m + 1]
        token = refs[4 * m + 2]
        x, y, c = _place()
        me = 2 * x + y
        for i in sorted(range(m), key=lambda i: i not in to_all):
            for j, (cx, cy) in enumerate(_other_chips(x, y)):
                src = ins[i] if i in to_all else ins[i].at[2 * cx + cy]
                _remote(src, land_in[i].at[me], send.at[3 * i + j], recv.at[3 * i + j], (cx, cy, c)).start()
        token[...] = jnp.zeros_like(token)

    res = pl.pallas_call(
        body, name=name, in_specs=[HBM_SPEC] * (2 * m),
        out_specs=[SEM_SPEC, SEM_SPEC] + [HBM_SPEC] * (2 * m) + [TOKEN_SPEC],
        out_shape=[_sems(3 * m), _sems(3 * m)] + _hbm_like(sums) + _hbm_like(lands) + [TOKEN],
        input_output_aliases={i: 2 + i for i in range(2 * m)},
        compiler_params=pltpu.CompilerParams(has_side_effects=EFFECT),
    )(*_in_hbm(sums), *_in_hbm(lands))
    return (res[0], res[1]), list(res[2:2 + m]), list(res[2 + m:2 + 2 * m]), res[2 + 2 * m]


def _chip_wait(sems, sums, lands, after, name, to_all=()):
    m = len(sums)

    def body(*refs):
        ins, land_in = refs[:m], refs[m:2 * m]
        send_in, recv_in = refs[2 * m], refs[2 * m + 1]
        x, y, c = _place()
        for i in range(m):
            for j, (cx, cy) in enumerate(_other_chips(x, y)):
                cj = 2 * cx + cy
                src = ins[i] if i in to_all else ins[i].at[cj]
                cp = _remote(src, land_in[i].at[cj], send_in.at[3 * i + j], recv_in.at[3 * i + j], (cx, cy, c))
                cp.wait_send()
                cp.wait_recv()

    res = pl.pallas_call(
        body, name=name, in_specs=[HBM_SPEC] * (2 * m) + [SEM_SPEC, SEM_SPEC] + [HBM] * len(after),
        out_specs=[HBM_SPEC] * (2 * m), out_shape=_hbm_like(sums) + _hbm_like(lands),
        input_output_aliases={i: i for i in range(2 * m)},
        compiler_params=pltpu.CompilerParams(has_side_effects=EFFECT),
    )(*sums, *lands, sems[0], sems[1], *after)
    return list(res[:m]), list(res[m:])


def _half_exchange(bufs, name):
    n = len(bufs)
    parts = []
    for w, a in enumerate(bufs):
        parts += [(w, None)] if a.ndim == 3 else [(w, j) for j in range(a.shape[0])]
    np_ = len(parts)

    def body(*refs):
        outs = refs[n:2 * n]
        send_sems, recv_sems = refs[2 * n:]
        x, y, c = _place()
        sibling = (x, y, 1 - c)

        def half(w, j, h):
            return outs[w].at[h] if j is None else outs[w].at[j, h]

        cps = []
        for k, (w, j) in enumerate(parts):
            rc = _remote(half(w, j, c), half(w, j, c), send_sems.at[k], recv_sems.at[k], sibling)
            rc.start()
            cps.append(rc)
        for k, (w, j) in enumerate(parts):
            theirs = half(w, j, 1 - c)
            _remote(theirs, theirs, send_sems.at[k], recv_sems.at[k], sibling).wait_recv()
        for cp in cps:
            cp.wait_send()

    return pl.pallas_call(
        body, name=name, in_specs=[HBM] * n, out_specs=[HBM] * n,
        out_shape=[jax.ShapeDtypeStruct(a.shape, a.dtype) for a in bufs],
        input_output_aliases={w: w for w in range(n)},
        scratch_shapes=[_sems(np_), _sems(np_)],
    )(*bufs)


def _all_gather8(block, name, after=()):
    def body(in_ref, *rest):
        out_ref, send_sems, recv_sems, local_sem = rest[len(after):]
        x, y, c = _place()
        me = 4 * x + 2 * y + c
        mine = pltpu.make_async_copy(in_ref, out_ref.at[me], local_sem)
        mine.start()
        flips = [(fx, fy, fc) for fx in (0, 1) for fy in (0, 1) for fc in (0, 1)][1:]
        cps = []
        for k, (fx, fy, fc) in enumerate(flips):
            cp = pltpu.make_async_remote_copy(
                src_ref=in_ref, dst_ref=out_ref.at[me], send_sem=send_sems.at[k], recv_sem=recv_sems.at[k],
                device_id=(x ^ fx, y ^ fy, c ^ fc), device_id_type=MESH)
            cp.start()
            cps.append(cp)
        for k, (fx, fy, fc) in enumerate(flips):
            px, py, pc = x ^ fx, y ^ fy, c ^ fc
            theirs = out_ref.at[4 * px + 2 * py + pc]
            pltpu.make_async_remote_copy(
                src_ref=theirs, dst_ref=theirs, send_sem=send_sems.at[k], recv_sem=recv_sems.at[k],
                device_id=(px, py, pc), device_id_type=MESH).wait_recv()
        for cp in cps:
            cp.wait_send()
        mine.wait()

    return pl.pallas_call(
        body, name=name, in_specs=[HBM] * (1 + len(after)), out_specs=HBM,
        out_shape=jax.ShapeDtypeStruct((N_DEV,) + block.shape, block.dtype),
        scratch_shapes=[pltpu.SemaphoreType.DMA((N_DEV - 1,)), pltpu.SemaphoreType.DMA((N_DEV - 1,)),
                        pltpu.SemaphoreType.DMA],
    )(block, *after)


def _pair_sum(grad, recv, c_arr, name):
    _, _, rh, cols = grad.shape
    tr = _row_tile(rh, cols * grad.dtype.itemsize, STREAM_TILE // 2)

    def body(c_ref, g_ref, r_ref, o_ref):
        del c_ref
        o_ref[...] = (g_ref[...].astype(F32) + r_ref[...].astype(F32)).astype(o_ref.dtype)

    spec = pl.BlockSpec((None, tr, cols), lambda j, i, c_ref: (j, i, 0))
    return pl.pallas_call(
        body, name=name,
        grid_spec=pltpu.PrefetchScalarGridSpec(
            num_scalar_prefetch=1, grid=(N_CHIPS, rh // tr),
            in_specs=[pl.BlockSpec((None, None, tr, cols), lambda j, i, c_ref: (j, c_ref[0], i, 0)), spec],
            out_specs=spec),
        out_shape=jax.ShapeDtypeStruct(recv.shape, grad.dtype), compiler_params=_params(32),
    )(c_arr, grad, recv)


def _sum_by_chip(chip, p_ref, own_ref, o_ref):
    o_ref[...] = jnp.zeros_like(o_ref)
    for k in range(N_CHIPS):
        @pl.when(chip == k)
        def _():
            o_ref[...] += own_ref[...].astype(F32)

        @pl.when(chip != k)
        def _(k=k):
            o_ref[...] += p_ref[k].astype(F32)


def _chip_sum_all(parts, own, place_arr, name):
    _, nj, rh, cols = parts.shape
    tr = _row_tile(rh, cols * 4, STREAM_TILE // 2)

    def body(place_ref, p_ref, own_ref, o_ref):
        _sum_by_chip(place_ref[0], p_ref, own_ref, o_ref)

    return pl.pallas_call(
        body, name=name,
        grid_spec=pltpu.PrefetchScalarGridSpec(
            num_scalar_prefetch=1, grid=(nj, rh // tr),
            in_specs=[pl.BlockSpec((N_CHIPS, None, tr, cols), lambda j, i, place_ref: (0, j, i, 0)),
                      pl.BlockSpec((None, tr, cols), lambda j, i, place_ref: (j, i, 0))],
            out_specs=pl.BlockSpec((None, None, tr, cols), lambda j, i, place_ref: (j, place_ref[1], i, 0))),
        out_shape=jax.ShapeDtypeStruct((nj, 2, rh, cols), F32), compiler_params=_params(32),
    )(place_arr, parts, own)


def _chip_sum(parts, own, place_arr, name):
    _, rh, cols = parts.shape
    tr = _row_tile(rh, cols * 4, STREAM_TILE // 2)

    def body(place_ref, p_ref, own_ref, o_ref):
        _sum_by_chip(place_ref[0], p_ref, own_ref, o_ref)

    return pl.pallas_call(
        body, name=name,
        grid_spec=pltpu.PrefetchScalarGridSpec(
            num_scalar_prefetch=1, grid=(rh // tr,),
            in_specs=[pl.BlockSpec((N_CHIPS, tr, cols), lambda i, place_ref: (0, i, 0)),
                      pl.BlockSpec((None, tr, cols), lambda i, place_ref: (place_ref[0], i, 0))],
            out_specs=pl.BlockSpec((None, tr, cols), lambda i, place_ref: (place_ref[1], i, 0))),
        out_shape=jax.ShapeDtypeStruct((2, rh, cols), F32), compiler_params=_params(32),
    )(place_arr, parts, own)


def _adamw_math(w, g, m, v):
    m = ADAM_B1 * m + (1.0 - ADAM_B1) * g
    v = ADAM_B2 * v + (1.0 - ADAM_B2) * (g * g)
    m_hat = m / (1.0 - ADAM_B1 ** ADAM_STEP)
    v_hat = v / (1.0 - ADAM_B2 ** ADAM_STEP)
    delta = -ADAM_LR * (m_hat / (jnp.sqrt(v_hat) + ADAM_EPS) + ADAM_WD * w)
    return delta, m, v


def _adamw(w, g, m, v, name):
    rows, cols = w.shape
    tr = _row_tile(rows, cols * 4)

    def body(w_ref, g_ref, m_ref, v_ref, go_ref, d_ref, nm_ref, nv_ref):
        gv = g_ref[...]
        go_ref[...] = gv
        d_ref[...], nm_ref[...], nv_ref[...] = _adamw_math(w_ref[...], gv, m_ref[...], v_ref[...])

    spec = pl.BlockSpec((tr, cols), lambda i: (i, 0))
    return pl.pallas_call(
        body, name=name, grid=(rows // tr,), in_specs=[spec] * 4, out_specs=[spec] * 4,
        out_shape=[jax.ShapeDtypeStruct(w.shape, F32)] * 4, compiler_params=_params(32),
    )(w, g, m, v)


def _sum8_adamw_row(parts, w, m, v, name):
    cols = parts.shape[2]

    def body(p_ref, w_ref, m_ref, v_ref, g_ref, d_ref, nm_ref, nv_ref):
        g = p_ref[0, 0:1, :]
        for k in range(1, N_DEV):
            g = g + p_ref[k, 0:1, :]
        g_ref[...] = g
        d_ref[...], nm_ref[...], nv_ref[...] = _adamw_math(w_ref[...], g, m_ref[...], v_ref[...])

    return pl.pallas_call(
        body, name=name, out_shape=[jax.ShapeDtypeStruct((1, cols), F32)] * 4, compiler_params=_params(32),
    )(parts, w, m, v)


def _pack_rows(pieces, rows, name):
    cols = pieces[0].shape[1]
    n = len(pieces)

    def body(*refs):
        o_ref = refs[n]
        o_ref[...] = jnp.zeros_like(o_ref)
        at = 0
        for p_ref in refs[:n]:
            r = p_ref.shape[0]
            o_ref[at:at + r, :] = p_ref[...]
            at += r

    return pl.pallas_call(
        body, name=name, out_shape=jax.ShapeDtypeStruct((rows, cols), F32), compiler_params=_params(32),
    )(*pieces)


def kernel(x, norm_mix_g, w_in, conv_w, conv_b, lru_wa, lru_ba, lru_wx, lru_bx, lru_lambda, w_proj_attn, w_proj_lru, w_out, norm_mlp_g, w_up, w_down, norm_final_g, loss_target, m_norm_mix_g, m_w_in, m_conv_w, m_conv_b, m_lru_wa, m_lru_ba, m_lru_wx, m_lru_bx, m_lru_lambda, m_w_proj_attn, m_w_proj_lru, m_w_out, m_norm_mlp_g, m_w_up, m_w_down, m_norm_final_g, v_norm_mix_g, v_w_in, v_conv_w, v_conv_b, v_lru_wa, v_lru_ba, v_lru_wx, v_lru_bx, v_lru_lambda, v_w_proj_attn, v_w_proj_lru, v_w_out, v_norm_mlp_g, v_w_up, v_w_down, v_norm_final_g):
    s, d = x.shape[1], x.shape[2]
    ff = w_up.shape[2] * N_CHIPS
    heads = d // HEAD_DIM
    u = d // 4
    assert s % (max(DILATIONS) * ATTN_BLK) == 0 and d % (4 * HEAD_DIM) == 0 and ff == 4 * d and DILATIONS[0] == 1
    xs, target = _in_hbm([x[0], loss_target[0]])
    gf = norm_final_g.reshape(1, d)
    wa, wx = lru_wa[0], lru_wx[0]
    core = lax.axis_index("c").astype(jnp.int32)
    chip = (2 * lax.axis_index("x") + lax.axis_index("y")).astype(jnp.int32)
    cidx = core.reshape(1)
    chip_arr = chip.reshape(1)
    place_arr = jnp.stack([chip, core])
    slopes = jnp.broadcast_to(
        (2.0 ** (-8.0 * jnp.arange(1, heads + 1, dtype=F32) / heads))[:, None, None], (heads, 1, HEAD_DIM))

    big = _in_hbm([w_in[0], w_proj_attn[0], w_proj_lru[0], w_out[0], w_up[0], w_down[0]])
    names = ["w_in", "w_proj_attn", "w_proj_lru", "w_out", "w_up", "w_down"]
    cw_pad = jnp.pad(conv_w[0], ((0, 8 - CONV_TAPS), (0, 0)))
    cw_all = _all_gather8(cw_pad, "gather_conv_w")
    conv_w_full = jnp.concatenate([cw_all[2 * j] for j in range(N_CHIPS)], axis=1)
    (sem_a,), buf_a = _gather_start([_cast_shard(big[0], chip_arr, "cast_w_in")], [([0], NEIGHBOURS)], [cw_all],
                                    "gather_start_w_in")

    xn = _rms_fwd(xs, norm_mix_g, "norm_mix")
    flip_bits = (2, 1, 3)

    def w_in_view():
        return buf_a[0].reshape(N_CHIPS, d, N_SLOTS * u)

    proj = _proj_in_shard(xn, w_in_view(), chip_arr, None, "proj_in_own")
    bufs = [_cast_shard(w, chip_arr, "cast_" + nm) for w, nm in zip(big[1:], names[1:])]
    (fs, fr, rs, rr), buf_a = _gather_forward_relay(buf_a, sem_a[1], [proj] + bufs, "gather_forward_w_in")
    (sem_b, sem_c, sem_d), bufs = _gather_start(
        bufs, [([0, 1, 2], ALL_FLIPS), ([3], ALL_FLIPS), ([4], ALL_FLIPS)], buf_a, "gather_start_rest")
    buf_a = _gather_finish(buf_a, sem_a[0], fs, fr, [bufs[0]], "gather_finish_w_in", flips=NEIGHBOURS)
    for j in NEIGHBOURS:
        proj = _proj_in_shard(xn, w_in_view(), chip_arr ^ flip_bits[j], proj, "proj_in_from_%d" % j)
    (ds, dr), buf_a = _gather_forward_diag(buf_a, rr, [proj], "gather_forward_w_in_diag")
    buf_a = _gather_finish_diag(buf_a, rs, ds, dr, [], "gather_finish_w_in_diag")
    proj = _proj_in_shard(xn, w_in_view(), chip_arr ^ flip_bits[2], proj, "proj_in_from_2")
    w_in_g = w_in_view()
    proj_a = proj_b = proj
    y_attn, lse = _attn_fwd(proj_a, slopes)
    y_lru, h_lru = _lru_fwd(proj_b, conv_w_full, conv_b, wa, lru_ba, wx, lru_bx, lru_lambda)
    fsem_b, buf_b = _gather_forward(bufs[:3], sem_b[1], [y_attn, y_lru], "gather_forward_proj")
    fsem_c, buf_c = _gather_forward(bufs[3:4], sem_c[1], [buf_b[0]], "gather_forward_w_up")
    buf_b = _gather_finish(buf_b, sem_b[0], fsem_b[0], fsem_b[1], [buf_c[0]], "gather_finish_proj")
    wpa_g = buf_b[0].reshape(d, d)
    wpl_g = buf_b[1].reshape(d, d)
    wout_g = buf_b[2].reshape(d, d)

    tn = u
    sd_f32 = jax.ShapeDtypeStruct((s, d), F32)
    sd_bf16 = jax.ShapeDtypeStruct((s, d), BF16)
    col = pl.BlockSpec((s, tn), lambda i, j, k: (0, j))

    def slot(n):
        return pl.BlockSpec((None, s, tn), lambda i, j, k: (n, 0, j))

    p_attn = _mm_nn("proj_attn", y_attn, wpa_g, [], [], [sd_f32], [col], _store, tn)[0]

    def merge(acc, extras, outs):
        pa_ref, ga_ref, gl_ref = extras
        merged = _sigmoid(ga_ref[...]) * pa_ref[...] + _sigmoid(gl_ref[...]) * acc
        outs[0][...] = merged.astype(BF16)
        outs[1][...] = acc

    tn2 = max(HEAD_DIM, u // 2)
    col2 = pl.BlockSpec((s, tn2), lambda i, j, k: (0, j))

    def slot2(n):
        return pl.BlockSpec((None, s, tn2), lambda i, j, k: (n, 0, j))

    merged, p_lru = _mm_nn("proj_lru_merge", y_lru, wpl_g, [p_attn, proj, proj], [col2, slot2(5), slot2(6)],
                           [sd_bf16, sd_f32], [col2, col2], merge, tn2)

    def add_resid(acc, extras, outs):
        outs[0][...] = extras[0][...] + acc

    h1 = _mm_nn("w_out_resid", merged, wout_g, [xs], [col], [sd_f32], [col], add_resid, tn)[0]
    hn = _rms_fwd(h1, norm_mlp_g, "norm_mlp")
    fsem_d, buf_d = _gather_forward(bufs[4:], sem_d[1], [hn], "gather_forward_w_down")
    buf_c = _gather_finish(buf_c, sem_c[0], fsem_c[0], fsem_c[1], [buf_d[0]], "gather_finish_w_up")
    wup_g = buf_c[0].reshape(N_CHIPS, d, d)

    def relu_sq(acc, extras, outs):
        r = jnp.maximum(acc, 0.0)
        outs[0][...] = (r * r).astype(BF16)
        outs[1][...] = r.astype(BF16)

    sf_bf16 = jax.ShapeDtypeStruct((s, ff), BF16)
    hid, relu_up = _mm(
        "w_up_relu2", [hn, wup_g],
        [pl.BlockSpec((s, d), lambda i, j, k: (0, 0)),
         pl.BlockSpec((None, d, tn), lambda i, j, k: (j // 4, 0, j % 4))],
        [sf_bf16, sf_bf16], [col, col], (1, ff // tn, 1), NN, relu_sq)
    wdown_g = _gather_finish(buf_d, sem_d[0], fsem_d[0], fsem_d[1], [hid], "gather_finish_w_down")[0].reshape(ff, d)
    h2 = _mm(
        "w_down_resid", [hid, wdown_g, h1],
        [pl.BlockSpec((s, d), lambda i, j, k: (0, k)), pl.BlockSpec((d, tn), lambda i, j, k: (k, j)), col],
        [sd_f32], [col], (1, d // tn, ff // d), NN, add_resid, nk=ff // d, acc_shape=(s, tn))[0]
    loss_part, dh2, dh2_b, d_gf = _loss_head(h2, gf, target)
    loss = lax.psum(loss_part[0, 0], AXES)

    def relu_sq_bwd(acc, extras, outs):
        outs[0][...] = (acc * (2.0 * extras[0][...].astype(F32))).astype(BF16)

    dup = _mm_nt("d_hid", dh2_b, wdown_g, [relu_up], [col], [sf_bf16], [col], relu_sq_bwd, tn)[0]
    tok_d = pl.BlockSpec((s, d), lambda i, j: (0, 0))
    g_wdown = _mm_tn(
        "g_w_down", hid, dh2_b, pl.BlockSpec((s, d), lambda i, j: (0, i)),
        pl.BlockSpec((s, tn), lambda i, j: (0, j)), jax.ShapeDtypeStruct((ff, d), BF16),
        pl.BlockSpec((d, tn), lambda i, j: (i, j)), (ff // d, d // tn), d, tn, s)
    dhn = _mm(
        "d_hn", [dup, wup_g],
        [pl.BlockSpec((s, d), lambda i, j, k: (0, k)), pl.BlockSpec((None, tn, d), lambda i, j, k: (k, j, 0))],
        [sd_f32], [col], (1, d // tn, ff // d), NT, _store, nk=ff // d, acc_shape=(s, tn))[0]
    g_wup = _mm_tn(
        "g_w_up", hn, dup, tok_d, pl.BlockSpec((s, tn), lambda i, j: (0, j)),
        jax.ShapeDtypeStruct((N_CHIPS, d, d), BF16), pl.BlockSpec((None, d, tn), lambda i, j: (j // 4, 0, j % 4)),
        (1, ff // tn), d, tn, s)
    big_m = _in_hbm([m_w_in[0], m_w_proj_attn[0], m_w_proj_lru[0], m_w_out[0], m_w_up[0], m_w_down[0]])
    big_v = _in_hbm([v_w_in[0], v_w_proj_attn[0], v_w_proj_lru[0], v_w_out[0], v_w_up[0], v_w_down[0]])
    big_out = {}

    def reduce_begin(ids, gs, tag, everywhere=None):
        g4 = [g.reshape(N_CHIPS, 2, big[i].shape[0] // 2, big[i].shape[1]) for i, g in zip(ids, gs)]
        tags = [names[i] for i in ids]
        if everywhere is not None:
            g4.append(everywhere.reshape(N_CHIPS, 2, everywhere.shape[0] // (2 * N_CHIPS), everywhere.shape[1]))
            tags.append("small_" + tag)
        from_sibling = _pair_exchange(g4, "pair_exchange_" + tag)
        sums = [_pair_sum(g, r, cidx, "pair_sum_" + t) for t, g, r in zip(tags, g4, from_sibling)]
        to_all = () if everywhere is None else (len(ids),)
        return _chip_start(sums, "chip_start_" + tag, to_all), to_all

    def pair_begin(ids, gs, tag):
        g4 = [g.reshape(N_CHIPS, 2, big[i].shape[0] // 2, big[i].shape[1]) for i, g in zip(ids, gs)]
        return _pair_start(g4, "pair_start_" + tag)

    def reduce_begin_paired(ids, paired, after, tag):
        sems, g4, lands, _ = paired
        g4, from_sibling = _pair_wait(sems, g4, lands, after, "pair_wait_" + tag)
        sums = [_pair_sum(g, r, cidx, "pair_sum_" + names[i]) for i, g, r in zip(ids, g4, from_sibling)]
        return _chip_start(sums, "chip_start_" + tag), ()

    def reduce_end(ids, begun, after, tag):
        (sems, sums, lands, _), to_all = begun
        sums, lands = _chip_wait(sems, sums, lands, after, "chip_wait_" + tag, to_all)
        halves = [_chip_sum(p, own, place_arr, "chip_sum_" + names[i]) for i, p, own in zip(ids, lands, sums)]
        if to_all:
            halves.append(_chip_sum_all(lands[-1], sums[-1], place_arr, "chip_sum_small_" + tag))
        full = _half_exchange(halves, "half_exchange_" + tag)
        last = None
        for i, g in zip(ids, full):
            res = _adamw(big[i], g.reshape(big[i].shape), big_m[i], big_v[i], "adamw_" + names[i])
            big_out[names[i]] = tuple(a[None] for a in res)
            last = res[1]
        everywhere = full[-1].reshape(-1, full[-1].shape[-1]) if to_all else None
        return everywhere, last

    def after_token(a, begun):
        return a + begun[0][3][:1, :1]

    pair_mlp = pair_begin([4, 5], [g_wup, g_wdown], "mlp")
    dh1, dh1_b, d_gmlp = _rms_bwd(h1, norm_mlp_g + pair_mlp[3][:1, :1], dhn, dh2, "norm_mlp_bwd")

    g_wout = _mm_tn(
        "g_w_out", merged, dh1_b, tok_d, pl.BlockSpec((s, tn), lambda i, j: (0, j)),
        jax.ShapeDtypeStruct((d, d), BF16), pl.BlockSpec((d, tn), lambda i, j: (0, j)), (1, d // tn), d, tn, s)

    def merge_bwd(acc, extras, outs):
        pa_ref, pl_ref, ga_ref, gl_ref = extras
        sa, sl = _sigmoid(ga_ref[...]), _sigmoid(gl_ref[...])
        outs[0][...] = (acc * sa).astype(BF16)
        outs[1][...] = (acc * sl).astype(BF16)
        outs[2][0] = (acc * pa_ref[...] * (sa * (1.0 - sa))).astype(BF16)
        outs[2][1] = (acc * pl_ref[...] * (sl * (1.0 - sl))).astype(BF16)

    nb = N_SLOTS - N_QKV
    d_pa, d_pl, dproj_b = _mm_nt(
        "d_merged", dh1_b, wout_g, [p_attn, p_lru, proj, proj], [col2, col2, slot2(5), slot2(6)],
        [sd_bf16, sd_bf16, jax.ShapeDtypeStruct((nb, s, d), BF16)],
        [col2, col2, pl.BlockSpec((2, s, tn2), lambda i, j, k: (1, 0, j))], merge_bwd, tn2)
    red_mlp = reduce_begin_paired([4, 5], pair_mlp, [d_pa], "mlp")
    dy_attn = _mm_nt("d_y_attn", d_pa, wpa_g, [], [], [sd_f32], [col], _store, tn)[0]
    dy_lru = _mm_nt("d_y_lru", d_pl, wpl_g, [], [], [sd_f32], [col], _store, tn)[0]
    g_wpa = _mm_tn(
        "g_w_proj_attn", y_attn, d_pa, tok_d, pl.BlockSpec((s, tn), lambda i, j: (0, j)),
        jax.ShapeDtypeStruct((d, d), BF16), pl.BlockSpec((d, tn), lambda i, j: (0, j)), (1, d // tn), d, tn, s)
    g_wpl = _mm_tn(
        "g_w_proj_lru", y_lru, d_pl, tok_d, pl.BlockSpec((s, tn), lambda i, j: (0, j)),
        jax.ShapeDtypeStruct((d, d), BF16), pl.BlockSpec((d, tn), lambda i, j: (0, j)), (1, d // tn), d, tn, s)

    pair_proj = pair_begin([1, 2, 3], [g_wpa, g_wpl, g_wout], "proj")

    dproj_b, d_cw, d_cb, d_wa, d_ba, d_wx, d_bx, d_lam = _lru_bwd(
        proj_b, h_lru, dy_lru, conv_w_full, conv_b, wa, lru_ba, wx, lru_bx, lru_lambda + pair_proj[3][:1, :1],
        dproj_b)
    red_proj = reduce_begin_paired([1, 2, 3], pair_proj, [dproj_b], "proj")
    dproj_a = _attn_bwd(proj_a, after_token(slopes, red_proj), y_attn, lse, dy_attn)
    per = N_SLOTS
    g_win_shape = jax.ShapeDtypeStruct((N_CHIPS, d, N_SLOTS * u), BF16)

    def g_win_part(name, dproj, first, prev):
        n_units = 4 * dproj.shape[0]
        return _mm_tn(
            name, xn, dproj, tok_d, pl.BlockSpec((None, s, u), lambda i, j: (j // 4, 0, j % 4)),
            g_win_shape, pl.BlockSpec((None, d, u), lambda i, j: ((j + first) // per, 0, (j + first) % per)),
            (1, n_units), d, u, s, aliases=None if prev is None else {2: 0}, extra=prev)

    mat_rows = heads * HEAD_DIM * HEAD_DIM // d
    vec_names = ["norm_mix_g", "conv_b", "lru_ba", "lru_bx", "lru_lambda", "norm_mlp_g", "norm_final_g"]

    def pack(wa_, wx_, cw_, vecs, name):
        rows = [wa_.reshape(mat_rows, d), wx_.reshape(mat_rows, d), cw_] + [a.reshape(1, d) for a in vecs]
        n = sum(a.shape[0] for a in rows)
        return _pack_rows(rows, n + (-n % 64), name)

    zero_cw = jnp.zeros((CONV_TAPS, d), F32)
    small_g = pack(d_wa, d_wx, d_cw, [jnp.zeros((1, d), F32), d_cb, d_ba, d_bx, d_lam, d_gmlp, d_gf], "pack_small_g")
    g_win = g_win_part("g_w_in_qkv", dproj_a, 0, None)
    g_win = g_win_part("g_w_in_rest", dproj_b, 4 * N_QKV, g_win)
    red_in = reduce_begin([0], [g_win], "w_in", everywhere=small_g)
    dxn = _dxn(dproj_a, dproj_b, w_in_g, 2 * tn, [red_in[0][3]])
    grad_x, _, d_gmix = _rms_bwd(xs, norm_mix_g, dxn, dh1, "norm_mix_bwd")

    _, done = reduce_end([4, 5], red_mlp, [grad_x], "mlp")
    _, done = reduce_end([1, 2, 3], red_proj, [done], "proj")
    small_w = pack(wa, wx, zero_cw, [norm_mix_g, conv_b, lru_ba, lru_bx, lru_lambda, norm_mlp_g, norm_final_g],
                   "pack_small_w")
    small_m = pack(m_lru_wa[0], m_lru_wx[0], zero_cw,
                   [m_norm_mix_g, m_conv_b, m_lru_ba, m_lru_bx, m_lru_lambda, m_norm_mlp_g, m_norm_final_g],
                   "pack_small_m")
    small_v = pack(v_lru_wa[0], v_lru_wx[0], zero_cw,
                   [v_norm_mix_g, v_conv_b, v_lru_ba, v_lru_bx, v_lru_lambda, v_norm_mlp_g, v_norm_final_g],
                   "pack_small_v")
    small_sum, _ = reduce_end([0], red_in, [done, small_w, small_m, small_v], "w_in")
    small = _adamw(small_w, small_sum, small_m, small_v, "adamw_small")
    g_cw = lax.dynamic_slice(small_sum[2 * mat_rows:2 * mat_rows + CONV_TAPS], (0, chip * u), (CONV_TAPS, u))
    cw_out = _adamw(conv_w[0], g_cw, m_conv_w[0], v_conv_w[0], "adamw_conv_w")
    gmix_parts = _all_gather8(jnp.pad(d_gmix, ((0, 7), (0, 0))), "gather_gain_grad", [small[1]])
    gmix_out = _sum8_adamw_row(gmix_parts, norm_mix_g, m_norm_mix_g, v_norm_mix_g, "sum_adamw_norm_mix_g")

    def small_leaf(kind, name):
        a = small[kind]
        if name == "norm_mix_g":
            return gmix_out[kind]
        if name == "lru_wa":
            return a[0:mat_rows].reshape(lru_wa.shape)
        if name == "lru_wx":
            return a[mat_rows:2 * mat_rows].reshape(lru_wx.shape)
        if name == "conv_w":
            return cw_out[kind][None]
        row = a[2 * mat_rows + CONV_TAPS + vec_names.index(name)]
        return row if name == "norm_final_g" else row[None]

    order = ["norm_mix_g", "w_in", "conv_w", "conv_b", "lru_wa", "lru_ba", "lru_wx", "lru_bx", "lru_lambda",
             "w_proj_attn", "w_proj_lru", "w_out", "norm_mlp_g", "w_up", "w_down", "norm_final_g"]
    outs = [loss, grad_x[None]]
    for kind in range(4):
        for name in order:
            outs.append(big_out[name][kind] if name in big_out else small_leaf(kind, name))
    return tuple(outs)
```

```python
import functools

import jax
import jax.numpy as jnp
from jax import lax
from jax.experimental import pallas as pl
from jax.experimental.pallas import tpu as pltpu

F32 = jnp.float32
BF16 = jnp.bfloat16
MESH = pl.DeviceIdType.MESH
AXES = ("x", "y", "c")

N_CHIPS = 4
N_DEV = 8
HEAD_DIM = 128
ATTN_BLK = 128
DILATIONS = (1, 4, 16)
ATTN_UNROLL = 8
CONV_TAPS = 4
LRU_C = 8.0
EPS = 1e-6
N_SLOTS = 7
N_QKV = 3
VMEM_MIB = 2 ** 20
VMEM_V7X = 64 * VMEM_MIB
STREAM_TILE = 4 * VMEM_MIB

ADAM_LR = 0.001
ADAM_B1 = 0.9
ADAM_B2 = 0.999
ADAM_EPS = 1e-08
ADAM_WD = 0.01
ADAM_STEP = 10

NN = (((1,), (0,)), ((), ()))
NT = (((1,), (1,)), ((), ()))
TN = (((0,), (0,)), ((), ()))


def _params(vmem_mib=None, **kw):
    limit = None if vmem_mib is None else min(vmem_mib * VMEM_MIB, VMEM_V7X - 8 * VMEM_MIB)
    return pltpu.CompilerParams(vmem_limit_bytes=limit, **kw)


def _row_tile(rows, row_bytes, budget=VMEM_MIB):
    t = rows
    while t % 16 == 0 and t * row_bytes > budget:
        t //= 2
    return t


def _dot(a, b, dims):
    return lax.dot_general(a.astype(BF16), b.astype(BF16), dims, preferred_element_type=F32)


def _sigmoid(x):
    return jax.nn.sigmoid(x)


def _rms_fwd(x, g, name):
    s, d = x.shape
    tm = _row_tile(s, d * 4)

    def body(x_ref, g_ref, o_ref):
        xf = x_ref[...]
        r = lax.rsqrt(jnp.mean(xf * xf, axis=-1, keepdims=True) + EPS)
        o_ref[...] = (xf * r * g_ref[...]).astype(o_ref.dtype)

    return pl.pallas_call(
        body, name=name, grid=(s // tm,),
        in_specs=[pl.BlockSpec((tm, d), lambda i: (i, 0)), pl.BlockSpec((1, d), lambda i: (0, 0))],
        out_specs=pl.BlockSpec((tm, d), lambda i: (i, 0)),
        out_shape=jax.ShapeDtypeStruct((s, d), BF16), compiler_params=_params(32),
    )(x, g)


def _rms_bwd(x, g, dy, resid, name):
    s, d = x.shape
    tm = _row_tile(s, d * 4)

    def body(x_ref, g_ref, dy_ref, res_ref, dx_ref, dxb_ref, dg_ref):
        xf = x_ref[...]
        r = lax.rsqrt(jnp.mean(xf * xf, axis=-1, keepdims=True) + EPS)
        xh = xf * r
        dyv = dy_ref[...]
        dxh = dyv * g_ref[...]
        dx = r * (dxh - xh * jnp.mean(dxh * xh, axis=-1, keepdims=True)) + res_ref[...]
        dx_ref[...] = dx
        dxb_ref[...] = dx.astype(BF16)
        part = jnp.sum(dyv * xh, axis=0, keepdims=True)

        @pl.when(pl.program_id(0) == 0)
        def _():
            dg_ref[...] = part

        @pl.when(pl.program_id(0) > 0)
        def _():
            dg_ref[...] += part

    row = pl.BlockSpec((tm, d), lambda i: (i, 0))
    vec = pl.BlockSpec((1, d), lambda i: (0, 0))
    return pl.pallas_call(
        body, name=name, grid=(s // tm,),
        in_specs=[row, vec, row, row], out_specs=[row, row, vec],
        out_shape=[jax.ShapeDtypeStruct((s, d), F32), jax.ShapeDtypeStruct((s, d), BF16),
                   jax.ShapeDtypeStruct((1, d), F32)],
        compiler_params=_params(32),
    )(x, g, dy, resid)


def _loss_head(h2, g, target):
    s, d = h2.shape
    tm = _row_tile(s, d * 4)

    def body(x_ref, g_ref, t_ref, loss_ref, dx_ref, dxb_ref, dg_ref):
        xf = x_ref[...]
        gv = g_ref[...]
        r = lax.rsqrt(jnp.mean(xf * xf, axis=-1, keepdims=True) + EPS)
        xh = xf * r
        err = xh * gv - t_ref[...]
        part = jnp.sum(jnp.sum(err * err, axis=1, keepdims=True), axis=0, keepdims=True) * (0.5 / d)
        dyv = err * (1.0 / d)
        dxh = dyv * gv
        dx = r * (dxh - xh * jnp.mean(dxh * xh, axis=-1, keepdims=True))
        dx_ref[...] = dx
        dxb_ref[...] = dx.astype(BF16)
        dgp = jnp.sum(dyv * xh, axis=0, keepdims=True)

        @pl.when(pl.program_id(0) == 0)
        def _():
            dg_ref[...] = dgp
            loss_ref[...] = jnp.broadcast_to(part, loss_ref.shape)

        @pl.when(pl.program_id(0) > 0)
        def _():
            dg_ref[...] += dgp
            loss_ref[...] += jnp.broadcast_to(part, loss_ref.shape)

    row = pl.BlockSpec((tm, d), lambda i: (i, 0))
    vec = pl.BlockSpec((1, d), lambda i: (0, 0))
    return pl.pallas_call(
        body, name="loss_head", grid=(s // tm,),
        in_specs=[row, vec, row],
        out_specs=[pl.BlockSpec((8, 128), lambda i: (0, 0)), row, row, vec],
        out_shape=[jax.ShapeDtypeStruct((8, 128), F32), jax.ShapeDtypeStruct((s, d), F32),
                   jax.ShapeDtypeStruct((s, d), BF16), jax.ShapeDtypeStruct((1, d), F32)],
        compiler_params=_params(32),
    )(h2, g, target)


def _mm(name, operands, in_specs, out_shape, out_specs, grid, dims, epilogue, nk=1, acc_shape=None,
        vmem_mib=56, aliases=None):
    n_in = len(operands)
    n_out = len(out_shape)

    def body(*refs):
        a_ref, b_ref = refs[0], refs[1]
        extras = refs[2:n_in]
        outs = refs[n_in:n_in + n_out]

        def prod():
            return _dot(a_ref[...], b_ref[...], dims)

        if nk == 1:
            epilogue(prod(), extras, outs)
        else:
            acc = refs[n_in + n_out]
            k = pl.program_id(2)

            @pl.when(k == 0)
            def _():
                acc[...] = prod()

            @pl.when(k > 0)
            def _():
                acc[...] += prod()

            @pl.when(k == nk - 1)
            def _():
                epilogue(acc[...], extras, outs)

    scratch = [] if nk == 1 else [pltpu.VMEM(acc_shape, F32)]
    return pl.pallas_call(
        body, name=name, grid=grid, in_specs=in_specs, out_specs=out_specs, out_shape=out_shape,
        scratch_shapes=scratch, input_output_aliases=aliases or {},
        compiler_params=_params(vmem_mib),
    )(*operands)


def _store(acc, extras, outs):
    outs[0][...] = acc.astype(outs[0].dtype)


def _proj_in_shard(xn, w_in_g, shard_arr, prev, name):
    s, d = xn.shape
    u = d // 4
    per = N_SLOTS
    n_prev = 0 if prev is None else 1

    def body(sh_ref, x_ref, w_ref, *rest):
        del sh_ref
        rest[n_prev][...] = _dot(x_ref[...], w_ref[...], NN)

    def out_map(j, sh_ref):
        unit = per * sh_ref[0] + j
        return (unit // 4, 0, unit % 4)

    return pl.pallas_call(
        body, name=name,
        grid_spec=pltpu.PrefetchScalarGridSpec(
            num_scalar_prefetch=1, grid=(per,),
            in_specs=[pl.BlockSpec((s, d), lambda j, sh_ref: (0, 0)),
                      pl.BlockSpec((None, d, u), lambda j, sh_ref: (sh_ref[0], 0, j))] + [HBM] * n_prev,
            out_specs=pl.BlockSpec((None, s, u), out_map)),
        out_shape=jax.ShapeDtypeStruct((N_SLOTS, s, d), F32),
        input_output_aliases={3: 0} if n_prev else {},
        compiler_params=_params(56),
    )(shard_arr, xn, w_in_g, *([] if prev is None else [prev]))


def _mm_nn(name, a, b, extras, extra_specs, out_shape, out_specs, epilogue, tn, aliases=None):
    s, kdim = a.shape
    n = b.shape[1]
    return _mm(
        name, [a, b] + list(extras),
        [pl.BlockSpec((s, kdim), lambda i, j, k: (0, 0)), pl.BlockSpec((kdim, tn), lambda i, j, k: (0, j))]
        + list(extra_specs),
        out_shape, out_specs, (1, n // tn, 1), NN, epilogue, aliases=aliases)


def _mm_nt(name, a, b, extras, extra_specs, out_shape, out_specs, epilogue, tn, aliases=None):
    s, kdim = a.shape
    n = b.shape[0]
    return _mm(
        name, [a, b] + list(extras),
        [pl.BlockSpec((s, kdim), lambda i, j, k: (0, 0)), pl.BlockSpec((tn, kdim), lambda i, j, k: (j, 0))]
        + list(extra_specs),
        out_shape, out_specs, (1, n // tn, 1), NT, epilogue, aliases=aliases)


def _mm_tn(name, a, b, a_spec, b_spec, out_shape, out_spec, grid, m, tn, s, aliases=None, extra=None):
    ch = 256
    n_in = 2 if extra is None else 3

    def body(*refs):
        a_ref, b_ref = refs[0], refs[1]
        o_ref, at_ref = refs[n_in], refs[n_in + 1]

        @pl.when(pl.program_id(1) == 0)
        def _():
            for c0 in range(0, s, ch):
                at_ref[:, c0:c0 + ch] = a_ref[c0:c0 + ch, :].astype(F32).T.astype(BF16)

        o_ref[...] = _dot(at_ref[...], b_ref[...], NN).astype(o_ref.dtype)

    operands = [a, b] + ([] if extra is None else [extra])
    in_specs = [a_spec, b_spec] + ([] if extra is None else [pl.BlockSpec(memory_space=pl.ANY)])
    return pl.pallas_call(
        body, name=name, grid=grid, in_specs=in_specs, out_specs=out_spec, out_shape=out_shape,
        scratch_shapes=[pltpu.VMEM((m, s), BF16)], input_output_aliases=aliases or {},
        compiler_params=_params(56),
    )(*operands)


def _dxn(dproj_a, dproj_b, w_in_g, tn, after):
    n_a, s, d = dproj_a.shape
    u = d // 4
    ua = 4 * n_a
    nk = 4 * N_SLOTS
    per = N_SLOTS

    def body(a_ref, b_ref, w_ref, *rest):
        o_ref = rest[len(after)]
        k = pl.program_id(2)

        @pl.when(k == 0)
        def _():
            o_ref[...] = jnp.zeros_like(o_ref)

        @pl.when(k < ua)
        def _():
            o_ref[...] += _dot(a_ref[...], w_ref[...], NT)

        @pl.when(k >= ua)
        def _():
            o_ref[...] += _dot(b_ref[...], w_ref[...], NT)

    def a_map(i, j, k):
        kk = jnp.minimum(k, ua - 1)
        return (kk // 4, 0, kk % 4)

    def b_map(i, j, k):
        kk = jnp.maximum(k - ua, 0)
        return (kk // 4, 0, kk % 4)

    return pl.pallas_call(
        body, name="dxn", grid=(1, d // tn, nk),
        in_specs=[pl.BlockSpec((None, s, u), a_map), pl.BlockSpec((None, s, u), b_map),
                  pl.BlockSpec((None, tn, u), lambda i, j, k: (k // per, j, k % per))] + [HBM] * len(after),
        out_specs=pl.BlockSpec((s, tn), lambda i, j, k: (0, j)),
        out_shape=jax.ShapeDtypeStruct((s, d), F32),
        compiler_params=_params(48),
    )(dproj_a, dproj_b, w_in_g, *after)


def _attn_masks(slope, dil):
    ii = lax.broadcasted_iota(jnp.int32, (ATTN_BLK, 2 * ATTN_BLK), 0)
    jj = lax.broadcasted_iota(jnp.int32, (ATTN_BLK, 2 * ATTN_BLK), 1)
    diff = ATTN_BLK + ii - jj
    band = (diff >= 0) & (diff <= ATTN_BLK)
    bias = -(slope * float(dil)) * diff.astype(F32)
    return band, bias, jj


def _attn_window(t, nblk):
    cur = pl.ds(pl.multiple_of(t * ATTN_BLK, ATTN_BLK), ATTN_BLK)
    prev = pl.ds(pl.multiple_of(jnp.maximum(t - 1, 0) * ATTN_BLK, ATTN_BLK), ATTN_BLK)
    first = jnp.where(t % nblk == 0, ATTN_BLK, 0)
    return prev, cur, first


def _unrolled_loop(n, step, init, unroll=ATTN_UNROLL):
    def trip(i, carry):
        for k in range(unroll):
            carry = step(i * unroll + k, carry)
        return carry

    return lax.fori_loop(0, n // unroll, trip, init)


def _streams(pairs, dil, s):
    if dil == 1:
        return [src for _, src in pairs]
    seg = s // dil
    for dst, src in pairs:
        for r in range(dil):
            dst[r * seg:(r + 1) * seg, :] = src[pl.ds(r, seg, stride=dil), :].astype(dst.dtype)
    return [dst for dst, _ in pairs]


def _attn_fwd(proj_a, slopes):
    _, s, d = proj_a.shape
    heads = d // HEAD_DIM
    scale = HEAD_DIM ** -0.5
    n_t = s // ATTN_BLK
    ng = len(DILATIONS)

    def body(q_ref, k_ref, v_ref, sl_ref, o_ref, lse_ref, qd, kd, vd, od, ld, og, lg):
        slope = sl_ref[...][:, :1]
        for g, dil in enumerate(DILATIONS):
            nblk = s // dil // ATTN_BLK
            qs, ks, vs = _streams([(qd, q_ref), (kd, k_ref), (vd, v_ref)], dil, s)
            o_t, l_t = (og.at[g], lg.at[g]) if dil == 1 else (od, ld)
            band, bias, jj = _attn_masks(slope, dil)

            def blk(t, carry, nblk=nblk, band=band, bias=bias, jj=jj, qs=qs, ks=ks, vs=vs, o_t=o_t, l_t=l_t):
                prev, cur, first = _attn_window(t, nblk)
                kk = jnp.concatenate([ks[prev, :], ks[cur, :]], axis=0)
                vv = jnp.concatenate([vs[prev, :], vs[cur, :]], axis=0)
                sc = _dot(qs[cur, :], kk, NT) * scale + bias
                sc = jnp.where(band & (jj >= first), sc, -jnp.inf)
                m = jnp.max(sc, axis=1, keepdims=True)
                p = jnp.exp(sc - m)
                l = jnp.sum(p, axis=1, keepdims=True)
                o_t[cur, :] = _dot(p, vv, NN) / l
                l_t[cur, :] = jnp.broadcast_to(m + jnp.log(l), (ATTN_BLK, HEAD_DIM))
                return carry

            _unrolled_loop(n_t, blk, 0)
            seg = s // dil
            if dil > 1:
                for r in range(dil):
                    og[g, pl.ds(r, seg, stride=dil), :] = od[r * seg:(r + 1) * seg, :]
                    lg[g, pl.ds(r, seg, stride=dil), :] = ld[r * seg:(r + 1) * seg, :]

        ch = 256

        def combine(c, carry):
            rows = pl.ds(pl.multiple_of(c * ch, ch), ch)
            ls = [lg[g, rows, :] for g in range(ng)]
            mx = functools.reduce(jnp.maximum, ls)
            es = [jnp.exp(x - mx) for x in ls]
            den = functools.reduce(jnp.add, es)
            num = functools.reduce(jnp.add, [es[g] * og[g, rows, :] for g in range(ng)])
            o_ref[rows, :] = (num / den).astype(o_ref.dtype)
            lse_ref[rows, :] = mx + jnp.log(den)
            return carry

        lax.fori_loop(0, s // ch, combine, 0)

    def col(slot):
        return pl.BlockSpec((None, s, HEAD_DIM), lambda h: (slot, 0, h))

    head = pl.BlockSpec((s, HEAD_DIM), lambda h: (0, h))
    return pl.pallas_call(
        body, name="attn_fwd", grid=(heads,),
        in_specs=[col(0), col(1), col(2), pl.BlockSpec((None, 1, HEAD_DIM), lambda h: (h, 0, 0))],
        out_specs=[head, head],
        out_shape=[jax.ShapeDtypeStruct((s, d), BF16), jax.ShapeDtypeStruct((s, d), F32)],
        scratch_shapes=[pltpu.VMEM((s, HEAD_DIM), BF16)] * 3 + [pltpu.VMEM((s, HEAD_DIM), F32)] * 2
        + [pltpu.VMEM((ng, s, HEAD_DIM), F32)] * 2,
        compiler_params=_params(40),
    )(proj_a, proj_a, proj_a, slopes)


def _attn_bwd(proj_a, slopes, y_attn, lse, dy):
    _, s, d = proj_a.shape
    heads = d // HEAD_DIM
    scale = HEAD_DIM ** -0.5
    n_t = s // ATTN_BLK

    def body(q_ref, k_ref, v_ref, sl_ref, o_ref, lse_ref, dy_ref, out_ref,
             qd, kd, vd, dod, lsd, dld, delta, dqd, dkd, dvd, dqa, dka, dva):
        slope = sl_ref[...][:, :1]
        dyv = dy_ref[...]
        delta[...] = jnp.broadcast_to(
            jnp.sum(dyv * o_ref[...].astype(F32), axis=1, keepdims=True), (s, HEAD_DIM))
        for g, dil in enumerate(DILATIONS):
            nblk = s // dil // ATTN_BLK
            seg = s // dil
            qs, ks, vs, dos, lss, dls = _streams(
                [(qd, q_ref), (kd, k_ref), (vd, v_ref), (dod, dy_ref), (lsd, lse_ref), (dld, delta)], dil, s)
            dq_t, dk_t, dv_t = (dqa, dka, dva) if dil == 1 else (dqd, dkd, dvd)
            band, bias, jj = _attn_masks(slope, dil)

            def blk(t, carry, nblk=nblk, band=band, bias=bias, jj=jj, qs=qs, ks=ks, vs=vs, dos=dos, lss=lss,
                    dls=dls, dq_t=dq_t, dk_t=dk_t, dv_t=dv_t):
                ck, cv = carry
                prev, cur, first = _attn_window(t, nblk)
                q = qs[cur, :]
                do = dos[cur, :]
                lse_b = lss[cur, :]
                dl_b = dls[cur, :]
                kk = jnp.concatenate([ks[prev, :], ks[cur, :]], axis=0)
                vv = jnp.concatenate([vs[prev, :], vs[cur, :]], axis=0)
                sc = _dot(q, kk, NT) * scale + bias
                p = jnp.where(band & (jj >= first), jnp.exp(sc - jnp.concatenate([lse_b, lse_b], axis=1)), 0.0)
                dp = _dot(do, vv, NT)
                ds = p * (dp - jnp.concatenate([dl_b, dl_b], axis=1))
                dv_b = _dot(p, do, TN)
                dk_b = _dot(ds, q, TN) * scale
                dq_t[cur, :] = _dot(ds, kk, NN) * scale
                done = pl.ds(pl.multiple_of(jnp.where(t == 0, n_t, t - 1) * ATTN_BLK, ATTN_BLK), ATTN_BLK)
                dk_t[done, :] = ck + dk_b[:ATTN_BLK]
                dv_t[done, :] = cv + dv_b[:ATTN_BLK]
                return dk_b[ATTN_BLK:], dv_b[ATTN_BLK:]

            zero = jnp.zeros((ATTN_BLK, HEAD_DIM), F32)
            ck, cv = _unrolled_loop(n_t, blk, (zero, zero))
            dk_t[(n_t - 1) * ATTN_BLK:n_t * ATTN_BLK, :] = ck
            dv_t[(n_t - 1) * ATTN_BLK:n_t * ATTN_BLK, :] = cv
            if dil > 1:
                for acc, part in ((dqa, dqd), (dka, dkd), (dva, dvd)):
                    for r in range(dil):
                        acc[pl.ds(r, seg, stride=dil), :] += part[r * seg:(r + 1) * seg, :]
        out_ref[0] = dqa[...].astype(out_ref.dtype)
        out_ref[1] = dka[0:s, :].astype(out_ref.dtype)
        out_ref[2] = dva[0:s, :].astype(out_ref.dtype)

    def col(slot):
        return pl.BlockSpec((None, s, HEAD_DIM), lambda h: (slot, 0, h))

    head = pl.BlockSpec((s, HEAD_DIM), lambda h: (0, h))
    return pl.pallas_call(
        body, name="attn_bwd", grid=(heads,),
        in_specs=[col(0), col(1), col(2), pl.BlockSpec((None, 1, HEAD_DIM), lambda h: (h, 0, 0)),
                  head, head, head],
        out_specs=pl.BlockSpec((N_QKV, s, HEAD_DIM), lambda h: (0, 0, h)),
        out_shape=jax.ShapeDtypeStruct((N_QKV, s, d), BF16),
        scratch_shapes=[pltpu.VMEM((s, HEAD_DIM), BF16)] * 4 + [pltpu.VMEM((s, HEAD_DIM), F32)] * 4
        + [pltpu.VMEM((s + ATTN_BLK, HEAD_DIM), F32)] * 2 + [pltpu.VMEM((s, HEAD_DIM), F32)]
        + [pltpu.VMEM((s + ATTN_BLK, HEAD_DIM), F32)] * 2,
        compiler_params=_params(48),
    )(proj_a, proj_a, proj_a, slopes, y_attn, lse, dy)


def _expm1(x):
    small = x * (1.0 + x * (0.5 + x * (1.0 / 6.0 + x * (1.0 / 24.0 + x * (1.0 / 120.0)))))
    return jnp.where(jnp.abs(x) < 0.1, small, jnp.exp(x) - 1.0)


def _softplus(x):
    return jnp.maximum(x, 0.0) + jnp.log1p(jnp.exp(-jnp.abs(x)))


GELU_K = 0.7978845608028654
GELU_C = 0.044715


def _gelu(x):
    t = jnp.tanh(GELU_K * (x + GELU_C * x * x * x))
    return 0.5 * x * (1.0 + t), t


def _gelu_grad(x, t):
    return 0.5 * (1.0 + t) + 0.5 * x * (1.0 - t * t) * GELU_K * (1.0 + 3.0 * GELU_C * x * x)


def _lru_gates(xc, wa, ba, wx, bx, sp):
    r = _sigmoid(_dot(xc, wa, NN) + ba)
    ig = _sigmoid(_dot(xc, wx, NN) + bx)
    log_a = -LRU_C * r * sp
    a = jnp.exp(log_a)
    mult = jnp.sqrt(-_expm1(2.0 * log_a))
    return r, ig, a, mult


def _scan_fwd(a, u, tt):
    row = lax.broadcasted_iota(jnp.int32, a.shape, 0)
    sh = 1
    while sh < tt:
        keep = row >= sh
        a_s = jnp.where(keep, pltpu.roll(a, sh, 0), 1.0)
        u_s = jnp.where(keep, pltpu.roll(u, sh, 0), 0.0)
        u = a * u_s + u
        a = a * a_s
        sh *= 2
    return a, u


def _scan_bwd(b, g, tt):
    row = lax.broadcasted_iota(jnp.int32, b.shape, 0)
    sh = 1
    while sh < tt:
        keep = row < tt - sh
        b_s = jnp.where(keep, pltpu.roll(b, tt - sh, 0), 1.0)
        g_s = jnp.where(keep, pltpu.roll(g, tt - sh, 0), 0.0)
        g = g + b * g_s
        b = b * b_s
        sh *= 2
    return b, g


def _conv_rows(xpad_ref, cw, cb, s):
    acc = cb
    for j in range(CONV_TAPS):
        off = 8 - (CONV_TAPS - 1) + j
        acc = acc + cw[j:j + 1, :] * xpad_ref[off:off + s, :]
    return acc


LRU_TILE = 128
LRU_UNROLL = 4


def _lru_specs(s, d):
    heads = d // HEAD_DIM

    def col(slot):
        return pl.BlockSpec((None, s, HEAD_DIM), lambda h: (slot, 0, h))

    vec = pl.BlockSpec((1, HEAD_DIM), lambda h: (0, h))
    mat = pl.BlockSpec((None, HEAD_DIM, HEAD_DIM), lambda h: (h, 0, 0))
    cw = pl.BlockSpec((8, HEAD_DIM), lambda h: (0, h))
    head = pl.BlockSpec((s, HEAD_DIM), lambda h: (0, h))
    return heads, col, vec, mat, cw, head


def _lru_fwd(proj_b, conv_w, conv_b, wa, ba, wx, bx, lam):
    _, s, d = proj_b.shape
    heads, col, vec, mat, cws, head = _lru_specs(s, d)
    tt = LRU_TILE

    def body(xr_ref, xg_ref, cw_ref, cb_ref, wa_ref, ba_ref, wx_ref, bx_ref, lam_ref, y_ref, h_ref, xpad, xc_s):
        xpad[0:8, :] = jnp.zeros((8, HEAD_DIM), F32)
        xpad[8:8 + s, :] = xr_ref[...]
        xc_s[...] = _conv_rows(xpad, cw_ref[...], cb_ref[...], s)
        sp = _softplus(-lam_ref[...])
        wav, wxv, bav, bxv = wa_ref[...], wx_ref[...], ba_ref[...], bx_ref[...]

        def tile(i, hc):
            rows = pl.ds(pl.multiple_of(i * tt, tt), tt)
            xc = xc_s[rows, :]
            _, ig, a, mult = _lru_gates(xc, wav, bav, wxv, bxv, sp)
            pa, hl = _scan_fwd(a, mult * (ig * xc), tt)
            h = hl + pa * hc
            h_ref[rows, :] = h
            gel, _ = _gelu(xg_ref[rows, :])
            y_ref[rows, :] = (h * gel).astype(y_ref.dtype)
            return h[tt - 1:tt, :]

        _unrolled_loop(s // tt, tile, jnp.zeros((1, HEAD_DIM), F32), LRU_UNROLL)

    return pl.pallas_call(
        body, name="lru_fwd", grid=(heads,),
        in_specs=[col(N_QKV), col(N_QKV + 1), cws, vec, mat, vec, mat, vec, vec],
        out_specs=[head, head],
        out_shape=[jax.ShapeDtypeStruct((s, d), BF16), jax.ShapeDtypeStruct((s, d), F32)],
        scratch_shapes=[pltpu.VMEM((s + 8, HEAD_DIM), F32), pltpu.VMEM((s, HEAD_DIM), F32)],
        compiler_params=_params(32),
    )(proj_b, proj_b, conv_w, conv_b, wa, ba, wx, bx, lam)


def _lru_bwd(proj_b, h_lru, dy, conv_w, conv_b, wa, ba, wx, bx, lam, dproj_b):
    _, s, d = proj_b.shape
    heads, col, vec, mat, cws, head = _lru_specs(s, d)
    tt = LRU_TILE
    n_t = s // tt

    def body(xr_ref, xg_ref, h_ref, dy_ref, cw_ref, cb_ref, wa_ref, ba_ref, wx_ref, bx_ref, lam_ref, alias_ref,
             out_ref, dcw_ref, dcb_ref, dwa_ref, dba_ref, dwx_ref, dbx_ref, dlam_ref, xpad, xc_s, dxc_s):
        del alias_ref
        xpad[0:8, :] = jnp.zeros((8, HEAD_DIM), F32)
        xpad[8:8 + s, :] = xr_ref[...]
        cwv = cw_ref[...]
        xc_s[...] = _conv_rows(xpad, cwv, cb_ref[...], s)
        dxc_s[s:s + 8, :] = jnp.zeros((8, HEAD_DIM), F32)
        lamv = lam_ref[...]
        sp = _softplus(-lamv)
        wav, wxv, bav, bxv = wa_ref[...], wx_ref[...], ba_ref[...], bx_ref[...]
        dwa_ref[...] = jnp.zeros_like(dwa_ref)
        dwx_ref[...] = jnp.zeros_like(dwx_ref)
        zero = jnp.zeros((1, HEAD_DIM), F32)
        row = lax.broadcasted_iota(jnp.int32, (tt, HEAD_DIM), 0)

        def tile(it, carry):
            dh_next, a_next, dba, dbx, dsp, dcb = carry
            i = n_t - 1 - it
            t0 = pl.multiple_of(i * tt, tt)
            rows = pl.ds(t0, tt)
            xc = xc_s[rows, :]
            r, ig, a, mult = _lru_gates(xc, wav, bav, wxv, bxv, sp)
            h = h_ref[rows, :]
            before = h_ref[pl.ds(pl.multiple_of(jnp.maximum(t0 - 8, 0), 8), 8), :][7:8, :]
            before = before * (i > 0).astype(F32)
            h_prev = jnp.where(row == 0, before, pltpu.roll(h, 1, 0))
            xg = xg_ref[rows, :]
            dyv = dy_ref[rows, :]
            gel, th = _gelu(xg)
            out_ref[1, rows, :] = (dyv * h * _gelu_grad(xg, th)).astype(out_ref.dtype)
            b = jnp.where(row == tt - 1, a_next, pltpu.roll(a, tt - 1, 0))
            pb, z = _scan_bwd(b, dyv * gel, tt)
            dh = z + pb * dh_next
            da = dh * h_prev
            dmult = dh * (ig * xc)
            dig = dh * (mult * xc)
            dla = da * a - dmult * (a * a / mult)
            dzr = dla * (-LRU_C * sp) * (r * (1.0 - r))
            dzx = dig * (ig * (1.0 - ig))
            dxc = dh * (mult * ig) + _dot(dzr, wav, NT) + _dot(dzx, wxv, NT)
            dxc_s[rows, :] = dxc
            dwa_ref[...] += _dot(xc, dzr, TN)
            dwx_ref[...] += _dot(xc, dzx, TN)
            return (dh[0:1, :], a[0:1, :],
                    dba + jnp.sum(dzr, axis=0, keepdims=True),
                    dbx + jnp.sum(dzx, axis=0, keepdims=True),
                    dsp + jnp.sum(dla * (-LRU_C * r), axis=0, keepdims=True),
                    dcb + jnp.sum(dxc, axis=0, keepdims=True))

        _, _, dba, dbx, dsp, dcb = _unrolled_loop(n_t, tile, (zero, zero, zero, zero, zero, zero), LRU_UNROLL)
        dba_ref[...] = dba
        dbx_ref[...] = dbx
        dcb_ref[...] = dcb
        dlam_ref[...] = -dsp * _sigmoid(-lamv)
        dxr = jnp.zeros((s, HEAD_DIM), F32)
        for j in range(CONV_TAPS):
            back = CONV_TAPS - 1 - j
            off = 8 - back
            dcw_ref[j:j + 1, :] = jnp.sum(dxc_s[0:s, :] * xpad[off:off + s, :], axis=0, keepdims=True)
            dxr = dxr + cwv[j:j + 1, :] * dxc_s[back:back + s, :]
        out_ref[0] = dxr.astype(out_ref.dtype)

    return pl.pallas_call(
        body, name="lru_bwd", grid=(heads,),
        in_specs=[col(N_QKV), col(N_QKV + 1), head, head, cws, vec, mat, vec, mat, vec, vec,
                  pl.BlockSpec(memory_space=pl.ANY)],
        out_specs=[pl.BlockSpec((2, s, HEAD_DIM), lambda h: (0, 0, h)),
                   pl.BlockSpec((CONV_TAPS, HEAD_DIM), lambda h: (0, h)), vec, mat, vec, mat, vec, vec],
        out_shape=[jax.ShapeDtypeStruct(dproj_b.shape, dproj_b.dtype),
                   jax.ShapeDtypeStruct((CONV_TAPS, d), F32), jax.ShapeDtypeStruct((1, d), F32),
                   jax.ShapeDtypeStruct(wa.shape, F32), jax.ShapeDtypeStruct((1, d), F32),
                   jax.ShapeDtypeStruct(wx.shape, F32), jax.ShapeDtypeStruct((1, d), F32),
                   jax.ShapeDtypeStruct((1, d), F32)],
        scratch_shapes=[pltpu.VMEM((s + 8, HEAD_DIM), F32), pltpu.VMEM((s, HEAD_DIM), F32),
                        pltpu.VMEM((s + 8, HEAD_DIM), F32)],
        input_output_aliases={11: 0},
        compiler_params=_params(32),
    )(proj_b, proj_b, h_lru, dy, conv_w, conv_b, wa, ba, wx, bx, lam, dproj_b)


def _place():
    x, y, c = (lax.axis_index(n) for n in AXES)
    return x, y, c


def _other_chips(x, y):
    return [(1 - x, y), (x, 1 - y), (1 - x, 1 - y)]


HBM = pl.BlockSpec(memory_space=pl.ANY)


def _cast_shard(w, chip_arr, name):
    r, cols = w.shape
    rh = r // 2
    tr = _row_tile(rh, cols * 4, STREAM_TILE)
    nt = rh // tr

    def body(chip_ref, w_ref, o_ref):
        del chip_ref
        o_ref[...] = w_ref[...].astype(BF16)

    return pl.pallas_call(
        body, name=name,
        grid_spec=pltpu.PrefetchScalarGridSpec(
            num_scalar_prefetch=1, grid=(2, nt),
            in_specs=[pl.BlockSpec((tr, cols), lambda h, i, chip_ref: (h * nt + i, 0))],
            out_specs=pl.BlockSpec((None, None, tr, cols), lambda h, i, chip_ref: (chip_ref[0], h, i, 0))),
        out_shape=jax.ShapeDtypeStruct((N_CHIPS, 2, rh, cols), BF16), compiler_params=_params(32),
    )(chip_arr, w)


HBM_SPEC = pl.BlockSpec(memory_space=pltpu.HBM)
SEM_SPEC = pl.BlockSpec(memory_space=pltpu.SEMAPHORE)
EFFECT = pltpu.SideEffectType.DATAFLOW_SIDE_EFFECTING
TOKEN = jax.ShapeDtypeStruct((8, 128), F32)
TOKEN_SPEC = pl.BlockSpec(memory_space=pltpu.VMEM)


def _in_hbm(arrays):
    return [pltpu.with_memory_space_constraint(a, pltpu.HBM) for a in arrays]


def _hbm_like(arrays):
    return [pltpu.HBM(a.shape, a.dtype) for a in arrays]


def _sems(n):
    return pltpu.SemaphoreType.DMA((n,))


def _remote(src, dst, send_sem, recv_sem, to):
    return pltpu.make_async_remote_copy(src_ref=src, dst_ref=dst, send_sem=send_sem, recv_sem=recv_sem,
                                        device_id=to, device_id_type=MESH)


ALL_FLIPS = (0, 1, 2)


def _gather_start(bufs, groups, after, name):
    n = len(bufs)
    ng = len(groups)

    def body(*refs):
        ins = refs[:n]
        sems = refs[n + len(after):n + len(after) + 2 * ng]
        x, y, c = _place()
        me = 2 * x + y
        chips = _other_chips(x, y)
        for g, (ws, flips) in enumerate(groups):
            for i, w in enumerate(ws):
                for jj, j in enumerate(flips):
                    k = len(flips) * i + jj
                    mine = ins[w].at[me, c]
                    _remote(mine, mine, sems[2 * g].at[k], sems[2 * g + 1].at[k], (*chips[j], c)).start()

    sem_shapes = []
    for ws, flips in groups:
        sem_shapes += [_sems(len(flips) * len(ws))] * 2
    res = pl.pallas_call(
        body, name=name, in_specs=[HBM_SPEC] * n + [HBM] * len(after),
        out_specs=[SEM_SPEC] * (2 * ng) + [HBM_SPEC] * n, out_shape=sem_shapes + _hbm_like(bufs),
        input_output_aliases={w: 2 * ng + w for w in range(n)},
        compiler_params=pltpu.CompilerParams(has_side_effects=EFFECT),
    )(*_in_hbm(bufs), *after)
    return [(res[2 * g], res[2 * g + 1]) for g in range(ng)], list(res[2 * ng:])


def _gather_forward(bufs, recv, after, name, flips=ALL_FLIPS):
    m = len(bufs)
    nf = len(flips)

    def body(*refs):
        ins, recv_in = refs[:m], refs[m]
        fsend, frecv = refs[m + 1 + len(after)], refs[m + 2 + len(after)]
        x, y, c = _place()
        chips = _other_chips(x, y)
        for jj, j in enumerate(flips):
            cx, cy = chips[j]
            for i in range(m):
                landed = ins[i].at[2 * cx + cy, c]
                k = nf * i + jj
                _remote(landed, landed, fsend.at[k], recv_in.at[k], (cx, cy, c)).wait_recv()
                _remote(landed, landed, fsend.at[k], frecv.at[k], (x, y, 1 - c)).start()

    res = pl.pallas_call(
        body, name=name, in_specs=[HBM_SPEC] * m + [SEM_SPEC] + [HBM] * len(after),
        out_specs=[SEM_SPEC, SEM_SPEC] + [HBM_SPEC] * m, out_shape=[_sems(nf * m), _sems(nf * m)] + _hbm_like(bufs),
        input_output_aliases={i: 2 + i for i in range(m)},
        compiler_params=pltpu.CompilerParams(has_side_effects=EFFECT),
    )(*bufs, recv, *after)
    return (res[0], res[1]), list(res[2:])


NEIGHBOURS = (0, 1)


def _relay_partner(x, y, c):
    return 2 * (x ^ (1 - c)) + (y ^ c), (x ^ c, y ^ (1 - c))


def _gather_forward_relay(bufs, recv, after, name):
    m = len(bufs)
    nf = len(NEIGHBOURS)

    def body(*refs):
        ins, recv_in = refs[:m], refs[m]
        fsend, frecv, rsend, rrecv = refs[m + 1 + len(after):m + 5 + len(after)]
        x, y, c = _place()
        chips = _other_chips(x, y)
        for jj, j in enumerate(NEIGHBOURS):
            cx, cy = chips[j]
            for i in range(m):
                landed = ins[i].at[2 * cx + cy, c]
                k = nf * i + jj
                _remote(landed, landed, fsend.at[k], recv_in.at[k], (cx, cy, c)).wait_recv()
        row, (px, py) = _relay_partner(x, y, c)
        for i in range(m):
            relayed = ins[i].at[row, c]
            _remote(relayed, relayed, rsend.at[i], rrecv.at[i], (px, py, c)).start()
        for jj, j in enumerate(NEIGHBOURS):
            cx, cy = chips[j]
            for i in range(m):
                landed = ins[i].at[2 * cx + cy, c]
                k = nf * i + jj
                _remote(landed, landed, fsend.at[k], frecv.at[k], (x, y, 1 - c)).start()

    res = pl.pallas_call(
        body, name=name, in_specs=[HBM_SPEC] * m + [SEM_SPEC] + [HBM] * len(after),
        out_specs=[SEM_SPEC] * 4 + [HBM_SPEC] * m,
        out_shape=[_sems(nf * m), _sems(nf * m), _sems(m), _sems(m)] + _hbm_like(bufs),
        input_output_aliases={i: 4 + i for i in range(m)},
        compiler_params=pltpu.CompilerParams(has_side_effects=EFFECT),
    )(*bufs, recv, *after)
    return tuple(res[:4]), list(res[4:])


def _gather_forward_diag(bufs, rrecv, after, name):
    m = len(bufs)

    def body(*refs):
        ins, rrecv_in = refs[:m], refs[m]
        dsend, drecv = refs[m + 1 + len(after)], refs[m + 2 + len(after)]
        x, y, c = _place()
        diag = (2 * x + y) ^ 3
        _, (px, py) = _relay_partner(x, y, c)
        for i in range(m):
            landed = ins[i].at[diag, c]
            _remote(landed, landed, dsend.at[i], rrecv_in.at[i], (px, py, c)).wait_recv()
            _remote(landed, landed, dsend.at[i], drecv.at[i], (x, y, 1 - c)).start()

    res = pl.pallas_call(
        body, name=name, in_specs=[HBM_SPEC] * m + [SEM_SPEC] + [HBM] * len(after),
        out_specs=[SEM_SPEC, SEM_SPEC] + [HBM_SPEC] * m, out_shape=[_sems(m), _sems(m)] + _hbm_like(bufs),
        input_output_aliases={i: 2 + i for i in range(m)},
        compiler_params=pltpu.CompilerParams(has_side_effects=EFFECT),
    )(*bufs, rrecv, *after)
    return (res[0], res[1]), list(res[2:])


def _gather_finish_diag(bufs, rsend, dsend, drecv, after, name):
    m = len(bufs)

    def body(*refs):
        ins = refs[:m]
        rsend_in, dsend_in, drecv_in = refs[m:m + 3]
        x, y, c = _place()
        diag = (2 * x + y) ^ 3
        row, (px, py) = _relay_partner(x, y, c)
        for i in range(m):
            relayed = ins[i].at[row, c]
            _remote(relayed, relayed, rsend_in.at[i], drecv_in.at[i], (px, py, c)).wait_send()
            landed = ins[i].at[diag, c]
            _remote(landed, landed, dsend_in.at[i], drecv_in.at[i], (x, y, 1 - c)).wait_send()
            theirs = ins[i].at[diag, 1 - c]
            _remote(theirs, theirs, dsend_in.at[i], drecv_in.at[i], (x, y, 1 - c)).wait_recv()

    return list(pl.pallas_call(
        body, name=name, in_specs=[HBM_SPEC] * m + [SEM_SPEC] * 3 + [HBM] * len(after),
        out_specs=[HBM_SPEC] * m, out_shape=_hbm_like(bufs),
        input_output_aliases={i: i for i in range(m)},
        compiler_params=pltpu.CompilerParams(has_side_effects=EFFECT),
    )(*bufs, rsend, dsend, drecv, *after))


def _gather_finish(bufs, send, fsend, frecv, after, name, flips=ALL_FLIPS):
    m = len(bufs)
    nf = len(flips)

    def body(*refs):
        ins = refs[:m]
        send_in, fsend_in, frecv_in = refs[m:m + 3]
        x, y, c = _place()
        me = 2 * x + y
        chips = _other_chips(x, y)
        for jj, j in enumerate(flips):
            cx, cy = chips[j]
            cj = 2 * cx + cy
            for i in range(m):
                k = nf * i + jj
                mine = ins[i].at[me, c]
                _remote(mine, mine, send_in.at[k], frecv_in.at[k], (cx, cy, c)).wait_send()
                landed = ins[i].at[cj, c]
                _remote(landed, landed, fsend_in.at[k], frecv_in.at[k], (x, y, 1 - c)).wait_send()
                theirs = ins[i].at[cj, 1 - c]
                _remote(theirs, theirs, fsend_in.at[k], frecv_in.at[k], (x, y, 1 - c)).wait_recv()

    return list(pl.pallas_call(
        body, name=name, in_specs=[HBM_SPEC] * m + [SEM_SPEC] * 3 + [HBM] * len(after),
        out_specs=[HBM_SPEC] * m, out_shape=_hbm_like(bufs),
        input_output_aliases={i: i for i in range(m)},
        compiler_params=pltpu.CompilerParams(has_side_effects=EFFECT),
    )(*bufs, send, fsend, frecv, *after))


def _pair_exchange(grads, name):
    n = len(grads)

    def body(*refs):
        ins, outs = refs[:n], refs[n:2 * n]
        send_sems, recv_sems = refs[2 * n:]
        x, y, c = _place()
        sibling = (x, y, 1 - c)
        cps = []
        for w in range(n):
            for j in range(N_CHIPS):
                cp = pltpu.make_async_remote_copy(
                    src_ref=ins[w].at[j, 1 - c], dst_ref=outs[w].at[j], send_sem=send_sems.at[N_CHIPS * w + j],
                    recv_sem=recv_sems.at[N_CHIPS * w + j], device_id=sibling, device_id_type=MESH)
                cp.start()
                cps.append(cp)
        for cp in cps:
            cp.wait()

    return pl.pallas_call(
        body, name=name, in_specs=[HBM] * n, out_specs=[HBM] * n,
        out_shape=[jax.ShapeDtypeStruct((N_CHIPS,) + a.shape[2:], a.dtype) for a in grads],
        scratch_shapes=[pltpu.SemaphoreType.DMA((N_CHIPS * n,)), pltpu.SemaphoreType.DMA((N_CHIPS * n,))],
    )(*grads)


def _pair_start(grads, name):
    n = len(grads)
    lands = [lax.empty((N_CHIPS,) + a.shape[2:], a.dtype) for a in grads]

    def body(*refs):
        ins, land_in = refs[:n], refs[n:2 * n]
        send, recv = refs[2 * n], refs[2 * n + 1]
        token = refs[4 * n + 2]
        x, y, c = _place()
        for w in range(n):
            for j in range(N_CHIPS):
                k = N_CHIPS * w + j
                _remote(ins[w].at[j, 1 - c], land_in[w].at[j], send.at[k], recv.at[k], (x, y, 1 - c)).start()
        token[...] = jnp.zeros_like(token)

    res = pl.pallas_call(
        body, name=name, in_specs=[HBM_SPEC] * (2 * n),
        out_specs=[SEM_SPEC, SEM_SPEC] + [HBM_SPEC] * (2 * n) + [TOKEN_SPEC],
        out_shape=[_sems(N_CHIPS * n), _sems(N_CHIPS * n)] + _hbm_like(grads) + _hbm_like(lands) + [TOKEN],
        input_output_aliases={i: 2 + i for i in range(2 * n)},
        compiler_params=pltpu.CompilerParams(has_side_effects=EFFECT),
    )(*_in_hbm(grads), *_in_hbm(lands))
    return (res[0], res[1]), list(res[2:2 + n]), list(res[2 + n:2 + 2 * n]), res[2 + 2 * n]


def _pair_wait(sems, grads, lands, after, name):
    n = len(grads)

    def body(*refs):
        ins, land_in = refs[:n], refs[n:2 * n]
        send_in, recv_in = refs[2 * n], refs[2 * n + 1]
        x, y, c = _place()
        for w in range(n):
            for j in range(N_CHIPS):
                k = N_CHIPS * w + j
                cp = _remote(ins[w].at[j, 1 - c], land_in[w].at[j], send_in.at[k], recv_in.at[k], (x, y, 1 - c))
                cp.wait_send()
                cp.wait_recv()

    res = pl.pallas_call(
        body, name=name, in_specs=[HBM_SPEC] * (2 * n) + [SEM_SPEC, SEM_SPEC] + [HBM] * len(after),
        out_specs=[HBM_SPEC] * (2 * n), out_shape=_hbm_like(grads) + _hbm_like(lands),
        input_output_aliases={i: i for i in range(2 * n)},
        compiler_params=pltpu.CompilerParams(has_side_effects=EFFECT),
    )(*grads, *lands, sems[0], sems[1], *after)
    return list(res[:n]), list(res[n:])


def _chip_start(sums, name, to_all=()):
    m = len(sums)
    lands = [lax.empty(((N_CHIPS,) if i in to_all else ()) + a.shape, a.dtype) for i, a in enumerate(sums)]

    def body(*refs):
        ins, land_in = refs[:m], refs[m:2 * m]
        send, recv = refs[2 * m], refs[2 * m + 1]
        token = refs[4 * m + 2]
        x, y, c = _place()
        me = 2 * x + y
        for i in sorted(range(m), key=lambda i: i not in to_all):
            for j, (cx, cy) in enumerate(_other_chips(x, y)):
                src = ins[i] if i in to_all else ins[i].at[2 * cx + cy]
                _remote(src, land_in[i].at[me], send.at[3 * i + j], recv.at[3 * i + j], (cx, cy, c)).start()
        token[...] = jnp.zeros_like(token)

    res = pl.pallas_call(
        body, name=name, in_specs=[HBM_SPEC] * (2 * m),
        out_specs=[SEM_SPEC, SEM_SPEC] + [HBM_SPEC] * (2 * m) + [TOKEN_SPEC],
        out_shape=[_sems(3 * m), _sems(3 * m)] + _hbm_like(sums) + _hbm_like(lands) + [TOKEN],
        input_output_aliases={i: 2 + i for i in range(2 * m)},
        compiler_params=pltpu.CompilerParams(has_side_effects=EFFECT),
    )(*_in_hbm(sums), *_in_hbm(lands))
    return (res[0], res[1]), list(res[2:2 + m]), list(res[2 + m:2 + 2 * m]), res[2 + 2 * m]


def _chip_wait(sems, sums, lands, after, name, to_all=()):
    m = len(sums)

    def body(*refs):
        ins, land_in = refs[:m], refs[m:2 * m]
        send_in, recv_in = refs[2 * m], refs[2 * m + 1]
        x, y, c = _place()
        for i in range(m):
            for j, (cx, cy) in enumerate(_other_chips(x, y)):
                cj = 2 * cx + cy
                src = ins[i] if i in to_all else ins[i].at[cj]
                cp = _remote(src, land_in[i].at[cj], send_in.at[3 * i + j], recv_in.at[3 * i + j], (cx, cy, c))
                cp.wait_send()
                cp.wait_recv()

    res = pl.pallas_call(
        body, name=name, in_specs=[HBM_SPEC] * (2 * m) + [SEM_SPEC, SEM_SPEC] + [HBM] * len(after),
        out_specs=[HBM_SPEC] * (2 * m), out_shape=_hbm_like(sums) + _hbm_like(lands),
        input_output_aliases={i: i for i in range(2 * m)},
        compiler_params=pltpu.CompilerParams(has_side_effects=EFFECT),
    )(*sums, *lands, sems[0], sems[1], *after)
    return list(res[:m]), list(res[m:])


def _half_parts(bufs):
    parts = []
    for w, a in enumerate(bufs):
        parts += [(w, None)] if a.ndim == 3 else [(w, j) for j in range(a.shape[0])]
    return parts


def _half_ref(refs, w, j, h):
    return refs[w].at[h] if j is None else refs[w].at[j, h]


def _half_start(bufs, name):
    n = len(bufs)
    parts = _half_parts(bufs)

    def body(*refs):
        ins = refs[:n]
        send, recv = refs[n], refs[n + 1]
        x, y, c = _place()
        for k, (w, j) in enumerate(parts):
            mine = _half_ref(ins, w, j, c)
            _remote(mine, mine, send.at[k], recv.at[k], (x, y, 1 - c)).start()

    res = pl.pallas_call(
        body, name=name, in_specs=[HBM_SPEC] * n, out_specs=[SEM_SPEC, SEM_SPEC] + [HBM_SPEC] * n,
        out_shape=[_sems(len(parts)), _sems(len(parts))] + _hbm_like(bufs),
        input_output_aliases={w: 2 + w for w in range(n)},
        compiler_params=pltpu.CompilerParams(has_side_effects=EFFECT),
    )(*_in_hbm(bufs))
    return (res[0], res[1]), list(res[2:])


def _half_wait(sems, bufs, after, name):
    n = len(bufs)
    parts = _half_parts(bufs)

    def body(*refs):
        ins = refs[:n]
        send_in, recv_in = refs[n], refs[n + 1]
        x, y, c = _place()
        for k, (w, j) in enumerate(parts):
            mine = _half_ref(ins, w, j, c)
            _remote(mine, mine, send_in.at[k], recv_in.at[k], (x, y, 1 - c)).wait_send()
            theirs = _half_ref(ins, w, j, 1 - c)
            _remote(theirs, theirs, send_in.at[k], recv_in.at[k], (x, y, 1 - c)).wait_recv()

    return list(pl.pallas_call(
        body, name=name, in_specs=[HBM_SPEC] * n + [SEM_SPEC, SEM_SPEC] + [HBM] * len(after),
        out_specs=[HBM_SPEC] * n, out_shape=_hbm_like(bufs),
        input_output_aliases={w: w for w in range(n)},
        compiler_params=pltpu.CompilerParams(has_side_effects=EFFECT),
    )(*bufs, sems[0], sems[1], *after))


def _all_gather8(block, name, after=()):
    def body(in_ref, *rest):
        out_ref, send_sems, recv_sems, local_sem = rest[len(after):]
        x, y, c = _place()
        me = 4 * x + 2 * y + c
        mine = pltpu.make_async_copy(in_ref, out_ref.at[me], local_sem)
        mine.start()
        flips = [(fx, fy, fc) for fx in (0, 1) for fy in (0, 1) for fc in (0, 1)][1:]
        cps = []
        for k, (fx, fy, fc) in enumerate(flips):
            cp = pltpu.make_async_remote_copy(
                src_ref=in_ref, dst_ref=out_ref.at[me], send_sem=send_sems.at[k], recv_sem=recv_sems.at[k],
                device_id=(x ^ fx, y ^ fy, c ^ fc), device_id_type=MESH)
            cp.start()
            cps.append(cp)
        for k, (fx, fy, fc) in enumerate(flips):
            px, py, pc = x ^ fx, y ^ fy, c ^ fc
            theirs = out_ref.at[4 * px + 2 * py + pc]
            pltpu.make_async_remote_copy(
                src_ref=theirs, dst_ref=theirs, send_sem=send_sems.at[k], recv_sem=recv_sems.at[k],
                device_id=(px, py, pc), device_id_type=MESH).wait_recv()
        for cp in cps:
            cp.wait_send()
        mine.wait()

    return pl.pallas_call(
        body, name=name, in_specs=[HBM] * (1 + len(after)), out_specs=HBM,
        out_shape=jax.ShapeDtypeStruct((N_DEV,) + block.shape, block.dtype),
        scratch_shapes=[pltpu.SemaphoreType.DMA((N_DEV - 1,)), pltpu.SemaphoreType.DMA((N_DEV - 1,)),
                        pltpu.SemaphoreType.DMA],
    )(block, *after)


def _pair_sum(grad, recv, c_arr, name):
    _, _, rh, cols = grad.shape
    tr = _row_tile(rh, cols * grad.dtype.itemsize, STREAM_TILE // 2)

    def body(c_ref, g_ref, r_ref, o_ref):
        del c_ref
        o_ref[...] = (g_ref[...].astype(F32) + r_ref[...].astype(F32)).astype(o_ref.dtype)

    spec = pl.BlockSpec((None, tr, cols), lambda j, i, c_ref: (j, i, 0))
    return pl.pallas_call(
        body, name=name,
        grid_spec=pltpu.PrefetchScalarGridSpec(
            num_scalar_prefetch=1, grid=(N_CHIPS, rh // tr),
            in_specs=[pl.BlockSpec((None, None, tr, cols), lambda j, i, c_ref: (j, c_ref[0], i, 0)), spec],
            out_specs=spec),
        out_shape=jax.ShapeDtypeStruct(recv.shape, grad.dtype), compiler_params=_params(32),
    )(c_arr, grad, recv)


def _sum_by_chip(chip, p_ref, own_ref, o_ref):
    o_ref[...] = jnp.zeros_like(o_ref)
    for k in range(N_CHIPS):
        @pl.when(chip == k)
        def _():
            o_ref[...] += own_ref[...].astype(F32)

        @pl.when(chip != k)
        def _(k=k):
            o_ref[...] += p_ref[k].astype(F32)


def _chip_sum_all(parts, own, place_arr, name):
    _, nj, rh, cols = parts.shape
    tr = _row_tile(rh, cols * 4, STREAM_TILE // 2)

    def body(place_ref, p_ref, own_ref, o_ref):
        _sum_by_chip(place_ref[0], p_ref, own_ref, o_ref)

    return pl.pallas_call(
        body, name=name,
        grid_spec=pltpu.PrefetchScalarGridSpec(
            num_scalar_prefetch=1, grid=(nj, rh // tr),
            in_specs=[pl.BlockSpec((N_CHIPS, None, tr, cols), lambda j, i, place_ref: (0, j, i, 0)),
                      pl.BlockSpec((None, tr, cols), lambda j, i, place_ref: (j, i, 0))],
            out_specs=pl.BlockSpec((None, None, tr, cols), lambda j, i, place_ref: (j, place_ref[1], i, 0))),
        out_shape=jax.ShapeDtypeStruct((nj, 2, rh, cols), F32), compiler_params=_params(32),
    )(place_arr, parts, own)


def _chip_sum(parts, own, place_arr, name):
    _, rh, cols = parts.shape
    tr = _row_tile(rh, cols * 4, STREAM_TILE // 2)

    def body(place_ref, p_ref, own_ref, o_ref):
        _sum_by_chip(place_ref[0], p_ref, own_ref, o_ref)

    return pl.pallas_call(
        body, name=name,
        grid_spec=pltpu.PrefetchScalarGridSpec(
            num_scalar_prefetch=1, grid=(rh // tr,),
            in_specs=[pl.BlockSpec((N_CHIPS, tr, cols), lambda i, place_ref: (0, i, 0)),
                      pl.BlockSpec((None, tr, cols), lambda i, place_ref: (place_ref[0], i, 0))],
            out_specs=pl.BlockSpec((None, tr, cols), lambda i, place_ref: (place_ref[1], i, 0))),
        out_shape=jax.ShapeDtypeStruct((2, rh, cols), F32), compiler_params=_params(32),
    )(place_arr, parts, own)


def _adamw_math(w, g, m, v):
    m = ADAM_B1 * m + (1.0 - ADAM_B1) * g
    v = ADAM_B2 * v + (1.0 - ADAM_B2) * (g * g)
    m_hat = m / (1.0 - ADAM_B1 ** ADAM_STEP)
    v_hat = v / (1.0 - ADAM_B2 ** ADAM_STEP)
    delta = -ADAM_LR * (m_hat / (jnp.sqrt(v_hat) + ADAM_EPS) + ADAM_WD * w)
    return delta, m, v


def _adamw(w, g, m, v, name):
    rows, cols = w.shape
    tr = _row_tile(rows, cols * 4)

    def body(w_ref, g_ref, m_ref, v_ref, go_ref, d_ref, nm_ref, nv_ref):
        gv = g_ref[...]
        go_ref[...] = gv
        d_ref[...], nm_ref[...], nv_ref[...] = _adamw_math(w_ref[...], gv, m_ref[...], v_ref[...])

    spec = pl.BlockSpec((tr, cols), lambda i: (i, 0))
    return pl.pallas_call(
        body, name=name, grid=(rows // tr,), in_specs=[spec] * 4, out_specs=[spec] * 4,
        out_shape=[jax.ShapeDtypeStruct(w.shape, F32)] * 4, compiler_params=_params(32),
    )(w, g, m, v)


def _sum8_adamw_row(parts, w, m, v, name):
    cols = parts.shape[2]

    def body(p_ref, w_ref, m_ref, v_ref, g_ref, d_ref, nm_ref, nv_ref):
        g = p_ref[0, 0:1, :]
        for k in range(1, N_DEV):
            g = g + p_ref[k, 0:1, :]
        g_ref[...] = g
        d_ref[...], nm_ref[...], nv_ref[...] = _adamw_math(w_ref[...], g, m_ref[...], v_ref[...])

    return pl.pallas_call(
        body, name=name, out_shape=[jax.ShapeDtypeStruct((1, cols), F32)] * 4, compiler_params=_params(32),
    )(parts, w, m, v)


def _pack_rows(pieces, rows, name):
    cols = pieces[0].shape[1]
    n = len(pieces)

    def body(*refs):
        o_ref = refs[n]
        o_ref[...] = jnp.zeros_like(o_ref)
        at = 0
        for p_ref in refs[:n]:
            r = p_ref.shape[0]
            o_ref[at:at + r, :] = p_ref[...]
            at += r

    return pl.pallas_call(
        body, name=name, out_shape=jax.ShapeDtypeStruct((rows, cols), F32), compiler_params=_params(32),
    )(*pieces)


def kernel(x, norm_mix_g, w_in, conv_w, conv_b, lru_wa, lru_ba, lru_wx, lru_bx, lru_lambda, w_proj_attn, w_proj_lru, w_out, norm_mlp_g, w_up, w_down, norm_final_g, loss_target, m_norm_mix_g, m_w_in, m_conv_w, m_conv_b, m_lru_wa, m_lru_ba, m_lru_wx, m_lru_bx, m_lru_lambda, m_w_proj_attn, m_w_proj_lru, m_w_out, m_norm_mlp_g, m_w_up, m_w_down, m_norm_final_g, v_norm_mix_g, v_w_in, v_conv_w, v_conv_b, v_lru_wa, v_lru_ba, v_lru_wx, v_lru_bx, v_lru_lambda, v_w_proj_attn, v_w_proj_lru, v_w_out, v_norm_mlp_g, v_w_up, v_w_down, v_norm_final_g):
    s, d = x.shape[1], x.shape[2]
    ff = w_up.shape[2] * N_CHIPS
    heads = d // HEAD_DIM
    u = d // 4
    assert s % (max(DILATIONS) * ATTN_BLK) == 0 and d % (4 * HEAD_DIM) == 0 and ff == 4 * d and DILATIONS[0] == 1
    xs, target = _in_hbm([x[0], loss_target[0]])
    gf = norm_final_g.reshape(1, d)
    wa, wx = lru_wa[0], lru_wx[0]
    core = lax.axis_index("c").astype(jnp.int32)
    chip = (2 * lax.axis_index("x") + lax.axis_index("y")).astype(jnp.int32)
    cidx = core.reshape(1)
    chip_arr = chip.reshape(1)
    place_arr = jnp.stack([chip, core])
    slopes = jnp.broadcast_to(
        (2.0 ** (-8.0 * jnp.arange(1, heads + 1, dtype=F32) / heads))[:, None, None], (heads, 1, HEAD_DIM))

    big = _in_hbm([w_in[0], w_proj_attn[0], w_proj_lru[0], w_out[0], w_up[0], w_down[0]])
    names = ["w_in", "w_proj_attn", "w_proj_lru", "w_out", "w_up", "w_down"]
    cw_pad = jnp.pad(conv_w[0], ((0, 8 - CONV_TAPS), (0, 0)))
    cw_all = _all_gather8(cw_pad, "gather_conv_w")
    conv_w_full = jnp.concatenate([cw_all[2 * j] for j in range(N_CHIPS)], axis=1)
    (sem_a,), buf_a = _gather_start([_cast_shard(big[0], chip_arr, "cast_w_in")], [([0], NEIGHBOURS)], [cw_all],
                                    "gather_start_w_in")

    xn = _rms_fwd(xs, norm_mix_g, "norm_mix")
    flip_bits = (2, 1, 3)

    def w_in_view():
        return buf_a[0].reshape(N_CHIPS, d, N_SLOTS * u)

    proj = _proj_in_shard(xn, w_in_view(), chip_arr, None, "proj_in_own")
    bufs = [_cast_shard(w, chip_arr, "cast_" + nm) for w, nm in zip(big[1:], names[1:])]
    (fs, fr, rs, rr), buf_a = _gather_forward_relay(buf_a, sem_a[1], [proj] + bufs, "gather_forward_w_in")
    buf_a = _gather_finish(buf_a, sem_a[0], fs, fr, [], "gather_finish_w_in", flips=NEIGHBOURS)
    for j in NEIGHBOURS:
        proj = _proj_in_shard(xn, w_in_view(), chip_arr ^ flip_bits[j], proj, "proj_in_from_%d" % j)
    (sem_b, sem_c, sem_d), bufs = _gather_start(
        bufs, [([0, 1, 2], ALL_FLIPS), ([3], ALL_FLIPS), ([4], ALL_FLIPS)], [proj], "gather_start_rest")
    (ds, dr), buf_a = _gather_forward_diag(buf_a, rr, [proj, bufs[0]], "gather_forward_w_in_diag")
    buf_a = _gather_finish_diag(buf_a, rs, ds, dr, [], "gather_finish_w_in_diag")
    proj = _proj_in_shard(xn, w_in_view(), chip_arr ^ flip_bits[2], proj, "proj_in_from_2")
    w_in_g = w_in_view()
    proj_a = proj_b = proj
    y_attn, lse = _attn_fwd(proj_a, slopes)
    y_lru, h_lru = _lru_fwd(proj_b, conv_w_full, conv_b, wa, lru_ba, wx, lru_bx, lru_lambda)
    fsem_b, buf_b = _gather_forward(bufs[:3], sem_b[1], [y_attn, y_lru], "gather_forward_proj")
    fsem_c, buf_c = _gather_forward(bufs[3:4], sem_c[1], [buf_b[0]], "gather_forward_w_up")
    buf_b = _gather_finish(buf_b, sem_b[0], fsem_b[0], fsem_b[1], [buf_c[0]], "gather_finish_proj")
    wpa_g = buf_b[0].reshape(d, d)
    wpl_g = buf_b[1].reshape(d, d)
    wout_g = buf_b[2].reshape(d, d)

    tn = u
    sd_f32 = jax.ShapeDtypeStruct((s, d), F32)
    sd_bf16 = jax.ShapeDtypeStruct((s, d), BF16)
    col = pl.BlockSpec((s, tn), lambda i, j, k: (0, j))

    def slot(n):
        return pl.BlockSpec((None, s, tn), lambda i, j, k: (n, 0, j))

    p_attn = _mm_nn("proj_attn", y_attn, wpa_g, [], [], [sd_f32], [col], _store, tn)[0]

    def merge(acc, extras, outs):
        pa_ref, ga_ref, gl_ref = extras
        merged = _sigmoid(ga_ref[...]) * pa_ref[...] + _sigmoid(gl_ref[...]) * acc
        outs[0][...] = merged.astype(BF16)
        outs[1][...] = acc

    tn2 = max(HEAD_DIM, u // 2)
    col2 = pl.BlockSpec((s, tn2), lambda i, j, k: (0, j))

    def slot2(n):
        return pl.BlockSpec((None, s, tn2), lambda i, j, k: (n, 0, j))

    merged, p_lru = _mm_nn("proj_lru_merge", y_lru, wpl_g, [p_attn, proj, proj], [col2, slot2(5), slot2(6)],
                           [sd_bf16, sd_f32], [col2, col2], merge, tn2)

    def add_resid(acc, extras, outs):
        outs[0][...] = extras[0][...] + acc

    h1 = _mm_nn("w_out_resid", merged, wout_g, [xs], [col], [sd_f32], [col], add_resid, tn)[0]
    hn = _rms_fwd(h1, norm_mlp_g, "norm_mlp")
    fsem_d, buf_d = _gather_forward(bufs[4:], sem_d[1], [hn], "gather_forward_w_down")
    buf_c = _gather_finish(buf_c, sem_c[0], fsem_c[0], fsem_c[1], [buf_d[0]], "gather_finish_w_up")
    wup_g = buf_c[0].reshape(N_CHIPS, d, d)

    def relu_sq(acc, extras, outs):
        r = jnp.maximum(acc, 0.0)
        outs[0][...] = (r * r).astype(BF16)
        outs[1][...] = r.astype(BF16)

    sf_bf16 = jax.ShapeDtypeStruct((s, ff), BF16)
    hid, relu_up = _mm(
        "w_up_relu2", [hn, wup_g],
        [pl.BlockSpec((s, d), lambda i, j, k: (0, 0)),
         pl.BlockSpec((None, d, tn), lambda i, j, k: (j // 4, 0, j % 4))],
        [sf_bf16, sf_bf16], [col, col], (1, ff // tn, 1), NN, relu_sq)
    wdown_g = _gather_finish(buf_d, sem_d[0], fsem_d[0], fsem_d[1], [hid], "gather_finish_w_down")[0].reshape(ff, d)
    h2 = _mm(
        "w_down_resid", [hid, wdown_g, h1],
        [pl.BlockSpec((s, d), lambda i, j, k: (0, k)), pl.BlockSpec((d, tn), lambda i, j, k: (k, j)), col],
        [sd_f32], [col], (1, d // tn, ff // d), NN, add_resid, nk=ff // d, acc_shape=(s, tn))[0]
    loss_part, dh2, dh2_b, d_gf = _loss_head(h2, gf, target)
    loss = lax.psum(loss_part[0, 0], AXES)

    def relu_sq_bwd(acc, extras, outs):
        outs[0][...] = (acc * (2.0 * extras[0][...].astype(F32))).astype(BF16)

    dup = _mm_nt("d_hid", dh2_b, wdown_g, [relu_up], [col], [sf_bf16], [col], relu_sq_bwd, tn)[0]
    tok_d = pl.BlockSpec((s, d), lambda i, j: (0, 0))
    g_wdown = _mm_tn(
        "g_w_down", hid, dh2_b, pl.BlockSpec((s, d), lambda i, j: (0, i)),
        pl.BlockSpec((s, tn), lambda i, j: (0, j)), jax.ShapeDtypeStruct((ff, d), BF16),
        pl.BlockSpec((d, tn), lambda i, j: (i, j)), (ff // d, d // tn), d, tn, s)
    dhn = _mm(
        "d_hn", [dup, wup_g],
        [pl.BlockSpec((s, d), lambda i, j, k: (0, k)), pl.BlockSpec((None, tn, d), lambda i, j, k: (k, j, 0))],
        [sd_f32], [col], (1, d // tn, ff // d), NT, _store, nk=ff // d, acc_shape=(s, tn))[0]
    g_wup = _mm_tn(
        "g_w_up", hn, dup, tok_d, pl.BlockSpec((s, tn), lambda i, j: (0, j)),
        jax.ShapeDtypeStruct((N_CHIPS, d, d), BF16), pl.BlockSpec((None, d, tn), lambda i, j: (j // 4, 0, j % 4)),
        (1, ff // tn), d, tn, s)
    big_m = _in_hbm([m_w_in[0], m_w_proj_attn[0], m_w_proj_lru[0], m_w_out[0], m_w_up[0], m_w_down[0]])
    big_v = _in_hbm([v_w_in[0], v_w_proj_attn[0], v_w_proj_lru[0], v_w_out[0], v_w_up[0], v_w_down[0]])
    big_out = {}

    def reduce_begin(ids, gs, tag, everywhere=None):
        g4 = [g.reshape(N_CHIPS, 2, big[i].shape[0] // 2, big[i].shape[1]) for i, g in zip(ids, gs)]
        tags = [names[i] for i in ids]
        if everywhere is not None:
            g4.append(everywhere.reshape(N_CHIPS, 2, everywhere.shape[0] // (2 * N_CHIPS), everywhere.shape[1]))
            tags.append("small_" + tag)
        from_sibling = _pair_exchange(g4, "pair_exchange_" + tag)
        sums = [_pair_sum(g, r, cidx, "pair_sum_" + t) for t, g, r in zip(tags, g4, from_sibling)]
        to_all = () if everywhere is None else (len(ids),)
        return _chip_start(sums, "chip_start_" + tag, to_all), to_all

    def pair_begin(ids, gs, tag):
        g4 = [g.reshape(N_CHIPS, 2, big[i].shape[0] // 2, big[i].shape[1]) for i, g in zip(ids, gs)]
        return _pair_start(g4, "pair_start_" + tag)

    def reduce_begin_paired(ids, paired, after, tag):
        sems, g4, lands, _ = paired
        g4, from_sibling = _pair_wait(sems, g4, lands, after, "pair_wait_" + tag)
        sums = [_pair_sum(g, r, cidx, "pair_sum_" + names[i]) for i, g, r in zip(ids, g4, from_sibling)]
        return _chip_start(sums, "chip_start_" + tag), ()

    def reduce_mid(ids, begun, after, tag):
        (sems, sums, lands, _), to_all = begun
        sums, lands = _chip_wait(sems, sums, lands, after, "chip_wait_" + tag, to_all)
        halves = [_chip_sum(p, own, place_arr, "chip_sum_" + names[i]) for i, p, own in zip(ids, lands, sums)]
        if to_all:
            halves.append(_chip_sum_all(lands[-1], sums[-1], place_arr, "chip_sum_small_" + tag))
        return _half_start(halves, "half_start_" + tag), to_all

    def reduce_end(ids, mid, after, tag):
        (hsems, halves), to_all = mid
        full = _half_wait(hsems, halves, after, "half_wait_" + tag)
        last = None
        for i, g in zip(ids, full):
            res = _adamw(big[i], g.reshape(big[i].shape), big_m[i], big_v[i], "adamw_" + names[i])
            big_out[names[i]] = tuple(a[None] for a in res)
            last = res[1]
        everywhere = full[-1].reshape(-1, full[-1].shape[-1]) if to_all else None
        return everywhere, last

    def after_token(a, begun):
        return a + begun[0][3][:1, :1]

    pair_mlp = pair_begin([4, 5], [g_wup, g_wdown], "mlp")
    dh1, dh1_b, d_gmlp = _rms_bwd(h1, norm_mlp_g + pair_mlp[3][:1, :1], dhn, dh2, "norm_mlp_bwd")

    g_wout = _mm_tn(
        "g_w_out", merged, dh1_b, tok_d, pl.BlockSpec((s, tn), lambda i, j: (0, j)),
        jax.ShapeDtypeStruct((d, d), BF16), pl.BlockSpec((d, tn), lambda i, j: (0, j)), (1, d // tn), d, tn, s)

    def merge_bwd(acc, extras, outs):
        pa_ref, pl_ref, ga_ref, gl_ref = extras
        sa, sl = _sigmoid(ga_ref[...]), _sigmoid(gl_ref[...])
        outs[0][...] = (acc * sa).astype(BF16)
        outs[1][...] = (acc * sl).astype(BF16)
        outs[2][0] = (acc * pa_ref[...] * (sa * (1.0 - sa))).astype(BF16)
        outs[2][1] = (acc * pl_ref[...] * (sl * (1.0 - sl))).astype(BF16)

    nb = N_SLOTS - N_QKV
    d_pa, d_pl, dproj_b = _mm_nt(
        "d_merged", dh1_b, wout_g, [p_attn, p_lru, proj, proj], [col2, col2, slot2(5), slot2(6)],
        [sd_bf16, sd_bf16, jax.ShapeDtypeStruct((nb, s, d), BF16)],
        [col2, col2, pl.BlockSpec((2, s, tn2), lambda i, j, k: (1, 0, j))], merge_bwd, tn2)
    red_mlp = reduce_begin_paired([4, 5], pair_mlp, [d_pa], "mlp")
    dy_attn = _mm_nt("d_y_attn", d_pa, wpa_g, [], [], [sd_f32], [col], _store, tn)[0]
    dy_lru = _mm_nt("d_y_lru", d_pl, wpl_g, [], [], [sd_f32], [col], _store, tn)[0]
    g_wpa = _mm_tn(
        "g_w_proj_attn", y_attn, d_pa, tok_d, pl.BlockSpec((s, tn), lambda i, j: (0, j)),
        jax.ShapeDtypeStruct((d, d), BF16), pl.BlockSpec((d, tn), lambda i, j: (0, j)), (1, d // tn), d, tn, s)
    g_wpl = _mm_tn(
        "g_w_proj_lru", y_lru, d_pl, tok_d, pl.BlockSpec((s, tn), lambda i, j: (0, j)),
        jax.ShapeDtypeStruct((d, d), BF16), pl.BlockSpec((d, tn), lambda i, j: (0, j)), (1, d // tn), d, tn, s)

    pair_proj = pair_begin([1, 2, 3], [g_wpa, g_wpl, g_wout], "proj")

    dproj_b, d_cw, d_cb, d_wa, d_ba, d_wx, d_bx, d_lam = _lru_bwd(
        proj_b, h_lru, dy_lru, conv_w_full, conv_b, wa, lru_ba, wx, lru_bx, lru_lambda + pair_proj[3][:1, :1],
        dproj_b)
    red_proj = reduce_begin_paired([1, 2, 3], pair_proj, [dproj_b], "proj")
    dproj_a = _attn_bwd(proj_a, after_token(slopes, red_proj), y_attn, lse, dy_attn)
    per = N_SLOTS
    g_win_shape = jax.ShapeDtypeStruct((N_CHIPS, d, N_SLOTS * u), BF16)

    def g_win_part(name, dproj, first, prev):
        n_units = 4 * dproj.shape[0]
        return _mm_tn(
            name, xn, dproj, tok_d, pl.BlockSpec((None, s, u), lambda i, j: (j // 4, 0, j % 4)),
            g_win_shape, pl.BlockSpec((None, d, u), lambda i, j: ((j + first) // per, 0, (j + first) % per)),
            (1, n_units), d, u, s, aliases=None if prev is None else {2: 0}, extra=prev)

    mat_rows = heads * HEAD_DIM * HEAD_DIM // d
    vec_names = ["norm_mix_g", "conv_b", "lru_ba", "lru_bx", "lru_lambda", "norm_mlp_g", "norm_final_g"]

    def pack(wa_, wx_, cw_, vecs, name):
        rows = [wa_.reshape(mat_rows, d), wx_.reshape(mat_rows, d), cw_] + [a.reshape(1, d) for a in vecs]
        n = sum(a.shape[0] for a in rows)
        return _pack_rows(rows, n + (-n % 64), name)

    zero_cw = jnp.zeros((CONV_TAPS, d), F32)
    small_g = pack(d_wa, d_wx, d_cw, [jnp.zeros((1, d), F32), d_cb, d_ba, d_bx, d_lam, d_gmlp, d_gf], "pack_small_g")
    g_win = g_win_part("g_w_in_qkv", dproj_a, 0, None)
    g_win = g_win_part("g_w_in_rest", dproj_b, 4 * N_QKV, g_win)
    red_in = reduce_begin([0], [g_win], "w_in", everywhere=small_g)
    dxn = _dxn(dproj_a, dproj_b, w_in_g, 2 * tn, [red_in[0][3]])
    grad_x, _, d_gmix = _rms_bwd(xs, norm_mix_g, dxn, dh1, "norm_mix_bwd")

    mid_mlp = reduce_mid([4, 5], red_mlp, [grad_x], "mlp")
    mid_proj = reduce_mid([1, 2, 3], red_proj, [mid_mlp[0][1][0]], "proj")
    _, done = reduce_end([4, 5], mid_mlp, [mid_proj[0][1][0]], "mlp")
    _, done = reduce_end([1, 2, 3], mid_proj, [done], "proj")
    small_w = pack(wa, wx, zero_cw, [norm_mix_g, conv_b, lru_ba, lru_bx, lru_lambda, norm_mlp_g, norm_final_g],
                   "pack_small_w")
    small_m = pack(m_lru_wa[0], m_lru_wx[0], zero_cw,
                   [m_norm_mix_g, m_conv_b, m_lru_ba, m_lru_bx, m_lru_lambda, m_norm_mlp_g, m_norm_final_g],
                   "pack_small_m")
    small_v = pack(v_lru_wa[0], v_lru_wx[0], zero_cw,
                   [v_norm_mix_g, v_conv_b, v_lru_ba, v_lru_bx, v_lru_lambda, v_norm_mlp_g, v_norm_final_g],
                   "pack_small_v")
    mid_in = reduce_mid([0], red_in, [done, small_w, small_m, small_v], "w_in")
    gmix_parts = _all_gather8(jnp.pad(d_gmix, ((0, 7), (0, 0))), "gather_gain_grad", [mid_in[0][1][0]])
    gmix_out = _sum8_adamw_row(gmix_parts, norm_mix_g, m_norm_mix_g, v_norm_mix_g, "sum_adamw_norm_mix_g")
    small_sum, _ = reduce_end([0], mid_in, [gmix_out[1]], "w_in")
    small = _adamw(small_w, small_sum, small_m, small_v, "adamw_small")
    g_cw = lax.dynamic_slice(small_sum[2 * mat_rows:2 * mat_rows + CONV_TAPS], (0, chip * u), (CONV_TAPS, u))
    cw_out = _adamw(conv_w[0], g_cw, m_conv_w[0], v_conv_w[0], "adamw_conv_w")

    def small_leaf(kind, name):
        a = small[kind]
        if name == "norm_mix_g":
            return gmix_out[kind]
        if name == "lru_wa":
            return a[0:mat_rows].reshape(lru_wa.shape)
        if name == "lru_wx":
            return a[mat_rows:2 * mat_rows].reshape(lru_wx.shape)
        if name == "conv_w":
            return cw_out[kind][None]
        row = a[2 * mat_rows + CONV_TAPS + vec_names.index(name)]
        return row if name == "norm_final_g" else row[None]

    order = ["norm_mix_g", "w_in", "conv_w", "conv_b", "lru_wa", "lru_ba", "lru_wx", "lru_bx", "lru_lambda",
             "w_proj_attn", "w_proj_lru", "w_out", "norm_mlp_g", "w_up", "w_down", "norm_final_g"]
    outs = [loss, grad_x[None]]
    for kind in range(4):
        for name in order:
            outs.append(big_out[name][kind] if name in big_out else small_leaf(kind, name))
    return tuple(outs)
```

```python
import functools

import jax
import jax.numpy as jnp
from jax import lax
from jax.experimental import pallas as pl
from jax.experimental.pallas import tpu as pltpu

F32 = jnp.float32
BF16 = jnp.bfloat16
MESH = pl.DeviceIdType.MESH
AXES = ("x", "y", "c")

N_CHIPS = 4
N_DEV = 8
HEAD_DIM = 128
ATTN_BLK = 128
DILATIONS = (1, 4, 16)
ATTN_UNROLL = 8
CONV_TAPS = 4
LRU_C = 8.0
EPS = 1e-6
N_SLOTS = 7
N_QKV = 3
VMEM_MIB = 2 ** 20
VMEM_V7X = 64 * VMEM_MIB
STREAM_TILE = 4 * VMEM_MIB

ADAM_LR = 0.001
ADAM_B1 = 0.9
ADAM_B2 = 0.999
ADAM_EPS = 1e-08
ADAM_WD = 0.01
ADAM_STEP = 10

NN = (((1,), (0,)), ((), ()))
NT = (((1,), (1,)), ((), ()))
TN = (((0,), (0,)), ((), ()))


def _params(vmem_mib=None, **kw):
    limit = None if vmem_mib is None else min(vmem_mib * VMEM_MIB, VMEM_V7X - 8 * VMEM_MIB)
    return pltpu.CompilerParams(vmem_limit_bytes=limit, **kw)


def _row_tile(rows, row_bytes, budget=VMEM_MIB):
    t = rows
    while t % 16 == 0 and t * row_bytes > budget:
        t //= 2
    return t


def _dot(a, b, dims):
    return lax.dot_general(a.astype(BF16), b.astype(BF16), dims, preferred_element_type=F32)


def _sigmoid(x):
    return jax.nn.sigmoid(x)


def _rms_fwd(x, g, name):
    s, d = x.shape
    tm = _row_tile(s, d * 4)

    def body(x_ref, g_ref, o_ref):
        xf = x_ref[...]
        r = lax.rsqrt(jnp.mean(xf * xf, axis=-1, keepdims=True) + EPS)
        o_ref[...] = (xf * r * g_ref[...]).astype(o_ref.dtype)

    return pl.pallas_call(
        body, name=name, grid=(s // tm,),
        in_specs=[pl.BlockSpec((tm, d), lambda i: (i, 0)), pl.BlockSpec((1, d), lambda i: (0, 0))],
        out_specs=pl.BlockSpec((tm, d), lambda i: (i, 0)),
        out_shape=jax.ShapeDtypeStruct((s, d), BF16), compiler_params=_params(32),
    )(x, g)


def _rms_bwd(x, g, dy, resid, name):
    s, d = x.shape
    tm = _row_tile(s, d * 4)

    def body(x_ref, g_ref, dy_ref, res_ref, dx_ref, dxb_ref, dg_ref):
        xf = x_ref[...]
        r = lax.rsqrt(jnp.mean(xf * xf, axis=-1, keepdims=True) + EPS)
        xh = xf * r
        dyv = dy_ref[...]
        dxh = dyv * g_ref[...]
        dx = r * (dxh - xh * jnp.mean(dxh * xh, axis=-1, keepdims=True)) + res_ref[...]
        dx_ref[...] = dx
        dxb_ref[...] = dx.astype(BF16)
        part = jnp.sum(dyv * xh, axis=0, keepdims=True)

        @pl.when(pl.program_id(0) == 0)
        def _():
            dg_ref[...] = part

        @pl.when(pl.program_id(0) > 0)
        def _():
            dg_ref[...] += part

    row = pl.BlockSpec((tm, d), lambda i: (i, 0))
    vec = pl.BlockSpec((1, d), lambda i: (0, 0))
    return pl.pallas_call(
        body, name=name, grid=(s // tm,),
        in_specs=[row, vec, row, row], out_specs=[row, row, vec],
        out_shape=[jax.ShapeDtypeStruct((s, d), F32), jax.ShapeDtypeStruct((s, d), BF16),
                   jax.ShapeDtypeStruct((1, d), F32)],
        compiler_params=_params(32),
    )(x, g, dy, resid)


def _loss_head(h2, g, target):
    s, d = h2.shape
    tm = _row_tile(s, d * 4)

    def body(x_ref, g_ref, t_ref, loss_ref, dx_ref, dxb_ref, dg_ref):
        xf = x_ref[...]
        gv = g_ref[...]
        r = lax.rsqrt(jnp.mean(xf * xf, axis=-1, keepdims=True) + EPS)
        xh = xf * r
        err = xh * gv - t_ref[...]
        part = jnp.sum(jnp.sum(err * err, axis=1, keepdims=True), axis=0, keepdims=True) * (0.5 / d)
        dyv = err * (1.0 / d)
        dxh = dyv * gv
        dx = r * (dxh - xh * jnp.mean(dxh * xh, axis=-1, keepdims=True))
        dx_ref[...] = dx
        dxb_ref[...] = dx.astype(BF16)
        dgp = jnp.sum(dyv * xh, axis=0, keepdims=True)

        @pl.when(pl.program_id(0) == 0)
        def _():
            dg_ref[...] = dgp
            loss_ref[...] = jnp.broadcast_to(part, loss_ref.shape)

        @pl.when(pl.program_id(0) > 0)
        def _():
            dg_ref[...] += dgp
            loss_ref[...] += jnp.broadcast_to(part, loss_ref.shape)

    row = pl.BlockSpec((tm, d), lambda i: (i, 0))
    vec = pl.BlockSpec((1, d), lambda i: (0, 0))
    return pl.pallas_call(
        body, name="loss_head", grid=(s // tm,),
        in_specs=[row, vec, row],
        out_specs=[pl.BlockSpec((8, 128), lambda i: (0, 0)), row, row, vec],
        out_shape=[jax.ShapeDtypeStruct((8, 128), F32), jax.ShapeDtypeStruct((s, d), F32),
                   jax.ShapeDtypeStruct((s, d), BF16), jax.ShapeDtypeStruct((1, d), F32)],
        compiler_params=_params(32),
    )(h2, g, target)


def _mm(name, operands, in_specs, out_shape, out_specs, grid, dims, epilogue, nk=1, acc_shape=None,
        vmem_mib=56, aliases=None):
    n_in = len(operands)
    n_out = len(out_shape)

    def body(*refs):
        a_ref, b_ref = refs[0], refs[1]
        extras = refs[2:n_in]
        outs = refs[n_in:n_in + n_out]

        def prod():
            return _dot(a_ref[...], b_ref[...], dims)

        if nk == 1:
            epilogue(prod(), extras, outs)
        else:
            acc = refs[n_in + n_out]
            k = pl.program_id(2)

            @pl.when(k == 0)
            def _():
                acc[...] = prod()

            @pl.when(k > 0)
            def _():
                acc[...] += prod()

            @pl.when(k == nk - 1)
            def _():
                epilogue(acc[...], extras, outs)

    scratch = [] if nk == 1 else [pltpu.VMEM(acc_shape, F32)]
    return pl.pallas_call(
        body, name=name, grid=grid, in_specs=in_specs, out_specs=out_specs, out_shape=out_shape,
        scratch_shapes=scratch, input_output_aliases=aliases or {},
        compiler_params=_params(vmem_mib),
    )(*operands)


def _store(acc, extras, outs):
    outs[0][...] = acc.astype(outs[0].dtype)


def _proj_in_shard(xn, w_in_g, shard_arr, prev, name):
    s, d = xn.shape
    u = d // 4
    per = N_SLOTS
    n_prev = 0 if prev is None else 1

    def body(sh_ref, x_ref, w_ref, *rest):
        del sh_ref
        rest[n_prev][...] = _dot(x_ref[...], w_ref[...], NN)

    def out_map(j, sh_ref):
        unit = per * sh_ref[0] + j
        return (unit // 4, 0, unit % 4)

    return pl.pallas_call(
        body, name=name,
        grid_spec=pltpu.PrefetchScalarGridSpec(
            num_scalar_prefetch=1, grid=(per,),
            in_specs=[pl.BlockSpec((s, d), lambda j, sh_ref: (0, 0)),
                      pl.BlockSpec((None, d, u), lambda j, sh_ref: (sh_ref[0], 0, j))] + [HBM] * n_prev,
            out_specs=pl.BlockSpec((None, s, u), out_map)),
        out_shape=jax.ShapeDtypeStruct((N_SLOTS, s, d), F32),
        input_output_aliases={3: 0} if n_prev else {},
        compiler_params=_params(56),
    )(shard_arr, xn, w_in_g, *([] if prev is None else [prev]))


def _mm_nn(name, a, b, extras, extra_specs, out_shape, out_specs, epilogue, tn, aliases=None):
    s, kdim = a.shape
    n = b.shape[1]
    return _mm(
        name, [a, b] + list(extras),
        [pl.BlockSpec((s, kdim), lambda i, j, k: (0, 0)), pl.BlockSpec((kdim, tn), lambda i, j, k: (0, j))]
        + list(extra_specs),
        out_shape, out_specs, (1, n // tn, 1), NN, epilogue, aliases=aliases)


def _mm_nt(name, a, b, extras, extra_specs, out_shape, out_specs, epilogue, tn, aliases=None):
    s, kdim = a.shape
    n = b.shape[0]
    return _mm(
        name, [a, b] + list(extras),
        [pl.BlockSpec((s, kdim), lambda i, j, k: (0, 0)), pl.BlockSpec((tn, kdim), lambda i, j, k: (j, 0))]
        + list(extra_specs),
        out_shape, out_specs, (1, n // tn, 1), NT, epilogue, aliases=aliases)


def _mm_tn(name, a, b, a_spec, b_spec, out_shape, out_spec, grid, m, tn, s, aliases=None, extra=None):
    ch = 256
    n_in = 2 if extra is None else 3

    def body(*refs):
        a_ref, b_ref = refs[0], refs[1]
        o_ref, at_ref = refs[n_in], refs[n_in + 1]

        @pl.when(pl.program_id(1) == 0)
        def _():
            for c0 in range(0, s, ch):
                at_ref[:, c0:c0 + ch] = a_ref[c0:c0 + ch, :].astype(F32).T.astype(BF16)

        o_ref[...] = _dot(at_ref[...], b_ref[...], NN).astype(o_ref.dtype)

    operands = [a, b] + ([] if extra is None else [extra])
    in_specs = [a_spec, b_spec] + ([] if extra is None else [pl.BlockSpec(memory_space=pl.ANY)])
    return pl.pallas_call(
        body, name=name, grid=grid, in_specs=in_specs, out_specs=out_spec, out_shape=out_shape,
        scratch_shapes=[pltpu.VMEM((m, s), BF16)], input_output_aliases=aliases or {},
        compiler_params=_params(56),
    )(*operands)


def _dxn(dproj_a, dproj_b, w_in_g, tn, after):
    n_a, s, d = dproj_a.shape
    u = d // 4
    ua = 4 * n_a
    nk = 4 * N_SLOTS
    per = N_SLOTS

    def body(a_ref, b_ref, w_ref, *rest):
        o_ref = rest[len(after)]
        k = pl.program_id(2)

        @pl.when(k == 0)
        def _():
            o_ref[...] = jnp.zeros_like(o_ref)

        @pl.when(k < ua)
        def _():
            o_ref[...] += _dot(a_ref[...], w_ref[...], NT)

        @pl.when(k >= ua)
        def _():
            o_ref[...] += _dot(b_ref[...], w_ref[...], NT)

    def a_map(i, j, k):
        kk = jnp.minimum(k, ua - 1)
        return (kk // 4, 0, kk % 4)

    def b_map(i, j, k):
        kk = jnp.maximum(k - ua, 0)
        return (kk // 4, 0, kk % 4)

    return pl.pallas_call(
        body, name="dxn", grid=(1, d // tn, nk),
        in_specs=[pl.BlockSpec((None, s, u), a_map), pl.BlockSpec((None, s, u), b_map),
                  pl.BlockSpec((None, tn, u), lambda i, j, k: (k // per, j, k % per))] + [HBM] * len(after),
        out_specs=pl.BlockSpec((s, tn), lambda i, j, k: (0, j)),
        out_shape=jax.ShapeDtypeStruct((s, d), F32),
        compiler_params=_params(48),
    )(dproj_a, dproj_b, w_in_g, *after)


def _attn_masks(slope, dil):
    ii = lax.broadcasted_iota(jnp.int32, (ATTN_BLK, 2 * ATTN_BLK), 0)
    jj = lax.broadcasted_iota(jnp.int32, (ATTN_BLK, 2 * ATTN_BLK), 1)
    diff = ATTN_BLK + ii - jj
    band = (diff >= 0) & (diff <= ATTN_BLK)
    bias = -(slope * float(dil)) * diff.astype(F32)
    return band, bias, jj


def _attn_window(t, nblk):
    cur = pl.ds(pl.multiple_of(t * ATTN_BLK, ATTN_BLK), ATTN_BLK)
    prev = pl.ds(pl.multiple_of(jnp.maximum(t - 1, 0) * ATTN_BLK, ATTN_BLK), ATTN_BLK)
    first = jnp.where(t % nblk == 0, ATTN_BLK, 0)
    return prev, cur, first


def _unrolled_loop(n, step, init, unroll=ATTN_UNROLL):
    def trip(i, carry):
        for k in range(unroll):
            carry = step(i * unroll + k, carry)
        return carry

    return lax.fori_loop(0, n // unroll, trip, init)


def _streams(pairs, dil, s):
    if dil == 1:
        return [src for _, src in pairs]
    seg = s // dil
    for dst, src in pairs:
        for r in range(dil):
            dst[r * seg:(r + 1) * seg, :] = src[pl.ds(r, seg, stride=dil), :].astype(dst.dtype)
    return [dst for dst, _ in pairs]


def _attn_fwd(proj_a, slopes):
    _, s, d = proj_a.shape
    heads = d // HEAD_DIM
    scale = HEAD_DIM ** -0.5
    n_t = s // ATTN_BLK
    ng = len(DILATIONS)

    def body(q_ref, k_ref, v_ref, sl_ref, o_ref, lse_ref, qd, kd, vd, od, ld, og, lg):
        slope = sl_ref[...][:, :1]
        for g, dil in enumerate(DILATIONS):
            nblk = s // dil // ATTN_BLK
            qs, ks, vs = _streams([(qd, q_ref), (kd, k_ref), (vd, v_ref)], dil, s)
            o_t, l_t = (og.at[g], lg.at[g]) if dil == 1 else (od, ld)
            band, bias, jj = _attn_masks(slope, dil)

            def blk(t, carry, nblk=nblk, band=band, bias=bias, jj=jj, qs=qs, ks=ks, vs=vs, o_t=o_t, l_t=l_t):
                prev, cur, first = _attn_window(t, nblk)
                kk = jnp.concatenate([ks[prev, :], ks[cur, :]], axis=0)
                vv = jnp.concatenate([vs[prev, :], vs[cur, :]], axis=0)
                sc = _dot(qs[cur, :], kk, NT) * scale + bias
                sc = jnp.where(band & (jj >= first), sc, -jnp.inf)
                m = jnp.max(sc, axis=1, keepdims=True)
                p = jnp.exp(sc - m)
                l = jnp.sum(p, axis=1, keepdims=True)
                o_t[cur, :] = _dot(p, vv, NN) / l
                l_t[cur, :] = jnp.broadcast_to(m + jnp.log(l), (ATTN_BLK, HEAD_DIM))
                return carry

            _unrolled_loop(n_t, blk, 0)
            seg = s // dil
            if dil > 1:
                for r in range(dil):
                    og[g, pl.ds(r, seg, stride=dil), :] = od[r * seg:(r + 1) * seg, :]
                    lg[g, pl.ds(r, seg, stride=dil), :] = ld[r * seg:(r + 1) * seg, :]

        ch = 256

        def combine(c, carry):
            rows = pl.ds(pl.multiple_of(c * ch, ch), ch)
            ls = [lg[g, rows, :] for g in range(ng)]
            mx = functools.reduce(jnp.maximum, ls)
            es = [jnp.exp(x - mx) for x in ls]
            den = functools.reduce(jnp.add, es)
            num = functools.reduce(jnp.add, [es[g] * og[g, rows, :] for g in range(ng)])
            o_ref[rows, :] = (num / den).astype(o_ref.dtype)
            lse_ref[rows, :] = mx + jnp.log(den)
            return carry

        lax.fori_loop(0, s // ch, combine, 0)

    def col(slot):
        return pl.BlockSpec((None, s, HEAD_DIM), lambda h: (slot, 0, h))

    head = pl.BlockSpec((s, HEAD_DIM), lambda h: (0, h))
    return pl.pallas_call(
        body, name="attn_fwd", grid=(heads,),
        in_specs=[col(0), col(1), col(2), pl.BlockSpec((None, 1, HEAD_DIM), lambda h: (h, 0, 0))],
        out_specs=[head, head],
        out_shape=[jax.ShapeDtypeStruct((s, d), BF16), jax.ShapeDtypeStruct((s, d), F32)],
        scratch_shapes=[pltpu.VMEM((s, HEAD_DIM), BF16)] * 3 + [pltpu.VMEM((s, HEAD_DIM), F32)] * 2
        + [pltpu.VMEM((ng, s, HEAD_DIM), F32)] * 2,
        compiler_params=_params(40),
    )(proj_a, proj_a, proj_a, slopes)


def _attn_bwd(proj_a, slopes, y_attn, lse, dy):
    _, s, d = proj_a.shape
    heads = d // HEAD_DIM
    scale = HEAD_DIM ** -0.5
    n_t = s // ATTN_BLK

    def body(q_ref, k_ref, v_ref, sl_ref, o_ref, lse_ref, dy_ref, out_ref,
             qd, kd, vd, dod, lsd, dld, delta, dqd, dkd, dvd, dqa, dka, dva):
        slope = sl_ref[...][:, :1]
        dyv = dy_ref[...]
        delta[...] = jnp.broadcast_to(
            jnp.sum(dyv * o_ref[...].astype(F32), axis=1, keepdims=True), (s, HEAD_DIM))
        for g, dil in enumerate(DILATIONS):
            nblk = s // dil // ATTN_BLK
            seg = s // dil
            qs, ks, vs, dos, lss, dls = _streams(
                [(qd, q_ref), (kd, k_ref), (vd, v_ref), (dod, dy_ref), (lsd, lse_ref), (dld, delta)], dil, s)
            dq_t, dk_t, dv_t = (dqa, dka, dva) if dil == 1 else (dqd, dkd, dvd)
            band, bias, jj = _attn_masks(slope, dil)

            def blk(t, carry, nblk=nblk, band=band, bias=bias, jj=jj, qs=qs, ks=ks, vs=vs, dos=dos, lss=lss,
                    dls=dls, dq_t=dq_t, dk_t=dk_t, dv_t=dv_t):
                ck, cv = carry
                prev, cur, first = _attn_window(t, nblk)
                q = qs[cur, :]
                do = dos[cur, :]
                lse_b = lss[cur, :]
                dl_b = dls[cur, :]
                kk = jnp.concatenate([ks[prev, :], ks[cur, :]], axis=0)
                vv = jnp.concatenate([vs[prev, :], vs[cur, :]], axis=0)
                sc = _dot(q, kk, NT) * scale + bias
                p = jnp.where(band & (jj >= first), jnp.exp(sc - jnp.concatenate([lse_b, lse_b], axis=1)), 0.0)
                dp = _dot(do, vv, NT)
                ds = p * (dp - jnp.concatenate([dl_b, dl_b], axis=1))
                dv_b = _dot(p, do, TN)
                dk_b = _dot(ds, q, TN) * scale
                dq_t[cur, :] = _dot(ds, kk, NN) * scale
                done = pl.ds(pl.multiple_of(jnp.where(t == 0, n_t, t - 1) * ATTN_BLK, ATTN_BLK), ATTN_BLK)
                dk_t[done, :] = ck + dk_b[:ATTN_BLK]
                dv_t[done, :] = cv + dv_b[:ATTN_BLK]
                return dk_b[ATTN_BLK:], dv_b[ATTN_BLK:]

            zero = jnp.zeros((ATTN_BLK, HEAD_DIM), F32)
            ck, cv = _unrolled_loop(n_t, blk, (zero, zero))
            dk_t[(n_t - 1) * ATTN_BLK:n_t * ATTN_BLK, :] = ck
            dv_t[(n_t - 1) * ATTN_BLK:n_t * ATTN_BLK, :] = cv
            if dil > 1:
                for acc, part in ((dqa, dqd), (dka, dkd), (dva, dvd)):
                    for r in range(dil):
                        acc[pl.ds(r, seg, stride=dil), :] += part[r * seg:(r + 1) * seg, :]
        out_ref[0] = dqa[...].astype(out_ref.dtype)
        out_ref[1] = dka[0:s, :].astype(out_ref.dtype)
        out_ref[2] = dva[0:s, :].astype(out_ref.dtype)

    def col(slot):
        return pl.BlockSpec((None, s, HEAD_DIM), lambda h: (slot, 0, h))

    head = pl.BlockSpec((s, HEAD_DIM), lambda h: (0, h))
    return pl.pallas_call(
        body, name="attn_bwd", grid=(heads,),
        in_specs=[col(0), col(1), col(2), pl.BlockSpec((None, 1, HEAD_DIM), lambda h: (h, 0, 0)),
                  head, head, head],
        out_specs=pl.BlockSpec((N_QKV, s, HEAD_DIM), lambda h: (0, 0, h)),
        out_shape=jax.ShapeDtypeStruct((N_QKV, s, d), BF16),
        scratch_shapes=[pltpu.VMEM((s, HEAD_DIM), BF16)] * 4 + [pltpu.VMEM((s, HEAD_DIM), F32)] * 4
        + [pltpu.VMEM((s + ATTN_BLK, HEAD_DIM), F32)] * 2 + [pltpu.VMEM((s, HEAD_DIM), F32)]
        + [pltpu.VMEM((s + ATTN_BLK, HEAD_DIM), F32)] * 2,
        compiler_params=_params(48),
    )(proj_a, proj_a, proj_a, slopes, y_attn, lse, dy)


def _expm1(x):
    small = x * (1.0 + x * (0.5 + x * (1.0 / 6.0 + x * (1.0 / 24.0 + x * (1.0 / 120.0)))))
    return jnp.where(jnp.abs(x) < 0.1, small, jnp.exp(x) - 1.0)


def _softplus(x):
    return jnp.maximum(x, 0.0) + jnp.log1p(jnp.exp(-jnp.abs(x)))


GELU_K = 0.7978845608028654
GELU_C = 0.044715


def _gelu(x):
    t = jnp.tanh(GELU_K * (x + GELU_C * x * x * x))
    return 0.5 * x * (1.0 + t), t


def _gelu_grad(x, t):
    return 0.5 * (1.0 + t) + 0.5 * x * (1.0 - t * t) * GELU_K * (1.0 + 3.0 * GELU_C * x * x)


def _lru_gates(xc, wa, ba, wx, bx, sp):
    r = _sigmoid(_dot(xc, wa, NN) + ba)
    ig = _sigmoid(_dot(xc, wx, NN) + bx)
    log_a = -LRU_C * r * sp
    a = jnp.exp(log_a)
    mult = jnp.sqrt(-_expm1(2.0 * log_a))
    return r, ig, a, mult


def _scan_fwd(a, u, tt):
    row = lax.broadcasted_iota(jnp.int32, a.shape, 0)
    sh = 1
    while sh < tt:
        keep = row >= sh
        a_s = jnp.where(keep, pltpu.roll(a, sh, 0), 1.0)
        u_s = jnp.where(keep, pltpu.roll(u, sh, 0), 0.0)
        u = a * u_s + u
        a = a * a_s
        sh *= 2
    return a, u


def _scan_bwd(b, g, tt):
    row = lax.broadcasted_iota(jnp.int32, b.shape, 0)
    sh = 1
    while sh < tt:
        keep = row < tt - sh
        b_s = jnp.where(keep, pltpu.roll(b, tt - sh, 0), 1.0)
        g_s = jnp.where(keep, pltpu.roll(g, tt - sh, 0), 0.0)
        g = g + b * g_s
        b = b * b_s
        sh *= 2
    return b, g


def _conv_rows(xpad_ref, cw, cb, s):
    acc = cb
    for j in range(CONV_TAPS):
        off = 8 - (CONV_TAPS - 1) + j
        acc = acc + cw[j:j + 1, :] * xpad_ref[off:off + s, :]
    return acc


LRU_TILE = 128
LRU_UNROLL = 4


def _lru_specs(s, d):
    heads = d // HEAD_DIM

    def col(slot):
        return pl.BlockSpec((None, s, HEAD_DIM), lambda h: (slot, 0, h))

    vec = pl.BlockSpec((1, HEAD_DIM), lambda h: (0, h))
    mat = pl.BlockSpec((None, HEAD_DIM, HEAD_DIM), lambda h: (h, 0, 0))
    cw = pl.BlockSpec((8, HEAD_DIM), lambda h: (0, h))
    head = pl.BlockSpec((s, HEAD_DIM), lambda h: (0, h))
    return heads, col, vec, mat, cw, head


def _lru_fwd(proj_b, conv_w, conv_b, wa, ba, wx, bx, lam):
    _, s, d = proj_b.shape
    heads, col, vec, mat, cws, head = _lru_specs(s, d)
    tt = LRU_TILE

    def body(xr_ref, xg_ref, cw_ref, cb_ref, wa_ref, ba_ref, wx_ref, bx_ref, lam_ref, y_ref, h_ref, xpad, xc_s):
        xpad[0:8, :] = jnp.zeros((8, HEAD_DIM), F32)
        xpad[8:8 + s, :] = xr_ref[...]
        xc_s[...] = _conv_rows(xpad, cw_ref[...], cb_ref[...], s)
        sp = _softplus(-lam_ref[...])
        wav, wxv, bav, bxv = wa_ref[...], wx_ref[...], ba_ref[...], bx_ref[...]

        def tile(i, hc):
            rows = pl.ds(pl.multiple_of(i * tt, tt), tt)
            xc = xc_s[rows, :]
            _, ig, a, mult = _lru_gates(xc, wav, bav, wxv, bxv, sp)
            pa, hl = _scan_fwd(a, mult * (ig * xc), tt)
            h = hl + pa * hc
            h_ref[rows, :] = h
            gel, _ = _gelu(xg_ref[rows, :])
            y_ref[rows, :] = (h * gel).astype(y_ref.dtype)
            return h[tt - 1:tt, :]

        _unrolled_loop(s // tt, tile, jnp.zeros((1, HEAD_DIM), F32), LRU_UNROLL)

    return pl.pallas_call(
        body, name="lru_fwd", grid=(heads,),
        in_specs=[col(N_QKV), col(N_QKV + 1), cws, vec, mat, vec, mat, vec, vec],
        out_specs=[head, head],
        out_shape=[jax.ShapeDtypeStruct((s, d), BF16), jax.ShapeDtypeStruct((s, d), F32)],
        scratch_shapes=[pltpu.VMEM((s + 8, HEAD_DIM), F32), pltpu.VMEM((s, HEAD_DIM), F32)],
        compiler_params=_params(32),
    )(proj_b, proj_b, conv_w, conv_b, wa, ba, wx, bx, lam)


def _lru_bwd(proj_b, h_lru, dy, conv_w, conv_b, wa, ba, wx, bx, lam, dproj_b):
    _, s, d = proj_b.shape
    heads, col, vec, mat, cws, head = _lru_specs(s, d)
    tt = LRU_TILE
    n_t = s // tt

    def body(xr_ref, xg_ref, h_ref, dy_ref, cw_ref, cb_ref, wa_ref, ba_ref, wx_ref, bx_ref, lam_ref, alias_ref,
             out_ref, dcw_ref, dcb_ref, dwa_ref, dba_ref, dwx_ref, dbx_ref, dlam_ref, xpad, xc_s, dxc_s):
        del alias_ref
        xpad[0:8, :] = jnp.zeros((8, HEAD_DIM), F32)
        xpad[8:8 + s, :] = xr_ref[...]
        cwv = cw_ref[...]
        xc_s[...] = _conv_rows(xpad, cwv, cb_ref[...], s)
        dxc_s[s:s + 8, :] = jnp.zeros((8, HEAD_DIM), F32)
        lamv = lam_ref[...]
        sp = _softplus(-lamv)
        wav, wxv, bav, bxv = wa_ref[...], wx_ref[...], ba_ref[...], bx_ref[...]
        dwa_ref[...] = jnp.zeros_like(dwa_ref)
        dwx_ref[...] = jnp.zeros_like(dwx_ref)
        zero = jnp.zeros((1, HEAD_DIM), F32)
        row = lax.broadcasted_iota(jnp.int32, (tt, HEAD_DIM), 0)

        def tile(it, carry):
            dh_next, a_next, dba, dbx, dsp, dcb = carry
            i = n_t - 1 - it
            t0 = pl.multiple_of(i * tt, tt)
            rows = pl.ds(t0, tt)
            xc = xc_s[rows, :]
            r, ig, a, mult = _lru_gates(xc, wav, bav, wxv, bxv, sp)
            h = h_ref[rows, :]
            before = h_ref[pl.ds(pl.multiple_of(jnp.maximum(t0 - 8, 0), 8), 8), :][7:8, :]
            before = before * (i > 0).astype(F32)
            h_prev = jnp.where(row == 0, before, pltpu.roll(h, 1, 0))
            xg = xg_ref[rows, :]
            dyv = dy_ref[rows, :]
            gel, th = _gelu(xg)
            out_ref[1, rows, :] = (dyv * h * _gelu_grad(xg, th)).astype(out_ref.dtype)
            b = jnp.where(row == tt - 1, a_next, pltpu.roll(a, tt - 1, 0))
            pb, z = _scan_bwd(b, dyv * gel, tt)
            dh = z + pb * dh_next
            da = dh * h_prev
            dmult = dh * (ig * xc)
            dig = dh * (mult * xc)
            dla = da * a - dmult * (a * a / mult)
            dzr = dla * (-LRU_C * sp) * (r * (1.0 - r))
            dzx = dig * (ig * (1.0 - ig))
            dxc = dh * (mult * ig) + _dot(dzr, wav, NT) + _dot(dzx, wxv, NT)
            dxc_s[rows, :] = dxc
            dwa_ref[...] += _dot(xc, dzr, TN)
            dwx_ref[...] += _dot(xc, dzx, TN)
            return (dh[0:1, :], a[0:1, :],
                    dba + jnp.sum(dzr, axis=0, keepdims=True),
                    dbx + jnp.sum(dzx, axis=0, keepdims=True),
                    dsp + jnp.sum(dla * (-LRU_C * r), axis=0, keepdims=True),
                    dcb + jnp.sum(dxc, axis=0, keepdims=True))

        _, _, dba, dbx, dsp, dcb = _unrolled_loop(n_t, tile, (zero, zero, zero, zero, zero, zero), LRU_UNROLL)
        dba_ref[...] = dba
        dbx_ref[...] = dbx
        dcb_ref[...] = dcb
        dlam_ref[...] = -dsp * _sigmoid(-lamv)
        dxr = jnp.zeros((s, HEAD_DIM), F32)
        for j in range(CONV_TAPS):
            back = CONV_TAPS - 1 - j
            off = 8 - back
            dcw_ref[j:j + 1, :] = jnp.sum(dxc_s[0:s, :] * xpad[off:off + s, :], axis=0, keepdims=True)
            dxr = dxr + cwv[j:j + 1, :] * dxc_s[back:back + s, :]
        out_ref[0] = dxr.astype(out_ref.dtype)

    return pl.pallas_call(
        body, name="lru_bwd", grid=(heads,),
        in_specs=[col(N_QKV), col(N_QKV + 1), head, head, cws, vec, mat, vec, mat, vec, vec,
                  pl.BlockSpec(memory_space=pl.ANY)],
        out_specs=[pl.BlockSpec((2, s, HEAD_DIM), lambda h: (0, 0, h)),
                   pl.BlockSpec((CONV_TAPS, HEAD_DIM), lambda h: (0, h)), vec, mat, vec, mat, vec, vec],
        out_shape=[jax.ShapeDtypeStruct(dproj_b.shape, dproj_b.dtype),
                   jax.ShapeDtypeStruct((CONV_TAPS, d), F32), jax.ShapeDtypeStruct((1, d), F32),
                   jax.ShapeDtypeStruct(wa.shape, F32), jax.ShapeDtypeStruct((1, d), F32),
                   jax.ShapeDtypeStruct(wx.shape, F32), jax.ShapeDtypeStruct((1, d), F32),
                   jax.ShapeDtypeStruct((1, d), F32)],
        scratch_shapes=[pltpu.VMEM((s + 8, HEAD_DIM), F32), pltpu.VMEM((s, HEAD_DIM), F32),
                        pltpu.VMEM((s + 8, HEAD_DIM), F32)],
        input_output_aliases={11: 0},
        compiler_params=_params(32),
    )(proj_b, proj_b, h_lru, dy, conv_w, conv_b, wa, ba, wx, bx, lam, dproj_b)


def _place():
    x, y, c = (lax.axis_index(n) for n in AXES)
    return x, y, c


def _other_chips(x, y):
    return [(1 - x, y), (x, 1 - y), (1 - x, 1 - y)]


HBM = pl.BlockSpec(memory_space=pl.ANY)


def _cast_shard(w, chip_arr, name):
    r, cols = w.shape
    rh = r // 2
    tr = _row_tile(rh, cols * 4, STREAM_TILE)
    nt = rh // tr

    def body(chip_ref, w_ref, o_ref):
        del chip_ref
        o_ref[...] = w_ref[...].astype(BF16)

    return pl.pallas_call(
        body, name=name,
        grid_spec=pltpu.PrefetchScalarGridSpec(
            num_scalar_prefetch=1, grid=(2, nt),
            in_specs=[pl.BlockSpec((tr, cols), lambda h, i, chip_ref: (h * nt + i, 0))],
            out_specs=pl.BlockSpec((None, None, tr, cols), lambda h, i, chip_ref: (chip_ref[0], h, i, 0))),
        out_shape=jax.ShapeDtypeStruct((N_CHIPS, 2, rh, cols), BF16), compiler_params=_params(32),
    )(chip_arr, w)


HBM_SPEC = pl.BlockSpec(memory_space=pltpu.HBM)
SEM_SPEC = pl.BlockSpec(memory_space=pltpu.SEMAPHORE)
EFFECT = pltpu.SideEffectType.DATAFLOW_SIDE_EFFECTING
TOKEN = jax.ShapeDtypeStruct((8, 128), F32)
TOKEN_SPEC = pl.BlockSpec(memory_space=pltpu.VMEM)


def _in_hbm(arrays):
    return [pltpu.with_memory_space_constraint(a, pltpu.HBM) for a in arrays]


def _hbm_like(arrays):
    return [pltpu.HBM(a.shape, a.dtype) for a in arrays]


def _sems(n):
    return pltpu.SemaphoreType.DMA((n,))


def _remote(src, dst, send_sem, recv_sem, to):
    return pltpu.make_async_remote_copy(src_ref=src, dst_ref=dst, send_sem=send_sem, recv_sem=recv_sem,
                                        device_id=to, device_id_type=MESH)


ALL_FLIPS = (0, 1, 2)


def _gather_start(bufs, groups, after, name):
    n = len(bufs)
    ng = len(groups)

    def body(*refs):
        ins = refs[:n]
        sems = refs[n + len(after):n + len(after) + 2 * ng]
        x, y, c = _place()
        me = 2 * x + y
        chips = _other_chips(x, y)
        for g, (ws, flips) in enumerate(groups):
            for i, w in enumerate(ws):
                for jj, j in enumerate(flips):
                    k = len(flips) * i + jj
                    mine = ins[w].at[me, c]
                    _remote(mine, mine, sems[2 * g].at[k], sems[2 * g + 1].at[k], (*chips[j], c)).start()

    sem_shapes = []
    for ws, flips in groups:
        sem_shapes += [_sems(len(flips) * len(ws))] * 2
    res = pl.pallas_call(
        body, name=name, in_specs=[HBM_SPEC] * n + [HBM] * len(after),
        out_specs=[SEM_SPEC] * (2 * ng) + [HBM_SPEC] * n, out_shape=sem_shapes + _hbm_like(bufs),
        input_output_aliases={w: 2 * ng + w for w in range(n)},
        compiler_params=pltpu.CompilerParams(has_side_effects=EFFECT),
    )(*_in_hbm(bufs), *after)
    return [(res[2 * g], res[2 * g + 1]) for g in range(ng)], list(res[2 * ng:])


def _gather_forward(bufs, recv, after, name, flips=ALL_FLIPS):
    m = len(bufs)
    nf = len(flips)

    def body(*refs):
        ins, recv_in = refs[:m], refs[m]
        fsend, frecv = refs[m + 1 + len(after)], refs[m + 2 + len(after)]
        x, y, c = _place()
        chips = _other_chips(x, y)
        for jj, j in enumerate(flips):
            cx, cy = chips[j]
            for i in range(m):
                landed = ins[i].at[2 * cx + cy, c]
                k = nf * i + jj
                _remote(landed, landed, fsend.at[k], recv_in.at[k], (cx, cy, c)).wait_recv()
                _remote(landed, landed, fsend.at[k], frecv.at[k], (x, y, 1 - c)).start()

    res = pl.pallas_call(
        body, name=name, in_specs=[HBM_SPEC] * m + [SEM_SPEC] + [HBM] * len(after),
        out_specs=[SEM_SPEC, SEM_SPEC] + [HBM_SPEC] * m, out_shape=[_sems(nf * m), _sems(nf * m)] + _hbm_like(bufs),
        input_output_aliases={i: 2 + i for i in range(m)},
        compiler_params=pltpu.CompilerParams(has_side_effects=EFFECT),
    )(*bufs, recv, *after)
    return (res[0], res[1]), list(res[2:])


NEIGHBOURS = (0, 1)


def _relay_partner(x, y, c):
    return 2 * (x ^ (1 - c)) + (y ^ c), (x ^ c, y ^ (1 - c))


def _gather_forward_relay(bufs, recv, after, name):
    m = len(bufs)
    nf = len(NEIGHBOURS)

    def body(*refs):
        ins, recv_in = refs[:m], refs[m]
        fsend, frecv, rsend, rrecv = refs[m + 1 + len(after):m + 5 + len(after)]
        x, y, c = _place()
        chips = _other_chips(x, y)
        for jj, j in enumerate(NEIGHBOURS):
            cx, cy = chips[j]
            for i in range(m):
                landed = ins[i].at[2 * cx + cy, c]
                k = nf * i + jj
                _remote(landed, landed, fsend.at[k], recv_in.at[k], (cx, cy, c)).wait_recv()
        row, (px, py) = _relay_partner(x, y, c)
        for i in range(m):
            relayed = ins[i].at[row, c]
            _remote(relayed, relayed, rsend.at[i], rrecv.at[i], (px, py, c)).start()
        for jj, j in enumerate(NEIGHBOURS):
            cx, cy = chips[j]
            for i in range(m):
                landed = ins[i].at[2 * cx + cy, c]
                k = nf * i + jj
                _remote(landed, landed, fsend.at[k], frecv.at[k], (x, y, 1 - c)).start()

    res = pl.pallas_call(
        body, name=name, in_specs=[HBM_SPEC] * m + [SEM_SPEC] + [HBM] * len(after),
        out_specs=[SEM_SPEC] * 4 + [HBM_SPEC] * m,
        out_shape=[_sems(nf * m), _sems(nf * m), _sems(m), _sems(m)] + _hbm_like(bufs),
        input_output_aliases={i: 4 + i for i in range(m)},
        compiler_params=pltpu.CompilerParams(has_side_effects=EFFECT),
    )(*bufs, recv, *after)
    return tuple(res[:4]), list(res[4:])


def _gather_forward_diag(bufs, rrecv, after, name):
    m = len(bufs)

    def body(*refs):
        ins, rrecv_in = refs[:m], refs[m]
        dsend, drecv = refs[m + 1 + len(after)], refs[m + 2 + len(after)]
        x, y, c = _place()
        diag = (2 * x + y) ^ 3
        _, (px, py) = _relay_partner(x, y, c)
        for i in range(m):
            landed = ins[i].at[diag, c]
            _remote(landed, landed, dsend.at[i], rrecv_in.at[i], (px, py, c)).wait_recv()
            _remote(landed, landed, dsend.at[i], drecv.at[i], (x, y, 1 - c)).start()

    res = pl.pallas_call(
        body, name=name, in_specs=[HBM_SPEC] * m + [SEM_SPEC] + [HBM] * len(after),
        out_specs=[SEM_SPEC, SEM_SPEC] + [HBM_SPEC] * m, out_shape=[_sems(m), _sems(m)] + _hbm_like(bufs),
        input_output_aliases={i: 2 + i for i in range(m)},
        compiler_params=pltpu.CompilerParams(has_side_effects=EFFECT),
    )(*bufs, rrecv, *after)
    return (res[0], res[1]), list(res[2:])


def _gather_finish_diag(bufs, rsend, dsend, drecv, after, name):
    m = len(bufs)

    def body(*refs):
        ins = refs[:m]
        rsend_in, dsend_in, drecv_in = refs[m:m + 3]
        x, y, c = _place()
        diag = (2 * x + y) ^ 3
        row, (px, py) = _relay_partner(x, y, c)
        for i in range(m):
            relayed = ins[i].at[row, c]
            _remote(relayed, relayed, rsend_in.at[i], drecv_in.at[i], (px, py, c)).wait_send()
            landed = ins[i].at[diag, c]
            _remote(landed, landed, dsend_in.at[i], drecv_in.at[i], (x, y, 1 - c)).wait_send()
            theirs = ins[i].at[diag, 1 - c]
            _remote(theirs, theirs, dsend_in.at[i], drecv_in.at[i], (x, y, 1 - c)).wait_recv()

    return list(pl.pallas_call(
        body, name=name, in_specs=[HBM_SPEC] * m + [SEM_SPEC] * 3 + [HBM] * len(after),
        out_specs=[HBM_SPEC] * m, out_shape=_hbm_like(bufs),
        input_output_aliases={i: i for i in range(m)},
        compiler_params=pltpu.CompilerParams(has_side_effects=EFFECT),
    )(*bufs, rsend, dsend, drecv, *after))


def _gather_finish(bufs, send, fsend, frecv, after, name, flips=ALL_FLIPS):
    m = len(bufs)
    nf = len(flips)

    def body(*refs):
        ins = refs[:m]
        send_in, fsend_in, frecv_in = refs[m:m + 3]
        x, y, c = _place()
        me = 2 * x + y
        chips = _other_chips(x, y)
        for jj, j in enumerate(flips):
            cx, cy = chips[j]
            cj = 2 * cx + cy
            for i in range(m):
                k = nf * i + jj
                mine = ins[i].at[me, c]
                _remote(mine, mine, send_in.at[k], frecv_in.at[k], (cx, cy, c)).wait_send()
                landed = ins[i].at[cj, c]
                _remote(landed, landed, fsend_in.at[k], frecv_in.at[k], (x, y, 1 - c)).wait_send()
                theirs = ins[i].at[cj, 1 - c]
                _remote(theirs, theirs, fsend_in.at[k], frecv_in.at[k], (x, y, 1 - c)).wait_recv()

    return list(pl.pallas_call(
        body, name=name, in_specs=[HBM_SPEC] * m + [SEM_SPEC] * 3 + [HBM] * len(after),
        out_specs=[HBM_SPEC] * m, out_shape=_hbm_like(bufs),
        input_output_aliases={i: i for i in range(m)},
        compiler_params=pltpu.CompilerParams(has_side_effects=EFFECT),
    )(*bufs, send, fsend, frecv, *after))


def _pair_exchange(grads, name):
    n = len(grads)

    def body(*refs):
        ins, outs = refs[:n], refs[n:2 * n]
        send_sems, recv_sems = refs[2 * n:]
        x, y, c = _place()
        sibling = (x, y, 1 - c)
        cps = []
        for w in range(n):
            for j in range(N_CHIPS):
                cp = pltpu.make_async_remote_copy(
                    src_ref=ins[w].at[j, 1 - c], dst_ref=outs[w].at[j], send_sem=send_sems.at[N_CHIPS * w + j],
                    recv_sem=recv_sems.at[N_CHIPS * w + j], device_id=sibling, device_id_type=MESH)
                cp.start()
                cps.append(cp)
        for cp in cps:
            cp.wait()

    return pl.pallas_call(
        body, name=name, in_specs=[HBM] * n, out_specs=[HBM] * n,
        out_shape=[jax.ShapeDtypeStruct((N_CHIPS,) + a.shape[2:], a.dtype) for a in grads],
        scratch_shapes=[pltpu.SemaphoreType.DMA((N_CHIPS * n,)), pltpu.SemaphoreType.DMA((N_CHIPS * n,))],
    )(*grads)


def _pair_start(grads, name):
    n = len(grads)
    lands = [lax.empty((N_CHIPS,) + a.shape[2:], a.dtype) for a in grads]

    def body(*refs):
        ins, land_in = refs[:n], refs[n:2 * n]
        send, recv = refs[2 * n], refs[2 * n + 1]
        token = refs[4 * n + 2]
        x, y, c = _place()
        for w in range(n):
            for j in range(N_CHIPS):
                k = N_CHIPS * w + j
                _remote(ins[w].at[j, 1 - c], land_in[w].at[j], send.at[k], recv.at[k], (x, y, 1 - c)).start()
        token[...] = jnp.zeros_like(token)

    res = pl.pallas_call(
        body, name=name, in_specs=[HBM_SPEC] * (2 * n),
        out_specs=[SEM_SPEC, SEM_SPEC] + [HBM_SPEC] * (2 * n) + [TOKEN_SPEC],
        out_shape=[_sems(N_CHIPS * n), _sems(N_CHIPS * n)] + _hbm_like(grads) + _hbm_like(lands) + [TOKEN],
        input_output_aliases={i: 2 + i for i in range(2 * n)},
        compiler_params=pltpu.CompilerParams(has_side_effects=EFFECT),
    )(*_in_hbm(grads), *_in_hbm(lands))
    return (res[0], res[1]), list(res[2:2 + n]), list(res[2 + n:2 + 2 * n]), res[2 + 2 * n]


def _pair_wait(sems, grads, lands, after, name):
    n = len(grads)

    def body(*refs):
        ins, land_in = refs[:n], refs[n:2 * n]
        send_in, recv_in = refs[2 * n], refs[2 * n + 1]
        x, y, c = _place()
        for w in range(n):
            for j in range(N_CHIPS):
                k = N_CHIPS * w + j
                cp = _remote(ins[w].at[j, 1 - c], land_in[w].at[j], send_in.at[k], recv_in.at[k], (x, y, 1 - c))
                cp.wait_send()
                cp.wait_recv()

    res = pl.pallas_call(
        body, name=name, in_specs=[HBM_SPEC] * (2 * n) + [SEM_SPEC, SEM_SPEC] + [HBM] * len(after),
        out_specs=[HBM_SPEC] * (2 * n), out_shape=_hbm_like(grads) + _hbm_like(lands),
        input_output_aliases={i: i for i in range(2 * n)},
        compiler_params=pltpu.CompilerParams(has_side_effects=EFFECT),
    )(*grads, *lands, sems[0], sems[1], *after)
    return list(res[:n]), list(res[n:])


def _chip_start(sums, name, to_all=()):
    m = len(sums)
    lands = [lax.empty(((N_CHIPS,) if i in to_all else ()) + a.shape, a.dtype) for i, a in enumerate(sums)]

    def body(*refs):
        ins, land_in = refs[:m], refs[m:2 * m]
        send, recv = refs[2 * m], refs[2 * m + 1]
        token = refs[4 * m + 2]
        x, y, c = _place()
        me = 2 * x + y
        for i in sorted(range(m), key=lambda i: i not in to_all):
            for j, (cx, cy) in enumerate(_other_chips(x, y)):
                src = ins[i] if i in to_all else ins[i].at[2 * cx + cy]
                _remote(src, land_in[i].at[me], send.at[3 * i + j], recv.at[3 * i + j], (cx, cy, c)).start()
        token[...] = jnp.zeros_like(token)

    res = pl.pallas_call(
        body, name=name, in_specs=[HBM_SPEC] * (2 * m),
        out_specs=[SEM_SPEC, SEM_SPEC] + [HBM_SPEC] * (2 * m) + [TOKEN_SPEC],
        out_shape=[_sems(3 * m), _sems(3 * m)] + _hbm_like(sums) + _hbm_like(lands) + [TOKEN],
        input_output_aliases={i: 2 + i for i in range(2 * m)},
        compiler_params=pltpu.CompilerParams(has_side_effects=EFFECT),
    )(*_in_hbm(sums), *_in_hbm(lands))
    return (res[0], res[1]), list(res[2:2 + m]), list(res[2 + m:2 + 2 * m]), res[2 + 2 * m]


def _chip_wait(sems, sums, lands, after, name, to_all=()):
    m = len(sums)

    def body(*refs):
        ins, land_in = refs[:m], refs[m:2 * m]
        send_in, recv_in = refs[2 * m], refs[2 * m + 1]
        x, y, c = _place()
        for i in range(m):
            for j, (cx, cy) in enumerate(_other_chips(x, y)):
                cj = 2 * cx + cy
                src = ins[i] if i in to_all else ins[i].at[cj]
                cp = _remote(src, land_in[i].at[cj], send_in.at[3 * i + j], recv_in.at[3 * i + j], (cx, cy, c))
                cp.wait_send()
                cp.wait_recv()

    res = pl.pallas_call(
        body, name=name, in_specs=[HBM_SPEC] * (2 * m) + [SEM_SPEC, SEM_SPEC] + [HBM] * len(after),
        out_specs=[HBM_SPEC] * (2 * m), out_shape=_hbm_like(sums) + _hbm_like(lands),
        input_output_aliases={i: i for i in range(2 * m)},
        compiler_params=pltpu.CompilerParams(has_side_effects=EFFECT),
    )(*sums, *lands, sems[0], sems[1], *after)
    return list(res[:m]), list(res[m:])


def _half_parts(bufs):
    parts = []
    for w, a in enumerate(bufs):
        parts += [(w, None)] if a.ndim == 3 else [(w, j) for j in range(a.shape[0])]
    return parts


def _half_ref(refs, w, j, h):
    return refs[w].at[h] if j is None else refs[w].at[j, h]


def _half_start(bufs, name):
    n = len(bufs)
    parts = _half_parts(bufs)

    def body(*refs):
        ins = refs[:n]
        send, recv = refs[n], refs[n + 1]
        x, y, c = _place()
        for k, (w, j) in enumerate(parts):
            mine = _half_ref(ins, w, j, c)
            _remote(mine, mine, send.at[k], recv.at[k], (x, y, 1 - c)).start()

    res = pl.pallas_call(
        body, name=name, in_specs=[HBM_SPEC] * n, out_specs=[SEM_SPEC, SEM_SPEC] + [HBM_SPEC] * n,
        out_shape=[_sems(len(parts)), _sems(len(parts))] + _hbm_like(bufs),
        input_output_aliases={w: 2 + w for w in range(n)},
        compiler_params=pltpu.CompilerParams(has_side_effects=EFFECT),
    )(*_in_hbm(bufs))
    return (res[0], res[1]), list(res[2:])


def _half_wait(sems, bufs, after, name):
    n = len(bufs)
    parts = _half_parts(bufs)

    def body(*refs):
        ins = refs[:n]
        send_in, recv_in = refs[n], refs[n + 1]
        x, y, c = _place()
        for k, (w, j) in enumerate(parts):
            mine = _half_ref(ins, w, j, c)
            _remote(mine, mine, send_in.at[k], recv_in.at[k], (x, y, 1 - c)).wait_send()
            theirs = _half_ref(ins, w, j, 1 - c)
            _remote(theirs, theirs, send_in.at[k], recv_in.at[k], (x, y, 1 - c)).wait_recv()

    return list(pl.pallas_call(
        body, name=name, in_specs=[HBM_SPEC] * n + [SEM_SPEC, SEM_SPEC] + [HBM] * len(after),
        out_specs=[HBM_SPEC] * n, out_shape=_hbm_like(bufs),
        input_output_aliases={w: w for w in range(n)},
        compiler_params=pltpu.CompilerParams(has_side_effects=EFFECT),
    )(*bufs, sems[0], sems[1], *after))


def _all_gather8(block, name, after=()):
    def body(in_ref, *rest):
        out_ref, send_sems, recv_sems, local_sem = rest[len(after):]
        x, y, c = _place()
        me = 4 * x + 2 * y + c
        mine = pltpu.make_async_copy(in_ref, out_ref.at[me], local_sem)
        mine.start()
        flips = [(fx, fy, fc) for fx in (0, 1) for fy in (0, 1) for fc in (0, 1)][1:]
        cps = []
        for k, (fx, fy, fc) in enumerate(flips):
            cp = pltpu.make_async_remote_copy(
                src_ref=in_ref, dst_ref=out_ref.at[me], send_sem=send_sems.at[k], recv_sem=recv_sems.at[k],
                device_id=(x ^ fx, y ^ fy, c ^ fc), device_id_type=MESH)
            cp.start()
            cps.append(cp)
        for k, (fx, fy, fc) in enumerate(flips):
            px, py, pc = x ^ fx, y ^ fy, c ^ fc
            theirs = out_ref.at[4 * px + 2 * py + pc]
            pltpu.make_async_remote_copy(
                src_ref=theirs, dst_ref=theirs, send_sem=send_sems.at[k], recv_sem=recv_sems.at[k],
                device_id=(px, py, pc), device_id_type=MESH).wait_recv()
        for cp in cps:
            cp.wait_send()
        mine.wait()

    return pl.pallas_call(
        body, name=name, in_specs=[HBM] * (1 + len(after)), out_specs=HBM,
        out_shape=jax.ShapeDtypeStruct((N_DEV,) + block.shape, block.dtype),
        scratch_shapes=[pltpu.SemaphoreType.DMA((N_DEV - 1,)), pltpu.SemaphoreType.DMA((N_DEV - 1,)),
                        pltpu.SemaphoreType.DMA],
    )(block, *after)


def _pair_sum(grad, recv, c_arr, name):
    _, _, rh, cols = grad.shape
    tr = _row_tile(rh, cols * grad.dtype.itemsize, STREAM_TILE // 2)

    def body(c_ref, g_ref, r_ref, o_ref):
        del c_ref
        o_ref[...] = (g_ref[...].astype(F32) + r_ref[...].astype(F32)).astype(o_ref.dtype)

    spec = pl.BlockSpec((None, tr, cols), lambda j, i, c_ref: (j, i, 0))
    return pl.pallas_call(
        body, name=name,
        grid_spec=pltpu.PrefetchScalarGridSpec(
            num_scalar_prefetch=1, grid=(N_CHIPS, rh // tr),
            in_specs=[pl.BlockSpec((None, None, tr, cols), lambda j, i, c_ref: (j, c_ref[0], i, 0)), spec],
            out_specs=spec),
        out_shape=jax.ShapeDtypeStruct(recv.shape, grad.dtype), compiler_params=_params(32),
    )(c_arr, grad, recv)


def _sum_by_chip(chip, p_ref, own_ref, o_ref):
    o_ref[...] = jnp.zeros_like(o_ref)
    for k in range(N_CHIPS):
        @pl.when(chip == k)
        def _():
            o_ref[...] += own_ref[...].astype(F32)

        @pl.when(chip != k)
        def _(k=k):
            o_ref[...] += p_ref[k].astype(F32)


def _chip_sum_all(parts, own, place_arr, name):
    _, nj, rh, cols = parts.shape
    tr = _row_tile(rh, cols * 4, STREAM_TILE // 2)

    def body(place_ref, p_ref, own_ref, o_ref):
        _sum_by_chip(place_ref[0], p_ref, own_ref, o_ref)

    return pl.pallas_call(
        body, name=name,
        grid_spec=pltpu.PrefetchScalarGridSpec(
            num_scalar_prefetch=1, grid=(nj, rh // tr),
            in_specs=[pl.BlockSpec((N_CHIPS, None, tr, cols), lambda j, i, place_ref: (0, j, i, 0)),
                      pl.BlockSpec((None, tr, cols), lambda j, i, place_ref: (j, i, 0))],
            out_specs=pl.BlockSpec((None, None, tr, cols), lambda j, i, place_ref: (j, place_ref[1], i, 0))),
        out_shape=jax.ShapeDtypeStruct((nj, 2, rh, cols), F32), compiler_params=_params(32),
    )(place_arr, parts, own)


def _chip_sum(parts, own, place_arr, name):
    _, rh, cols = parts.shape
    tr = _row_tile(rh, cols * 4, STREAM_TILE // 2)

    def body(place_ref, p_ref, own_ref, o_ref):
        _sum_by_chip(place_ref[0], p_ref, own_ref, o_ref)

    return pl.pallas_call(
        body, name=name,
        grid_spec=pltpu.PrefetchScalarGridSpec(
            num_scalar_prefetch=1, grid=(rh // tr,),
            in_specs=[pl.BlockSpec((N_CHIPS, tr, cols), lambda i, place_ref: (0, i, 0)),
                      pl.BlockSpec((None, tr, cols), lambda i, place_ref: (place_ref[0], i, 0))],
            out_specs=pl.BlockSpec((None, tr, cols), lambda i, place_ref: (place_ref[1], i, 0))),
        out_shape=jax.ShapeDtypeStruct((2, rh, cols), F32), compiler_params=_params(32),
    )(place_arr, parts, own)


def _adamw_math(w, g, m, v):
    m = ADAM_B1 * m + (1.0 - ADAM_B1) * g
    v = ADAM_B2 * v + (1.0 - ADAM_B2) * (g * g)
    m_hat = m / (1.0 - ADAM_B1 ** ADAM_STEP)
    v_hat = v / (1.0 - ADAM_B2 ** ADAM_STEP)
    delta = -ADAM_LR * (m_hat / (jnp.sqrt(v_hat) + ADAM_EPS) + ADAM_WD * w)
    return delta, m, v


def _adamw(w, g, m, v, name):
    rows, cols = w.shape
    tr = _row_tile(rows, cols * 4)

    def body(w_ref, g_ref, m_ref, v_ref, go_ref, d_ref, nm_ref, nv_ref):
        gv = g_ref[...]
        go_ref[...] = gv
        d_ref[...], nm_ref[...], nv_ref[...] = _adamw_math(w_ref[...], gv, m_ref[...], v_ref[...])

    spec = pl.BlockSpec((tr, cols), lambda i: (i, 0))
    return pl.pallas_call(
        body, name=name, grid=(rows // tr,), in_specs=[spec] * 4, out_specs=[spec] * 4,
        out_shape=[jax.ShapeDtypeStruct(w.shape, F32)] * 4, compiler_params=_params(32),
    )(w, g, m, v)


def _sum8_adamw_row(parts, w, m, v, name):
    cols = parts.shape[2]

    def body(p_ref, w_ref, m_ref, v_ref, g_ref, d_ref, nm_ref, nv_ref):
        g = p_ref[0, 0:1, :]
        for k in range(1, N_DEV):
            g = g + p_ref[k, 0:1, :]
        g_ref[...] = g
        d_ref[...], nm_ref[...], nv_ref[...] = _adamw_math(w_ref[...], g, m_ref[...], v_ref[...])

    return pl.pallas_call(
        body, name=name, out_shape=[jax.ShapeDtypeStruct((1, cols), F32)] * 4, compiler_params=_params(32),
    )(parts, w, m, v)


def _pack_rows(pieces, rows, name):
    cols = pieces[0].shape[1]
    n = len(pieces)

    def body(*refs):
        o_ref = refs[n]
        o_ref[...] = jnp.zeros_like(o_ref)
        at = 0
        for p_ref in refs[:n]:
            r = p_ref.shape[0]
            o_ref[at:at + r, :] = p_ref[...]
            at += r

    return pl.pallas_call(
        body, name=name, out_shape=jax.ShapeDtypeStruct((rows, cols), F32), compiler_params=_params(32),
    )(*pieces)


def kernel(x, norm_mix_g, w_in, conv_w, conv_b, lru_wa, lru_ba, lru_wx, lru_bx, lru_lambda, w_proj_attn, w_proj_lru, w_out, norm_mlp_g, w_up, w_down, norm_final_g, loss_target, m_norm_mix_g, m_w_in, m_conv_w, m_conv_b, m_lru_wa, m_lru_ba, m_lru_wx, m_lru_bx, m_lru_lambda, m_w_proj_attn, m_w_proj_lru, m_w_out, m_norm_mlp_g, m_w_up, m_w_down, m_norm_final_g, v_norm_mix_g, v_w_in, v_conv_w, v_conv_b, v_lru_wa, v_lru_ba, v_lru_wx, v_lru_bx, v_lru_lambda, v_w_proj_attn, v_w_proj_lru, v_w_out, v_norm_mlp_g, v_w_up, v_w_down, v_norm_final_g):
    s, d = x.shape[1], x.shape[2]
    ff = w_up.shape[2] * N_CHIPS
    heads = d // HEAD_DIM
    u = d // 4
    assert s % (max(DILATIONS) * ATTN_BLK) == 0 and d % (4 * HEAD_DIM) == 0 and ff == 4 * d and DILATIONS[0] == 1
    xs, target = _in_hbm([x[0], loss_target[0]])
    gf = norm_final_g.reshape(1, d)
    wa, wx = lru_wa[0], lru_wx[0]
    core = lax.axis_index("c").astype(jnp.int32)
    chip = (2 * lax.axis_index("x") + lax.axis_index("y")).astype(jnp.int32)
    cidx = core.reshape(1)
    chip_arr = chip.reshape(1)
    place_arr = jnp.stack([chip, core])
    slopes = jnp.broadcast_to(
        (2.0 ** (-8.0 * jnp.arange(1, heads + 1, dtype=F32) / heads))[:, None, None], (heads, 1, HEAD_DIM))

    big = _in_hbm([w_in[0], w_proj_attn[0], w_proj_lru[0], w_out[0], w_up[0], w_down[0]])
    names = ["w_in", "w_proj_attn", "w_proj_lru", "w_out", "w_up", "w_down"]
    cw_pad = jnp.pad(conv_w[0], ((0, 8 - CONV_TAPS), (0, 0)))
    cw_all = _all_gather8(cw_pad, "gather_conv_w")
    conv_w_full = jnp.concatenate([cw_all[2 * j] for j in range(N_CHIPS)], axis=1)
    (sem_a,), buf_a = _gather_start([_cast_shard(big[0], chip_arr, "cast_w_in")], [([0], NEIGHBOURS)], [cw_all],
                                    "gather_start_w_in")

    xn = _rms_fwd(xs, norm_mix_g, "norm_mix")
    flip_bits = (2, 1, 3)

    def w_in_view():
        return buf_a[0].reshape(N_CHIPS, d, N_SLOTS * u)

    proj = _proj_in_shard(xn, w_in_view(), chip_arr, None, "proj_in_own")
    bufs = [_cast_shard(w, chip_arr, "cast_" + nm) for w, nm in zip(big[1:], names[1:])]
    (fs, fr, rs, rr), buf_a = _gather_forward_relay(buf_a, sem_a[1], [proj] + bufs, "gather_forward_w_in")
    buf_a = _gather_finish(buf_a, sem_a[0], fs, fr, [], "gather_finish_w_in", flips=NEIGHBOURS)
    for j in NEIGHBOURS:
        proj = _proj_in_shard(xn, w_in_view(), chip_arr ^ flip_bits[j], proj, "proj_in_from_%d" % j)
    (sem_b, sem_c, sem_d), bufs = _gather_start(
        bufs, [([0, 1, 2], ALL_FLIPS), ([3], ALL_FLIPS), ([4], ALL_FLIPS)], [proj], "gather_start_rest")
    (ds, dr), buf_a = _gather_forward_diag(buf_a, rr, [proj, bufs[0]], "gather_forward_w_in_diag")
    buf_a = _gather_finish_diag(buf_a, rs, ds, dr, [], "gather_finish_w_in_diag")
    proj = _proj_in_shard(xn, w_in_view(), chip_arr ^ flip_bits[2], proj, "proj_in_from_2")
    w_in_g = w_in_view()
    proj_a = proj_b = proj
    y_attn, lse = _attn_fwd(proj_a, slopes)
    y_lru, h_lru = _lru_fwd(proj_b, conv_w_full, conv_b, wa, lru_ba, wx, lru_bx, lru_lambda)
    fsem_b, buf_b = _gather_forward(bufs[:3], sem_b[1], [y_attn, y_lru], "gather_forward_proj")
    fsem_c, buf_c = _gather_forward(bufs[3:4], sem_c[1], [buf_b[0]], "gather_forward_w_up")
    buf_b = _gather_finish(buf_b, sem_b[0], fsem_b[0], fsem_b[1], [buf_c[0]], "gather_finish_proj")
    wpa_g = buf_b[0].reshape(d, d)
    wpl_g = buf_b[1].reshape(d, d)
    wout_g = buf_b[2].reshape(d, d)

    tn = u
    sd_f32 = jax.ShapeDtypeStruct((s, d), F32)
    sd_bf16 = jax.ShapeDtypeStruct((s, d), BF16)
    col = pl.BlockSpec((s, tn), lambda i, j, k: (0, j))

    def slot(n):
        return pl.BlockSpec((None, s, tn), lambda i, j, k: (n, 0, j))

    p_attn = _mm_nn("proj_attn", y_attn, wpa_g, [], [], [sd_f32], [col], _store, tn)[0]

    def merge(acc, extras, outs):
        pa_ref, ga_ref, gl_ref = extras
        merged = _sigmoid(ga_ref[...]) * pa_ref[...] + _sigmoid(gl_ref[...]) * acc
        outs[0][...] = merged.astype(BF16)
        outs[1][...] = acc

    tn2 = max(HEAD_DIM, u // 2)
    col2 = pl.BlockSpec((s, tn2), lambda i, j, k: (0, j))

    def slot2(n):
        return pl.BlockSpec((None, s, tn2), lambda i, j, k: (n, 0, j))

    merged, p_lru = _mm_nn("proj_lru_merge", y_lru, wpl_g, [p_attn, proj, proj], [col2, slot2(5), slot2(6)],
                           [sd_bf16, sd_f32], [col2, col2], merge, tn2)

    def add_resid(acc, extras, outs):
        outs[0][...] = extras[0][...] + acc

    h1 = _mm_nn("w_out_resid", merged, wout_g, [xs], [col], [sd_f32], [col], add_resid, tn)[0]
    hn = _rms_fwd(h1, norm_mlp_g, "norm_mlp")
    buf_c = _gather_finish(buf_c, sem_c[0], fsem_c[0], fsem_c[1], [hn], "gather_finish_w_up")
    wup_g = buf_c[0].reshape(N_CHIPS, d, d)

    def relu_sq(acc, extras, outs):
        r = jnp.maximum(acc, 0.0)
        outs[0][...] = (r * r).astype(BF16)
        outs[1][...] = r.astype(BF16)

    sf_bf16 = jax.ShapeDtypeStruct((s, ff), BF16)
    hid, relu_up = _mm(
        "w_up_relu2", [hn, wup_g],
        [pl.BlockSpec((s, d), lambda i, j, k: (0, 0)),
         pl.BlockSpec((None, d, tn), lambda i, j, k: (j // 4, 0, j % 4))],
        [sf_bf16, sf_bf16], [col, col], (1, ff // tn, 1), NN, relu_sq)
    fsem_d, buf_d = _gather_forward(bufs[4:], sem_d[1], [hid], "gather_forward_w_down")
    wdown_g = _gather_finish(buf_d, sem_d[0], fsem_d[0], fsem_d[1], [], "gather_finish_w_down")[0].reshape(ff, d)
    h2 = _mm(
        "w_down_resid", [hid, wdown_g, h1],
        [pl.BlockSpec((s, d), lambda i, j, k: (0, k)), pl.BlockSpec((d, tn), lambda i, j, k: (k, j)), col],
        [sd_f32], [col], (1, d // tn, ff // d), NN, add_resid, nk=ff // d, acc_shape=(s, tn))[0]
    loss_part, dh2, dh2_b, d_gf = _loss_head(h2, gf, target)
    loss = lax.psum(loss_part[0, 0], AXES)

    def relu_sq_bwd(acc, extras, outs):
        outs[0][...] = (acc * (2.0 * extras[0][...].astype(F32))).astype(BF16)

    dup = _mm_nt("d_hid", dh2_b, wdown_g, [relu_up], [col], [sf_bf16], [col], relu_sq_bwd, tn)[0]
    tok_d = pl.BlockSpec((s, d), lambda i, j: (0, 0))
    g_wdown = _mm_tn(
        "g_w_down", hid, dh2_b, pl.BlockSpec((s, d), lambda i, j: (0, i)),
        pl.BlockSpec((s, tn), lambda i, j: (0, j)), jax.ShapeDtypeStruct((ff, d), BF16),
        pl.BlockSpec((d, tn), lambda i, j: (i, j)), (ff // d, d // tn), d, tn, s)
    dhn = _mm(
        "d_hn", [dup, wup_g],
        [pl.BlockSpec((s, d), lambda i, j, k: (0, k)), pl.BlockSpec((None, tn, d), lambda i, j, k: (k, j, 0))],
        [sd_f32], [col], (1, d // tn, ff // d), NT, _store, nk=ff // d, acc_shape=(s, tn))[0]
    g_wup = _mm_tn(
        "g_w_up", hn, dup, tok_d, pl.BlockSpec((s, tn), lambda i, j: (0, j)),
        jax.ShapeDtypeStruct((N_CHIPS, d, d), BF16), pl.BlockSpec((None, d, tn), lambda i, j: (j // 4, 0, j % 4)),
        (1, ff // tn), d, tn, s)
    big_m = _in_hbm([m_w_in[0], m_w_proj_attn[0], m_w_proj_lru[0], m_w_out[0], m_w_up[0], m_w_down[0]])
    big_v = _in_hbm([v_w_in[0], v_w_proj_attn[0], v_w_proj_lru[0], v_w_out[0], v_w_up[0], v_w_down[0]])
    big_out = {}

    def reduce_begin(ids, gs, tag, everywhere=None):
        g4 = [g.reshape(N_CHIPS, 2, big[i].shape[0] // 2, big[i].shape[1]) for i, g in zip(ids, gs)]
        tags = [names[i] for i in ids]
        if everywhere is not None:
            g4.append(everywhere.reshape(N_CHIPS, 2, everywhere.shape[0] // (2 * N_CHIPS), everywhere.shape[1]))
            tags.append("small_" + tag)
        from_sibling = _pair_exchange(g4, "pair_exchange_" + tag)
        sums = [_pair_sum(g, r, cidx, "pair_sum_" + t) for t, g, r in zip(tags, g4, from_sibling)]
        to_all = () if everywhere is None else (len(ids),)
        return _chip_start(sums, "chip_start_" + tag, to_all), to_all

    def pair_begin(ids, gs, tag):
        g4 = [g.reshape(N_CHIPS, 2, big[i].shape[0] // 2, big[i].shape[1]) for i, g in zip(ids, gs)]
        return _pair_start(g4, "pair_start_" + tag)

    def reduce_begin_paired(ids, paired, after, tag):
        sems, g4, lands, _ = paired
        g4, from_sibling = _pair_wait(sems, g4, lands, after, "pair_wait_" + tag)
        sums = [_pair_sum(g, r, cidx, "pair_sum_" + names[i]) for i, g, r in zip(ids, g4, from_sibling)]
        return _chip_start(sums, "chip_start_" + tag), ()

    def reduce_mid(ids, begun, after, tag):
        (sems, sums, lands, _), to_all = begun
        sums, lands = _chip_wait(sems, sums, lands, after, "chip_wait_" + tag, to_all)
        halves = [_chip_sum(p, own, place_arr, "chip_sum_" + names[i]) for i, p, own in zip(ids, lands, sums)]
        if to_all:
            halves.append(_chip_sum_all(lands[-1], sums[-1], place_arr, "chip_sum_small_" + tag))
        return _half_start(halves, "half_start_" + tag), to_all

    def reduce_end(ids, mid, after, tag):
        (hsems, halves), to_all = mid
        full = _half_wait(hsems, halves, after, "half_wait_" + tag)
        last = None
        for i, g in zip(ids, full):
            res = _adamw(big[i], g.reshape(big[i].shape), big_m[i], big_v[i], "adamw_" + names[i])
            big_out[names[i]] = tuple(a[None] for a in res)
            last = res[1]
        everywhere = full[-1].reshape(-1, full[-1].shape[-1]) if to_all else None
        return everywhere, last

    def after_token(a, begun):
        return a + begun[0][3][:1, :1]

    pair_mlp = pair_begin([4, 5], [g_wup, g_wdown], "mlp")
    dh1, dh1_b, d_gmlp = _rms_bwd(h1, norm_mlp_g + pair_mlp[3][:1, :1], dhn, dh2, "norm_mlp_bwd")

    g_wout = _mm_tn(
        "g_w_out", merged, dh1_b, tok_d, pl.BlockSpec((s, tn), lambda i, j: (0, j)),
        jax.ShapeDtypeStruct((d, d), BF16), pl.BlockSpec((d, tn), lambda i, j: (0, j)), (1, d // tn), d, tn, s)

    def merge_bwd(acc, extras, outs):
        pa_ref, pl_ref, ga_ref, gl_ref = extras
        sa, sl = _sigmoid(ga_ref[...]), _sigmoid(gl_ref[...])
        outs[0][...] = (acc * sa).astype(BF16)
        outs[1][...] = (acc * sl).astype(BF16)
        outs[2][0] = (acc * pa_ref[...] * (sa * (1.0 - sa))).astype(BF16)
        outs[2][1] = (acc * pl_ref[...] * (sl * (1.0 - sl))).astype(BF16)

    nb = N_SLOTS - N_QKV
    d_pa, d_pl, dproj_b = _mm_nt(
        "d_merged", dh1_b, wout_g, [p_attn, p_lru, proj, proj], [col2, col2, slot2(5), slot2(6)],
        [sd_bf16, sd_bf16, jax.ShapeDtypeStruct((nb, s, d), BF16)],
        [col2, col2, pl.BlockSpec((2, s, tn2), lambda i, j, k: (1, 0, j))], merge_bwd, tn2)
    red_mlp = reduce_begin_paired([4, 5], pair_mlp, [d_pa], "mlp")
    dy_attn = _mm_nt("d_y_attn", d_pa, wpa_g, [], [], [sd_f32], [col], _store, tn)[0]
    dy_lru = _mm_nt("d_y_lru", d_pl, wpl_g, [], [], [sd_f32], [col], _store, tn)[0]
    g_wpa = _mm_tn(
        "g_w_proj_attn", y_attn, d_pa, tok_d, pl.BlockSpec((s, tn), lambda i, j: (0, j)),
        jax.ShapeDtypeStruct((d, d), BF16), pl.BlockSpec((d, tn), lambda i, j: (0, j)), (1, d // tn), d, tn, s)
    g_wpl = _mm_tn(
        "g_w_proj_lru", y_lru, d_pl, tok_d, pl.BlockSpec((s, tn), lambda i, j: (0, j)),
        jax.ShapeDtypeStruct((d, d), BF16), pl.BlockSpec((d, tn), lambda i, j: (0, j)), (1, d // tn), d, tn, s)

    pair_proj = pair_begin([1, 2, 3], [g_wpa, g_wpl, g_wout], "proj")

    dproj_b, d_cw, d_cb, d_wa, d_ba, d_wx, d_bx, d_lam = _lru_bwd(
        proj_b, h_lru, dy_lru, conv_w_full, conv_b, wa, lru_ba, wx, lru_bx, lru_lambda + pair_proj[3][:1, :1],
        dproj_b)
    red_proj = reduce_begin_paired([1, 2, 3], pair_proj, [dproj_b], "proj")
    dproj_a = _attn_bwd(proj_a, after_token(slopes, red_proj), y_attn, lse, dy_attn)
    per = N_SLOTS
    g_win_shape = jax.ShapeDtypeStruct((N_CHIPS, d, N_SLOTS * u), BF16)

    def g_win_part(name, dproj, first, prev):
        n_units = 4 * dproj.shape[0]
        return _mm_tn(
            name, xn, dproj, tok_d, pl.BlockSpec((None, s, u), lambda i, j: (j // 4, 0, j % 4)),
            g_win_shape, pl.BlockSpec((None, d, u), lambda i, j: ((j + first) // per, 0, (j + first) % per)),
            (1, n_units), d, u, s, aliases=None if prev is None else {2: 0}, extra=prev)

    mat_rows = heads * HEAD_DIM * HEAD_DIM // d
    vec_names = ["norm_mix_g", "conv_b", "lru_ba", "lru_bx", "lru_lambda", "norm_mlp_g", "norm_final_g"]

    def pack(wa_, wx_, cw_, vecs, name):
        rows = [wa_.reshape(mat_rows, d), wx_.reshape(mat_rows, d), cw_] + [a.reshape(1, d) for a in vecs]
        n = sum(a.shape[0] for a in rows)
        return _pack_rows(rows, n + (-n % 64), name)

    zero_cw = jnp.zeros((CONV_TAPS, d), F32)
    small_g = pack(d_wa, d_wx, d_cw, [jnp.zeros((1, d), F32), d_cb, d_ba, d_bx, d_lam, d_gmlp, d_gf], "pack_small_g")
    g_win = g_win_part("g_w_in_qkv", dproj_a, 0, None)
    g_win = g_win_part("g_w_in_rest", dproj_b, 4 * N_QKV, g_win)
    red_in = reduce_begin([0], [g_win], "w_in", everywhere=small_g)
    dxn = _dxn(dproj_a, dproj_b, w_in_g, 2 * tn, [red_in[0][3]])
    grad_x, _, d_gmix = _rms_bwd(xs, norm_mix_g, dxn, dh1, "norm_mix_bwd")

    mid_mlp = reduce_mid([4, 5], red_mlp, [grad_x], "mlp")
    mid_proj = reduce_mid([1, 2, 3], red_proj, [mid_mlp[0][1][0]], "proj")
    _, done = reduce_end([4, 5], mid_mlp, [mid_proj[0][1][0]], "mlp")
    _, done = reduce_end([1, 2, 3], mid_proj, [done], "proj")
    small_w = pack(wa, wx, zero_cw, [norm_mix_g, conv_b, lru_ba, lru_bx, lru_lambda, norm_mlp_g, norm_final_g],
                   "pack_small_w")
    small_m = pack(m_lru_wa[0], m_lru_wx[0], zero_cw,
                   [m_norm_mix_g, m_conv_b, m_lru_ba, m_lru_bx, m_lru_lambda, m_norm_mlp_g, m_norm_final_g],
                   "pack_small_m")
    small_v = pack(v_lru_wa[0], v_lru_wx[0], zero_cw,
                   [v_norm_mix_g, v_conv_b, v_lru_ba, v_lru_bx, v_lru_lambda, v_norm_mlp_g, v_norm_final_g],
                   "pack_small_v")
    mid_in = reduce_mid([0], red_in, [done, small_w, small_m, small_v], "w_in")
    gmix_parts = _all_gather8(jnp.pad(d_gmix, ((0, 7), (0, 0))), "gather_gain_grad", [mid_in[0][1][0]])
    gmix_out = _sum8_adamw_row(gmix_parts, norm_mix_g, m_norm_mix_g, v_norm_mix_g, "sum_adamw_norm_mix_g")
    small_sum, _ = reduce_end([0], mid_in, [gmix_out[1]], "w_in")
    small = _adamw(small_w, small_sum, small_m, small_v, "adamw_small")
    g_cw = lax.dynamic_slice(small_sum[2 * mat_rows:2 * mat_rows + CONV_TAPS], (0, chip * u), (CONV_TAPS, u))
    cw_out = _adamw(conv_w[0], g_cw, m_conv_w[0], v_conv_w[0], "adamw_conv_w")

    def small_leaf(kind, name):
        a = small[kind]
        if name == "norm_mix_g":
            return gmix_out[kind]
        if name == "lru_wa":
            return a[0:mat_rows].reshape(lru_wa.shape)
        if name == "lru_wx":
            return a[mat_rows:2 * mat_rows].reshape(lru_wx.shape)
        if name == "conv_w":
            return cw_out[kind][None]
        row = a[2 * mat_rows + CONV_TAPS + vec_names.index(name)]
        return row if name == "norm_final_g" else row[None]

    order = ["norm_mix_g", "w_in", "conv_w", "conv_b", "lru_wa", "lru_ba", "lru_wx", "lru_bx", "lru_lambda",
             "w_proj_attn", "w_proj_lru", "w_out", "norm_mlp_g", "w_up", "w_down", "norm_final_g"]
    outs = [loss, grad_x[None]]
    for kind in range(4):
        for name in order:
            outs.append(big_out[name][kind] if name in big_out else small_leaf(kind, name))
    return tuple(outs)
```

```python
import functools

import jax
import jax.numpy as jnp
from jax import lax
from jax.experimental import pallas as pl
from jax.experimental.pallas import tpu as pltpu

F32 = jnp.float32
BF16 = jnp.bfloat16
MESH = pl.DeviceIdType.MESH
AXES = ("x", "y", "c")

N_CHIPS = 4
N_DEV = 8
HEAD_DIM = 128
ATTN_BLK = 128
DILATIONS = (1, 4, 16)
ATTN_UNROLL = 8
CONV_TAPS = 4
LRU_C = 8.0
EPS = 1e-6
N_SLOTS = 7
N_QKV = 3
VMEM_MIB = 2 ** 20
VMEM_V7X = 64 * VMEM_MIB
STREAM_TILE = 4 * VMEM_MIB

ADAM_LR = 0.001
ADAM_B1 = 0.9
ADAM_B2 = 0.999
ADAM_EPS = 1e-08
ADAM_WD = 0.01
ADAM_STEP = 10

NN = (((1,), (0,)), ((), ()))
NT = (((1,), (1,)), ((), ()))
TN = (((0,), (0,)), ((), ()))


def _params(vmem_mib=None, **kw):
    limit = None if vmem_mib is None else min(vmem_mib * VMEM_MIB, VMEM_V7X - 8 * VMEM_MIB)
    return pltpu.CompilerParams(vmem_limit_bytes=limit, **kw)


def _row_tile(rows, row_bytes, budget=VMEM_MIB):
    t = rows
    while t % 16 == 0 and t * row_bytes > budget:
        t //= 2
    return t


def _dot(a, b, dims):
    return lax.dot_general(a.astype(BF16), b.astype(BF16), dims, preferred_element_type=F32)


def _sigmoid(x):
    return jax.nn.sigmoid(x)


def _rms_fwd(x, g, name):
    s, d = x.shape
    tm = _row_tile(s, d * 4)

    def body(x_ref, g_ref, o_ref):
        xf = x_ref[...]
        r = lax.rsqrt(jnp.mean(xf * xf, axis=-1, keepdims=True) + EPS)
        o_ref[...] = (xf * r * g_ref[...]).astype(o_ref.dtype)

    return pl.pallas_call(
        body, name=name, grid=(s // tm,),
        in_specs=[pl.BlockSpec((tm, d), lambda i: (i, 0)), pl.BlockSpec((1, d), lambda i: (0, 0))],
        out_specs=pl.BlockSpec((tm, d), lambda i: (i, 0)),
        out_shape=jax.ShapeDtypeStruct((s, d), BF16), compiler_params=_params(32),
    )(x, g)


def _rms_bwd(x, g, dy, resid, name):
    s, d = x.shape
    tm = _row_tile(s, d * 4)

    def body(x_ref, g_ref, dy_ref, res_ref, dx_ref, dxb_ref, dg_ref):
        xf = x_ref[...]
        r = lax.rsqrt(jnp.mean(xf * xf, axis=-1, keepdims=True) + EPS)
        xh = xf * r
        dyv = dy_ref[...]
        dxh = dyv * g_ref[...]
        dx = r * (dxh - xh * jnp.mean(dxh * xh, axis=-1, keepdims=True)) + res_ref[...]
        dx_ref[...] = dx
        dxb_ref[...] = dx.astype(BF16)
        part = jnp.sum(dyv * xh, axis=0, keepdims=True)

        @pl.when(pl.program_id(0) == 0)
        def _():
            dg_ref[...] = part

        @pl.when(pl.program_id(0) > 0)
        def _():
            dg_ref[...] += part

    row = pl.BlockSpec((tm, d), lambda i: (i, 0))
    vec = pl.BlockSpec((1, d), lambda i: (0, 0))
    return pl.pallas_call(
        body, name=name, grid=(s // tm,),
        in_specs=[row, vec, row, row], out_specs=[row, row, vec],
        out_shape=[jax.ShapeDtypeStruct((s, d), F32), jax.ShapeDtypeStruct((s, d), BF16),
                   jax.ShapeDtypeStruct((1, d), F32)],
        compiler_params=_params(32),
    )(x, g, dy, resid)


def _loss_head(h2, g, target):
    s, d = h2.shape
    tm = _row_tile(s, d * 4)

    def body(x_ref, g_ref, t_ref, loss_ref, dx_ref, dxb_ref, dg_ref):
        xf = x_ref[...]
        gv = g_ref[...]
        r = lax.rsqrt(jnp.mean(xf * xf, axis=-1, keepdims=True) + EPS)
        xh = xf * r
        err = xh * gv - t_ref[...]
        part = jnp.sum(jnp.sum(err * err, axis=1, keepdims=True), axis=0, keepdims=True) * (0.5 / d)
        dyv = err * (1.0 / d)
        dxh = dyv * gv
        dx = r * (dxh - xh * jnp.mean(dxh * xh, axis=-1, keepdims=True))
        dx_ref[...] = dx
        dxb_ref[...] = dx.astype(BF16)
        dgp = jnp.sum(dyv * xh, axis=0, keepdims=True)

        @pl.when(pl.program_id(0) == 0)
        def _():
            dg_ref[...] = dgp
            loss_ref[...] = jnp.broadcast_to(part, loss_ref.shape)

        @pl.when(pl.program_id(0) > 0)
        def _():
            dg_ref[...] += dgp
            loss_ref[...] += jnp.broadcast_to(part, loss_ref.shape)

    row = pl.BlockSpec((tm, d), lambda i: (i, 0))
    vec = pl.BlockSpec((1, d), lambda i: (0, 0))
    return pl.pallas_call(
        body, name="loss_head", grid=(s // tm,),
        in_specs=[row, vec, row],
        out_specs=[pl.BlockSpec((8, 128), lambda i: (0, 0)), row, row, vec],
        out_shape=[jax.ShapeDtypeStruct((8, 128), F32), jax.ShapeDtypeStruct((s, d), F32),
                   jax.ShapeDtypeStruct((s, d), BF16), jax.ShapeDtypeStruct((1, d), F32)],
        compiler_params=_params(32),
    )(h2, g, target)


def _mm(name, operands, in_specs, out_shape, out_specs, grid, dims, epilogue, nk=1, acc_shape=None,
        vmem_mib=56, aliases=None):
    n_in = len(operands)
    n_out = len(out_shape)

    def body(*refs):
        a_ref, b_ref = refs[0], refs[1]
        extras = refs[2:n_in]
        outs = refs[n_in:n_in + n_out]

        def prod():
            return _dot(a_ref[...], b_ref[...], dims)

        if nk == 1:
            epilogue(prod(), extras, outs)
        else:
            acc = refs[n_in + n_out]
            k = pl.program_id(2)

            @pl.when(k == 0)
            def _():
                acc[...] = prod()

            @pl.when(k > 0)
            def _():
                acc[...] += prod()

            @pl.when(k == nk - 1)
            def _():
                epilogue(acc[...], extras, outs)

    scratch = [] if nk == 1 else [pltpu.VMEM(acc_shape, F32)]
    return pl.pallas_call(
        body, name=name, grid=grid, in_specs=in_specs, out_specs=out_specs, out_shape=out_shape,
        scratch_shapes=scratch, input_output_aliases=aliases or {},
        compiler_params=_params(vmem_mib),
    )(*operands)


def _store(acc, extras, outs):
    outs[0][...] = acc.astype(outs[0].dtype)


def _proj_in_shard(xn, w_in_g, shard_arr, prev, name):
    s, d = xn.shape
    u = d // 4
    per = N_SLOTS
    n_prev = 0 if prev is None else 1

    def body(sh_ref, x_ref, w_ref, *rest):
        del sh_ref
        rest[n_prev][...] = _dot(x_ref[...], w_ref[...], NN)

    def out_map(j, sh_ref):
        unit = per * sh_ref[0] + j
        return (unit // 4, 0, unit % 4)

    return pl.pallas_call(
        body, name=name,
        grid_spec=pltpu.PrefetchScalarGridSpec(
            num_scalar_prefetch=1, grid=(per,),
            in_specs=[pl.BlockSpec((s, d), lambda j, sh_ref: (0, 0)),
                      pl.BlockSpec((None, d, u), lambda j, sh_ref: (sh_ref[0], 0, j))] + [HBM] * n_prev,
            out_specs=pl.BlockSpec((None, s, u), out_map)),
        out_shape=jax.ShapeDtypeStruct((N_SLOTS, s, d), F32),
        input_output_aliases={3: 0} if n_prev else {},
        compiler_params=_params(56),
    )(shard_arr, xn, w_in_g, *([] if prev is None else [prev]))


def _mm_nn(name, a, b, extras, extra_specs, out_shape, out_specs, epilogue, tn, aliases=None):
    s, kdim = a.shape
    n = b.shape[1]
    return _mm(
        name, [a, b] + list(extras),
        [pl.BlockSpec((s, kdim), lambda i, j, k: (0, 0)), pl.BlockSpec((kdim, tn), lambda i, j, k: (0, j))]
        + list(extra_specs),
        out_shape, out_specs, (1, n // tn, 1), NN, epilogue, aliases=aliases)


def _mm_nt(name, a, b, extras, extra_specs, out_shape, out_specs, epilogue, tn, aliases=None):
    s, kdim = a.shape
    n = b.shape[0]
    return _mm(
        name, [a, b] + list(extras),
        [pl.BlockSpec((s, kdim), lambda i, j, k: (0, 0)), pl.BlockSpec((tn, kdim), lambda i, j, k: (j, 0))]
        + list(extra_specs),
        out_shape, out_specs, (1, n // tn, 1), NT, epilogue, aliases=aliases)


def _mm_tn(name, a, b, a_spec, b_spec, out_shape, out_spec, grid, m, tn, s, aliases=None, extra=None):
    ch = 256
    n_in = 2 if extra is None else 3

    def body(*refs):
        a_ref, b_ref = refs[0], refs[1]
        o_ref, at_ref = refs[n_in], refs[n_in + 1]

        @pl.when(pl.program_id(1) == 0)
        def _():
            for c0 in range(0, s, ch):
                at_ref[:, c0:c0 + ch] = a_ref[c0:c0 + ch, :].astype(F32).T.astype(BF16)

        o_ref[...] = _dot(at_ref[...], b_ref[...], NN).astype(o_ref.dtype)

    operands = [a, b] + ([] if extra is None else [extra])
    in_specs = [a_spec, b_spec] + ([] if extra is None else [pl.BlockSpec(memory_space=pl.ANY)])
    return pl.pallas_call(
        body, name=name, grid=grid, in_specs=in_specs, out_specs=out_spec, out_shape=out_shape,
        scratch_shapes=[pltpu.VMEM((m, s), BF16)], input_output_aliases=aliases or {},
        compiler_params=_params(56),
    )(*operands)


def _dxn(dproj_a, dproj_b, w_in_g, tn, after):
    n_a, s, d = dproj_a.shape
    u = d // 4
    ua = 4 * n_a
    nk = 4 * N_SLOTS
    per = N_SLOTS

    def body(a_ref, b_ref, w_ref, *rest):
        o_ref = rest[len(after)]
        k = pl.program_id(2)

        @pl.when(k == 0)
        def _():
            o_ref[...] = jnp.zeros_like(o_ref)

        @pl.when(k < ua)
        def _():
            o_ref[...] += _dot(a_ref[...], w_ref[...], NT)

        @pl.when(k >= ua)
        def _():
            o_ref[...] += _dot(b_ref[...], w_ref[...], NT)

    def a_map(i, j, k):
        kk = jnp.minimum(k, ua - 1)
        return (kk // 4, 0, kk % 4)

    def b_map(i, j, k):
        kk = jnp.maximum(k - ua, 0)
        return (kk // 4, 0, kk % 4)

    return pl.pallas_call(
        body, name="dxn", grid=(1, d // tn, nk),
        in_specs=[pl.BlockSpec((None, s, u), a_map), pl.BlockSpec((None, s, u), b_map),
                  pl.BlockSpec((None, tn, u), lambda i, j, k: (k // per, j, k % per))] + [HBM] * len(after),
        out_specs=pl.BlockSpec((s, tn), lambda i, j, k: (0, j)),
        out_shape=jax.ShapeDtypeStruct((s, d), F32),
        compiler_params=_params(48),
    )(dproj_a, dproj_b, w_in_g, *after)


def _attn_masks(slope, dil):
    ii = lax.broadcasted_iota(jnp.int32, (ATTN_BLK, 2 * ATTN_BLK), 0)
    jj = lax.broadcasted_iota(jnp.int32, (ATTN_BLK, 2 * ATTN_BLK), 1)
    diff = ATTN_BLK + ii - jj
    band = (diff >= 0) & (diff <= ATTN_BLK)
    bias = -(slope * float(dil)) * diff.astype(F32)
    return band, bias, jj


def _attn_window(t, nblk):
    cur = pl.ds(pl.multiple_of(t * ATTN_BLK, ATTN_BLK), ATTN_BLK)
    prev = pl.ds(pl.multiple_of(jnp.maximum(t - 1, 0) * ATTN_BLK, ATTN_BLK), ATTN_BLK)
    first = jnp.where(t % nblk == 0, ATTN_BLK, 0)
    return prev, cur, first


def _unrolled_loop(n, step, init, unroll=ATTN_UNROLL):
    def trip(i, carry):
        for k in range(unroll):
            carry = step(i * unroll + k, carry)
        return carry

    return lax.fori_loop(0, n // unroll, trip, init)


def _streams(pairs, dil, s):
    if dil == 1:
        return [src for _, src in pairs]
    seg = s // dil
    for dst, src in pairs:
        for r in range(dil):
            dst[r * seg:(r + 1) * seg, :] = src[pl.ds(r, seg, stride=dil), :].astype(dst.dtype)
    return [dst for dst, _ in pairs]


def _attn_fwd(proj_a, slopes):
    _, s, d = proj_a.shape
    heads = d // HEAD_DIM
    scale = HEAD_DIM ** -0.5
    n_t = s // ATTN_BLK
    ng = len(DILATIONS)

    def body(q_ref, k_ref, v_ref, sl_ref, o_ref, lse_ref, qd, kd, vd, od, ld, og, lg):
        slope = sl_ref[...][:, :1]
        for g, dil in enumerate(DILATIONS):
            nblk = s // dil // ATTN_BLK
            qs, ks, vs = _streams([(qd, q_ref), (kd, k_ref), (vd, v_ref)], dil, s)
            o_t, l_t = (og.at[g], lg.at[g]) if dil == 1 else (od, ld)
            band, bias, jj = _attn_masks(slope, dil)

            def blk(t, carry, nblk=nblk, band=band, bias=bias, jj=jj, qs=qs, ks=ks, vs=vs, o_t=o_t, l_t=l_t):
                prev, cur, first = _attn_window(t, nblk)
                kk = jnp.concatenate([ks[prev, :], ks[cur, :]], axis=0)
                vv = jnp.concatenate([vs[prev, :], vs[cur, :]], axis=0)
                sc = _dot(qs[cur, :], kk, NT) * scale + bias
                sc = jnp.where(band & (jj >= first), sc, -jnp.inf)
                m = jnp.max(sc, axis=1, keepdims=True)
                p = jnp.exp(sc - m)
                l = jnp.sum(p, axis=1, keepdims=True)
                o_t[cur, :] = _dot(p, vv, NN) / l
                l_t[cur, :] = jnp.broadcast_to(m + jnp.log(l), (ATTN_BLK, HEAD_DIM))
                return carry

            _unrolled_loop(n_t, blk, 0)
            seg = s // dil
            if dil > 1:
                for r in range(dil):
                    og[g, pl.ds(r, seg, stride=dil), :] = od[r * seg:(r + 1) * seg, :]
                    lg[g, pl.ds(r, seg, stride=dil), :] = ld[r * seg:(r + 1) * seg, :]

        ch = 256

        def combine(c, carry):
            rows = pl.ds(pl.multiple_of(c * ch, ch), ch)
            ls = [lg[g, rows, :] for g in range(ng)]
            mx = functools.reduce(jnp.maximum, ls)
            es = [jnp.exp(x - mx) for x in ls]
            den = functools.reduce(jnp.add, es)
            num = functools.reduce(jnp.add, [es[g] * og[g, rows, :] for g in range(ng)])
            o_ref[rows, :] = (num / den).astype(o_ref.dtype)
            lse_ref[rows, :] = mx + jnp.log(den)
            return carry

        lax.fori_loop(0, s // ch, combine, 0)

    def col(slot):
        return pl.BlockSpec((None, s, HEAD_DIM), lambda h: (slot, 0, h))

    head = pl.BlockSpec((s, HEAD_DIM), lambda h: (0, h))
    return pl.pallas_call(
        body, name="attn_fwd", grid=(heads,),
        in_specs=[col(0), col(1), col(2), pl.BlockSpec((None, 1, HEAD_DIM), lambda h: (h, 0, 0))],
        out_specs=[head, head],
        out_shape=[jax.ShapeDtypeStruct((s, d), BF16), jax.ShapeDtypeStruct((s, d), F32)],
        scratch_shapes=[pltpu.VMEM((s, HEAD_DIM), BF16)] * 3 + [pltpu.VMEM((s, HEAD_DIM), F32)] * 2
        + [pltpu.VMEM((ng, s, HEAD_DIM), F32)] * 2,
        compiler_params=_params(40),
    )(proj_a, proj_a, proj_a, slopes)


def _attn_bwd(proj_a, slopes, y_attn, lse, dy):
    _, s, d = proj_a.shape
    heads = d // HEAD_DIM
    scale = HEAD_DIM ** -0.5
    n_t = s // ATTN_BLK

    def body(q_ref, k_ref, v_ref, sl_ref, o_ref, lse_ref, dy_ref, out_ref,
             qd, kd, vd, dod, lsd, dld, delta, dqd, dkd, dvd, dqa, dka, dva):
        slope = sl_ref[...][:, :1]
        dyv = dy_ref[...]
        delta[...] = jnp.broadcast_to(
            jnp.sum(dyv * o_ref[...].astype(F32), axis=1, keepdims=True), (s, HEAD_DIM))
        for g, dil in enumerate(DILATIONS):
            nblk = s // dil // ATTN_BLK
            seg = s // dil
            qs, ks, vs, dos, lss, dls = _streams(
                [(qd, q_ref), (kd, k_ref), (vd, v_ref), (dod, dy_ref), (lsd, lse_ref), (dld, delta)], dil, s)
            dq_t, dk_t, dv_t = (dqa, dka, dva) if dil == 1 else (dqd, dkd, dvd)
            band, bias, jj = _attn_masks(slope, dil)

            def blk(t, carry, nblk=nblk, band=band, bias=bias, jj=jj, qs=qs, ks=ks, vs=vs, dos=dos, lss=lss,
                    dls=dls, dq_t=dq_t, dk_t=dk_t, dv_t=dv_t):
                ck, cv = carry
                prev, cur, first = _attn_window(t, nblk)
                q = qs[cur, :]
                do = dos[cur, :]
                lse_b = lss[cur, :]
                dl_b = dls[cur, :]
                kk = jnp.concatenate([ks[prev, :], ks[cur, :]], axis=0)
                vv = jnp.concatenate([vs[prev, :], vs[cur, :]], axis=0)
                sc = _dot(q, kk, NT) * scale + bias
                p = jnp.where(band & (jj >= first), jnp.exp(sc - jnp.concatenate([lse_b, lse_b], axis=1)), 0.0)
                dp = _dot(do, vv, NT)
                ds = p * (dp - jnp.concatenate([dl_b, dl_b], axis=1))
                dv_b = _dot(p, do, TN)
                dk_b = _dot(ds, q, TN) * scale
                dq_t[cur, :] = _dot(ds, kk, NN) * scale
                done = pl.ds(pl.multiple_of(jnp.where(t == 0, n_t, t - 1) * ATTN_BLK, ATTN_BLK), ATTN_BLK)
                dk_t[done, :] = ck + dk_b[:ATTN_BLK]
                dv_t[done, :] = cv + dv_b[:ATTN_BLK]
                return dk_b[ATTN_BLK:], dv_b[ATTN_BLK:]

            zero = jnp.zeros((ATTN_BLK, HEAD_DIM), F32)
            ck, cv = _unrolled_loop(n_t, blk, (zero, zero))
            dk_t[(n_t - 1) * ATTN_BLK:n_t * ATTN_BLK, :] = ck
            dv_t[(n_t - 1) * ATTN_BLK:n_t * ATTN_BLK, :] = cv
            if dil > 1:
                for acc, part in ((dqa, dqd), (dka, dkd), (dva, dvd)):
                    for r in range(dil):
                        acc[pl.ds(r, seg, stride=dil), :] += part[r * seg:(r + 1) * seg, :]
        out_ref[0] = dqa[...].astype(out_ref.dtype)
        out_ref[1] = dka[0:s, :].astype(out_ref.dtype)
        out_ref[2] = dva[0:s, :].astype(out_ref.dtype)

    def col(slot):
        return pl.BlockSpec((None, s, HEAD_DIM), lambda h: (slot, 0, h))

    head = pl.BlockSpec((s, HEAD_DIM), lambda h: (0, h))
    return pl.pallas_call(
        body, name="attn_bwd", grid=(heads,),
        in_specs=[col(0), col(1), col(2), pl.BlockSpec((None, 1, HEAD_DIM), lambda h: (h, 0, 0)),
                  head, head, head],
        out_specs=pl.BlockSpec((N_QKV, s, HEAD_DIM), lambda h: (0, 0, h)),
        out_shape=jax.ShapeDtypeStruct((N_QKV, s, d), BF16),
        scratch_shapes=[pltpu.VMEM((s, HEAD_DIM), BF16)] * 4 + [pltpu.VMEM((s, HEAD_DIM), F32)] * 4
        + [pltpu.VMEM((s + ATTN_BLK, HEAD_DIM), F32)] * 2 + [pltpu.VMEM((s, HEAD_DIM), F32)]
        + [pltpu.VMEM((s + ATTN_BLK, HEAD_DIM), F32)] * 2,
        compiler_params=_params(48),
    )(proj_a, proj_a, proj_a, slopes, y_attn, lse, dy)


def _expm1(x):
    small = x * (1.0 + x * (0.5 + x * (1.0 / 6.0 + x * (1.0 / 24.0 + x * (1.0 / 120.0)))))
    return jnp.where(jnp.abs(x) < 0.1, small, jnp.exp(x) - 1.0)


def _softplus(x):
    return jnp.maximum(x, 0.0) + jnp.log1p(jnp.exp(-jnp.abs(x)))


GELU_K = 0.7978845608028654
GELU_C = 0.044715


def _gelu(x):
    t = jnp.tanh(GELU_K * (x + GELU_C * x * x * x))
    return 0.5 * x * (1.0 + t), t


def _gelu_grad(x, t):
    return 0.5 * (1.0 + t) + 0.5 * x * (1.0 - t * t) * GELU_K * (1.0 + 3.0 * GELU_C * x * x)


def _lru_gates(xc, wa, ba, wx, bx, sp):
    r = _sigmoid(_dot(xc, wa, NN) + ba)
    ig = _sigmoid(_dot(xc, wx, NN) + bx)
    log_a = -LRU_C * r * sp
    a = jnp.exp(log_a)
    mult = jnp.sqrt(-_expm1(2.0 * log_a))
    return r, ig, a, mult


def _scan_fwd(a, u, tt):
    row = lax.broadcasted_iota(jnp.int32, a.shape, 0)
    sh = 1
    while sh < tt:
        keep = row >= sh
        a_s = jnp.where(keep, pltpu.roll(a, sh, 0), 1.0)
        u_s = jnp.where(keep, pltpu.roll(u, sh, 0), 0.0)
        u = a * u_s + u
        a = a * a_s
        sh *= 2
    return a, u


def _scan_bwd(b, g, tt):
    row = lax.broadcasted_iota(jnp.int32, b.shape, 0)
    sh = 1
    while sh < tt:
        keep = row < tt - sh
        b_s = jnp.where(keep, pltpu.roll(b, tt - sh, 0), 1.0)
        g_s = jnp.where(keep, pltpu.roll(g, tt - sh, 0), 0.0)
        g = g + b * g_s
        b = b * b_s
        sh *= 2
    return b, g


ROW_CHUNK = 256


def _pad_copy(xpad_ref, x_ref, s):
    xpad_ref[0:8, :] = jnp.zeros((8, HEAD_DIM), F32)
    for c0 in range(0, s, ROW_CHUNK):
        xpad_ref[8 + c0:8 + c0 + ROW_CHUNK, :] = x_ref[c0:c0 + ROW_CHUNK, :]


def _conv_rows(dst_ref, xpad_ref, cw, cb, s):
    for c0 in range(0, s, ROW_CHUNK):
        acc = cb
        for j in range(CONV_TAPS):
            off = 8 - (CONV_TAPS - 1) + j + c0
            acc = acc + cw[j:j + 1, :] * xpad_ref[off:off + ROW_CHUNK, :]
        dst_ref[c0:c0 + ROW_CHUNK, :] = acc


LRU_TILE = 128
LRU_UNROLL = 4


def _lru_specs(s, d):
    heads = d // HEAD_DIM

    def col(slot):
        return pl.BlockSpec((None, s, HEAD_DIM), lambda h: (slot, 0, h))

    vec = pl.BlockSpec((1, HEAD_DIM), lambda h: (0, h))
    mat = pl.BlockSpec((None, HEAD_DIM, HEAD_DIM), lambda h: (h, 0, 0))
    cw = pl.BlockSpec((8, HEAD_DIM), lambda h: (0, h))
    head = pl.BlockSpec((s, HEAD_DIM), lambda h: (0, h))
    return heads, col, vec, mat, cw, head


def _lru_fwd(proj_b, conv_w, conv_b, wa, ba, wx, bx, lam):
    _, s, d = proj_b.shape
    heads, col, vec, mat, cws, head = _lru_specs(s, d)
    tt = LRU_TILE

    def body(xr_ref, xg_ref, cw_ref, cb_ref, wa_ref, ba_ref, wx_ref, bx_ref, lam_ref, y_ref, h_ref, xpad, xc_s):
        _pad_copy(xpad, xr_ref, s)
        _conv_rows(xc_s, xpad, cw_ref[...], cb_ref[...], s)
        sp = _softplus(-lam_ref[...])
        wav, wxv, bav, bxv = wa_ref[...], wx_ref[...], ba_ref[...], bx_ref[...]

        def tile(i, hc):
            rows = pl.ds(pl.multiple_of(i * tt, tt), tt)
            xc = xc_s[rows, :]
            _, ig, a, mult = _lru_gates(xc, wav, bav, wxv, bxv, sp)
            pa, hl = _scan_fwd(a, mult * (ig * xc), tt)
            h = hl + pa * hc
            h_ref[rows, :] = h
            gel, _ = _gelu(xg_ref[rows, :])
            y_ref[rows, :] = (h * gel).astype(y_ref.dtype)
            return h[tt - 1:tt, :]

        _unrolled_loop(s // tt, tile, jnp.zeros((1, HEAD_DIM), F32), LRU_UNROLL)

    return pl.pallas_call(
        body, name="lru_fwd", grid=(heads,),
        in_specs=[col(N_QKV), col(N_QKV + 1), cws, vec, mat, vec, mat, vec, vec],
        out_specs=[head, head],
        out_shape=[jax.ShapeDtypeStruct((s, d), BF16), jax.ShapeDtypeStruct((s, d), F32)],
        scratch_shapes=[pltpu.VMEM((s + 8, HEAD_DIM), F32), pltpu.VMEM((s, HEAD_DIM), F32)],
        compiler_params=_params(32),
    )(proj_b, proj_b, conv_w, conv_b, wa, ba, wx, bx, lam)


def _lru_bwd(proj_b, h_lru, dy, conv_w, conv_b, wa, ba, wx, bx, lam, dproj_b):
    _, s, d = proj_b.shape
    heads, col, vec, mat, cws, head = _lru_specs(s, d)
    tt = LRU_TILE
    n_t = s // tt

    def body(xr_ref, xg_ref, h_ref, dy_ref, cw_ref, cb_ref, wa_ref, ba_ref, wx_ref, bx_ref, lam_ref, alias_ref,
             out_ref, dcw_ref, dcb_ref, dwa_ref, dba_ref, dwx_ref, dbx_ref, dlam_ref, xpad, xc_s, dxc_s):
        del alias_ref
        _pad_copy(xpad, xr_ref, s)
        cwv = cw_ref[...]
        _conv_rows(xc_s, xpad, cwv, cb_ref[...], s)
        dxc_s[s:s + 8, :] = jnp.zeros((8, HEAD_DIM), F32)
        lamv = lam_ref[...]
        sp = _softplus(-lamv)
        wav, wxv, bav, bxv = wa_ref[...], wx_ref[...], ba_ref[...], bx_ref[...]
        dwa_ref[...] = jnp.zeros_like(dwa_ref)
        dwx_ref[...] = jnp.zeros_like(dwx_ref)
        zero = jnp.zeros((1, HEAD_DIM), F32)
        row = lax.broadcasted_iota(jnp.int32, (tt, HEAD_DIM), 0)

        def tile(it, carry):
            dh_next, a_next, dba, dbx, dsp, dcb = carry
            i = n_t - 1 - it
            t0 = pl.multiple_of(i * tt, tt)
            rows = pl.ds(t0, tt)
            xc = xc_s[rows, :]
            r, ig, a, mult = _lru_gates(xc, wav, bav, wxv, bxv, sp)
            h = h_ref[rows, :]
            before = h_ref[pl.ds(pl.multiple_of(jnp.maximum(t0 - 8, 0), 8), 8), :][7:8, :]
            before = before * (i > 0).astype(F32)
            h_prev = jnp.where(row == 0, before, pltpu.roll(h, 1, 0))
            xg = xg_ref[rows, :]
            dyv = dy_ref[rows, :]
            gel, th = _gelu(xg)
            out_ref[1, rows, :] = (dyv * h * _gelu_grad(xg, th)).astype(out_ref.dtype)
            b = jnp.where(row == tt - 1, a_next, pltpu.roll(a, tt - 1, 0))
            pb, z = _scan_bwd(b, dyv * gel, tt)
            dh = z + pb * dh_next
            da = dh * h_prev
            dmult = dh * (ig * xc)
            dig = dh * (mult * xc)
            dla = da * a - dmult * (a * a / mult)
            dzr = dla * (-LRU_C * sp) * (r * (1.0 - r))
            dzx = dig * (ig * (1.0 - ig))
            dxc = dh * (mult * ig) + _dot(dzr, wav, NT) + _dot(dzx, wxv, NT)
            dxc_s[rows, :] = dxc
            dwa_ref[...] += _dot(xc, dzr, TN)
            dwx_ref[...] += _dot(xc, dzx, TN)
            return (dh[0:1, :], a[0:1, :],
                    dba + jnp.sum(dzr, axis=0, keepdims=True),
                    dbx + jnp.sum(dzx, axis=0, keepdims=True),
                    dsp + jnp.sum(dla * (-LRU_C * r), axis=0, keepdims=True),
                    dcb + jnp.sum(dxc, axis=0, keepdims=True))

        _, _, dba, dbx, dsp, dcb = _unrolled_loop(n_t, tile, (zero, zero, zero, zero, zero, zero), LRU_UNROLL)
        dba_ref[...] = dba
        dbx_ref[...] = dbx
        dcb_ref[...] = dcb
        dlam_ref[...] = -dsp * _sigmoid(-lamv)
        dcw = [zero] * CONV_TAPS
        for c0 in range(0, s, ROW_CHUNK):
            dxc_c = dxc_s[c0:c0 + ROW_CHUNK, :]
            dxr = jnp.zeros((ROW_CHUNK, HEAD_DIM), F32)
            for j in range(CONV_TAPS):
                back = CONV_TAPS - 1 - j
                off = 8 - back + c0
                dcw[j] = dcw[j] + jnp.sum(dxc_c * xpad[off:off + ROW_CHUNK, :], axis=0, keepdims=True)
                dxr = dxr + cwv[j:j + 1, :] * dxc_s[back + c0:back + c0 + ROW_CHUNK, :]
            out_ref[0, c0:c0 + ROW_CHUNK, :] = dxr.astype(out_ref.dtype)
        for j in range(CONV_TAPS):
            dcw_ref[j:j + 1, :] = dcw[j]

    return pl.pallas_call(
        body, name="lru_bwd", grid=(heads,),
        in_specs=[col(N_QKV), col(N_QKV + 1), head, head, cws, vec, mat, vec, mat, vec, vec,
                  pl.BlockSpec(memory_space=pl.ANY)],
        out_specs=[pl.BlockSpec((2, s, HEAD_DIM), lambda h: (0, 0, h)),
                   pl.BlockSpec((CONV_TAPS, HEAD_DIM), lambda h: (0, h)), vec, mat, vec, mat, vec, vec],
        out_shape=[jax.ShapeDtypeStruct(dproj_b.shape, dproj_b.dtype),
                   jax.ShapeDtypeStruct((CONV_TAPS, d), F32), jax.ShapeDtypeStruct((1, d), F32),
                   jax.ShapeDtypeStruct(wa.shape, F32), jax.ShapeDtypeStruct((1, d), F32),
                   jax.ShapeDtypeStruct(wx.shape, F32), jax.ShapeDtypeStruct((1, d), F32),
                   jax.ShapeDtypeStruct((1, d), F32)],
        scratch_shapes=[pltpu.VMEM((s + 8, HEAD_DIM), F32), pltpu.VMEM((s, HEAD_DIM), F32),
                        pltpu.VMEM((s + 8, HEAD_DIM), F32)],
        input_output_aliases={11: 0},
        compiler_params=_params(32),
    )(proj_b, proj_b, h_lru, dy, conv_w, conv_b, wa, ba, wx, bx, lam, dproj_b)


def _place():
    x, y, c = (lax.axis_index(n) for n in AXES)
    return x, y, c


def _other_chips(x, y):
    return [(1 - x, y), (x, 1 - y), (1 - x, 1 - y)]


HBM = pl.BlockSpec(memory_space=pl.ANY)


def _cast_shard(w, chip_arr, name):
    r, cols = w.shape
    rh = r // 2
    tr = _row_tile(rh, cols * 4, STREAM_TILE)
    nt = rh // tr

    def body(chip_ref, w_ref, o_ref):
        del chip_ref
        o_ref[...] = w_ref[...].astype(BF16)

    return pl.pallas_call(
        body, name=name,
        grid_spec=pltpu.PrefetchScalarGridSpec(
            num_scalar_prefetch=1, grid=(2, nt),
            in_specs=[pl.BlockSpec((tr, cols), lambda h, i, chip_ref: (h * nt + i, 0))],
            out_specs=pl.BlockSpec((None, None, tr, cols), lambda h, i, chip_ref: (chip_ref[0], h, i, 0))),
        out_shape=jax.ShapeDtypeStruct((N_CHIPS, 2, rh, cols), BF16), compiler_params=_params(32),
    )(chip_arr, w)


HBM_SPEC = pl.BlockSpec(memory_space=pltpu.HBM)
SEM_SPEC = pl.BlockSpec(memory_space=pltpu.SEMAPHORE)
EFFECT = pltpu.SideEffectType.DATAFLOW_SIDE_EFFECTING
TOKEN = jax.ShapeDtypeStruct((8, 128), F32)
TOKEN_SPEC = pl.BlockSpec(memory_space=pltpu.VMEM)


def _in_hbm(arrays):
    return [pltpu.with_memory_space_constraint(a, pltpu.HBM) for a in arrays]


def _hbm_like(arrays):
    return [pltpu.HBM(a.shape, a.dtype) for a in arrays]


def _sems(n):
    return pltpu.SemaphoreType.DMA((n,))


def _remote(src, dst, send_sem, recv_sem, to):
    return pltpu.make_async_remote_copy(src_ref=src, dst_ref=dst, send_sem=send_sem, recv_sem=recv_sem,
                                        device_id=to, device_id_type=MESH)


ALL_FLIPS = (0, 1, 2)


def _gather_start(bufs, groups, after, name):
    n = len(bufs)
    ng = len(groups)

    def body(*refs):
        ins = refs[:n]
        sems = refs[n + len(after):n + len(after) + 2 * ng]
        x, y, c = _place()
        me = 2 * x + y
        chips = _other_chips(x, y)
        for g, (ws, flips) in enumerate(groups):
            for i, w in enumerate(ws):
                for jj, j in enumerate(flips):
                    k = len(flips) * i + jj
                    mine = ins[w].at[me, c]
                    _remote(mine, mine, sems[2 * g].at[k], sems[2 * g + 1].at[k], (*chips[j], c)).start()

    sem_shapes = []
    for ws, flips in groups:
        sem_shapes += [_sems(len(flips) * len(ws))] * 2
    res = pl.pallas_call(
        body, name=name, in_specs=[HBM_SPEC] * n + [HBM] * len(after),
        out_specs=[SEM_SPEC] * (2 * ng) + [HBM_SPEC] * n, out_shape=sem_shapes + _hbm_like(bufs),
        input_output_aliases={w: 2 * ng + w for w in range(n)},
        compiler_params=pltpu.CompilerParams(has_side_effects=EFFECT),
    )(*_in_hbm(bufs), *after)
    return [(res[2 * g], res[2 * g + 1]) for g in range(ng)], list(res[2 * ng:])


def _gather_forward(bufs, recv, after, name, flips=ALL_FLIPS):
    m = len(bufs)
    nf = len(flips)

    def body(*refs):
        ins, recv_in = refs[:m], refs[m]
        fsend, frecv = refs[m + 1 + len(after)], refs[m + 2 + len(after)]
        x, y, c = _place()
        chips = _other_chips(x, y)
        for jj, j in enumerate(flips):
            cx, cy = chips[j]
            for i in range(m):
                landed = ins[i].at[2 * cx + cy, c]
                k = nf * i + jj
                _remote(landed, landed, fsend.at[k], recv_in.at[k], (cx, cy, c)).wait_recv()
                _remote(landed, landed, fsend.at[k], frecv.at[k], (x, y, 1 - c)).start()

    res = pl.pallas_call(
        body, name=name, in_specs=[HBM_SPEC] * m + [SEM_SPEC] + [HBM] * len(after),
        out_specs=[SEM_SPEC, SEM_SPEC] + [HBM_SPEC] * m, out_shape=[_sems(nf * m), _sems(nf * m)] + _hbm_like(bufs),
        input_output_aliases={i: 2 + i for i in range(m)},
        compiler_params=pltpu.CompilerParams(has_side_effects=EFFECT),
    )(*bufs, recv, *after)
    return (res[0], res[1]), list(res[2:])


NEIGHBOURS = (0, 1)


def _relay_partner(x, y, c):
    return 2 * (x ^ (1 - c)) + (y ^ c), (x ^ c, y ^ (1 - c))


def _gather_forward_relay(bufs, recv, after, name):
    m = len(bufs)
    nf = len(NEIGHBOURS)

    def body(*refs):
        ins, recv_in = refs[:m], refs[m]
        fsend, frecv, rsend, rrecv = refs[m + 1 + len(after):m + 5 + len(after)]
        x, y, c = _place()
        chips = _other_chips(x, y)
        for jj, j in enumerate(NEIGHBOURS):
            cx, cy = chips[j]
            for i in range(m):
                landed = ins[i].at[2 * cx + cy, c]
                k = nf * i + jj
                _remote(landed, landed, fsend.at[k], recv_in.at[k], (cx, cy, c)).wait_recv()
        row, (px, py) = _relay_partner(x, y, c)
        for i in range(m):
            relayed = ins[i].at[row, c]
            _remote(relayed, relayed, rsend.at[i], rrecv.at[i], (px, py, c)).start()
        for jj, j in enumerate(NEIGHBOURS):
            cx, cy = chips[j]
            for i in range(m):
                landed = ins[i].at[2 * cx + cy, c]
                k = nf * i + jj
                _remote(landed, landed, fsend.at[k], frecv.at[k], (x, y, 1 - c)).start()

    res = pl.pallas_call(
        body, name=name, in_specs=[HBM_SPEC] * m + [SEM_SPEC] + [HBM] * len(after),
        out_specs=[SEM_SPEC] * 4 + [HBM_SPEC] * m,
        out_shape=[_sems(nf * m), _sems(nf * m), _sems(m), _sems(m)] + _hbm_like(bufs),
        input_output_aliases={i: 4 + i for i in range(m)},
        compiler_params=pltpu.CompilerParams(has_side_effects=EFFECT),
    )(*bufs, recv, *after)
    return tuple(res[:4]), list(res[4:])


def _gather_forward_diag(bufs, rrecv, after, name):
    m = len(bufs)

    def body(*refs):
        ins, rrecv_in = refs[:m], refs[m]
        dsend, drecv = refs[m + 1 + len(after)], refs[m + 2 + len(after)]
        x, y, c = _place()
        diag = (2 * x + y) ^ 3
        _, (px, py) = _relay_partner(x, y, c)
        for i in range(m):
            landed = ins[i].at[diag, c]
            _remote(landed, landed, dsend.at[i], rrecv_in.at[i], (px, py, c)).wait_recv()
            _remote(landed, landed, dsend.at[i], drecv.at[i], (x, y, 1 - c)).start()

    res = pl.pallas_call(
        body, name=name, in_specs=[HBM_SPEC] * m + [SEM_SPEC] + [HBM] * len(after),
        out_specs=[SEM_SPEC, SEM_SPEC] + [HBM_SPEC] * m, out_shape=[_sems(m), _sems(m)] + _hbm_like(bufs),
        input_output_aliases={i: 2 + i for i in range(m)},
        compiler_params=pltpu.CompilerParams(has_side_effects=EFFECT),
    )(*bufs, rrecv, *after)
    return (res[0], res[1]), list(res[2:])


def _gather_finish_diag(bufs, rsend, dsend, drecv, after, name):
    m = len(bufs)

    def body(*refs):
        ins = refs[:m]
        rsend_in, dsend_in, drecv_in = refs[m:m + 3]
        x, y, c = _place()
        diag = (2 * x + y) ^ 3
        row, (px, py) = _relay_partner(x, y, c)
        for i in range(m):
            relayed = ins[i].at[row, c]
            _remote(relayed, relayed, rsend_in.at[i], drecv_in.at[i], (px, py, c)).wait_send()
            landed = ins[i].at[diag, c]
            _remote(landed, landed, dsend_in.at[i], drecv_in.at[i], (x, y, 1 - c)).wait_send()
            theirs = ins[i].at[diag, 1 - c]
            _remote(theirs, theirs, dsend_in.at[i], drecv_in.at[i], (x, y, 1 - c)).wait_recv()

    return list(pl.pallas_call(
        body, name=name, in_specs=[HBM_SPEC] * m + [SEM_SPEC] * 3 + [HBM] * len(after),
        out_specs=[HBM_SPEC] * m, out_shape=_hbm_like(bufs),
        input_output_aliases={i: i for i in range(m)},
        compiler_params=pltpu.CompilerParams(has_side_effects=EFFECT),
    )(*bufs, rsend, dsend, drecv, *after))


def _gather_finish(bufs, send, fsend, frecv, after, name, flips=ALL_FLIPS):
    m = len(bufs)
    nf = len(flips)

    def body(*refs):
        ins = refs[:m]
        send_in, fsend_in, frecv_in = refs[m:m + 3]
        x, y, c = _place()
        me = 2 * x + y
        chips = _other_chips(x, y)
        for jj, j in enumerate(flips):
            cx, cy = chips[j]
            cj = 2 * cx + cy
            for i in range(m):
                k = nf * i + jj
                mine = ins[i].at[me, c]
                _remote(mine, mine, send_in.at[k], frecv_in.at[k], (cx, cy, c)).wait_send()
                landed = ins[i].at[cj, c]
                _remote(landed, landed, fsend_in.at[k], frecv_in.at[k], (x, y, 1 - c)).wait_send()
                theirs = ins[i].at[cj, 1 - c]
                _remote(theirs, theirs, fsend_in.at[k], frecv_in.at[k], (x, y, 1 - c)).wait_recv()

    return list(pl.pallas_call(
        body, name=name, in_specs=[HBM_SPEC] * m + [SEM_SPEC] * 3 + [HBM] * len(after),
        out_specs=[HBM_SPEC] * m, out_shape=_hbm_like(bufs),
        input_output_aliases={i: i for i in range(m)},
        compiler_params=pltpu.CompilerParams(has_side_effects=EFFECT),
    )(*bufs, send, fsend, frecv, *after))


def _pair_exchange(grads, name):
    n = len(grads)

    def body(*refs):
        ins, outs = refs[:n], refs[n:2 * n]
        send_sems, recv_sems = refs[2 * n:]
        x, y, c = _place()
        sibling = (x, y, 1 - c)
        cps = []
        for w in range(n):
            for j in range(N_CHIPS):
                cp = pltpu.make_async_remote_copy(
                    src_ref=ins[w].at[j, 1 - c], dst_ref=outs[w].at[j], send_sem=send_sems.at[N_CHIPS * w + j],
                    recv_sem=recv_sems.at[N_CHIPS * w + j], device_id=sibling, device_id_type=MESH)
                cp.start()
                cps.append(cp)
        for cp in cps:
            cp.wait()

    return pl.pallas_call(
        body, name=name, in_specs=[HBM] * n, out_specs=[HBM] * n,
        out_shape=[jax.ShapeDtypeStruct((N_CHIPS,) + a.shape[2:], a.dtype) for a in grads],
        scratch_shapes=[pltpu.SemaphoreType.DMA((N_CHIPS * n,)), pltpu.SemaphoreType.DMA((N_CHIPS * n,))],
    )(*grads)


def _pair_start(grads, name):
    n = len(grads)
    lands = [lax.empty((N_CHIPS,) + a.shape[2:], a.dtype) for a in grads]

    def body(*refs):
        ins, land_in = refs[:n], refs[n:2 * n]
        send, recv = refs[2 * n], refs[2 * n + 1]
        token = refs[4 * n + 2]
        x, y, c = _place()
        for w in range(n):
            for j in range(N_CHIPS):
                k = N_CHIPS * w + j
                _remote(ins[w].at[j, 1 - c], land_in[w].at[j], send.at[k], recv.at[k], (x, y, 1 - c)).start()
        token[...] = jnp.zeros_like(token)

    res = pl.pallas_call(
        body, name=name, in_specs=[HBM_SPEC] * (2 * n),
        out_specs=[SEM_SPEC, SEM_SPEC] + [HBM_SPEC] * (2 * n) + [TOKEN_SPEC],
        out_shape=[_sems(N_CHIPS * n), _sems(N_CHIPS * n)] + _hbm_like(grads) + _hbm_like(lands) + [TOKEN],
        input_output_aliases={i: 2 + i for i in range(2 * n)},
        compiler_params=pltpu.CompilerParams(has_side_effects=EFFECT),
    )(*_in_hbm(grads), *_in_hbm(lands))
    return (res[0], res[1]), list(res[2:2 + n]), list(res[2 + n:2 + 2 * n]), res[2 + 2 * n]


def _pair_wait(sems, grads, lands, after, name):
    n = len(grads)

    def body(*refs):
        ins, land_in = refs[:n], refs[n:2 * n]
        send_in, recv_in = refs[2 * n], refs[2 * n + 1]
        x, y, c = _place()
        for w in range(n):
            for j in range(N_CHIPS):
                k = N_CHIPS * w + j
                cp = _remote(ins[w].at[j, 1 - c], land_in[w].at[j], send_in.at[k], recv_in.at[k], (x, y, 1 - c))
                cp.wait_send()
                cp.wait_recv()

    res = pl.pallas_call(
        body, name=name, in_specs=[HBM_SPEC] * (2 * n) + [SEM_SPEC, SEM_SPEC] + [HBM] * len(after),
        out_specs=[HBM_SPEC] * (2 * n), out_shape=_hbm_like(grads) + _hbm_like(lands),
        input_output_aliases={i: i for i in range(2 * n)},
        compiler_params=pltpu.CompilerParams(has_side_effects=EFFECT),
    )(*grads, *lands, sems[0], sems[1], *after)
    return list(res[:n]), list(res[n:])


def _chip_start(sums, name, to_all=()):
    m = len(sums)
    lands = [lax.empty(((N_CHIPS,) if i in to_all else ()) + a.shape, a.dtype) for i, a in enumerate(sums)]

    def body(*refs):
        ins, land_in = refs[:m], refs[m:2 * m]
        send, recv = refs[2 * m], refs[2 * m + 1]
        token = refs[4 * m + 2]
        x, y, c = _place()
        me = 2 * x + y
        for i in sorted(range(m), key=lambda i: i not in to_all):
            for j, (cx, cy) in enumerate(_other_chips(x, y)):
                src = ins[i] if i in to_all else ins[i].at[2 * cx + cy]
                _remote(src, land_in[i].at[me], send.at[3 * i + j], recv.at[3 * i + j], (cx, cy, c)).start()
        token[...] = jnp.zeros_like(token)

    res = pl.pallas_call(
        body, name=name, in_specs=[HBM_SPEC] * (2 * m),
        out_specs=[SEM_SPEC, SEM_SPEC] + [HBM_SPEC] * (2 * m) + [TOKEN_SPEC],
        out_shape=[_sems(3 * m), _sems(3 * m)] + _hbm_like(sums) + _hbm_like(lands) + [TOKEN],
        input_output_aliases={i: 2 + i for i in range(2 * m)},
        compiler_params=pltpu.CompilerParams(has_side_effects=EFFECT),
    )(*_in_hbm(sums), *_in_hbm(lands))
    return (res[0], res[1]), list(res[2:2 + m]), list(res[2 + m:2 + 2 * m]), res[2 + 2 * m]


def _chip_wait(sems, sums, lands, after, name, to_all=()):
    m = len(sums)

    def body(*refs):
        ins, land_in = refs[:m], refs[m:2 * m]
        send_in, recv_in = refs[2 * m], refs[2 * m + 1]
        x, y, c = _place()
        for i in range(m):
            for j, (cx, cy) in enumerate(_other_chips(x, y)):
                cj = 2 * cx + cy
                src = ins[i] if i in to_all else ins[i].at[cj]
                cp = _remote(src, land_in[i].at[cj], send_in.at[3 * i + j], recv_in.at[3 * i + j], (cx, cy, c))
                cp.wait_send()
                cp.wait_recv()

    res = pl.pallas_call(
        body, name=name, in_specs=[HBM_SPEC] * (2 * m) + [SEM_SPEC, SEM_SPEC] + [HBM] * len(after),
        out_specs=[HBM_SPEC] * (2 * m), out_shape=_hbm_like(sums) + _hbm_like(lands),
        input_output_aliases={i: i for i in range(2 * m)},
        compiler_params=pltpu.CompilerParams(has_side_effects=EFFECT),
    )(*sums, *lands, sems[0], sems[1], *after)
    return list(res[:m]), list(res[m:])


def _half_parts(bufs):
    parts = []
    for w, a in enumerate(bufs):
        parts += [(w, None)] if a.ndim == 3 else [(w, j) for j in range(a.shape[0])]
    return parts


def _half_ref(refs, w, j, h):
    return refs[w].at[h] if j is None else refs[w].at[j, h]


def _half_start(bufs, name):
    n = len(bufs)
    parts = _half_parts(bufs)

    def body(*refs):
        ins = refs[:n]
        send, recv = refs[n], refs[n + 1]
        x, y, c = _place()
        for k, (w, j) in enumerate(parts):
            mine = _half_ref(ins, w, j, c)
            _remote(mine, mine, send.at[k], recv.at[k], (x, y, 1 - c)).start()

    res = pl.pallas_call(
        body, name=name, in_specs=[HBM_SPEC] * n, out_specs=[SEM_SPEC, SEM_SPEC] + [HBM_SPEC] * n,
        out_shape=[_sems(len(parts)), _sems(len(parts))] + _hbm_like(bufs),
        input_output_aliases={w: 2 + w for w in range(n)},
        compiler_params=pltpu.CompilerParams(has_side_effects=EFFECT),
    )(*_in_hbm(bufs))
    return (res[0], res[1]), list(res[2:])


def _half_wait(sems, bufs, after, name):
    n = len(bufs)
    parts = _half_parts(bufs)

    def body(*refs):
        ins = refs[:n]
        send_in, recv_in = refs[n], refs[n + 1]
        x, y, c = _place()
        for k, (w, j) in enumerate(parts):
            mine = _half_ref(ins, w, j, c)
            _remote(mine, mine, send_in.at[k], recv_in.at[k], (x, y, 1 - c)).wait_send()
            theirs = _half_ref(ins, w, j, 1 - c)
            _remote(theirs, theirs, send_in.at[k], recv_in.at[k], (x, y, 1 - c)).wait_recv()

    return list(pl.pallas_call(
        body, name=name, in_specs=[HBM_SPEC] * n + [SEM_SPEC, SEM_SPEC] + [HBM] * len(after),
        out_specs=[HBM_SPEC] * n, out_shape=_hbm_like(bufs),
        input_output_aliases={w: w for w in range(n)},
        compiler_params=pltpu.CompilerParams(has_side_effects=EFFECT),
    )(*bufs, sems[0], sems[1], *after))


def _all_gather8(block, name, after=()):
    def body(in_ref, *rest):
        out_ref, send_sems, recv_sems, local_sem = rest[len(after):]
        x, y, c = _place()
        me = 4 * x + 2 * y + c
        mine = pltpu.make_async_copy(in_ref, out_ref.at[me], local_sem)
        mine.start()
        flips = [(fx, fy, fc) for fx in (0, 1) for fy in (0, 1) for fc in (0, 1)][1:]
        cps = []
        for k, (fx, fy, fc) in enumerate(flips):
            cp = pltpu.make_async_remote_copy(
                src_ref=in_ref, dst_ref=out_ref.at[me], send_sem=send_sems.at[k], recv_sem=recv_sems.at[k],
                device_id=(x ^ fx, y ^ fy, c ^ fc), device_id_type=MESH)
            cp.start()
            cps.append(cp)
        for k, (fx, fy, fc) in enumerate(flips):
            px, py, pc = x ^ fx, y ^ fy, c ^ fc
            theirs = out_ref.at[4 * px + 2 * py + pc]
            pltpu.make_async_remote_copy(
                src_ref=theirs, dst_ref=theirs, send_sem=send_sems.at[k], recv_sem=recv_sems.at[k],
                device_id=(px, py, pc), device_id_type=MESH).wait_recv()
        for cp in cps:
            cp.wait_send()
        mine.wait()

    return pl.pallas_call(
        body, name=name, in_specs=[HBM] * (1 + len(after)), out_specs=HBM,
        out_shape=jax.ShapeDtypeStruct((N_DEV,) + block.shape, block.dtype),
        scratch_shapes=[pltpu.SemaphoreType.DMA((N_DEV - 1,)), pltpu.SemaphoreType.DMA((N_DEV - 1,)),
                        pltpu.SemaphoreType.DMA],
    )(block, *after)


def _pair_sum(grad, recv, c_arr, name):
    _, _, rh, cols = grad.shape
    tr = _row_tile(rh, cols * grad.dtype.itemsize, STREAM_TILE // 2)

    def body(c_ref, g_ref, r_ref, o_ref):
        del c_ref
        o_ref[...] = (g_ref[...].astype(F32) + r_ref[...].astype(F32)).astype(o_ref.dtype)

    spec = pl.BlockSpec((None, tr, cols), lambda j, i, c_ref: (j, i, 0))
    return pl.pallas_call(
        body, name=name,
        grid_spec=pltpu.PrefetchScalarGridSpec(
            num_scalar_prefetch=1, grid=(N_CHIPS, rh // tr),
            in_specs=[pl.BlockSpec((None, None, tr, cols), lambda j, i, c_ref: (j, c_ref[0], i, 0)), spec],
            out_specs=spec),
        out_shape=jax.ShapeDtypeStruct(recv.shape, grad.dtype), compiler_params=_params(32),
    )(c_arr, grad, recv)


def _sum_by_chip(chip, p_ref, own_ref, o_ref):
    o_ref[...] = jnp.zeros_like(o_ref)
    for k in range(N_CHIPS):
        @pl.when(chip == k)
        def _():
            o_ref[...] += own_ref[...].astype(F32)

        @pl.when(chip != k)
        def _(k=k):
            o_ref[...] += p_ref[k].astype(F32)


def _chip_sum_all(parts, own, place_arr, name):
    _, nj, rh, cols = parts.shape
    tr = _row_tile(rh, cols * 4, STREAM_TILE // 2)

    def body(place_ref, p_ref, own_ref, o_ref):
        _sum_by_chip(place_ref[0], p_ref, own_ref, o_ref)

    return pl.pallas_call(
        body, name=name,
        grid_spec=pltpu.PrefetchScalarGridSpec(
            num_scalar_prefetch=1, grid=(nj, rh // tr),
            in_specs=[pl.BlockSpec((N_CHIPS, None, tr, cols), lambda j, i, place_ref: (0, j, i, 0)),
                      pl.BlockSpec((None, tr, cols), lambda j, i, place_ref: (j, i, 0))],
            out_specs=pl.BlockSpec((None, None, tr, cols), lambda j, i, place_ref: (j, place_ref[1], i, 0))),
        out_shape=jax.ShapeDtypeStruct((nj, 2, rh, cols), F32), compiler_params=_params(32),
    )(place_arr, parts, own)


def _chip_sum(parts, own, place_arr, name):
    _, rh, cols = parts.shape
    tr = _row_tile(rh, cols * 4, STREAM_TILE // 2)

    def body(place_ref, p_ref, own_ref, o_ref):
        _sum_by_chip(place_ref[0], p_ref, own_ref, o_ref)

    return pl.pallas_call(
        body, name=name,
        grid_spec=pltpu.PrefetchScalarGridSpec(
            num_scalar_prefetch=1, grid=(rh // tr,),
            in_specs=[pl.BlockSpec((N_CHIPS, tr, cols), lambda i, place_ref: (0, i, 0)),
                      pl.BlockSpec((None, tr, cols), lambda i, place_ref: (place_ref[0], i, 0))],
            out_specs=pl.BlockSpec((None, tr, cols), lambda i, place_ref: (place_ref[1], i, 0))),
        out_shape=jax.ShapeDtypeStruct((2, rh, cols), F32), compiler_params=_params(32),
    )(place_arr, parts, own)


def _adamw_math(w, g, m, v):
    m = ADAM_B1 * m + (1.0 - ADAM_B1) * g
    v = ADAM_B2 * v + (1.0 - ADAM_B2) * (g * g)
    m_hat = m / (1.0 - ADAM_B1 ** ADAM_STEP)
    v_hat = v / (1.0 - ADAM_B2 ** ADAM_STEP)
    delta = -ADAM_LR * (m_hat / (jnp.sqrt(v_hat) + ADAM_EPS) + ADAM_WD * w)
    return delta, m, v


def _adamw(w, g, m, v, name):
    rows, cols = w.shape
    tr = _row_tile(rows, cols * 4)

    def body(w_ref, g_ref, m_ref, v_ref, go_ref, d_ref, nm_ref, nv_ref):
        gv = g_ref[...]
        go_ref[...] = gv
        d_ref[...], nm_ref[...], nv_ref[...] = _adamw_math(w_ref[...], gv, m_ref[...], v_ref[...])

    spec = pl.BlockSpec((tr, cols), lambda i: (i, 0))
    return pl.pallas_call(
        body, name=name, grid=(rows // tr,), in_specs=[spec] * 4, out_specs=[spec] * 4,
        out_shape=[jax.ShapeDtypeStruct(w.shape, F32)] * 4, compiler_params=_params(32),
    )(w, g, m, v)


def _sum8_adamw_row(parts, w, m, v, name):
    cols = parts.shape[2]

    def body(p_ref, w_ref, m_ref, v_ref, g_ref, d_ref, nm_ref, nv_ref):
        g = p_ref[0, 0:1, :]
        for k in range(1, N_DEV):
            g = g + p_ref[k, 0:1, :]
        g_ref[...] = g
        d_ref[...], nm_ref[...], nv_ref[...] = _adamw_math(w_ref[...], g, m_ref[...], v_ref[...])

    return pl.pallas_call(
        body, name=name, out_shape=[jax.ShapeDtypeStruct((1, cols), F32)] * 4, compiler_params=_params(32),
    )(parts, w, m, v)


def _pack_rows(pieces, rows, name):
    cols = pieces[0].shape[1]
    n = len(pieces)

    def body(*refs):
        o_ref = refs[n]
        o_ref[...] = jnp.zeros_like(o_ref)
        at = 0
        for p_ref in refs[:n]:
            r = p_ref.shape[0]
            o_ref[at:at + r, :] = p_ref[...]
            at += r

    return pl.pallas_call(
        body, name=name, out_shape=jax.ShapeDtypeStruct((rows, cols), F32), compiler_params=_params(32),
    )(*pieces)


def kernel(x, norm_mix_g, w_in, conv_w, conv_b, lru_wa, lru_ba, lru_wx, lru_bx, lru_lambda, w_proj_attn, w_proj_lru, w_out, norm_mlp_g, w_up, w_down, norm_final_g, loss_target, m_norm_mix_g, m_w_in, m_conv_w, m_conv_b, m_lru_wa, m_lru_ba, m_lru_wx, m_lru_bx, m_lru_lambda, m_w_proj_attn, m_w_proj_lru, m_w_out, m_norm_mlp_g, m_w_up, m_w_down, m_norm_final_g, v_norm_mix_g, v_w_in, v_conv_w, v_conv_b, v_lru_wa, v_lru_ba, v_lru_wx, v_lru_bx, v_lru_lambda, v_w_proj_attn, v_w_proj_lru, v_w_out, v_norm_mlp_g, v_w_up, v_w_down, v_norm_final_g):
    s, d = x.shape[1], x.shape[2]
    ff = w_up.shape[2] * N_CHIPS
    heads = d // HEAD_DIM
    u = d // 4
    assert s % (max(DILATIONS) * ATTN_BLK) == 0 and d % (4 * HEAD_DIM) == 0 and ff == 4 * d and DILATIONS[0] == 1
    xs, target = _in_hbm([x[0], loss_target[0]])
    gf = norm_final_g.reshape(1, d)
    wa, wx = lru_wa[0], lru_wx[0]
    core = lax.axis_index("c").astype(jnp.int32)
    chip = (2 * lax.axis_index("x") + lax.axis_index("y")).astype(jnp.int32)
    cidx = core.reshape(1)
    chip_arr = chip.reshape(1)
    place_arr = jnp.stack([chip, core])
    slopes = jnp.broadcast_to(
        (2.0 ** (-8.0 * jnp.arange(1, heads + 1, dtype=F32) / heads))[:, None, None], (heads, 1, HEAD_DIM))

    big = _in_hbm([w_in[0], w_proj_attn[0], w_proj_lru[0], w_out[0], w_up[0], w_down[0]])
    names = ["w_in", "w_proj_attn", "w_proj_lru", "w_out", "w_up", "w_down"]
    cw_pad = jnp.pad(conv_w[0], ((0, 8 - CONV_TAPS), (0, 0)))
    cw_all = _all_gather8(cw_pad, "gather_conv_w")
    conv_w_full = jnp.concatenate([cw_all[2 * j] for j in range(N_CHIPS)], axis=1)
    (sem_a,), buf_a = _gather_start([_cast_shard(big[0], chip_arr, "cast_w_in")], [([0], NEIGHBOURS)], [cw_all],
                                    "gather_start_w_in")

    xn = _rms_fwd(xs, norm_mix_g, "norm_mix")
    flip_bits = (2, 1, 3)

    def w_in_view():
        return buf_a[0].reshape(N_CHIPS, d, N_SLOTS * u)

    proj = _proj_in_shard(xn, w_in_view(), chip_arr, None, "proj_in_own")
    bufs = [_cast_shard(w, chip_arr, "cast_" + nm) for w, nm in zip(big[1:], names[1:])]
    (fs, fr, rs, rr), buf_a = _gather_forward_relay(buf_a, sem_a[1], [proj] + bufs, "gather_forward_w_in")
    buf_a = _gather_finish(buf_a, sem_a[0], fs, fr, [], "gather_finish_w_in", flips=NEIGHBOURS)
    for j in NEIGHBOURS:
        proj = _proj_in_shard(xn, w_in_view(), chip_arr ^ flip_bits[j], proj, "proj_in_from_%d" % j)
    (sem_b, sem_c, sem_d), bufs = _gather_start(
        bufs, [([0, 1, 2], ALL_FLIPS), ([3], ALL_FLIPS), ([4], ALL_FLIPS)], [proj], "gather_start_rest")
    (ds, dr), buf_a = _gather_forward_diag(buf_a, rr, [proj, bufs[0]], "gather_forward_w_in_diag")
    buf_a = _gather_finish_diag(buf_a, rs, ds, dr, [], "gather_finish_w_in_diag")
    proj = _proj_in_shard(xn, w_in_view(), chip_arr ^ flip_bits[2], proj, "proj_in_from_2")
    w_in_g = w_in_view()
    proj_a = proj_b = proj
    y_attn, lse = _attn_fwd(proj_a, slopes)
    y_lru, h_lru = _lru_fwd(proj_b, conv_w_full, conv_b, wa, lru_ba, wx, lru_bx, lru_lambda)
    fsem_b, buf_b = _gather_forward(bufs[:3], sem_b[1], [y_attn, y_lru], "gather_forward_proj")
    fsem_c, buf_c = _gather_forward(bufs[3:4], sem_c[1], [buf_b[0]], "gather_forward_w_up")
    buf_b = _gather_finish(buf_b, sem_b[0], fsem_b[0], fsem_b[1], [buf_c[0]], "gather_finish_proj")
    wpa_g = buf_b[0].reshape(d, d)
    wpl_g = buf_b[1].reshape(d, d)
    wout_g = buf_b[2].reshape(d, d)

    tn = u
    sd_f32 = jax.ShapeDtypeStruct((s, d), F32)
    sd_bf16 = jax.ShapeDtypeStruct((s, d), BF16)
    col = pl.BlockSpec((s, tn), lambda i, j, k: (0, j))

    def slot(n):
        return pl.BlockSpec((None, s, tn), lambda i, j, k: (n, 0, j))

    p_attn = _mm_nn("proj_attn", y_attn, wpa_g, [], [], [sd_f32], [col], _store, tn)[0]

    def merge(acc, extras, outs):
        pa_ref, ga_ref, gl_ref = extras
        merged = _sigmoid(ga_ref[...]) * pa_ref[...] + _sigmoid(gl_ref[...]) * acc
        outs[0][...] = merged.astype(BF16)
        outs[1][...] = acc

    tn2 = max(HEAD_DIM, u // 2)
    col2 = pl.BlockSpec((s, tn2), lambda i, j, k: (0, j))

    def slot2(n):
        return pl.BlockSpec((None, s, tn2), lambda i, j, k: (n, 0, j))

    merged, p_lru = _mm_nn("proj_lru_merge", y_lru, wpl_g, [p_attn, proj, proj], [col2, slot2(5), slot2(6)],
                           [sd_bf16, sd_f32], [col2, col2], merge, tn2)

    def add_resid(acc, extras, outs):
        outs[0][...] = extras[0][...] + acc

    h1 = _mm_nn("w_out_resid", merged, wout_g, [xs], [col], [sd_f32], [col], add_resid, tn)[0]
    hn = _rms_fwd(h1, norm_mlp_g, "norm_mlp")
    buf_c = _gather_finish(buf_c, sem_c[0], fsem_c[0], fsem_c[1], [hn], "gather_finish_w_up")
    wup_g = buf_c[0].reshape(N_CHIPS, d, d)

    def relu_sq(acc, extras, outs):
        r = jnp.maximum(acc, 0.0)
        outs[0][...] = (r * r).astype(BF16)
        outs[1][...] = r.astype(BF16)

    sf_bf16 = jax.ShapeDtypeStruct((s, ff), BF16)
    hid, relu_up = _mm(
        "w_up_relu2", [hn, wup_g],
        [pl.BlockSpec((s, d), lambda i, j, k: (0, 0)),
         pl.BlockSpec((None, d, tn), lambda i, j, k: (j // 4, 0, j % 4))],
        [sf_bf16, sf_bf16], [col, col], (1, ff // tn, 1), NN, relu_sq)
    fsem_d, buf_d = _gather_forward(bufs[4:], sem_d[1], [hid], "gather_forward_w_down")
    wdown_g = _gather_finish(buf_d, sem_d[0], fsem_d[0], fsem_d[1], [], "gather_finish_w_down")[0].reshape(ff, d)
    h2 = _mm(
        "w_down_resid", [hid, wdown_g, h1],
        [pl.BlockSpec((s, d), lambda i, j, k: (0, k)), pl.BlockSpec((d, tn), lambda i, j, k: (k, j)), col],
        [sd_f32], [col], (1, d // tn, ff // d), NN, add_resid, nk=ff // d, acc_shape=(s, tn))[0]
    loss_part, dh2, dh2_b, d_gf = _loss_head(h2, gf, target)
    loss = lax.psum(loss_part[0, 0], AXES)

    def relu_sq_bwd(acc, extras, outs):
        outs[0][...] = (acc * (2.0 * extras[0][...].astype(F32))).astype(BF16)

    dup = _mm_nt("d_hid", dh2_b, wdown_g, [relu_up], [col], [sf_bf16], [col], relu_sq_bwd, tn)[0]
    tok_d = pl.BlockSpec((s, d), lambda i, j: (0, 0))
    g_wdown = _mm_tn(
        "g_w_down", hid, dh2_b, pl.BlockSpec((s, d), lambda i, j: (0, i)),
        pl.BlockSpec((s, tn), lambda i, j: (0, j)), jax.ShapeDtypeStruct((ff, d), BF16),
        pl.BlockSpec((d, tn), lambda i, j: (i, j)), (ff // d, d // tn), d, tn, s)
    dhn = _mm(
        "d_hn", [dup, wup_g],
        [pl.BlockSpec((s, d), lambda i, j, k: (0, k)), pl.BlockSpec((None, tn, d), lambda i, j, k: (k, j, 0))],
        [sd_f32], [col], (1, d // tn, ff // d), NT, _store, nk=ff // d, acc_shape=(s, tn))[0]
    g_wup = _mm_tn(
        "g_w_up", hn, dup, tok_d, pl.BlockSpec((s, tn), lambda i, j: (0, j)),
        jax.ShapeDtypeStruct((N_CHIPS, d, d), BF16), pl.BlockSpec((None, d, tn), lambda i, j: (j // 4, 0, j % 4)),
        (1, ff // tn), d, tn, s)
    big_m = _in_hbm([m_w_in[0], m_w_proj_attn[0], m_w_proj_lru[0], m_w_out[0], m_w_up[0], m_w_down[0]])
    big_v = _in_hbm([v_w_in[0], v_w_proj_attn[0], v_w_proj_lru[0], v_w_out[0], v_w_up[0], v_w_down[0]])
    big_out = {}

    def reduce_begin(ids, gs, tag, everywhere=None):
        g4 = [g.reshape(N_CHIPS, 2, big[i].shape[0] // 2, big[i].shape[1]) for i, g in zip(ids, gs)]
        tags = [names[i] for i in ids]
        if everywhere is not None:
            g4.append(everywhere.reshape(N_CHIPS, 2, everywhere.shape[0] // (2 * N_CHIPS), everywhere.shape[1]))
            tags.append("small_" + tag)
        from_sibling = _pair_exchange(g4, "pair_exchange_" + tag)
        sums = [_pair_sum(g, r, cidx, "pair_sum_" + t) for t, g, r in zip(tags, g4, from_sibling)]
        to_all = () if everywhere is None else (len(ids),)
        return _chip_start(sums, "chip_start_" + tag, to_all), to_all

    def pair_begin(ids, gs, tag):
        g4 = [g.reshape(N_CHIPS, 2, big[i].shape[0] // 2, big[i].shape[1]) for i, g in zip(ids, gs)]
        return _pair_start(g4, "pair_start_" + tag)

    def reduce_begin_paired(ids, paired, after, tag):
        sems, g4, lands, _ = paired
        g4, from_sibling = _pair_wait(sems, g4, lands, after, "pair_wait_" + tag)
        sums = [_pair_sum(g, r, cidx, "pair_sum_" + names[i]) for i, g, r in zip(ids, g4, from_sibling)]
        return _chip_start(sums, "chip_start_" + tag), ()

    def reduce_mid(ids, begun, after, tag):
        (sems, sums, lands, _), to_all = begun
        sums, lands = _chip_wait(sems, sums, lands, after, "chip_wait_" + tag, to_all)
        halves = [_chip_sum(p, own, place_arr, "chip_sum_" + names[i]) for i, p, own in zip(ids, lands, sums)]
        if to_all:
            halves.append(_chip_sum_all(lands[-1], sums[-1], place_arr, "chip_sum_small_" + tag))
        return _half_start(halves, "half_start_" + tag), to_all

    def reduce_end(ids, mid, after, tag):
        (hsems, halves), to_all = mid
        full = _half_wait(hsems, halves, after, "half_wait_" + tag)
        done = []
        for i, g in zip(ids, full):
            res = _adamw(big[i], g.reshape(big[i].shape), big_m[i], big_v[i], "adamw_" + names[i])
            big_out[names[i]] = tuple(a[None] for a in res)
            done.append(res[1])
        everywhere = full[-1].reshape(-1, full[-1].shape[-1]) if to_all else None
        return everywhere, done

    def after_token(a, begun):
        return a + begun[0][3][:1, :1]

    pair_mlp = pair_begin([4, 5], [g_wup, g_wdown], "mlp")
    dh1, dh1_b, d_gmlp = _rms_bwd(h1, norm_mlp_g + pair_mlp[3][:1, :1], dhn, dh2, "norm_mlp_bwd")

    g_wout = _mm_tn(
        "g_w_out", merged, dh1_b, tok_d, pl.BlockSpec((s, tn), lambda i, j: (0, j)),
        jax.ShapeDtypeStruct((d, d), BF16), pl.BlockSpec((d, tn), lambda i, j: (0, j)), (1, d // tn), d, tn, s)

    def merge_bwd(acc, extras, outs):
        pa_ref, pl_ref, ga_ref, gl_ref = extras
        sa, sl = _sigmoid(ga_ref[...]), _sigmoid(gl_ref[...])
        outs[0][...] = (acc * sa).astype(BF16)
        outs[1][...] = (acc * sl).astype(BF16)
        outs[2][0] = (acc * pa_ref[...] * (sa * (1.0 - sa))).astype(BF16)
        outs[2][1] = (acc * pl_ref[...] * (sl * (1.0 - sl))).astype(BF16)

    nb = N_SLOTS - N_QKV
    d_pa, d_pl, dproj_b = _mm_nt(
        "d_merged", dh1_b, wout_g, [p_attn, p_lru, proj, proj], [col2, col2, slot2(5), slot2(6)],
        [sd_bf16, sd_bf16, jax.ShapeDtypeStruct((nb, s, d), BF16)],
        [col2, col2, pl.BlockSpec((2, s, tn2), lambda i, j, k: (1, 0, j))], merge_bwd, tn2)
    red_mlp = reduce_begin_paired([4, 5], pair_mlp, [d_pa], "mlp")
    dy_attn = _mm_nt("d_y_attn", d_pa, wpa_g, [], [], [sd_f32], [col], _store, tn)[0]
    dy_lru = _mm_nt("d_y_lru", d_pl, wpl_g, [], [], [sd_f32], [col], _store, tn)[0]
    g_wpa = _mm_tn(
        "g_w_proj_attn", y_attn, d_pa, tok_d, pl.BlockSpec((s, tn), lambda i, j: (0, j)),
        jax.ShapeDtypeStruct((d, d), BF16), pl.BlockSpec((d, tn), lambda i, j: (0, j)), (1, d // tn), d, tn, s)
    g_wpl = _mm_tn(
        "g_w_proj_lru", y_lru, d_pl, tok_d, pl.BlockSpec((s, tn), lambda i, j: (0, j)),
        jax.ShapeDtypeStruct((d, d), BF16), pl.BlockSpec((d, tn), lambda i, j: (0, j)), (1, d // tn), d, tn, s)

    pair_proj = pair_begin([1, 2, 3], [g_wpa, g_wpl, g_wout], "proj")

    dproj_b, d_cw, d_cb, d_wa, d_ba, d_wx, d_bx, d_lam = _lru_bwd(
        proj_b, h_lru, dy_lru, conv_w_full, conv_b, wa, lru_ba, wx, lru_bx, lru_lambda + pair_proj[3][:1, :1],
        dproj_b)
    red_proj = reduce_begin_paired([1, 2, 3], pair_proj, [dproj_b], "proj")
    dproj_a = _attn_bwd(proj_a, after_token(slopes, red_proj), y_attn, lse, dy_attn)
    per = N_SLOTS
    g_win_shape = jax.ShapeDtypeStruct((N_CHIPS, d, N_SLOTS * u), BF16)

    def g_win_part(name, dproj, first, prev):
        n_units = 4 * dproj.shape[0]
        return _mm_tn(
            name, xn, dproj, tok_d, pl.BlockSpec((None, s, u), lambda i, j: (j // 4, 0, j % 4)),
            g_win_shape, pl.BlockSpec((None, d, u), lambda i, j: ((j + first) // per, 0, (j + first) % per)),
            (1, n_units), d, u, s, aliases=None if prev is None else {2: 0}, extra=prev)

    mat_rows = heads * HEAD_DIM * HEAD_DIM // d
    vec_names = ["norm_mix_g", "conv_b", "lru_ba", "lru_bx", "lru_lambda", "norm_mlp_g", "norm_final_g"]

    def pack(wa_, wx_, cw_, vecs, name):
        rows = [wa_.reshape(mat_rows, d), wx_.reshape(mat_rows, d), cw_] + [a.reshape(1, d) for a in vecs]
        n = sum(a.shape[0] for a in rows)
        return _pack_rows(rows, n + (-n % 64), name)

    zero_cw = jnp.zeros((CONV_TAPS, d), F32)
    small_g = pack(d_wa, d_wx, d_cw, [jnp.zeros((1, d), F32), d_cb, d_ba, d_bx, d_lam, d_gmlp, d_gf], "pack_small_g")
    g_win = g_win_part("g_w_in_qkv", dproj_a, 0, None)
    g_win = g_win_part("g_w_in_rest", dproj_b, 4 * N_QKV, g_win)
    red_in = reduce_begin([0], [g_win], "w_in", everywhere=small_g)
    dxn = _dxn(dproj_a, dproj_b, w_in_g, 2 * tn, [red_in[0][3]])
    grad_x, _, d_gmix = _rms_bwd(xs, norm_mix_g, dxn, dh1, "norm_mix_bwd")

    mid_mlp = reduce_mid([4, 5], red_mlp, [grad_x], "mlp")
    mid_proj = reduce_mid([1, 2, 3], red_proj, [mid_mlp[0][1][0]], "proj")
    _, done_mlp = reduce_end([4, 5], mid_mlp, [mid_proj[0][1][0]], "mlp")
    _, done_proj = reduce_end([1, 2, 3], mid_proj, done_mlp[-1:], "proj")
    done = done_mlp + done_proj
    small_w = pack(wa, wx, zero_cw, [norm_mix_g, conv_b, lru_ba, lru_bx, lru_lambda, norm_mlp_g, norm_final_g],
                   "pack_small_w")
    small_m = pack(m_lru_wa[0], m_lru_wx[0], zero_cw,
                   [m_norm_mix_g, m_conv_b, m_lru_ba, m_lru_bx, m_lru_lambda, m_norm_mlp_g, m_norm_final_g],
                   "pack_small_m")
    small_v = pack(v_lru_wa[0], v_lru_wx[0], zero_cw,
                   [v_norm_mix_g, v_conv_b, v_lru_ba, v_lru_bx, v_lru_lambda, v_norm_mlp_g, v_norm_final_g],
                   "pack_small_v")
    mid_in = reduce_mid([0], red_in, done + [small_w, small_m, small_v], "w_in")
    gmix_parts = _all_gather8(jnp.pad(d_gmix, ((0, 7), (0, 0))), "gather_gain_grad", [mid_in[0][1][0]])
    gmix_out = _sum8_adamw_row(gmix_parts, norm_mix_g, m_norm_mix_g, v_norm_mix_g, "sum_adamw_norm_mix_g")
    small_sum, _ = reduce_end([0], mid_in, [gmix_out[1]], "w_in")
    small = _adamw(small_w, small_sum, small_m, small_v, "adamw_small")
    g_cw = lax.dynamic_slice(small_sum[2 * mat_rows:2 * mat_rows + CONV_TAPS], (0, chip * u), (CONV_TAPS, u))
    cw_out = _adamw(conv_w[0], g_cw, m_conv_w[0], v_conv_w[0], "adamw_conv_w")

    def small_leaf(kind, name):
        a = small[kind]
        if name == "norm_mix_g":
            return gmix_out[kind]
        if name == "lru_wa":
            return a[0:mat_rows].reshape(lru_wa.shape)
        if name == "lru_wx":
            return a[mat_rows:2 * mat_rows].reshape(lru_wx.shape)
        if name == "conv_w":
            return cw_out[kind][None]
        row = a[2 * mat_rows + CONV_TAPS + vec_names.index(name)]
        return row if name == "norm_final_g" else row[None]

    order = ["norm_mix_g", "w_in", "conv_w", "conv_b", "lru_wa", "lru_ba", "lru_wx", "lru_bx", "lru_lambda",
             "w_proj_attn", "w_proj_lru", "w_out", "norm_mlp_g", "w_up", "w_down", "norm_final_g"]
    outs = [loss, grad_x[None]]
    for kind in range(4):
        for name in order:
            outs.append(big_out[name][kind] if name in big_out else small_leaf(kind, name))
    return tuple(outs)
```

```python
import functools

import jax
import jax.numpy as jnp
from jax import lax
from jax.experimental import pallas as pl
from jax.experimental.pallas import tpu as pltpu

F32 = jnp.float32
BF16 = jnp.bfloat16
MESH = pl.DeviceIdType.MESH
AXES = ("x", "y", "c")

N_CHIPS = 4
N_DEV = 8
HEAD_DIM = 128
ATTN_BLK = 128
DILATIONS = (1, 4, 16)
ATTN_UNROLL = 16
CONV_TAPS = 4
LRU_C = 8.0
EPS = 1e-6
N_SLOTS = 7
N_QKV = 3
VMEM_MIB = 2 ** 20
VMEM_V7X = 64 * VMEM_MIB
STREAM_TILE = 4 * VMEM_MIB
VMEM_STREAM_MIB = 32
VMEM_TILES_MIB = 56

ADAM_LR = 0.001
ADAM_B1 = 0.9
ADAM_B2 = 0.999
ADAM_EPS = 1e-08
ADAM_WD = 0.01
ADAM_STEP = 10

NN = (((1,), (0,)), ((), ()))
NT = (((1,), (1,)), ((), ()))
TN = (((0,), (0,)), ((), ()))


def _params(vmem_mib=None, **kw):
    limit = None if vmem_mib is None else min(vmem_mib * VMEM_MIB, VMEM_V7X - 8 * VMEM_MIB)
    return pltpu.CompilerParams(vmem_limit_bytes=limit, **kw)


def _row_tile(rows, row_bytes, budget=VMEM_MIB):
    t = rows
    while t % 16 == 0 and t * row_bytes > budget:
        t //= 2
    return t


def _dot(a, b, dims):
    return lax.dot_general(a.astype(BF16), b.astype(BF16), dims, preferred_element_type=F32)


def _sigmoid(x):
    return jax.nn.sigmoid(x)


def _rms_fwd(x, g, name):
    s, d = x.shape
    tm = _row_tile(s, d * 4)

    def body(x_ref, g_ref, o_ref):
        xf = x_ref[...]
        r = lax.rsqrt(jnp.mean(xf * xf, axis=-1, keepdims=True) + EPS)
        o_ref[...] = (xf * r * g_ref[...]).astype(o_ref.dtype)

    return pl.pallas_call(
        body, name=name, grid=(s // tm,),
        in_specs=[pl.BlockSpec((tm, d), lambda i: (i, 0)), pl.BlockSpec((1, d), lambda i: (0, 0))],
        out_specs=pl.BlockSpec((tm, d), lambda i: (i, 0)),
        out_shape=jax.ShapeDtypeStruct((s, d), BF16), compiler_params=_params(VMEM_STREAM_MIB),
    )(x, g)


def _rms_bwd(x, g, dy, resid, name):
    s, d = x.shape
    tm = _row_tile(s, d * 4)

    def body(x_ref, g_ref, dy_ref, res_ref, dx_ref, dxb_ref, dg_ref):
        xf = x_ref[...]
        r = lax.rsqrt(jnp.mean(xf * xf, axis=-1, keepdims=True) + EPS)
        xh = xf * r
        dyv = dy_ref[...]
        dxh = dyv * g_ref[...]
        dx = r * (dxh - xh * jnp.mean(dxh * xh, axis=-1, keepdims=True)) + res_ref[...]
        dx_ref[...] = dx
        dxb_ref[...] = dx.astype(BF16)
        part = jnp.sum(dyv * xh, axis=0, keepdims=True)

        @pl.when(pl.program_id(0) == 0)
        def _():
            dg_ref[...] = part

        @pl.when(pl.program_id(0) > 0)
        def _():
            dg_ref[...] += part

    row = pl.BlockSpec((tm, d), lambda i: (i, 0))
    vec = pl.BlockSpec((1, d), lambda i: (0, 0))
    return pl.pallas_call(
        body, name=name, grid=(s // tm,),
        in_specs=[row, vec, row, row], out_specs=[row, row, vec],
        out_shape=[jax.ShapeDtypeStruct((s, d), F32), jax.ShapeDtypeStruct((s, d), BF16),
                   jax.ShapeDtypeStruct((1, d), F32)],
        compiler_params=_params(VMEM_STREAM_MIB),
    )(x, g, dy, resid)


def _loss_head(h2, g, target):
    s, d = h2.shape
    tm = _row_tile(s, d * 4)

    def body(x_ref, g_ref, t_ref, loss_ref, dx_ref, dxb_ref, dg_ref):
        xf = x_ref[...]
        gv = g_ref[...]
        r = lax.rsqrt(jnp.mean(xf * xf, axis=-1, keepdims=True) + EPS)
        xh = xf * r
        err = xh * gv - t_ref[...]
        part = jnp.sum(jnp.sum(err * err, axis=1, keepdims=True), axis=0, keepdims=True) * (0.5 / d)
        dyv = err * (1.0 / d)
        dxh = dyv * gv
        dx = r * (dxh - xh * jnp.mean(dxh * xh, axis=-1, keepdims=True))
        dx_ref[...] = dx
        dxb_ref[...] = dx.astype(BF16)
        dgp = jnp.sum(dyv * xh, axis=0, keepdims=True)

        @pl.when(pl.program_id(0) == 0)
        def _():
            dg_ref[...] = dgp
            loss_ref[...] = jnp.broadcast_to(part, loss_ref.shape)

        @pl.when(pl.program_id(0) > 0)
        def _():
            dg_ref[...] += dgp
            loss_ref[...] += jnp.broadcast_to(part, loss_ref.shape)

    row = pl.BlockSpec((tm, d), lambda i: (i, 0))
    vec = pl.BlockSpec((1, d), lambda i: (0, 0))
    return pl.pallas_call(
        body, name="loss_head", grid=(s // tm,),
        in_specs=[row, vec, row],
        out_specs=[pl.BlockSpec((8, 128), lambda i: (0, 0)), row, row, vec],
        out_shape=[jax.ShapeDtypeStruct((8, 128), F32), jax.ShapeDtypeStruct((s, d), F32),
                   jax.ShapeDtypeStruct((s, d), BF16), jax.ShapeDtypeStruct((1, d), F32)],
        compiler_params=_params(VMEM_STREAM_MIB),
    )(h2, g, target)


def _mm(name, operands, in_specs, out_shape, out_specs, grid, dims, epilogue, nk=1, acc_shape=None,
        aliases=None):
    n_in = len(operands)
    n_out = len(out_shape)

    def body(*refs):
        a_ref, b_ref = refs[0], refs[1]
        extras = refs[2:n_in]
        outs = refs[n_in:n_in + n_out]

        def prod():
            return _dot(a_ref[...], b_ref[...], dims)

        if nk == 1:
            epilogue(prod(), extras, outs)
        else:
            acc = refs[n_in + n_out]
            k = pl.program_id(2)

            @pl.when(k == 0)
            def _():
                acc[...] = prod()

            @pl.when(k > 0)
            def _():
                acc[...] += prod()

            @pl.when(k == nk - 1)
            def _():
                epilogue(acc[...], extras, outs)

    scratch = [] if nk == 1 else [pltpu.VMEM(acc_shape, F32)]
    return pl.pallas_call(
        body, name=name, grid=grid, in_specs=in_specs, out_specs=out_specs, out_shape=out_shape,
        scratch_shapes=scratch, input_output_aliases=aliases or {},
        compiler_params=_params(VMEM_TILES_MIB),
    )(*operands)


def _store(acc, extras, outs):
    outs[0][...] = acc.astype(outs[0].dtype)


def _proj_in_shard(xn, w_in_g, shard_arr, prev, name):
    s, d = xn.shape
    u = d // 4
    per = N_SLOTS
    n_prev = 0 if prev is None else 1

    def body(sh_ref, x_ref, w_ref, *rest):
        del sh_ref
        rest[n_prev][...] = _dot(x_ref[...], w_ref[...], NN)

    def out_map(j, sh_ref):
        unit = per * sh_ref[0] + j
        return (unit // 4, 0, unit % 4)

    return pl.pallas_call(
        body, name=name,
        grid_spec=pltpu.PrefetchScalarGridSpec(
            num_scalar_prefetch=1, grid=(per,),
            in_specs=[pl.BlockSpec((s, d), lambda j, sh_ref: (0, 0)),
                      pl.BlockSpec((None, d, u), lambda j, sh_ref: (sh_ref[0], 0, j))] + [HBM] * n_prev,
            out_specs=pl.BlockSpec((None, s, u), out_map)),
        out_shape=jax.ShapeDtypeStruct((N_SLOTS, s, d), F32),
        input_output_aliases={3: 0} if n_prev else {},
        compiler_params=_params(VMEM_TILES_MIB),
    )(shard_arr, xn, w_in_g, *([] if prev is None else [prev]))


def _mm_nn(name, a, b, extras, extra_specs, out_shape, out_specs, epilogue, tn, aliases=None):
    s, kdim = a.shape
    n = b.shape[1]
    return _mm(
        name, [a, b] + list(extras),
        [pl.BlockSpec((s, kdim), lambda i, j, k: (0, 0)), pl.BlockSpec((kdim, tn), lambda i, j, k: (0, j))]
        + list(extra_specs),
        out_shape, out_specs, (1, n // tn, 1), NN, epilogue, aliases=aliases)


def _mm_nt(name, a, b, extras, extra_specs, out_shape, out_specs, epilogue, tn, aliases=None):
    s, kdim = a.shape
    n = b.shape[0]
    return _mm(
        name, [a, b] + list(extras),
        [pl.BlockSpec((s, kdim), lambda i, j, k: (0, 0)), pl.BlockSpec((tn, kdim), lambda i, j, k: (j, 0))]
        + list(extra_specs),
        out_shape, out_specs, (1, n // tn, 1), NT, epilogue, aliases=aliases)


def _mm_tn(name, a, b, a_spec, b_spec, out_shape, out_spec, grid, m, tn, s, aliases=None, extra=None):
    ch = 256
    n_in = 2 if extra is None else 3

    def body(*refs):
        a_ref, b_ref = refs[0], refs[1]
        o_ref, at_ref = refs[n_in], refs[n_in + 1]

        @pl.when(pl.program_id(1) == 0)
        def _():
            for c0 in range(0, s, ch):
                at_ref[:, c0:c0 + ch] = a_ref[c0:c0 + ch, :].astype(F32).T.astype(BF16)

        o_ref[...] = _dot(at_ref[...], b_ref[...], NN).astype(o_ref.dtype)

    operands = [a, b] + ([] if extra is None else [extra])
    in_specs = [a_spec, b_spec] + ([] if extra is None else [pl.BlockSpec(memory_space=pl.ANY)])
    return pl.pallas_call(
        body, name=name, grid=grid, in_specs=in_specs, out_specs=out_spec, out_shape=out_shape,
        scratch_shapes=[pltpu.VMEM((m, s), BF16)], input_output_aliases=aliases or {},
        compiler_params=_params(VMEM_TILES_MIB),
    )(*operands)


def _dxn(dproj_a, dproj_b, w_in_g, tn, after):
    n_a, s, d = dproj_a.shape
    u = d // 4
    ua = 4 * n_a
    nk = 4 * N_SLOTS
    per = N_SLOTS

    def body(a_ref, b_ref, w_ref, *rest):
        o_ref = rest[len(after)]
        k = pl.program_id(2)

        @pl.when(k == 0)
        def _():
            o_ref[...] = jnp.zeros_like(o_ref)

        @pl.when(k < ua)
        def _():
            o_ref[...] += _dot(a_ref[...], w_ref[...], NT)

        @pl.when(k >= ua)
        def _():
            o_ref[...] += _dot(b_ref[...], w_ref[...], NT)

    def a_map(i, j, k):
        kk = jnp.minimum(k, ua - 1)
        return (kk // 4, 0, kk % 4)

    def b_map(i, j, k):
        kk = jnp.maximum(k - ua, 0)
        return (kk // 4, 0, kk % 4)

    return pl.pallas_call(
        body, name="dxn", grid=(1, d // tn, nk),
        in_specs=[pl.BlockSpec((None, s, u), a_map), pl.BlockSpec((None, s, u), b_map),
                  pl.BlockSpec((None, tn, u), lambda i, j, k: (k // per, j, k % per))] + [HBM] * len(after),
        out_specs=pl.BlockSpec((s, tn), lambda i, j, k: (0, j)),
        out_shape=jax.ShapeDtypeStruct((s, d), F32),
        compiler_params=_params(VMEM_TILES_MIB),
    )(dproj_a, dproj_b, w_in_g, *after)


def _attn_masks(slope, dil):
    ii = lax.broadcasted_iota(jnp.int32, (ATTN_BLK, 2 * ATTN_BLK), 0)
    jj = lax.broadcasted_iota(jnp.int32, (ATTN_BLK, 2 * ATTN_BLK), 1)
    diff = ATTN_BLK + ii - jj
    band = (diff >= 0) & (diff <= ATTN_BLK)
    bias = -(slope * float(dil)) * diff.astype(F32)
    return band, bias, jj


def _attn_window(t, nblk):
    cur = pl.ds(pl.multiple_of(t * ATTN_BLK, ATTN_BLK), ATTN_BLK)
    prev = pl.ds(pl.multiple_of(jnp.maximum(t - 1, 0) * ATTN_BLK, ATTN_BLK), ATTN_BLK)
    first = jnp.where(t % nblk == 0, ATTN_BLK, 0)
    return prev, cur, first


def _unrolled_loop(n, step, init, unroll=ATTN_UNROLL):
    def trip(i, carry):
        for k in range(unroll):
            carry = step(i * unroll + k, carry)
        return carry

    return lax.fori_loop(0, n // unroll, trip, init)


def _streams(pairs, dil, s):
    if dil == 1:
        return [src for _, src in pairs]
    seg = s // dil
    for dst, src in pairs:
        for r in range(dil):
            dst[r * seg:(r + 1) * seg, :] = src[pl.ds(r, seg, stride=dil), :].astype(dst.dtype)
    return [dst for dst, _ in pairs]


def _attn_fwd(proj_a, slopes):
    _, s, d = proj_a.shape
    heads = d // HEAD_DIM
    scale = HEAD_DIM ** -0.5
    n_t = s // ATTN_BLK
    ng = len(DILATIONS)

    def body(q_ref, k_ref, v_ref, sl_ref, o_ref, lse_ref, qd, kd, vd, od, ld, og, lg):
        slope = sl_ref[...][:, :1]
        for g, dil in enumerate(DILATIONS):
            nblk = s // dil // ATTN_BLK
            qs, ks, vs = _streams([(qd, q_ref), (kd, k_ref), (vd, v_ref)], dil, s)
            o_t, l_t = (og.at[g], lg.at[g]) if dil == 1 else (od, ld)
            band, bias, jj = _attn_masks(slope, dil)

            def blk(t, carry, nblk=nblk, band=band, bias=bias, jj=jj, qs=qs, ks=ks, vs=vs, o_t=o_t, l_t=l_t):
                prev, cur, first = _attn_window(t, nblk)
                kk = jnp.concatenate([ks[prev, :], ks[cur, :]], axis=0)
                vv = jnp.concatenate([vs[prev, :], vs[cur, :]], axis=0)
                sc = _dot(qs[cur, :], kk, NT) * scale + bias
                sc = jnp.where(band & (jj >= first), sc, -jnp.inf)
                m = jnp.max(sc, axis=1, keepdims=True)
                p = jnp.exp(sc - m)
                l = jnp.sum(p, axis=1, keepdims=True)
                o_t[cur, :] = _dot(p, vv, NN) / l
                l_t[cur, :] = jnp.broadcast_to(m + jnp.log(l), (ATTN_BLK, HEAD_DIM))
                return carry

            _unrolled_loop(n_t, blk, 0)
            seg = s // dil
            if dil > 1:
                for r in range(dil):
                    og[g, pl.ds(r, seg, stride=dil), :] = od[r * seg:(r + 1) * seg, :]
                    lg[g, pl.ds(r, seg, stride=dil), :] = ld[r * seg:(r + 1) * seg, :]

        ch = 256

        def combine(c, carry):
            rows = pl.ds(pl.multiple_of(c * ch, ch), ch)
            ls = [lg[g, rows, :] for g in range(ng)]
            mx = functools.reduce(jnp.maximum, ls)
            es = [jnp.exp(x - mx) for x in ls]
            den = functools.reduce(jnp.add, es)
            num = functools.reduce(jnp.add, [es[g] * og[g, rows, :] for g in range(ng)])
            o_ref[rows, :] = (num / den).astype(o_ref.dtype)
            lse_ref[rows, :] = mx + jnp.log(den)
            return carry

        lax.fori_loop(0, s // ch, combine, 0)

    def col(slot):
        return pl.BlockSpec((None, s, HEAD_DIM), lambda h: (slot, 0, h))

    head = pl.BlockSpec((s, HEAD_DIM), lambda h: (0, h))
    return pl.pallas_call(
        body, name="attn_fwd", grid=(heads,),
        in_specs=[col(0), col(1), col(2), pl.BlockSpec((None, 1, HEAD_DIM), lambda h: (h, 0, 0))],
        out_specs=[head, head],
        out_shape=[jax.ShapeDtypeStruct((s, d), BF16), jax.ShapeDtypeStruct((s, d), F32)],
        scratch_shapes=[pltpu.VMEM((s, HEAD_DIM), BF16)] * 3 + [pltpu.VMEM((s, HEAD_DIM), F32)] * 2
        + [pltpu.VMEM((ng, s, HEAD_DIM), F32)] * 2,
        compiler_params=_params(VMEM_TILES_MIB),
    )(proj_a, proj_a, proj_a, slopes)


def _attn_bwd(proj_a, slopes, y_attn, lse, dy):
    _, s, d = proj_a.shape
    heads = d // HEAD_DIM
    scale = HEAD_DIM ** -0.5
    n_t = s // ATTN_BLK

    def body(q_ref, k_ref, v_ref, sl_ref, o_ref, lse_ref, dy_ref, out_ref,
             qd, kd, vd, dod, lsd, dld, delta, dqd, dkd, dvd, dqa, dka, dva):
        slope = sl_ref[...][:, :1]
        dyv = dy_ref[...]
        delta[...] = jnp.broadcast_to(
            jnp.sum(dyv * o_ref[...].astype(F32), axis=1, keepdims=True), (s, HEAD_DIM))
        for g, dil in enumerate(DILATIONS):
            nblk = s // dil // ATTN_BLK
            seg = s // dil
            qs, ks, vs, dos, lss, dls = _streams(
                [(qd, q_ref), (kd, k_ref), (vd, v_ref), (dod, dy_ref), (lsd, lse_ref), (dld, delta)], dil, s)
            dq_t, dk_t, dv_t = (dqa, dka, dva) if dil == 1 else (dqd, dkd, dvd)
            band, bias, jj = _attn_masks(slope, dil)

            def blk(t, carry, nblk=nblk, band=band, bias=bias, jj=jj, qs=qs, ks=ks, vs=vs, dos=dos, lss=lss,
                    dls=dls, dq_t=dq_t, dk_t=dk_t, dv_t=dv_t):
                ck, cv = carry
                prev, cur, first = _attn_window(t, nblk)
                q = qs[cur, :]
                do = dos[cur, :]
                lse_b = lss[cur, :]
                dl_b = dls[cur, :]
                kk = jnp.concatenate([ks[prev, :], ks[cur, :]], axis=0)
                vv = jnp.concatenate([vs[prev, :], vs[cur, :]], axis=0)
                sc = _dot(q, kk, NT) * scale + bias
                p = jnp.where(band & (jj >= first), jnp.exp(sc - jnp.concatenate([lse_b, lse_b], axis=1)), 0.0)
                dp = _dot(do, vv, NT)
                ds = p * (dp - jnp.concatenate([dl_b, dl_b], axis=1))
                dv_b = _dot(p, do, TN)
                dk_b = _dot(ds, q, TN) * scale
                dq_t[cur, :] = _dot(ds, kk, NN) * scale
                done = pl.ds(pl.multiple_of(jnp.where(t == 0, n_t, t - 1) * ATTN_BLK, ATTN_BLK), ATTN_BLK)
                dk_t[done, :] = ck + dk_b[:ATTN_BLK]
                dv_t[done, :] = cv + dv_b[:ATTN_BLK]
                return dk_b[ATTN_BLK:], dv_b[ATTN_BLK:]

            zero = jnp.zeros((ATTN_BLK, HEAD_DIM), F32)
            ck, cv = _unrolled_loop(n_t, blk, (zero, zero))
            dk_t[(n_t - 1) * ATTN_BLK:n_t * ATTN_BLK, :] = ck
            dv_t[(n_t - 1) * ATTN_BLK:n_t * ATTN_BLK, :] = cv
            if dil > 1:
                for acc, part in ((dqa, dqd), (dka, dkd), (dva, dvd)):
                    for r in range(dil):
                        acc[pl.ds(r, seg, stride=dil), :] += part[r * seg:(r + 1) * seg, :]
        out_ref[0] = dqa[...].astype(out_ref.dtype)
        out_ref[1] = dka[0:s, :].astype(out_ref.dtype)
        out_ref[2] = dva[0:s, :].astype(out_ref.dtype)

    def col(slot):
        return pl.BlockSpec((None, s, HEAD_DIM), lambda h: (slot, 0, h))

    head = pl.BlockSpec((s, HEAD_DIM), lambda h: (0, h))
    return pl.pallas_call(
        body, name="attn_bwd", grid=(heads,),
        in_specs=[col(0), col(1), col(2), pl.BlockSpec((None, 1, HEAD_DIM), lambda h: (h, 0, 0)),
                  head, head, head],
        out_specs=pl.BlockSpec((N_QKV, s, HEAD_DIM), lambda h: (0, 0, h)),
        out_shape=jax.ShapeDtypeStruct((N_QKV, s, d), BF16),
        scratch_shapes=[pltpu.VMEM((s, HEAD_DIM), BF16)] * 4 + [pltpu.VMEM((s, HEAD_DIM), F32)] * 4
        + [pltpu.VMEM((s + ATTN_BLK, HEAD_DIM), F32)] * 2 + [pltpu.VMEM((s, HEAD_DIM), F32)]
        + [pltpu.VMEM((s + ATTN_BLK, HEAD_DIM), F32)] * 2,
        compiler_params=_params(VMEM_TILES_MIB),
    )(proj_a, proj_a, proj_a, slopes, y_attn, lse, dy)


def _expm1(x):
    small = x * (1.0 + x * (0.5 + x * (1.0 / 6.0 + x * (1.0 / 24.0 + x * (1.0 / 120.0)))))
    return jnp.where(jnp.abs(x) < 0.1, small, jnp.exp(x) - 1.0)


def _softplus(x):
    return jnp.maximum(x, 0.0) + jnp.log1p(jnp.exp(-jnp.abs(x)))


GELU_K = 0.7978845608028654
GELU_C = 0.044715


def _gelu(x):
    t = jnp.tanh(GELU_K * (x + GELU_C * x * x * x))
    return 0.5 * x * (1.0 + t), t


def _gelu_grad(x, t):
    return 0.5 * (1.0 + t) + 0.5 * x * (1.0 - t * t) * GELU_K * (1.0 + 3.0 * GELU_C * x * x)


def _lru_gates(xc, wa, ba, wx, bx, sp):
    r = _sigmoid(_dot(xc, wa, NN) + ba)
    ig = _sigmoid(_dot(xc, wx, NN) + bx)
    log_a = -LRU_C * r * sp
    a = jnp.exp(log_a)
    mult = jnp.sqrt(-_expm1(2.0 * log_a))
    return r, ig, a, mult


def _scan_fwd(a, u, tt):
    row = lax.broadcasted_iota(jnp.int32, a.shape, 0)
    sh = 1
    while sh < tt:
        keep = row >= sh
        a_s = jnp.where(keep, pltpu.roll(a, sh, 0), 1.0)
        u_s = jnp.where(keep, pltpu.roll(u, sh, 0), 0.0)
        u = a * u_s + u
        a = a * a_s
        sh *= 2
    return a, u


def _scan_bwd(b, g, tt):
    row = lax.broadcasted_iota(jnp.int32, b.shape, 0)
    sh = 1
    while sh < tt:
        keep = row < tt - sh
        b_s = jnp.where(keep, pltpu.roll(b, tt - sh, 0), 1.0)
        g_s = jnp.where(keep, pltpu.roll(g, tt - sh, 0), 0.0)
        g = g + b * g_s
        b = b * b_s
        sh *= 2
    return b, g


ROW_CHUNK = 256


def _pad_copy(xpad_ref, x_ref, s):
    xpad_ref[0:8, :] = jnp.zeros((8, HEAD_DIM), F32)
    for c0 in range(0, s, ROW_CHUNK):
        xpad_ref[8 + c0:8 + c0 + ROW_CHUNK, :] = x_ref[c0:c0 + ROW_CHUNK, :]


def _conv_rows(dst_ref, xpad_ref, cw, cb, s):
    for c0 in range(0, s, ROW_CHUNK):
        acc = cb
        for j in range(CONV_TAPS):
            off = 8 - (CONV_TAPS - 1) + j + c0
            acc = acc + cw[j:j + 1, :] * xpad_ref[off:off + ROW_CHUNK, :]
        dst_ref[c0:c0 + ROW_CHUNK, :] = acc


LRU_TILE = 128
LRU_UNROLL = 4


def _lru_specs(s, d):
    heads = d // HEAD_DIM

    def col(slot):
        return pl.BlockSpec((None, s, HEAD_DIM), lambda h: (slot, 0, h))

    vec = pl.BlockSpec((1, HEAD_DIM), lambda h: (0, h))
    mat = pl.BlockSpec((None, HEAD_DIM, HEAD_DIM), lambda h: (h, 0, 0))
    cw = pl.BlockSpec((8, HEAD_DIM), lambda h: (0, h))
    head = pl.BlockSpec((s, HEAD_DIM), lambda h: (0, h))
    return heads, col, vec, mat, cw, head


def _lru_fwd(proj_b, conv_w, conv_b, wa, ba, wx, bx, lam):
    _, s, d = proj_b.shape
    heads, col, vec, mat, cws, head = _lru_specs(s, d)
    tt = LRU_TILE

    def body(xr_ref, xg_ref, cw_ref, cb_ref, wa_ref, ba_ref, wx_ref, bx_ref, lam_ref, y_ref, h_ref, xpad, xc_s):
        _pad_copy(xpad, xr_ref, s)
        _conv_rows(xc_s, xpad, cw_ref[...], cb_ref[...], s)
        sp = _softplus(-lam_ref[...])
        wav, wxv, bav, bxv = wa_ref[...], wx_ref[...], ba_ref[...], bx_ref[...]

        def tile(i, hc):
            rows = pl.ds(pl.multiple_of(i * tt, tt), tt)
            xc = xc_s[rows, :]
            _, ig, a, mult = _lru_gates(xc, wav, bav, wxv, bxv, sp)
            pa, hl = _scan_fwd(a, mult * (ig * xc), tt)
            h = hl + pa * hc
            h_ref[rows, :] = h
            gel, _ = _gelu(xg_ref[rows, :])
            y_ref[rows, :] = (h * gel).astype(y_ref.dtype)
            return h[tt - 1:tt, :]

        _unrolled_loop(s // tt, tile, jnp.zeros((1, HEAD_DIM), F32), LRU_UNROLL)

    return pl.pallas_call(
        body, name="lru_fwd", grid=(heads,),
        in_specs=[col(N_QKV), col(N_QKV + 1), cws, vec, mat, vec, mat, vec, vec],
        out_specs=[head, head],
        out_shape=[jax.ShapeDtypeStruct((s, d), BF16), jax.ShapeDtypeStruct((s, d), F32)],
        scratch_shapes=[pltpu.VMEM((s + 8, HEAD_DIM), F32), pltpu.VMEM((s, HEAD_DIM), F32)],
        compiler_params=_params(VMEM_STREAM_MIB),
    )(proj_b, proj_b, conv_w, conv_b, wa, ba, wx, bx, lam)


def _lru_bwd(proj_b, h_lru, dy, conv_w, conv_b, wa, ba, wx, bx, lam, dproj_b):
    _, s, d = proj_b.shape
    heads, col, vec, mat, cws, head = _lru_specs(s, d)
    tt = LRU_TILE
    n_t = s // tt

    def body(xr_ref, xg_ref, h_ref, dy_ref, cw_ref, cb_ref, wa_ref, ba_ref, wx_ref, bx_ref, lam_ref, alias_ref,
             out_ref, dcw_ref, dcb_ref, dwa_ref, dba_ref, dwx_ref, dbx_ref, dlam_ref, xpad, xc_s, dxc_s):
        del alias_ref
        _pad_copy(xpad, xr_ref, s)
        cwv = cw_ref[...]
        _conv_rows(xc_s, xpad, cwv, cb_ref[...], s)
        dxc_s[s:s + 8, :] = jnp.zeros((8, HEAD_DIM), F32)
        lamv = lam_ref[...]
        sp = _softplus(-lamv)
        wav, wxv, bav, bxv = wa_ref[...], wx_ref[...], ba_ref[...], bx_ref[...]
        dwa_ref[...] = jnp.zeros_like(dwa_ref)
        dwx_ref[...] = jnp.zeros_like(dwx_ref)
        zero = jnp.zeros((1, HEAD_DIM), F32)
        row = lax.broadcasted_iota(jnp.int32, (tt, HEAD_DIM), 0)

        def tile(it, carry):
            dh_next, a_next, dba, dbx, dsp, dcb = carry
            i = n_t - 1 - it
            t0 = pl.multiple_of(i * tt, tt)
            rows = pl.ds(t0, tt)
            xc = xc_s[rows, :]
            r, ig, a, mult = _lru_gates(xc, wav, bav, wxv, bxv, sp)
            h = h_ref[rows, :]
            before = h_ref[pl.ds(pl.multiple_of(jnp.maximum(t0 - 8, 0), 8), 8), :][7:8, :]
            before = before * (i > 0).astype(F32)
            h_prev = jnp.where(row == 0, before, pltpu.roll(h, 1, 0))
            xg = xg_ref[rows, :]
            dyv = dy_ref[rows, :]
            gel, th = _gelu(xg)
            out_ref[1, rows, :] = (dyv * h * _gelu_grad(xg, th)).astype(out_ref.dtype)
            b = jnp.where(row == tt - 1, a_next, pltpu.roll(a, tt - 1, 0))
            pb, z = _scan_bwd(b, dyv * gel, tt)
            dh = z + pb * dh_next
            da = dh * h_prev
            dmult = dh * (ig * xc)
            dig = dh * (mult * xc)
            dla = da * a - dmult * (a * a / mult)
            dzr = dla * (-LRU_C * sp) * (r * (1.0 - r))
            dzx = dig * (ig * (1.0 - ig))
            dxc = dh * (mult * ig) + _dot(dzr, wav, NT) + _dot(dzx, wxv, NT)
            dxc_s[rows, :] = dxc
            dwa_ref[...] += _dot(xc, dzr, TN)
            dwx_ref[...] += _dot(xc, dzx, TN)
            return (dh[0:1, :], a[0:1, :],
                    dba + jnp.sum(dzr, axis=0, keepdims=True),
                    dbx + jnp.sum(dzx, axis=0, keepdims=True),
                    dsp + jnp.sum(dla * (-LRU_C * r), axis=0, keepdims=True),
                    dcb + jnp.sum(dxc, axis=0, keepdims=True))

        _, _, dba, dbx, dsp, dcb = _unrolled_loop(n_t, tile, (zero, zero, zero, zero, zero, zero), LRU_UNROLL)
        dba_ref[...] = dba
        dbx_ref[...] = dbx
        dcb_ref[...] = dcb
        dlam_ref[...] = -dsp * _sigmoid(-lamv)
        dcw = [zero] * CONV_TAPS
        for c0 in range(0, s, ROW_CHUNK):
            dxc_c = dxc_s[c0:c0 + ROW_CHUNK, :]
            dxr = jnp.zeros((ROW_CHUNK, HEAD_DIM), F32)
            for j in range(CONV_TAPS):
                back = CONV_TAPS - 1 - j
                off = 8 - back + c0
                dcw[j] = dcw[j] + jnp.sum(dxc_c * xpad[off:off + ROW_CHUNK, :], axis=0, keepdims=True)
                dxr = dxr + cwv[j:j + 1, :] * dxc_s[back + c0:back + c0 + ROW_CHUNK, :]
            out_ref[0, c0:c0 + ROW_CHUNK, :] = dxr.astype(out_ref.dtype)
        for j in range(CONV_TAPS):
            dcw_ref[j:j + 1, :] = dcw[j]

    return pl.pallas_call(
        body, name="lru_bwd", grid=(heads,),
        in_specs=[col(N_QKV), col(N_QKV + 1), head, head, cws, vec, mat, vec, mat, vec, vec,
                  pl.BlockSpec(memory_space=pl.ANY)],
        out_specs=[pl.BlockSpec((2, s, HEAD_DIM), lambda h: (0, 0, h)),
                   pl.BlockSpec((CONV_TAPS, HEAD_DIM), lambda h: (0, h)), vec, mat, vec, mat, vec, vec],
        out_shape=[jax.ShapeDtypeStruct(dproj_b.shape, dproj_b.dtype),
                   jax.ShapeDtypeStruct((CONV_TAPS, d), F32), jax.ShapeDtypeStruct((1, d), F32),
                   jax.ShapeDtypeStruct(wa.shape, F32), jax.ShapeDtypeStruct((1, d), F32),
                   jax.ShapeDtypeStruct(wx.shape, F32), jax.ShapeDtypeStruct((1, d), F32),
                   jax.ShapeDtypeStruct((1, d), F32)],
        scratch_shapes=[pltpu.VMEM((s + 8, HEAD_DIM), F32), pltpu.VMEM((s, HEAD_DIM), F32),
                        pltpu.VMEM((s + 8, HEAD_DIM), F32)],
        input_output_aliases={11: 0},
        compiler_params=_params(VMEM_STREAM_MIB),
    )(proj_b, proj_b, h_lru, dy, conv_w, conv_b, wa, ba, wx, bx, lam, dproj_b)


def _place():
    x, y, c = (lax.axis_index(n) for n in AXES)
    return x, y, c


def _other_chips(x, y):
    return [(1 - x, y), (x, 1 - y), (1 - x, 1 - y)]


HBM = pl.BlockSpec(memory_space=pl.ANY)


def _cast_shard(w, chip_arr, name):
    r, cols = w.shape
    rh = r // 2
    tr = _row_tile(rh, cols * 4, STREAM_TILE)
    nt = rh // tr

    def body(chip_ref, w_ref, o_ref):
        del chip_ref
        o_ref[...] = w_ref[...].astype(BF16)

    return pl.pallas_call(
        body, name=name,
        grid_spec=pltpu.PrefetchScalarGridSpec(
            num_scalar_prefetch=1, grid=(2, nt),
            in_specs=[pl.BlockSpec((tr, cols), lambda h, i, chip_ref: (h * nt + i, 0))],
            out_specs=pl.BlockSpec((None, None, tr, cols), lambda h, i, chip_ref: (chip_ref[0], h, i, 0))),
        out_shape=jax.ShapeDtypeStruct((N_CHIPS, 2, rh, cols), BF16), compiler_params=_params(VMEM_STREAM_MIB),
    )(chip_arr, w)


HBM_SPEC = pl.BlockSpec(memory_space=pltpu.HBM)
SEM_SPEC = pl.BlockSpec(memory_space=pltpu.SEMAPHORE)
EFFECT = pltpu.SideEffectType.DATAFLOW_SIDE_EFFECTING
TOKEN = jax.ShapeDtypeStruct((8, 128), F32)
TOKEN_SPEC = pl.BlockSpec(memory_space=pltpu.VMEM)


def _in_hbm(arrays):
    return [pltpu.with_memory_space_constraint(a, pltpu.HBM) for a in arrays]


def _hbm_like(arrays):
    return [pltpu.HBM(a.shape, a.dtype) for a in arrays]


def _sems(n):
    return pltpu.SemaphoreType.DMA((n,))


def _remote(src, dst, send_sem, recv_sem, to):
    return pltpu.make_async_remote_copy(src_ref=src, dst_ref=dst, send_sem=send_sem, recv_sem=recv_sem,
                                        device_id=to, device_id_type=MESH)


ALL_FLIPS = (0, 1, 2)


def _gather_start(bufs, groups, after, name):
    n = len(bufs)
    ng = len(groups)

    def body(*refs):
        ins = refs[:n]
        sems = refs[n + len(after):n + len(after) + 2 * ng]
        x, y, c = _place()
        me = 2 * x + y
        chips = _other_chips(x, y)
        for g, (ws, flips) in enumerate(groups):
            for i, w in enumerate(ws):
                for jj, j in enumerate(flips):
                    k = len(flips) * i + jj
                    mine = ins[w].at[me, c]
                    _remote(mine, mine, sems[2 * g].at[k], sems[2 * g + 1].at[k], (*chips[j], c)).start()

    sem_shapes = []
    for ws, flips in groups:
        sem_shapes += [_sems(len(flips) * len(ws))] * 2
    res = pl.pallas_call(
        body, name=name, in_specs=[HBM_SPEC] * n + [HBM] * len(after),
        out_specs=[SEM_SPEC] * (2 * ng) + [HBM_SPEC] * n, out_shape=sem_shapes + _hbm_like(bufs),
        input_output_aliases={w: 2 * ng + w for w in range(n)},
        compiler_params=pltpu.CompilerParams(has_side_effects=EFFECT),
    )(*_in_hbm(bufs), *after)
    return [(res[2 * g], res[2 * g + 1]) for g in range(ng)], list(res[2 * ng:])


def _gather_forward(bufs, recv, after, name, flips=ALL_FLIPS):
    m = len(bufs)
    nf = len(flips)

    def body(*refs):
        ins, recv_in = refs[:m], refs[m]
        fsend, frecv = refs[m + 1 + len(after)], refs[m + 2 + len(after)]
        x, y, c = _place()
        chips = _other_chips(x, y)
        for jj, j in enumerate(flips):
            cx, cy = chips[j]
            for i in range(m):
                landed = ins[i].at[2 * cx + cy, c]
                k = nf * i + jj
                _remote(landed, landed, fsend.at[k], recv_in.at[k], (cx, cy, c)).wait_recv()
                _remote(landed, landed, fsend.at[k], frecv.at[k], (x, y, 1 - c)).start()

    res = pl.pallas_call(
        body, name=name, in_specs=[HBM_SPEC] * m + [SEM_SPEC] + [HBM] * len(after),
        out_specs=[SEM_SPEC, SEM_SPEC] + [HBM_SPEC] * m, out_shape=[_sems(nf * m), _sems(nf * m)] + _hbm_like(bufs),
        input_output_aliases={i: 2 + i for i in range(m)},
        compiler_params=pltpu.CompilerParams(has_side_effects=EFFECT),
    )(*bufs, recv, *after)
    return (res[0], res[1]), list(res[2:])


NEIGHBOURS = (0, 1)


def _relay_partner(x, y, c):
    return 2 * (x ^ (1 - c)) + (y ^ c), (x ^ c, y ^ (1 - c))


def _gather_forward_relay(bufs, recv, after, name):
    m = len(bufs)
    nf = len(NEIGHBOURS)

    def body(*refs):
        ins, recv_in = refs[:m], refs[m]
        fsend, frecv, rsend, rrecv = refs[m + 1 + len(after):m + 5 + len(after)]
        x, y, c = _place()
        chips = _other_chips(x, y)
        for jj, j in enumerate(NEIGHBOURS):
            cx, cy = chips[j]
            for i in range(m):
                landed = ins[i].at[2 * cx + cy, c]
                k = nf * i + jj
                _remote(landed, landed, fsend.at[k], recv_in.at[k], (cx, cy, c)).wait_recv()
        row, (px, py) = _relay_partner(x, y, c)
        for i in range(m):
            relayed = ins[i].at[row, c]
            _remote(relayed, relayed, rsend.at[i], rrecv.at[i], (px, py, c)).start()
        for jj, j in enumerate(NEIGHBOURS):
            cx, cy = chips[j]
            for i in range(m):
                landed = ins[i].at[2 * cx + cy, c]
                k = nf * i + jj
                _remote(landed, landed, fsend.at[k], frecv.at[k], (x, y, 1 - c)).start()

    res = pl.pallas_call(
        body, name=name, in_specs=[HBM_SPEC] * m + [SEM_SPEC] + [HBM] * len(after),
        out_specs=[SEM_SPEC] * 4 + [HBM_SPEC] * m,
        out_shape=[_sems(nf * m), _sems(nf * m), _sems(m), _sems(m)] + _hbm_like(bufs),
        input_output_aliases={i: 4 + i for i in range(m)},
        compiler_params=pltpu.CompilerParams(has_side_effects=EFFECT),
    )(*bufs, recv, *after)
    return tuple(res[:4]), list(res[4:])


def _gather_forward_diag(bufs, rrecv, after, name):
    m = len(bufs)

    def body(*refs):
        ins, rrecv_in = refs[:m], refs[m]
        dsend, drecv = refs[m + 1 + len(after)], refs[m + 2 + len(after)]
        x, y, c = _place()
        diag = (2 * x + y) ^ 3
        _, (px, py) = _relay_partner(x, y, c)
        for i in range(m):
            landed = ins[i].at[diag, c]
            _remote(landed, landed, dsend.at[i], rrecv_in.at[i], (px, py, c)).wait_recv()
            _remote(landed, landed, dsend.at[i], drecv.at[i], (x, y, 1 - c)).start()

    res = pl.pallas_call(
        body, name=name, in_specs=[HBM_SPEC] * m + [SEM_SPEC] + [HBM] * len(after),
        out_specs=[SEM_SPEC, SEM_SPEC] + [HBM_SPEC] * m, out_shape=[_sems(m), _sems(m)] + _hbm_like(bufs),
        input_output_aliases={i: 2 + i for i in range(m)},
        compiler_params=pltpu.CompilerParams(has_side_effects=EFFECT),
    )(*bufs, rrecv, *after)
    return (res[0], res[1]), list(res[2:])


def _gather_finish_diag(bufs, rsend, dsend, drecv, after, name):
    m = len(bufs)

    def body(*refs):
        ins = refs[:m]
        rsend_in, dsend_in, drecv_in = refs[m:m + 3]
        x, y, c = _place()
        diag = (2 * x + y) ^ 3
        row, (px, py) = _relay_partner(x, y, c)
        for i in range(m):
            relayed = ins[i].at[row, c]
            _remote(relayed, relayed, rsend_in.at[i], drecv_in.at[i], (px, py, c)).wait_send()
            landed = ins[i].at[diag, c]
            _remote(landed, landed, dsend_in.at[i], drecv_in.at[i], (x, y, 1 - c)).wait_send()
            theirs = ins[i].at[diag, 1 - c]
            _remote(theirs, theirs, dsend_in.at[i], drecv_in.at[i], (x, y, 1 - c)).wait_recv()

    return list(pl.pallas_call(
        body, name=name, in_specs=[HBM_SPEC] * m + [SEM_SPEC] * 3 + [HBM] * len(after),
        out_specs=[HBM_SPEC] * m, out_shape=_hbm_like(bufs),
        input_output_aliases={i: i for i in range(m)},
        compiler_params=pltpu.CompilerParams(has_side_effects=EFFECT),
    )(*bufs, rsend, dsend, drecv, *after))


def _gather_finish(bufs, send, fsend, frecv, after, name, flips=ALL_FLIPS):
    m = len(bufs)
    nf = len(flips)

    def body(*refs):
        ins = refs[:m]
        send_in, fsend_in, frecv_in = refs[m:m + 3]
        x, y, c = _place()
        me = 2 * x + y
        chips = _other_chips(x, y)
        for jj, j in enumerate(flips):
            cx, cy = chips[j]
            cj = 2 * cx + cy
            for i in range(m):
                k = nf * i + jj
                mine = ins[i].at[me, c]
                _remote(mine, mine, send_in.at[k], frecv_in.at[k], (cx, cy, c)).wait_send()
                landed = ins[i].at[cj, c]
                _remote(landed, landed, fsend_in.at[k], frecv_in.at[k], (x, y, 1 - c)).wait_send()
                theirs = ins[i].at[cj, 1 - c]
                _remote(theirs, theirs, fsend_in.at[k], frecv_in.at[k], (x, y, 1 - c)).wait_recv()

    return list(pl.pallas_call(
        body, name=name, in_specs=[HBM_SPEC] * m + [SEM_SPEC] * 3 + [HBM] * len(after),
        out_specs=[HBM_SPEC] * m, out_shape=_hbm_like(bufs),
        input_output_aliases={i: i for i in range(m)},
        compiler_params=pltpu.CompilerParams(has_side_effects=EFFECT),
    )(*bufs, send, fsend, frecv, *after))


def _pair_exchange(grads, name):
    n = len(grads)

    def body(*refs):
        ins, outs = refs[:n], refs[n:2 * n]
        send_sems, recv_sems = refs[2 * n:]
        x, y, c = _place()
        sibling = (x, y, 1 - c)
        cps = []
        for w in range(n):
            for j in range(N_CHIPS):
                cp = pltpu.make_async_remote_copy(
                    src_ref=ins[w].at[j, 1 - c], dst_ref=outs[w].at[j], send_sem=send_sems.at[N_CHIPS * w + j],
                    recv_sem=recv_sems.at[N_CHIPS * w + j], device_id=sibling, device_id_type=MESH)
                cp.start()
                cps.append(cp)
        for cp in cps:
            cp.wait()

    return pl.pallas_call(
        body, name=name, in_specs=[HBM] * n, out_specs=[HBM] * n,
        out_shape=[jax.ShapeDtypeStruct((N_CHIPS,) + a.shape[2:], a.dtype) for a in grads],
        scratch_shapes=[pltpu.SemaphoreType.DMA((N_CHIPS * n,)), pltpu.SemaphoreType.DMA((N_CHIPS * n,))],
    )(*grads)


def _pair_start(grads, name):
    n = len(grads)
    lands = [lax.empty((N_CHIPS,) + a.shape[2:], a.dtype) for a in grads]

    def body(*refs):
        ins, land_in = refs[:n], refs[n:2 * n]
        send, recv = refs[2 * n], refs[2 * n + 1]
        token = refs[4 * n + 2]
        x, y, c = _place()
        for w in range(n):
            for j in range(N_CHIPS):
                k = N_CHIPS * w + j
                _remote(ins[w].at[j, 1 - c], land_in[w].at[j], send.at[k], recv.at[k], (x, y, 1 - c)).start()
        token[...] = jnp.zeros_like(token)

    res = pl.pallas_call(
        body, name=name, in_specs=[HBM_SPEC] * (2 * n),
        out_specs=[SEM_SPEC, SEM_SPEC] + [HBM_SPEC] * (2 * n) + [TOKEN_SPEC],
        out_shape=[_sems(N_CHIPS * n), _sems(N_CHIPS * n)] + _hbm_like(grads) + _hbm_like(lands) + [TOKEN],
        input_output_aliases={i: 2 + i for i in range(2 * n)},
        compiler_params=pltpu.CompilerParams(has_side_effects=EFFECT),
    )(*_in_hbm(grads), *_in_hbm(lands))
    return (res[0], res[1]), list(res[2:2 + n]), list(res[2 + n:2 + 2 * n]), res[2 + 2 * n]


def _pair_wait(sems, grads, lands, after, name):
    n = len(grads)

    def body(*refs):
        ins, land_in = refs[:n], refs[n:2 * n]
        send_in, recv_in = refs[2 * n], refs[2 * n + 1]
        x, y, c = _place()
        for w in range(n):
            for j in range(N_CHIPS):
                k = N_CHIPS * w + j
                cp = _remote(ins[w].at[j, 1 - c], land_in[w].at[j], send_in.at[k], recv_in.at[k], (x, y, 1 - c))
                cp.wait_send()
                cp.wait_recv()

    res = pl.pallas_call(
        body, name=name, in_specs=[HBM_SPEC] * (2 * n) + [SEM_SPEC, SEM_SPEC] + [HBM] * len(after),
        out_specs=[HBM_SPEC] * (2 * n), out_shape=_hbm_like(grads) + _hbm_like(lands),
        input_output_aliases={i: i for i in range(2 * n)},
        compiler_params=pltpu.CompilerParams(has_side_effects=EFFECT),
    )(*grads, *lands, sems[0], sems[1], *after)
    return list(res[:n]), list(res[n:])


def _chip_start(sums, name, to_all=()):
    m = len(sums)
    lands = [lax.empty(((N_CHIPS,) if i in to_all else ()) + a.shape, a.dtype) for i, a in enumerate(sums)]

    def body(*refs):
        ins, land_in = refs[:m], refs[m:2 * m]
        send, recv = refs[2 * m], refs[2 * m + 1]
        token = refs[4 * m + 2]
        x, y, c = _place()
        me = 2 * x + y
        for i in sorted(range(m), key=lambda i: i not in to_all):
            for j, (cx, cy) in enumerate(_other_chips(x, y)):
                src = ins[i] if i in to_all else ins[i].at[2 * cx + cy]
                _remote(src, land_in[i].at[me], send.at[3 * i + j], recv.at[3 * i + j], (cx, cy, c)).start()
        token[...] = jnp.zeros_like(token)

    res = pl.pallas_call(
        body, name=name, in_specs=[HBM_SPEC] * (2 * m),
        out_specs=[SEM_SPEC, SEM_SPEC] + [HBM_SPEC] * (2 * m) + [TOKEN_SPEC],
        out_shape=[_sems(3 * m), _sems(3 * m)] + _hbm_like(sums) + _hbm_like(lands) + [TOKEN],
        input_output_aliases={i: 2 + i for i in range(2 * m)},
        compiler_params=pltpu.CompilerParams(has_side_effects=EFFECT),
    )(*_in_hbm(sums), *_in_hbm(lands))
    return (res[0], res[1]), list(res[2:2 + m]), list(res[2 + m:2 + 2 * m]), res[2 + 2 * m]


def _chip_wait(sems, sums, lands, after, name, to_all=()):
    m = len(sums)

    def body(*refs):
        ins, land_in = refs[:m], refs[m:2 * m]
        send_in, recv_in = refs[2 * m], refs[2 * m + 1]
        x, y, c = _place()
        for i in range(m):
            for j, (cx, cy) in enumerate(_other_chips(x, y)):
                cj = 2 * cx + cy
                src = ins[i] if i in to_all else ins[i].at[cj]
                cp = _remote(src, land_in[i].at[cj], send_in.at[3 * i + j], recv_in.at[3 * i + j], (cx, cy, c))
                cp.wait_send()
                cp.wait_recv()

    res = pl.pallas_call(
        body, name=name, in_specs=[HBM_SPEC] * (2 * m) + [SEM_SPEC, SEM_SPEC] + [HBM] * len(after),
        out_specs=[HBM_SPEC] * (2 * m), out_shape=_hbm_like(sums) + _hbm_like(lands),
        input_output_aliases={i: i for i in range(2 * m)},
        compiler_params=pltpu.CompilerParams(has_side_effects=EFFECT),
    )(*sums, *lands, sems[0], sems[1], *after)
    return list(res[:m]), list(res[m:])


def _half_parts(bufs):
    parts = []
    for w, a in enumerate(bufs):
        parts += [(w, None)] if a.ndim == 3 else [(w, j) for j in range(a.shape[0])]
    return parts


def _half_ref(refs, w, j, h):
    return refs[w].at[h] if j is None else refs[w].at[j, h]


def _half_start(bufs, name):
    n = len(bufs)
    parts = _half_parts(bufs)

    def body(*refs):
        ins = refs[:n]
        send, recv = refs[n], refs[n + 1]
        x, y, c = _place()
        for k, (w, j) in enumerate(parts):
            mine = _half_ref(ins, w, j, c)
            _remote(mine, mine, send.at[k], recv.at[k], (x, y, 1 - c)).start()

    res = pl.pallas_call(
        body, name=name, in_specs=[HBM_SPEC] * n, out_specs=[SEM_SPEC, SEM_SPEC] + [HBM_SPEC] * n,
        out_shape=[_sems(len(parts)), _sems(len(parts))] + _hbm_like(bufs),
        input_output_aliases={w: 2 + w for w in range(n)},
        compiler_params=pltpu.CompilerParams(has_side_effects=EFFECT),
    )(*_in_hbm(bufs))
    return (res[0], res[1]), list(res[2:])


def _half_wait(sems, bufs, after, name):
    n = len(bufs)
    parts = _half_parts(bufs)

    def body(*refs):
        ins = refs[:n]
        send_in, recv_in = refs[n], refs[n + 1]
        x, y, c = _place()
        for k, (w, j) in enumerate(parts):
            mine = _half_ref(ins, w, j, c)
            _remote(mine, mine, send_in.at[k], recv_in.at[k], (x, y, 1 - c)).wait_send()
            theirs = _half_ref(ins, w, j, 1 - c)
            _remote(theirs, theirs, send_in.at[k], recv_in.at[k], (x, y, 1 - c)).wait_recv()

    return list(pl.pallas_call(
        body, name=name, in_specs=[HBM_SPEC] * n + [SEM_SPEC, SEM_SPEC] + [HBM] * len(after),
        out_specs=[HBM_SPEC] * n, out_shape=_hbm_like(bufs),
        input_output_aliases={w: w for w in range(n)},
        compiler_params=pltpu.CompilerParams(has_side_effects=EFFECT),
    )(*bufs, sems[0], sems[1], *after))


def _all_gather8(block, name, after=()):
    def body(in_ref, *rest):
        out_ref, send_sems, recv_sems, local_sem = rest[len(after):]
        x, y, c = _place()
        me = 4 * x + 2 * y + c
        mine = pltpu.make_async_copy(in_ref, out_ref.at[me], local_sem)
        mine.start()
        flips = [(fx, fy, fc) for fx in (0, 1) for fy in (0, 1) for fc in (0, 1)][1:]
        cps = []
        for k, (fx, fy, fc) in enumerate(flips):
            cp = pltpu.make_async_remote_copy(
                src_ref=in_ref, dst_ref=out_ref.at[me], send_sem=send_sems.at[k], recv_sem=recv_sems.at[k],
                device_id=(x ^ fx, y ^ fy, c ^ fc), device_id_type=MESH)
            cp.start()
            cps.append(cp)
        for k, (fx, fy, fc) in enumerate(flips):
            px, py, pc = x ^ fx, y ^ fy, c ^ fc
            theirs = out_ref.at[4 * px + 2 * py + pc]
            pltpu.make_async_remote_copy(
                src_ref=theirs, dst_ref=theirs, send_sem=send_sems.at[k], recv_sem=recv_sems.at[k],
                device_id=(px, py, pc), device_id_type=MESH).wait_recv()
        for cp in cps:
            cp.wait_send()
        mine.wait()

    return pl.pallas_call(
        body, name=name, in_specs=[HBM] * (1 + len(after)), out_specs=HBM,
        out_shape=jax.ShapeDtypeStruct((N_DEV,) + block.shape, block.dtype),
        scratch_shapes=[pltpu.SemaphoreType.DMA((N_DEV - 1,)), pltpu.SemaphoreType.DMA((N_DEV - 1,)),
                        pltpu.SemaphoreType.DMA],
    )(block, *after)


def _pair_sum(grad, recv, c_arr, name):
    _, _, rh, cols = grad.shape
    tr = _row_tile(rh, cols * grad.dtype.itemsize, STREAM_TILE // 2)

    def body(c_ref, g_ref, r_ref, o_ref):
        del c_ref
        o_ref[...] = (g_ref[...].astype(F32) + r_ref[...].astype(F32)).astype(o_ref.dtype)

    spec = pl.BlockSpec((None, tr, cols), lambda j, i, c_ref: (j, i, 0))
    return pl.pallas_call(
        body, name=name,
        grid_spec=pltpu.PrefetchScalarGridSpec(
            num_scalar_prefetch=1, grid=(N_CHIPS, rh // tr),
            in_specs=[pl.BlockSpec((None, None, tr, cols), lambda j, i, c_ref: (j, c_ref[0], i, 0)), spec],
            out_specs=spec),
        out_shape=jax.ShapeDtypeStruct(recv.shape, grad.dtype), compiler_params=_params(VMEM_STREAM_MIB),
    )(c_arr, grad, recv)


def _sum_by_chip(chip, p_ref, own_ref, o_ref):
    o_ref[...] = jnp.zeros_like(o_ref)
    for k in range(N_CHIPS):
        @pl.when(chip == k)
        def _():
            o_ref[...] += own_ref[...].astype(F32)

        @pl.when(chip != k)
        def _(k=k):
            o_ref[...] += p_ref[k].astype(F32)


def _chip_sum_all(parts, own, place_arr, name):
    _, nj, rh, cols = parts.shape
    tr = _row_tile(rh, cols * 4, STREAM_TILE // 2)

    def body(place_ref, p_ref, own_ref, o_ref):
        _sum_by_chip(place_ref[0], p_ref, own_ref, o_ref)

    return pl.pallas_call(
        body, name=name,
        grid_spec=pltpu.PrefetchScalarGridSpec(
            num_scalar_prefetch=1, grid=(nj, rh // tr),
            in_specs=[pl.BlockSpec((N_CHIPS, None, tr, cols), lambda j, i, place_ref: (0, j, i, 0)),
                      pl.BlockSpec((None, tr, cols), lambda j, i, place_ref: (j, i, 0))],
            out_specs=pl.BlockSpec((None, None, tr, cols), lambda j, i, place_ref: (j, place_ref[1], i, 0))),
        out_shape=jax.ShapeDtypeStruct((nj, 2, rh, cols), F32), compiler_params=_params(VMEM_STREAM_MIB),
    )(place_arr, parts, own)


def _chip_sum(parts, own, place_arr, name):
    _, rh, cols = parts.shape
    tr = _row_tile(rh, cols * 4, STREAM_TILE // 2)

    def body(place_ref, p_ref, own_ref, o_ref):
        _sum_by_chip(place_ref[0], p_ref, own_ref, o_ref)

    return pl.pallas_call(
        body, name=name,
        grid_spec=pltpu.PrefetchScalarGridSpec(
            num_scalar_prefetch=1, grid=(rh // tr,),
            in_specs=[pl.BlockSpec((N_CHIPS, tr, cols), lambda i, place_ref: (0, i, 0)),
                      pl.BlockSpec((None, tr, cols), lambda i, place_ref: (place_ref[0], i, 0))],
            out_specs=pl.BlockSpec((None, tr, cols), lambda i, place_ref: (place_ref[1], i, 0))),
        out_shape=jax.ShapeDtypeStruct((2, rh, cols), F32), compiler_params=_params(VMEM_STREAM_MIB),
    )(place_arr, parts, own)


def _adamw_math(w, g, m, v):
    m = ADAM_B1 * m + (1.0 - ADAM_B1) * g
    v = ADAM_B2 * v + (1.0 - ADAM_B2) * (g * g)
    m_hat = m / (1.0 - ADAM_B1 ** ADAM_STEP)
    v_hat = v / (1.0 - ADAM_B2 ** ADAM_STEP)
    delta = -ADAM_LR * (m_hat / (jnp.sqrt(v_hat) + ADAM_EPS) + ADAM_WD * w)
    return delta, m, v


def _adamw(w, g, m, v, name):
    rows, cols = w.shape
    tr = _row_tile(rows, cols * 4)

    def body(w_ref, g_ref, m_ref, v_ref, go_ref, d_ref, nm_ref, nv_ref):
        gv = g_ref[...]
        go_ref[...] = gv
        d_ref[...], nm_ref[...], nv_ref[...] = _adamw_math(w_ref[...], gv, m_ref[...], v_ref[...])

    spec = pl.BlockSpec((tr, cols), lambda i: (i, 0))
    return pl.pallas_call(
        body, name=name, grid=(rows // tr,), in_specs=[spec] * 4, out_specs=[spec] * 4,
        out_shape=[jax.ShapeDtypeStruct(w.shape, F32)] * 4, compiler_params=_params(VMEM_STREAM_MIB),
    )(w, g, m, v)


def _sum8_adamw_row(parts, w, m, v, name):
    cols = parts.shape[2]

    def body(p_ref, w_ref, m_ref, v_ref, g_ref, d_ref, nm_ref, nv_ref):
        g = p_ref[0, 0:1, :]
        for k in range(1, N_DEV):
            g = g + p_ref[k, 0:1, :]
        g_ref[...] = g
        d_ref[...], nm_ref[...], nv_ref[...] = _adamw_math(w_ref[...], g, m_ref[...], v_ref[...])

    return pl.pallas_call(
        body, name=name, out_shape=[jax.ShapeDtypeStruct((1, cols), F32)] * 4, compiler_params=_params(VMEM_STREAM_MIB),
    )(parts, w, m, v)


def _pack_rows(pieces, rows, name):
    cols = pieces[0].shape[1]
    n = len(pieces)

    def body(*refs):
        o_ref = refs[n]
        o_ref[...] = jnp.zeros_like(o_ref)
        at = 0
        for p_ref in refs[:n]:
            r = p_ref.shape[0]
            o_ref[at:at + r, :] = p_ref[...]
            at += r

    return pl.pallas_call(
        body, name=name, out_shape=jax.ShapeDtypeStruct((rows, cols), F32), compiler_params=_params(VMEM_STREAM_MIB),
    )(*pieces)


def kernel(x, norm_mix_g, w_in, conv_w, conv_b, lru_wa, lru_ba, lru_wx, lru_bx, lru_lambda, w_proj_attn, w_proj_lru, w_out, norm_mlp_g, w_up, w_down, norm_final_g, loss_target, m_norm_mix_g, m_w_in, m_conv_w, m_conv_b, m_lru_wa, m_lru_ba, m_lru_wx, m_lru_bx, m_lru_lambda, m_w_proj_attn, m_w_proj_lru, m_w_out, m_norm_mlp_g, m_w_up, m_w_down, m_norm_final_g, v_norm_mix_g, v_w_in, v_conv_w, v_conv_b, v_lru_wa, v_lru_ba, v_lru_wx, v_lru_bx, v_lru_lambda, v_w_proj_attn, v_w_proj_lru, v_w_out, v_norm_mlp_g, v_w_up, v_w_down, v_norm_final_g):
    s, d = x.shape[1], x.shape[2]
    ff = w_up.shape[2] * N_CHIPS
    heads = d // HEAD_DIM
    u = d // 4
    assert s % (max(DILATIONS) * ATTN_BLK) == 0 and d % (4 * HEAD_DIM) == 0 and ff == 4 * d and DILATIONS[0] == 1
    xs, target = _in_hbm([x[0], loss_target[0]])
    gf = norm_final_g.reshape(1, d)
    wa, wx = lru_wa[0], lru_wx[0]
    core = lax.axis_index("c").astype(jnp.int32)
    chip = (2 * lax.axis_index("x") + lax.axis_index("y")).astype(jnp.int32)
    cidx = core.reshape(1)
    chip_arr = chip.reshape(1)
    place_arr = jnp.stack([chip, core])
    slopes = jnp.broadcast_to(
        (2.0 ** (-8.0 * jnp.arange(1, heads + 1, dtype=F32) / heads))[:, None, None], (heads, 1, HEAD_DIM))

    big = _in_hbm([w_in[0], w_proj_attn[0], w_proj_lru[0], w_out[0], w_up[0], w_down[0]])
    names = ["w_in", "w_proj_attn", "w_proj_lru", "w_out", "w_up", "w_down"]
    cw_pad = jnp.pad(conv_w[0], ((0, 8 - CONV_TAPS), (0, 0)))
    cw_all = _all_gather8(cw_pad, "gather_conv_w")
    conv_w_full = jnp.concatenate([cw_all[2 * j] for j in range(N_CHIPS)], axis=1)
    (sem_a,), buf_a = _gather_start([_cast_shard(big[0], chip_arr, "cast_w_in")], [([0], NEIGHBOURS)], [cw_all],
                                    "gather_start_w_in")

    xn = _rms_fwd(xs, norm_mix_g, "norm_mix")
    flip_bits = (2, 1, 3)

    def w_in_view():
        return buf_a[0].reshape(N_CHIPS, d, N_SLOTS * u)

    proj = _proj_in_shard(xn, w_in_view(), chip_arr, None, "proj_in_own")
    bufs = [_cast_shard(w, chip_arr, "cast_" + nm) for w, nm in zip(big[1:], names[1:])]
    (fs, fr, rs, rr), buf_a = _gather_forward_relay(buf_a, sem_a[1], [proj] + bufs, "gather_forward_w_in")
    buf_a = _gather_finish(buf_a, sem_a[0], fs, fr, [], "gather_finish_w_in", flips=NEIGHBOURS)
    for j in NEIGHBOURS:
        proj = _proj_in_shard(xn, w_in_view(), chip_arr ^ flip_bits[j], proj, "proj_in_from_%d" % j)
    (sem_b, sem_c, sem_d), bufs = _gather_start(
        bufs, [([0, 1, 2], ALL_FLIPS), ([3], ALL_FLIPS), ([4], ALL_FLIPS)], [proj], "gather_start_rest")
    (ds, dr), buf_a = _gather_forward_diag(buf_a, rr, [proj, bufs[0]], "gather_forward_w_in_diag")
    buf_a = _gather_finish_diag(buf_a, rs, ds, dr, [], "gather_finish_w_in_diag")
    proj = _proj_in_shard(xn, w_in_view(), chip_arr ^ flip_bits[2], proj, "proj_in_from_2")
    w_in_g = w_in_view()
    proj_a = proj_b = proj
    y_attn, lse = _attn_fwd(proj_a, slopes)
    y_lru, h_lru = _lru_fwd(proj_b, conv_w_full, conv_b, wa, lru_ba, wx, lru_bx, lru_lambda)
    fsem_b, buf_b = _gather_forward(bufs[:3], sem_b[1], [y_attn, y_lru], "gather_forward_proj")
    buf_b = _gather_finish(buf_b, sem_b[0], fsem_b[0], fsem_b[1], [], "gather_finish_proj")
    wpa_g = buf_b[0].reshape(d, d)
    wpl_g = buf_b[1].reshape(d, d)
    wout_g = buf_b[2].reshape(d, d)

    tn = u
    sd_f32 = jax.ShapeDtypeStruct((s, d), F32)
    sd_bf16 = jax.ShapeDtypeStruct((s, d), BF16)
    col = pl.BlockSpec((s, tn), lambda i, j, k: (0, j))

    def slot(n):
        return pl.BlockSpec((None, s, tn), lambda i, j, k: (n, 0, j))

    p_attn = _mm_nn("proj_attn", y_attn, wpa_g, [], [], [sd_f32], [col], _store, tn)[0]
    fsem_c, buf_c = _gather_forward(bufs[3:4], sem_c[1], [p_attn], "gather_forward_w_up")

    def merge(acc, extras, outs):
        pa_ref, ga_ref, gl_ref = extras
        merged = _sigmoid(ga_ref[...]) * pa_ref[...] + _sigmoid(gl_ref[...]) * acc
        outs[0][...] = merged.astype(BF16)
        outs[1][...] = acc

    tn2 = max(HEAD_DIM, u // 2)
    col2 = pl.BlockSpec((s, tn2), lambda i, j, k: (0, j))

    def slot2(n):
        return pl.BlockSpec((None, s, tn2), lambda i, j, k: (n, 0, j))

    merged, p_lru = _mm_nn("proj_lru_merge", y_lru, wpl_g, [p_attn, proj, proj], [col2, slot2(5), slot2(6)],
                           [sd_bf16, sd_f32], [col2, col2], merge, tn2)

    def add_resid(acc, extras, outs):
        outs[0][...] = extras[0][...] + acc

    h1 = _mm_nn("w_out_resid", merged, wout_g, [xs], [col], [sd_f32], [col], add_resid, tn)[0]
    hn = _rms_fwd(h1, norm_mlp_g, "norm_mlp")
    buf_c = _gather_finish(buf_c, sem_c[0], fsem_c[0], fsem_c[1], [hn], "gather_finish_w_up")
    wup_g = buf_c[0].reshape(N_CHIPS, d, d)

    def relu_sq(acc, extras, outs):
        r = jnp.maximum(acc, 0.0)
        outs[0][...] = (r * r).astype(BF16)
        outs[1][...] = r.astype(BF16)

    sf_bf16 = jax.ShapeDtypeStruct((s, ff), BF16)
    hid, relu_up = _mm(
        "w_up_relu2", [hn, wup_g],
        [pl.BlockSpec((s, d), lambda i, j, k: (0, 0)),
         pl.BlockSpec((None, d, tn), lambda i, j, k: (j // 4, 0, j % 4))],
        [sf_bf16, sf_bf16], [col, col], (1, ff // tn, 1), NN, relu_sq)
    fsem_d, buf_d = _gather_forward(bufs[4:], sem_d[1], [hid], "gather_forward_w_down")
    wdown_g = _gather_finish(buf_d, sem_d[0], fsem_d[0], fsem_d[1], [], "gather_finish_w_down")[0].reshape(ff, d)
    h2 = _mm(
        "w_down_resid", [hid, wdown_g, h1],
        [pl.BlockSpec((s, d), lambda i, j, k: (0, k)), pl.BlockSpec((d, tn), lambda i, j, k: (k, j)), col],
        [sd_f32], [col], (1, d // tn, ff // d), NN, add_resid, nk=ff // d, acc_shape=(s, tn))[0]
    loss_part, dh2, dh2_b, d_gf = _loss_head(h2, gf, target)
    loss = lax.psum(loss_part[0, 0], AXES)

    def relu_sq_bwd(acc, extras, outs):
        outs[0][...] = (acc * (2.0 * extras[0][...].astype(F32))).astype(BF16)

    dup = _mm_nt("d_hid", dh2_b, wdown_g, [relu_up], [col], [sf_bf16], [col], relu_sq_bwd, tn)[0]
    tok_d = pl.BlockSpec((s, d), lambda i, j: (0, 0))
    g_wdown = _mm_tn(
        "g_w_down", hid, dh2_b, pl.BlockSpec((s, d), lambda i, j: (0, i)),
        pl.BlockSpec((s, tn), lambda i, j: (0, j)), jax.ShapeDtypeStruct((ff, d), BF16),
        pl.BlockSpec((d, tn), lambda i, j: (i, j)), (ff // d, d // tn), d, tn, s)
    dhn = _mm(
        "d_hn", [dup, wup_g],
        [pl.BlockSpec((s, d), lambda i, j, k: (0, k)), pl.BlockSpec((None, tn, d), lambda i, j, k: (k, j, 0))],
        [sd_f32], [col], (1, d // tn, ff // d), NT, _store, nk=ff // d, acc_shape=(s, tn))[0]
    g_wup = _mm_tn(
        "g_w_up", hn, dup, tok_d, pl.BlockSpec((s, tn), lambda i, j: (0, j)),
        jax.ShapeDtypeStruct((N_CHIPS, d, d), BF16), pl.BlockSpec((None, d, tn), lambda i, j: (j // 4, 0, j % 4)),
        (1, ff // tn), d, tn, s)
    big_m = _in_hbm([m_w_in[0], m_w_proj_attn[0], m_w_proj_lru[0], m_w_out[0], m_w_up[0], m_w_down[0]])
    big_v = _in_hbm([v_w_in[0], v_w_proj_attn[0], v_w_proj_lru[0], v_w_out[0], v_w_up[0], v_w_down[0]])
    big_out = {}

    def reduce_begin(ids, gs, tag, everywhere=None):
        g4 = [g.reshape(N_CHIPS, 2, big[i].shape[0] // 2, big[i].shape[1]) for i, g in zip(ids, gs)]
        tags = [names[i] for i in ids]
        if everywhere is not None:
            g4.append(everywhere.reshape(N_CHIPS, 2, everywhere.shape[0] // (2 * N_CHIPS), everywhere.shape[1]))
            tags.append("small_" + tag)
        from_sibling = _pair_exchange(g4, "pair_exchange_" + tag)
        sums = [_pair_sum(g, r, cidx, "pair_sum_" + t) for t, g, r in zip(tags, g4, from_sibling)]
        to_all = () if everywhere is None else (len(ids),)
        return _chip_start(sums, "chip_start_" + tag, to_all), to_all

    def pair_begin(ids, gs, tag):
        g4 = [g.reshape(N_CHIPS, 2, big[i].shape[0] // 2, big[i].shape[1]) for i, g in zip(ids, gs)]
        return _pair_start(g4, "pair_start_" + tag)

    def reduce_begin_paired(ids, paired, after, tag):
        sems, g4, lands, _ = paired
        g4, from_sibling = _pair_wait(sems, g4, lands, after, "pair_wait_" + tag)
        sums = [_pair_sum(g, r, cidx, "pair_sum_" + names[i]) for i, g, r in zip(ids, g4, from_sibling)]
        return _chip_start(sums, "chip_start_" + tag), ()

    def reduce_mid(ids, begun, after, tag):
        (sems, sums, lands, _), to_all = begun
        sums, lands = _chip_wait(sems, sums, lands, after, "chip_wait_" + tag, to_all)
        halves = [_chip_sum(p, own, place_arr, "chip_sum_" + names[i]) for i, p, own in zip(ids, lands, sums)]
        if to_all:
            halves.append(_chip_sum_all(lands[-1], sums[-1], place_arr, "chip_sum_small_" + tag))
        return _half_start(halves, "half_start_" + tag), to_all

    def reduce_end(ids, mid, after, tag):
        (hsems, halves), to_all = mid
        full = _half_wait(hsems, halves, after, "half_wait_" + tag)
        done = []
        for i, g in zip(ids, full):
            res = _adamw(big[i], g.reshape(big[i].shape), big_m[i], big_v[i], "adamw_" + names[i])
            big_out[names[i]] = tuple(a[None] for a in res)
            done.append(res[1])
        everywhere = full[-1].reshape(-1, full[-1].shape[-1]) if to_all else None
        return everywhere, done

    def after_token(a, begun):
        return a + begun[0][3][:1, :1]

    pair_mlp = pair_begin([4, 5], [g_wup, g_wdown], "mlp")
    dh1, dh1_b, d_gmlp = _rms_bwd(h1, norm_mlp_g + pair_mlp[3][:1, :1], dhn, dh2, "norm_mlp_bwd")

    g_wout = _mm_tn(
        "g_w_out", merged, dh1_b, tok_d, pl.BlockSpec((s, tn), lambda i, j: (0, j)),
        jax.ShapeDtypeStruct((d, d), BF16), pl.BlockSpec((d, tn), lambda i, j: (0, j)), (1, d // tn), d, tn, s)

    def merge_bwd(acc, extras, outs):
        pa_ref, pl_ref, ga_ref, gl_ref = extras
        sa, sl = _sigmoid(ga_ref[...]), _sigmoid(gl_ref[...])
        outs[0][...] = (acc * sa).astype(BF16)
        outs[1][...] = (acc * sl).astype(BF16)
        outs[2][0] = (acc * pa_ref[...] * (sa * (1.0 - sa))).astype(BF16)
        outs[2][1] = (acc * pl_ref[...] * (sl * (1.0 - sl))).astype(BF16)

    nb = N_SLOTS - N_QKV
    d_pa, d_pl, dproj_b = _mm_nt(
        "d_merged", dh1_b, wout_g, [p_attn, p_lru, proj, proj], [col2, col2, slot2(5), slot2(6)],
        [sd_bf16, sd_bf16, jax.ShapeDtypeStruct((nb, s, d), BF16)],
        [col2, col2, pl.BlockSpec((2, s, tn2), lambda i, j, k: (1, 0, j))], merge_bwd, tn2)
    red_mlp = reduce_begin_paired([4, 5], pair_mlp, [d_pa], "mlp")
    dy_attn = _mm_nt("d_y_attn", d_pa, wpa_g, [], [], [sd_f32], [col], _store, tn)[0]
    dy_lru = _mm_nt("d_y_lru", d_pl, wpl_g, [], [], [sd_f32], [col], _store, tn)[0]
    g_wpa = _mm_tn(
        "g_w_proj_attn", y_attn, d_pa, tok_d, pl.BlockSpec((s, tn), lambda i, j: (0, j)),
        jax.ShapeDtypeStruct((d, d), BF16), pl.BlockSpec((d, tn), lambda i, j: (0, j)), (1, d // tn), d, tn, s)
    g_wpl = _mm_tn(
        "g_w_proj_lru", y_lru, d_pl, tok_d, pl.BlockSpec((s, tn), lambda i, j: (0, j)),
        jax.ShapeDtypeStruct((d, d), BF16), pl.BlockSpec((d, tn), lambda i, j: (0, j)), (1, d // tn), d, tn, s)

    pair_proj = pair_begin([1, 2, 3], [g_wpa, g_wpl, g_wout], "proj")

    dproj_b, d_cw, d_cb, d_wa, d_ba, d_wx, d_bx, d_lam = _lru_bwd(
        proj_b, h_lru, dy_lru, conv_w_full, conv_b, wa, lru_ba, wx, lru_bx, lru_lambda + pair_proj[3][:1, :1],
        dproj_b)
    red_proj = reduce_begin_paired([1, 2, 3], pair_proj, [dproj_b], "proj")
    dproj_a = _attn_bwd(proj_a, after_token(slopes, red_proj), y_attn, lse, dy_attn)
    per = N_SLOTS
    g_win_shape = jax.ShapeDtypeStruct((N_CHIPS, d, N_SLOTS * u), BF16)

    def g_win_part(name, dproj, first, prev):
        n_units = 4 * dproj.shape[0]
        return _mm_tn(
            name, xn, dproj, tok_d, pl.BlockSpec((None, s, u), lambda i, j: (j // 4, 0, j % 4)),
            g_win_shape, pl.BlockSpec((None, d, u), lambda i, j: ((j + first) // per, 0, (j + first) % per)),
            (1, n_units), d, u, s, aliases=None if prev is None else {2: 0}, extra=prev)

    mat_rows = heads * HEAD_DIM * HEAD_DIM // d
    vec_names = ["norm_mix_g", "conv_b", "lru_ba", "lru_bx", "lru_lambda", "norm_mlp_g", "norm_final_g"]

    def pack(wa_, wx_, cw_, vecs, name):
        rows = [wa_.reshape(mat_rows, d), wx_.reshape(mat_rows, d), cw_] + [a.reshape(1, d) for a in vecs]
        n = sum(a.shape[0] for a in rows)
        return _pack_rows(rows, n + (-n % 64), name)

    zero_cw = jnp.zeros((CONV_TAPS, d), F32)
    small_g = pack(d_wa, d_wx, d_cw, [jnp.zeros((1, d), F32), d_cb, d_ba, d_bx, d_lam, d_gmlp, d_gf], "pack_small_g")
    g_win = g_win_part("g_w_in_qkv", dproj_a, 0, None)
    g_win = g_win_part("g_w_in_rest", dproj_b, 4 * N_QKV, g_win)
    red_in = reduce_begin([0], [g_win], "w_in", everywhere=small_g)
    dxn = _dxn(dproj_a, dproj_b, w_in_g, 2 * tn, [red_in[0][3]])
    grad_x, _, d_gmix = _rms_bwd(xs, norm_mix_g, dxn, dh1, "norm_mix_bwd")

    mid_mlp = reduce_mid([4, 5], red_mlp, [grad_x], "mlp")
    mid_proj = reduce_mid([1, 2, 3], red_proj, [mid_mlp[0][1][0]], "proj")
    _, done_mlp = reduce_end([4, 5], mid_mlp, [mid_proj[0][1][0]], "mlp")
    _, done_proj = reduce_end([1, 2, 3], mid_proj, done_mlp[-1:], "proj")
    done = done_mlp + done_proj
    small_w = pack(wa, wx, zero_cw, [norm_mix_g, conv_b, lru_ba, lru_bx, lru_lambda, norm_mlp_g, norm_final_g],
                   "pack_small_w")
    small_m = pack(m_lru_wa[0], m_lru_wx[0], zero_cw,
                   [m_norm_mix_g, m_conv_b, m_lru_ba, m_lru_bx, m_lru_lambda, m_norm_mlp_g, m_norm_final_g],
                   "pack_small_m")
    small_v = pack(v_lru_wa[0], v_lru_wx[0], zero_cw,
                   [v_norm_mix_g, v_conv_b, v_lru_ba, v_lru_bx, v_lru_lambda, v_norm_mlp_g, v_norm_final_g],
                   "pack_small_v")
    mid_in = reduce_mid([0], red_in, done + [small_w, small_m, small_v], "w_in")
    gmix_parts = _all_gather8(jnp.pad(d_gmix, ((0, 7), (0, 0))), "gather_gain_grad", [mid_in[0][1][0]])
    gmix_out = _sum8_adamw_row(gmix_parts, norm_mix_g, m_norm_mix_g, v_norm_mix_g, "sum_adamw_norm_mix_g")
    small_sum, _ = reduce_end([0], mid_in, [gmix_out[1]], "w_in")
    small = _adamw(small_w, small_sum, small_m, small_v, "adamw_small")
    g_cw = lax.dynamic_slice(small_sum[2 * mat_rows:2 * mat_rows + CONV_TAPS], (0, chip * u), (CONV_TAPS, u))
    cw_out = _adamw(conv_w[0], g_cw, m_conv_w[0], v_conv_w[0], "adamw_conv_w")

    def small_leaf(kind, name):
        a = small[kind]
        if name == "norm_mix_g":
            return gmix_out[kind]
        if name == "lru_wa":
            return a[0:mat_rows].reshape(lru_wa.shape)
        if name == "lru_wx":
            return a[mat_rows:2 * mat_rows].reshape(lru_wx.shape)
        if name == "conv_w":
            return cw_out[kind][None]
        row = a[2 * mat_rows + CONV_TAPS + vec_names.index(name)]
        return row if name == "norm_final_g" else row[None]

    order = ["norm_mix_g", "w_in", "conv_w", "conv_b", "lru_wa", "lru_ba", "lru_wx", "lru_bx", "lru_lambda",
             "w_proj_attn", "w_proj_lru", "w_out", "norm_mlp_g", "w_up", "w_down", "norm_final_g"]
    outs = [loss, grad_x[None]]
    for kind in range(4):
        for name in order:
            outs.append(big_out[name][kind] if name in big_out else small_leaf(kind, name))
    return tuple(outs)
```

```python
import functools

import jax
import jax.numpy as jnp
from jax import lax
from jax.experimental import pallas as pl
from jax.experimental.pallas import tpu as pltpu

F32 = jnp.float32
BF16 = jnp.bfloat16
MESH = pl.DeviceIdType.MESH
AXES = ("x", "y", "c")

N_CHIPS = 4
N_DEV = 8
HEAD_DIM = 128
ATTN_BLK = 128
DILATIONS = (1, 4, 16)
ATTN_UNROLL = 16
CONV_TAPS = 4
LRU_C = 8.0
EPS = 1e-6
N_SLOTS = 7
N_QKV = 3
VMEM_MIB = 2 ** 20
VMEM_V7X = 64 * VMEM_MIB
STREAM_TILE = 4 * VMEM_MIB
VMEM_STREAM_MIB = 32
VMEM_TILES_MIB = 56

ADAM_LR = 0.001
ADAM_B1 = 0.9
ADAM_B2 = 0.999
ADAM_EPS = 1e-08
ADAM_WD = 0.01
ADAM_STEP = 10

NN = (((1,), (0,)), ((), ()))
NT = (((1,), (1,)), ((), ()))
TN = (((0,), (0,)), ((), ()))


def _params(vmem_mib=None, **kw):
    limit = None if vmem_mib is None else min(vmem_mib * VMEM_MIB, VMEM_V7X - 8 * VMEM_MIB)
    return pltpu.CompilerParams(vmem_limit_bytes=limit, **kw)


def _row_tile(rows, row_bytes, budget=VMEM_MIB):
    t = rows
    while t % 16 == 0 and t * row_bytes > budget:
        t //= 2
    return t


def _dot(a, b, dims):
    return lax.dot_general(a.astype(BF16), b.astype(BF16), dims, preferred_element_type=F32)


def _sigmoid(x):
    return jax.nn.sigmoid(x)


def _rms_fwd(x, g, name):
    s, d = x.shape
    tm = _row_tile(s, d * 4)

    def body(x_ref, g_ref, o_ref):
        xf = x_ref[...]
        r = lax.rsqrt(jnp.mean(xf * xf, axis=-1, keepdims=True) + EPS)
        o_ref[...] = (xf * r * g_ref[...]).astype(o_ref.dtype)

    return pl.pallas_call(
        body, name=name, grid=(s // tm,),
        in_specs=[pl.BlockSpec((tm, d), lambda i: (i, 0)), pl.BlockSpec((1, d), lambda i: (0, 0))],
        out_specs=pl.BlockSpec((tm, d), lambda i: (i, 0)),
        out_shape=jax.ShapeDtypeStruct((s, d), BF16), compiler_params=_params(VMEM_STREAM_MIB),
    )(x, g)


def _rms_bwd(x, g, dy, resid, name):
    s, d = x.shape
    tm = _row_tile(s, d * 4)

    def body(x_ref, g_ref, dy_ref, res_ref, dx_ref, dxb_ref, dg_ref):
        xf = x_ref[...]
        r = lax.rsqrt(jnp.mean(xf * xf, axis=-1, keepdims=True) + EPS)
        xh = xf * r
        dyv = dy_ref[...]
        dxh = dyv * g_ref[...]
        dx = r * (dxh - xh * jnp.mean(dxh * xh, axis=-1, keepdims=True)) + res_ref[...]
        dx_ref[...] = dx
        dxb_ref[...] = dx.astype(BF16)
        part = jnp.sum(dyv * xh, axis=0, keepdims=True)

        @pl.when(pl.program_id(0) == 0)
        def _():
            dg_ref[...] = part

        @pl.when(pl.program_id(0) > 0)
        def _():
            dg_ref[...] += part

    row = pl.BlockSpec((tm, d), lambda i: (i, 0))
    vec = pl.BlockSpec((1, d), lambda i: (0, 0))
    return pl.pallas_call(
        body, name=name, grid=(s // tm,),
        in_specs=[row, vec, row, row], out_specs=[row, row, vec],
        out_shape=[jax.ShapeDtypeStruct((s, d), F32), jax.ShapeDtypeStruct((s, d), BF16),
                   jax.ShapeDtypeStruct((1, d), F32)],
        compiler_params=_params(VMEM_STREAM_MIB),
    )(x, g, dy, resid)


def _loss_head(h2, g, target):
    s, d = h2.shape
    tm = _row_tile(s, d * 4)

    def body(x_ref, g_ref, t_ref, loss_ref, dx_ref, dxb_ref, dg_ref):
        xf = x_ref[...]
        gv = g_ref[...]
        r = lax.rsqrt(jnp.mean(xf * xf, axis=-1, keepdims=True) + EPS)
        xh = xf * r
        err = xh * gv - t_ref[...]
        part = jnp.sum(jnp.sum(err * err, axis=1, keepdims=True), axis=0, keepdims=True) * (0.5 / d)
        dyv = err * (1.0 / d)
        dxh = dyv * gv
        dx = r * (dxh - xh * jnp.mean(dxh * xh, axis=-1, keepdims=True))
        dx_ref[...] = dx
        dxb_ref[...] = dx.astype(BF16)
        dgp = jnp.sum(dyv * xh, axis=0, keepdims=True)

        @pl.when(pl.program_id(0) == 0)
        def _():
            dg_ref[...] = dgp
            loss_ref[...] = jnp.broadcast_to(part, loss_ref.shape)

        @pl.when(pl.program_id(0) > 0)
        def _():
            dg_ref[...] += dgp
            loss_ref[...] += jnp.broadcast_to(part, loss_ref.shape)

    row = pl.BlockSpec((tm, d), lambda i: (i, 0))
    vec = pl.BlockSpec((1, d), lambda i: (0, 0))
    return pl.pallas_call(
        body, name="loss_head", grid=(s // tm,),
        in_specs=[row, vec, row],
        out_specs=[pl.BlockSpec((8, 128), lambda i: (0, 0)), row, row, vec],
        out_shape=[jax.ShapeDtypeStruct((8, 128), F32), jax.ShapeDtypeStruct((s, d), F32),
                   jax.ShapeDtypeStruct((s, d), BF16), jax.ShapeDtypeStruct((1, d), F32)],
        compiler_params=_params(VMEM_STREAM_MIB),
    )(h2, g, target)


def _mm(name, operands, in_specs, out_shape, out_specs, grid, dims, epilogue, nk=1, acc_shape=None,
        aliases=None):
    n_in = len(operands)
    n_out = len(out_shape)

    def body(*refs):
        a_ref, b_ref = refs[0], refs[1]
        extras = refs[2:n_in]
        outs = refs[n_in:n_in + n_out]

        def prod():
            return _dot(a_ref[...], b_ref[...], dims)

        if nk == 1:
            epilogue(prod(), extras, outs)
        else:
            acc = refs[n_in + n_out]
            k = pl.program_id(2)

            @pl.when(k == 0)
            def _():
                acc[...] = prod()

            @pl.when(k > 0)
            def _():
                acc[...] += prod()

            @pl.when(k == nk - 1)
            def _():
                epilogue(acc[...], extras, outs)

    scratch = [] if nk == 1 else [pltpu.VMEM(acc_shape, F32)]
    return pl.pallas_call(
        body, name=name, grid=grid, in_specs=in_specs, out_specs=out_specs, out_shape=out_shape,
        scratch_shapes=scratch, input_output_aliases=aliases or {},
        compiler_params=_params(VMEM_TILES_MIB),
    )(*operands)


def _store(acc, extras, outs):
    outs[0][...] = acc.astype(outs[0].dtype)


def _proj_in_shard(xn, w_in_g, shard_arr, prev, name):
    s, d = xn.shape
    u = d // 4
    per = N_SLOTS
    n_prev = 0 if prev is None else 1

    def body(sh_ref, x_ref, w_ref, *rest):
        del sh_ref
        rest[n_prev][...] = _dot(x_ref[...], w_ref[...], NN)

    def out_map(j, sh_ref):
        unit = per * sh_ref[0] + j
        return (unit // 4, 0, unit % 4)

    return pl.pallas_call(
        body, name=name,
        grid_spec=pltpu.PrefetchScalarGridSpec(
            num_scalar_prefetch=1, grid=(per,),
            in_specs=[pl.BlockSpec((s, d), lambda j, sh_ref: (0, 0)),
                      pl.BlockSpec((None, d, u), lambda j, sh_ref: (sh_ref[0], 0, j))] + [HBM] * n_prev,
            out_specs=pl.BlockSpec((None, s, u), out_map)),
        out_shape=jax.ShapeDtypeStruct((N_SLOTS, s, d), F32),
        input_output_aliases={3: 0} if n_prev else {},
        compiler_params=_params(VMEM_TILES_MIB),
    )(shard_arr, xn, w_in_g, *([] if prev is None else [prev]))


def _mm_nn(name, a, b, extras, extra_specs, out_shape, out_specs, epilogue, tn, aliases=None):
    s, kdim = a.shape
    n = b.shape[1]
    return _mm(
        name, [a, b] + list(extras),
        [pl.BlockSpec((s, kdim), lambda i, j, k: (0, 0)), pl.BlockSpec((kdim, tn), lambda i, j, k: (0, j))]
        + list(extra_specs),
        out_shape, out_specs, (1, n // tn, 1), NN, epilogue, aliases=aliases)


def _mm_nt(name, a, b, extras, extra_specs, out_shape, out_specs, epilogue, tn, aliases=None):
    s, kdim = a.shape
    n = b.shape[0]
    return _mm(
        name, [a, b] + list(extras),
        [pl.BlockSpec((s, kdim), lambda i, j, k: (0, 0)), pl.BlockSpec((tn, kdim), lambda i, j, k: (j, 0))]
        + list(extra_specs),
        out_shape, out_specs, (1, n // tn, 1), NT, epilogue, aliases=aliases)


def _mm_tn(name, a, b, a_spec, b_spec, out_shape, out_spec, grid, m, tn, s, aliases=None, extra=None):
    ch = 256
    n_in = 2 if extra is None else 3

    def body(*refs):
        a_ref, b_ref = refs[0], refs[1]
        o_ref, at_ref = refs[n_in], refs[n_in + 1]

        @pl.when(pl.program_id(1) == 0)
        def _():
            for c0 in range(0, s, ch):
                at_ref[:, c0:c0 + ch] = a_ref[c0:c0 + ch, :].astype(F32).T.astype(BF16)

        o_ref[...] = _dot(at_ref[...], b_ref[...], NN).astype(o_ref.dtype)

    operands = [a, b] + ([] if extra is None else [extra])
    in_specs = [a_spec, b_spec] + ([] if extra is None else [pl.BlockSpec(memory_space=pl.ANY)])
    return pl.pallas_call(
        body, name=name, grid=grid, in_specs=in_specs, out_specs=out_spec, out_shape=out_shape,
        scratch_shapes=[pltpu.VMEM((m, s), BF16)], input_output_aliases=aliases or {},
        compiler_params=_params(VMEM_TILES_MIB),
    )(*operands)


def _dxn(dproj_a, dproj_b, w_in_g, tn, after):
    n_a, s, d = dproj_a.shape
    u = d // 4
    ua = 4 * n_a
    nk = 4 * N_SLOTS
    per = N_SLOTS

    def body(a_ref, b_ref, w_ref, *rest):
        o_ref = rest[len(after)]
        k = pl.program_id(2)

        @pl.when(k == 0)
        def _():
            o_ref[...] = jnp.zeros_like(o_ref)

        @pl.when(k < ua)
        def _():
            o_ref[...] += _dot(a_ref[...], w_ref[...], NT)

        @pl.when(k >= ua)
        def _():
            o_ref[...] += _dot(b_ref[...], w_ref[...], NT)

    def a_map(i, j, k):
        kk = jnp.minimum(k, ua - 1)
        return (kk // 4, 0, kk % 4)

    def b_map(i, j, k):
        kk = jnp.maximum(k - ua, 0)
        return (kk // 4, 0, kk % 4)

    return pl.pallas_call(
        body, name="dxn", grid=(1, d // tn, nk),
        in_specs=[pl.BlockSpec((None, s, u), a_map), pl.BlockSpec((None, s, u), b_map),
                  pl.BlockSpec((None, tn, u), lambda i, j, k: (k // per, j, k % per))] + [HBM] * len(after),
        out_specs=pl.BlockSpec((s, tn), lambda i, j, k: (0, j)),
        out_shape=jax.ShapeDtypeStruct((s, d), F32),
        compiler_params=_params(VMEM_TILES_MIB),
    )(dproj_a, dproj_b, w_in_g, *after)


def _attn_masks(slope, dil):
    ii = lax.broadcasted_iota(jnp.int32, (ATTN_BLK, 2 * ATTN_BLK), 0)
    jj = lax.broadcasted_iota(jnp.int32, (ATTN_BLK, 2 * ATTN_BLK), 1)
    diff = ATTN_BLK + ii - jj
    band = (diff >= 0) & (diff <= ATTN_BLK)
    bias = -(slope * float(dil)) * diff.astype(F32)
    return band, bias, jj


def _attn_window(t, nblk):
    cur = pl.ds(pl.multiple_of(t * ATTN_BLK, ATTN_BLK), ATTN_BLK)
    prev = pl.ds(pl.multiple_of(jnp.maximum(t - 1, 0) * ATTN_BLK, ATTN_BLK), ATTN_BLK)
    first = jnp.where(t % nblk == 0, ATTN_BLK, 0)
    return prev, cur, first


def _unrolled_loop(n, step, init, unroll=ATTN_UNROLL):
    def trip(i, carry):
        for k in range(unroll):
            carry = step(i * unroll + k, carry)
        return carry

    return lax.fori_loop(0, n // unroll, trip, init)


def _streams(pairs, dil, s):
    if dil == 1:
        return [src for _, src in pairs]
    seg = s // dil
    for dst, src in pairs:
        for r in range(dil):
            dst[r * seg:(r + 1) * seg, :] = src[pl.ds(r, seg, stride=dil), :].astype(dst.dtype)
    return [dst for dst, _ in pairs]


def _attn_fwd(proj_a, slopes):
    _, s, d = proj_a.shape
    heads = d // HEAD_DIM
    scale = HEAD_DIM ** -0.5
    n_t = s // ATTN_BLK
    ng = len(DILATIONS)

    def body(q_ref, k_ref, v_ref, sl_ref, o_ref, lse_ref, qd, kd, vd, od, ld, og, lg):
        slope = sl_ref[...][:, :1]
        for g, dil in enumerate(DILATIONS):
            nblk = s // dil // ATTN_BLK
            qs, ks, vs = _streams([(qd, q_ref), (kd, k_ref), (vd, v_ref)], dil, s)
            o_t, l_t = (og.at[g], lg.at[g]) if dil == 1 else (od, ld)
            band, bias, jj = _attn_masks(slope, dil)

            def blk(t, carry, nblk=nblk, band=band, bias=bias, jj=jj, qs=qs, ks=ks, vs=vs, o_t=o_t, l_t=l_t):
                prev, cur, first = _attn_window(t, nblk)
                kk = jnp.concatenate([ks[prev, :], ks[cur, :]], axis=0)
                vv = jnp.concatenate([vs[prev, :], vs[cur, :]], axis=0)
                sc = _dot(qs[cur, :], kk, NT) * scale + bias
                sc = jnp.where(band & (jj >= first), sc, -jnp.inf)
                m = jnp.max(sc, axis=1, keepdims=True)
                p = jnp.exp(sc - m)
                l = jnp.sum(p, axis=1, keepdims=True)
                o_t[cur, :] = _dot(p, vv, NN) / l
                l_t[cur, :] = jnp.broadcast_to(m + jnp.log(l), (ATTN_BLK, HEAD_DIM))
                return carry

            _unrolled_loop(n_t, blk, 0)
            seg = s // dil
            if dil > 1:
                for r in range(dil):
                    og[g, pl.ds(r, seg, stride=dil), :] = od[r * seg:(r + 1) * seg, :]
                    lg[g, pl.ds(r, seg, stride=dil), :] = ld[r * seg:(r + 1) * seg, :]

        ch = 256

        def combine(c, carry):
            rows = pl.ds(pl.multiple_of(c * ch, ch), ch)
            ls = [lg[g, rows, :] for g in range(ng)]
            mx = functools.reduce(jnp.maximum, ls)
            es = [jnp.exp(x - mx) for x in ls]
            den = functools.reduce(jnp.add, es)
            num = functools.reduce(jnp.add, [es[g] * og[g, rows, :] for g in range(ng)])
            o_ref[rows, :] = (num / den).astype(o_ref.dtype)
            lse_ref[rows, :] = mx + jnp.log(den)
            return carry

        lax.fori_loop(0, s // ch, combine, 0)

    def col(slot):
        return pl.BlockSpec((None, s, HEAD_DIM), lambda h: (slot, 0, h))

    head = pl.BlockSpec((s, HEAD_DIM), lambda h: (0, h))
    return pl.pallas_call(
        body, name="attn_fwd", grid=(heads,),
        in_specs=[col(0), col(1), col(2), pl.BlockSpec((None, 1, HEAD_DIM), lambda h: (h, 0, 0))],
        out_specs=[head, head],
        out_shape=[jax.ShapeDtypeStruct((s, d), BF16), jax.ShapeDtypeStruct((s, d), F32)],
        scratch_shapes=[pltpu.VMEM((s, HEAD_DIM), BF16)] * 3 + [pltpu.VMEM((s, HEAD_DIM), F32)] * 2
        + [pltpu.VMEM((ng, s, HEAD_DIM), F32)] * 2,
        compiler_params=_params(VMEM_TILES_MIB),
    )(proj_a, proj_a, proj_a, slopes)


def _attn_bwd(proj_a, slopes, y_attn, lse, dy):
    _, s, d = proj_a.shape
    heads = d // HEAD_DIM
    scale = HEAD_DIM ** -0.5
    n_t = s // ATTN_BLK

    def body(q_ref, k_ref, v_ref, sl_ref, o_ref, lse_ref, dy_ref, out_ref,
             qd, kd, vd, dod, lsd, dld, delta, dqd, dkd, dvd, dqa, dka, dva):
        slope = sl_ref[...][:, :1]
        dyv = dy_ref[...]
        delta[...] = jnp.broadcast_to(
            jnp.sum(dyv * o_ref[...].astype(F32), axis=1, keepdims=True), (s, HEAD_DIM))
        for g, dil in enumerate(DILATIONS):
            nblk = s // dil // ATTN_BLK
            seg = s // dil
            qs, ks, vs, dos, lss, dls = _streams(
                [(qd, q_ref), (kd, k_ref), (vd, v_ref), (dod, dy_ref), (lsd, lse_ref), (dld, delta)], dil, s)
            dq_t, dk_t, dv_t = (dqa, dka, dva) if dil == 1 else (dqd, dkd, dvd)
            band, bias, jj = _attn_masks(slope, dil)

            def blk(t, carry, nblk=nblk, band=band, bias=bias, jj=jj, qs=qs, ks=ks, vs=vs, dos=dos, lss=lss,
                    dls=dls, dq_t=dq_t, dk_t=dk_t, dv_t=dv_t):
                ck, cv = carry
                prev, cur, first = _attn_window(t, nblk)
                q = qs[cur, :]
                do = dos[cur, :]
                lse_b = lss[cur, :]
                dl_b = dls[cur, :]
                kk = jnp.concatenate([ks[prev, :], ks[cur, :]], axis=0)
                vv = jnp.concatenate([vs[prev, :], vs[cur, :]], axis=0)
                sc = _dot(q, kk, NT) * scale + bias
                p = jnp.where(band & (jj >= first), jnp.exp(sc - jnp.concatenate([lse_b, lse_b], axis=1)), 0.0)
                dp = _dot(do, vv, NT)
                ds = p * (dp - jnp.concatenate([dl_b, dl_b], axis=1))
                dv_b = _dot(p, do, TN)
                dk_b = _dot(ds, q, TN) * scale
                dq_t[cur, :] = _dot(ds, kk, NN) * scale
                done = pl.ds(pl.multiple_of(jnp.where(t == 0, n_t, t - 1) * ATTN_BLK, ATTN_BLK), ATTN_BLK)
                dk_t[done, :] = ck + dk_b[:ATTN_BLK]
                dv_t[done, :] = cv + dv_b[:ATTN_BLK]
                return dk_b[ATTN_BLK:], dv_b[ATTN_BLK:]

            zero = jnp.zeros((ATTN_BLK, HEAD_DIM), F32)
            ck, cv = _unrolled_loop(n_t, blk, (zero, zero))
            dk_t[(n_t - 1) * ATTN_BLK:n_t * ATTN_BLK, :] = ck
            dv_t[(n_t - 1) * ATTN_BLK:n_t * ATTN_BLK, :] = cv
            if dil > 1:
                for acc, part in ((dqa, dqd), (dka, dkd), (dva, dvd)):
                    for r in range(dil):
                        acc[pl.ds(r, seg, stride=dil), :] += part[r * seg:(r + 1) * seg, :]
        out_ref[0] = dqa[...].astype(out_ref.dtype)
        out_ref[1] = dka[0:s, :].astype(out_ref.dtype)
        out_ref[2] = dva[0:s, :].astype(out_ref.dtype)

    def col(slot):
        return pl.BlockSpec((None, s, HEAD_DIM), lambda h: (slot, 0, h))

    head = pl.BlockSpec((s, HEAD_DIM), lambda h: (0, h))
    return pl.pallas_call(
        body, name="attn_bwd", grid=(heads,),
        in_specs=[col(0), col(1), col(2), pl.BlockSpec((None, 1, HEAD_DIM), lambda h: (h, 0, 0)),
                  head, head, head],
        out_specs=pl.BlockSpec((N_QKV, s, HEAD_DIM), lambda h: (0, 0, h)),
        out_shape=jax.ShapeDtypeStruct((N_QKV, s, d), BF16),
        scratch_shapes=[pltpu.VMEM((s, HEAD_DIM), BF16)] * 4 + [pltpu.VMEM((s, HEAD_DIM), F32)] * 4
        + [pltpu.VMEM((s + ATTN_BLK, HEAD_DIM), F32)] * 2 + [pltpu.VMEM((s, HEAD_DIM), F32)]
        + [pltpu.VMEM((s + ATTN_BLK, HEAD_DIM), F32)] * 2,
        compiler_params=_params(VMEM_TILES_MIB),
    )(proj_a, proj_a, proj_a, slopes, y_attn, lse, dy)


def _expm1(x):
    small = x * (1.0 + x * (0.5 + x * (1.0 / 6.0 + x * (1.0 / 24.0 + x * (1.0 / 120.0)))))
    return jnp.where(jnp.abs(x) < 0.1, small, jnp.exp(x) - 1.0)


def _softplus(x):
    return jnp.maximum(x, 0.0) + jnp.log1p(jnp.exp(-jnp.abs(x)))


GELU_K = 0.7978845608028654
GELU_C = 0.044715


def _gelu(x):
    t = jnp.tanh(GELU_K * (x + GELU_C * x * x * x))
    return 0.5 * x * (1.0 + t), t


def _gelu_grad(x, t):
    return 0.5 * (1.0 + t) + 0.5 * x * (1.0 - t * t) * GELU_K * (1.0 + 3.0 * GELU_C * x * x)


def _lru_gates(xc, wa, ba, wx, bx, sp):
    r = _sigmoid(_dot(xc, wa, NN) + ba)
    ig = _sigmoid(_dot(xc, wx, NN) + bx)
    log_a = -LRU_C * r * sp
    a = jnp.exp(log_a)
    mult = jnp.sqrt(-_expm1(2.0 * log_a))
    return r, ig, a, mult


def _scan_fwd(a, u, tt):
    row = lax.broadcasted_iota(jnp.int32, a.shape, 0)
    sh = 1
    while sh < tt:
        keep = row >= sh
        a_s = jnp.where(keep, pltpu.roll(a, sh, 0), 1.0)
        u_s = jnp.where(keep, pltpu.roll(u, sh, 0), 0.0)
        u = a * u_s + u
        a = a * a_s
        sh *= 2
    return a, u


def _scan_bwd(b, g, tt):
    row = lax.broadcasted_iota(jnp.int32, b.shape, 0)
    sh = 1
    while sh < tt:
        keep = row < tt - sh
        b_s = jnp.where(keep, pltpu.roll(b, tt - sh, 0), 1.0)
        g_s = jnp.where(keep, pltpu.roll(g, tt - sh, 0), 0.0)
        g = g + b * g_s
        b = b * b_s
        sh *= 2
    return b, g


ROW_CHUNK = 256


def _pad_copy(xpad_ref, x_ref, s):
    xpad_ref[0:8, :] = jnp.zeros((8, HEAD_DIM), F32)
    for c0 in range(0, s, ROW_CHUNK):
        xpad_ref[8 + c0:8 + c0 + ROW_CHUNK, :] = x_ref[c0:c0 + ROW_CHUNK, :]


def _conv_rows(dst_ref, xpad_ref, cw, cb, s):
    for c0 in range(0, s, ROW_CHUNK):
        acc = cb
        for j in range(CONV_TAPS):
            off = 8 - (CONV_TAPS - 1) + j + c0
            acc = acc + cw[j:j + 1, :] * xpad_ref[off:off + ROW_CHUNK, :]
        dst_ref[c0:c0 + ROW_CHUNK, :] = acc


LRU_TILE = 128
LRU_UNROLL = 16


def _lru_specs(s, d):
    heads = d // HEAD_DIM

    def col(slot):
        return pl.BlockSpec((None, s, HEAD_DIM), lambda h: (slot, 0, h))

    vec = pl.BlockSpec((1, HEAD_DIM), lambda h: (0, h))
    mat = pl.BlockSpec((None, HEAD_DIM, HEAD_DIM), lambda h: (h, 0, 0))
    cw = pl.BlockSpec((8, HEAD_DIM), lambda h: (0, h))
    head = pl.BlockSpec((s, HEAD_DIM), lambda h: (0, h))
    return heads, col, vec, mat, cw, head


def _lru_fwd(proj_b, conv_w, conv_b, wa, ba, wx, bx, lam):
    _, s, d = proj_b.shape
    heads, col, vec, mat, cws, head = _lru_specs(s, d)
    tt = LRU_TILE

    def body(xr_ref, xg_ref, cw_ref, cb_ref, wa_ref, ba_ref, wx_ref, bx_ref, lam_ref, y_ref, h_ref, xpad, xc_s):
        _pad_copy(xpad, xr_ref, s)
        _conv_rows(xc_s, xpad, cw_ref[...], cb_ref[...], s)
        sp = _softplus(-lam_ref[...])
        wav, wxv, bav, bxv = wa_ref[...], wx_ref[...], ba_ref[...], bx_ref[...]

        def tile(i, hc):
            rows = pl.ds(pl.multiple_of(i * tt, tt), tt)
            xc = xc_s[rows, :]
            _, ig, a, mult = _lru_gates(xc, wav, bav, wxv, bxv, sp)
            pa, hl = _scan_fwd(a, mult * (ig * xc), tt)
            h = hl + pa * hc
            h_ref[rows, :] = h
            gel, _ = _gelu(xg_ref[rows, :])
            y_ref[rows, :] = (h * gel).astype(y_ref.dtype)
            return h[tt - 1:tt, :]

        _unrolled_loop(s // tt, tile, jnp.zeros((1, HEAD_DIM), F32), LRU_UNROLL)

    return pl.pallas_call(
        body, name="lru_fwd", grid=(heads,),
        in_specs=[col(N_QKV), col(N_QKV + 1), cws, vec, mat, vec, mat, vec, vec],
        out_specs=[head, head],
        out_shape=[jax.ShapeDtypeStruct((s, d), BF16), jax.ShapeDtypeStruct((s, d), F32)],
        scratch_shapes=[pltpu.VMEM((s + 8, HEAD_DIM), F32), pltpu.VMEM((s, HEAD_DIM), F32)],
        compiler_params=_params(VMEM_STREAM_MIB),
    )(proj_b, proj_b, conv_w, conv_b, wa, ba, wx, bx, lam)


def _lru_bwd(proj_b, h_lru, dy, conv_w, conv_b, wa, ba, wx, bx, lam, dproj_b):
    _, s, d = proj_b.shape
    heads, col, vec, mat, cws, head = _lru_specs(s, d)
    tt = LRU_TILE
    n_t = s // tt

    def body(xr_ref, xg_ref, h_ref, dy_ref, cw_ref, cb_ref, wa_ref, ba_ref, wx_ref, bx_ref, lam_ref, alias_ref,
             out_ref, dcw_ref, dcb_ref, dwa_ref, dba_ref, dwx_ref, dbx_ref, dlam_ref, xpad, xc_s, dxc_s):
        del alias_ref
        _pad_copy(xpad, xr_ref, s)
        cwv = cw_ref[...]
        _conv_rows(xc_s, xpad, cwv, cb_ref[...], s)
        dxc_s[s:s + 8, :] = jnp.zeros((8, HEAD_DIM), F32)
        lamv = lam_ref[...]
        sp = _softplus(-lamv)
        wav, wxv, bav, bxv = wa_ref[...], wx_ref[...], ba_ref[...], bx_ref[...]
        dwa_ref[...] = jnp.zeros_like(dwa_ref)
        dwx_ref[...] = jnp.zeros_like(dwx_ref)
        zero = jnp.zeros((1, HEAD_DIM), F32)
        row = lax.broadcasted_iota(jnp.int32, (tt, HEAD_DIM), 0)

        def tile(it, carry):
            dh_next, a_next, dba, dbx, dsp, dcb = carry
            i = n_t - 1 - it
            t0 = pl.multiple_of(i * tt, tt)
            rows = pl.ds(t0, tt)
            xc = xc_s[rows, :]
            r, ig, a, mult = _lru_gates(xc, wav, bav, wxv, bxv, sp)
            h = h_ref[rows, :]
            before = h_ref[pl.ds(pl.multiple_of(jnp.maximum(t0 - 8, 0), 8), 8), :][7:8, :]
            before = before * (i > 0).astype(F32)
            h_prev = jnp.where(row == 0, before, pltpu.roll(h, 1, 0))
            xg = xg_ref[rows, :]
            dyv = dy_ref[rows, :]
            gel, th = _gelu(xg)
            out_ref[1, rows, :] = (dyv * h * _gelu_grad(xg, th)).astype(out_ref.dtype)
            b = jnp.where(row == tt - 1, a_next, pltpu.roll(a, tt - 1, 0))
            pb, z = _scan_bwd(b, dyv * gel, tt)
            dh = z + pb * dh_next
            da = dh * h_prev
            dmult = dh * (ig * xc)
            dig = dh * (mult * xc)
            dla = da * a - dmult * (a * a / mult)
            dzr = dla * (-LRU_C * sp) * (r * (1.0 - r))
            dzx = dig * (ig * (1.0 - ig))
            dxc = dh * (mult * ig) + _dot(dzr, wav, NT) + _dot(dzx, wxv, NT)
            dxc_s[rows, :] = dxc
            dwa_ref[...] += _dot(xc, dzr, TN)
            dwx_ref[...] += _dot(xc, dzx, TN)
            return (dh[0:1, :], a[0:1, :],
                    dba + jnp.sum(dzr, axis=0, keepdims=True),
                    dbx + jnp.sum(dzx, axis=0, keepdims=True),
                    dsp + jnp.sum(dla * (-LRU_C * r), axis=0, keepdims=True),
                    dcb + jnp.sum(dxc, axis=0, keepdims=True))

        _, _, dba, dbx, dsp, dcb = _unrolled_loop(n_t, tile, (zero, zero, zero, zero, zero, zero), LRU_UNROLL)
        dba_ref[...] = dba
        dbx_ref[...] = dbx
        dcb_ref[...] = dcb
        dlam_ref[...] = -dsp * _sigmoid(-lamv)
        dcw = [zero] * CONV_TAPS
        for c0 in range(0, s, ROW_CHUNK):
            dxc_c = dxc_s[c0:c0 + ROW_CHUNK, :]
            dxr = jnp.zeros((ROW_CHUNK, HEAD_DIM), F32)
            for j in range(CONV_TAPS):
                back = CONV_TAPS - 1 - j
                off = 8 - back + c0
                dcw[j] = dcw[j] + jnp.sum(dxc_c * xpad[off:off + ROW_CHUNK, :], axis=0, keepdims=True)
                dxr = dxr + cwv[j:j + 1, :] * dxc_s[back + c0:back + c0 + ROW_CHUNK, :]
            out_ref[0, c0:c0 + ROW_CHUNK, :] = dxr.astype(out_ref.dtype)
        for j in range(CONV_TAPS):
            dcw_ref[j:j + 1, :] = dcw[j]

    return pl.pallas_call(
        body, name="lru_bwd", grid=(heads,),
        in_specs=[col(N_QKV), col(N_QKV + 1), head, head, cws, vec, mat, vec, mat, vec, vec,
                  pl.BlockSpec(memory_space=pl.ANY)],
        out_specs=[pl.BlockSpec((2, s, HEAD_DIM), lambda h: (0, 0, h)),
                   pl.BlockSpec((CONV_TAPS, HEAD_DIM), lambda h: (0, h)), vec, mat, vec, mat, vec, vec],
        out_shape=[jax.ShapeDtypeStruct(dproj_b.shape, dproj_b.dtype),
                   jax.ShapeDtypeStruct((CONV_TAPS, d), F32), jax.ShapeDtypeStruct((1, d), F32),
                   jax.ShapeDtypeStruct(wa.shape, F32), jax.ShapeDtypeStruct((1, d), F32),
                   jax.ShapeDtypeStruct(wx.shape, F32), jax.ShapeDtypeStruct((1, d), F32),
                   jax.ShapeDtypeStruct((1, d), F32)],
        scratch_shapes=[pltpu.VMEM((s + 8, HEAD_DIM), F32), pltpu.VMEM((s, HEAD_DIM), F32),
                        pltpu.VMEM((s + 8, HEAD_DIM), F32)],
        input_output_aliases={11: 0},
        compiler_params=_params(VMEM_STREAM_MIB),
    )(proj_b, proj_b, h_lru, dy, conv_w, conv_b, wa, ba, wx, bx, lam, dproj_b)


def _place():
    x, y, c = (lax.axis_index(n) for n in AXES)
    return x, y, c


def _other_chips(x, y):
    return [(1 - x, y), (x, 1 - y), (1 - x, 1 - y)]


HBM = pl.BlockSpec(memory_space=pl.ANY)


def _cast_shard(w, chip_arr, name):
    r, cols = w.shape
    rh = r // 2
    tr = _row_tile(rh, cols * 4, STREAM_TILE)
    nt = rh // tr

    def body(chip_ref, w_ref, o_ref):
        del chip_ref
        o_ref[...] = w_ref[...].astype(BF16)

    return pl.pallas_call(
        body, name=name,
        grid_spec=pltpu.PrefetchScalarGridSpec(
            num_scalar_prefetch=1, grid=(2, nt),
            in_specs=[pl.BlockSpec((tr, cols), lambda h, i, chip_ref: (h * nt + i, 0))],
            out_specs=pl.BlockSpec((None, None, tr, cols), lambda h, i, chip_ref: (chip_ref[0], h, i, 0))),
        out_shape=jax.ShapeDtypeStruct((N_CHIPS, 2, rh, cols), BF16), compiler_params=_params(VMEM_STREAM_MIB),
    )(chip_arr, w)


HBM_SPEC = pl.BlockSpec(memory_space=pltpu.HBM)
SEM_SPEC = pl.BlockSpec(memory_space=pltpu.SEMAPHORE)
EFFECT = pltpu.SideEffectType.DATAFLOW_SIDE_EFFECTING
TOKEN = jax.ShapeDtypeStruct((8, 128), F32)
TOKEN_SPEC = pl.BlockSpec(memory_space=pltpu.VMEM)


def _in_hbm(arrays):
    return [pltpu.with_memory_space_constraint(a, pltpu.HBM) for a in arrays]


def _hbm_like(arrays):
    return [pltpu.HBM(a.shape, a.dtype) for a in arrays]


def _sems(n):
    return pltpu.SemaphoreType.DMA((n,))


def _remote(src, dst, send_sem, recv_sem, to):
    return pltpu.make_async_remote_copy(src_ref=src, dst_ref=dst, send_sem=send_sem, recv_sem=recv_sem,
                                        device_id=to, device_id_type=MESH)


ALL_FLIPS = (0, 1, 2)


def _gather_start(bufs, groups, after, name):
    n = len(bufs)
    ng = len(groups)

    def body(*refs):
        ins = refs[:n]
        sems = refs[n + len(after):n + len(after) + 2 * ng]
        x, y, c = _place()
        me = 2 * x + y
        chips = _other_chips(x, y)
        for g, (ws, flips) in enumerate(groups):
            for i, w in enumerate(ws):
                for jj, j in enumerate(flips):
                    k = len(flips) * i + jj
                    mine = ins[w].at[me, c]
                    _remote(mine, mine, sems[2 * g].at[k], sems[2 * g + 1].at[k], (*chips[j], c)).start()

    sem_shapes = []
    for ws, flips in groups:
        sem_shapes += [_sems(len(flips) * len(ws))] * 2
    res = pl.pallas_call(
        body, name=name, in_specs=[HBM_SPEC] * n + [HBM] * len(after),
        out_specs=[SEM_SPEC] * (2 * ng) + [HBM_SPEC] * n, out_shape=sem_shapes + _hbm_like(bufs),
        input_output_aliases={w: 2 * ng + w for w in range(n)},
        compiler_params=pltpu.CompilerParams(has_side_effects=EFFECT),
    )(*_in_hbm(bufs), *after)
    return [(res[2 * g], res[2 * g + 1]) for g in range(ng)], list(res[2 * ng:])


def _gather_forward(bufs, recv, after, name, flips=ALL_FLIPS):
    m = len(bufs)
    nf = len(flips)

    def body(*refs):
        ins, recv_in = refs[:m], refs[m]
        fsend, frecv = refs[m + 1 + len(after)], refs[m + 2 + len(after)]
        x, y, c = _place()
        chips = _other_chips(x, y)
        for jj, j in enumerate(flips):
            cx, cy = chips[j]
            for i in range(m):
                landed = ins[i].at[2 * cx + cy, c]
                k = nf * i + jj
                _remote(landed, landed, fsend.at[k], recv_in.at[k], (cx, cy, c)).wait_recv()
                _remote(landed, landed, fsend.at[k], frecv.at[k], (x, y, 1 - c)).start()

    res = pl.pallas_call(
        body, name=name, in_specs=[HBM_SPEC] * m + [SEM_SPEC] + [HBM] * len(after),
        out_specs=[SEM_SPEC, SEM_SPEC] + [HBM_SPEC] * m, out_shape=[_sems(nf * m), _sems(nf * m)] + _hbm_like(bufs),
        input_output_aliases={i: 2 + i for i in range(m)},
        compiler_params=pltpu.CompilerParams(has_side_effects=EFFECT),
    )(*bufs, recv, *after)
    return (res[0], res[1]), list(res[2:])


NEIGHBOURS = (0, 1)


def _relay_partner(x, y, c):
    return 2 * (x ^ (1 - c)) + (y ^ c), (x ^ c, y ^ (1 - c))


def _gather_forward_relay(bufs, recv, after, name):
    m = len(bufs)
    nf = len(NEIGHBOURS)

    def body(*refs):
        ins, recv_in = refs[:m], refs[m]
        fsend, frecv, rsend, rrecv = refs[m + 1 + len(after):m + 5 + len(after)]
        x, y, c = _place()
        chips = _other_chips(x, y)
        for jj, j in enumerate(NEIGHBOURS):
            cx, cy = chips[j]
            for i in range(m):
                landed = ins[i].at[2 * cx + cy, c]
                k = nf * i + jj
                _remote(landed, landed, fsend.at[k], recv_in.at[k], (cx, cy, c)).wait_recv()
        row, (px, py) = _relay_partner(x, y, c)
        for i in range(m):
            relayed = ins[i].at[row, c]
            _remote(relayed, relayed, rsend.at[i], rrecv.at[i], (px, py, c)).start()
        for jj, j in enumerate(NEIGHBOURS):
            cx, cy = chips[j]
            for i in range(m):
                landed = ins[i].at[2 * cx + cy, c]
                k = nf * i + jj
                _remote(landed, landed, fsend.at[k], frecv.at[k], (x, y, 1 - c)).start()

    res = pl.pallas_call(
        body, name=name, in_specs=[HBM_SPEC] * m + [SEM_SPEC] + [HBM] * len(after),
        out_specs=[SEM_SPEC] * 4 + [HBM_SPEC] * m,
        out_shape=[_sems(nf * m), _sems(nf * m), _sems(m), _sems(m)] + _hbm_like(bufs),
        input_output_aliases={i: 4 + i for i in range(m)},
        compiler_params=pltpu.CompilerParams(has_side_effects=EFFECT),
    )(*bufs, recv, *after)
    return tuple(res[:4]), list(res[4:])


def _gather_forward_diag(bufs, rrecv, after, name):
    m = len(bufs)

    def body(*refs):
        ins, rrecv_in = refs[:m], refs[m]
        dsend, drecv = refs[m + 1 + len(after)], refs[m + 2 + len(after)]
        x, y, c = _place()
        diag = (2 * x + y) ^ 3
        _, (px, py) = _relay_partner(x, y, c)
        for i in range(m):
            landed = ins[i].at[diag, c]
            _remote(landed, landed, dsend.at[i], rrecv_in.at[i], (px, py, c)).wait_recv()
            _remote(landed, landed, dsend.at[i], drecv.at[i], (x, y, 1 - c)).start()

    res = pl.pallas_call(
        body, name=name, in_specs=[HBM_SPEC] * m + [SEM_SPEC] + [HBM] * len(after),
        out_specs=[SEM_SPEC, SEM_SPEC] + [HBM_SPEC] * m, out_shape=[_sems(m), _sems(m)] + _hbm_like(bufs),
        input_output_aliases={i: 2 + i for i in range(m)},
        compiler_params=pltpu.CompilerParams(has_side_effects=EFFECT),
    )(*bufs, rrecv, *after)
    return (res[0], res[1]), list(res[2:])


def _gather_finish_diag(bufs, rsend, dsend, drecv, after, name):
    m = len(bufs)

    def body(*refs):
        ins = refs[:m]
        rsend_in, dsend_in, drecv_in = refs[m:m + 3]
        x, y, c = _place()
        diag = (2 * x + y) ^ 3
        row, (px, py) = _relay_partner(x, y, c)
        for i in range(m):
            relayed = ins[i].at[row, c]
            _remote(relayed, relayed, rsend_in.at[i], drecv_in.at[i], (px, py, c)).wait_send()
            landed = ins[i].at[diag, c]
            _remote(landed, landed, dsend_in.at[i], drecv_in.at[i], (x, y, 1 - c)).wait_send()
            theirs = ins[i].at[diag, 1 - c]
            _remote(theirs, theirs, dsend_in.at[i], drecv_in.at[i], (x, y, 1 - c)).wait_recv()

    return list(pl.pallas_call(
        body, name=name, in_specs=[HBM_SPEC] * m + [SEM_SPEC] * 3 + [HBM] * len(after),
        out_specs=[HBM_SPEC] * m, out_shape=_hbm_like(bufs),
        input_output_aliases={i: i for i in range(m)},
        compiler_params=pltpu.CompilerParams(has_side_effects=EFFECT),
    )(*bufs, rsend, dsend, drecv, *after))


def _gather_finish(bufs, send, fsend, frecv, after, name, flips=ALL_FLIPS):
    m = len(bufs)
    nf = len(flips)

    def body(*refs):
        ins = refs[:m]
        send_in, fsend_in, frecv_in = refs[m:m + 3]
        x, y, c = _place()
        me = 2 * x + y
        chips = _other_chips(x, y)
        for jj, j in enumerate(flips):
            cx, cy = chips[j]
            cj = 2 * cx + cy
            for i in range(m):
                k = nf * i + jj
                mine = ins[i].at[me, c]
                _remote(mine, mine, send_in.at[k], frecv_in.at[k], (cx, cy, c)).wait_send()
                landed = ins[i].at[cj, c]
                _remote(landed, landed, fsend_in.at[k], frecv_in.at[k], (x, y, 1 - c)).wait_send()
                theirs = ins[i].at[cj, 1 - c]
                _remote(theirs, theirs, fsend_in.at[k], frecv_in.at[k], (x, y, 1 - c)).wait_recv()

    return list(pl.pallas_call(
        body, name=name, in_specs=[HBM_SPEC] * m + [SEM_SPEC] * 3 + [HBM] * len(after),
        out_specs=[HBM_SPEC] * m, out_shape=_hbm_like(bufs),
        input_output_aliases={i: i for i in range(m)},
        compiler_params=pltpu.CompilerParams(has_side_effects=EFFECT),
    )(*bufs, send, fsend, frecv, *after))


def _pair_exchange(grads, name):
    n = len(grads)

    def body(*refs):
        ins, outs = refs[:n], refs[n:2 * n]
        send_sems, recv_sems = refs[2 * n:]
        x, y, c = _place()
        sibling = (x, y, 1 - c)
        cps = []
        for w in range(n):
            for j in range(N_CHIPS):
                cp = pltpu.make_async_remote_copy(
                    src_ref=ins[w].at[j, 1 - c], dst_ref=outs[w].at[j], send_sem=send_sems.at[N_CHIPS * w + j],
                    recv_sem=recv_sems.at[N_CHIPS * w + j], device_id=sibling, device_id_type=MESH)
                cp.start()
                cps.append(cp)
        for cp in cps:
            cp.wait()

    return pl.pallas_call(
        body, name=name, in_specs=[HBM] * n, out_specs=[HBM] * n,
        out_shape=[jax.ShapeDtypeStruct((N_CHIPS,) + a.shape[2:], a.dtype) for a in grads],
        scratch_shapes=[pltpu.SemaphoreType.DMA((N_CHIPS * n,)), pltpu.SemaphoreType.DMA((N_CHIPS * n,))],
    )(*grads)


def _pair_start(grads, name):
    n = len(grads)
    lands = [lax.empty((N_CHIPS,) + a.shape[2:], a.dtype) for a in grads]

    def body(*refs):
        ins, land_in = refs[:n], refs[n:2 * n]
        send, recv = refs[2 * n], refs[2 * n + 1]
        token = refs[4 * n + 2]
        x, y, c = _place()
        for w in range(n):
            for j in range(N_CHIPS):
                k = N_CHIPS * w + j
                _remote(ins[w].at[j, 1 - c], land_in[w].at[j], send.at[k], recv.at[k], (x, y, 1 - c)).start()
        token[...] = jnp.zeros_like(token)

    res = pl.pallas_call(
        body, name=name, in_specs=[HBM_SPEC] * (2 * n),
        out_specs=[SEM_SPEC, SEM_SPEC] + [HBM_SPEC] * (2 * n) + [TOKEN_SPEC],
        out_shape=[_sems(N_CHIPS * n), _sems(N_CHIPS * n)] + _hbm_like(grads) + _hbm_like(lands) + [TOKEN],
        input_output_aliases={i: 2 + i for i in range(2 * n)},
        compiler_params=pltpu.CompilerParams(has_side_effects=EFFECT),
    )(*_in_hbm(grads), *_in_hbm(lands))
    return (res[0], res[1]), list(res[2:2 + n]), list(res[2 + n:2 + 2 * n]), res[2 + 2 * n]


def _pair_wait(sems, grads, lands, after, name):
    n = len(grads)

    def body(*refs):
        ins, land_in = refs[:n], refs[n:2 * n]
        send_in, recv_in = refs[2 * n], refs[2 * n + 1]
        x, y, c = _place()
        for w in range(n):
            for j in range(N_CHIPS):
                k = N_CHIPS * w + j
                cp = _remote(ins[w].at[j, 1 - c], land_in[w].at[j], send_in.at[k], recv_in.at[k], (x, y, 1 - c))
                cp.wait_send()
                cp.wait_recv()

    res = pl.pallas_call(
        body, name=name, in_specs=[HBM_SPEC] * (2 * n) + [SEM_SPEC, SEM_SPEC] + [HBM] * len(after),
        out_specs=[HBM_SPEC] * (2 * n), out_shape=_hbm_like(grads) + _hbm_like(lands),
        input_output_aliases={i: i for i in range(2 * n)},
        compiler_params=pltpu.CompilerParams(has_side_effects=EFFECT),
    )(*grads, *lands, sems[0], sems[1], *after)
    return list(res[:n]), list(res[n:])


def _chip_start(sums, name, to_all=()):
    m = len(sums)
    lands = [lax.empty(((N_CHIPS,) if i in to_all else ()) + a.shape, a.dtype) for i, a in enumerate(sums)]

    def body(*refs):
        ins, land_in = refs[:m], refs[m:2 * m]
        send, recv = refs[2 * m], refs[2 * m + 1]
        token = refs[4 * m + 2]
        x, y, c = _place()
        me = 2 * x + y
        for i in sorted(range(m), key=lambda i: i not in to_all):
            for j, (cx, cy) in enumerate(_other_chips(x, y)):
                src = ins[i] if i in to_all else ins[i].at[2 * cx + cy]
                _remote(src, land_in[i].at[me], send.at[3 * i + j], recv.at[3 * i + j], (cx, cy, c)).start()
        token[...] = jnp.zeros_like(token)

    res = pl.pallas_call(
        body, name=name, in_specs=[HBM_SPEC] * (2 * m),
        out_specs=[SEM_SPEC, SEM_SPEC] + [HBM_SPEC] * (2 * m) + [TOKEN_SPEC],
        out_shape=[_sems(3 * m), _sems(3 * m)] + _hbm_like(sums) + _hbm_like(lands) + [TOKEN],
        input_output_aliases={i: 2 + i for i in range(2 * m)},
        compiler_params=pltpu.CompilerParams(has_side_effects=EFFECT),
    )(*_in_hbm(sums), *_in_hbm(lands))
    return (res[0], res[1]), list(res[2:2 + m]), list(res[2 + m:2 + 2 * m]), res[2 + 2 * m]


def _chip_wait(sems, sums, lands, after, name, to_all=()):
    m = len(sums)

    def body(*refs):
        ins, land_in = refs[:m], refs[m:2 * m]
        send_in, recv_in = refs[2 * m], refs[2 * m + 1]
        x, y, c = _place()
        for i in range(m):
            for j, (cx, cy) in enumerate(_other_chips(x, y)):
                cj = 2 * cx + cy
                src = ins[i] if i in to_all else ins[i].at[cj]
                cp = _remote(src, land_in[i].at[cj], send_in.at[3 * i + j], recv_in.at[3 * i + j], (cx, cy, c))
                cp.wait_send()
                cp.wait_recv()

    res = pl.pallas_call(
        body, name=name, in_specs=[HBM_SPEC] * (2 * m) + [SEM_SPEC, SEM_SPEC] + [HBM] * len(after),
        out_specs=[HBM_SPEC] * (2 * m), out_shape=_hbm_like(sums) + _hbm_like(lands),
        input_output_aliases={i: i for i in range(2 * m)},
        compiler_params=pltpu.CompilerParams(has_side_effects=EFFECT),
    )(*sums, *lands, sems[0], sems[1], *after)
    return list(res[:m]), list(res[m:])


def _half_parts(bufs):
    parts = []
    for w, a in enumerate(bufs):
        parts += [(w, None)] if a.ndim == 3 else [(w, j) for j in range(a.shape[0])]
    return parts


def _half_ref(refs, w, j, h):
    return refs[w].at[h] if j is None else refs[w].at[j, h]


def _half_start(bufs, name):
    n = len(bufs)
    parts = _half_parts(bufs)

    def body(*refs):
        ins = refs[:n]
        send, recv = refs[n], refs[n + 1]
        x, y, c = _place()
        for k, (w, j) in enumerate(parts):
            mine = _half_ref(ins, w, j, c)
            _remote(mine, mine, send.at[k], recv.at[k], (x, y, 1 - c)).start()

    res = pl.pallas_call(
        body, name=name, in_specs=[HBM_SPEC] * n, out_specs=[SEM_SPEC, SEM_SPEC] + [HBM_SPEC] * n,
        out_shape=[_sems(len(parts)), _sems(len(parts))] + _hbm_like(bufs),
        input_output_aliases={w: 2 + w for w in range(n)},
        compiler_params=pltpu.CompilerParams(has_side_effects=EFFECT),
    )(*_in_hbm(bufs))
    return (res[0], res[1]), list(res[2:])


def _half_wait(sems, bufs, after, name):
    n = len(bufs)
    parts = _half_parts(bufs)

    def body(*refs):
        ins = refs[:n]
        send_in, recv_in = refs[n], refs[n + 1]
        x, y, c = _place()
        for k, (w, j) in enumerate(parts):
            mine = _half_ref(ins, w, j, c)
            _remote(mine, mine, send_in.at[k], recv_in.at[k], (x, y, 1 - c)).wait_send()
            theirs = _half_ref(ins, w, j, 1 - c)
            _remote(theirs, theirs, send_in.at[k], recv_in.at[k], (x, y, 1 - c)).wait_recv()

    return list(pl.pallas_call(
        body, name=name, in_specs=[HBM_SPEC] * n + [SEM_SPEC, SEM_SPEC] + [HBM] * len(after),
        out_specs=[HBM_SPEC] * n, out_shape=_hbm_like(bufs),
        input_output_aliases={w: w for w in range(n)},
        compiler_params=pltpu.CompilerParams(has_side_effects=EFFECT),
    )(*bufs, sems[0], sems[1], *after))


def _all_gather8(block, name, after=()):
    def body(in_ref, *rest):
        out_ref, send_sems, recv_sems, local_sem = rest[len(after):]
        x, y, c = _place()
        me = 4 * x + 2 * y + c
        mine = pltpu.make_async_copy(in_ref, out_ref.at[me], local_sem)
        mine.start()
        flips = [(fx, fy, fc) for fx in (0, 1) for fy in (0, 1) for fc in (0, 1)][1:]
        cps = []
        for k, (fx, fy, fc) in enumerate(flips):
            cp = pltpu.make_async_remote_copy(
                src_ref=in_ref, dst_ref=out_ref.at[me], send_sem=send_sems.at[k], recv_sem=recv_sems.at[k],
                device_id=(x ^ fx, y ^ fy, c ^ fc), device_id_type=MESH)
            cp.start()
            cps.append(cp)
        for k, (fx, fy, fc) in enumerate(flips):
            px, py, pc = x ^ fx, y ^ fy, c ^ fc
            theirs = out_ref.at[4 * px + 2 * py + pc]
            pltpu.make_async_remote_copy(
                src_ref=theirs, dst_ref=theirs, send_sem=send_sems.at[k], recv_sem=recv_sems.at[k],
                device_id=(px, py, pc), device_id_type=MESH).wait_recv()
        for cp in cps:
            cp.wait_send()
        mine.wait()

    return pl.pallas_call(
        body, name=name, in_specs=[HBM] * (1 + len(after)), out_specs=HBM,
        out_shape=jax.ShapeDtypeStruct((N_DEV,) + block.shape, block.dtype),
        scratch_shapes=[pltpu.SemaphoreType.DMA((N_DEV - 1,)), pltpu.SemaphoreType.DMA((N_DEV - 1,)),
                        pltpu.SemaphoreType.DMA],
    )(block, *after)


def _pair_sum(grad, recv, c_arr, name):
    _, _, rh, cols = grad.shape
    tr = _row_tile(rh, cols * grad.dtype.itemsize, STREAM_TILE // 2)

    def body(c_ref, g_ref, r_ref, o_ref):
        del c_ref
        o_ref[...] = (g_ref[...].astype(F32) + r_ref[...].astype(F32)).astype(o_ref.dtype)

    spec = pl.BlockSpec((None, tr, cols), lambda j, i, c_ref: (j, i, 0))
    return pl.pallas_call(
        body, name=name,
        grid_spec=pltpu.PrefetchScalarGridSpec(
            num_scalar_prefetch=1, grid=(N_CHIPS, rh // tr),
            in_specs=[pl.BlockSpec((None, None, tr, cols), lambda j, i, c_ref: (j, c_ref[0], i, 0)), spec],
            out_specs=spec),
        out_shape=jax.ShapeDtypeStruct(recv.shape, grad.dtype), compiler_params=_params(VMEM_STREAM_MIB),
    )(c_arr, grad, recv)


def _sum_by_chip(chip, p_ref, own_ref, o_ref):
    o_ref[...] = jnp.zeros_like(o_ref)
    for k in range(N_CHIPS):
        @pl.when(chip == k)
        def _():
            o_ref[...] += own_ref[...].astype(F32)

        @pl.when(chip != k)
        def _(k=k):
            o_ref[...] += p_ref[k].astype(F32)


def _chip_sum_all(parts, own, place_arr, name):
    _, nj, rh, cols = parts.shape
    tr = _row_tile(rh, cols * 4, STREAM_TILE // 2)

    def body(place_ref, p_ref, own_ref, o_ref):
        _sum_by_chip(place_ref[0], p_ref, own_ref, o_ref)

    return pl.pallas_call(
        body, name=name,
        grid_spec=pltpu.PrefetchScalarGridSpec(
            num_scalar_prefetch=1, grid=(nj, rh // tr),
            in_specs=[pl.BlockSpec((N_CHIPS, None, tr, cols), lambda j, i, place_ref: (0, j, i, 0)),
                      pl.BlockSpec((None, tr, cols), lambda j, i, place_ref: (j, i, 0))],
            out_specs=pl.BlockSpec((None, None, tr, cols), lambda j, i, place_ref: (j, place_ref[1], i, 0))),
        out_shape=jax.ShapeDtypeStruct((nj, 2, rh, cols), F32), compiler_params=_params(VMEM_STREAM_MIB),
    )(place_arr, parts, own)


def _chip_sum(parts, own, place_arr, name):
    _, rh, cols = parts.shape
    tr = _row_tile(rh, cols * 4, STREAM_TILE // 2)

    def body(place_ref, p_ref, own_ref, o_ref):
        _sum_by_chip(place_ref[0], p_ref, own_ref, o_ref)

    return pl.pallas_call(
        body, name=name,
        grid_spec=pltpu.PrefetchScalarGridSpec(
            num_scalar_prefetch=1, grid=(rh // tr,),
            in_specs=[pl.BlockSpec((N_CHIPS, tr, cols), lambda i, place_ref: (0, i, 0)),
                      pl.BlockSpec((None, tr, cols), lambda i, place_ref: (place_ref[0], i, 0))],
            out_specs=pl.BlockSpec((None, tr, cols), lambda i, place_ref: (place_ref[1], i, 0))),
        out_shape=jax.ShapeDtypeStruct((2, rh, cols), F32), compiler_params=_params(VMEM_STREAM_MIB),
    )(place_arr, parts, own)


def _adamw_math(w, g, m, v):
    m = ADAM_B1 * m + (1.0 - ADAM_B1) * g
    v = ADAM_B2 * v + (1.0 - ADAM_B2) * (g * g)
    m_hat = m / (1.0 - ADAM_B1 ** ADAM_STEP)
    v_hat = v / (1.0 - ADAM_B2 ** ADAM_STEP)
    delta = -ADAM_LR * (m_hat / (jnp.sqrt(v_hat) + ADAM_EPS) + ADAM_WD * w)
    return delta, m, v


def _adamw(w, g, m, v, name):
    rows, cols = w.shape
    tr = _row_tile(rows, cols * 4)

    def body(w_ref, g_ref, m_ref, v_ref, go_ref, d_ref, nm_ref, nv_ref):
        gv = g_ref[...]
        go_ref[...] = gv
        d_ref[...], nm_ref[...], nv_ref[...] = _adamw_math(w_ref[...], gv, m_ref[...], v_ref[...])

    spec = pl.BlockSpec((tr, cols), lambda i: (i, 0))
    return pl.pallas_call(
        body, name=name, grid=(rows // tr,), in_specs=[spec] * 4, out_specs=[spec] * 4,
        out_shape=[jax.ShapeDtypeStruct(w.shape, F32)] * 4, compiler_params=_params(VMEM_STREAM_MIB),
    )(w, g, m, v)


def _sum8_adamw_row(parts, w, m, v, name):
    cols = parts.shape[2]

    def body(p_ref, w_ref, m_ref, v_ref, g_ref, d_ref, nm_ref, nv_ref):
        g = p_ref[0, 0:1, :]
        for k in range(1, N_DEV):
            g = g + p_ref[k, 0:1, :]
        g_ref[...] = g
        d_ref[...], nm_ref[...], nv_ref[...] = _adamw_math(w_ref[...], g, m_ref[...], v_ref[...])

    return pl.pallas_call(
        body, name=name, out_shape=[jax.ShapeDtypeStruct((1, cols), F32)] * 4, compiler_params=_params(VMEM_STREAM_MIB),
    )(parts, w, m, v)


def _pack_rows(pieces, rows, name):
    cols = pieces[0].shape[1]
    n = len(pieces)

    def body(*refs):
        o_ref = refs[n]
        o_ref[...] = jnp.zeros_like(o_ref)
        at = 0
        for p_ref in refs[:n]:
            r = p_ref.shape[0]
            o_ref[at:at + r, :] = p_ref[...]
            at += r

    return pl.pallas_call(
        body, name=name, out_shape=jax.ShapeDtypeStruct((rows, cols), F32), compiler_params=_params(VMEM_STREAM_MIB),
    )(*pieces)


def kernel(x, norm_mix_g, w_in, conv_w, conv_b, lru_wa, lru_ba, lru_wx, lru_bx, lru_lambda, w_proj_attn, w_proj_lru, w_out, norm_mlp_g, w_up, w_down, norm_final_g, loss_target, m_norm_mix_g, m_w_in, m_conv_w, m_conv_b, m_lru_wa, m_lru_ba, m_lru_wx, m_lru_bx, m_lru_lambda, m_w_proj_attn, m_w_proj_lru, m_w_out, m_norm_mlp_g, m_w_up, m_w_down, m_norm_final_g, v_norm_mix_g, v_w_in, v_conv_w, v_conv_b, v_lru_wa, v_lru_ba, v_lru_wx, v_lru_bx, v_lru_lambda, v_w_proj_attn, v_w_proj_lru, v_w_out, v_norm_mlp_g, v_w_up, v_w_down, v_norm_final_g):
    s, d = x.shape[1], x.shape[2]
    ff = w_up.shape[2] * N_CHIPS
    heads = d // HEAD_DIM
    u = d // 4
    assert s % (max(DILATIONS) * ATTN_BLK) == 0 and d % (4 * HEAD_DIM) == 0 and ff == 4 * d and DILATIONS[0] == 1
    xs, target = _in_hbm([x[0], loss_target[0]])
    gf = norm_final_g.reshape(1, d)
    wa, wx = lru_wa[0], lru_wx[0]
    core = lax.axis_index("c").astype(jnp.int32)
    chip = (2 * lax.axis_index("x") + lax.axis_index("y")).astype(jnp.int32)
    cidx = core.reshape(1)
    chip_arr = chip.reshape(1)
    place_arr = jnp.stack([chip, core])
    slopes = jnp.broadcast_to(
        (2.0 ** (-8.0 * jnp.arange(1, heads + 1, dtype=F32) / heads))[:, None, None], (heads, 1, HEAD_DIM))

    big = _in_hbm([w_in[0], w_proj_attn[0], w_proj_lru[0], w_out[0], w_up[0], w_down[0]])
    names = ["w_in", "w_proj_attn", "w_proj_lru", "w_out", "w_up", "w_down"]
    cw_pad = jnp.pad(conv_w[0], ((0, 8 - CONV_TAPS), (0, 0)))
    cw_all = _all_gather8(cw_pad, "gather_conv_w")
    conv_w_full = jnp.concatenate([cw_all[2 * j] for j in range(N_CHIPS)], axis=1)
    (sem_a,), buf_a = _gather_start([_cast_shard(big[0], chip_arr, "cast_w_in")], [([0], NEIGHBOURS)], [cw_all],
                                    "gather_start_w_in")

    xn = _rms_fwd(xs, norm_mix_g, "norm_mix")
    flip_bits = (2, 1, 3)

    def w_in_view():
        return buf_a[0].reshape(N_CHIPS, d, N_SLOTS * u)

    proj = _proj_in_shard(xn, w_in_view(), chip_arr, None, "proj_in_own")
    bufs = [_cast_shard(w, chip_arr, "cast_" + nm) for w, nm in zip(big[1:], names[1:])]
    (fs, fr, rs, rr), buf_a = _gather_forward_relay(buf_a, sem_a[1], [proj] + bufs, "gather_forward_w_in")
    buf_a = _gather_finish(buf_a, sem_a[0], fs, fr, [], "gather_finish_w_in", flips=NEIGHBOURS)
    for j in NEIGHBOURS:
        proj = _proj_in_shard(xn, w_in_view(), chip_arr ^ flip_bits[j], proj, "proj_in_from_%d" % j)
    (sem_b, sem_c, sem_d), bufs = _gather_start(
        bufs, [([0, 1, 2], ALL_FLIPS), ([3], ALL_FLIPS), ([4], ALL_FLIPS)], [proj], "gather_start_rest")
    (ds, dr), buf_a = _gather_forward_diag(buf_a, rr, [proj, bufs[0]], "gather_forward_w_in_diag")
    buf_a = _gather_finish_diag(buf_a, rs, ds, dr, [], "gather_finish_w_in_diag")
    proj = _proj_in_shard(xn, w_in_view(), chip_arr ^ flip_bits[2], proj, "proj_in_from_2")
    w_in_g = w_in_view()
    proj_a = proj_b = proj
    y_attn, lse = _attn_fwd(proj_a, slopes)
    y_lru, h_lru = _lru_fwd(proj_b, conv_w_full, conv_b, wa, lru_ba, wx, lru_bx, lru_lambda)
    fsem_b, buf_b = _gather_forward(bufs[:3], sem_b[1], [y_attn, y_lru], "gather_forward_proj")
    buf_b = _gather_finish(buf_b, sem_b[0], fsem_b[0], fsem_b[1], [], "gather_finish_proj")
    wpa_g = buf_b[0].reshape(d, d)
    wpl_g = buf_b[1].reshape(d, d)
    wout_g = buf_b[2].reshape(d, d)

    tn = u
    sd_f32 = jax.ShapeDtypeStruct((s, d), F32)
    sd_bf16 = jax.ShapeDtypeStruct((s, d), BF16)
    col = pl.BlockSpec((s, tn), lambda i, j, k: (0, j))

    def slot(n):
        return pl.BlockSpec((None, s, tn), lambda i, j, k: (n, 0, j))

    p_attn = _mm_nn("proj_attn", y_attn, wpa_g, [], [], [sd_f32], [col], _store, tn)[0]
    fsem_c, buf_c = _gather_forward(bufs[3:4], sem_c[1], [p_attn], "gather_forward_w_up")

    def merge(acc, extras, outs):
        pa_ref, ga_ref, gl_ref = extras
        merged = _sigmoid(ga_ref[...]) * pa_ref[...] + _sigmoid(gl_ref[...]) * acc
        outs[0][...] = merged.astype(BF16)
        outs[1][...] = acc

    tn2 = max(HEAD_DIM, u // 2)
    col2 = pl.BlockSpec((s, tn2), lambda i, j, k: (0, j))

    def slot2(n):
        return pl.BlockSpec((None, s, tn2), lambda i, j, k: (n, 0, j))

    merged, p_lru = _mm_nn("proj_lru_merge", y_lru, wpl_g, [p_attn, proj, proj], [col2, slot2(5), slot2(6)],
                           [sd_bf16, sd_f32], [col2, col2], merge, tn2)

    def add_resid(acc, extras, outs):
        outs[0][...] = extras[0][...] + acc

    h1 = _mm_nn("w_out_resid", merged, wout_g, [xs], [col], [sd_f32], [col], add_resid, tn)[0]
    hn = _rms_fwd(h1, norm_mlp_g, "norm_mlp")
    buf_c = _gather_finish(buf_c, sem_c[0], fsem_c[0], fsem_c[1], [hn], "gather_finish_w_up")
    wup_g = buf_c[0].reshape(N_CHIPS, d, d)

    def relu_sq(acc, extras, outs):
        r = jnp.maximum(acc, 0.0)
        outs[0][...] = (r * r).astype(BF16)
        outs[1][...] = r.astype(BF16)

    sf_bf16 = jax.ShapeDtypeStruct((s, ff), BF16)
    hid, relu_up = _mm(
        "w_up_relu2", [hn, wup_g],
        [pl.BlockSpec((s, d), lambda i, j, k: (0, 0)),
         pl.BlockSpec((None, d, tn), lambda i, j, k: (j // 4, 0, j % 4))],
        [sf_bf16, sf_bf16], [col, col], (1, ff // tn, 1), NN, relu_sq)
    fsem_d, buf_d = _gather_forward(bufs[4:], sem_d[1], [hid], "gather_forward_w_down")
    wdown_g = _gather_finish(buf_d, sem_d[0], fsem_d[0], fsem_d[1], [], "gather_finish_w_down")[0].reshape(ff, d)
    h2 = _mm(
        "w_down_resid", [hid, wdown_g, h1],
        [pl.BlockSpec((s, d), lambda i, j, k: (0, k)), pl.BlockSpec((d, tn), lambda i, j, k: (k, j)), col],
        [sd_f32], [col], (1, d // tn, ff // d), NN, add_resid, nk=ff // d, acc_shape=(s, tn))[0]
    loss_part, dh2, dh2_b, d_gf = _loss_head(h2, gf, target)
    loss = lax.psum(loss_part[0, 0], AXES)

    def relu_sq_bwd(acc, extras, outs):
        outs[0][...] = (acc * (2.0 * extras[0][...].astype(F32))).astype(BF16)

    dup = _mm_nt("d_hid", dh2_b, wdown_g, [relu_up], [col], [sf_bf16], [col], relu_sq_bwd, tn)[0]
    tok_d = pl.BlockSpec((s, d), lambda i, j: (0, 0))
    g_wdown = _mm_tn(
        "g_w_down", hid, dh2_b, pl.BlockSpec((s, d), lambda i, j: (0, i)),
        pl.BlockSpec((s, tn), lambda i, j: (0, j)), jax.ShapeDtypeStruct((ff, d), BF16),
        pl.BlockSpec((d, tn), lambda i, j: (i, j)), (ff // d, d // tn), d, tn, s)
    dhn = _mm(
        "d_hn", [dup, wup_g],
        [pl.BlockSpec((s, d), lambda i, j, k: (0, k)), pl.BlockSpec((None, tn, d), lambda i, j, k: (k, j, 0))],
        [sd_f32], [col], (1, d // tn, ff // d), NT, _store, nk=ff // d, acc_shape=(s, tn))[0]
    g_wup = _mm_tn(
        "g_w_up", hn, dup, tok_d, pl.BlockSpec((s, tn), lambda i, j: (0, j)),
        jax.ShapeDtypeStruct((N_CHIPS, d, d), BF16), pl.BlockSpec((None, d, tn), lambda i, j: (j // 4, 0, j % 4)),
        (1, ff // tn), d, tn, s)
    big_m = _in_hbm([m_w_in[0], m_w_proj_attn[0], m_w_proj_lru[0], m_w_out[0], m_w_up[0], m_w_down[0]])
    big_v = _in_hbm([v_w_in[0], v_w_proj_attn[0], v_w_proj_lru[0], v_w_out[0], v_w_up[0], v_w_down[0]])
    big_out = {}

    def reduce_begin(ids, gs, tag, everywhere=None):
        g4 = [g.reshape(N_CHIPS, 2, big[i].shape[0] // 2, big[i].shape[1]) for i, g in zip(ids, gs)]
        tags = [names[i] for i in ids]
        if everywhere is not None:
            g4.append(everywhere.reshape(N_CHIPS, 2, everywhere.shape[0] // (2 * N_CHIPS), everywhere.shape[1]))
            tags.append("small_" + tag)
        from_sibling = _pair_exchange(g4, "pair_exchange_" + tag)
        sums = [_pair_sum(g, r, cidx, "pair_sum_" + t) for t, g, r in zip(tags, g4, from_sibling)]
        to_all = () if everywhere is None else (len(ids),)
        return _chip_start(sums, "chip_start_" + tag, to_all), to_all

    def pair_begin(ids, gs, tag):
        g4 = [g.reshape(N_CHIPS, 2, big[i].shape[0] // 2, big[i].shape[1]) for i, g in zip(ids, gs)]
        return _pair_start(g4, "pair_start_" + tag)

    def reduce_begin_paired(ids, paired, after, tag):
        sems, g4, lands, _ = paired
        g4, from_sibling = _pair_wait(sems, g4, lands, after, "pair_wait_" + tag)
        sums = [_pair_sum(g, r, cidx, "pair_sum_" + names[i]) for i, g, r in zip(ids, g4, from_sibling)]
        return _chip_start(sums, "chip_start_" + tag), ()

    def reduce_mid(ids, begun, after, tag):
        (sems, sums, lands, _), to_all = begun
        sums, lands = _chip_wait(sems, sums, lands, after, "chip_wait_" + tag, to_all)
        halves = [_chip_sum(p, own, place_arr, "chip_sum_" + names[i]) for i, p, own in zip(ids, lands, sums)]
        if to_all:
            halves.append(_chip_sum_all(lands[-1], sums[-1], place_arr, "chip_sum_small_" + tag))
        return _half_start(halves, "half_start_" + tag), to_all

    def reduce_end(ids, mid, after, tag):
        (hsems, halves), to_all = mid
        full = _half_wait(hsems, halves, after, "half_wait_" + tag)
        done = []
        for i, g in zip(ids, full):
            res = _adamw(big[i], g.reshape(big[i].shape), big_m[i], big_v[i], "adamw_" + names[i])
            big_out[names[i]] = tuple(a[None] for a in res)
            done.append(res[1])
        everywhere = full[-1].reshape(-1, full[-1].shape[-1]) if to_all else None
        return everywhere, done

    def after_token(a, begun):
        return a + begun[0][3][:1, :1]

    pair_mlp = pair_begin([4, 5], [g_wup, g_wdown], "mlp")
    dh1, dh1_b, d_gmlp = _rms_bwd(h1, norm_mlp_g + pair_mlp[3][:1, :1], dhn, dh2, "norm_mlp_bwd")

    g_wout = _mm_tn(
        "g_w_out", merged, dh1_b, tok_d, pl.BlockSpec((s, tn), lambda i, j: (0, j)),
        jax.ShapeDtypeStruct((d, d), BF16), pl.BlockSpec((d, tn), lambda i, j: (0, j)), (1, d // tn), d, tn, s)

    def merge_bwd(acc, extras, outs):
        pa_ref, pl_ref, ga_ref, gl_ref = extras
        sa, sl = _sigmoid(ga_ref[...]), _sigmoid(gl_ref[...])
        outs[0][...] = (acc * sa).astype(BF16)
        outs[1][...] = (acc * sl).astype(BF16)
        outs[2][0] = (acc * pa_ref[...] * (sa * (1.0 - sa))).astype(BF16)
        outs[2][1] = (acc * pl_ref[...] * (sl * (1.0 - sl))).astype(BF16)

    nb = N_SLOTS - N_QKV
    d_pa, d_pl, dproj_b = _mm_nt(
        "d_merged", dh1_b, wout_g, [p_attn, p_lru, proj, proj], [col2, col2, slot2(5), slot2(6)],
        [sd_bf16, sd_bf16, jax.ShapeDtypeStruct((nb, s, d), BF16)],
        [col2, col2, pl.BlockSpec((2, s, tn2), lambda i, j, k: (1, 0, j))], merge_bwd, tn2)
    red_mlp = reduce_begin_paired([4, 5], pair_mlp, [d_pa], "mlp")
    dy_attn = _mm_nt("d_y_attn", d_pa, wpa_g, [], [], [sd_f32], [col], _store, tn)[0]
    dy_lru = _mm_nt("d_y_lru", d_pl, wpl_g, [], [], [sd_f32], [col], _store, tn)[0]
    g_wpa = _mm_tn(
        "g_w_proj_attn", y_attn, d_pa, tok_d, pl.BlockSpec((s, tn), lambda i, j: (0, j)),
        jax.ShapeDtypeStruct((d, d), BF16), pl.BlockSpec((d, tn), lambda i, j: (0, j)), (1, d // tn), d, tn, s)
    g_wpl = _mm_tn(
        "g_w_proj_lru", y_lru, d_pl, tok_d, pl.BlockSpec((s, tn), lambda i, j: (0, j)),
        jax.ShapeDtypeStruct((d, d), BF16), pl.BlockSpec((d, tn), lambda i, j: (0, j)), (1, d // tn), d, tn, s)

    pair_proj = pair_begin([1, 2, 3], [g_wpa, g_wpl, g_wout], "proj")

    dproj_b, d_cw, d_cb, d_wa, d_ba, d_wx, d_bx, d_lam = _lru_bwd(
        proj_b, h_lru, dy_lru, conv_w_full, conv_b, wa, lru_ba, wx, lru_bx, lru_lambda + pair_proj[3][:1, :1],
        dproj_b)
    red_proj = reduce_begin_paired([1, 2, 3], pair_proj, [dproj_b], "proj")
    dproj_a = _attn_bwd(proj_a, after_token(slopes, red_proj), y_attn, lse, dy_attn)
    per = N_SLOTS
    g_win_shape = jax.ShapeDtypeStruct((N_CHIPS, d, N_SLOTS * u), BF16)

    def g_win_part(name, dproj, first, prev):
        n_units = 4 * dproj.shape[0]
        return _mm_tn(
            name, xn, dproj, tok_d, pl.BlockSpec((None, s, u), lambda i, j: (j // 4, 0, j % 4)),
            g_win_shape, pl.BlockSpec((None, d, u), lambda i, j: ((j + first) // per, 0, (j + first) % per)),
            (1, n_units), d, u, s, aliases=None if prev is None else {2: 0}, extra=prev)

    mat_rows = heads * HEAD_DIM * HEAD_DIM // d
    vec_names = ["norm_mix_g", "conv_b", "lru_ba", "lru_bx", "lru_lambda", "norm_mlp_g", "norm_final_g"]

    def pack(wa_, wx_, cw_, vecs, name):
        rows = [wa_.reshape(mat_rows, d), wx_.reshape(mat_rows, d), cw_] + [a.reshape(1, d) for a in vecs]
        n = sum(a.shape[0] for a in rows)
        return _pack_rows(rows, n + (-n % 64), name)

    zero_cw = jnp.zeros((CONV_TAPS, d), F32)
    small_g = pack(d_wa, d_wx, d_cw, [jnp.zeros((1, d), F32), d_cb, d_ba, d_bx, d_lam, d_gmlp, d_gf], "pack_small_g")
    g_win = g_win_part("g_w_in_qkv", dproj_a, 0, None)
    g_win = g_win_part("g_w_in_rest", dproj_b, 4 * N_QKV, g_win)
    red_in = reduce_begin([0], [g_win], "w_in", everywhere=small_g)
    dxn = _dxn(dproj_a, dproj_b, w_in_g, 2 * tn, [red_in[0][3]])
    grad_x, _, d_gmix = _rms_bwd(xs, norm_mix_g, dxn, dh1, "norm_mix_bwd")

    mid_mlp = reduce_mid([4, 5], red_mlp, [grad_x], "mlp")
    mid_proj = reduce_mid([1, 2, 3], red_proj, [mid_mlp[0][1][0]], "proj")
    _, done_mlp = reduce_end([4, 5], mid_mlp, [mid_proj[0][1][0]], "mlp")
    _, done_proj = reduce_end([1, 2, 3], mid_proj, done_mlp[-1:], "proj")
    done = done_mlp + done_proj
    small_w = pack(wa, wx, zero_cw, [norm_mix_g, conv_b, lru_ba, lru_bx, lru_lambda, norm_mlp_g, norm_final_g],
                   "pack_small_w")
    small_m = pack(m_lru_wa[0], m_lru_wx[0], zero_cw,
                   [m_norm_mix_g, m_conv_b, m_lru_ba, m_lru_bx, m_lru_lambda, m_norm_mlp_g, m_norm_final_g],
                   "pack_small_m")
    small_v = pack(v_lru_wa[0], v_lru_wx[0], zero_cw,
                   [v_norm_mix_g, v_conv_b, v_lru_ba, v_lru_bx, v_lru_lambda, v_norm_mlp_g, v_norm_final_g],
                   "pack_small_v")
    mid_in = reduce_mid([0], red_in, done + [small_w, small_m, small_v], "w_in")
    gmix_parts = _all_gather8(jnp.pad(d_gmix, ((0, 7), (0, 0))), "gather_gain_grad", [mid_in[0][1][0]])
    gmix_out = _sum8_adamw_row(gmix_parts, norm_mix_g, m_norm_mix_g, v_norm_mix_g, "sum_adamw_norm_mix_g")
    small_sum, _ = reduce_end([0], mid_in, [gmix_out[1]], "w_in")
    small = _adamw(small_w, small_sum, small_m, small_v, "adamw_small")
    g_cw = lax.dynamic_slice(small_sum[2 * mat_rows:2 * mat_rows + CONV_TAPS], (0, chip * u), (CONV_TAPS, u))
    cw_out = _adamw(conv_w[0], g_cw, m_conv_w[0], v_conv_w[0], "adamw_conv_w")

    def small_leaf(kind, name):
        a = small[kind]
        if name == "norm_mix_g":
            return gmix_out[kind]
        if name == "lru_wa":
            return a[0:mat_rows].reshape(lru_wa.shape)
        if name == "lru_wx":
            return a[mat_rows:2 * mat_rows].reshape(lru_wx.shape)
        if name == "conv_w":
            return cw_out[kind][None]
        row = a[2 * mat_rows + CONV_TAPS + vec_names.index(name)]
        return row if name == "norm_final_g" else row[None]

    order = ["norm_mix_g", "w_in", "conv_w", "conv_b", "lru_wa", "lru_ba", "lru_wx", "lru_bx", "lru_lambda",
             "w_proj_attn", "w_proj_lru", "w_out", "norm_mlp_g", "w_up", "w_down", "norm_final_g"]
    outs = [loss, grad_x[None]]
    for kind in range(4):
        for name in order:
            outs.append(big_out[name][kind] if name in big_out else small_leaf(kind, name))
    return tuple(outs)
```

```python
import functools

import jax
import jax.numpy as jnp
from jax import lax
from jax.experimental import pallas as pl
from jax.experimental.pallas import tpu as pltpu

F32 = jnp.float32
BF16 = jnp.bfloat16
MESH = pl.DeviceIdType.MESH
AXES = ("x", "y", "c")

N_CHIPS = 4
N_DEV = 8
HEAD_DIM = 128
ATTN_BLK = 128
DILATIONS = (1, 4, 16)
ATTN_UNROLL = 16
CONV_TAPS = 4
LRU_C = 8.0
EPS = 1e-6
N_SLOTS = 7
N_QKV = 3
VMEM_MIB = 2 ** 20
VMEM_V7X = 64 * VMEM_MIB
STREAM_TILE = 4 * VMEM_MIB
VMEM_STREAM_MIB = 32
VMEM_TILES_MIB = 56

ADAM_LR = 0.001
ADAM_B1 = 0.9
ADAM_B2 = 0.999
ADAM_EPS = 1e-08
ADAM_WD = 0.01
ADAM_STEP = 10

NN = (((1,), (0,)), ((), ()))
NT = (((1,), (1,)), ((), ()))
TN = (((0,), (0,)), ((), ()))


def _params(vmem_mib=None, **kw):
    limit = None if vmem_mib is None else min(vmem_mib * VMEM_MIB, VMEM_V7X - 8 * VMEM_MIB)
    return pltpu.CompilerParams(vmem_limit_bytes=limit, **kw)


def _row_tile(rows, row_bytes, budget=VMEM_MIB):
    t = rows
    while t % 16 == 0 and t * row_bytes > budget:
        t //= 2
    return t


def _dot(a, b, dims):
    return lax.dot_general(a.astype(BF16), b.astype(BF16), dims, preferred_element_type=F32)


def _sigmoid(x):
    return jax.nn.sigmoid(x)


def _rms_fwd(x, g, name):
    s, d = x.shape
    tm = _row_tile(s, d * 4)

    def body(x_ref, g_ref, o_ref):
        xf = x_ref[...]
        r = lax.rsqrt(jnp.mean(xf * xf, axis=-1, keepdims=True) + EPS)
        o_ref[...] = (xf * r * g_ref[...]).astype(o_ref.dtype)

    return pl.pallas_call(
        body, name=name, grid=(s // tm,),
        in_specs=[pl.BlockSpec((tm, d), lambda i: (i, 0)), pl.BlockSpec((1, d), lambda i: (0, 0))],
        out_specs=pl.BlockSpec((tm, d), lambda i: (i, 0)),
        out_shape=jax.ShapeDtypeStruct((s, d), BF16), compiler_params=_params(VMEM_STREAM_MIB),
    )(x, g)


def _rms_bwd(x, g, dy, resid, name):
    s, d = x.shape
    tm = _row_tile(s, d * 4)

    def body(x_ref, g_ref, dy_ref, res_ref, dx_ref, dxb_ref, dg_ref):
        xf = x_ref[...]
        r = lax.rsqrt(jnp.mean(xf * xf, axis=-1, keepdims=True) + EPS)
        xh = xf * r
        dyv = dy_ref[...]
        dxh = dyv * g_ref[...]
        dx = r * (dxh - xh * jnp.mean(dxh * xh, axis=-1, keepdims=True)) + res_ref[...]
        dx_ref[...] = dx
        dxb_ref[...] = dx.astype(BF16)
        part = jnp.sum(dyv * xh, axis=0, keepdims=True)

        @pl.when(pl.program_id(0) == 0)
        def _():
            dg_ref[...] = part

        @pl.when(pl.program_id(0) > 0)
        def _():
            dg_ref[...] += part

    row = pl.BlockSpec((tm, d), lambda i: (i, 0))
    vec = pl.BlockSpec((1, d), lambda i: (0, 0))
    return pl.pallas_call(
        body, name=name, grid=(s // tm,),
        in_specs=[row, vec, row, row], out_specs=[row, row, vec],
        out_shape=[jax.ShapeDtypeStruct((s, d), F32), jax.ShapeDtypeStruct((s, d), BF16),
                   jax.ShapeDtypeStruct((1, d), F32)],
        compiler_params=_params(VMEM_STREAM_MIB),
    )(x, g, dy, resid)


def _loss_head(h2, g, target):
    s, d = h2.shape
    tm = _row_tile(s, d * 4)

    def body(x_ref, g_ref, t_ref, loss_ref, dx_ref, dxb_ref, dg_ref):
        xf = x_ref[...]
        gv = g_ref[...]
        r = lax.rsqrt(jnp.mean(xf * xf, axis=-1, keepdims=True) + EPS)
        xh = xf * r
        err = xh * gv - t_ref[...]
        part = jnp.sum(jnp.sum(err * err, axis=1, keepdims=True), axis=0, keepdims=True) * (0.5 / d)
        dyv = err * (1.0 / d)
        dxh = dyv * gv
        dx = r * (dxh - xh * jnp.mean(dxh * xh, axis=-1, keepdims=True))
        dx_ref[...] = dx
        dxb_ref[...] = dx.astype(BF16)
        dgp = jnp.sum(dyv * xh, axis=0, keepdims=True)

        @pl.when(pl.program_id(0) == 0)
        def _():
            dg_ref[...] = dgp
            loss_ref[...] = jnp.broadcast_to(part, loss_ref.shape)

        @pl.when(pl.program_id(0) > 0)
        def _():
            dg_ref[...] += dgp
            loss_ref[...] += jnp.broadcast_to(part, loss_ref.shape)

    row = pl.BlockSpec((tm, d), lambda i: (i, 0))
    vec = pl.BlockSpec((1, d), lambda i: (0, 0))
    return pl.pallas_call(
        body, name="loss_head", grid=(s // tm,),
        in_specs=[row, vec, row],
        out_specs=[pl.BlockSpec((8, 128), lambda i: (0, 0)), row, row, vec],
        out_shape=[jax.ShapeDtypeStruct((8, 128), F32), jax.ShapeDtypeStruct((s, d), F32),
                   jax.ShapeDtypeStruct((s, d), BF16), jax.ShapeDtypeStruct((1, d), F32)],
        compiler_params=_params(VMEM_STREAM_MIB),
    )(h2, g, target)


def _mm(name, operands, in_specs, out_shape, out_specs, grid, dims, epilogue, nk=1, acc_shape=None,
        aliases=None):
    n_in = len(operands)
    n_out = len(out_shape)

    def body(*refs):
        a_ref, b_ref = refs[0], refs[1]
        extras = refs[2:n_in]
        outs = refs[n_in:n_in + n_out]

        def prod():
            return _dot(a_ref[...], b_ref[...], dims)

        if nk == 1:
            epilogue(prod(), extras, outs)
        else:
            acc = refs[n_in + n_out]
            k = pl.program_id(2)

            @pl.when(k == 0)
            def _():
                acc[...] = prod()

            @pl.when(k > 0)
            def _():
                acc[...] += prod()

            @pl.when(k == nk - 1)
            def _():
                epilogue(acc[...], extras, outs)

    scratch = [] if nk == 1 else [pltpu.VMEM(acc_shape, F32)]
    return pl.pallas_call(
        body, name=name, grid=grid, in_specs=in_specs, out_specs=out_specs, out_shape=out_shape,
        scratch_shapes=scratch, input_output_aliases=aliases or {},
        compiler_params=_params(VMEM_TILES_MIB),
    )(*operands)


def _store(acc, extras, outs):
    outs[0][...] = acc.astype(outs[0].dtype)


def _proj_in_shard(xn, w_in_g, shard_arr, prev, name):
    s, d = xn.shape
    u = d // 4
    per = N_SLOTS
    n_prev = 0 if prev is None else 1

    def body(sh_ref, x_ref, w_ref, *rest):
        del sh_ref
        rest[n_prev][...] = _dot(x_ref[...], w_ref[...], NN)

    def out_map(j, sh_ref):
        unit = per * sh_ref[0] + j
        return (unit // 4, 0, unit % 4)

    return pl.pallas_call(
        body, name=name,
        grid_spec=pltpu.PrefetchScalarGridSpec(
            num_scalar_prefetch=1, grid=(per,),
            in_specs=[pl.BlockSpec((s, d), lambda j, sh_ref: (0, 0)),
                      pl.BlockSpec((None, d, u), lambda j, sh_ref: (sh_ref[0], 0, j))] + [HBM] * n_prev,
            out_specs=pl.BlockSpec((None, s, u), out_map)),
        out_shape=jax.ShapeDtypeStruct((N_SLOTS, s, d), F32),
        input_output_aliases={3: 0} if n_prev else {},
        compiler_params=_params(VMEM_TILES_MIB),
    )(shard_arr, xn, w_in_g, *([] if prev is None else [prev]))


def _mm_nn(name, a, b, extras, extra_specs, out_shape, out_specs, epilogue, tn, aliases=None):
    s, kdim = a.shape
    n = b.shape[1]
    return _mm(
        name, [a, b] + list(extras),
        [pl.BlockSpec((s, kdim), lambda i, j, k: (0, 0)), pl.BlockSpec((kdim, tn), lambda i, j, k: (0, j))]
        + list(extra_specs),
        out_shape, out_specs, (1, n // tn, 1), NN, epilogue, aliases=aliases)


def _mm_nt(name, a, b, extras, extra_specs, out_shape, out_specs, epilogue, tn, aliases=None):
    s, kdim = a.shape
    n = b.shape[0]
    return _mm(
        name, [a, b] + list(extras),
        [pl.BlockSpec((s, kdim), lambda i, j, k: (0, 0)), pl.BlockSpec((tn, kdim), lambda i, j, k: (j, 0))]
        + list(extra_specs),
        out_shape, out_specs, (1, n // tn, 1), NT, epilogue, aliases=aliases)


def _mm_tn(name, a, b, a_spec, b_spec, out_shape, out_spec, grid, m, tn, s, aliases=None, extra=None):
    ch = 256
    n_in = 2 if extra is None else 3

    def body(*refs):
        a_ref, b_ref = refs[0], refs[1]
        o_ref, at_ref = refs[n_in], refs[n_in + 1]

        @pl.when(pl.program_id(1) == 0)
        def _():
            for c0 in range(0, s, ch):
                at_ref[:, c0:c0 + ch] = a_ref[c0:c0 + ch, :].astype(F32).T.astype(BF16)

        o_ref[...] = _dot(at_ref[...], b_ref[...], NN).astype(o_ref.dtype)

    operands = [a, b] + ([] if extra is None else [extra])
    in_specs = [a_spec, b_spec] + ([] if extra is None else [pl.BlockSpec(memory_space=pl.ANY)])
    return pl.pallas_call(
        body, name=name, grid=grid, in_specs=in_specs, out_specs=out_spec, out_shape=out_shape,
        scratch_shapes=[pltpu.VMEM((m, s), BF16)], input_output_aliases=aliases or {},
        compiler_params=_params(VMEM_TILES_MIB),
    )(*operands)


def _dxn(dproj_a, dproj_b, w_in_g, tn, after):
    n_a, s, d = dproj_a.shape
    u = d // 4
    ua = 4 * n_a
    nk = 4 * N_SLOTS
    per = N_SLOTS

    def body(a_ref, b_ref, w_ref, *rest):
        o_ref = rest[len(after)]
        k = pl.program_id(2)

        @pl.when(k == 0)
        def _():
            o_ref[...] = jnp.zeros_like(o_ref)

        @pl.when(k < ua)
        def _():
            o_ref[...] += _dot(a_ref[...], w_ref[...], NT)

        @pl.when(k >= ua)
        def _():
            o_ref[...] += _dot(b_ref[...], w_ref[...], NT)

    def a_map(i, j, k):
        kk = jnp.minimum(k, ua - 1)
        return (kk // 4, 0, kk % 4)

    def b_map(i, j, k):
        kk = jnp.maximum(k - ua, 0)
        return (kk // 4, 0, kk % 4)

    return pl.pallas_call(
        body, name="dxn", grid=(1, d // tn, nk),
        in_specs=[pl.BlockSpec((None, s, u), a_map), pl.BlockSpec((None, s, u), b_map),
                  pl.BlockSpec((None, tn, u), lambda i, j, k: (k // per, j, k % per))] + [HBM] * len(after),
        out_specs=pl.BlockSpec((s, tn), lambda i, j, k: (0, j)),
        out_shape=jax.ShapeDtypeStruct((s, d), F32),
        compiler_params=_params(VMEM_TILES_MIB),
    )(dproj_a, dproj_b, w_in_g, *after)


def _attn_masks(slope, dil):
    ii = lax.broadcasted_iota(jnp.int32, (ATTN_BLK, 2 * ATTN_BLK), 0)
    jj = lax.broadcasted_iota(jnp.int32, (ATTN_BLK, 2 * ATTN_BLK), 1)
    diff = ATTN_BLK + ii - jj
    band = (diff >= 0) & (diff <= ATTN_BLK)
    bias = -(slope * float(dil)) * diff.astype(F32)
    return band, bias, jj


def _attn_window(t, nblk):
    cur = pl.ds(pl.multiple_of(t * ATTN_BLK, ATTN_BLK), ATTN_BLK)
    prev = pl.ds(pl.multiple_of(jnp.maximum(t - 1, 0) * ATTN_BLK, ATTN_BLK), ATTN_BLK)
    first = jnp.where(t % nblk == 0, ATTN_BLK, 0)
    return prev, cur, first


def _unrolled_loop(n, step, init, unroll=ATTN_UNROLL):
    def trip(i, carry):
        for k in range(unroll):
            carry = step(i * unroll + k, carry)
        return carry

    return lax.fori_loop(0, n // unroll, trip, init)


def _streams(pairs, dil, s):
    if dil == 1:
        return [src for _, src in pairs]
    seg = s // dil
    for dst, src in pairs:
        for r in range(dil):
            dst[r * seg:(r + 1) * seg, :] = src[pl.ds(r, seg, stride=dil), :].astype(dst.dtype)
    return [dst for dst, _ in pairs]


def _attn_fwd(proj_a, slopes):
    _, s, d = proj_a.shape
    heads = d // HEAD_DIM
    scale = HEAD_DIM ** -0.5
    n_t = s // ATTN_BLK
    ng = len(DILATIONS)

    def body(q_ref, k_ref, v_ref, sl_ref, o_ref, lse_ref, qd, kd, vd, od, ld, og, lg):
        slope = sl_ref[...][:, :1]
        for g, dil in enumerate(DILATIONS):
            nblk = s // dil // ATTN_BLK
            qs, ks, vs = _streams([(qd, q_ref), (kd, k_ref), (vd, v_ref)], dil, s)
            o_t, l_t = (og.at[g], lg.at[g]) if dil == 1 else (od, ld)
            band, bias, jj = _attn_masks(slope, dil)

            def blk(t, carry, nblk=nblk, band=band, bias=bias, jj=jj, qs=qs, ks=ks, vs=vs, o_t=o_t, l_t=l_t):
                prev, cur, first = _attn_window(t, nblk)
                kk = jnp.concatenate([ks[prev, :], ks[cur, :]], axis=0)
                vv = jnp.concatenate([vs[prev, :], vs[cur, :]], axis=0)
                sc = _dot(qs[cur, :], kk, NT) * scale + bias
                sc = jnp.where(band & (jj >= first), sc, -jnp.inf)
                m = jnp.max(sc, axis=1, keepdims=True)
                p = jnp.exp(sc - m)
                l = jnp.sum(p, axis=1, keepdims=True)
                o_t[cur, :] = _dot(p, vv, NN) / l
                l_t[cur, :] = jnp.broadcast_to(m + jnp.log(l), (ATTN_BLK, HEAD_DIM))
                return carry

            _unrolled_loop(n_t, blk, 0)
            seg = s // dil
            if dil > 1:
                for r in range(dil):
                    og[g, pl.ds(r, seg, stride=dil), :] = od[r * seg:(r + 1) * seg, :]
                    lg[g, pl.ds(r, seg, stride=dil), :] = ld[r * seg:(r + 1) * seg, :]

        ch = 256

        def combine(c, carry):
            rows = pl.ds(pl.multiple_of(c * ch, ch), ch)
            ls = [lg[g, rows, :] for g in range(ng)]
            mx = functools.reduce(jnp.maximum, ls)
            es = [jnp.exp(x - mx) for x in ls]
            den = functools.reduce(jnp.add, es)
            num = functools.reduce(jnp.add, [es[g] * og[g, rows, :] for g in range(ng)])
            o_ref[rows, :] = (num / den).astype(o_ref.dtype)
            lse_ref[rows, :] = mx + jnp.log(den)
            return carry

        lax.fori_loop(0, s // ch, combine, 0)

    def col(slot):
        return pl.BlockSpec((None, s, HEAD_DIM), lambda h: (slot, 0, h))

    head = pl.BlockSpec((s, HEAD_DIM), lambda h: (0, h))
    return pl.pallas_call(
        body, name="attn_fwd", grid=(heads,),
        in_specs=[col(0), col(1), col(2), pl.BlockSpec((None, 1, HEAD_DIM), lambda h: (h, 0, 0))],
        out_specs=[head, head],
        out_shape=[jax.ShapeDtypeStruct((s, d), BF16), jax.ShapeDtypeStruct((s, d), F32)],
        scratch_shapes=[pltpu.VMEM((s, HEAD_DIM), BF16)] * 3 + [pltpu.VMEM((s, HEAD_DIM), F32)] * 2
        + [pltpu.VMEM((ng, s, HEAD_DIM), F32)] * 2,
        compiler_params=_params(VMEM_TILES_MIB),
    )(proj_a, proj_a, proj_a, slopes)


def _attn_bwd(proj_a, slopes, y_attn, lse, dy):
    _, s, d = proj_a.shape
    heads = d // HEAD_DIM
    scale = HEAD_DIM ** -0.5
    n_t = s // ATTN_BLK

    def body(q_ref, k_ref, v_ref, sl_ref, o_ref, lse_ref, dy_ref, out_ref,
             qd, kd, vd, dod, lsd, dld, delta, dqd, dkd, dvd, dqa, dka, dva):
        slope = sl_ref[...][:, :1]
        dyv = dy_ref[...]
        delta[...] = jnp.broadcast_to(
            jnp.sum(dyv * o_ref[...].astype(F32), axis=1, keepdims=True), (s, HEAD_DIM))
        for g, dil in enumerate(DILATIONS):
            nblk = s // dil // ATTN_BLK
            seg = s // dil
            qs, ks, vs, dos, lss, dls = _streams(
                [(qd, q_ref), (kd, k_ref), (vd, v_ref), (dod, dy_ref), (lsd, lse_ref), (dld, delta)], dil, s)
            dq_t, dk_t, dv_t = (dqa, dka, dva) if dil == 1 else (dqd, dkd, dvd)
            band, bias, jj = _attn_masks(slope, dil)

            def blk(t, carry, nblk=nblk, band=band, bias=bias, jj=jj, qs=qs, ks=ks, vs=vs, dos=dos, lss=lss,
                    dls=dls, dq_t=dq_t, dk_t=dk_t, dv_t=dv_t):
                ck, cv = carry
                prev, cur, first = _attn_window(t, nblk)
                q = qs[cur, :]
                do = dos[cur, :]
                lse_b = lss[cur, :]
                dl_b = dls[cur, :]
                kk = jnp.concatenate([ks[prev, :], ks[cur, :]], axis=0)
                vv = jnp.concatenate([vs[prev, :], vs[cur, :]], axis=0)
                sc = _dot(q, kk, NT) * scale + bias
                p = jnp.where(band & (jj >= first), jnp.exp(sc - jnp.concatenate([lse_b, lse_b], axis=1)), 0.0)
                dp = _dot(do, vv, NT)
                ds = p * (dp - jnp.concatenate([dl_b, dl_b], axis=1))
                dv_b = _dot(p, do, TN)
                dk_b = _dot(ds, q, TN) * scale
                dq_t[cur, :] = _dot(ds, kk, NN) * scale
                done = pl.ds(pl.multiple_of(jnp.where(t == 0, n_t, t - 1) * ATTN_BLK, ATTN_BLK), ATTN_BLK)
                dk_t[done, :] = ck + dk_b[:ATTN_BLK]
                dv_t[done, :] = cv + dv_b[:ATTN_BLK]
                return dk_b[ATTN_BLK:], dv_b[ATTN_BLK:]

            zero = jnp.zeros((ATTN_BLK, HEAD_DIM), F32)
            ck, cv = _unrolled_loop(n_t, blk, (zero, zero))
            dk_t[(n_t - 1) * ATTN_BLK:n_t * ATTN_BLK, :] = ck
            dv_t[(n_t - 1) * ATTN_BLK:n_t * ATTN_BLK, :] = cv
            if dil > 1:
                for acc, part in ((dqa, dqd), (dka, dkd), (dva, dvd)):
                    for r in range(dil):
                        acc[pl.ds(r, seg, stride=dil), :] += part[r * seg:(r + 1) * seg, :]
        out_ref[0] = dqa[...].astype(out_ref.dtype)
        out_ref[1] = dka[0:s, :].astype(out_ref.dtype)
        out_ref[2] = dva[0:s, :].astype(out_ref.dtype)

    def col(slot):
        return pl.BlockSpec((None, s, HEAD_DIM), lambda h: (slot, 0, h))

    head = pl.BlockSpec((s, HEAD_DIM), lambda h: (0, h))
    return pl.pallas_call(
        body, name="attn_bwd", grid=(heads,),
        in_specs=[col(0), col(1), col(2), pl.BlockSpec((None, 1, HEAD_DIM), lambda h: (h, 0, 0)),
                  head, head, head],
        out_specs=pl.BlockSpec((N_QKV, s, HEAD_DIM), lambda h: (0, 0, h)),
        out_shape=jax.ShapeDtypeStruct((N_QKV, s, d), BF16),
        scratch_shapes=[pltpu.VMEM((s, HEAD_DIM), BF16)] * 4 + [pltpu.VMEM((s, HEAD_DIM), F32)] * 4
        + [pltpu.VMEM((s + ATTN_BLK, HEAD_DIM), F32)] * 2 + [pltpu.VMEM((s, HEAD_DIM), F32)]
        + [pltpu.VMEM((s + ATTN_BLK, HEAD_DIM), F32)] * 2,
        compiler_params=_params(VMEM_TILES_MIB),
    )(proj_a, proj_a, proj_a, slopes, y_attn, lse, dy)


def _expm1(x):
    small = x * (1.0 + x * (0.5 + x * (1.0 / 6.0 + x * (1.0 / 24.0 + x * (1.0 / 120.0)))))
    return jnp.where(jnp.abs(x) < 0.1, small, jnp.exp(x) - 1.0)


def _softplus(x):
    return jnp.maximum(x, 0.0) + jnp.log1p(jnp.exp(-jnp.abs(x)))


GELU_K = 0.7978845608028654
GELU_C = 0.044715


def _gelu(x):
    t = jnp.tanh(GELU_K * (x + GELU_C * x * x * x))
    return 0.5 * x * (1.0 + t), t


def _gelu_grad(x, t):
    return 0.5 * (1.0 + t) + 0.5 * x * (1.0 - t * t) * GELU_K * (1.0 + 3.0 * GELU_C * x * x)


def _lru_gates(xc, wa, ba, wx, bx, sp):
    r = _sigmoid(_dot(xc, wa, NN) + ba)
    ig = _sigmoid(_dot(xc, wx, NN) + bx)
    log_a = -LRU_C * r * sp
    a = jnp.exp(log_a)
    mult = jnp.sqrt(-_expm1(2.0 * log_a))
    return r, ig, a, mult


def _scan_fwd(a, u, tt):
    row = lax.broadcasted_iota(jnp.int32, a.shape, 0)
    sh = 1
    while sh < tt:
        keep = row >= sh
        a_s = jnp.where(keep, pltpu.roll(a, sh, 0), 1.0)
        u_s = jnp.where(keep, pltpu.roll(u, sh, 0), 0.0)
        u = a * u_s + u
        a = a * a_s
        sh *= 2
    return a, u


def _scan_bwd(b, g, tt):
    row = lax.broadcasted_iota(jnp.int32, b.shape, 0)
    sh = 1
    while sh < tt:
        keep = row < tt - sh
        b_s = jnp.where(keep, pltpu.roll(b, tt - sh, 0), 1.0)
        g_s = jnp.where(keep, pltpu.roll(g, tt - sh, 0), 0.0)
        g = g + b * g_s
        b = b * b_s
        sh *= 2
    return b, g


ROW_CHUNK = 256


def _pad_copy(xpad_ref, x_ref, s):
    xpad_ref[0:8, :] = jnp.zeros((8, HEAD_DIM), F32)
    for c0 in range(0, s, ROW_CHUNK):
        xpad_ref[8 + c0:8 + c0 + ROW_CHUNK, :] = x_ref[c0:c0 + ROW_CHUNK, :]


def _conv_rows(dst_ref, xpad_ref, cw, cb, s):
    for c0 in range(0, s, ROW_CHUNK):
        acc = cb
        for j in range(CONV_TAPS):
            off = 8 - (CONV_TAPS - 1) + j + c0
            acc = acc + cw[j:j + 1, :] * xpad_ref[off:off + ROW_CHUNK, :]
        dst_ref[c0:c0 + ROW_CHUNK, :] = acc


LRU_TILE = 128
LRU_UNROLL = 16


def _lru_specs(s, d):
    heads = d // HEAD_DIM

    def col(slot):
        return pl.BlockSpec((None, s, HEAD_DIM), lambda h: (slot, 0, h))

    vec = pl.BlockSpec((1, HEAD_DIM), lambda h: (0, h))
    mat = pl.BlockSpec((None, HEAD_DIM, HEAD_DIM), lambda h: (h, 0, 0))
    cw = pl.BlockSpec((8, HEAD_DIM), lambda h: (0, h))
    head = pl.BlockSpec((s, HEAD_DIM), lambda h: (0, h))
    return heads, col, vec, mat, cw, head


def _lru_fwd(proj_b, conv_w, conv_b, wa, ba, wx, bx, lam):
    _, s, d = proj_b.shape
    heads, col, vec, mat, cws, head = _lru_specs(s, d)
    tt = LRU_TILE

    def body(xr_ref, xg_ref, cw_ref, cb_ref, wa_ref, ba_ref, wx_ref, bx_ref, lam_ref, y_ref, h_ref, xpad, xc_s):
        _pad_copy(xpad, xr_ref, s)
        _conv_rows(xc_s, xpad, cw_ref[...], cb_ref[...], s)
        sp = _softplus(-lam_ref[...])
        wav, wxv, bav, bxv = wa_ref[...], wx_ref[...], ba_ref[...], bx_ref[...]

        def tile(i, hc):
            rows = pl.ds(pl.multiple_of(i * tt, tt), tt)
            xc = xc_s[rows, :]
            _, ig, a, mult = _lru_gates(xc, wav, bav, wxv, bxv, sp)
            pa, hl = _scan_fwd(a, mult * (ig * xc), tt)
            h = hl + pa * hc
            h_ref[rows, :] = h
            gel, _ = _gelu(xg_ref[rows, :])
            y_ref[rows, :] = (h * gel).astype(y_ref.dtype)
            return h[tt - 1:tt, :]

        _unrolled_loop(s // tt, tile, jnp.zeros((1, HEAD_DIM), F32), LRU_UNROLL)

    return pl.pallas_call(
        body, name="lru_fwd", grid=(heads,),
        in_specs=[col(N_QKV), col(N_QKV + 1), cws, vec, mat, vec, mat, vec, vec],
        out_specs=[head, head],
        out_shape=[jax.ShapeDtypeStruct((s, d), BF16), jax.ShapeDtypeStruct((s, d), F32)],
        scratch_shapes=[pltpu.VMEM((s + 8, HEAD_DIM), F32), pltpu.VMEM((s, HEAD_DIM), F32)],
        compiler_params=_params(VMEM_STREAM_MIB),
    )(proj_b, proj_b, conv_w, conv_b, wa, ba, wx, bx, lam)


def _lru_bwd(proj_b, h_lru, dy, conv_w, conv_b, wa, ba, wx, bx, lam, dproj_b):
    _, s, d = proj_b.shape
    heads, col, vec, mat, cws, head = _lru_specs(s, d)
    tt = LRU_TILE
    n_t = s // tt

    def body(xr_ref, xg_ref, h_ref, dy_ref, cw_ref, cb_ref, wa_ref, ba_ref, wx_ref, bx_ref, lam_ref, alias_ref,
             out_ref, dcw_ref, dcb_ref, dwa_ref, dba_ref, dwx_ref, dbx_ref, dlam_ref, xpad, xc_s, dxc_s):
        del alias_ref
        _pad_copy(xpad, xr_ref, s)
        cwv = cw_ref[...]
        _conv_rows(xc_s, xpad, cwv, cb_ref[...], s)
        dxc_s[s:s + 8, :] = jnp.zeros((8, HEAD_DIM), F32)
        lamv = lam_ref[...]
        sp = _softplus(-lamv)
        wav, wxv, bav, bxv = wa_ref[...], wx_ref[...], ba_ref[...], bx_ref[...]
        dwa_ref[...] = jnp.zeros_like(dwa_ref)
        dwx_ref[...] = jnp.zeros_like(dwx_ref)
        zero = jnp.zeros((1, HEAD_DIM), F32)
        row = lax.broadcasted_iota(jnp.int32, (tt, HEAD_DIM), 0)

        def tile(it, carry):
            dh_next, a_next, dba, dbx, dsp, dcb = carry
            i = n_t - 1 - it
            t0 = pl.multiple_of(i * tt, tt)
            rows = pl.ds(t0, tt)
            xc = xc_s[rows, :]
            r, ig, a, mult = _lru_gates(xc, wav, bav, wxv, bxv, sp)
            h = h_ref[rows, :]
            before = h_ref[pl.ds(pl.multiple_of(jnp.maximum(t0 - 8, 0), 8), 8), :][7:8, :]
            before = before * (i > 0).astype(F32)
            h_prev = jnp.where(row == 0, before, pltpu.roll(h, 1, 0))
            xg = xg_ref[rows, :]
            dyv = dy_ref[rows, :]
            gel, th = _gelu(xg)
            out_ref[1, rows, :] = (dyv * h * _gelu_grad(xg, th)).astype(out_ref.dtype)
            b = jnp.where(row == tt - 1, a_next, pltpu.roll(a, tt - 1, 0))
            pb, z = _scan_bwd(b, dyv * gel, tt)
            dh = z + pb * dh_next
            da = dh * h_prev
            dmult = dh * (ig * xc)
            dig = dh * (mult * xc)
            dla = da * a - dmult * (a * a / mult)
            dzr = dla * (-LRU_C * sp) * (r * (1.0 - r))
            dzx = dig * (ig * (1.0 - ig))
            dxc = dh * (mult * ig) + _dot(dzr, wav, NT) + _dot(dzx, wxv, NT)
            dxc_s[rows, :] = dxc
            dwa_ref[...] += _dot(xc, dzr, TN)
            dwx_ref[...] += _dot(xc, dzx, TN)
            return (dh[0:1, :], a[0:1, :],
                    dba + jnp.sum(dzr, axis=0, keepdims=True),
                    dbx + jnp.sum(dzx, axis=0, keepdims=True),
                    dsp + jnp.sum(dla * (-LRU_C * r), axis=0, keepdims=True),
                    dcb + jnp.sum(dxc, axis=0, keepdims=True))

        _, _, dba, dbx, dsp, dcb = _unrolled_loop(n_t, tile, (zero, zero, zero, zero, zero, zero), LRU_UNROLL)
        dba_ref[...] = dba
        dbx_ref[...] = dbx
        dcb_ref[...] = dcb
        dlam_ref[...] = -dsp * _sigmoid(-lamv)
        dcw = [zero] * CONV_TAPS
        for c0 in range(0, s, ROW_CHUNK):
            dxc_c = dxc_s[c0:c0 + ROW_CHUNK, :]
            dxr = jnp.zeros((ROW_CHUNK, HEAD_DIM), F32)
            for j in range(CONV_TAPS):
                back = CONV_TAPS - 1 - j
                off = 8 - back + c0
                dcw[j] = dcw[j] + jnp.sum(dxc_c * xpad[off:off + ROW_CHUNK, :], axis=0, keepdims=True)
                dxr = dxr + cwv[j:j + 1, :] * dxc_s[back + c0:back + c0 + ROW_CHUNK, :]
            out_ref[0, c0:c0 + ROW_CHUNK, :] = dxr.astype(out_ref.dtype)
        for j in range(CONV_TAPS):
            dcw_ref[j:j + 1, :] = dcw[j]

    return pl.pallas_call(
        body, name="lru_bwd", grid=(heads,),
        in_specs=[col(N_QKV), col(N_QKV + 1), head, head, cws, vec, mat, vec, mat, vec, vec,
                  pl.BlockSpec(memory_space=pl.ANY)],
        out_specs=[pl.BlockSpec((2, s, HEAD_DIM), lambda h: (0, 0, h)),
                   pl.BlockSpec((CONV_TAPS, HEAD_DIM), lambda h: (0, h)), vec, mat, vec, mat, vec, vec],
        out_shape=[jax.ShapeDtypeStruct(dproj_b.shape, dproj_b.dtype),
                   jax.ShapeDtypeStruct((CONV_TAPS, d), F32), jax.ShapeDtypeStruct((1, d), F32),
                   jax.ShapeDtypeStruct(wa.shape, F32), jax.ShapeDtypeStruct((1, d), F32),
                   jax.ShapeDtypeStruct(wx.shape, F32), jax.ShapeDtypeStruct((1, d), F32),
                   jax.ShapeDtypeStruct((1, d), F32)],
        scratch_shapes=[pltpu.VMEM((s + 8, HEAD_DIM), F32), pltpu.VMEM((s, HEAD_DIM), F32),
                        pltpu.VMEM((s + 8, HEAD_DIM), F32)],
        input_output_aliases={11: 0},
        compiler_params=_params(VMEM_STREAM_MIB),
    )(proj_b, proj_b, h_lru, dy, conv_w, conv_b, wa, ba, wx, bx, lam, dproj_b)


def _place():
    x, y, c = (lax.axis_index(n) for n in AXES)
    return x, y, c


def _other_chips(x, y):
    return [(1 - x, y), (x, 1 - y), (1 - x, 1 - y)]


HBM = pl.BlockSpec(memory_space=pl.ANY)


def _cast_shard(w, chip_arr, name):
    r, cols = w.shape
    rh = r // 2
    tr = _row_tile(rh, cols * 4, STREAM_TILE)
    nt = rh // tr

    def body(chip_ref, w_ref, o_ref):
        del chip_ref
        o_ref[...] = w_ref[...].astype(BF16)

    return pl.pallas_call(
        body, name=name,
        grid_spec=pltpu.PrefetchScalarGridSpec(
            num_scalar_prefetch=1, grid=(2, nt),
            in_specs=[pl.BlockSpec((tr, cols), lambda h, i, chip_ref: (h * nt + i, 0))],
            out_specs=pl.BlockSpec((None, None, tr, cols), lambda h, i, chip_ref: (chip_ref[0], h, i, 0))),
        out_shape=jax.ShapeDtypeStruct((N_CHIPS, 2, rh, cols), BF16), compiler_params=_params(VMEM_STREAM_MIB),
    )(chip_arr, w)


HBM_SPEC = pl.BlockSpec(memory_space=pltpu.HBM)
SEM_SPEC = pl.BlockSpec(memory_space=pltpu.SEMAPHORE)
EFFECT = pltpu.SideEffectType.DATAFLOW_SIDE_EFFECTING
TOKEN = jax.ShapeDtypeStruct((8, 128), F32)
TOKEN_SPEC = pl.BlockSpec(memory_space=pltpu.VMEM)


def _in_hbm(arrays):
    return [pltpu.with_memory_space_constraint(a, pltpu.HBM) for a in arrays]


def _hbm_like(arrays):
    return [pltpu.HBM(a.shape, a.dtype) for a in arrays]


def _sems(n):
    return pltpu.SemaphoreType.DMA((n,))


def _remote(src, dst, send_sem, recv_sem, to):
    return pltpu.make_async_remote_copy(src_ref=src, dst_ref=dst, send_sem=send_sem, recv_sem=recv_sem,
                                        device_id=to, device_id_type=MESH)


ALL_FLIPS = (0, 1, 2)


def _gather_start(bufs, groups, after, name):
    n = len(bufs)
    ng = len(groups)

    def body(*refs):
        ins = refs[:n]
        sems = refs[n + len(after):n + len(after) + 2 * ng]
        x, y, c = _place()
        me = 2 * x + y
        chips = _other_chips(x, y)
        for g, (ws, flips) in enumerate(groups):
            for i, w in enumerate(ws):
                for jj, j in enumerate(flips):
                    k = len(flips) * i + jj
                    mine = ins[w].at[me, c]
                    _remote(mine, mine, sems[2 * g].at[k], sems[2 * g + 1].at[k], (*chips[j], c)).start()

    sem_shapes = []
    for ws, flips in groups:
        sem_shapes += [_sems(len(flips) * len(ws))] * 2
    res = pl.pallas_call(
        body, name=name, in_specs=[HBM_SPEC] * n + [HBM] * len(after),
        out_specs=[SEM_SPEC] * (2 * ng) + [HBM_SPEC] * n, out_shape=sem_shapes + _hbm_like(bufs),
        input_output_aliases={w: 2 * ng + w for w in range(n)},
        compiler_params=pltpu.CompilerParams(has_side_effects=EFFECT),
    )(*_in_hbm(bufs), *after)
    return [(res[2 * g], res[2 * g + 1]) for g in range(ng)], list(res[2 * ng:])


def _gather_forward(bufs, recv, after, name, flips=ALL_FLIPS):
    m = len(bufs)
    nf = len(flips)

    def body(*refs):
        ins, recv_in = refs[:m], refs[m]
        fsend, frecv = refs[m + 1 + len(after)], refs[m + 2 + len(after)]
        x, y, c = _place()
        chips = _other_chips(x, y)
        for jj, j in enumerate(flips):
            cx, cy = chips[j]
            for i in range(m):
                landed = ins[i].at[2 * cx + cy, c]
                k = nf * i + jj
                _remote(landed, landed, fsend.at[k], recv_in.at[k], (cx, cy, c)).wait_recv()
                _remote(landed, landed, fsend.at[k], frecv.at[k], (x, y, 1 - c)).start()

    res = pl.pallas_call(
        body, name=name, in_specs=[HBM_SPEC] * m + [SEM_SPEC] + [HBM] * len(after),
        out_specs=[SEM_SPEC, SEM_SPEC] + [HBM_SPEC] * m, out_shape=[_sems(nf * m), _sems(nf * m)] + _hbm_like(bufs),
        input_output_aliases={i: 2 + i for i in range(m)},
        compiler_params=pltpu.CompilerParams(has_side_effects=EFFECT),
    )(*bufs, recv, *after)
    return (res[0], res[1]), list(res[2:])


NEIGHBOURS = (0, 1)


def _relay_partner(x, y, c):
    return 2 * (x ^ (1 - c)) + (y ^ c), (x ^ c, y ^ (1 - c))


def _gather_forward_relay(bufs, recv, after, name):
    m = len(bufs)
    nf = len(NEIGHBOURS)

    def body(*refs):
        ins, recv_in = refs[:m], refs[m]
        fsend, frecv, rsend, rrecv = refs[m + 1 + len(after):m + 5 + len(after)]
        x, y, c = _place()
        chips = _other_chips(x, y)
        for jj, j in enumerate(NEIGHBOURS):
            cx, cy = chips[j]
            for i in range(m):
                landed = ins[i].at[2 * cx + cy, c]
                k = nf * i + jj
                _remote(landed, landed, fsend.at[k], recv_in.at[k], (cx, cy, c)).wait_recv()
        row, (px, py) = _relay_partner(x, y, c)
        for i in range(m):
            relayed = ins[i].at[row, c]
            _remote(relayed, relayed, rsend.at[i], rrecv.at[i], (px, py, c)).start()
        for jj, j in enumerate(NEIGHBOURS):
            cx, cy = chips[j]
            for i in range(m):
                landed = ins[i].at[2 * cx + cy, c]
                k = nf * i + jj
                _remote(landed, landed, fsend.at[k], frecv.at[k], (x, y, 1 - c)).start()

    res = pl.pallas_call(
        body, name=name, in_specs=[HBM_SPEC] * m + [SEM_SPEC] + [HBM] * len(after),
        out_specs=[SEM_SPEC] * 4 + [HBM_SPEC] * m,
        out_shape=[_sems(nf * m), _sems(nf * m), _sems(m), _sems(m)] + _hbm_like(bufs),
        input_output_aliases={i: 4 + i for i in range(m)},
        compiler_params=pltpu.CompilerParams(has_side_effects=EFFECT),
    )(*bufs, recv, *after)
    return tuple(res[:4]), list(res[4:])


def _gather_forward_diag(bufs, rrecv, after, name):
    m = len(bufs)

    def body(*refs):
        ins, rrecv_in = refs[:m], refs[m]
        dsend, drecv = refs[m + 1 + len(after)], refs[m + 2 + len(after)]
        x, y, c = _place()
        diag = (2 * x + y) ^ 3
        _, (px, py) = _relay_partner(x, y, c)
        for i in range(m):
            landed = ins[i].at[diag, c]
            _remote(landed, landed, dsend.at[i], rrecv_in.at[i], (px, py, c)).wait_recv()
            _remote(landed, landed, dsend.at[i], drecv.at[i], (x, y, 1 - c)).start()

    res = pl.pallas_call(
        body, name=name, in_specs=[HBM_SPEC] * m + [SEM_SPEC] + [HBM] * len(after),
        out_specs=[SEM_SPEC, SEM_SPEC] + [HBM_SPEC] * m, out_shape=[_sems(m), _sems(m)] + _hbm_like(bufs),
        input_output_aliases={i: 2 + i for i in range(m)},
        compiler_params=pltpu.CompilerParams(has_side_effects=EFFECT),
    )(*bufs, rrecv, *after)
    return (res[0], res[1]), list(res[2:])


def _gather_finish_diag(bufs, rsend, dsend, drecv, after, name):
    m = len(bufs)

    def body(*refs):
        ins = refs[:m]
        rsend_in, dsend_in, drecv_in = refs[m:m + 3]
        x, y, c = _place()
        diag = (2 * x + y) ^ 3
        row, (px, py) = _relay_partner(x, y, c)
        for i in range(m):
            relayed = ins[i].at[row, c]
            _remote(relayed, relayed, rsend_in.at[i], drecv_in.at[i], (px, py, c)).wait_send()
            landed = ins[i].at[diag, c]
            _remote(landed, landed, dsend_in.at[i], drecv_in.at[i], (x, y, 1 - c)).wait_send()
            theirs = ins[i].at[diag, 1 - c]
            _remote(theirs, theirs, dsend_in.at[i], drecv_in.at[i], (x, y, 1 - c)).wait_recv()

    return list(pl.pallas_call(
        body, name=name, in_specs=[HBM_SPEC] * m + [SEM_SPEC] * 3 + [HBM] * len(after),
        out_specs=[HBM_SPEC] * m, out_shape=_hbm_like(bufs),
        input_output_aliases={i: i for i in range(m)},
        compiler_params=pltpu.CompilerParams(has_side_effects=EFFECT),
    )(*bufs, rsend, dsend, drecv, *after))


def _gather_finish(bufs, send, fsend, frecv, after, name, flips=ALL_FLIPS):
    m = len(bufs)
    nf = len(flips)

    def body(*refs):
        ins = refs[:m]
        send_in, fsend_in, frecv_in = refs[m:m + 3]
        x, y, c = _place()
        me = 2 * x + y
        chips = _other_chips(x, y)
        for jj, j in enumerate(flips):
            cx, cy = chips[j]
            cj = 2 * cx + cy
            for i in range(m):
                k = nf * i + jj
                mine = ins[i].at[me, c]
                _remote(mine, mine, send_in.at[k], frecv_in.at[k], (cx, cy, c)).wait_send()
                landed = ins[i].at[cj, c]
                _remote(landed, landed, fsend_in.at[k], frecv_in.at[k], (x, y, 1 - c)).wait_send()
                theirs = ins[i].at[cj, 1 - c]
                _remote(theirs, theirs, fsend_in.at[k], frecv_in.at[k], (x, y, 1 - c)).wait_recv()

    return list(pl.pallas_call(
        body, name=name, in_specs=[HBM_SPEC] * m + [SEM_SPEC] * 3 + [HBM] * len(after),
        out_specs=[HBM_SPEC] * m, out_shape=_hbm_like(bufs),
        input_output_aliases={i: i for i in range(m)},
        compiler_params=pltpu.CompilerParams(has_side_effects=EFFECT),
    )(*bufs, send, fsend, frecv, *after))


def _pair_exchange(grads, name):
    n = len(grads)

    def body(*refs):
        ins, outs = refs[:n], refs[n:2 * n]
        send_sems, recv_sems = refs[2 * n:]
        x, y, c = _place()
        sibling = (x, y, 1 - c)
        cps = []
        for w in range(n):
            for j in range(N_CHIPS):
                cp = pltpu.make_async_remote_copy(
                    src_ref=ins[w].at[j, 1 - c], dst_ref=outs[w].at[j], send_sem=send_sems.at[N_CHIPS * w + j],
                    recv_sem=recv_sems.at[N_CHIPS * w + j], device_id=sibling, device_id_type=MESH)
                cp.start()
                cps.append(cp)
        for cp in cps:
            cp.wait()

    return pl.pallas_call(
        body, name=name, in_specs=[HBM] * n, out_specs=[HBM] * n,
        out_shape=[jax.ShapeDtypeStruct((N_CHIPS,) + a.shape[2:], a.dtype) for a in grads],
        scratch_shapes=[pltpu.SemaphoreType.DMA((N_CHIPS * n,)), pltpu.SemaphoreType.DMA((N_CHIPS * n,))],
    )(*grads)


def _pair_start(grads, name):
    n = len(grads)
    lands = [lax.empty((N_CHIPS,) + a.shape[2:], a.dtype) for a in grads]

    def body(*refs):
        ins, land_in = refs[:n], refs[n:2 * n]
        send, recv = refs[2 * n], refs[2 * n + 1]
        token = refs[4 * n + 2]
        x, y, c = _place()
        for w in range(n):
            for j in range(N_CHIPS):
                k = N_CHIPS * w + j
                _remote(ins[w].at[j, 1 - c], land_in[w].at[j], send.at[k], recv.at[k], (x, y, 1 - c)).start()
        token[...] = jnp.zeros_like(token)

    res = pl.pallas_call(
        body, name=name, in_specs=[HBM_SPEC] * (2 * n),
        out_specs=[SEM_SPEC, SEM_SPEC] + [HBM_SPEC] * (2 * n) + [TOKEN_SPEC],
        out_shape=[_sems(N_CHIPS * n), _sems(N_CHIPS * n)] + _hbm_like(grads) + _hbm_like(lands) + [TOKEN],
        input_output_aliases={i: 2 + i for i in range(2 * n)},
        compiler_params=pltpu.CompilerParams(has_side_effects=EFFECT),
    )(*_in_hbm(grads), *_in_hbm(lands))
    return (res[0], res[1]), list(res[2:2 + n]), list(res[2 + n:2 + 2 * n]), res[2 + 2 * n]


def _pair_wait(sems, grads, lands, after, name):
    n = len(grads)

    def body(*refs):
        ins, land_in = refs[:n], refs[n:2 * n]
        send_in, recv_in = refs[2 * n], refs[2 * n + 1]
        x, y, c = _place()
        for w in range(n):
            for j in range(N_CHIPS):
                k = N_CHIPS * w + j
                cp = _remote(ins[w].at[j, 1 - c], land_in[w].at[j], send_in.at[k], recv_in.at[k], (x, y, 1 - c))
                cp.wait_send()
                cp.wait_recv()

    res = pl.pallas_call(
        body, name=name, in_specs=[HBM_SPEC] * (2 * n) + [SEM_SPEC, SEM_SPEC] + [HBM] * len(after),
        out_specs=[HBM_SPEC] * (2 * n), out_shape=_hbm_like(grads) + _hbm_like(lands),
        input_output_aliases={i: i for i in range(2 * n)},
        compiler_params=pltpu.CompilerParams(has_side_effects=EFFECT),
    )(*grads, *lands, sems[0], sems[1], *after)
    return list(res[:n]), list(res[n:])


def _chip_start(sums, name, to_all=()):
    m = len(sums)
    lands = [lax.empty(((N_CHIPS,) if i in to_all else ()) + a.shape, a.dtype) for i, a in enumerate(sums)]

    def body(*refs):
        ins, land_in = refs[:m], refs[m:2 * m]
        send, recv = refs[2 * m], refs[2 * m + 1]
        token = refs[4 * m + 2]
        x, y, c = _place()
        me = 2 * x + y
        for i in sorted(range(m), key=lambda i: i not in to_all):
            for j, (cx, cy) in enumerate(_other_chips(x, y)):
                src = ins[i] if i in to_all else ins[i].at[2 * cx + cy]
                _remote(src, land_in[i].at[me], send.at[3 * i + j], recv.at[3 * i + j], (cx, cy, c)).start()
        token[...] = jnp.zeros_like(token)

    res = pl.pallas_call(
        body, name=name, in_specs=[HBM_SPEC] * (2 * m),
        out_specs=[SEM_SPEC, SEM_SPEC] + [HBM_SPEC] * (2 * m) + [TOKEN_SPEC],
        out_shape=[_sems(3 * m), _sems(3 * m)] + _hbm_like(sums) + _hbm_like(lands) + [TOKEN],
        input_output_aliases={i: 2 + i for i in range(2 * m)},
        compiler_params=pltpu.CompilerParams(has_side_effects=EFFECT),
    )(*_in_hbm(sums), *_in_hbm(lands))
    return (res[0], res[1]), list(res[2:2 + m]), list(res[2 + m:2 + 2 * m]), res[2 + 2 * m]


def _chip_wait(sems, sums, lands, after, name, to_all=()):
    m = len(sums)

    def body(*refs):
        ins, land_in = refs[:m], refs[m:2 * m]
        send_in, recv_in = refs[2 * m], refs[2 * m + 1]
        x, y, c = _place()
        for i in range(m):
            for j, (cx, cy) in enumerate(_other_chips(x, y)):
                cj = 2 * cx + cy
                src = ins[i] if i in to_all else ins[i].at[cj]
                cp = _remote(src, land_in[i].at[cj], send_in.at[3 * i + j], recv_in.at[3 * i + j], (cx, cy, c))
                cp.wait_send()
                cp.wait_recv()

    res = pl.pallas_call(
        body, name=name, in_specs=[HBM_SPEC] * (2 * m) + [SEM_SPEC, SEM_SPEC] + [HBM] * len(after),
        out_specs=[HBM_SPEC] * (2 * m), out_shape=_hbm_like(sums) + _hbm_like(lands),
        input_output_aliases={i: i for i in range(2 * m)},
        compiler_params=pltpu.CompilerParams(has_side_effects=EFFECT),
    )(*sums, *lands, sems[0], sems[1], *after)
    return list(res[:m]), list(res[m:])


def _half_parts(bufs):
    parts = []
    for w, a in enumerate(bufs):
        parts += [(w, None)] if a.ndim == 3 else [(w, j) for j in range(a.shape[0])]
    return parts


def _half_ref(refs, w, j, h):
    return refs[w].at[h] if j is None else refs[w].at[j, h]


def _half_start(bufs, name):
    n = len(bufs)
    parts = _half_parts(bufs)

    def body(*refs):
        ins = refs[:n]
        send, recv = refs[n], refs[n + 1]
        x, y, c = _place()
        for k, (w, j) in enumerate(parts):
            mine = _half_ref(ins, w, j, c)
            _remote(mine, mine, send.at[k], recv.at[k], (x, y, 1 - c)).start()

    res = pl.pallas_call(
        body, name=name, in_specs=[HBM_SPEC] * n, out_specs=[SEM_SPEC, SEM_SPEC] + [HBM_SPEC] * n,
        out_shape=[_sems(len(parts)), _sems(len(parts))] + _hbm_like(bufs),
        input_output_aliases={w: 2 + w for w in range(n)},
        compiler_params=pltpu.CompilerParams(has_side_effects=EFFECT),
    )(*_in_hbm(bufs))
    return (res[0], res[1]), list(res[2:])


def _half_wait(sems, bufs, after, name):
    n = len(bufs)
    parts = _half_parts(bufs)

    def body(*refs):
        ins = refs[:n]
        send_in, recv_in = refs[n], refs[n + 1]
        x, y, c = _place()
        for k, (w, j) in enumerate(parts):
            mine = _half_ref(ins, w, j, c)
            _remote(mine, mine, send_in.at[k], recv_in.at[k], (x, y, 1 - c)).wait_send()
            theirs = _half_ref(ins, w, j, 1 - c)
            _remote(theirs, theirs, send_in.at[k], recv_in.at[k], (x, y, 1 - c)).wait_recv()

    return list(pl.pallas_call(
        body, name=name, in_specs=[HBM_SPEC] * n + [SEM_SPEC, SEM_SPEC] + [HBM] * len(after),
        out_specs=[HBM_SPEC] * n, out_shape=_hbm_like(bufs),
        input_output_aliases={w: w for w in range(n)},
        compiler_params=pltpu.CompilerParams(has_side_effects=EFFECT),
    )(*bufs, sems[0], sems[1], *after))


def _all_gather8(block, name, after=()):
    def body(in_ref, *rest):
        out_ref, send_sems, recv_sems, local_sem = rest[len(after):]
        x, y, c = _place()
        me = 4 * x + 2 * y + c
        mine = pltpu.make_async_copy(in_ref, out_ref.at[me], local_sem)
        mine.start()
        flips = [(fx, fy, fc) for fx in (0, 1) for fy in (0, 1) for fc in (0, 1)][1:]
        cps = []
        for k, (fx, fy, fc) in enumerate(flips):
            cp = pltpu.make_async_remote_copy(
                src_ref=in_ref, dst_ref=out_ref.at[me], send_sem=send_sems.at[k], recv_sem=recv_sems.at[k],
                device_id=(x ^ fx, y ^ fy, c ^ fc), device_id_type=MESH)
            cp.start()
            cps.append(cp)
        for k, (fx, fy, fc) in enumerate(flips):
            px, py, pc = x ^ fx, y ^ fy, c ^ fc
            theirs = out_ref.at[4 * px + 2 * py + pc]
            pltpu.make_async_remote_copy(
                src_ref=theirs, dst_ref=theirs, send_sem=send_sems.at[k], recv_sem=recv_sems.at[k],
                device_id=(px, py, pc), device_id_type=MESH).wait_recv()
        for cp in cps:
            cp.wait_send()
        mine.wait()

    return pl.pallas_call(
        body, name=name, in_specs=[HBM] * (1 + len(after)), out_specs=HBM,
        out_shape=jax.ShapeDtypeStruct((N_DEV,) + block.shape, block.dtype),
        scratch_shapes=[pltpu.SemaphoreType.DMA((N_DEV - 1,)), pltpu.SemaphoreType.DMA((N_DEV - 1,)),
                        pltpu.SemaphoreType.DMA],
    )(block, *after)


def _pair_sum(grad, recv, c_arr, name):
    _, _, rh, cols = grad.shape
    tr = _row_tile(rh, cols * grad.dtype.itemsize, STREAM_TILE // 2)

    def body(c_ref, g_ref, r_ref, o_ref):
        del c_ref
        o_ref[...] = (g_ref[...].astype(F32) + r_ref[...].astype(F32)).astype(o_ref.dtype)

    spec = pl.BlockSpec((None, tr, cols), lambda j, i, c_ref: (j, i, 0))
    return pl.pallas_call(
        body, name=name,
        grid_spec=pltpu.PrefetchScalarGridSpec(
            num_scalar_prefetch=1, grid=(N_CHIPS, rh // tr),
            in_specs=[pl.BlockSpec((None, None, tr, cols), lambda j, i, c_ref: (j, c_ref[0], i, 0)), spec],
            out_specs=spec),
        out_shape=jax.ShapeDtypeStruct(recv.shape, grad.dtype), compiler_params=_params(VMEM_STREAM_MIB),
    )(c_arr, grad, recv)


def _sum_by_chip(chip, p_ref, own_ref, o_ref):
    o_ref[...] = jnp.zeros_like(o_ref)
    for k in range(N_CHIPS):
        @pl.when(chip == k)
        def _():
            o_ref[...] += own_ref[...].astype(F32)

        @pl.when(chip != k)
        def _(k=k):
            o_ref[...] += p_ref[k].astype(F32)


def _chip_sum_all(parts, own, place_arr, name):
    _, nj, rh, cols = parts.shape
    tr = _row_tile(rh, cols * 4, STREAM_TILE // 2)

    def body(place_ref, p_ref, own_ref, o_ref):
        _sum_by_chip(place_ref[0], p_ref, own_ref, o_ref)

    return pl.pallas_call(
        body, name=name,
        grid_spec=pltpu.PrefetchScalarGridSpec(
            num_scalar_prefetch=1, grid=(nj, rh // tr),
            in_specs=[pl.BlockSpec((N_CHIPS, None, tr, cols), lambda j, i, place_ref: (0, j, i, 0)),
                      pl.BlockSpec((None, tr, cols), lambda j, i, place_ref: (j, i, 0))],
            out_specs=pl.BlockSpec((None, None, tr, cols), lambda j, i, place_ref: (j, place_ref[1], i, 0))),
        out_shape=jax.ShapeDtypeStruct((nj, 2, rh, cols), F32), compiler_params=_params(VMEM_STREAM_MIB),
    )(place_arr, parts, own)


def _chip_sum(parts, own, place_arr, name):
    _, rh, cols = parts.shape
    tr = _row_tile(rh, cols * 4, STREAM_TILE // 2)

    def body(place_ref, p_ref, own_ref, o_ref):
        _sum_by_chip(place_ref[0], p_ref, own_ref, o_ref)

    return pl.pallas_call(
        body, name=name,
        grid_spec=pltpu.PrefetchScalarGridSpec(
            num_scalar_prefetch=1, grid=(rh // tr,),
            in_specs=[pl.BlockSpec((N_CHIPS, tr, cols), lambda i, place_ref: (0, i, 0)),
                      pl.BlockSpec((None, tr, cols), lambda i, place_ref: (place_ref[0], i, 0))],
            out_specs=pl.BlockSpec((None, tr, cols), lambda i, place_ref: (place_ref[1], i, 0))),
        out_shape=jax.ShapeDtypeStruct((2, rh, cols), F32), compiler_params=_params(VMEM_STREAM_MIB),
    )(place_arr, parts, own)


def _adamw_math(w, g, m, v):
    m = ADAM_B1 * m + (1.0 - ADAM_B1) * g
    v = ADAM_B2 * v + (1.0 - ADAM_B2) * (g * g)
    m_hat = m / (1.0 - ADAM_B1 ** ADAM_STEP)
    v_hat = v / (1.0 - ADAM_B2 ** ADAM_STEP)
    delta = -ADAM_LR * (m_hat / (jnp.sqrt(v_hat) + ADAM_EPS) + ADAM_WD * w)
    return delta, m, v


def _adamw(w, g, m, v, name):
    rows, cols = w.shape
    tr = _row_tile(rows, cols * 4)

    def body(w_ref, g_ref, m_ref, v_ref, go_ref, d_ref, nm_ref, nv_ref):
        gv = g_ref[...]
        go_ref[...] = gv
        d_ref[...], nm_ref[...], nv_ref[...] = _adamw_math(w_ref[...], gv, m_ref[...], v_ref[...])

    spec = pl.BlockSpec((tr, cols), lambda i: (i, 0))
    return pl.pallas_call(
        body, name=name, grid=(rows // tr,), in_specs=[spec] * 4, out_specs=[spec] * 4,
        out_shape=[jax.ShapeDtypeStruct(w.shape, F32)] * 4, compiler_params=_params(VMEM_STREAM_MIB),
    )(w, g, m, v)


def _sum8_adamw_row(parts, w, m, v, name):
    cols = parts.shape[2]

    def body(p_ref, w_ref, m_ref, v_ref, g_ref, d_ref, nm_ref, nv_ref):
        g = p_ref[0, 0:1, :]
        for k in range(1, N_DEV):
            g = g + p_ref[k, 0:1, :]
        g_ref[...] = g
        d_ref[...], nm_ref[...], nv_ref[...] = _adamw_math(w_ref[...], g, m_ref[...], v_ref[...])

    return pl.pallas_call(
        body, name=name, out_shape=[jax.ShapeDtypeStruct((1, cols), F32)] * 4, compiler_params=_params(VMEM_STREAM_MIB),
    )(parts, w, m, v)


def _pack_rows(pieces, rows, name):
    cols = pieces[0].shape[1]
    n = len(pieces)

    def body(*refs):
        o_ref = refs[n]
        o_ref[...] = jnp.zeros_like(o_ref)
        at = 0
        for p_ref in refs[:n]:
            r = p_ref.shape[0]
            o_ref[at:at + r, :] = p_ref[...]
            at += r

    return pl.pallas_call(
        body, name=name, out_shape=jax.ShapeDtypeStruct((rows, cols), F32), compiler_params=_params(VMEM_STREAM_MIB),
    )(*pieces)


def kernel(x, norm_mix_g, w_in, conv_w, conv_b, lru_wa, lru_ba, lru_wx, lru_bx, lru_lambda, w_proj_attn, w_proj_lru, w_out, norm_mlp_g, w_up, w_down, norm_final_g, loss_target, m_norm_mix_g, m_w_in, m_conv_w, m_conv_b, m_lru_wa, m_lru_ba, m_lru_wx, m_lru_bx, m_lru_lambda, m_w_proj_attn, m_w_proj_lru, m_w_out, m_norm_mlp_g, m_w_up, m_w_down, m_norm_final_g, v_norm_mix_g, v_w_in, v_conv_w, v_conv_b, v_lru_wa, v_lru_ba, v_lru_wx, v_lru_bx, v_lru_lambda, v_w_proj_attn, v_w_proj_lru, v_w_out, v_norm_mlp_g, v_w_up, v_w_down, v_norm_final_g):
    s, d = x.shape[1], x.shape[2]
    ff = w_up.shape[2] * N_CHIPS
    heads = d // HEAD_DIM
    u = d // 4
    assert s % (max(DILATIONS) * ATTN_BLK) == 0 and d % (4 * HEAD_DIM) == 0 and ff == 4 * d and DILATIONS[0] == 1
    xs, target = _in_hbm([x[0], loss_target[0]])
    gf = norm_final_g.reshape(1, d)
    wa, wx = lru_wa[0], lru_wx[0]
    core = lax.axis_index("c").astype(jnp.int32)
    chip = (2 * lax.axis_index("x") + lax.axis_index("y")).astype(jnp.int32)
    cidx = core.reshape(1)
    chip_arr = chip.reshape(1)
    place_arr = jnp.stack([chip, core])
    slopes = jnp.broadcast_to(
        (2.0 ** (-8.0 * jnp.arange(1, heads + 1, dtype=F32) / heads))[:, None, None], (heads, 1, HEAD_DIM))

    big = _in_hbm([w_in[0], w_proj_attn[0], w_proj_lru[0], w_out[0], w_up[0], w_down[0]])
    names = ["w_in", "w_proj_attn", "w_proj_lru", "w_out", "w_up", "w_down"]
    cw_pad = jnp.pad(conv_w[0], ((0, 8 - CONV_TAPS), (0, 0)))
    cw_all = _all_gather8(cw_pad, "gather_conv_w")
    conv_w_full = jnp.concatenate([cw_all[2 * j] for j in range(N_CHIPS)], axis=1)
    (sem_a,), buf_a = _gather_start([_cast_shard(big[0], chip_arr, "cast_w_in")], [([0], NEIGHBOURS)], [cw_all],
                                    "gather_start_w_in")

    conv_b_h, wa_h, ba_h, wx_h, bx_h, lam_h, gmix_h, gmlp_h, gf_h = _in_hbm(
        [conv_b, wa, lru_ba, wx, lru_bx, lru_lambda, norm_mix_g, norm_mlp_g, gf])
    xn = _rms_fwd(xs, gmix_h, "norm_mix")
    flip_bits = (2, 1, 3)

    def w_in_view():
        return buf_a[0].reshape(N_CHIPS, d, N_SLOTS * u)

    proj = _proj_in_shard(xn, w_in_view(), chip_arr, None, "proj_in_own")
    bufs = [_cast_shard(w, chip_arr, "cast_" + nm) for w, nm in zip(big[1:], names[1:])]
    (fs, fr, rs, rr), buf_a = _gather_forward_relay(buf_a, sem_a[1], [proj] + bufs, "gather_forward_w_in")
    buf_a = _gather_finish(buf_a, sem_a[0], fs, fr, [], "gather_finish_w_in", flips=NEIGHBOURS)
    for j in NEIGHBOURS:
        proj = _proj_in_shard(xn, w_in_view(), chip_arr ^ flip_bits[j], proj, "proj_in_from_%d" % j)
    (sem_b, sem_c, sem_d), bufs = _gather_start(
        bufs, [([0, 1, 2], ALL_FLIPS), ([3], ALL_FLIPS), ([4], ALL_FLIPS)], [proj], "gather_start_rest")
    (ds, dr), buf_a = _gather_forward_diag(buf_a, rr, [proj, bufs[0]], "gather_forward_w_in_diag")
    buf_a = _gather_finish_diag(buf_a, rs, ds, dr, [], "gather_finish_w_in_diag")
    proj = _proj_in_shard(xn, w_in_view(), chip_arr ^ flip_bits[2], proj, "proj_in_from_2")
    w_in_g = w_in_view()
    proj_a = proj_b = proj
    y_attn, lse = _attn_fwd(proj_a, slopes)
    y_lru, h_lru = _lru_fwd(proj_b, conv_w_full, conv_b_h, wa_h, ba_h, wx_h, bx_h, lam_h)
    fsem_b, buf_b = _gather_forward(bufs[:3], sem_b[1], [y_attn, y_lru], "gather_forward_proj")
    buf_b = _gather_finish(buf_b, sem_b[0], fsem_b[0], fsem_b[1], [], "gather_finish_proj")
    wpa_g = buf_b[0].reshape(d, d)
    wpl_g = buf_b[1].reshape(d, d)
    wout_g = buf_b[2].reshape(d, d)

    tn = u
    sd_f32 = jax.ShapeDtypeStruct((s, d), F32)
    sd_bf16 = jax.ShapeDtypeStruct((s, d), BF16)
    col = pl.BlockSpec((s, tn), lambda i, j, k: (0, j))

    def slot(n):
        return pl.BlockSpec((None, s, tn), lambda i, j, k: (n, 0, j))

    p_attn = _mm_nn("proj_attn", y_attn, wpa_g, [], [], [sd_f32], [col], _store, tn)[0]
    fsem_c, buf_c = _gather_forward(bufs[3:4], sem_c[1], [p_attn], "gather_forward_w_up")

    def merge(acc, extras, outs):
        pa_ref, ga_ref, gl_ref = extras
        merged = _sigmoid(ga_ref[...]) * pa_ref[...] + _sigmoid(gl_ref[...]) * acc
        outs[0][...] = merged.astype(BF16)
        outs[1][...] = acc

    tn2 = max(HEAD_DIM, u // 2)
    col2 = pl.BlockSpec((s, tn2), lambda i, j, k: (0, j))

    def slot2(n):
        return pl.BlockSpec((None, s, tn2), lambda i, j, k: (n, 0, j))

    merged, p_lru = _mm_nn("proj_lru_merge", y_lru, wpl_g, [p_attn, proj, proj], [col2, slot2(5), slot2(6)],
                           [sd_bf16, sd_f32], [col2, col2], merge, tn2)

    def add_resid(acc, extras, outs):
        outs[0][...] = extras[0][...] + acc

    h1 = _mm_nn("w_out_resid", merged, wout_g, [xs], [col], [sd_f32], [col], add_resid, tn)[0]
    hn = _rms_fwd(h1, gmlp_h, "norm_mlp")
    buf_c = _gather_finish(buf_c, sem_c[0], fsem_c[0], fsem_c[1], [hn], "gather_finish_w_up")
    wup_g = buf_c[0].reshape(N_CHIPS, d, d)

    def relu_sq(acc, extras, outs):
        r = jnp.maximum(acc, 0.0)
        outs[0][...] = (r * r).astype(BF16)
        outs[1][...] = r.astype(BF16)

    sf_bf16 = jax.ShapeDtypeStruct((s, ff), BF16)
    hid, relu_up = _mm(
        "w_up_relu2", [hn, wup_g],
        [pl.BlockSpec((s, d), lambda i, j, k: (0, 0)),
         pl.BlockSpec((None, d, tn), lambda i, j, k: (j // 4, 0, j % 4))],
        [sf_bf16, sf_bf16], [col, col], (1, ff // tn, 1), NN, relu_sq)
    fsem_d, buf_d = _gather_forward(bufs[4:], sem_d[1], [hid], "gather_forward_w_down")
    wdown_g = _gather_finish(buf_d, sem_d[0], fsem_d[0], fsem_d[1], [], "gather_finish_w_down")[0].reshape(ff, d)
    h2 = _mm(
        "w_down_resid", [hid, wdown_g, h1],
        [pl.BlockSpec((s, d), lambda i, j, k: (0, k)), pl.BlockSpec((d, tn), lambda i, j, k: (k, j)), col],
        [sd_f32], [col], (1, d // tn, ff // d), NN, add_resid, nk=ff // d, acc_shape=(s, tn))[0]
    loss_part, dh2, dh2_b, d_gf = _loss_head(h2, gf_h, target)
    loss = lax.psum(loss_part[0, 0], AXES)

    def relu_sq_bwd(acc, extras, outs):
        outs[0][...] = (acc * (2.0 * extras[0][...].astype(F32))).astype(BF16)

    dup = _mm_nt("d_hid", dh2_b, wdown_g, [relu_up], [col], [sf_bf16], [col], relu_sq_bwd, tn)[0]
    tok_d = pl.BlockSpec((s, d), lambda i, j: (0, 0))
    g_wdown = _mm_tn(
        "g_w_down", hid, dh2_b, pl.BlockSpec((s, d), lambda i, j: (0, i)),
        pl.BlockSpec((s, tn), lambda i, j: (0, j)), jax.ShapeDtypeStruct((ff, d), BF16),
        pl.BlockSpec((d, tn), lambda i, j: (i, j)), (ff // d, d // tn), d, tn, s)
    dhn = _mm(
        "d_hn", [dup, wup_g],
        [pl.BlockSpec((s, d), lambda i, j, k: (0, k)), pl.BlockSpec((None, tn, d), lambda i, j, k: (k, j, 0))],
        [sd_f32], [col], (1, d // tn, ff // d), NT, _store, nk=ff // d, acc_shape=(s, tn))[0]
    g_wup = _mm_tn(
        "g_w_up", hn, dup, tok_d, pl.BlockSpec((s, tn), lambda i, j: (0, j)),
        jax.ShapeDtypeStruct((N_CHIPS, d, d), BF16), pl.BlockSpec((None, d, tn), lambda i, j: (j // 4, 0, j % 4)),
        (1, ff // tn), d, tn, s)
    big_m = _in_hbm([m_w_in[0], m_w_proj_attn[0], m_w_proj_lru[0], m_w_out[0], m_w_up[0], m_w_down[0]])
    big_v = _in_hbm([v_w_in[0], v_w_proj_attn[0], v_w_proj_lru[0], v_w_out[0], v_w_up[0], v_w_down[0]])
    big_out = {}

    def reduce_begin(ids, gs, tag, everywhere=None):
        g4 = [g.reshape(N_CHIPS, 2, big[i].shape[0] // 2, big[i].shape[1]) for i, g in zip(ids, gs)]
        tags = [names[i] for i in ids]
        if everywhere is not None:
            g4.append(everywhere.reshape(N_CHIPS, 2, everywhere.shape[0] // (2 * N_CHIPS), everywhere.shape[1]))
            tags.append("small_" + tag)
        from_sibling = _pair_exchange(g4, "pair_exchange_" + tag)
        sums = [_pair_sum(g, r, cidx, "pair_sum_" + t) for t, g, r in zip(tags, g4, from_sibling)]
        to_all = () if everywhere is None else (len(ids),)
        return _chip_start(sums, "chip_start_" + tag, to_all), to_all

    def pair_begin(ids, gs, tag):
        g4 = [g.reshape(N_CHIPS, 2, big[i].shape[0] // 2, big[i].shape[1]) for i, g in zip(ids, gs)]
        return _pair_start(g4, "pair_start_" + tag)

    def reduce_begin_paired(ids, paired, after, tag):
        sems, g4, lands, _ = paired
        g4, from_sibling = _pair_wait(sems, g4, lands, after, "pair_wait_" + tag)
        sums = [_pair_sum(g, r, cidx, "pair_sum_" + names[i]) for i, g, r in zip(ids, g4, from_sibling)]
        return _chip_start(sums, "chip_start_" + tag), ()

    def reduce_mid(ids, begun, after, tag):
        (sems, sums, lands, _), to_all = begun
        sums, lands = _chip_wait(sems, sums, lands, after, "chip_wait_" + tag, to_all)
        halves = [_chip_sum(p, own, place_arr, "chip_sum_" + names[i]) for i, p, own in zip(ids, lands, sums)]
        if to_all:
            halves.append(_chip_sum_all(lands[-1], sums[-1], place_arr, "chip_sum_small_" + tag))
        return _half_start(halves, "half_start_" + tag), to_all

    def reduce_end(ids, mid, after, tag):
        (hsems, halves), to_all = mid
        full = _half_wait(hsems, halves, after, "half_wait_" + tag)
        done = []
        for i, g in zip(ids, full):
            res = _adamw(big[i], g.reshape(big[i].shape), big_m[i], big_v[i], "adamw_" + names[i])
            big_out[names[i]] = tuple(a[None] for a in res)
            done.append(res[1])
        everywhere = full[-1].reshape(-1, full[-1].shape[-1]) if to_all else None
        return everywhere, done

    def after_token(a, begun):
        return a + begun[0][3][:1, :1]

    pair_mlp = pair_begin([4, 5], [g_wup, g_wdown], "mlp")
    dh1, dh1_b, d_gmlp = _rms_bwd(h1, norm_mlp_g + pair_mlp[3][:1, :1], dhn, dh2, "norm_mlp_bwd")

    g_wout = _mm_tn(
        "g_w_out", merged, dh1_b, tok_d, pl.BlockSpec((s, tn), lambda i, j: (0, j)),
        jax.ShapeDtypeStruct((d, d), BF16), pl.BlockSpec((d, tn), lambda i, j: (0, j)), (1, d // tn), d, tn, s)

    def merge_bwd(acc, extras, outs):
        pa_ref, pl_ref, ga_ref, gl_ref = extras
        sa, sl = _sigmoid(ga_ref[...]), _sigmoid(gl_ref[...])
        outs[0][...] = (acc * sa).astype(BF16)
        outs[1][...] = (acc * sl).astype(BF16)
        outs[2][0] = (acc * pa_ref[...] * (sa * (1.0 - sa))).astype(BF16)
        outs[2][1] = (acc * pl_ref[...] * (sl * (1.0 - sl))).astype(BF16)

    nb = N_SLOTS - N_QKV
    d_pa, d_pl, dproj_b = _mm_nt(
        "d_merged", dh1_b, wout_g, [p_attn, p_lru, proj, proj], [col2, col2, slot2(5), slot2(6)],
        [sd_bf16, sd_bf16, jax.ShapeDtypeStruct((nb, s, d), BF16)],
        [col2, col2, pl.BlockSpec((2, s, tn2), lambda i, j, k: (1, 0, j))], merge_bwd, tn2)
    red_mlp = reduce_begin_paired([4, 5], pair_mlp, [d_pa], "mlp")
    dy_attn = _mm_nt("d_y_attn", d_pa, wpa_g, [], [], [sd_f32], [col], _store, tn)[0]
    dy_lru = _mm_nt("d_y_lru", d_pl, wpl_g, [], [], [sd_f32], [col], _store, tn)[0]
    g_wpa = _mm_tn(
        "g_w_proj_attn", y_attn, d_pa, tok_d, pl.BlockSpec((s, tn), lambda i, j: (0, j)),
        jax.ShapeDtypeStruct((d, d), BF16), pl.BlockSpec((d, tn), lambda i, j: (0, j)), (1, d // tn), d, tn, s)
    g_wpl = _mm_tn(
        "g_w_proj_lru", y_lru, d_pl, tok_d, pl.BlockSpec((s, tn), lambda i, j: (0, j)),
        jax.ShapeDtypeStruct((d, d), BF16), pl.BlockSpec((d, tn), lambda i, j: (0, j)), (1, d // tn), d, tn, s)

    pair_proj = pair_begin([1, 2, 3], [g_wpa, g_wpl, g_wout], "proj")

    dproj_b, d_cw, d_cb, d_wa, d_ba, d_wx, d_bx, d_lam = _lru_bwd(
        proj_b, h_lru, dy_lru, conv_w_full, conv_b_h, wa_h, ba_h, wx_h, bx_h, lru_lambda + pair_proj[3][:1, :1],
        dproj_b)
    red_proj = reduce_begin_paired([1, 2, 3], pair_proj, [dproj_b], "proj")
    dproj_a = _attn_bwd(proj_a, after_token(slopes, red_proj), y_attn, lse, dy_attn)
    per = N_SLOTS
    g_win_shape = jax.ShapeDtypeStruct((N_CHIPS, d, N_SLOTS * u), BF16)

    def g_win_part(name, dproj, first, prev):
        n_units = 4 * dproj.shape[0]
        return _mm_tn(
            name, xn, dproj, tok_d, pl.BlockSpec((None, s, u), lambda i, j: (j // 4, 0, j % 4)),
            g_win_shape, pl.BlockSpec((None, d, u), lambda i, j: ((j + first) // per, 0, (j + first) % per)),
            (1, n_units), d, u, s, aliases=None if prev is None else {2: 0}, extra=prev)

    mat_rows = heads * HEAD_DIM * HEAD_DIM // d
    vec_names = ["norm_mix_g", "conv_b", "lru_ba", "lru_bx", "lru_lambda", "norm_mlp_g", "norm_final_g"]

    def pack(wa_, wx_, cw_, vecs, name):
        rows = [wa_.reshape(mat_rows, d), wx_.reshape(mat_rows, d), cw_] + [a.reshape(1, d) for a in vecs]
        n = sum(a.shape[0] for a in rows)
        return _pack_rows(_in_hbm(rows), n + (-n % 64), name)

    zero_cw = jnp.zeros((CONV_TAPS, d), F32)
    small_g = pack(d_wa, d_wx, d_cw, [jnp.zeros((1, d), F32), d_cb, d_ba, d_bx, d_lam, d_gmlp, d_gf], "pack_small_g")
    g_win = g_win_part("g_w_in_qkv", dproj_a, 0, None)
    g_win = g_win_part("g_w_in_rest", dproj_b, 4 * N_QKV, g_win)
    red_in = reduce_begin([0], [g_win], "w_in", everywhere=small_g)
    dxn = _dxn(dproj_a, dproj_b, w_in_g, 2 * tn, [red_in[0][3]])
    grad_x, _, d_gmix = _rms_bwd(xs, norm_mix_g, dxn, dh1, "norm_mix_bwd")

    mid_mlp = reduce_mid([4, 5], red_mlp, [grad_x], "mlp")
    mid_proj = reduce_mid([1, 2, 3], red_proj, [mid_mlp[0][1][0]], "proj")
    _, done_mlp = reduce_end([4, 5], mid_mlp, [mid_proj[0][1][0]], "mlp")
    _, done_proj = reduce_end([1, 2, 3], mid_proj, done_mlp[-1:], "proj")
    done = done_mlp + done_proj
    small_w = pack(wa, wx, zero_cw, [norm_mix_g, conv_b, lru_ba, lru_bx, lru_lambda, norm_mlp_g, norm_final_g],
                   "pack_small_w")
    small_m = pack(m_lru_wa[0], m_lru_wx[0], zero_cw,
                   [m_norm_mix_g, m_conv_b, m_lru_ba, m_lru_bx, m_lru_lambda, m_norm_mlp_g, m_norm_final_g],
                   "pack_small_m")
    small_v = pack(v_lru_wa[0], v_lru_wx[0], zero_cw,
                   [v_norm_mix_g, v_conv_b, v_lru_ba, v_lru_bx, v_lru_lambda, v_norm_mlp_g, v_norm_final_g],
                   "pack_small_v")
    mid_in = reduce_mid([0], red_in, done + [small_w, small_m, small_v], "w_in")
    gmix_parts = _all_gather8(jnp.pad(d_gmix, ((0, 7), (0, 0))), "gather_gain_grad", [mid_in[0][1][0]])
    gmix_out = _sum8_adamw_row(gmix_parts, gmix_h, *_in_hbm([m_norm_mix_g, v_norm_mix_g]), "sum_adamw_norm_mix_g")
    small_sum, _ = reduce_end([0], mid_in, [gmix_out[1]], "w_in")
    small = _adamw(small_w, small_sum, small_m, small_v, "adamw_small")
    g_cw = lax.dynamic_slice(small_sum[2 * mat_rows:2 * mat_rows + CONV_TAPS], (0, chip * u), (CONV_TAPS, u))
    cw_out = _adamw(*_in_hbm([conv_w[0], g_cw, m_conv_w[0], v_conv_w[0]]), "adamw_conv_w")

    def small_leaf(kind, name):
        a = small[kind]
        if name == "norm_mix_g":
            return gmix_out[kind]
        if name == "lru_wa":
            return a[0:mat_rows].reshape(lru_wa.shape)
        if name == "lru_wx":
            return a[mat_rows:2 * mat_rows].reshape(lru_wx.shape)
        if name == "conv_w":
            return cw_out[kind][None]
        row = a[2 * mat_rows + CONV_TAPS + vec_names.index(name)]
        return row if name == "norm_final_g" else row[None]

    order = ["norm_mix_g", "w_in", "conv_w", "conv_b", "lru_wa", "lru_ba", "lru_wx", "lru_bx", "lru_lambda",
             "w_proj_attn", "w_proj_lru", "w_out", "norm_mlp_g", "w_up", "w_down", "norm_final_g"]
    outs = [loss, grad_x[None]]
    for kind in range(4):
        for name in order:
            outs.append(big_out[name][kind] if name in big_out else small_leaf(kind, name))
    return tuple(outs)
```

```python
import functools

import jax
import jax.numpy as jnp
from jax import lax
from jax.experimental import pallas as pl
from jax.experimental.pallas import tpu as pltpu

F32 = jnp.float32
BF16 = jnp.bfloat16
MESH = pl.DeviceIdType.MESH
AXES = ("x", "y", "c")

N_CHIPS = 4
N_DEV = 8
HEAD_DIM = 128
ATTN_BLK = 128
DILATIONS = (1, 4, 16)
ATTN_UNROLL = 16
CONV_TAPS = 4
LRU_C = 8.0
EPS = 1e-6
N_SLOTS = 7
N_QKV = 3
VMEM_MIB = 2 ** 20
VMEM_V7X = 64 * VMEM_MIB
STREAM_TILE = 4 * VMEM_MIB
VMEM_STREAM_MIB = 32
VMEM_TILES_MIB = 56

ADAM_LR = 0.001
ADAM_B1 = 0.9
ADAM_B2 = 0.999
ADAM_EPS = 1e-08
ADAM_WD = 0.01
ADAM_STEP = 10

NN = (((1,), (0,)), ((), ()))
NT = (((1,), (1,)), ((), ()))
TN = (((0,), (0,)), ((), ()))


def _params(vmem_mib=None, **kw):
    limit = None if vmem_mib is None else min(vmem_mib * VMEM_MIB, VMEM_V7X - 8 * VMEM_MIB)
    return pltpu.CompilerParams(vmem_limit_bytes=limit, **kw)


def _row_tile(rows, row_bytes, budget=VMEM_MIB):
    t = rows
    while t % 16 == 0 and t * row_bytes > budget:
        t //= 2
    return t


def _dot(a, b, dims):
    return lax.dot_general(a.astype(BF16), b.astype(BF16), dims, preferred_element_type=F32)


def _sigmoid(x):
    return jax.nn.sigmoid(x)


def _rms_fwd(x, g, name):
    s, d = x.shape
    tm = _row_tile(s, d * 4)

    def body(x_ref, g_ref, o_ref):
        xf = x_ref[...]
        r = lax.rsqrt(jnp.mean(xf * xf, axis=-1, keepdims=True) + EPS)
        o_ref[...] = (xf * r * g_ref[...]).astype(o_ref.dtype)

    return pl.pallas_call(
        body, name=name, grid=(s // tm,),
        in_specs=[pl.BlockSpec((tm, d), lambda i: (i, 0)), pl.BlockSpec((1, d), lambda i: (0, 0))],
        out_specs=pl.BlockSpec((tm, d), lambda i: (i, 0)),
        out_shape=jax.ShapeDtypeStruct((s, d), BF16), compiler_params=_params(VMEM_STREAM_MIB),
    )(x, g)


def _rms_bwd(x, g, dy, resid, name):
    s, d = x.shape
    tm = _row_tile(s, d * 4)

    def body(x_ref, g_ref, dy_ref, res_ref, dx_ref, dxb_ref, dg_ref):
        xf = x_ref[...]
        r = lax.rsqrt(jnp.mean(xf * xf, axis=-1, keepdims=True) + EPS)
        xh = xf * r
        dyv = dy_ref[...]
        dxh = dyv * g_ref[...]
        dx = r * (dxh - xh * jnp.mean(dxh * xh, axis=-1, keepdims=True)) + res_ref[...]
        dx_ref[...] = dx
        dxb_ref[...] = dx.astype(BF16)
        part = jnp.sum(dyv * xh, axis=0, keepdims=True)

        @pl.when(pl.program_id(0) == 0)
        def _():
            dg_ref[...] = part

        @pl.when(pl.program_id(0) > 0)
        def _():
            dg_ref[...] += part

    row = pl.BlockSpec((tm, d), lambda i: (i, 0))
    vec = pl.BlockSpec((1, d), lambda i: (0, 0))
    return pl.pallas_call(
        body, name=name, grid=(s // tm,),
        in_specs=[row, vec, row, row], out_specs=[row, row, vec],
        out_shape=[jax.ShapeDtypeStruct((s, d), F32), jax.ShapeDtypeStruct((s, d), BF16),
                   jax.ShapeDtypeStruct((1, d), F32)],
        compiler_params=_params(VMEM_STREAM_MIB),
    )(x, g, dy, resid)


def _loss_head(h2, g, target):
    s, d = h2.shape
    tm = _row_tile(s, d * 4)

    def body(x_ref, g_ref, t_ref, loss_ref, dx_ref, dxb_ref, dg_ref):
        xf = x_ref[...]
        gv = g_ref[...]
        r = lax.rsqrt(jnp.mean(xf * xf, axis=-1, keepdims=True) + EPS)
        xh = xf * r
        err = xh * gv - t_ref[...]
        part = jnp.sum(jnp.sum(err * err, axis=1, keepdims=True), axis=0, keepdims=True) * (0.5 / d)
        dyv = err * (1.0 / d)
        dxh = dyv * gv
        dx = r * (dxh - xh * jnp.mean(dxh * xh, axis=-1, keepdims=True))
        dx_ref[...] = dx
        dxb_ref[...] = dx.astype(BF16)
        dgp = jnp.sum(dyv * xh, axis=0, keepdims=True)

        @pl.when(pl.program_id(0) == 0)
        def _():
            dg_ref[...] = dgp
            loss_ref[...] = jnp.broadcast_to(part, loss_ref.shape)

        @pl.when(pl.program_id(0) > 0)
        def _():
            dg_ref[...] += dgp
            loss_ref[...] += jnp.broadcast_to(part, loss_ref.shape)

    row = pl.BlockSpec((tm, d), lambda i: (i, 0))
    vec = pl.BlockSpec((1, d), lambda i: (0, 0))
    return pl.pallas_call(
        body, name="loss_head", grid=(s // tm,),
        in_specs=[row, vec, row],
        out_specs=[pl.BlockSpec((8, 128), lambda i: (0, 0)), row, row, vec],
        out_shape=[jax.ShapeDtypeStruct((8, 128), F32), jax.ShapeDtypeStruct((s, d), F32),
                   jax.ShapeDtypeStruct((s, d), BF16), jax.ShapeDtypeStruct((1, d), F32)],
        compiler_params=_params(VMEM_STREAM_MIB),
    )(h2, g, target)


def _mm(name, operands, in_specs, out_shape, out_specs, grid, dims, epilogue, nk=1, acc_shape=None,
        aliases=None):
    n_in = len(operands)
    n_out = len(out_shape)

    def body(*refs):
        a_ref, b_ref = refs[0], refs[1]
        extras = refs[2:n_in]
        outs = refs[n_in:n_in + n_out]

        def prod():
            return _dot(a_ref[...], b_ref[...], dims)

        if nk == 1:
            epilogue(prod(), extras, outs)
        else:
            acc = refs[n_in + n_out]
            k = pl.program_id(2)

            @pl.when(k == 0)
            def _():
                acc[...] = prod()

            @pl.when(k > 0)
            def _():
                acc[...] += prod()

            @pl.when(k == nk - 1)
            def _():
                epilogue(acc[...], extras, outs)

    scratch = [] if nk == 1 else [pltpu.VMEM(acc_shape, F32)]
    return pl.pallas_call(
        body, name=name, grid=grid, in_specs=in_specs, out_specs=out_specs, out_shape=out_shape,
        scratch_shapes=scratch, input_output_aliases=aliases or {},
        compiler_params=_params(VMEM_TILES_MIB),
    )(*operands)


def _store(acc, extras, outs):
    outs[0][...] = acc.astype(outs[0].dtype)


def _proj_in_shard(xn, w_in_g, shard_arr, prev, name):
    s, d = xn.shape
    u = d // 4
    per = N_SLOTS
    n_prev = 0 if prev is None else 1

    def body(sh_ref, x_ref, w_ref, *rest):
        del sh_ref
        rest[n_prev][...] = _dot(x_ref[...], w_ref[...], NN)

    def out_map(j, sh_ref):
        unit = per * sh_ref[0] + j
        return (unit // 4, 0, unit % 4)

    return pl.pallas_call(
        body, name=name,
        grid_spec=pltpu.PrefetchScalarGridSpec(
            num_scalar_prefetch=1, grid=(per,),
            in_specs=[pl.BlockSpec((s, d), lambda j, sh_ref: (0, 0)),
                      pl.BlockSpec((None, d, u), lambda j, sh_ref: (sh_ref[0], 0, j))] + [HBM] * n_prev,
            out_specs=pl.BlockSpec((None, s, u), out_map)),
        out_shape=jax.ShapeDtypeStruct((N_SLOTS, s, d), F32),
        input_output_aliases={3: 0} if n_prev else {},
        compiler_params=_params(VMEM_TILES_MIB),
    )(shard_arr, xn, w_in_g, *([] if prev is None else [prev]))


def _mm_nn(name, a, b, extras, extra_specs, out_shape, out_specs, epilogue, tn, aliases=None):
    s, kdim = a.shape
    n = b.shape[1]
    return _mm(
        name, [a, b] + list(extras),
        [pl.BlockSpec((s, kdim), lambda i, j, k: (0, 0)), pl.BlockSpec((kdim, tn), lambda i, j, k: (0, j))]
        + list(extra_specs),
        out_shape, out_specs, (1, n // tn, 1), NN, epilogue, aliases=aliases)


def _mm_nt(name, a, b, extras, extra_specs, out_shape, out_specs, epilogue, tn, aliases=None):
    s, kdim = a.shape
    n = b.shape[0]
    return _mm(
        name, [a, b] + list(extras),
        [pl.BlockSpec((s, kdim), lambda i, j, k: (0, 0)), pl.BlockSpec((tn, kdim), lambda i, j, k: (j, 0))]
        + list(extra_specs),
        out_shape, out_specs, (1, n // tn, 1), NT, epilogue, aliases=aliases)


def _mm_tn(name, a, b, a_spec, b_spec, out_shape, out_spec, grid, m, tn, s, aliases=None, extra=None):
    ch = 256
    n_in = 2 if extra is None else 3

    def body(*refs):
        a_ref, b_ref = refs[0], refs[1]
        o_ref, at_ref = refs[n_in], refs[n_in + 1]

        @pl.when(pl.program_id(1) == 0)
        def _():
            for c0 in range(0, s, ch):
                at_ref[:, c0:c0 + ch] = a_ref[c0:c0 + ch, :].astype(F32).T.astype(BF16)

        o_ref[...] = _dot(at_ref[...], b_ref[...], NN).astype(o_ref.dtype)

    operands = [a, b] + ([] if extra is None else [extra])
    in_specs = [a_spec, b_spec] + ([] if extra is None else [pl.BlockSpec(memory_space=pl.ANY)])
    return pl.pallas_call(
        body, name=name, grid=grid, in_specs=in_specs, out_specs=out_spec, out_shape=out_shape,
        scratch_shapes=[pltpu.VMEM((m, s), BF16)], input_output_aliases=aliases or {},
        compiler_params=_params(VMEM_TILES_MIB),
    )(*operands)


def _dxn(dproj_a, dproj_b, w_in_g, tn, after):
    n_a, s, d = dproj_a.shape
    u = d // 4
    ua = 4 * n_a
    nk = 4 * N_SLOTS
    per = N_SLOTS

    def body(a_ref, b_ref, w_ref, *rest):
        o_ref = rest[len(after)]
        k = pl.program_id(2)

        @pl.when(k == 0)
        def _():
            o_ref[...] = jnp.zeros_like(o_ref)

        @pl.when(k < ua)
        def _():
            o_ref[...] += _dot(a_ref[...], w_ref[...], NT)

        @pl.when(k >= ua)
        def _():
            o_ref[...] += _dot(b_ref[...], w_ref[...], NT)

    def a_map(i, j, k):
        kk = jnp.minimum(k, ua - 1)
        return (kk // 4, 0, kk % 4)

    def b_map(i, j, k):
        kk = jnp.maximum(k - ua, 0)
        return (kk // 4, 0, kk % 4)

    return pl.pallas_call(
        body, name="dxn", grid=(1, d // tn, nk),
        in_specs=[pl.BlockSpec((None, s, u), a_map), pl.BlockSpec((None, s, u), b_map),
                  pl.BlockSpec((None, tn, u), lambda i, j, k: (k // per, j, k % per))] + [HBM] * len(after),
        out_specs=pl.BlockSpec((s, tn), lambda i, j, k: (0, j)),
        out_shape=jax.ShapeDtypeStruct((s, d), F32),
        compiler_params=_params(VMEM_TILES_MIB),
    )(dproj_a, dproj_b, w_in_g, *after)


def _attn_masks(slope, dil):
    ii = lax.broadcasted_iota(jnp.int32, (ATTN_BLK, 2 * ATTN_BLK), 0)
    jj = lax.broadcasted_iota(jnp.int32, (ATTN_BLK, 2 * ATTN_BLK), 1)
    diff = ATTN_BLK + ii - jj
    band = (diff >= 0) & (diff <= ATTN_BLK)
    bias = -(slope * float(dil)) * diff.astype(F32)
    return band, bias, jj


def _attn_window(t, nblk):
    cur = pl.ds(pl.multiple_of(t * ATTN_BLK, ATTN_BLK), ATTN_BLK)
    prev = pl.ds(pl.multiple_of(jnp.maximum(t - 1, 0) * ATTN_BLK, ATTN_BLK), ATTN_BLK)
    first = jnp.where(t % nblk == 0, ATTN_BLK, 0)
    return prev, cur, first


def _unrolled_loop(n, step, init, unroll=ATTN_UNROLL):
    def trip(i, carry):
        for k in range(unroll):
            carry = step(i * unroll + k, carry)
        return carry

    return lax.fori_loop(0, n // unroll, trip, init)


def _streams(pairs, dil, s):
    if dil == 1:
        return [src for _, src in pairs]
    seg = s // dil
    for dst, src in pairs:
        for r in range(dil):
            dst[r * seg:(r + 1) * seg, :] = src[pl.ds(r, seg, stride=dil), :].astype(dst.dtype)
    return [dst for dst, _ in pairs]


def _attn_fwd(proj_a, slopes):
    _, s, d = proj_a.shape
    heads = d // HEAD_DIM
    scale = HEAD_DIM ** -0.5
    n_t = s // ATTN_BLK
    ng = len(DILATIONS)

    def body(q_ref, k_ref, v_ref, sl_ref, o_ref, lse_ref, qd, kd, vd, od, ld, og, lg):
        slope = sl_ref[...][:, :1]
        for g, dil in enumerate(DILATIONS):
            nblk = s // dil // ATTN_BLK
            qs, ks, vs = _streams([(qd, q_ref), (kd, k_ref), (vd, v_ref)], dil, s)
            o_t, l_t = (og.at[g], lg.at[g]) if dil == 1 else (od, ld)
            band, bias, jj = _attn_masks(slope, dil)

            def blk(t, carry, nblk=nblk, band=band, bias=bias, jj=jj, qs=qs, ks=ks, vs=vs, o_t=o_t, l_t=l_t):
                prev, cur, first = _attn_window(t, nblk)
                kk = jnp.concatenate([ks[prev, :], ks[cur, :]], axis=0)
                vv = jnp.concatenate([vs[prev, :], vs[cur, :]], axis=0)
                sc = _dot(qs[cur, :], kk, NT) * scale + bias
                sc = jnp.where(band & (jj >= first), sc, -jnp.inf)
                m = jnp.max(sc, axis=1, keepdims=True)
                p = jnp.exp(sc - m)
                l = jnp.sum(p, axis=1, keepdims=True)
                o_t[cur, :] = _dot(p, vv, NN) / l
                l_t[cur, :] = jnp.broadcast_to(m + jnp.log(l), (ATTN_BLK, HEAD_DIM))
                return carry

            _unrolled_loop(n_t, blk, 0)
            seg = s // dil
            if dil > 1:
                for r in range(dil):
                    og[g, pl.ds(r, seg, stride=dil), :] = od[r * seg:(r + 1) * seg, :]
                    lg[g, pl.ds(r, seg, stride=dil), :] = ld[r * seg:(r + 1) * seg, :]

        ch = 256

        def combine(c, carry):
            rows = pl.ds(pl.multiple_of(c * ch, ch), ch)
            ls = [lg[g, rows, :] for g in range(ng)]
            mx = functools.reduce(jnp.maximum, ls)
            es = [jnp.exp(x - mx) for x in ls]
            den = functools.reduce(jnp.add, es)
            num = functools.reduce(jnp.add, [es[g] * og[g, rows, :] for g in range(ng)])
            o_ref[rows, :] = (num / den).astype(o_ref.dtype)
            lse_ref[rows, :] = mx + jnp.log(den)
            return carry

        lax.fori_loop(0, s // ch, combine, 0)

    def col(slot):
        return pl.BlockSpec((None, s, HEAD_DIM), lambda h: (slot, 0, h))

    head = pl.BlockSpec((s, HEAD_DIM), lambda h: (0, h))
    return pl.pallas_call(
        body, name="attn_fwd", grid=(heads,),
        in_specs=[col(0), col(1), col(2), pl.BlockSpec((None, 1, HEAD_DIM), lambda h: (h, 0, 0))],
        out_specs=[head, head],
        out_shape=[jax.ShapeDtypeStruct((s, d), BF16), jax.ShapeDtypeStruct((s, d), F32)],
        scratch_shapes=[pltpu.VMEM((s, HEAD_DIM), BF16)] * 3 + [pltpu.VMEM((s, HEAD_DIM), F32)] * 2
        + [pltpu.VMEM((ng, s, HEAD_DIM), F32)] * 2,
        compiler_params=_params(VMEM_TILES_MIB),
    )(proj_a, proj_a, proj_a, slopes)


def _attn_bwd(proj_a, slopes, y_attn, lse, dy):
    _, s, d = proj_a.shape
    heads = d // HEAD_DIM
    scale = HEAD_DIM ** -0.5
    n_t = s // ATTN_BLK

    def body(q_ref, k_ref, v_ref, sl_ref, o_ref, lse_ref, dy_ref, out_ref,
             qd, kd, vd, dod, lsd, dld, delta, dqd, dkd, dvd, dqa, dka, dva):
        slope = sl_ref[...][:, :1]
        dyv = dy_ref[...]
        delta[...] = jnp.broadcast_to(
            jnp.sum(dyv * o_ref[...].astype(F32), axis=1, keepdims=True), (s, HEAD_DIM))
        for g, dil in enumerate(DILATIONS):
            nblk = s // dil // ATTN_BLK
            seg = s // dil
            qs, ks, vs, dos, lss, dls = _streams(
                [(qd, q_ref), (kd, k_ref), (vd, v_ref), (dod, dy_ref), (lsd, lse_ref), (dld, delta)], dil, s)
            dq_t, dk_t, dv_t = (dqa, dka, dva) if dil == 1 else (dqd, dkd, dvd)
            band, bias, jj = _attn_masks(slope, dil)

            def blk(t, carry, nblk=nblk, band=band, bias=bias, jj=jj, qs=qs, ks=ks, vs=vs, dos=dos, lss=lss,
                    dls=dls, dq_t=dq_t, dk_t=dk_t, dv_t=dv_t):
                ck, cv = carry
                prev, cur, first = _attn_window(t, nblk)
                q = qs[cur, :]
                do = dos[cur, :]
                lse_b = lss[cur, :]
                dl_b = dls[cur, :]
                kk = jnp.concatenate([ks[prev, :], ks[cur, :]], axis=0)
                vv = jnp.concatenate([vs[prev, :], vs[cur, :]], axis=0)
                sc = _dot(q, kk, NT) * scale + bias
                p = jnp.where(band & (jj >= first), jnp.exp(sc - jnp.concatenate([lse_b, lse_b], axis=1)), 0.0)
                dp = _dot(do, vv, NT)
                ds = p * (dp - jnp.concatenate([dl_b, dl_b], axis=1))
                dv_b = _dot(p, do, TN)
                dk_b = _dot(ds, q, TN) * scale
                dq_t[cur, :] = _dot(ds, kk, NN) * scale
                done = pl.ds(pl.multiple_of(jnp.where(t == 0, n_t, t - 1) * ATTN_BLK, ATTN_BLK), ATTN_BLK)
                dk_t[done, :] = ck + dk_b[:ATTN_BLK]
                dv_t[done, :] = cv + dv_b[:ATTN_BLK]
                return dk_b[ATTN_BLK:], dv_b[ATTN_BLK:]

            zero = jnp.zeros((ATTN_BLK, HEAD_DIM), F32)
            ck, cv = _unrolled_loop(n_t, blk, (zero, zero))
            dk_t[(n_t - 1) * ATTN_BLK:n_t * ATTN_BLK, :] = ck
            dv_t[(n_t - 1) * ATTN_BLK:n_t * ATTN_BLK, :] = cv
            if dil > 1:
                for acc, part in ((dqa, dqd), (dka, dkd), (dva, dvd)):
                    for r in range(dil):
                        acc[pl.ds(r, seg, stride=dil), :] += part[r * seg:(r + 1) * seg, :]
        out_ref[0] = dqa[...].astype(out_ref.dtype)
        out_ref[1] = dka[0:s, :].astype(out_ref.dtype)
        out_ref[2] = dva[0:s, :].astype(out_ref.dtype)

    def col(slot):
        return pl.BlockSpec((None, s, HEAD_DIM), lambda h: (slot, 0, h))

    head = pl.BlockSpec((s, HEAD_DIM), lambda h: (0, h))
    return pl.pallas_call(
        body, name="attn_bwd", grid=(heads,),
        in_specs=[col(0), col(1), col(2), pl.BlockSpec((None, 1, HEAD_DIM), lambda h: (h, 0, 0)),
                  head, head, head],
        out_specs=pl.BlockSpec((N_QKV, s, HEAD_DIM), lambda h: (0, 0, h)),
        out_shape=jax.ShapeDtypeStruct((N_QKV, s, d), BF16),
        scratch_shapes=[pltpu.VMEM((s, HEAD_DIM), BF16)] * 4 + [pltpu.VMEM((s, HEAD_DIM), F32)] * 4
        + [pltpu.VMEM((s + ATTN_BLK, HEAD_DIM), F32)] * 2 + [pltpu.VMEM((s, HEAD_DIM), F32)]
        + [pltpu.VMEM((s + ATTN_BLK, HEAD_DIM), F32)] * 2,
        compiler_params=_params(VMEM_TILES_MIB),
    )(proj_a, proj_a, proj_a, slopes, y_attn, lse, dy)


def _expm1(x):
    small = x * (1.0 + x * (0.5 + x * (1.0 / 6.0 + x * (1.0 / 24.0 + x * (1.0 / 120.0)))))
    return jnp.where(jnp.abs(x) < 0.1, small, jnp.exp(x) - 1.0)


def _softplus(x):
    return jnp.maximum(x, 0.0) + jnp.log1p(jnp.exp(-jnp.abs(x)))


GELU_K = 0.7978845608028654
GELU_C = 0.044715


def _gelu(x):
    t = jnp.tanh(GELU_K * (x + GELU_C * x * x * x))
    return 0.5 * x * (1.0 + t), t


def _gelu_grad(x, t):
    return 0.5 * (1.0 + t) + 0.5 * x * (1.0 - t * t) * GELU_K * (1.0 + 3.0 * GELU_C * x * x)


def _lru_gates(xc, wa, ba, wx, bx, sp):
    r = _sigmoid(_dot(xc, wa, NN) + ba)
    ig = _sigmoid(_dot(xc, wx, NN) + bx)
    log_a = -LRU_C * r * sp
    a = jnp.exp(log_a)
    mult = jnp.sqrt(-_expm1(2.0 * log_a))
    return r, ig, a, mult


def _scan_fwd(a, u, tt):
    row = lax.broadcasted_iota(jnp.int32, a.shape, 0)
    sh = 1
    while sh < tt:
        keep = row >= sh
        a_s = jnp.where(keep, pltpu.roll(a, sh, 0), 1.0)
        u_s = jnp.where(keep, pltpu.roll(u, sh, 0), 0.0)
        u = a * u_s + u
        a = a * a_s
        sh *= 2
    return a, u


def _scan_bwd(b, g, tt):
    row = lax.broadcasted_iota(jnp.int32, b.shape, 0)
    sh = 1
    while sh < tt:
        keep = row < tt - sh
        b_s = jnp.where(keep, pltpu.roll(b, tt - sh, 0), 1.0)
        g_s = jnp.where(keep, pltpu.roll(g, tt - sh, 0), 0.0)
        g = g + b * g_s
        b = b * b_s
        sh *= 2
    return b, g


ROW_CHUNK = 256


def _pad_copy(xpad_ref, x_ref, s):
    xpad_ref[0:8, :] = jnp.zeros((8, HEAD_DIM), F32)
    for c0 in range(0, s, ROW_CHUNK):
        xpad_ref[8 + c0:8 + c0 + ROW_CHUNK, :] = x_ref[c0:c0 + ROW_CHUNK, :]


def _conv_rows(dst_ref, xpad_ref, cw, cb, s):
    for c0 in range(0, s, ROW_CHUNK):
        acc = cb
        for j in range(CONV_TAPS):
            off = 8 - (CONV_TAPS - 1) + j + c0
            acc = acc + cw[j:j + 1, :] * xpad_ref[off:off + ROW_CHUNK, :]
        dst_ref[c0:c0 + ROW_CHUNK, :] = acc


LRU_TILE = 128
LRU_UNROLL = 16


def _lru_specs(s, d):
    heads = d // HEAD_DIM

    def col(slot):
        return pl.BlockSpec((None, s, HEAD_DIM), lambda h: (slot, 0, h))

    vec = pl.BlockSpec((1, HEAD_DIM), lambda h: (0, h))
    mat = pl.BlockSpec((None, HEAD_DIM, HEAD_DIM), lambda h: (h, 0, 0))
    cw = pl.BlockSpec((8, HEAD_DIM), lambda h: (0, h))
    head = pl.BlockSpec((s, HEAD_DIM), lambda h: (0, h))
    return heads, col, vec, mat, cw, head


def _lru_fwd(proj_b, conv_w, conv_b, wa, ba, wx, bx, lam):
    _, s, d = proj_b.shape
    heads, col, vec, mat, cws, head = _lru_specs(s, d)
    tt = LRU_TILE

    def body(xr_ref, xg_ref, cw_ref, cb_ref, wa_ref, ba_ref, wx_ref, bx_ref, lam_ref, y_ref, h_ref, xpad, xc_s):
        _pad_copy(xpad, xr_ref, s)
        _conv_rows(xc_s, xpad, cw_ref[...], cb_ref[...], s)
        sp = _softplus(-lam_ref[...])
        wav, wxv, bav, bxv = wa_ref[...], wx_ref[...], ba_ref[...], bx_ref[...]

        def tile(i, hc):
            rows = pl.ds(pl.multiple_of(i * tt, tt), tt)
            xc = xc_s[rows, :]
            _, ig, a, mult = _lru_gates(xc, wav, bav, wxv, bxv, sp)
            pa, hl = _scan_fwd(a, mult * (ig * xc), tt)
            h = hl + pa * hc
            h_ref[rows, :] = h
            gel, _ = _gelu(xg_ref[rows, :])
            y_ref[rows, :] = (h * gel).astype(y_ref.dtype)
            return h[tt - 1:tt, :]

        _unrolled_loop(s // tt, tile, jnp.zeros((1, HEAD_DIM), F32), LRU_UNROLL)

    return pl.pallas_call(
        body, name="lru_fwd", grid=(heads,),
        in_specs=[col(N_QKV), col(N_QKV + 1), cws, vec, mat, vec, mat, vec, vec],
        out_specs=[head, head],
        out_shape=[jax.ShapeDtypeStruct((s, d), BF16), jax.ShapeDtypeStruct((s, d), F32)],
        scratch_shapes=[pltpu.VMEM((s + 8, HEAD_DIM), F32), pltpu.VMEM((s, HEAD_DIM), F32)],
        compiler_params=_params(VMEM_STREAM_MIB),
    )(proj_b, proj_b, conv_w, conv_b, wa, ba, wx, bx, lam)


def _lru_bwd(proj_b, h_lru, dy, conv_w, conv_b, wa, ba, wx, bx, lam, dproj_b):
    _, s, d = proj_b.shape
    heads, col, vec, mat, cws, head = _lru_specs(s, d)
    tt = LRU_TILE
    n_t = s // tt

    def body(xr_ref, xg_ref, h_ref, dy_ref, cw_ref, cb_ref, wa_ref, ba_ref, wx_ref, bx_ref, lam_ref, alias_ref,
             out_ref, dcw_ref, dcb_ref, dwa_ref, dba_ref, dwx_ref, dbx_ref, dlam_ref, xpad, xc_s, dxc_s):
        del alias_ref
        _pad_copy(xpad, xr_ref, s)
        cwv = cw_ref[...]
        _conv_rows(xc_s, xpad, cwv, cb_ref[...], s)
        dxc_s[s:s + 8, :] = jnp.zeros((8, HEAD_DIM), F32)
        lamv = lam_ref[...]
        sp = _softplus(-lamv)
        wav, wxv, bav, bxv = wa_ref[...], wx_ref[...], ba_ref[...], bx_ref[...]
        dwa_ref[...] = jnp.zeros_like(dwa_ref)
        dwx_ref[...] = jnp.zeros_like(dwx_ref)
        zero = jnp.zeros((1, HEAD_DIM), F32)
        row = lax.broadcasted_iota(jnp.int32, (tt, HEAD_DIM), 0)

        def tile(it, carry):
            dh_next, a_next, dba, dbx, dsp, dcb = carry
            i = n_t - 1 - it
            t0 = pl.multiple_of(i * tt, tt)
            rows = pl.ds(t0, tt)
            xc = xc_s[rows, :]
            r, ig, a, mult = _lru_gates(xc, wav, bav, wxv, bxv, sp)
            h = h_ref[rows, :]
            before = h_ref[pl.ds(pl.multiple_of(jnp.maximum(t0 - 8, 0), 8), 8), :][7:8, :]
            before = before * (i > 0).astype(F32)
            h_prev = jnp.where(row == 0, before, pltpu.roll(h, 1, 0))
            xg = xg_ref[rows, :]
            dyv = dy_ref[rows, :]
            gel, th = _gelu(xg)
            out_ref[1, rows, :] = (dyv * h * _gelu_grad(xg, th)).astype(out_ref.dtype)
            b = jnp.where(row == tt - 1, a_next, pltpu.roll(a, tt - 1, 0))
            pb, z = _scan_bwd(b, dyv * gel, tt)
            dh = z + pb * dh_next
            da = dh * h_prev
            dmult = dh * (ig * xc)
            dig = dh * (mult * xc)
            dla = da * a - dmult * (a * a / mult)
            dzr = dla * (-LRU_C * sp) * (r * (1.0 - r))
            dzx = dig * (ig * (1.0 - ig))
            dxc = dh * (mult * ig) + _dot(dzr, wav, NT) + _dot(dzx, wxv, NT)
            dxc_s[rows, :] = dxc
            dwa_ref[...] += _dot(xc, dzr, TN)
            dwx_ref[...] += _dot(xc, dzx, TN)
            return (dh[0:1, :], a[0:1, :],
                    dba + jnp.sum(dzr, axis=0, keepdims=True),
                    dbx + jnp.sum(dzx, axis=0, keepdims=True),
                    dsp + jnp.sum(dla * (-LRU_C * r), axis=0, keepdims=True),
                    dcb + jnp.sum(dxc, axis=0, keepdims=True))

        _, _, dba, dbx, dsp, dcb = _unrolled_loop(n_t, tile, (zero, zero, zero, zero, zero, zero), LRU_UNROLL)
        dba_ref[...] = dba
        dbx_ref[...] = dbx
        dcb_ref[...] = dcb
        dlam_ref[...] = -dsp * _sigmoid(-lamv)
        dcw = [zero] * CONV_TAPS
        for c0 in range(0, s, ROW_CHUNK):
            dxc_c = dxc_s[c0:c0 + ROW_CHUNK, :]
            dxr = jnp.zeros((ROW_CHUNK, HEAD_DIM), F32)
            for j in range(CONV_TAPS):
                back = CONV_TAPS - 1 - j
                off = 8 - back + c0
                dcw[j] = dcw[j] + jnp.sum(dxc_c * xpad[off:off + ROW_CHUNK, :], axis=0, keepdims=True)
                dxr = dxr + cwv[j:j + 1, :] * dxc_s[back + c0:back + c0 + ROW_CHUNK, :]
            out_ref[0, c0:c0 + ROW_CHUNK, :] = dxr.astype(out_ref.dtype)
        for j in range(CONV_TAPS):
            dcw_ref[j:j + 1, :] = dcw[j]

    return pl.pallas_call(
        body, name="lru_bwd", grid=(heads,),
        in_specs=[col(N_QKV), col(N_QKV + 1), head, head, cws, vec, mat, vec, mat, vec, vec,
                  pl.BlockSpec(memory_space=pl.ANY)],
        out_specs=[pl.BlockSpec((2, s, HEAD_DIM), lambda h: (0, 0, h)),
                   pl.BlockSpec((CONV_TAPS, HEAD_DIM), lambda h: (0, h)), vec, mat, vec, mat, vec, vec],
        out_shape=[jax.ShapeDtypeStruct(dproj_b.shape, dproj_b.dtype),
                   jax.ShapeDtypeStruct((CONV_TAPS, d), F32), jax.ShapeDtypeStruct((1, d), F32),
                   jax.ShapeDtypeStruct(wa.shape, F32), jax.ShapeDtypeStruct((1, d), F32),
                   jax.ShapeDtypeStruct(wx.shape, F32), jax.ShapeDtypeStruct((1, d), F32),
                   jax.ShapeDtypeStruct((1, d), F32)],
        scratch_shapes=[pltpu.VMEM((s + 8, HEAD_DIM), F32), pltpu.VMEM((s, HEAD_DIM), F32),
                        pltpu.VMEM((s + 8, HEAD_DIM), F32)],
        input_output_aliases={11: 0},
        compiler_params=_params(VMEM_STREAM_MIB),
    )(proj_b, proj_b, h_lru, dy, conv_w, conv_b, wa, ba, wx, bx, lam, dproj_b)


def _place():
    x, y, c = (lax.axis_index(n) for n in AXES)
    return x, y, c


def _other_chips(x, y):
    return [(1 - x, y), (x, 1 - y), (1 - x, 1 - y)]


HBM = pl.BlockSpec(memory_space=pl.ANY)


def _cast_shard(w, chip_arr, name):
    r, cols = w.shape
    rh = r // 2
    tr = _row_tile(rh, cols * 4, STREAM_TILE)
    nt = rh // tr

    def body(chip_ref, w_ref, o_ref):
        del chip_ref
        o_ref[...] = w_ref[...].astype(BF16)

    return pl.pallas_call(
        body, name=name,
        grid_spec=pltpu.PrefetchScalarGridSpec(
            num_scalar_prefetch=1, grid=(2, nt),
            in_specs=[pl.BlockSpec((tr, cols), lambda h, i, chip_ref: (h * nt + i, 0))],
            out_specs=pl.BlockSpec((None, None, tr, cols), lambda h, i, chip_ref: (chip_ref[0], h, i, 0))),
        out_shape=pltpu.HBM((N_CHIPS, 2, rh, cols), BF16), compiler_params=_params(VMEM_STREAM_MIB),
    )(chip_arr, w)


HBM_SPEC = pl.BlockSpec(memory_space=pltpu.HBM)
SEM_SPEC = pl.BlockSpec(memory_space=pltpu.SEMAPHORE)
EFFECT = pltpu.SideEffectType.DATAFLOW_SIDE_EFFECTING
TOKEN = jax.ShapeDtypeStruct((8, 128), F32)
TOKEN_SPEC = pl.BlockSpec(memory_space=pltpu.VMEM)


def _in_hbm(arrays):
    return [pltpu.with_memory_space_constraint(a, pltpu.HBM) for a in arrays]


def _hbm_like(arrays):
    return [pltpu.HBM(a.shape, a.dtype) for a in arrays]


def _sems(n):
    return pltpu.SemaphoreType.DMA((n,))


def _remote(src, dst, send_sem, recv_sem, to):
    return pltpu.make_async_remote_copy(src_ref=src, dst_ref=dst, send_sem=send_sem, recv_sem=recv_sem,
                                        device_id=to, device_id_type=MESH)


ALL_FLIPS = (0, 1, 2)


def _gather_start(bufs, groups, after, name):
    n = len(bufs)
    ng = len(groups)

    def body(*refs):
        ins = refs[:n]
        sems = refs[n + len(after):n + len(after) + 2 * ng]
        x, y, c = _place()
        me = 2 * x + y
        chips = _other_chips(x, y)
        for g, (ws, flips) in enumerate(groups):
            for i, w in enumerate(ws):
                for jj, j in enumerate(flips):
                    k = len(flips) * i + jj
                    mine = ins[w].at[me, c]
                    _remote(mine, mine, sems[2 * g].at[k], sems[2 * g + 1].at[k], (*chips[j], c)).start()

    sem_shapes = []
    for ws, flips in groups:
        sem_shapes += [_sems(len(flips) * len(ws))] * 2
    res = pl.pallas_call(
        body, name=name, in_specs=[HBM_SPEC] * n + [HBM] * len(after),
        out_specs=[SEM_SPEC] * (2 * ng) + [HBM_SPEC] * n, out_shape=sem_shapes + _hbm_like(bufs),
        input_output_aliases={w: 2 * ng + w for w in range(n)},
        compiler_params=pltpu.CompilerParams(has_side_effects=EFFECT),
    )(*_in_hbm(bufs), *after)
    return [(res[2 * g], res[2 * g + 1]) for g in range(ng)], list(res[2 * ng:])


def _gather_forward(bufs, recv, after, name, flips=ALL_FLIPS):
    m = len(bufs)
    nf = len(flips)

    def body(*refs):
        ins, recv_in = refs[:m], refs[m]
        fsend, frecv = refs[m + 1 + len(after)], refs[m + 2 + len(after)]
        x, y, c = _place()
        chips = _other_chips(x, y)
        for jj, j in enumerate(flips):
            cx, cy = chips[j]
            for i in range(m):
                landed = ins[i].at[2 * cx + cy, c]
                k = nf * i + jj
                _remote(landed, landed, fsend.at[k], recv_in.at[k], (cx, cy, c)).wait_recv()
                _remote(landed, landed, fsend.at[k], frecv.at[k], (x, y, 1 - c)).start()

    res = pl.pallas_call(
        body, name=name, in_specs=[HBM_SPEC] * m + [SEM_SPEC] + [HBM] * len(after),
        out_specs=[SEM_SPEC, SEM_SPEC] + [HBM_SPEC] * m, out_shape=[_sems(nf * m), _sems(nf * m)] + _hbm_like(bufs),
        input_output_aliases={i: 2 + i for i in range(m)},
        compiler_params=pltpu.CompilerParams(has_side_effects=EFFECT),
    )(*bufs, recv, *after)
    return (res[0], res[1]), list(res[2:])


NEIGHBOURS = (0, 1)


def _relay_partner(x, y, c):
    return 2 * (x ^ (1 - c)) + (y ^ c), (x ^ c, y ^ (1 - c))


def _gather_forward_relay(bufs, recv, after, name):
    m = len(bufs)
    nf = len(NEIGHBOURS)

    def body(*refs):
        ins, recv_in = refs[:m], refs[m]
        fsend, frecv, rsend, rrecv = refs[m + 1 + len(after):m + 5 + len(after)]
        x, y, c = _place()
        chips = _other_chips(x, y)
        for jj, j in enumerate(NEIGHBOURS):
            cx, cy = chips[j]
            for i in range(m):
                landed = ins[i].at[2 * cx + cy, c]
                k = nf * i + jj
                _remote(landed, landed, fsend.at[k], recv_in.at[k], (cx, cy, c)).wait_recv()
        row, (px, py) = _relay_partner(x, y, c)
        for i in range(m):
            relayed = ins[i].at[row, c]
            _remote(relayed, relayed, rsend.at[i], rrecv.at[i], (px, py, c)).start()
        for jj, j in enumerate(NEIGHBOURS):
            cx, cy = chips[j]
            for i in range(m):
                landed = ins[i].at[2 * cx + cy, c]
                k = nf * i + jj
                _remote(landed, landed, fsend.at[k], frecv.at[k], (x, y, 1 - c)).start()

    res = pl.pallas_call(
        body, name=name, in_specs=[HBM_SPEC] * m + [SEM_SPEC] + [HBM] * len(after),
        out_specs=[SEM_SPEC] * 4 + [HBM_SPEC] * m,
        out_shape=[_sems(nf * m), _sems(nf * m), _sems(m), _sems(m)] + _hbm_like(bufs),
        input_output_aliases={i: 4 + i for i in range(m)},
        compiler_params=pltpu.CompilerParams(has_side_effects=EFFECT),
    )(*bufs, recv, *after)
    return tuple(res[:4]), list(res[4:])


def _gather_forward_diag(bufs, rrecv, after, name):
    m = len(bufs)

    def body(*refs):
        ins, rrecv_in = refs[:m], refs[m]
        dsend, drecv = refs[m + 1 + len(after)], refs[m + 2 + len(after)]
        x, y, c = _place()
        diag = (2 * x + y) ^ 3
        _, (px, py) = _relay_partner(x, y, c)
        for i in range(m):
            landed = ins[i].at[diag, c]
            _remote(landed, landed, dsend.at[i], rrecv_in.at[i], (px, py, c)).wait_recv()
            _remote(landed, landed, dsend.at[i], drecv.at[i], (x, y, 1 - c)).start()

    res = pl.pallas_call(
        body, name=name, in_specs=[HBM_SPEC] * m + [SEM_SPEC] + [HBM] * len(after),
        out_specs=[SEM_SPEC, SEM_SPEC] + [HBM_SPEC] * m, out_shape=[_sems(m), _sems(m)] + _hbm_like(bufs),
        input_output_aliases={i: 2 + i for i in range(m)},
        compiler_params=pltpu.CompilerParams(has_side_effects=EFFECT),
    )(*bufs, rrecv, *after)
    return (res[0], res[1]), list(res[2:])


def _gather_finish_diag(bufs, rsend, dsend, drecv, after, name):
    m = len(bufs)

    def body(*refs):
        ins = refs[:m]
        rsend_in, dsend_in, drecv_in = refs[m:m + 3]
        x, y, c = _place()
        diag = (2 * x + y) ^ 3
        row, (px, py) = _relay_partner(x, y, c)
        for i in range(m):
            relayed = ins[i].at[row, c]
            _remote(relayed, relayed, rsend_in.at[i], drecv_in.at[i], (px, py, c)).wait_send()
            landed = ins[i].at[diag, c]
            _remote(landed, landed, dsend_in.at[i], drecv_in.at[i], (x, y, 1 - c)).wait_send()
            theirs = ins[i].at[diag, 1 - c]
            _remote(theirs, theirs, dsend_in.at[i], drecv_in.at[i], (x, y, 1 - c)).wait_recv()

    return list(pl.pallas_call(
        body, name=name, in_specs=[HBM_SPEC] * m + [SEM_SPEC] * 3 + [HBM] * len(after),
        out_specs=[HBM_SPEC] * m, out_shape=_hbm_like(bufs),
        input_output_aliases={i: i for i in range(m)},
        compiler_params=pltpu.CompilerParams(has_side_effects=EFFECT),
    )(*bufs, rsend, dsend, drecv, *after))


def _gather_finish(bufs, send, fsend, frecv, after, name, flips=ALL_FLIPS):
    m = len(bufs)
    nf = len(flips)

    def body(*refs):
        ins = refs[:m]
        send_in, fsend_in, frecv_in = refs[m:m + 3]
        x, y, c = _place()
        me = 2 * x + y
        chips = _other_chips(x, y)
        for jj, j in enumerate(flips):
            cx, cy = chips[j]
            cj = 2 * cx + cy
            for i in range(m):
                k = nf * i + jj
                mine = ins[i].at[me, c]
                _remote(mine, mine, send_in.at[k], frecv_in.at[k], (cx, cy, c)).wait_send()
                landed = ins[i].at[cj, c]
                _remote(landed, landed, fsend_in.at[k], frecv_in.at[k], (x, y, 1 - c)).wait_send()
                theirs = ins[i].at[cj, 1 - c]
                _remote(theirs, theirs, fsend_in.at[k], frecv_in.at[k], (x, y, 1 - c)).wait_recv()

    return list(pl.pallas_call(
        body, name=name, in_specs=[HBM_SPEC] * m + [SEM_SPEC] * 3 + [HBM] * len(after),
        out_specs=[HBM_SPEC] * m, out_shape=_hbm_like(bufs),
        input_output_aliases={i: i for i in range(m)},
        compiler_params=pltpu.CompilerParams(has_side_effects=EFFECT),
    )(*bufs, send, fsend, frecv, *after))


def _pair_exchange(grads, name):
    n = len(grads)

    def body(*refs):
        ins, outs = refs[:n], refs[n:2 * n]
        send_sems, recv_sems = refs[2 * n:]
        x, y, c = _place()
        sibling = (x, y, 1 - c)
        cps = []
        for w in range(n):
            for j in range(N_CHIPS):
                cp = pltpu.make_async_remote_copy(
                    src_ref=ins[w].at[j, 1 - c], dst_ref=outs[w].at[j], send_sem=send_sems.at[N_CHIPS * w + j],
                    recv_sem=recv_sems.at[N_CHIPS * w + j], device_id=sibling, device_id_type=MESH)
                cp.start()
                cps.append(cp)
        for cp in cps:
            cp.wait()

    return pl.pallas_call(
        body, name=name, in_specs=[HBM] * n, out_specs=[HBM] * n,
        out_shape=[jax.ShapeDtypeStruct((N_CHIPS,) + a.shape[2:], a.dtype) for a in grads],
        scratch_shapes=[pltpu.SemaphoreType.DMA((N_CHIPS * n,)), pltpu.SemaphoreType.DMA((N_CHIPS * n,))],
    )(*grads)


def _pair_start(grads, name):
    n = len(grads)
    lands = [lax.empty((N_CHIPS,) + a.shape[2:], a.dtype) for a in grads]

    def body(*refs):
        ins, land_in = refs[:n], refs[n:2 * n]
        send, recv = refs[2 * n], refs[2 * n + 1]
        token = refs[4 * n + 2]
        x, y, c = _place()
        for w in range(n):
            for j in range(N_CHIPS):
                k = N_CHIPS * w + j
                _remote(ins[w].at[j, 1 - c], land_in[w].at[j], send.at[k], recv.at[k], (x, y, 1 - c)).start()
        token[...] = jnp.zeros_like(token)

    res = pl.pallas_call(
        body, name=name, in_specs=[HBM_SPEC] * (2 * n),
        out_specs=[SEM_SPEC, SEM_SPEC] + [HBM_SPEC] * (2 * n) + [TOKEN_SPEC],
        out_shape=[_sems(N_CHIPS * n), _sems(N_CHIPS * n)] + _hbm_like(grads) + _hbm_like(lands) + [TOKEN],
        input_output_aliases={i: 2 + i for i in range(2 * n)},
        compiler_params=pltpu.CompilerParams(has_side_effects=EFFECT),
    )(*_in_hbm(grads), *_in_hbm(lands))
    return (res[0], res[1]), list(res[2:2 + n]), list(res[2 + n:2 + 2 * n]), res[2 + 2 * n]


def _pair_wait(sems, grads, lands, after, name):
    n = len(grads)

    def body(*refs):
        ins, land_in = refs[:n], refs[n:2 * n]
        send_in, recv_in = refs[2 * n], refs[2 * n + 1]
        x, y, c = _place()
        for w in range(n):
            for j in range(N_CHIPS):
                k = N_CHIPS * w + j
                cp = _remote(ins[w].at[j, 1 - c], land_in[w].at[j], send_in.at[k], recv_in.at[k], (x, y, 1 - c))
                cp.wait_send()
                cp.wait_recv()

    res = pl.pallas_call(
        body, name=name, in_specs=[HBM_SPEC] * (2 * n) + [SEM_SPEC, SEM_SPEC] + [HBM] * len(after),
        out_specs=[HBM_SPEC] * (2 * n), out_shape=_hbm_like(grads) + _hbm_like(lands),
        input_output_aliases={i: i for i in range(2 * n)},
        compiler_params=pltpu.CompilerParams(has_side_effects=EFFECT),
    )(*grads, *lands, sems[0], sems[1], *after)
    return list(res[:n]), list(res[n:])


def _chip_start(sums, name, to_all=()):
    m = len(sums)
    lands = [lax.empty(((N_CHIPS,) if i in to_all else ()) + a.shape, a.dtype) for i, a in enumerate(sums)]

    def body(*refs):
        ins, land_in = refs[:m], refs[m:2 * m]
        send, recv = refs[2 * m], refs[2 * m + 1]
        token = refs[4 * m + 2]
        x, y, c = _place()
        me = 2 * x + y
        for i in sorted(range(m), key=lambda i: i not in to_all):
            for j, (cx, cy) in enumerate(_other_chips(x, y)):
                src = ins[i] if i in to_all else ins[i].at[2 * cx + cy]
                _remote(src, land_in[i].at[me], send.at[3 * i + j], recv.at[3 * i + j], (cx, cy, c)).start()
        token[...] = jnp.zeros_like(token)

    res = pl.pallas_call(
        body, name=name, in_specs=[HBM_SPEC] * (2 * m),
        out_specs=[SEM_SPEC, SEM_SPEC] + [HBM_SPEC] * (2 * m) + [TOKEN_SPEC],
        out_shape=[_sems(3 * m), _sems(3 * m)] + _hbm_like(sums) + _hbm_like(lands) + [TOKEN],
        input_output_aliases={i: 2 + i for i in range(2 * m)},
        compiler_params=pltpu.CompilerParams(has_side_effects=EFFECT),
    )(*_in_hbm(sums), *_in_hbm(lands))
    return (res[0], res[1]), list(res[2:2 + m]), list(res[2 + m:2 + 2 * m]), res[2 + 2 * m]


def _chip_wait(sems, sums, lands, after, name, to_all=()):
    m = len(sums)

    def body(*refs):
        ins, land_in = refs[:m], refs[m:2 * m]
        send_in, recv_in = refs[2 * m], refs[2 * m + 1]
        x, y, c = _place()
        for i in range(m):
            for j, (cx, cy) in enumerate(_other_chips(x, y)):
                cj = 2 * cx + cy
                src = ins[i] if i in to_all else ins[i].at[cj]
                cp = _remote(src, land_in[i].at[cj], send_in.at[3 * i + j], recv_in.at[3 * i + j], (cx, cy, c))
                cp.wait_send()
                cp.wait_recv()

    res = pl.pallas_call(
        body, name=name, in_specs=[HBM_SPEC] * (2 * m) + [SEM_SPEC, SEM_SPEC] + [HBM] * len(after),
        out_specs=[HBM_SPEC] * (2 * m), out_shape=_hbm_like(sums) + _hbm_like(lands),
        input_output_aliases={i: i for i in range(2 * m)},
        compiler_params=pltpu.CompilerParams(has_side_effects=EFFECT),
    )(*sums, *lands, sems[0], sems[1], *after)
    return list(res[:m]), list(res[m:])


def _half_parts(bufs):
    parts = []
    for w, a in enumerate(bufs):
        parts += [(w, None)] if a.ndim == 3 else [(w, j) for j in range(a.shape[0])]
    return parts


def _half_ref(refs, w, j, h):
    return refs[w].at[h] if j is None else refs[w].at[j, h]


def _half_start(bufs, name):
    n = len(bufs)
    parts = _half_parts(bufs)

    def body(*refs):
        ins = refs[:n]
        send, recv = refs[n], refs[n + 1]
        x, y, c = _place()
        for k, (w, j) in enumerate(parts):
            mine = _half_ref(ins, w, j, c)
            _remote(mine, mine, send.at[k], recv.at[k], (x, y, 1 - c)).start()

    res = pl.pallas_call(
        body, name=name, in_specs=[HBM_SPEC] * n, out_specs=[SEM_SPEC, SEM_SPEC] + [HBM_SPEC] * n,
        out_shape=[_sems(len(parts)), _sems(len(parts))] + _hbm_like(bufs),
        input_output_aliases={w: 2 + w for w in range(n)},
        compiler_params=pltpu.CompilerParams(has_side_effects=EFFECT),
    )(*_in_hbm(bufs))
    return (res[0], res[1]), list(res[2:])


def _half_wait(sems, bufs, after, name):
    n = len(bufs)
    parts = _half_parts(bufs)

    def body(*refs):
        ins = refs[:n]
        send_in, recv_in = refs[n], refs[n + 1]
        x, y, c = _place()
        for k, (w, j) in enumerate(parts):
            mine = _half_ref(ins, w, j, c)
            _remote(mine, mine, send_in.at[k], recv_in.at[k], (x, y, 1 - c)).wait_send()
            theirs = _half_ref(ins, w, j, 1 - c)
            _remote(theirs, theirs, send_in.at[k], recv_in.at[k], (x, y, 1 - c)).wait_recv()

    return list(pl.pallas_call(
        body, name=name, in_specs=[HBM_SPEC] * n + [SEM_SPEC, SEM_SPEC] + [HBM] * len(after),
        out_specs=[HBM_SPEC] * n, out_shape=_hbm_like(bufs),
        input_output_aliases={w: w for w in range(n)},
        compiler_params=pltpu.CompilerParams(has_side_effects=EFFECT),
    )(*bufs, sems[0], sems[1], *after))


def _all_gather8(block, name, after=()):
    def body(in_ref, *rest):
        out_ref, send_sems, recv_sems, local_sem = rest[len(after):]
        x, y, c = _place()
        me = 4 * x + 2 * y + c
        mine = pltpu.make_async_copy(in_ref, out_ref.at[me], local_sem)
        mine.start()
        flips = [(fx, fy, fc) for fx in (0, 1) for fy in (0, 1) for fc in (0, 1)][1:]
        cps = []
        for k, (fx, fy, fc) in enumerate(flips):
            cp = pltpu.make_async_remote_copy(
                src_ref=in_ref, dst_ref=out_ref.at[me], send_sem=send_sems.at[k], recv_sem=recv_sems.at[k],
                device_id=(x ^ fx, y ^ fy, c ^ fc), device_id_type=MESH)
            cp.start()
            cps.append(cp)
        for k, (fx, fy, fc) in enumerate(flips):
            px, py, pc = x ^ fx, y ^ fy, c ^ fc
            theirs = out_ref.at[4 * px + 2 * py + pc]
            pltpu.make_async_remote_copy(
                src_ref=theirs, dst_ref=theirs, send_sem=send_sems.at[k], recv_sem=recv_sems.at[k],
                device_id=(px, py, pc), device_id_type=MESH).wait_recv()
        for cp in cps:
            cp.wait_send()
        mine.wait()

    return pl.pallas_call(
        body, name=name, in_specs=[HBM] * (1 + len(after)), out_specs=HBM,
        out_shape=jax.ShapeDtypeStruct((N_DEV,) + block.shape, block.dtype),
        scratch_shapes=[pltpu.SemaphoreType.DMA((N_DEV - 1,)), pltpu.SemaphoreType.DMA((N_DEV - 1,)),
                        pltpu.SemaphoreType.DMA],
    )(block, *after)


def _pair_sum(grad, recv, c_arr, name):
    _, _, rh, cols = grad.shape
    tr = _row_tile(rh, cols * grad.dtype.itemsize, STREAM_TILE // 2)

    def body(c_ref, g_ref, r_ref, o_ref):
        del c_ref
        o_ref[...] = (g_ref[...].astype(F32) + r_ref[...].astype(F32)).astype(o_ref.dtype)

    spec = pl.BlockSpec((None, tr, cols), lambda j, i, c_ref: (j, i, 0))
    return pl.pallas_call(
        body, name=name,
        grid_spec=pltpu.PrefetchScalarGridSpec(
            num_scalar_prefetch=1, grid=(N_CHIPS, rh // tr),
            in_specs=[pl.BlockSpec((None, None, tr, cols), lambda j, i, c_ref: (j, c_ref[0], i, 0)), spec],
            out_specs=spec),
        out_shape=pltpu.HBM(recv.shape, grad.dtype), compiler_params=_params(VMEM_STREAM_MIB),
    )(c_arr, grad, recv)


def _sum_by_chip(chip, p_ref, own_ref, o_ref):
    o_ref[...] = jnp.zeros_like(o_ref)
    for k in range(N_CHIPS):
        @pl.when(chip == k)
        def _():
            o_ref[...] += own_ref[...].astype(F32)

        @pl.when(chip != k)
        def _(k=k):
            o_ref[...] += p_ref[k].astype(F32)


def _chip_sum_all(parts, own, place_arr, name):
    _, nj, rh, cols = parts.shape
    tr = _row_tile(rh, cols * 4, STREAM_TILE // 2)

    def body(place_ref, p_ref, own_ref, o_ref):
        _sum_by_chip(place_ref[0], p_ref, own_ref, o_ref)

    return pl.pallas_call(
        body, name=name,
        grid_spec=pltpu.PrefetchScalarGridSpec(
            num_scalar_prefetch=1, grid=(nj, rh // tr),
            in_specs=[pl.BlockSpec((N_CHIPS, None, tr, cols), lambda j, i, place_ref: (0, j, i, 0)),
                      pl.BlockSpec((None, tr, cols), lambda j, i, place_ref: (j, i, 0))],
            out_specs=pl.BlockSpec((None, None, tr, cols), lambda j, i, place_ref: (j, place_ref[1], i, 0))),
        out_shape=pltpu.HBM((nj, 2, rh, cols), F32), compiler_params=_params(VMEM_STREAM_MIB),
    )(place_arr, parts, own)


def _chip_sum(parts, own, place_arr, name):
    _, rh, cols = parts.shape
    tr = _row_tile(rh, cols * 4, STREAM_TILE // 2)

    def body(place_ref, p_ref, own_ref, o_ref):
        _sum_by_chip(place_ref[0], p_ref, own_ref, o_ref)

    return pl.pallas_call(
        body, name=name,
        grid_spec=pltpu.PrefetchScalarGridSpec(
            num_scalar_prefetch=1, grid=(rh // tr,),
            in_specs=[pl.BlockSpec((N_CHIPS, tr, cols), lambda i, place_ref: (0, i, 0)),
                      pl.BlockSpec((None, tr, cols), lambda i, place_ref: (place_ref[0], i, 0))],
            out_specs=pl.BlockSpec((None, tr, cols), lambda i, place_ref: (place_ref[1], i, 0))),
        out_shape=pltpu.HBM((2, rh, cols), F32), compiler_params=_params(VMEM_STREAM_MIB),
    )(place_arr, parts, own)


def _adamw_math(w, g, m, v):
    m = ADAM_B1 * m + (1.0 - ADAM_B1) * g
    v = ADAM_B2 * v + (1.0 - ADAM_B2) * (g * g)
    m_hat = m / (1.0 - ADAM_B1 ** ADAM_STEP)
    v_hat = v / (1.0 - ADAM_B2 ** ADAM_STEP)
    delta = -ADAM_LR * (m_hat / (jnp.sqrt(v_hat) + ADAM_EPS) + ADAM_WD * w)
    return delta, m, v


def _adamw(w, g, m, v, name):
    rows, cols = w.shape
    tr = _row_tile(rows, cols * 4)

    def body(w_ref, g_ref, m_ref, v_ref, go_ref, d_ref, nm_ref, nv_ref):
        gv = g_ref[...]
        go_ref[...] = gv
        d_ref[...], nm_ref[...], nv_ref[...] = _adamw_math(w_ref[...], gv, m_ref[...], v_ref[...])

    spec = pl.BlockSpec((tr, cols), lambda i: (i, 0))
    return pl.pallas_call(
        body, name=name, grid=(rows // tr,), in_specs=[spec] * 4, out_specs=[spec] * 4,
        out_shape=[jax.ShapeDtypeStruct(w.shape, F32)] * 4, compiler_params=_params(VMEM_STREAM_MIB),
    )(w, g, m, v)


def _sum8_adamw_row(parts, w, m, v, name):
    cols = parts.shape[2]

    def body(p_ref, w_ref, m_ref, v_ref, g_ref, d_ref, nm_ref, nv_ref):
        g = p_ref[0, 0:1, :]
        for k in range(1, N_DEV):
            g = g + p_ref[k, 0:1, :]
        g_ref[...] = g
        d_ref[...], nm_ref[...], nv_ref[...] = _adamw_math(w_ref[...], g, m_ref[...], v_ref[...])

    return pl.pallas_call(
        body, name=name, out_shape=[jax.ShapeDtypeStruct((1, cols), F32)] * 4, compiler_params=_params(VMEM_STREAM_MIB),
    )(parts, w, m, v)


def _pack_rows(pieces, rows, name):
    cols = pieces[0].shape[1]
    n = len(pieces)

    def body(*refs):
        o_ref = refs[n]
        o_ref[...] = jnp.zeros_like(o_ref)
        at = 0
        for p_ref in refs[:n]:
            r = p_ref.shape[0]
            o_ref[at:at + r, :] = p_ref[...]
            at += r

    return pl.pallas_call(
        body, name=name, out_shape=jax.ShapeDtypeStruct((rows, cols), F32), compiler_params=_params(VMEM_STREAM_MIB),
    )(*pieces)


def kernel(x, norm_mix_g, w_in, conv_w, conv_b, lru_wa, lru_ba, lru_wx, lru_bx, lru_lambda, w_proj_attn, w_proj_lru, w_out, norm_mlp_g, w_up, w_down, norm_final_g, loss_target, m_norm_mix_g, m_w_in, m_conv_w, m_conv_b, m_lru_wa, m_lru_ba, m_lru_wx, m_lru_bx, m_lru_lambda, m_w_proj_attn, m_w_proj_lru, m_w_out, m_norm_mlp_g, m_w_up, m_w_down, m_norm_final_g, v_norm_mix_g, v_w_in, v_conv_w, v_conv_b, v_lru_wa, v_lru_ba, v_lru_wx, v_lru_bx, v_lru_lambda, v_w_proj_attn, v_w_proj_lru, v_w_out, v_norm_mlp_g, v_w_up, v_w_down, v_norm_final_g):
    s, d = x.shape[1], x.shape[2]
    ff = w_up.shape[2] * N_CHIPS
    heads = d // HEAD_DIM
    u = d // 4
    assert s % (max(DILATIONS) * ATTN_BLK) == 0 and d % (4 * HEAD_DIM) == 0 and ff == 4 * d and DILATIONS[0] == 1
    xs, target = _in_hbm([x[0], loss_target[0]])
    gf = norm_final_g.reshape(1, d)
    wa, wx = lru_wa[0], lru_wx[0]
    core = lax.axis_index("c").astype(jnp.int32)
    chip = (2 * lax.axis_index("x") + lax.axis_index("y")).astype(jnp.int32)
    cidx = core.reshape(1)
    chip_arr = chip.reshape(1)
    place_arr = jnp.stack([chip, core])
    slopes = jnp.broadcast_to(
        (2.0 ** (-8.0 * jnp.arange(1, heads + 1, dtype=F32) / heads))[:, None, None], (heads, 1, HEAD_DIM))

    big = _in_hbm([w_in[0], w_proj_attn[0], w_proj_lru[0], w_out[0], w_up[0], w_down[0]])
    names = ["w_in", "w_proj_attn", "w_proj_lru", "w_out", "w_up", "w_down"]
    cw_pad = jnp.pad(conv_w[0], ((0, 8 - CONV_TAPS), (0, 0)))
    cw_all = _all_gather8(cw_pad, "gather_conv_w")
    conv_w_full = jnp.concatenate([cw_all[2 * j] for j in range(N_CHIPS)], axis=1)
    (sem_a,), buf_a = _gather_start([_cast_shard(big[0], chip_arr, "cast_w_in")], [([0], NEIGHBOURS)], [cw_all],
                                    "gather_start_w_in")

    xn = _rms_fwd(xs, norm_mix_g, "norm_mix")
    flip_bits = (2, 1, 3)

    def w_in_view():
        return buf_a[0].reshape(N_CHIPS, d, N_SLOTS * u)

    proj = _proj_in_shard(xn, w_in_view(), chip_arr, None, "proj_in_own")
    bufs = [_cast_shard(w, chip_arr, "cast_" + nm) for w, nm in zip(big[1:], names[1:])]
    (fs, fr, rs, rr), buf_a = _gather_forward_relay(buf_a, sem_a[1], [proj] + bufs, "gather_forward_w_in")
    buf_a = _gather_finish(buf_a, sem_a[0], fs, fr, [], "gather_finish_w_in", flips=NEIGHBOURS)
    for j in NEIGHBOURS:
        proj = _proj_in_shard(xn, w_in_view(), chip_arr ^ flip_bits[j], proj, "proj_in_from_%d" % j)
    (sem_b, sem_c, sem_d), bufs = _gather_start(
        bufs, [([0, 1, 2], ALL_FLIPS), ([3], ALL_FLIPS), ([4], ALL_FLIPS)], [proj], "gather_start_rest")
    (ds, dr), buf_a = _gather_forward_diag(buf_a, rr, [proj, bufs[0]], "gather_forward_w_in_diag")
    buf_a = _gather_finish_diag(buf_a, rs, ds, dr, [], "gather_finish_w_in_diag")
    proj = _proj_in_shard(xn, w_in_view(), chip_arr ^ flip_bits[2], proj, "proj_in_from_2")
    w_in_g = w_in_view()
    proj_a = proj_b = proj
    y_attn, lse = _attn_fwd(proj_a, slopes)
    y_lru, h_lru = _lru_fwd(proj_b, conv_w_full, conv_b, wa, lru_ba, wx, lru_bx, lru_lambda)
    fsem_b, buf_b = _gather_forward(bufs[:3], sem_b[1], [y_attn, y_lru], "gather_forward_proj")
    buf_b = _gather_finish(buf_b, sem_b[0], fsem_b[0], fsem_b[1], [], "gather_finish_proj")
    wpa_g = buf_b[0].reshape(d, d)
    wpl_g = buf_b[1].reshape(d, d)
    wout_g = buf_b[2].reshape(d, d)

    tn = u
    sd_f32 = jax.ShapeDtypeStruct((s, d), F32)
    sd_bf16 = jax.ShapeDtypeStruct((s, d), BF16)
    col = pl.BlockSpec((s, tn), lambda i, j, k: (0, j))

    def slot(n):
        return pl.BlockSpec((None, s, tn), lambda i, j, k: (n, 0, j))

    p_attn = _mm_nn("proj_attn", y_attn, wpa_g, [], [], [sd_f32], [col], _store, tn)[0]
    fsem_c, buf_c = _gather_forward(bufs[3:4], sem_c[1], [p_attn], "gather_forward_w_up")

    def merge(acc, extras, outs):
        pa_ref, ga_ref, gl_ref = extras
        merged = _sigmoid(ga_ref[...]) * pa_ref[...] + _sigmoid(gl_ref[...]) * acc
        outs[0][...] = merged.astype(BF16)
        outs[1][...] = acc

    tn2 = max(HEAD_DIM, u // 2)
    col2 = pl.BlockSpec((s, tn2), lambda i, j, k: (0, j))

    def slot2(n):
        return pl.BlockSpec((None, s, tn2), lambda i, j, k: (n, 0, j))

    merged, p_lru = _mm_nn("proj_lru_merge", y_lru, wpl_g, [p_attn, proj, proj], [col2, slot2(5), slot2(6)],
                           [sd_bf16, sd_f32], [col2, col2], merge, tn2)

    def add_resid(acc, extras, outs):
        outs[0][...] = extras[0][...] + acc

    h1 = _mm_nn("w_out_resid", merged, wout_g, [xs], [col], [sd_f32], [col], add_resid, tn)[0]
    hn = _rms_fwd(h1, norm_mlp_g, "norm_mlp")
    buf_c = _gather_finish(buf_c, sem_c[0], fsem_c[0], fsem_c[1], [hn], "gather_finish_w_up")
    wup_g = buf_c[0].reshape(N_CHIPS, d, d)

    def relu_sq(acc, extras, outs):
        r = jnp.maximum(acc, 0.0)
        outs[0][...] = (r * r).astype(BF16)
        outs[1][...] = r.astype(BF16)

    sf_bf16 = jax.ShapeDtypeStruct((s, ff), BF16)
    hid, relu_up = _mm(
        "w_up_relu2", [hn, wup_g],
        [pl.BlockSpec((s, d), lambda i, j, k: (0, 0)),
         pl.BlockSpec((None, d, tn), lambda i, j, k: (j // 4, 0, j % 4))],
        [sf_bf16, sf_bf16], [col, col], (1, ff // tn, 1), NN, relu_sq)
    fsem_d, buf_d = _gather_forward(bufs[4:], sem_d[1], [hid], "gather_forward_w_down")
    wdown_g = _gather_finish(buf_d, sem_d[0], fsem_d[0], fsem_d[1], [], "gather_finish_w_down")[0].reshape(ff, d)
    h2 = _mm(
        "w_down_resid", [hid, wdown_g, h1],
        [pl.BlockSpec((s, d), lambda i, j, k: (0, k)), pl.BlockSpec((d, tn), lambda i, j, k: (k, j)), col],
        [sd_f32], [col], (1, d // tn, ff // d), NN, add_resid, nk=ff // d, acc_shape=(s, tn))[0]
    loss_part, dh2, dh2_b, d_gf = _loss_head(h2, gf, target)
    loss = lax.psum(loss_part[0, 0], AXES)

    def relu_sq_bwd(acc, extras, outs):
        outs[0][...] = (acc * (2.0 * extras[0][...].astype(F32))).astype(BF16)

    dup = _mm_nt("d_hid", dh2_b, wdown_g, [relu_up], [col], [sf_bf16], [col], relu_sq_bwd, tn)[0]
    tok_d = pl.BlockSpec((s, d), lambda i, j: (0, 0))
    g_wdown = _mm_tn(
        "g_w_down", hid, dh2_b, pl.BlockSpec((s, d), lambda i, j: (0, i)),
        pl.BlockSpec((s, tn), lambda i, j: (0, j)), jax.ShapeDtypeStruct((ff, d), BF16),
        pl.BlockSpec((d, tn), lambda i, j: (i, j)), (ff // d, d // tn), d, tn, s)
    dhn = _mm(
        "d_hn", [dup, wup_g],
        [pl.BlockSpec((s, d), lambda i, j, k: (0, k)), pl.BlockSpec((None, tn, d), lambda i, j, k: (k, j, 0))],
        [sd_f32], [col], (1, d // tn, ff // d), NT, _store, nk=ff // d, acc_shape=(s, tn))[0]
    g_wup = _mm_tn(
        "g_w_up", hn, dup, tok_d, pl.BlockSpec((s, tn), lambda i, j: (0, j)),
        jax.ShapeDtypeStruct((N_CHIPS, d, d), BF16), pl.BlockSpec((None, d, tn), lambda i, j: (j // 4, 0, j % 4)),
        (1, ff // tn), d, tn, s)
    big_m = _in_hbm([m_w_in[0], m_w_proj_attn[0], m_w_proj_lru[0], m_w_out[0], m_w_up[0], m_w_down[0]])
    big_v = _in_hbm([v_w_in[0], v_w_proj_attn[0], v_w_proj_lru[0], v_w_out[0], v_w_up[0], v_w_down[0]])
    big_out = {}

    def reduce_begin(ids, gs, tag, everywhere=None):
        g4 = [g.reshape(N_CHIPS, 2, big[i].shape[0] // 2, big[i].shape[1]) for i, g in zip(ids, gs)]
        tags = [names[i] for i in ids]
        if everywhere is not None:
            g4.append(everywhere.reshape(N_CHIPS, 2, everywhere.shape[0] // (2 * N_CHIPS), everywhere.shape[1]))
            tags.append("small_" + tag)
        from_sibling = _pair_exchange(g4, "pair_exchange_" + tag)
        sums = [_pair_sum(g, r, cidx, "pair_sum_" + t) for t, g, r in zip(tags, g4, from_sibling)]
        to_all = () if everywhere is None else (len(ids),)
        return _chip_start(sums, "chip_start_" + tag, to_all), to_all

    def pair_begin(ids, gs, tag):
        g4 = [g.reshape(N_CHIPS, 2, big[i].shape[0] // 2, big[i].shape[1]) for i, g in zip(ids, gs)]
        return _pair_start(g4, "pair_start_" + tag)

    def reduce_begin_paired(ids, paired, after, tag):
        sems, g4, lands, _ = paired
        g4, from_sibling = _pair_wait(sems, g4, lands, after, "pair_wait_" + tag)
        sums = [_pair_sum(g, r, cidx, "pair_sum_" + names[i]) for i, g, r in zip(ids, g4, from_sibling)]
        return _chip_start(sums, "chip_start_" + tag), ()

    def reduce_mid(ids, begun, after, tag):
        (sems, sums, lands, _), to_all = begun
        sums, lands = _chip_wait(sems, sums, lands, after, "chip_wait_" + tag, to_all)
        halves = [_chip_sum(p, own, place_arr, "chip_sum_" + names[i]) for i, p, own in zip(ids, lands, sums)]
        if to_all:
            halves.append(_chip_sum_all(lands[-1], sums[-1], place_arr, "chip_sum_small_" + tag))
        return _half_start(halves, "half_start_" + tag), to_all

    def reduce_end(ids, mid, after, tag):
        (hsems, halves), to_all = mid
        full = _half_wait(hsems, halves, after, "half_wait_" + tag)
        done = []
        for i, g in zip(ids, full):
            res = _adamw(big[i], g.reshape(big[i].shape), big_m[i], big_v[i], "adamw_" + names[i])
            big_out[names[i]] = tuple(a[None] for a in res)
            done.append(res[1])
        everywhere = full[-1].reshape(-1, full[-1].shape[-1]) if to_all else None
        return everywhere, done

    def after_token(a, begun):
        return a + begun[0][3][:1, :1]

    pair_mlp = pair_begin([4, 5], [g_wup, g_wdown], "mlp")
    dh1, dh1_b, d_gmlp = _rms_bwd(h1, norm_mlp_g + pair_mlp[3][:1, :1], dhn, dh2, "norm_mlp_bwd")

    g_wout = _mm_tn(
        "g_w_out", merged, dh1_b, tok_d, pl.BlockSpec((s, tn), lambda i, j: (0, j)),
        jax.ShapeDtypeStruct((d, d), BF16), pl.BlockSpec((d, tn), lambda i, j: (0, j)), (1, d // tn), d, tn, s)

    def merge_bwd(acc, extras, outs):
        pa_ref, pl_ref, ga_ref, gl_ref = extras
        sa, sl = _sigmoid(ga_ref[...]), _sigmoid(gl_ref[...])
        outs[0][...] = (acc * sa).astype(BF16)
        outs[1][...] = (acc * sl).astype(BF16)
        outs[2][0] = (acc * pa_ref[...] * (sa * (1.0 - sa))).astype(BF16)
        outs[2][1] = (acc * pl_ref[...] * (sl * (1.0 - sl))).astype(BF16)

    nb = N_SLOTS - N_QKV
    d_pa, d_pl, dproj_b = _mm_nt(
        "d_merged", dh1_b, wout_g, [p_attn, p_lru, proj, proj], [col2, col2, slot2(5), slot2(6)],
        [sd_bf16, sd_bf16, jax.ShapeDtypeStruct((nb, s, d), BF16)],
        [col2, col2, pl.BlockSpec((2, s, tn2), lambda i, j, k: (1, 0, j))], merge_bwd, tn2)
    red_mlp = reduce_begin_paired([4, 5], pair_mlp, [d_pa], "mlp")
    dy_attn = _mm_nt("d_y_attn", d_pa, wpa_g, [], [], [sd_f32], [col], _store, tn)[0]
    dy_lru = _mm_nt("d_y_lru", d_pl, wpl_g, [], [], [sd_f32], [col], _store, tn)[0]
    g_wpa = _mm_tn(
        "g_w_proj_attn", y_attn, d_pa, tok_d, pl.BlockSpec((s, tn), lambda i, j: (0, j)),
        jax.ShapeDtypeStruct((d, d), BF16), pl.BlockSpec((d, tn), lambda i, j: (0, j)), (1, d // tn), d, tn, s)
    g_wpl = _mm_tn(
        "g_w_proj_lru", y_lru, d_pl, tok_d, pl.BlockSpec((s, tn), lambda i, j: (0, j)),
        jax.ShapeDtypeStruct((d, d), BF16), pl.BlockSpec((d, tn), lambda i, j: (0, j)), (1, d // tn), d, tn, s)

    pair_proj = pair_begin([1, 2, 3], [g_wpa, g_wpl, g_wout], "proj")

    dproj_b, d_cw, d_cb, d_wa, d_ba, d_wx, d_bx, d_lam = _lru_bwd(
        proj_b, h_lru, dy_lru, conv_w_full, conv_b, wa, lru_ba, wx, lru_bx, lru_lambda + pair_proj[3][:1, :1],
        dproj_b)
    red_proj = reduce_begin_paired([1, 2, 3], pair_proj, [dproj_b], "proj")
    dproj_a = _attn_bwd(proj_a, after_token(slopes, red_proj), y_attn, lse, dy_attn)
    per = N_SLOTS
    g_win_shape = jax.ShapeDtypeStruct((N_CHIPS, d, N_SLOTS * u), BF16)

    def g_win_part(name, dproj, first, prev):
        n_units = 4 * dproj.shape[0]
        return _mm_tn(
            name, xn, dproj, tok_d, pl.BlockSpec((None, s, u), lambda i, j: (j // 4, 0, j % 4)),
            g_win_shape, pl.BlockSpec((None, d, u), lambda i, j: ((j + first) // per, 0, (j + first) % per)),
            (1, n_units), d, u, s, aliases=None if prev is None else {2: 0}, extra=prev)

    mat_rows = heads * HEAD_DIM * HEAD_DIM // d
    vec_names = ["norm_mix_g", "conv_b", "lru_ba", "lru_bx", "lru_lambda", "norm_mlp_g", "norm_final_g"]

    def pack(wa_, wx_, cw_, vecs, name):
        rows = [wa_.reshape(mat_rows, d), wx_.reshape(mat_rows, d), cw_] + [a.reshape(1, d) for a in vecs]
        n = sum(a.shape[0] for a in rows)
        return _pack_rows(rows, n + (-n % 64), name)

    zero_cw = jnp.zeros((CONV_TAPS, d), F32)
    small_g = pack(d_wa, d_wx, d_cw, [jnp.zeros((1, d), F32), d_cb, d_ba, d_bx, d_lam, d_gmlp, d_gf], "pack_small_g")
    g_win = g_win_part("g_w_in_qkv", dproj_a, 0, None)
    g_win = g_win_part("g_w_in_rest", dproj_b, 4 * N_QKV, g_win)
    red_in = reduce_begin([0], [g_win], "w_in", everywhere=small_g)
    dxn = _dxn(dproj_a, dproj_b, w_in_g, 2 * tn, [red_in[0][3]])
    grad_x, _, d_gmix = _rms_bwd(xs, norm_mix_g, dxn, dh1, "norm_mix_bwd")

    mid_mlp = reduce_mid([4, 5], red_mlp, [grad_x], "mlp")
    mid_proj = reduce_mid([1, 2, 3], red_proj, [mid_mlp[0][1][0]], "proj")
    _, done_mlp = reduce_end([4, 5], mid_mlp, [mid_proj[0][1][0]], "mlp")
    _, done_proj = reduce_end([1, 2, 3], mid_proj, done_mlp[-1:], "proj")
    done = done_mlp + done_proj
    small_w = pack(wa, wx, zero_cw, [norm_mix_g, conv_b, lru_ba, lru_bx, lru_lambda, norm_mlp_g, norm_final_g],
                   "pack_small_w")
    small_m = pack(m_lru_wa[0], m_lru_wx[0], zero_cw,
                   [m_norm_mix_g, m_conv_b, m_lru_ba, m_lru_bx, m_lru_lambda, m_norm_mlp_g, m_norm_final_g],
                   "pack_small_m")
    small_v = pack(v_lru_wa[0], v_lru_wx[0], zero_cw,
                   [v_norm_mix_g, v_conv_b, v_lru_ba, v_lru_bx, v_lru_lambda, v_norm_mlp_g, v_norm_final_g],
                   "pack_small_v")
    mid_in = reduce_mid([0], red_in, done + [small_w, small_m, small_v], "w_in")
    gmix_parts = _all_gather8(jnp.pad(d_gmix, ((0, 7), (0, 0))), "gather_gain_grad", [mid_in[0][1][0]])
    gmix_out = _sum8_adamw_row(gmix_parts, norm_mix_g, m_norm_mix_g, v_norm_mix_g, "sum_adamw_norm_mix_g")
    small_sum, _ = reduce_end([0], mid_in, [gmix_out[1]], "w_in")
    small = _adamw(small_w, small_sum, small_m, small_v, "adamw_small")
    g_cw = lax.dynamic_slice(small_sum[2 * mat_rows:2 * mat_rows + CONV_TAPS], (0, chip * u), (CONV_TAPS, u))
    cw_out = _adamw(conv_w[0], g_cw, m_conv_w[0], v_conv_w[0], "adamw_conv_w")

    def small_leaf(kind, name):
        a = small[kind]
        if name == "norm_mix_g":
            return gmix_out[kind]
        if name == "lru_wa":
            return a[0:mat_rows].reshape(lru_wa.shape)
        if name == "lru_wx":
            return a[mat_rows:2 * mat_rows].reshape(lru_wx.shape)
        if name == "conv_w":
            return cw_out[kind][None]
        row = a[2 * mat_rows + CONV_TAPS + vec_names.index(name)]
        return row if name == "norm_final_g" else row[None]

    order = ["norm_mix_g", "w_in", "conv_w", "conv_b", "lru_wa", "lru_ba", "lru_wx", "lru_bx", "lru_lambda",
             "w_proj_attn", "w_proj_lru", "w_out", "norm_mlp_g", "w_up", "w_down", "norm_final_g"]
    outs = [loss, grad_x[None]]
    for kind in range(4):
        for name in order:
            outs.append(big_out[name][kind] if name in big_out else small_leaf(kind, name))
    return tuple(outs)
```

```python
import functools

import jax
import jax.numpy as jnp
from jax import lax
from jax.experimental import pallas as pl
from jax.experimental.pallas import tpu as pltpu

F32 = jnp.float32
BF16 = jnp.bfloat16
MESH = pl.DeviceIdType.MESH
AXES = ("x", "y", "c")

N_CHIPS = 4
N_DEV = 8
HEAD_DIM = 128
ATTN_BLK = 128
DILATIONS = (1, 4, 16)
ATTN_UNROLL = 16
CONV_TAPS = 4
LRU_C = 8.0
EPS = 1e-6
N_SLOTS = 7
N_QKV = 3
VMEM_MIB = 2 ** 20
VMEM_V7X = 64 * VMEM_MIB
STREAM_TILE = 4 * VMEM_MIB
VMEM_STREAM_MIB = 32
VMEM_TILES_MIB = 56

ADAM_LR = 0.001
ADAM_B1 = 0.9
ADAM_B2 = 0.999
ADAM_EPS = 1e-08
ADAM_WD = 0.01
ADAM_STEP = 10

NN = (((1,), (0,)), ((), ()))
NT = (((1,), (1,)), ((), ()))
TN = (((0,), (0,)), ((), ()))


def _params(vmem_mib=None, **kw):
    limit = None if vmem_mib is None else min(vmem_mib * VMEM_MIB, VMEM_V7X - 8 * VMEM_MIB)
    return pltpu.CompilerParams(vmem_limit_bytes=limit, **kw)


def _row_tile(rows, row_bytes, budget=VMEM_MIB):
    t = rows
    while t % 16 == 0 and t * row_bytes > budget:
        t //= 2
    return t


def _dot(a, b, dims):
    return lax.dot_general(a.astype(BF16), b.astype(BF16), dims, preferred_element_type=F32)


def _sigmoid(x):
    return jax.nn.sigmoid(x)


def _rms_fwd(x, g, name):
    s, d = x.shape
    tm = _row_tile(s, d * 4)

    def body(x_ref, g_ref, o_ref):
        xf = x_ref[...]
        r = lax.rsqrt(jnp.mean(xf * xf, axis=-1, keepdims=True) + EPS)
        o_ref[...] = (xf * r * g_ref[...]).astype(o_ref.dtype)

    return pl.pallas_call(
        body, name=name, grid=(s // tm,),
        in_specs=[pl.BlockSpec((tm, d), lambda i: (i, 0)), pl.BlockSpec((1, d), lambda i: (0, 0))],
        out_specs=pl.BlockSpec((tm, d), lambda i: (i, 0)),
        out_shape=jax.ShapeDtypeStruct((s, d), BF16), compiler_params=_params(VMEM_STREAM_MIB),
    )(x, g)


def _rms_bwd(x, g, dy, resid, name):
    s, d = x.shape
    tm = _row_tile(s, d * 4)

    def body(x_ref, g_ref, dy_ref, res_ref, dx_ref, dxb_ref, dg_ref):
        xf = x_ref[...]
        r = lax.rsqrt(jnp.mean(xf * xf, axis=-1, keepdims=True) + EPS)
        xh = xf * r
        dyv = dy_ref[...]
        dxh = dyv * g_ref[...]
        dx = r * (dxh - xh * jnp.mean(dxh * xh, axis=-1, keepdims=True)) + res_ref[...]
        dx_ref[...] = dx
        dxb_ref[...] = dx.astype(BF16)
        part = jnp.sum(dyv * xh, axis=0, keepdims=True)

        @pl.when(pl.program_id(0) == 0)
        def _():
            dg_ref[...] = part

        @pl.when(pl.program_id(0) > 0)
        def _():
            dg_ref[...] += part

    row = pl.BlockSpec((tm, d), lambda i: (i, 0))
    vec = pl.BlockSpec((1, d), lambda i: (0, 0))
    return pl.pallas_call(
        body, name=name, grid=(s // tm,),
        in_specs=[row, vec, row, row], out_specs=[row, row, vec],
        out_shape=[jax.ShapeDtypeStruct((s, d), F32), jax.ShapeDtypeStruct((s, d), BF16),
                   jax.ShapeDtypeStruct((1, d), F32)],
        compiler_params=_params(VMEM_STREAM_MIB),
    )(x, g, dy, resid)


def _loss_head(h2, g, target):
    s, d = h2.shape
    tm = _row_tile(s, d * 4)

    def body(x_ref, g_ref, t_ref, loss_ref, dx_ref, dxb_ref, dg_ref):
        xf = x_ref[...]
        gv = g_ref[...]
        r = lax.rsqrt(jnp.mean(xf * xf, axis=-1, keepdims=True) + EPS)
        xh = xf * r
        err = xh * gv - t_ref[...]
        part = jnp.sum(jnp.sum(err * err, axis=1, keepdims=True), axis=0, keepdims=True) * (0.5 / d)
        dyv = err * (1.0 / d)
        dxh = dyv * gv
        dx = r * (dxh - xh * jnp.mean(dxh * xh, axis=-1, keepdims=True))
        dx_ref[...] = dx
        dxb_ref[...] = dx.astype(BF16)
        dgp = jnp.sum(dyv * xh, axis=0, keepdims=True)

        @pl.when(pl.program_id(0) == 0)
        def _():
            dg_ref[...] = dgp
            loss_ref[...] = jnp.broadcast_to(part, loss_ref.shape)

        @pl.when(pl.program_id(0) > 0)
        def _():
            dg_ref[...] += dgp
            loss_ref[...] += jnp.broadcast_to(part, loss_ref.shape)

    row = pl.BlockSpec((tm, d), lambda i: (i, 0))
    vec = pl.BlockSpec((1, d), lambda i: (0, 0))
    return pl.pallas_call(
        body, name="loss_head", grid=(s // tm,),
        in_specs=[row, vec, row],
        out_specs=[pl.BlockSpec((8, 128), lambda i: (0, 0)), row, row, vec],
        out_shape=[jax.ShapeDtypeStruct((8, 128), F32), jax.ShapeDtypeStruct((s, d), F32),
                   jax.ShapeDtypeStruct((s, d), BF16), jax.ShapeDtypeStruct((1, d), F32)],
        compiler_params=_params(VMEM_STREAM_MIB),
    )(h2, g, target)


def _mm(name, operands, in_specs, out_shape, out_specs, grid, dims, epilogue, nk=1, acc_shape=None,
        aliases=None):
    n_in = len(operands)
    n_out = len(out_shape)

    def body(*refs):
        a_ref, b_ref = refs[0], refs[1]
        extras = refs[2:n_in]
        outs = refs[n_in:n_in + n_out]

        def prod():
            return _dot(a_ref[...], b_ref[...], dims)

        if nk == 1:
            epilogue(prod(), extras, outs)
        else:
            acc = refs[n_in + n_out]
            k = pl.program_id(2)

            @pl.when(k == 0)
            def _():
                acc[...] = prod()

            @pl.when(k > 0)
            def _():
                acc[...] += prod()

            @pl.when(k == nk - 1)
            def _():
                epilogue(acc[...], extras, outs)

    scratch = [] if nk == 1 else [pltpu.VMEM(acc_shape, F32)]
    return pl.pallas_call(
        body, name=name, grid=grid, in_specs=in_specs, out_specs=out_specs, out_shape=out_shape,
        scratch_shapes=scratch, input_output_aliases=aliases or {},
        compiler_params=_params(VMEM_TILES_MIB),
    )(*operands)


def _store(acc, extras, outs):
    outs[0][...] = acc.astype(outs[0].dtype)


def _proj_in_shard(xn, w_in_g, shard_arr, prev, name):
    s, d = xn.shape
    u = d // 4
    per = N_SLOTS
    n_prev = 0 if prev is None else 1

    def body(sh_ref, x_ref, w_ref, *rest):
        del sh_ref
        rest[n_prev][...] = _dot(x_ref[...], w_ref[...], NN)

    def out_map(j, sh_ref):
        unit = per * sh_ref[0] + j
        return (unit // 4, 0, unit % 4)

    return pl.pallas_call(
        body, name=name,
        grid_spec=pltpu.PrefetchScalarGridSpec(
            num_scalar_prefetch=1, grid=(per,),
            in_specs=[pl.BlockSpec((s, d), lambda j, sh_ref: (0, 0)),
                      pl.BlockSpec((None, d, u), lambda j, sh_ref: (sh_ref[0], 0, j))] + [HBM] * n_prev,
            out_specs=pl.BlockSpec((None, s, u), out_map)),
        out_shape=jax.ShapeDtypeStruct((N_SLOTS, s, d), F32),
        input_output_aliases={3: 0} if n_prev else {},
        compiler_params=_params(VMEM_TILES_MIB),
    )(shard_arr, xn, w_in_g, *([] if prev is None else [prev]))


def _mm_nn(name, a, b, extras, extra_specs, out_shape, out_specs, epilogue, tn, aliases=None):
    s, kdim = a.shape
    n = b.shape[1]
    return _mm(
        name, [a, b] + list(extras),
        [pl.BlockSpec((s, kdim), lambda i, j, k: (0, 0)), pl.BlockSpec((kdim, tn), lambda i, j, k: (0, j))]
        + list(extra_specs),
        out_shape, out_specs, (1, n // tn, 1), NN, epilogue, aliases=aliases)


def _mm_nt(name, a, b, extras, extra_specs, out_shape, out_specs, epilogue, tn, aliases=None):
    s, kdim = a.shape
    n = b.shape[0]
    return _mm(
        name, [a, b] + list(extras),
        [pl.BlockSpec((s, kdim), lambda i, j, k: (0, 0)), pl.BlockSpec((tn, kdim), lambda i, j, k: (j, 0))]
        + list(extra_specs),
        out_shape, out_specs, (1, n // tn, 1), NT, epilogue, aliases=aliases)


def _mm_tn(name, a, b, a_spec, b_spec, out_shape, out_spec, grid, m, tn, s, aliases=None, extra=None):
    ch = 256
    n_in = 2 if extra is None else 3

    def body(*refs):
        a_ref, b_ref = refs[0], refs[1]
        o_ref, at_ref = refs[n_in], refs[n_in + 1]

        @pl.when(pl.program_id(1) == 0)
        def _():
            for c0 in range(0, s, ch):
                at_ref[:, c0:c0 + ch] = a_ref[c0:c0 + ch, :].astype(F32).T.astype(BF16)

        o_ref[...] = _dot(at_ref[...], b_ref[...], NN).astype(o_ref.dtype)

    operands = [a, b] + ([] if extra is None else [extra])
    in_specs = [a_spec, b_spec] + ([] if extra is None else [pl.BlockSpec(memory_space=pl.ANY)])
    return pl.pallas_call(
        body, name=name, grid=grid, in_specs=in_specs, out_specs=out_spec, out_shape=out_shape,
        scratch_shapes=[pltpu.VMEM((m, s), BF16)], input_output_aliases=aliases or {},
        compiler_params=_params(VMEM_TILES_MIB),
    )(*operands)


def _dxn(dproj_a, dproj_b, w_in_g, tn, after):
    n_a, s, d = dproj_a.shape
    u = d // 4
    ua = 4 * n_a
    nk = 4 * N_SLOTS
    per = N_SLOTS

    def body(a_ref, b_ref, w_ref, *rest):
        o_ref = rest[len(after)]
        k = pl.program_id(2)

        @pl.when(k == 0)
        def _():
            o_ref[...] = jnp.zeros_like(o_ref)

        @pl.when(k < ua)
        def _():
            o_ref[...] += _dot(a_ref[...], w_ref[...], NT)

        @pl.when(k >= ua)
        def _():
            o_ref[...] += _dot(b_ref[...], w_ref[...], NT)

    def a_map(i, j, k):
        kk = jnp.minimum(k, ua - 1)
        return (kk // 4, 0, kk % 4)

    def b_map(i, j, k):
        kk = jnp.maximum(k - ua, 0)
        return (kk // 4, 0, kk % 4)

    return pl.pallas_call(
        body, name="dxn", grid=(1, d // tn, nk),
        in_specs=[pl.BlockSpec((None, s, u), a_map), pl.BlockSpec((None, s, u), b_map),
                  pl.BlockSpec((None, tn, u), lambda i, j, k: (k // per, j, k % per))] + [HBM] * len(after),
        out_specs=pl.BlockSpec((s, tn), lambda i, j, k: (0, j)),
        out_shape=jax.ShapeDtypeStruct((s, d), F32),
        compiler_params=_params(VMEM_TILES_MIB),
    )(dproj_a, dproj_b, w_in_g, *after)


def _attn_masks(slope, dil):
    ii = lax.broadcasted_iota(jnp.int32, (ATTN_BLK, 2 * ATTN_BLK), 0)
    jj = lax.broadcasted_iota(jnp.int32, (ATTN_BLK, 2 * ATTN_BLK), 1)
    diff = ATTN_BLK + ii - jj
    band = (diff >= 0) & (diff <= ATTN_BLK)
    bias = -(slope * float(dil)) * diff.astype(F32)
    return band, bias, jj


def _attn_window(t, nblk):
    cur = pl.ds(pl.multiple_of(t * ATTN_BLK, ATTN_BLK), ATTN_BLK)
    prev = pl.ds(pl.multiple_of(jnp.maximum(t - 1, 0) * ATTN_BLK, ATTN_BLK), ATTN_BLK)
    first = jnp.where(t % nblk == 0, ATTN_BLK, 0)
    return prev, cur, first


def _unrolled_loop(n, step, init, unroll=ATTN_UNROLL):
    def trip(i, carry):
        for k in range(unroll):
            carry = step(i * unroll + k, carry)
        return carry

    return lax.fori_loop(0, n // unroll, trip, init)


def _streams(pairs, dil, s):
    if dil == 1:
        return [src for _, src in pairs]
    seg = s // dil
    for dst, src in pairs:
        for r in range(dil):
            dst[r * seg:(r + 1) * seg, :] = src[pl.ds(r, seg, stride=dil), :].astype(dst.dtype)
    return [dst for dst, _ in pairs]


def _attn_fwd(proj_a, slopes):
    _, s, d = proj_a.shape
    heads = d // HEAD_DIM
    scale = HEAD_DIM ** -0.5
    n_t = s // ATTN_BLK
    ng = len(DILATIONS)

    def body(q_ref, k_ref, v_ref, sl_ref, o_ref, lse_ref, qd, kd, vd, od, ld, og, lg):
        slope = sl_ref[...][:, :1]
        for g, dil in enumerate(DILATIONS):
            nblk = s // dil // ATTN_BLK
            qs, ks, vs = _streams([(qd, q_ref), (kd, k_ref), (vd, v_ref)], dil, s)
            o_t, l_t = (og.at[g], lg.at[g]) if dil == 1 else (od, ld)
            band, bias, jj = _attn_masks(slope, dil)

            def blk(t, carry, nblk=nblk, band=band, bias=bias, jj=jj, qs=qs, ks=ks, vs=vs, o_t=o_t, l_t=l_t):
                prev, cur, first = _attn_window(t, nblk)
                kk = jnp.concatenate([ks[prev, :], ks[cur, :]], axis=0)
                vv = jnp.concatenate([vs[prev, :], vs[cur, :]], axis=0)
                sc = _dot(qs[cur, :], kk, NT) * scale + bias
                sc = jnp.where(band & (jj >= first), sc, -jnp.inf)
                m = jnp.max(sc, axis=1, keepdims=True)
                p = jnp.exp(sc - m)
                l = jnp.sum(p, axis=1, keepdims=True)
                o_t[cur, :] = _dot(p, vv, NN) / l
                l_t[cur, :] = jnp.broadcast_to(m + jnp.log(l), (ATTN_BLK, HEAD_DIM))
                return carry

            _unrolled_loop(n_t, blk, 0)
            seg = s // dil
            if dil > 1:
                for r in range(dil):
                    og[g, pl.ds(r, seg, stride=dil), :] = od[r * seg:(r + 1) * seg, :]
                    lg[g, pl.ds(r, seg, stride=dil), :] = ld[r * seg:(r + 1) * seg, :]

        ch = 256

        def combine(c, carry):
            rows = pl.ds(pl.multiple_of(c * ch, ch), ch)
            ls = [lg[g, rows, :] for g in range(ng)]
            mx = functools.reduce(jnp.maximum, ls)
            es = [jnp.exp(x - mx) for x in ls]
            den = functools.reduce(jnp.add, es)
            num = functools.reduce(jnp.add, [es[g] * og[g, rows, :] for g in range(ng)])
            o_ref[rows, :] = (num / den).astype(o_ref.dtype)
            lse_ref[rows, :] = mx + jnp.log(den)
            return carry

        lax.fori_loop(0, s // ch, combine, 0)

    def col(slot):
        return pl.BlockSpec((None, s, HEAD_DIM), lambda h: (slot, 0, h))

    head = pl.BlockSpec((s, HEAD_DIM), lambda h: (0, h))
    return pl.pallas_call(
        body, name="attn_fwd", grid=(heads,),
        in_specs=[col(0), col(1), col(2), pl.BlockSpec((None, 1, HEAD_DIM), lambda h: (h, 0, 0))],
        out_specs=[head, head],
        out_shape=[jax.ShapeDtypeStruct((s, d), BF16), jax.ShapeDtypeStruct((s, d), F32)],
        scratch_shapes=[pltpu.VMEM((s, HEAD_DIM), BF16)] * 3 + [pltpu.VMEM((s, HEAD_DIM), F32)] * 2
        + [pltpu.VMEM((ng, s, HEAD_DIM), F32)] * 2,
        compiler_params=_params(VMEM_TILES_MIB),
    )(proj_a, proj_a, proj_a, slopes)


def _attn_bwd(proj_a, slopes, y_attn, lse, dy):
    _, s, d = proj_a.shape
    heads = d // HEAD_DIM
    scale = HEAD_DIM ** -0.5
    n_t = s // ATTN_BLK

    def body(q_ref, k_ref, v_ref, sl_ref, o_ref, lse_ref, dy_ref, out_ref,
             qd, kd, vd, dod, lsd, dld, delta, dqd, dkd, dvd, dqa, dka, dva):
        slope = sl_ref[...][:, :1]
        dyv = dy_ref[...]
        delta[...] = jnp.broadcast_to(
            jnp.sum(dyv * o_ref[...].astype(F32), axis=1, keepdims=True), (s, HEAD_DIM))
        for g, dil in enumerate(DILATIONS):
            nblk = s // dil // ATTN_BLK
            seg = s // dil
            qs, ks, vs, dos, lss, dls = _streams(
                [(qd, q_ref), (kd, k_ref), (vd, v_ref), (dod, dy_ref), (lsd, lse_ref), (dld, delta)], dil, s)
            dq_t, dk_t, dv_t = (dqa, dka, dva) if dil == 1 else (dqd, dkd, dvd)
            band, bias, jj = _attn_masks(slope, dil)

            def blk(t, carry, nblk=nblk, band=band, bias=bias, jj=jj, qs=qs, ks=ks, vs=vs, dos=dos, lss=lss,
                    dls=dls, dq_t=dq_t, dk_t=dk_t, dv_t=dv_t):
                ck, cv = carry
                prev, cur, first = _attn_window(t, nblk)
                q = qs[cur, :]
                do = dos[cur, :]
                lse_b = lss[cur, :]
                dl_b = dls[cur, :]
                kk = jnp.concatenate([ks[prev, :], ks[cur, :]], axis=0)
                vv = jnp.concatenate([vs[prev, :], vs[cur, :]], axis=0)
                sc = _dot(q, kk, NT) * scale + bias
                p = jnp.where(band & (jj >= first), jnp.exp(sc - jnp.concatenate([lse_b, lse_b], axis=1)), 0.0)
                dp = _dot(do, vv, NT)
                ds = p * (dp - jnp.concatenate([dl_b, dl_b], axis=1))
                dv_b = _dot(p, do, TN)
                dk_b = _dot(ds, q, TN) * scale
                dq_t[cur, :] = _dot(ds, kk, NN) * scale
                done = pl.ds(pl.multiple_of(jnp.where(t == 0, n_t, t - 1) * ATTN_BLK, ATTN_BLK), ATTN_BLK)
                dk_t[done, :] = ck + dk_b[:ATTN_BLK]
                dv_t[done, :] = cv + dv_b[:ATTN_BLK]
                return dk_b[ATTN_BLK:], dv_b[ATTN_BLK:]

            zero = jnp.zeros((ATTN_BLK, HEAD_DIM), F32)
            ck, cv = _unrolled_loop(n_t, blk, (zero, zero))
            dk_t[(n_t - 1) * ATTN_BLK:n_t * ATTN_BLK, :] = ck
            dv_t[(n_t - 1) * ATTN_BLK:n_t * ATTN_BLK, :] = cv
            if dil > 1:
                for acc, part in ((dqa, dqd), (dka, dkd), (dva, dvd)):
                    for r in range(dil):
                        acc[pl.ds(r, seg, stride=dil), :] += part[r * seg:(r + 1) * seg, :]
        out_ref[0] = dqa[...].astype(out_ref.dtype)
        out_ref[1] = dka[0:s, :].astype(out_ref.dtype)
        out_ref[2] = dva[0:s, :].astype(out_ref.dtype)

    def col(slot):
        return pl.BlockSpec((None, s, HEAD_DIM), lambda h: (slot, 0, h))

    head = pl.BlockSpec((s, HEAD_DIM), lambda h: (0, h))
    return pl.pallas_call(
        body, name="attn_bwd", grid=(heads,),
        in_specs=[col(0), col(1), col(2), pl.BlockSpec((None, 1, HEAD_DIM), lambda h: (h, 0, 0)),
                  head, head, head],
        out_specs=pl.BlockSpec((N_QKV, s, HEAD_DIM), lambda h: (0, 0, h)),
        out_shape=jax.ShapeDtypeStruct((N_QKV, s, d), BF16),
        scratch_shapes=[pltpu.VMEM((s, HEAD_DIM), BF16)] * 4 + [pltpu.VMEM((s, HEAD_DIM), F32)] * 4
        + [pltpu.VMEM((s + ATTN_BLK, HEAD_DIM), F32)] * 2 + [pltpu.VMEM((s, HEAD_DIM), F32)]
        + [pltpu.VMEM((s + ATTN_BLK, HEAD_DIM), F32)] * 2,
        compiler_params=_params(VMEM_TILES_MIB),
    )(proj_a, proj_a, proj_a, slopes, y_attn, lse, dy)


def _expm1(x):
    small = x * (1.0 + x * (0.5 + x * (1.0 / 6.0 + x * (1.0 / 24.0 + x * (1.0 / 120.0)))))
    return jnp.where(jnp.abs(x) < 0.1, small, jnp.exp(x) - 1.0)


def _softplus(x):
    return jnp.maximum(x, 0.0) + jnp.log1p(jnp.exp(-jnp.abs(x)))


GELU_K = 0.7978845608028654
GELU_C = 0.044715


def _gelu(x):
    t = jnp.tanh(GELU_K * (x + GELU_C * x * x * x))
    return 0.5 * x * (1.0 + t), t


def _gelu_grad(x, t):
    return 0.5 * (1.0 + t) + 0.5 * x * (1.0 - t * t) * GELU_K * (1.0 + 3.0 * GELU_C * x * x)


def _lru_gates(xc, wa, ba, wx, bx, sp):
    r = _sigmoid(_dot(xc, wa, NN) + ba)
    ig = _sigmoid(_dot(xc, wx, NN) + bx)
    log_a = -LRU_C * r * sp
    a = jnp.exp(log_a)
    mult = jnp.sqrt(-_expm1(2.0 * log_a))
    return r, ig, a, mult


def _scan_fwd(a, u, tt):
    row = lax.broadcasted_iota(jnp.int32, a.shape, 0)
    sh = 1
    while sh < tt:
        keep = row >= sh
        a_s = jnp.where(keep, pltpu.roll(a, sh, 0), 1.0)
        u_s = jnp.where(keep, pltpu.roll(u, sh, 0), 0.0)
        u = a * u_s + u
        a = a * a_s
        sh *= 2
    return a, u


def _scan_bwd(b, g, tt):
    row = lax.broadcasted_iota(jnp.int32, b.shape, 0)
    sh = 1
    while sh < tt:
        keep = row < tt - sh
        b_s = jnp.where(keep, pltpu.roll(b, tt - sh, 0), 1.0)
        g_s = jnp.where(keep, pltpu.roll(g, tt - sh, 0), 0.0)
        g = g + b * g_s
        b = b * b_s
        sh *= 2
    return b, g


ROW_CHUNK = 256


def _pad_copy(xpad_ref, x_ref, s):
    xpad_ref[0:8, :] = jnp.zeros((8, HEAD_DIM), F32)
    for c0 in range(0, s, ROW_CHUNK):
        xpad_ref[8 + c0:8 + c0 + ROW_CHUNK, :] = x_ref[c0:c0 + ROW_CHUNK, :]


def _conv_rows(dst_ref, xpad_ref, cw, cb, s):
    for c0 in range(0, s, ROW_CHUNK):
        acc = cb
        for j in range(CONV_TAPS):
            off = 8 - (CONV_TAPS - 1) + j + c0
            acc = acc + cw[j:j + 1, :] * xpad_ref[off:off + ROW_CHUNK, :]
        dst_ref[c0:c0 + ROW_CHUNK, :] = acc


LRU_TILE = 128
LRU_UNROLL = 16


def _lru_specs(s, d):
    heads = d // HEAD_DIM

    def col(slot):
        return pl.BlockSpec((None, s, HEAD_DIM), lambda h: (slot, 0, h))

    vec = pl.BlockSpec((1, HEAD_DIM), lambda h: (0, h))
    mat = pl.BlockSpec((None, HEAD_DIM, HEAD_DIM), lambda h: (h, 0, 0))
    cw = pl.BlockSpec((8, HEAD_DIM), lambda h: (0, h))
    head = pl.BlockSpec((s, HEAD_DIM), lambda h: (0, h))
    return heads, col, vec, mat, cw, head


def _lru_fwd(proj_b, conv_w, conv_b, wa, ba, wx, bx, lam):
    _, s, d = proj_b.shape
    heads, col, vec, mat, cws, head = _lru_specs(s, d)
    tt = LRU_TILE

    def body(xr_ref, xg_ref, cw_ref, cb_ref, wa_ref, ba_ref, wx_ref, bx_ref, lam_ref, y_ref, h_ref, xpad, xc_s):
        _pad_copy(xpad, xr_ref, s)
        _conv_rows(xc_s, xpad, cw_ref[...], cb_ref[...], s)
        sp = _softplus(-lam_ref[...])
        wav, wxv, bav, bxv = wa_ref[...], wx_ref[...], ba_ref[...], bx_ref[...]

        def tile(i, hc):
            rows = pl.ds(pl.multiple_of(i * tt, tt), tt)
            xc = xc_s[rows, :]
            _, ig, a, mult = _lru_gates(xc, wav, bav, wxv, bxv, sp)
            pa, hl = _scan_fwd(a, mult * (ig * xc), tt)
            h = hl + pa * hc
            h_ref[rows, :] = h
            gel, _ = _gelu(xg_ref[rows, :])
            y_ref[rows, :] = (h * gel).astype(y_ref.dtype)
            return h[tt - 1:tt, :]

        _unrolled_loop(s // tt, tile, jnp.zeros((1, HEAD_DIM), F32), LRU_UNROLL)

    return pl.pallas_call(
        body, name="lru_fwd", grid=(heads,),
        in_specs=[col(N_QKV), col(N_QKV + 1), cws, vec, mat, vec, mat, vec, vec],
        out_specs=[head, head],
        out_shape=[jax.ShapeDtypeStruct((s, d), BF16), jax.ShapeDtypeStruct((s, d), F32)],
        scratch_shapes=[pltpu.VMEM((s + 8, HEAD_DIM), F32), pltpu.VMEM((s, HEAD_DIM), F32)],
        compiler_params=_params(VMEM_STREAM_MIB),
    )(proj_b, proj_b, conv_w, conv_b, wa, ba, wx, bx, lam)


def _lru_bwd(proj_b, h_lru, dy, conv_w, conv_b, wa, ba, wx, bx, lam, dproj_b):
    _, s, d = proj_b.shape
    heads, col, vec, mat, cws, head = _lru_specs(s, d)
    tt = LRU_TILE
    n_t = s // tt

    def body(xr_ref, xg_ref, h_ref, dy_ref, cw_ref, cb_ref, wa_ref, ba_ref, wx_ref, bx_ref, lam_ref, alias_ref,
             out_ref, dcw_ref, dcb_ref, dwa_ref, dba_ref, dwx_ref, dbx_ref, dlam_ref, xpad, xc_s, dxc_s):
        del alias_ref
        _pad_copy(xpad, xr_ref, s)
        cwv = cw_ref[...]
        _conv_rows(xc_s, xpad, cwv, cb_ref[...], s)
        dxc_s[s:s + 8, :] = jnp.zeros((8, HEAD_DIM), F32)
        lamv = lam_ref[...]
        sp = _softplus(-lamv)
        wav, wxv, bav, bxv = wa_ref[...], wx_ref[...], ba_ref[...], bx_ref[...]
        dwa_ref[...] = jnp.zeros_like(dwa_ref)
        dwx_ref[...] = jnp.zeros_like(dwx_ref)
        zero = jnp.zeros((1, HEAD_DIM), F32)
        row = lax.broadcasted_iota(jnp.int32, (tt, HEAD_DIM), 0)

        def tile(it, carry):
            dh_next, a_next, dba, dbx, dsp, dcb = carry
            i = n_t - 1 - it
            t0 = pl.multiple_of(i * tt, tt)
            rows = pl.ds(t0, tt)
            xc = xc_s[rows, :]
            r, ig, a, mult = _lru_gates(xc, wav, bav, wxv, bxv, sp)
            h = h_ref[rows, :]
            before = h_ref[pl.ds(pl.multiple_of(jnp.maximum(t0 - 8, 0), 8), 8), :][7:8, :]
            before = before * (i > 0).astype(F32)
            h_prev = jnp.where(row == 0, before, pltpu.roll(h, 1, 0))
            xg = xg_ref[rows, :]
            dyv = dy_ref[rows, :]
            gel, th = _gelu(xg)
            out_ref[1, rows, :] = (dyv * h * _gelu_grad(xg, th)).astype(out_ref.dtype)
            b = jnp.where(row == tt - 1, a_next, pltpu.roll(a, tt - 1, 0))
            pb, z = _scan_bwd(b, dyv * gel, tt)
            dh = z + pb * dh_next
            da = dh * h_prev
            dmult = dh * (ig * xc)
            dig = dh * (mult * xc)
            dla = da * a - dmult * (a * a / mult)
            dzr = dla * (-LRU_C * sp) * (r * (1.0 - r))
            dzx = dig * (ig * (1.0 - ig))
            dxc = dh * (mult * ig) + _dot(dzr, wav, NT) + _dot(dzx, wxv, NT)
            dxc_s[rows, :] = dxc
            dwa_ref[...] += _dot(xc, dzr, TN)
            dwx_ref[...] += _dot(xc, dzx, TN)
            return (dh[0:1, :], a[0:1, :],
                    dba + jnp.sum(dzr, axis=0, keepdims=True),
                    dbx + jnp.sum(dzx, axis=0, keepdims=True),
                    dsp + jnp.sum(dla * (-LRU_C * r), axis=0, keepdims=True),
                    dcb + jnp.sum(dxc, axis=0, keepdims=True))

        _, _, dba, dbx, dsp, dcb = _unrolled_loop(n_t, tile, (zero, zero, zero, zero, zero, zero), LRU_UNROLL)
        dba_ref[...] = dba
        dbx_ref[...] = dbx
        dcb_ref[...] = dcb
        dlam_ref[...] = -dsp * _sigmoid(-lamv)
        dcw = [zero] * CONV_TAPS
        for c0 in range(0, s, ROW_CHUNK):
            dxc_c = dxc_s[c0:c0 + ROW_CHUNK, :]
            dxr = jnp.zeros((ROW_CHUNK, HEAD_DIM), F32)
            for j in range(CONV_TAPS):
                back = CONV_TAPS - 1 - j
                off = 8 - back + c0
                dcw[j] = dcw[j] + jnp.sum(dxc_c * xpad[off:off + ROW_CHUNK, :], axis=0, keepdims=True)
                dxr = dxr + cwv[j:j + 1, :] * dxc_s[back + c0:back + c0 + ROW_CHUNK, :]
            out_ref[0, c0:c0 + ROW_CHUNK, :] = dxr.astype(out_ref.dtype)
        for j in range(CONV_TAPS):
            dcw_ref[j:j + 1, :] = dcw[j]

    return pl.pallas_call(
        body, name="lru_bwd", grid=(heads,),
        in_specs=[col(N_QKV), col(N_QKV + 1), head, head, cws, vec, mat, vec, mat, vec, vec,
                  pl.BlockSpec(memory_space=pl.ANY)],
        out_specs=[pl.BlockSpec((2, s, HEAD_DIM), lambda h: (0, 0, h)),
                   pl.BlockSpec((CONV_TAPS, HEAD_DIM), lambda h: (0, h)), vec, mat, vec, mat, vec, vec],
        out_shape=[jax.ShapeDtypeStruct(dproj_b.shape, dproj_b.dtype),
                   jax.ShapeDtypeStruct((CONV_TAPS, d), F32), jax.ShapeDtypeStruct((1, d), F32),
                   jax.ShapeDtypeStruct(wa.shape, F32), jax.ShapeDtypeStruct((1, d), F32),
                   jax.ShapeDtypeStruct(wx.shape, F32), jax.ShapeDtypeStruct((1, d), F32),
                   jax.ShapeDtypeStruct((1, d), F32)],
        scratch_shapes=[pltpu.VMEM((s + 8, HEAD_DIM), F32), pltpu.VMEM((s, HEAD_DIM), F32),
                        pltpu.VMEM((s + 8, HEAD_DIM), F32)],
        input_output_aliases={11: 0},
        compiler_params=_params(VMEM_STREAM_MIB),
    )(proj_b, proj_b, h_lru, dy, conv_w, conv_b, wa, ba, wx, bx, lam, dproj_b)


def _place():
    x, y, c = (lax.axis_index(n) for n in AXES)
    return x, y, c


def _other_chips(x, y):
    return [(1 - x, y), (x, 1 - y), (1 - x, 1 - y)]


HBM = pl.BlockSpec(memory_space=pl.ANY)


def _cast_shard(w, chip_arr, name):
    r, cols = w.shape
    rh = r // 2
    tr = _row_tile(rh, cols * 4, STREAM_TILE)
    nt = rh // tr

    def body(chip_ref, w_ref, o_ref):
        del chip_ref
        o_ref[...] = w_ref[...].astype(BF16)

    return pl.pallas_call(
        body, name=name,
        grid_spec=pltpu.PrefetchScalarGridSpec(
            num_scalar_prefetch=1, grid=(2, nt),
            in_specs=[pl.BlockSpec((tr, cols), lambda h, i, chip_ref: (h * nt + i, 0))],
            out_specs=pl.BlockSpec((None, None, tr, cols), lambda h, i, chip_ref: (chip_ref[0], h, i, 0))),
        out_shape=pltpu.HBM((N_CHIPS, 2, rh, cols), BF16), compiler_params=_params(VMEM_STREAM_MIB),
    )(chip_arr, w)


HBM_SPEC = pl.BlockSpec(memory_space=pltpu.HBM)
SEM_SPEC = pl.BlockSpec(memory_space=pltpu.SEMAPHORE)
EFFECT = pltpu.SideEffectType.DATAFLOW_SIDE_EFFECTING
TOKEN = jax.ShapeDtypeStruct((8, 128), F32)
TOKEN_SPEC = pl.BlockSpec(memory_space=pltpu.VMEM)


def _in_hbm(arrays):
    return [pltpu.with_memory_space_constraint(a, pltpu.HBM) for a in arrays]


def _hbm_like(arrays):
    return [pltpu.HBM(a.shape, a.dtype) for a in arrays]


def _sems(n):
    return pltpu.SemaphoreType.DMA((n,))


def _remote(src, dst, send_sem, recv_sem, to):
    return pltpu.make_async_remote_copy(src_ref=src, dst_ref=dst, send_sem=send_sem, recv_sem=recv_sem,
                                        device_id=to, device_id_type=MESH)


ALL_FLIPS = (0, 1, 2)


def _gather_start(bufs, groups, after, name):
    n = len(bufs)
    ng = len(groups)

    def body(*refs):
        ins = refs[:n]
        sems = refs[n + len(after):n + len(after) + 2 * ng]
        x, y, c = _place()
        me = 2 * x + y
        chips = _other_chips(x, y)
        for g, (ws, flips) in enumerate(groups):
            for i, w in enumerate(ws):
                for jj, j in enumerate(flips):
                    k = len(flips) * i + jj
                    mine = ins[w].at[me, c]
                    _remote(mine, mine, sems[2 * g].at[k], sems[2 * g + 1].at[k], (*chips[j], c)).start()

    sem_shapes = []
    for ws, flips in groups:
        sem_shapes += [_sems(len(flips) * len(ws))] * 2
    res = pl.pallas_call(
        body, name=name, in_specs=[HBM_SPEC] * n + [HBM] * len(after),
        out_specs=[SEM_SPEC] * (2 * ng) + [HBM_SPEC] * n, out_shape=sem_shapes + _hbm_like(bufs),
        input_output_aliases={w: 2 * ng + w for w in range(n)},
        compiler_params=pltpu.CompilerParams(has_side_effects=EFFECT),
    )(*_in_hbm(bufs), *after)
    return [(res[2 * g], res[2 * g + 1]) for g in range(ng)], list(res[2 * ng:])


def _gather_forward(bufs, recv, after, name, flips=ALL_FLIPS):
    m = len(bufs)
    nf = len(flips)

    def body(*refs):
        ins, recv_in = refs[:m], refs[m]
        fsend, frecv = refs[m + 1 + len(after)], refs[m + 2 + len(after)]
        x, y, c = _place()
        chips = _other_chips(x, y)
        for jj, j in enumerate(flips):
            cx, cy = chips[j]
            for i in range(m):
                landed = ins[i].at[2 * cx + cy, c]
                k = nf * i + jj
                _remote(landed, landed, fsend.at[k], recv_in.at[k], (cx, cy, c)).wait_recv()
                _remote(landed, landed, fsend.at[k], frecv.at[k], (x, y, 1 - c)).start()

    res = pl.pallas_call(
        body, name=name, in_specs=[HBM_SPEC] * m + [SEM_SPEC] + [HBM] * len(after),
        out_specs=[SEM_SPEC, SEM_SPEC] + [HBM_SPEC] * m, out_shape=[_sems(nf * m), _sems(nf * m)] + _hbm_like(bufs),
        input_output_aliases={i: 2 + i for i in range(m)},
        compiler_params=pltpu.CompilerParams(has_side_effects=EFFECT),
    )(*bufs, recv, *after)
    return (res[0], res[1]), list(res[2:])


NEIGHBOURS = (0, 1)


def _relay_partner(x, y, c):
    return 2 * (x ^ (1 - c)) + (y ^ c), (x ^ c, y ^ (1 - c))


def _gather_forward_relay(bufs, recv, after, name):
    m = len(bufs)
    nf = len(NEIGHBOURS)

    def body(*refs):
        ins, recv_in = refs[:m], refs[m]
        fsend, frecv, rsend, rrecv = refs[m + 1 + len(after):m + 5 + len(after)]
        x, y, c = _place()
        chips = _other_chips(x, y)
        for jj, j in enumerate(NEIGHBOURS):
            cx, cy = chips[j]
            for i in range(m):
                landed = ins[i].at[2 * cx + cy, c]
                k = nf * i + jj
                _remote(landed, landed, fsend.at[k], recv_in.at[k], (cx, cy, c)).wait_recv()
        row, (px, py) = _relay_partner(x, y, c)
        for i in range(m):
            relayed = ins[i].at[row, c]
            _remote(relayed, relayed, rsend.at[i], rrecv.at[i], (px, py, c)).start()
        for jj, j in enumerate(NEIGHBOURS):
            cx, cy = chips[j]
            for i in range(m):
                landed = ins[i].at[2 * cx + cy, c]
                k = nf * i + jj
                _remote(landed, landed, fsend.at[k], frecv.at[k], (x, y, 1 - c)).start()

    res = pl.pallas_call(
        body, name=name, in_specs=[HBM_SPEC] * m + [SEM_SPEC] + [HBM] * len(after),
        out_specs=[SEM_SPEC] * 4 + [HBM_SPEC] * m,
        out_shape=[_sems(nf * m), _sems(nf * m), _sems(m), _sems(m)] + _hbm_like(bufs),
        input_output_aliases={i: 4 + i for i in range(m)},
        compiler_params=pltpu.CompilerParams(has_side_effects=EFFECT),
    )(*bufs, recv, *after)
    return tuple(res[:4]), list(res[4:])


def _gather_forward_diag(bufs, rrecv, after, name):
    m = len(bufs)

    def body(*refs):
        ins, rrecv_in = refs[:m], refs[m]
        dsend, drecv = refs[m + 1 + len(after)], refs[m + 2 + len(after)]
        x, y, c = _place()
        diag = (2 * x + y) ^ 3
        _, (px, py) = _relay_partner(x, y, c)
        for i in range(m):
            landed = ins[i].at[diag, c]
            _remote(landed, landed, dsend.at[i], rrecv_in.at[i], (px, py, c)).wait_recv()
            _remote(landed, landed, dsend.at[i], drecv.at[i], (x, y, 1 - c)).start()

    res = pl.pallas_call(
        body, name=name, in_specs=[HBM_SPEC] * m + [SEM_SPEC] + [HBM] * len(after),
        out_specs=[SEM_SPEC, SEM_SPEC] + [HBM_SPEC] * m, out_shape=[_sems(m), _sems(m)] + _hbm_like(bufs),
        input_output_aliases={i: 2 + i for i in range(m)},
        compiler_params=pltpu.CompilerParams(has_side_effects=EFFECT),
    )(*bufs, rrecv, *after)
    return (res[0], res[1]), list(res[2:])


def _gather_finish_diag(bufs, rsend, dsend, drecv, after, name):
    m = len(bufs)

    def body(*refs):
        ins = refs[:m]
        rsend_in, dsend_in, drecv_in = refs[m:m + 3]
        x, y, c = _place()
        diag = (2 * x + y) ^ 3
        row, (px, py) = _relay_partner(x, y, c)
        for i in range(m):
            relayed = ins[i].at[row, c]
            _remote(relayed, relayed, rsend_in.at[i], drecv_in.at[i], (px, py, c)).wait_send()
            landed = ins[i].at[diag, c]
            _remote(landed, landed, dsend_in.at[i], drecv_in.at[i], (x, y, 1 - c)).wait_send()
            theirs = ins[i].at[diag, 1 - c]
            _remote(theirs, theirs, dsend_in.at[i], drecv_in.at[i], (x, y, 1 - c)).wait_recv()

    return list(pl.pallas_call(
        body, name=name, in_specs=[HBM_SPEC] * m + [SEM_SPEC] * 3 + [HBM] * len(after),
        out_specs=[HBM_SPEC] * m, out_shape=_hbm_like(bufs),
        input_output_aliases={i: i for i in range(m)},
        compiler_params=pltpu.CompilerParams(has_side_effects=EFFECT),
    )(*bufs, rsend, dsend, drecv, *after))


def _gather_finish(bufs, send, fsend, frecv, after, name, flips=ALL_FLIPS):
    m = len(bufs)
    nf = len(flips)

    def body(*refs):
        ins = refs[:m]
        send_in, fsend_in, frecv_in = refs[m:m + 3]
        x, y, c = _place()
        me = 2 * x + y
        chips = _other_chips(x, y)
        for jj, j in enumerate(flips):
            cx, cy = chips[j]
            cj = 2 * cx + cy
            for i in range(m):
                k = nf * i + jj
                mine = ins[i].at[me, c]
                _remote(mine, mine, send_in.at[k], frecv_in.at[k], (cx, cy, c)).wait_send()
                landed = ins[i].at[cj, c]
                _remote(landed, landed, fsend_in.at[k], frecv_in.at[k], (x, y, 1 - c)).wait_send()
                theirs = ins[i].at[cj, 1 - c]
                _remote(theirs, theirs, fsend_in.at[k], frecv_in.at[k], (x, y, 1 - c)).wait_recv()

    return list(pl.pallas_call(
        body, name=name, in_specs=[HBM_SPEC] * m + [SEM_SPEC] * 3 + [HBM] * len(after),
        out_specs=[HBM_SPEC] * m, out_shape=_hbm_like(bufs),
        input_output_aliases={i: i for i in range(m)},
        compiler_params=pltpu.CompilerParams(has_side_effects=EFFECT),
    )(*bufs, send, fsend, frecv, *after))


def _pair_exchange(grads, name):
    n = len(grads)

    def body(*refs):
        ins, outs = refs[:n], refs[n:2 * n]
        send_sems, recv_sems = refs[2 * n:]
        x, y, c = _place()
        sibling = (x, y, 1 - c)
        cps = []
        for w in range(n):
            for j in range(N_CHIPS):
                cp = pltpu.make_async_remote_copy(
                    src_ref=ins[w].at[j, 1 - c], dst_ref=outs[w].at[j], send_sem=send_sems.at[N_CHIPS * w + j],
                    recv_sem=recv_sems.at[N_CHIPS * w + j], device_id=sibling, device_id_type=MESH)
                cp.start()
                cps.append(cp)
        for cp in cps:
            cp.wait()

    return pl.pallas_call(
        body, name=name, in_specs=[HBM] * n, out_specs=[HBM] * n,
        out_shape=[jax.ShapeDtypeStruct((N_CHIPS,) + a.shape[2:], a.dtype) for a in grads],
        scratch_shapes=[pltpu.SemaphoreType.DMA((N_CHIPS * n,)), pltpu.SemaphoreType.DMA((N_CHIPS * n,))],
    )(*grads)


def _pair_start(grads, name):
    n = len(grads)
    lands = [lax.empty((N_CHIPS,) + a.shape[2:], a.dtype) for a in grads]

    def body(*refs):
        ins, land_in = refs[:n], refs[n:2 * n]
        send, recv = refs[2 * n], refs[2 * n + 1]
        token = refs[4 * n + 2]
        x, y, c = _place()
        for w in range(n):
            for j in range(N_CHIPS):
                k = N_CHIPS * w + j
                _remote(ins[w].at[j, 1 - c], land_in[w].at[j], send.at[k], recv.at[k], (x, y, 1 - c)).start()
        token[...] = jnp.zeros_like(token)

    res = pl.pallas_call(
        body, name=name, in_specs=[HBM_SPEC] * (2 * n),
        out_specs=[SEM_SPEC, SEM_SPEC] + [HBM_SPEC] * (2 * n) + [TOKEN_SPEC],
        out_shape=[_sems(N_CHIPS * n), _sems(N_CHIPS * n)] + _hbm_like(grads) + _hbm_like(lands) + [TOKEN],
        input_output_aliases={i: 2 + i for i in range(2 * n)},
        compiler_params=pltpu.CompilerParams(has_side_effects=EFFECT),
    )(*_in_hbm(grads), *_in_hbm(lands))
    return (res[0], res[1]), list(res[2:2 + n]), list(res[2 + n:2 + 2 * n]), res[2 + 2 * n]


def _pair_wait(sems, grads, lands, after, name):
    n = len(grads)

    def body(*refs):
        ins, land_in = refs[:n], refs[n:2 * n]
        send_in, recv_in = refs[2 * n], refs[2 * n + 1]
        x, y, c = _place()
        for w in range(n):
            for j in range(N_CHIPS):
                k = N_CHIPS * w + j
                cp = _remote(ins[w].at[j, 1 - c], land_in[w].at[j], send_in.at[k], recv_in.at[k], (x, y, 1 - c))
                cp.wait_send()
                cp.wait_recv()

    res = pl.pallas_call(
        body, name=name, in_specs=[HBM_SPEC] * (2 * n) + [SEM_SPEC, SEM_SPEC] + [HBM] * len(after),
        out_specs=[HBM_SPEC] * (2 * n), out_shape=_hbm_like(grads) + _hbm_like(lands),
        input_output_aliases={i: i for i in range(2 * n)},
        compiler_params=pltpu.CompilerParams(has_side_effects=EFFECT),
    )(*grads, *lands, sems[0], sems[1], *after)
    return list(res[:n]), list(res[n:])


def _chip_start(sums, name, to_all=()):
    m = len(sums)
    lands = [lax.empty(((N_CHIPS,) if i in to_all else ()) + a.shape, a.dtype) for i, a in enumerate(sums)]

    def body(*refs):
        ins, land_in = refs[:m], refs[m:2 * m]
        send, recv = refs[2 * m], refs[2 * m + 1]
        token = refs[4 * m + 2]
        x, y, c = _place()
        me = 2 * x + y
        for i in sorted(range(m), key=lambda i: i not in to_all):
            for j, (cx, cy) in enumerate(_other_chips(x, y)):
                src = ins[i] if i in to_all else ins[i].at[2 * cx + cy]
                _remote(src, land_in[i].at[me], send.at[3 * i + j], recv.at[3 * i + j], (cx, cy, c)).start()
        token[...] = jnp.zeros_like(token)

    res = pl.pallas_call(
        body, name=name, in_specs=[HBM_SPEC] * (2 * m),
        out_specs=[SEM_SPEC, SEM_SPEC] + [HBM_SPEC] * (2 * m) + [TOKEN_SPEC],
        out_shape=[_sems(3 * m), _sems(3 * m)] + _hbm_like(sums) + _hbm_like(lands) + [TOKEN],
        input_output_aliases={i: 2 + i for i in range(2 * m)},
        compiler_params=pltpu.CompilerParams(has_side_effects=EFFECT),
    )(*_in_hbm(sums), *_in_hbm(lands))
    return (res[0], res[1]), list(res[2:2 + m]), list(res[2 + m:2 + 2 * m]), res[2 + 2 * m]


def _chip_wait(sems, sums, lands, after, name, to_all=()):
    m = len(sums)

    def body(*refs):
        ins, land_in = refs[:m], refs[m:2 * m]
        send_in, recv_in = refs[2 * m], refs[2 * m + 1]
        x, y, c = _place()
        for i in range(m):
            for j, (cx, cy) in enumerate(_other_chips(x, y)):
                cj = 2 * cx + cy
                src = ins[i] if i in to_all else ins[i].at[cj]
                cp = _remote(src, land_in[i].at[cj], send_in.at[3 * i + j], recv_in.at[3 * i + j], (cx, cy, c))
                cp.wait_send()
                cp.wait_recv()

    res = pl.pallas_call(
        body, name=name, in_specs=[HBM_SPEC] * (2 * m) + [SEM_SPEC, SEM_SPEC] + [HBM] * len(after),
        out_specs=[HBM_SPEC] * (2 * m), out_shape=_hbm_like(sums) + _hbm_like(lands),
        input_output_aliases={i: i for i in range(2 * m)},
        compiler_params=pltpu.CompilerParams(has_side_effects=EFFECT),
    )(*sums, *lands, sems[0], sems[1], *after)
    return list(res[:m]), list(res[m:])


def _half_parts(bufs):
    parts = []
    for w, a in enumerate(bufs):
        parts += [(w, None)] if a.ndim == 3 else [(w, j) for j in range(a.shape[0])]
    return parts


def _half_ref(refs, w, j, h):
    return refs[w].at[h] if j is None else refs[w].at[j, h]


def _half_start(bufs, name):
    n = len(bufs)
    parts = _half_parts(bufs)

    def body(*refs):
        ins = refs[:n]
        send, recv = refs[n], refs[n + 1]
        x, y, c = _place()
        for k, (w, j) in enumerate(parts):
            mine = _half_ref(ins, w, j, c)
            _remote(mine, mine, send.at[k], recv.at[k], (x, y, 1 - c)).start()

    res = pl.pallas_call(
        body, name=name, in_specs=[HBM_SPEC] * n, out_specs=[SEM_SPEC, SEM_SPEC] + [HBM_SPEC] * n,
        out_shape=[_sems(len(parts)), _sems(len(parts))] + _hbm_like(bufs),
        input_output_aliases={w: 2 + w for w in range(n)},
        compiler_params=pltpu.CompilerParams(has_side_effects=EFFECT),
    )(*_in_hbm(bufs))
    return (res[0], res[1]), list(res[2:])


def _half_wait(sems, bufs, after, name):
    n = len(bufs)
    parts = _half_parts(bufs)

    def body(*refs):
        ins = refs[:n]
        send_in, recv_in = refs[n], refs[n + 1]
        x, y, c = _place()
        for k, (w, j) in enumerate(parts):
            mine = _half_ref(ins, w, j, c)
            _remote(mine, mine, send_in.at[k], recv_in.at[k], (x, y, 1 - c)).wait_send()
            theirs = _half_ref(ins, w, j, 1 - c)
            _remote(theirs, theirs, send_in.at[k], recv_in.at[k], (x, y, 1 - c)).wait_recv()

    return list(pl.pallas_call(
        body, name=name, in_specs=[HBM_SPEC] * n + [SEM_SPEC, SEM_SPEC] + [HBM] * len(after),
        out_specs=[HBM_SPEC] * n, out_shape=_hbm_like(bufs),
        input_output_aliases={w: w for w in range(n)},
        compiler_params=pltpu.CompilerParams(has_side_effects=EFFECT),
    )(*bufs, sems[0], sems[1], *after))


def _all_gather8(block, name, after=()):
    def body(in_ref, *rest):
        out_ref, send_sems, recv_sems, local_sem = rest[len(after):]
        x, y, c = _place()
        me = 4 * x + 2 * y + c
        mine = pltpu.make_async_copy(in_ref, out_ref.at[me], local_sem)
        mine.start()
        flips = [(fx, fy, fc) for fx in (0, 1) for fy in (0, 1) for fc in (0, 1)][1:]
        cps = []
        for k, (fx, fy, fc) in enumerate(flips):
            cp = pltpu.make_async_remote_copy(
                src_ref=in_ref, dst_ref=out_ref.at[me], send_sem=send_sems.at[k], recv_sem=recv_sems.at[k],
                device_id=(x ^ fx, y ^ fy, c ^ fc), device_id_type=MESH)
            cp.start()
            cps.append(cp)
        for k, (fx, fy, fc) in enumerate(flips):
            px, py, pc = x ^ fx, y ^ fy, c ^ fc
            theirs = out_ref.at[4 * px + 2 * py + pc]
            pltpu.make_async_remote_copy(
                src_ref=theirs, dst_ref=theirs, send_sem=send_sems.at[k], recv_sem=recv_sems.at[k],
                device_id=(px, py, pc), device_id_type=MESH).wait_recv()
        for cp in cps:
            cp.wait_send()
        mine.wait()

    return pl.pallas_call(
        body, name=name, in_specs=[HBM] * (1 + len(after)), out_specs=HBM,
        out_shape=jax.ShapeDtypeStruct((N_DEV,) + block.shape, block.dtype),
        scratch_shapes=[pltpu.SemaphoreType.DMA((N_DEV - 1,)), pltpu.SemaphoreType.DMA((N_DEV - 1,)),
                        pltpu.SemaphoreType.DMA],
    )(block, *after)


def _pair_sum(grad, recv, c_arr, name):
    _, _, rh, cols = grad.shape
    tr = _row_tile(rh, cols * grad.dtype.itemsize, STREAM_TILE // 2)

    def body(c_ref, g_ref, r_ref, o_ref):
        del c_ref
        o_ref[...] = (g_ref[...].astype(F32) + r_ref[...].astype(F32)).astype(o_ref.dtype)

    spec = pl.BlockSpec((None, tr, cols), lambda j, i, c_ref: (j, i, 0))
    return pl.pallas_call(
        body, name=name,
        grid_spec=pltpu.PrefetchScalarGridSpec(
            num_scalar_prefetch=1, grid=(N_CHIPS, rh // tr),
            in_specs=[pl.BlockSpec((None, None, tr, cols), lambda j, i, c_ref: (j, c_ref[0], i, 0)), spec],
            out_specs=spec),
        out_shape=pltpu.HBM(recv.shape, grad.dtype), compiler_params=_params(VMEM_STREAM_MIB),
    )(c_arr, grad, recv)


def _sum_by_chip(chip, p_ref, own_ref, o_ref):
    o_ref[...] = jnp.zeros_like(o_ref)
    for k in range(N_CHIPS):
        @pl.when(chip == k)
        def _():
            o_ref[...] += own_ref[...].astype(F32)

        @pl.when(chip != k)
        def _(k=k):
            o_ref[...] += p_ref[k].astype(F32)


def _chip_sum_all(parts, own, place_arr, name):
    _, nj, rh, cols = parts.shape
    tr = _row_tile(rh, cols * 4, STREAM_TILE // 2)

    def body(place_ref, p_ref, own_ref, o_ref):
        _sum_by_chip(place_ref[0], p_ref, own_ref, o_ref)

    return pl.pallas_call(
        body, name=name,
        grid_spec=pltpu.PrefetchScalarGridSpec(
            num_scalar_prefetch=1, grid=(nj, rh // tr),
            in_specs=[pl.BlockSpec((N_CHIPS, None, tr, cols), lambda j, i, place_ref: (0, j, i, 0)),
                      pl.BlockSpec((None, tr, cols), lambda j, i, place_ref: (j, i, 0))],
            out_specs=pl.BlockSpec((None, None, tr, cols), lambda j, i, place_ref: (j, place_ref[1], i, 0))),
        out_shape=pltpu.HBM((nj, 2, rh, cols), F32), compiler_params=_params(VMEM_STREAM_MIB),
    )(place_arr, parts, own)


def _chip_sum(parts, own, place_arr, name):
    _, rh, cols = parts.shape
    tr = _row_tile(rh, cols * 4, STREAM_TILE // 2)

    def body(place_ref, p_ref, own_ref, o_ref):
        _sum_by_chip(place_ref[0], p_ref, own_ref, o_ref)

    return pl.pallas_call(
        body, name=name,
        grid_spec=pltpu.PrefetchScalarGridSpec(
            num_scalar_prefetch=1, grid=(rh // tr,),
            in_specs=[pl.BlockSpec((N_CHIPS, tr, cols), lambda i, place_ref: (0, i, 0)),
                      pl.BlockSpec((None, tr, cols), lambda i, place_ref: (place_ref[0], i, 0))],
            out_specs=pl.BlockSpec((None, tr, cols), lambda i, place_ref: (place_ref[1], i, 0))),
        out_shape=pltpu.HBM((2, rh, cols), F32), compiler_params=_params(VMEM_STREAM_MIB),
    )(place_arr, parts, own)


def _adamw_math(w, g, m, v):
    m = ADAM_B1 * m + (1.0 - ADAM_B1) * g
    v = ADAM_B2 * v + (1.0 - ADAM_B2) * (g * g)
    m_hat = m / (1.0 - ADAM_B1 ** ADAM_STEP)
    v_hat = v / (1.0 - ADAM_B2 ** ADAM_STEP)
    delta = -ADAM_LR * (m_hat / (jnp.sqrt(v_hat) + ADAM_EPS) + ADAM_WD * w)
    return delta, m, v


def _adamw(w, g, m, v, name):
    rows, cols = w.shape
    tr = _row_tile(rows, cols * 4)

    def body(w_ref, g_ref, m_ref, v_ref, go_ref, d_ref, nm_ref, nv_ref):
        gv = g_ref[...]
        go_ref[...] = gv
        d_ref[...], nm_ref[...], nv_ref[...] = _adamw_math(w_ref[...], gv, m_ref[...], v_ref[...])

    spec = pl.BlockSpec((tr, cols), lambda i: (i, 0))
    return pl.pallas_call(
        body, name=name, grid=(rows // tr,), in_specs=[spec] * 4, out_specs=[spec] * 4,
        out_shape=[jax.ShapeDtypeStruct(w.shape, F32)] * 4, compiler_params=_params(VMEM_STREAM_MIB),
    )(w, g, m, v)


def _sum8_adamw_row(parts, w, m, v, name):
    cols = parts.shape[2]

    def body(p_ref, w_ref, m_ref, v_ref, g_ref, d_ref, nm_ref, nv_ref):
        g = p_ref[0, 0:1, :]
        for k in range(1, N_DEV):
            g = g + p_ref[k, 0:1, :]
        g_ref[...] = g
        d_ref[...], nm_ref[...], nv_ref[...] = _adamw_math(w_ref[...], g, m_ref[...], v_ref[...])

    return pl.pallas_call(
        body, name=name, out_shape=[jax.ShapeDtypeStruct((1, cols), F32)] * 4, compiler_params=_params(VMEM_STREAM_MIB),
    )(parts, w, m, v)


def _pack_rows(pieces, rows, name):
    cols = pieces[0].shape[1]
    n = len(pieces)

    def body(*refs):
        o_ref = refs[n]
        o_ref[...] = jnp.zeros_like(o_ref)
        at = 0
        for p_ref in refs[:n]:
            r = p_ref.shape[0]
            o_ref[at:at + r, :] = p_ref[...]
            at += r

    return pl.pallas_call(
        body, name=name, out_shape=jax.ShapeDtypeStruct((rows, cols), F32), compiler_params=_params(VMEM_STREAM_MIB),
    )(*pieces)


def kernel(x, norm_mix_g, w_in, conv_w, conv_b, lru_wa, lru_ba, lru_wx, lru_bx, lru_lambda, w_proj_attn, w_proj_lru, w_out, norm_mlp_g, w_up, w_down, norm_final_g, loss_target, m_norm_mix_g, m_w_in, m_conv_w, m_conv_b, m_lru_wa, m_lru_ba, m_lru_wx, m_lru_bx, m_lru_lambda, m_w_proj_attn, m_w_proj_lru, m_w_out, m_norm_mlp_g, m_w_up, m_w_down, m_norm_final_g, v_norm_mix_g, v_w_in, v_conv_w, v_conv_b, v_lru_wa, v_lru_ba, v_lru_wx, v_lru_bx, v_lru_lambda, v_w_proj_attn, v_w_proj_lru, v_w_out, v_norm_mlp_g, v_w_up, v_w_down, v_norm_final_g):
    s, d = x.shape[1], x.shape[2]
    ff = w_up.shape[2] * N_CHIPS
    heads = d // HEAD_DIM
    u = d // 4
    assert s % (max(DILATIONS) * ATTN_BLK) == 0 and d % (4 * HEAD_DIM) == 0 and ff == 4 * d and DILATIONS[0] == 1
    xs, target = _in_hbm([x[0], loss_target[0]])
    gf = norm_final_g.reshape(1, d)
    wa, wx = lru_wa[0], lru_wx[0]
    core = lax.axis_index("c").astype(jnp.int32)
    chip = (2 * lax.axis_index("x") + lax.axis_index("y")).astype(jnp.int32)
    cidx = core.reshape(1)
    chip_arr = chip.reshape(1)
    place_arr = jnp.stack([chip, core])
    slopes = jnp.broadcast_to(
        (2.0 ** (-8.0 * jnp.arange(1, heads + 1, dtype=F32) / heads))[:, None, None], (heads, 1, HEAD_DIM))

    big = _in_hbm([w_in[0], w_proj_attn[0], w_proj_lru[0], w_out[0], w_up[0], w_down[0]])
    names = ["w_in", "w_proj_attn", "w_proj_lru", "w_out", "w_up", "w_down"]
    cw_pad = jnp.pad(conv_w[0], ((0, 8 - CONV_TAPS), (0, 0)))
    cw_all = _all_gather8(cw_pad, "gather_conv_w")
    conv_w_full = jnp.concatenate([cw_all[2 * j] for j in range(N_CHIPS)], axis=1)
    (sem_a,), buf_a = _gather_start([_cast_shard(big[0], chip_arr, "cast_w_in")], [([0], NEIGHBOURS)], [cw_all],
                                    "gather_start_w_in")

    xn = _rms_fwd(xs, norm_mix_g, "norm_mix")
    flip_bits = (2, 1, 3)

    def w_in_view():
        return buf_a[0].reshape(N_CHIPS, d, N_SLOTS * u)

    proj = _proj_in_shard(xn, w_in_view(), chip_arr, None, "proj_in_own")
    bufs = [_cast_shard(w, chip_arr, "cast_" + nm) for w, nm in zip(big[1:], names[1:])]
    (fs, fr, rs, rr), buf_a = _gather_forward_relay(buf_a, sem_a[1], [proj] + bufs, "gather_forward_w_in")
    buf_a = _gather_finish(buf_a, sem_a[0], fs, fr, [], "gather_finish_w_in", flips=NEIGHBOURS)
    for j in NEIGHBOURS:
        proj = _proj_in_shard(xn, w_in_view(), chip_arr ^ flip_bits[j], proj, "proj_in_from_%d" % j)
    (sem_b, sem_c, sem_d), bufs = _gather_start(
        bufs, [([0, 1, 2], ALL_FLIPS), ([3], ALL_FLIPS), ([4], ALL_FLIPS)], [proj], "gather_start_rest")
    (ds, dr), buf_a = _gather_forward_diag(buf_a, rr, [proj, bufs[0]], "gather_forward_w_in_diag")
    buf_a = _gather_finish_diag(buf_a, rs, ds, dr, [], "gather_finish_w_in_diag")
    proj = _proj_in_shard(xn, w_in_view(), chip_arr ^ flip_bits[2], proj, "proj_in_from_2")
    w_in_g = w_in_view()
    proj_a = proj_b = proj
    y_attn, lse = _attn_fwd(proj_a, slopes)
    y_lru, h_lru = _lru_fwd(proj_b, conv_w_full, conv_b, wa, lru_ba, wx, lru_bx, lru_lambda)
    fsem_b, buf_b = _gather_forward(bufs[:3], sem_b[1], [y_attn, y_lru], "gather_forward_proj")
    buf_b = _gather_finish(buf_b, sem_b[0], fsem_b[0], fsem_b[1], [], "gather_finish_proj")
    wpa_g = buf_b[0].reshape(d, d)
    wpl_g = buf_b[1].reshape(d, d)
    wout_g = buf_b[2].reshape(d, d)

    tn = u
    sd_f32 = jax.ShapeDtypeStruct((s, d), F32)
    sd_bf16 = jax.ShapeDtypeStruct((s, d), BF16)
    col = pl.BlockSpec((s, tn), lambda i, j, k: (0, j))

    def slot(n):
        return pl.BlockSpec((None, s, tn), lambda i, j, k: (n, 0, j))

    p_attn = _mm_nn("proj_attn", y_attn, wpa_g, [], [], [sd_f32], [col], _store, tn)[0]
    fsem_c, buf_c = _gather_forward(bufs[3:4], sem_c[1], [p_attn], "gather_forward_w_up")

    def merge(acc, extras, outs):
        pa_ref, ga_ref, gl_ref = extras
        merged = _sigmoid(ga_ref[...]) * pa_ref[...] + _sigmoid(gl_ref[...]) * acc
        outs[0][...] = merged.astype(BF16)
        outs[1][...] = acc

    tn2 = max(HEAD_DIM, u // 2)
    col2 = pl.BlockSpec((s, tn2), lambda i, j, k: (0, j))

    def slot2(n):
        return pl.BlockSpec((None, s, tn2), lambda i, j, k: (n, 0, j))

    merged, p_lru = _mm_nn("proj_lru_merge", y_lru, wpl_g, [p_attn, proj, proj], [col2, slot2(5), slot2(6)],
                           [sd_bf16, sd_f32], [col2, col2], merge, tn2)

    def add_resid(acc, extras, outs):
        outs[0][...] = extras[0][...] + acc

    h1 = _mm_nn("w_out_resid", merged, wout_g, [xs], [col], [sd_f32], [col], add_resid, tn)[0]
    hn = _rms_fwd(h1, norm_mlp_g, "norm_mlp")
    buf_c = _gather_finish(buf_c, sem_c[0], fsem_c[0], fsem_c[1], [hn], "gather_finish_w_up")
    wup_g = buf_c[0].reshape(N_CHIPS, d, d)

    def relu_sq(acc, extras, outs):
        r = jnp.maximum(acc, 0.0)
        outs[0][...] = (r * r).astype(BF16)
        outs[1][...] = r.astype(BF16)

    sf_bf16 = jax.ShapeDtypeStruct((s, ff), BF16)
    hid, relu_up = _mm(
        "w_up_relu2", [hn, wup_g],
        [pl.BlockSpec((s, d), lambda i, j, k: (0, 0)),
         pl.BlockSpec((None, d, tn), lambda i, j, k: (j // 4, 0, j % 4))],
        [sf_bf16, sf_bf16], [col, col], (1, ff // tn, 1), NN, relu_sq)
    fsem_d, buf_d = _gather_forward(bufs[4:], sem_d[1], [hid], "gather_forward_w_down")
    wdown_g = _gather_finish(buf_d, sem_d[0], fsem_d[0], fsem_d[1], [], "gather_finish_w_down")[0].reshape(ff, d)
    h2 = _mm(
        "w_down_resid", [hid, wdown_g, h1],
        [pl.BlockSpec((s, d), lambda i, j, k: (0, k)), pl.BlockSpec((d, tn), lambda i, j, k: (k, j)), col],
        [sd_f32], [col], (1, d // tn, ff // d), NN, add_resid, nk=ff // d, acc_shape=(s, tn))[0]
    loss_part, dh2, dh2_b, d_gf = _loss_head(h2, gf, target)
    loss = lax.psum(loss_part[0, 0], AXES)

    def relu_sq_bwd(acc, extras, outs):
        outs[0][...] = (acc * (2.0 * extras[0][...].astype(F32))).astype(BF16)

    dup = _mm_nt("d_hid", dh2_b, wdown_g, [relu_up], [col], [sf_bf16], [col], relu_sq_bwd, tn)[0]
    tok_d = pl.BlockSpec((s, d), lambda i, j: (0, 0))
    g_wdown = _mm_tn(
        "g_w_down", hid, dh2_b, pl.BlockSpec((s, d), lambda i, j: (0, i)),
        pl.BlockSpec((s, tn), lambda i, j: (0, j)), jax.ShapeDtypeStruct((ff, d), BF16),
        pl.BlockSpec((d, tn), lambda i, j: (i, j)), (ff // d, d // tn), d, tn, s)
    dhn = _mm(
        "d_hn", [dup, wup_g],
        [pl.BlockSpec((s, d), lambda i, j, k: (0, k)), pl.BlockSpec((None, tn, d), lambda i, j, k: (k, j, 0))],
        [sd_f32], [col], (1, d // tn, ff // d), NT, _store, nk=ff // d, acc_shape=(s, tn))[0]
    g_wup = _mm_tn(
        "g_w_up", hn, dup, tok_d, pl.BlockSpec((s, tn), lambda i, j: (0, j)),
        jax.ShapeDtypeStruct((N_CHIPS, d, d), BF16), pl.BlockSpec((None, d, tn), lambda i, j: (j // 4, 0, j % 4)),
        (1, ff // tn), d, tn, s)
    big_m = _in_hbm([m_w_in[0], m_w_proj_attn[0], m_w_proj_lru[0], m_w_out[0], m_w_up[0], m_w_down[0]])
    big_v = _in_hbm([v_w_in[0], v_w_proj_attn[0], v_w_proj_lru[0], v_w_out[0], v_w_up[0], v_w_down[0]])
    big_out = {}

    def reduce_begin(ids, gs, tag, everywhere=None):
        g4 = [g.reshape(N_CHIPS, 2, big[i].shape[0] // 2, big[i].shape[1]) for i, g in zip(ids, gs)]
        tags = [names[i] for i in ids]
        if everywhere is not None:
            g4.append(everywhere.reshape(N_CHIPS, 2, everywhere.shape[0] // (2 * N_CHIPS), everywhere.shape[1]))
            tags.append("small_" + tag)
        from_sibling = _pair_exchange(g4, "pair_exchange_" + tag)
        sums = [_pair_sum(g, r, cidx, "pair_sum_" + t) for t, g, r in zip(tags, g4, from_sibling)]
        to_all = () if everywhere is None else (len(ids),)
        return _chip_start(sums, "chip_start_" + tag, to_all), to_all

    def pair_begin(ids, gs, tag):
        g4 = [g.reshape(N_CHIPS, 2, big[i].shape[0] // 2, big[i].shape[1]) for i, g in zip(ids, gs)]
        return _pair_start(g4, "pair_start_" + tag)

    def reduce_begin_paired(ids, paired, after, tag):
        sems, g4, lands, _ = paired
        g4, from_sibling = _pair_wait(sems, g4, lands, after, "pair_wait_" + tag)
        sums = [_pair_sum(g, r, cidx, "pair_sum_" + names[i]) for i, g, r in zip(ids, g4, from_sibling)]
        return _chip_start(sums, "chip_start_" + tag), ()

    def reduce_mid(ids, begun, after, tag):
        (sems, sums, lands, _), to_all = begun
        sums, lands = _chip_wait(sems, sums, lands, after, "chip_wait_" + tag, to_all)
        halves = [_chip_sum(p, own, place_arr, "chip_sum_" + names[i]) for i, p, own in zip(ids, lands, sums)]
        if to_all:
            halves.append(_chip_sum_all(lands[-1], sums[-1], place_arr, "chip_sum_small_" + tag))
        return _half_start(halves, "half_start_" + tag), to_all

    def reduce_end(ids, mid, after, tag):
        (hsems, halves), to_all = mid
        full = _half_wait(hsems, halves, after, "half_wait_" + tag)
        done = []
        for i, g in zip(ids, full):
            res = _adamw(big[i], g.reshape(big[i].shape), big_m[i], big_v[i], "adamw_" + names[i])
            big_out[names[i]] = tuple(a[None] for a in res)
            done.append(res[1])
        everywhere = full[-1].reshape(-1, full[-1].shape[-1]) if to_all else None
        return everywhere, done

    def after_token(a, begun):
        return a + begun[0][3][:1, :1]

    pair_mlp = pair_begin([4, 5], [g_wup, g_wdown], "mlp")
    dh1, dh1_b, d_gmlp = _rms_bwd(h1, norm_mlp_g + pair_mlp[3][:1, :1], dhn, dh2, "norm_mlp_bwd")

    g_wout = _mm_tn(
        "g_w_out", merged, dh1_b, tok_d, pl.BlockSpec((s, tn), lambda i, j: (0, j)),
        jax.ShapeDtypeStruct((d, d), BF16), pl.BlockSpec((d, tn), lambda i, j: (0, j)), (1, d // tn), d, tn, s)

    def merge_bwd(acc, extras, outs):
        pa_ref, pl_ref, ga_ref, gl_ref = extras
        sa, sl = _sigmoid(ga_ref[...]), _sigmoid(gl_ref[...])
        outs[0][...] = (acc * sa).astype(BF16)
        outs[1][...] = (acc * sl).astype(BF16)
        outs[2][0] = (acc * pa_ref[...] * (sa * (1.0 - sa))).astype(BF16)
        outs[2][1] = (acc * pl_ref[...] * (sl * (1.0 - sl))).astype(BF16)

    nb = N_SLOTS - N_QKV
    d_pa, d_pl, dproj_b = _mm_nt(
        "d_merged", dh1_b, wout_g, [p_attn, p_lru, proj, proj], [col2, col2, slot2(5), slot2(6)],
        [sd_bf16, sd_bf16, jax.ShapeDtypeStruct((nb, s, d), BF16)],
        [col2, col2, pl.BlockSpec((2, s, tn2), lambda i, j, k: (1, 0, j))], merge_bwd, tn2)
    red_mlp = reduce_begin_paired([4, 5], pair_mlp, [d_pa], "mlp")
    dy_attn = _mm_nt("d_y_attn", d_pa, wpa_g, [], [], [sd_f32], [col], _store, tn)[0]
    dy_lru = _mm_nt("d_y_lru", d_pl, wpl_g, [], [], [sd_f32], [col], _store, tn)[0]
    g_wpa = _mm_tn(
        "g_w_proj_attn", y_attn, d_pa, tok_d, pl.BlockSpec((s, tn), lambda i, j: (0, j)),
        jax.ShapeDtypeStruct((d, d), BF16), pl.BlockSpec((d, tn), lambda i, j: (0, j)), (1, d // tn), d, tn, s)
    g_wpl = _mm_tn(
        "g_w_proj_lru", y_lru, d_pl, tok_d, pl.BlockSpec((s, tn), lambda i, j: (0, j)),
        jax.ShapeDtypeStruct((d, d), BF16), pl.BlockSpec((d, tn), lambda i, j: (0, j)), (1, d // tn), d, tn, s)

    pair_proj = pair_begin([1, 2, 3], [g_wpa, g_wpl, g_wout], "proj")

    dproj_b, d_cw, d_cb, d_wa, d_ba, d_wx, d_bx, d_lam = _lru_bwd(
        proj_b, h_lru, dy_lru, conv_w_full, conv_b, wa, lru_ba, wx, lru_bx, lru_lambda + pair_proj[3][:1, :1],
        dproj_b)
    red_proj = reduce_begin_paired([1, 2, 3], pair_proj, [dproj_b], "proj")
    dproj_a = _attn_bwd(proj_a, after_token(slopes, red_proj), y_attn, lse, dy_attn)
    per = N_SLOTS
    g_win_shape = jax.ShapeDtypeStruct((N_CHIPS, d, N_SLOTS * u), BF16)

    def g_win_part(name, dproj, first, prev):
        n_units = 4 * dproj.shape[0]
        return _mm_tn(
            name, xn, dproj, tok_d, pl.BlockSpec((None, s, u), lambda i, j: (j // 4, 0, j % 4)),
            g_win_shape, pl.BlockSpec((None, d, u), lambda i, j: ((j + first) // per, 0, (j + first) % per)),
            (1, n_units), d, u, s, aliases=None if prev is None else {2: 0}, extra=prev)

    mat_rows = heads * HEAD_DIM * HEAD_DIM // d
    vec_names = ["norm_mix_g", "conv_b", "lru_ba", "lru_bx", "lru_lambda", "norm_mlp_g", "norm_final_g"]

    def pack(wa_, wx_, cw_, vecs, name):
        rows = [wa_.reshape(mat_rows, d), wx_.reshape(mat_rows, d), cw_] + [a.reshape(1, d) for a in vecs]
        n = sum(a.shape[0] for a in rows)
        return _pack_rows(rows, n + (-n % 64), name)

    zero_cw = jnp.zeros((CONV_TAPS, d), F32)
    small_g = pack(d_wa, d_wx, d_cw, [jnp.zeros((1, d), F32), d_cb, d_ba, d_bx, d_lam, d_gmlp, d_gf], "pack_small_g")
    g_win = g_win_part("g_w_in_qkv", dproj_a, 0, None)
    g_win = g_win_part("g_w_in_rest", dproj_b, 4 * N_QKV, g_win)
    red_in = reduce_begin([0], [g_win], "w_in", everywhere=small_g)
    dxn = _dxn(dproj_a, dproj_b, w_in_g, d, [red_in[0][3]])
    grad_x, _, d_gmix = _rms_bwd(xs, norm_mix_g, dxn, dh1, "norm_mix_bwd")

    mid_mlp = reduce_mid([4, 5], red_mlp, [grad_x], "mlp")
    mid_proj = reduce_mid([1, 2, 3], red_proj, [mid_mlp[0][1][0]], "proj")
    _, done_mlp = reduce_end([4, 5], mid_mlp, [mid_proj[0][1][0]], "mlp")
    _, done_proj = reduce_end([1, 2, 3], mid_proj, done_mlp[-1:], "proj")
    done = done_mlp + done_proj
    small_w = pack(wa, wx, zero_cw, [norm_mix_g, conv_b, lru_ba, lru_bx, lru_lambda, norm_mlp_g, norm_final_g],
                   "pack_small_w")
    small_m = pack(m_lru_wa[0], m_lru_wx[0], zero_cw,
                   [m_norm_mix_g, m_conv_b, m_lru_ba, m_lru_bx, m_lru_lambda, m_norm_mlp_g, m_norm_final_g],
                   "pack_small_m")
    small_v = pack(v_lru_wa[0], v_lru_wx[0], zero_cw,
                   [v_norm_mix_g, v_conv_b, v_lru_ba, v_lru_bx, v_lru_lambda, v_norm_mlp_g, v_norm_final_g],
                   "pack_small_v")
    mid_in = reduce_mid([0], red_in, done + [small_w, small_m, small_v], "w_in")
    gmix_parts = _all_gather8(jnp.pad(d_gmix, ((0, 7), (0, 0))), "gather_gain_grad", [mid_in[0][1][0]])
    gmix_out = _sum8_adamw_row(gmix_parts, norm_mix_g, m_norm_mix_g, v_norm_mix_g, "sum_adamw_norm_mix_g")
    small_sum, _ = reduce_end([0], mid_in, [gmix_out[1]], "w_in")
    small = _adamw(small_w, small_sum, small_m, small_v, "adamw_small")
    g_cw = lax.dynamic_slice(small_sum[2 * mat_rows:2 * mat_rows + CONV_TAPS], (0, chip * u), (CONV_TAPS, u))
    cw_out = _adamw(conv_w[0], g_cw, m_conv_w[0], v_conv_w[0], "adamw_conv_w")

    def small_leaf(kind, name):
        a = small[kind]
        if name == "norm_mix_g":
            return gmix_out[kind]
        if name == "lru_wa":
            return a[0:mat_rows].reshape(lru_wa.shape)
        if name == "lru_wx":
            return a[mat_rows:2 * mat_rows].reshape(lru_wx.shape)
        if name == "conv_w":
            return cw_out[kind][None]
        row = a[2 * mat_rows + CONV_TAPS + vec_names.index(name)]
        return row if name == "norm_final_g" else row[None]

    order = ["norm_mix_g", "w_in", "conv_w", "conv_b", "lru_wa", "lru_ba", "lru_wx", "lru_bx", "lru_lambda",
             "w_proj_attn", "w_proj_lru", "w_out", "norm_mlp_g", "w_up", "w_down", "norm_final_g"]
    outs = [loss, grad_x[None]]
    for kind in range(4):
        for name in order:
            outs.append(big_out[name][kind] if name in big_out else small_leaf(kind, name))
    return tuple(outs)
```

```python
import functools

import jax
import jax.numpy as jnp
from jax import lax
from jax.experimental import pallas as pl
from jax.experimental.pallas import tpu as pltpu

F32 = jnp.float32
BF16 = jnp.bfloat16
MESH = pl.DeviceIdType.MESH
AXES = ("x", "y", "c")

N_CHIPS = 4
N_DEV = 8
HEAD_DIM = 128
ATTN_BLK = 128
DILATIONS = (1, 4, 16)
ATTN_UNROLL = 16
CONV_TAPS = 4
LRU_C = 8.0
EPS = 1e-6
N_SLOTS = 7
N_QKV = 3
VMEM_MIB = 2 ** 20
VMEM_V7X = 64 * VMEM_MIB
STREAM_TILE = 4 * VMEM_MIB
VMEM_STREAM_MIB = 32
VMEM_TILES_MIB = 56

ADAM_LR = 0.001
ADAM_B1 = 0.9
ADAM_B2 = 0.999
ADAM_EPS = 1e-08
ADAM_WD = 0.01
ADAM_STEP = 10

NN = (((1,), (0,)), ((), ()))
NT = (((1,), (1,)), ((), ()))
TN = (((0,), (0,)), ((), ()))


def _params(vmem_mib=None, **kw):
    limit = None if vmem_mib is None else min(vmem_mib * VMEM_MIB, VMEM_V7X - 8 * VMEM_MIB)
    return pltpu.CompilerParams(vmem_limit_bytes=limit, **kw)


def _row_tile(rows, row_bytes, budget=VMEM_MIB):
    t = rows
    while t % 16 == 0 and t * row_bytes > budget:
        t //= 2
    return t


def _dot(a, b, dims):
    return lax.dot_general(a.astype(BF16), b.astype(BF16), dims, preferred_element_type=F32)


def _sigmoid(x):
    return jax.nn.sigmoid(x)


def _rms_fwd(x, g, name):
    s, d = x.shape
    tm = _row_tile(s, d * 4)

    def body(x_ref, g_ref, o_ref):
        xf = x_ref[...]
        r = lax.rsqrt(jnp.mean(xf * xf, axis=-1, keepdims=True) + EPS)
        o_ref[...] = (xf * r * g_ref[...]).astype(o_ref.dtype)

    return pl.pallas_call(
        body, name=name, grid=(s // tm,),
        in_specs=[pl.BlockSpec((tm, d), lambda i: (i, 0)), pl.BlockSpec((1, d), lambda i: (0, 0))],
        out_specs=pl.BlockSpec((tm, d), lambda i: (i, 0)),
        out_shape=jax.ShapeDtypeStruct((s, d), BF16), compiler_params=_params(VMEM_STREAM_MIB),
    )(x, g)


def _rms_bwd(x, g, dy, resid, name):
    s, d = x.shape
    tm = _row_tile(s, d * 4)

    def body(x_ref, g_ref, dy_ref, res_ref, dx_ref, dxb_ref, dg_ref):
        xf = x_ref[...]
        r = lax.rsqrt(jnp.mean(xf * xf, axis=-1, keepdims=True) + EPS)
        xh = xf * r
        dyv = dy_ref[...]
        dxh = dyv * g_ref[...]
        dx = r * (dxh - xh * jnp.mean(dxh * xh, axis=-1, keepdims=True)) + res_ref[...]
        dx_ref[...] = dx
        dxb_ref[...] = dx.astype(BF16)
        part = jnp.sum(dyv * xh, axis=0, keepdims=True)

        @pl.when(pl.program_id(0) == 0)
        def _():
            dg_ref[...] = part

        @pl.when(pl.program_id(0) > 0)
        def _():
            dg_ref[...] += part

    row = pl.BlockSpec((tm, d), lambda i: (i, 0))
    vec = pl.BlockSpec((1, d), lambda i: (0, 0))
    return pl.pallas_call(
        body, name=name, grid=(s // tm,),
        in_specs=[row, vec, row, row], out_specs=[row, row, vec],
        out_shape=[jax.ShapeDtypeStruct((s, d), F32), jax.ShapeDtypeStruct((s, d), BF16),
                   jax.ShapeDtypeStruct((1, d), F32)],
        compiler_params=_params(VMEM_STREAM_MIB),
    )(x, g, dy, resid)


def _loss_head(h2, g, target):
    s, d = h2.shape
    tm = _row_tile(s, d * 4)

    def body(x_ref, g_ref, t_ref, loss_ref, dx_ref, dxb_ref, dg_ref):
        xf = x_ref[...]
        gv = g_ref[...]
        r = lax.rsqrt(jnp.mean(xf * xf, axis=-1, keepdims=True) + EPS)
        xh = xf * r
        err = xh * gv - t_ref[...]
        part = jnp.sum(jnp.sum(err * err, axis=1, keepdims=True), axis=0, keepdims=True) * (0.5 / d)
        dyv = err * (1.0 / d)
        dxh = dyv * gv
        dx = r * (dxh - xh * jnp.mean(dxh * xh, axis=-1, keepdims=True))
        dx_ref[...] = dx
        dxb_ref[...] = dx.astype(BF16)
        dgp = jnp.sum(dyv * xh, axis=0, keepdims=True)

        @pl.when(pl.program_id(0) == 0)
        def _():
            dg_ref[...] = dgp
            loss_ref[...] = jnp.broadcast_to(part, loss_ref.shape)

        @pl.when(pl.program_id(0) > 0)
        def _():
            dg_ref[...] += dgp
            loss_ref[...] += jnp.broadcast_to(part, loss_ref.shape)

    row = pl.BlockSpec((tm, d), lambda i: (i, 0))
    vec = pl.BlockSpec((1, d), lambda i: (0, 0))
    return pl.pallas_call(
        body, name="loss_head", grid=(s // tm,),
        in_specs=[row, vec, row],
        out_specs=[pl.BlockSpec((8, 128), lambda i: (0, 0)), row, row, vec],
        out_shape=[jax.ShapeDtypeStruct((8, 128), F32), jax.ShapeDtypeStruct((s, d), F32),
                   jax.ShapeDtypeStruct((s, d), BF16), jax.ShapeDtypeStruct((1, d), F32)],
        compiler_params=_params(VMEM_STREAM_MIB),
    )(h2, g, target)


def _mm(name, operands, in_specs, out_shape, out_specs, grid, dims, epilogue, nk=1, acc_shape=None,
        aliases=None):
    n_in = len(operands)
    n_out = len(out_shape)

    def body(*refs):
        a_ref, b_ref = refs[0], refs[1]
        extras = refs[2:n_in]
        outs = refs[n_in:n_in + n_out]

        def prod():
            return _dot(a_ref[...], b_ref[...], dims)

        if nk == 1:
            epilogue(prod(), extras, outs)
        else:
            acc = refs[n_in + n_out]
            k = pl.program_id(2)

            @pl.when(k == 0)
            def _():
                acc[...] = prod()

            @pl.when(k > 0)
            def _():
                acc[...] += prod()

            @pl.when(k == nk - 1)
            def _():
                epilogue(acc[...], extras, outs)

    scratch = [] if nk == 1 else [pltpu.VMEM(acc_shape, F32)]
    return pl.pallas_call(
        body, name=name, grid=grid, in_specs=in_specs, out_specs=out_specs, out_shape=out_shape,
        scratch_shapes=scratch, input_output_aliases=aliases or {},
        compiler_params=_params(VMEM_TILES_MIB),
    )(*operands)


def _store(acc, extras, outs):
    outs[0][...] = acc.astype(outs[0].dtype)


def _proj_in_shard(xn, w_in_g, shard_arr, prev, name):
    s, d = xn.shape
    u = d // 4
    per = N_SLOTS
    n_prev = 0 if prev is None else 1

    def body(sh_ref, x_ref, w_ref, *rest):
        del sh_ref
        rest[n_prev][...] = _dot(x_ref[...], w_ref[...], NN)

    def out_map(j, sh_ref):
        unit = per * sh_ref[0] + j
        return (unit // 4, 0, unit % 4)

    return pl.pallas_call(
        body, name=name,
        grid_spec=pltpu.PrefetchScalarGridSpec(
            num_scalar_prefetch=1, grid=(per,),
            in_specs=[pl.BlockSpec((s, d), lambda j, sh_ref: (0, 0)),
                      pl.BlockSpec((None, d, u), lambda j, sh_ref: (sh_ref[0], 0, j))] + [HBM] * n_prev,
            out_specs=pl.BlockSpec((None, s, u), out_map)),
        out_shape=jax.ShapeDtypeStruct((N_SLOTS, s, d), F32),
        input_output_aliases={3: 0} if n_prev else {},
        compiler_params=_params(VMEM_TILES_MIB),
    )(shard_arr, xn, w_in_g, *([] if prev is None else [prev]))


def _mm_nn(name, a, b, extras, extra_specs, out_shape, out_specs, epilogue, tn, aliases=None):
    s, kdim = a.shape
    n = b.shape[1]
    return _mm(
        name, [a, b] + list(extras),
        [pl.BlockSpec((s, kdim), lambda i, j, k: (0, 0)), pl.BlockSpec((kdim, tn), lambda i, j, k: (0, j))]
        + list(extra_specs),
        out_shape, out_specs, (1, n // tn, 1), NN, epilogue, aliases=aliases)


def _mm_nt(name, a, b, extras, extra_specs, out_shape, out_specs, epilogue, tn, aliases=None):
    s, kdim = a.shape
    n = b.shape[0]
    return _mm(
        name, [a, b] + list(extras),
        [pl.BlockSpec((s, kdim), lambda i, j, k: (0, 0)), pl.BlockSpec((tn, kdim), lambda i, j, k: (j, 0))]
        + list(extra_specs),
        out_shape, out_specs, (1, n // tn, 1), NT, epilogue, aliases=aliases)


def _mm_tn(name, a, b, a_spec, b_spec, out_shape, out_spec, grid, m, tn, s, aliases=None, extra=None):
    ch = ROW_CHUNK
    n_in = 2 if extra is None else 3

    def body(*refs):
        a_ref, b_ref = refs[0], refs[1]
        o_ref, at_ref = refs[n_in], refs[n_in + 1]

        @pl.when(pl.program_id(1) == 0)
        def _():
            for c0 in range(0, s, ch):
                at_ref[:, c0:c0 + ch] = a_ref[c0:c0 + ch, :].astype(F32).T.astype(BF16)

        o_ref[...] = _dot(at_ref[...], b_ref[...], NN).astype(o_ref.dtype)

    operands = [a, b] + ([] if extra is None else [extra])
    in_specs = [a_spec, b_spec] + ([] if extra is None else [pl.BlockSpec(memory_space=pl.ANY)])
    return pl.pallas_call(
        body, name=name, grid=grid, in_specs=in_specs, out_specs=out_spec, out_shape=out_shape,
        scratch_shapes=[pltpu.VMEM((m, s), BF16)], input_output_aliases=aliases or {},
        compiler_params=_params(VMEM_TILES_MIB),
    )(*operands)


def _dxn(dproj_a, dproj_b, w_in_g, tn, after):
    n_a, s, d = dproj_a.shape
    u = d // 4
    ua = 4 * n_a
    nk = 4 * N_SLOTS
    per = N_SLOTS

    def body(a_ref, b_ref, w_ref, *rest):
        o_ref = rest[len(after)]
        k = pl.program_id(2)

        @pl.when(k == 0)
        def _():
            o_ref[...] = jnp.zeros_like(o_ref)

        @pl.when(k < ua)
        def _():
            o_ref[...] += _dot(a_ref[...], w_ref[...], NT)

        @pl.when(k >= ua)
        def _():
            o_ref[...] += _dot(b_ref[...], w_ref[...], NT)

    def a_map(i, j, k):
        kk = jnp.minimum(k, ua - 1)
        return (kk // 4, 0, kk % 4)

    def b_map(i, j, k):
        kk = jnp.maximum(k - ua, 0)
        return (kk // 4, 0, kk % 4)

    return pl.pallas_call(
        body, name="dxn", grid=(1, d // tn, nk),
        in_specs=[pl.BlockSpec((None, s, u), a_map), pl.BlockSpec((None, s, u), b_map),
                  pl.BlockSpec((None, tn, u), lambda i, j, k: (k // per, j, k % per))] + [HBM] * len(after),
        out_specs=pl.BlockSpec((s, tn), lambda i, j, k: (0, j)),
        out_shape=jax.ShapeDtypeStruct((s, d), F32),
        compiler_params=_params(VMEM_TILES_MIB),
    )(dproj_a, dproj_b, w_in_g, *after)


def _attn_masks(slope, dil):
    ii = lax.broadcasted_iota(jnp.int32, (ATTN_BLK, 2 * ATTN_BLK), 0)
    jj = lax.broadcasted_iota(jnp.int32, (ATTN_BLK, 2 * ATTN_BLK), 1)
    diff = ATTN_BLK + ii - jj
    band = (diff >= 0) & (diff <= ATTN_BLK)
    bias = -(slope * float(dil)) * diff.astype(F32)
    return band, bias, jj


def _attn_window(t, nblk):
    cur = pl.ds(pl.multiple_of(t * ATTN_BLK, ATTN_BLK), ATTN_BLK)
    prev = pl.ds(pl.multiple_of(jnp.maximum(t - 1, 0) * ATTN_BLK, ATTN_BLK), ATTN_BLK)
    first = jnp.where(t % nblk == 0, ATTN_BLK, 0)
    return prev, cur, first


def _unrolled_loop(n, step, init, unroll=ATTN_UNROLL):
    def trip(i, carry):
        for k in range(unroll):
            carry = step(i * unroll + k, carry)
        return carry

    return lax.fori_loop(0, n // unroll, trip, init)


def _streams(pairs, dil, s):
    if dil == 1:
        return [src for _, src in pairs]
    seg = s // dil
    for dst, src in pairs:
        for r in range(dil):
            dst[r * seg:(r + 1) * seg, :] = src[pl.ds(r, seg, stride=dil), :].astype(dst.dtype)
    return [dst for dst, _ in pairs]


def _attn_fwd(proj_a, slopes):
    _, s, d = proj_a.shape
    heads = d // HEAD_DIM
    scale = HEAD_DIM ** -0.5
    n_t = s // ATTN_BLK
    ng = len(DILATIONS)

    def body(q_ref, k_ref, v_ref, sl_ref, o_ref, lse_ref, qd, kd, vd, od, ld, og, lg):
        slope = sl_ref[...][:, :1]
        for g, dil in enumerate(DILATIONS):
            nblk = s // dil // ATTN_BLK
            qs, ks, vs = _streams([(qd, q_ref), (kd, k_ref), (vd, v_ref)], dil, s)
            o_t, l_t = (og.at[g], lg.at[g]) if dil == 1 else (od, ld)
            band, bias, jj = _attn_masks(slope, dil)

            def blk(t, carry, nblk=nblk, band=band, bias=bias, jj=jj, qs=qs, ks=ks, vs=vs, o_t=o_t, l_t=l_t):
                prev, cur, first = _attn_window(t, nblk)
                kk = jnp.concatenate([ks[prev, :], ks[cur, :]], axis=0)
                vv = jnp.concatenate([vs[prev, :], vs[cur, :]], axis=0)
                sc = _dot(qs[cur, :], kk, NT) * scale + bias
                sc = jnp.where(band & (jj >= first), sc, -jnp.inf)
                m = jnp.max(sc, axis=1, keepdims=True)
                p = jnp.exp(sc - m)
                l = jnp.sum(p, axis=1, keepdims=True)
                o_t[cur, :] = _dot(p, vv, NN) / l
                l_t[cur, :] = jnp.broadcast_to(m + jnp.log(l), (ATTN_BLK, HEAD_DIM))
                return carry

            _unrolled_loop(n_t, blk, 0)
            seg = s // dil
            if dil > 1:
                for r in range(dil):
                    og[g, pl.ds(r, seg, stride=dil), :] = od[r * seg:(r + 1) * seg, :]
                    lg[g, pl.ds(r, seg, stride=dil), :] = ld[r * seg:(r + 1) * seg, :]

        ch = ROW_CHUNK

        def combine(c, carry):
            rows = pl.ds(pl.multiple_of(c * ch, ch), ch)
            ls = [lg[g, rows, :] for g in range(ng)]
            mx = functools.reduce(jnp.maximum, ls)
            es = [jnp.exp(x - mx) for x in ls]
            den = functools.reduce(jnp.add, es)
            num = functools.reduce(jnp.add, [es[g] * og[g, rows, :] for g in range(ng)])
            o_ref[rows, :] = (num / den).astype(o_ref.dtype)
            lse_ref[rows, :] = mx + jnp.log(den)
            return carry

        lax.fori_loop(0, s // ch, combine, 0)

    def col(slot):
        return pl.BlockSpec((None, s, HEAD_DIM), lambda h: (slot, 0, h))

    head = pl.BlockSpec((s, HEAD_DIM), lambda h: (0, h))
    return pl.pallas_call(
        body, name="attn_fwd", grid=(heads,),
        in_specs=[col(0), col(1), col(2), pl.BlockSpec((None, 1, HEAD_DIM), lambda h: (h, 0, 0))],
        out_specs=[head, head],
        out_shape=[jax.ShapeDtypeStruct((s, d), BF16), jax.ShapeDtypeStruct((s, d), F32)],
        scratch_shapes=[pltpu.VMEM((s, HEAD_DIM), BF16)] * 3 + [pltpu.VMEM((s, HEAD_DIM), F32)] * 2
        + [pltpu.VMEM((ng, s, HEAD_DIM), F32)] * 2,
        compiler_params=_params(VMEM_TILES_MIB),
    )(proj_a, proj_a, proj_a, slopes)


def _attn_bwd(proj_a, slopes, y_attn, lse, dy):
    _, s, d = proj_a.shape
    heads = d // HEAD_DIM
    scale = HEAD_DIM ** -0.5
    n_t = s // ATTN_BLK

    def body(q_ref, k_ref, v_ref, sl_ref, o_ref, lse_ref, dy_ref, out_ref,
             qd, kd, vd, dod, lsd, dld, delta, dqd, dkd, dvd, dqa, dka, dva):
        slope = sl_ref[...][:, :1]
        dyv = dy_ref[...]
        delta[...] = jnp.broadcast_to(
            jnp.sum(dyv * o_ref[...].astype(F32), axis=1, keepdims=True), (s, HEAD_DIM))
        for g, dil in enumerate(DILATIONS):
            nblk = s // dil // ATTN_BLK
            seg = s // dil
            qs, ks, vs, dos, lss, dls = _streams(
                [(qd, q_ref), (kd, k_ref), (vd, v_ref), (dod, dy_ref), (lsd, lse_ref), (dld, delta)], dil, s)
            dq_t, dk_t, dv_t = (dqa, dka, dva) if dil == 1 else (dqd, dkd, dvd)
            band, bias, jj = _attn_masks(slope, dil)

            def blk(t, carry, nblk=nblk, band=band, bias=bias, jj=jj, qs=qs, ks=ks, vs=vs, dos=dos, lss=lss,
                    dls=dls, dq_t=dq_t, dk_t=dk_t, dv_t=dv_t):
                ck, cv = carry
                prev, cur, first = _attn_window(t, nblk)
                q = qs[cur, :]
                do = dos[cur, :]
                lse_b = lss[cur, :]
                dl_b = dls[cur, :]
                kk = jnp.concatenate([ks[prev, :], ks[cur, :]], axis=0)
                vv = jnp.concatenate([vs[prev, :], vs[cur, :]], axis=0)
                sc = _dot(q, kk, NT) * scale + bias
                p = jnp.where(band & (jj >= first), jnp.exp(sc - jnp.concatenate([lse_b, lse_b], axis=1)), 0.0)
                dp = _dot(do, vv, NT)
                ds = p * (dp - jnp.concatenate([dl_b, dl_b], axis=1))
                dv_b = _dot(p, do, TN)
                dk_b = _dot(ds, q, TN) * scale
                dq_t[cur, :] = _dot(ds, kk, NN) * scale
                done = pl.ds(pl.multiple_of(jnp.where(t == 0, n_t, t - 1) * ATTN_BLK, ATTN_BLK), ATTN_BLK)
                dk_t[done, :] = ck + dk_b[:ATTN_BLK]
                dv_t[done, :] = cv + dv_b[:ATTN_BLK]
                return dk_b[ATTN_BLK:], dv_b[ATTN_BLK:]

            zero = jnp.zeros((ATTN_BLK, HEAD_DIM), F32)
            ck, cv = _unrolled_loop(n_t, blk, (zero, zero))
            dk_t[(n_t - 1) * ATTN_BLK:n_t * ATTN_BLK, :] = ck
            dv_t[(n_t - 1) * ATTN_BLK:n_t * ATTN_BLK, :] = cv
            if dil > 1:
                for acc, part in ((dqa, dqd), (dka, dkd), (dva, dvd)):
                    for r in range(dil):
                        acc[pl.ds(r, seg, stride=dil), :] += part[r * seg:(r + 1) * seg, :]
        out_ref[0] = dqa[...].astype(out_ref.dtype)
        out_ref[1] = dka[0:s, :].astype(out_ref.dtype)
        out_ref[2] = dva[0:s, :].astype(out_ref.dtype)

    def col(slot):
        return pl.BlockSpec((None, s, HEAD_DIM), lambda h: (slot, 0, h))

    head = pl.BlockSpec((s, HEAD_DIM), lambda h: (0, h))
    return pl.pallas_call(
        body, name="attn_bwd", grid=(heads,),
        in_specs=[col(0), col(1), col(2), pl.BlockSpec((None, 1, HEAD_DIM), lambda h: (h, 0, 0)),
                  head, head, head],
        out_specs=pl.BlockSpec((N_QKV, s, HEAD_DIM), lambda h: (0, 0, h)),
        out_shape=jax.ShapeDtypeStruct((N_QKV, s, d), BF16),
        scratch_shapes=[pltpu.VMEM((s, HEAD_DIM), BF16)] * 4 + [pltpu.VMEM((s, HEAD_DIM), F32)] * 4
        + [pltpu.VMEM((s + ATTN_BLK, HEAD_DIM), F32)] * 2 + [pltpu.VMEM((s, HEAD_DIM), F32)]
        + [pltpu.VMEM((s + ATTN_BLK, HEAD_DIM), F32)] * 2,
        compiler_params=_params(VMEM_TILES_MIB),
    )(proj_a, proj_a, proj_a, slopes, y_attn, lse, dy)


def _expm1(x):
    small = x * (1.0 + x * (0.5 + x * (1.0 / 6.0 + x * (1.0 / 24.0 + x * (1.0 / 120.0)))))
    return jnp.where(jnp.abs(x) < 0.1, small, jnp.exp(x) - 1.0)


def _softplus(x):
    return jnp.maximum(x, 0.0) + jnp.log1p(jnp.exp(-jnp.abs(x)))


GELU_K = 0.7978845608028654
GELU_C = 0.044715


def _gelu(x):
    t = jnp.tanh(GELU_K * (x + GELU_C * x * x * x))
    return 0.5 * x * (1.0 + t), t


def _gelu_grad(x, t):
    return 0.5 * (1.0 + t) + 0.5 * x * (1.0 - t * t) * GELU_K * (1.0 + 3.0 * GELU_C * x * x)


def _lru_gates(xc, wa, ba, wx, bx, sp):
    r = _sigmoid(_dot(xc, wa, NN) + ba)
    ig = _sigmoid(_dot(xc, wx, NN) + bx)
    log_a = -LRU_C * r * sp
    a = jnp.exp(log_a)
    mult = jnp.sqrt(-_expm1(2.0 * log_a))
    return r, ig, a, mult


def _scan_fwd(a, u, tt):
    row = lax.broadcasted_iota(jnp.int32, a.shape, 0)
    sh = 1
    while sh < tt:
        keep = row >= sh
        a_s = jnp.where(keep, pltpu.roll(a, sh, 0), 1.0)
        u_s = jnp.where(keep, pltpu.roll(u, sh, 0), 0.0)
        u = a * u_s + u
        a = a * a_s
        sh *= 2
    return a, u


def _scan_bwd(b, g, tt):
    row = lax.broadcasted_iota(jnp.int32, b.shape, 0)
    sh = 1
    while sh < tt:
        keep = row < tt - sh
        b_s = jnp.where(keep, pltpu.roll(b, tt - sh, 0), 1.0)
        g_s = jnp.where(keep, pltpu.roll(g, tt - sh, 0), 0.0)
        g = g + b * g_s
        b = b * b_s
        sh *= 2
    return b, g


ROW_CHUNK = 256


def _pad_copy(xpad_ref, x_ref, s):
    xpad_ref[0:8, :] = jnp.zeros((8, HEAD_DIM), F32)
    for c0 in range(0, s, ROW_CHUNK):
        xpad_ref[8 + c0:8 + c0 + ROW_CHUNK, :] = x_ref[c0:c0 + ROW_CHUNK, :]


def _conv_rows(dst_ref, xpad_ref, cw, cb, s):
    for c0 in range(0, s, ROW_CHUNK):
        acc = cb
        for j in range(CONV_TAPS):
            off = 8 - (CONV_TAPS - 1) + j + c0
            acc = acc + cw[j:j + 1, :] * xpad_ref[off:off + ROW_CHUNK, :]
        dst_ref[c0:c0 + ROW_CHUNK, :] = acc


LRU_TILE = 128
LRU_UNROLL = 16


def _lru_specs(s, d):
    heads = d // HEAD_DIM

    def col(slot):
        return pl.BlockSpec((None, s, HEAD_DIM), lambda h: (slot, 0, h))

    vec = pl.BlockSpec((1, HEAD_DIM), lambda h: (0, h))
    mat = pl.BlockSpec((None, HEAD_DIM, HEAD_DIM), lambda h: (h, 0, 0))
    cw = pl.BlockSpec((8, HEAD_DIM), lambda h: (0, h))
    head = pl.BlockSpec((s, HEAD_DIM), lambda h: (0, h))
    return heads, col, vec, mat, cw, head


def _lru_fwd(proj_b, conv_w, conv_b, wa, ba, wx, bx, lam):
    _, s, d = proj_b.shape
    heads, col, vec, mat, cws, head = _lru_specs(s, d)
    tt = LRU_TILE

    def body(xr_ref, xg_ref, cw_ref, cb_ref, wa_ref, ba_ref, wx_ref, bx_ref, lam_ref, y_ref, h_ref, xpad, xc_s):
        _pad_copy(xpad, xr_ref, s)
        _conv_rows(xc_s, xpad, cw_ref[...], cb_ref[...], s)
        sp = _softplus(-lam_ref[...])
        wav, wxv, bav, bxv = wa_ref[...], wx_ref[...], ba_ref[...], bx_ref[...]

        def tile(i, hc):
            rows = pl.ds(pl.multiple_of(i * tt, tt), tt)
            xc = xc_s[rows, :]
            _, ig, a, mult = _lru_gates(xc, wav, bav, wxv, bxv, sp)
            pa, hl = _scan_fwd(a, mult * (ig * xc), tt)
            h = hl + pa * hc
            h_ref[rows, :] = h
            gel, _ = _gelu(xg_ref[rows, :])
            y_ref[rows, :] = (h * gel).astype(y_ref.dtype)
            return h[tt - 1:tt, :]

        _unrolled_loop(s // tt, tile, jnp.zeros((1, HEAD_DIM), F32), LRU_UNROLL)

    return pl.pallas_call(
        body, name="lru_fwd", grid=(heads,),
        in_specs=[col(N_QKV), col(N_QKV + 1), cws, vec, mat, vec, mat, vec, vec],
        out_specs=[head, head],
        out_shape=[jax.ShapeDtypeStruct((s, d), BF16), jax.ShapeDtypeStruct((s, d), F32)],
        scratch_shapes=[pltpu.VMEM((s + 8, HEAD_DIM), F32), pltpu.VMEM((s, HEAD_DIM), F32)],
        compiler_params=_params(VMEM_STREAM_MIB),
    )(proj_b, proj_b, conv_w, conv_b, wa, ba, wx, bx, lam)


def _lru_bwd(proj_b, h_lru, dy, conv_w, conv_b, wa, ba, wx, bx, lam, dproj_b):
    _, s, d = proj_b.shape
    heads, col, vec, mat, cws, head = _lru_specs(s, d)
    tt = LRU_TILE
    n_t = s // tt

    def body(xr_ref, xg_ref, h_ref, dy_ref, cw_ref, cb_ref, wa_ref, ba_ref, wx_ref, bx_ref, lam_ref, alias_ref,
             out_ref, dcw_ref, dcb_ref, dwa_ref, dba_ref, dwx_ref, dbx_ref, dlam_ref, xpad, xc_s, dxc_s):
        del alias_ref
        _pad_copy(xpad, xr_ref, s)
        cwv = cw_ref[...]
        _conv_rows(xc_s, xpad, cwv, cb_ref[...], s)
        dxc_s[s:s + 8, :] = jnp.zeros((8, HEAD_DIM), F32)
        lamv = lam_ref[...]
        sp = _softplus(-lamv)
        wav, wxv, bav, bxv = wa_ref[...], wx_ref[...], ba_ref[...], bx_ref[...]
        dwa_ref[...] = jnp.zeros_like(dwa_ref)
        dwx_ref[...] = jnp.zeros_like(dwx_ref)
        zero = jnp.zeros((1, HEAD_DIM), F32)
        row = lax.broadcasted_iota(jnp.int32, (tt, HEAD_DIM), 0)

        def tile(it, carry):
            dh_next, a_next, dba, dbx, dsp, dcb = carry
            i = n_t - 1 - it
            t0 = pl.multiple_of(i * tt, tt)
            rows = pl.ds(t0, tt)
            xc = xc_s[rows, :]
            r, ig, a, mult = _lru_gates(xc, wav, bav, wxv, bxv, sp)
            h = h_ref[rows, :]
            before = h_ref[pl.ds(pl.multiple_of(jnp.maximum(t0 - 8, 0), 8), 8), :][7:8, :]
            before = before * (i > 0).astype(F32)
            h_prev = jnp.where(row == 0, before, pltpu.roll(h, 1, 0))
            xg = xg_ref[rows, :]
            dyv = dy_ref[rows, :]
            gel, th = _gelu(xg)
            out_ref[1, rows, :] = (dyv * h * _gelu_grad(xg, th)).astype(out_ref.dtype)
            b = jnp.where(row == tt - 1, a_next, pltpu.roll(a, tt - 1, 0))
            pb, z = _scan_bwd(b, dyv * gel, tt)
            dh = z + pb * dh_next
            da = dh * h_prev
            dmult = dh * (ig * xc)
            dig = dh * (mult * xc)
            dla = da * a - dmult * (a * a / mult)
            dzr = dla * (-LRU_C * sp) * (r * (1.0 - r))
            dzx = dig * (ig * (1.0 - ig))
            dxc = dh * (mult * ig) + _dot(dzr, wav, NT) + _dot(dzx, wxv, NT)
            dxc_s[rows, :] = dxc
            dwa_ref[...] += _dot(xc, dzr, TN)
            dwx_ref[...] += _dot(xc, dzx, TN)
            return (dh[0:1, :], a[0:1, :],
                    dba + jnp.sum(dzr, axis=0, keepdims=True),
                    dbx + jnp.sum(dzx, axis=0, keepdims=True),
                    dsp + jnp.sum(dla * (-LRU_C * r), axis=0, keepdims=True),
                    dcb + jnp.sum(dxc, axis=0, keepdims=True))

        _, _, dba, dbx, dsp, dcb = _unrolled_loop(n_t, tile, (zero, zero, zero, zero, zero, zero), LRU_UNROLL)
        dba_ref[...] = dba
        dbx_ref[...] = dbx
        dcb_ref[...] = dcb
        dlam_ref[...] = -dsp * _sigmoid(-lamv)
        dcw = [zero] * CONV_TAPS
        for c0 in range(0, s, ROW_CHUNK):
            dxc_c = dxc_s[c0:c0 + ROW_CHUNK, :]
            dxr = jnp.zeros((ROW_CHUNK, HEAD_DIM), F32)
            for j in range(CONV_TAPS):
                back = CONV_TAPS - 1 - j
                off = 8 - back + c0
                dcw[j] = dcw[j] + jnp.sum(dxc_c * xpad[off:off + ROW_CHUNK, :], axis=0, keepdims=True)
                dxr = dxr + cwv[j:j + 1, :] * dxc_s[back + c0:back + c0 + ROW_CHUNK, :]
            out_ref[0, c0:c0 + ROW_CHUNK, :] = dxr.astype(out_ref.dtype)
        for j in range(CONV_TAPS):
            dcw_ref[j:j + 1, :] = dcw[j]

    return pl.pallas_call(
        body, name="lru_bwd", grid=(heads,),
        in_specs=[col(N_QKV), col(N_QKV + 1), head, head, cws, vec, mat, vec, mat, vec, vec,
                  pl.BlockSpec(memory_space=pl.ANY)],
        out_specs=[pl.BlockSpec((2, s, HEAD_DIM), lambda h: (0, 0, h)),
                   pl.BlockSpec((CONV_TAPS, HEAD_DIM), lambda h: (0, h)), vec, mat, vec, mat, vec, vec],
        out_shape=[jax.ShapeDtypeStruct(dproj_b.shape, dproj_b.dtype),
                   jax.ShapeDtypeStruct((CONV_TAPS, d), F32), jax.ShapeDtypeStruct((1, d), F32),
                   jax.ShapeDtypeStruct(wa.shape, F32), jax.ShapeDtypeStruct((1, d), F32),
                   jax.ShapeDtypeStruct(wx.shape, F32), jax.ShapeDtypeStruct((1, d), F32),
                   jax.ShapeDtypeStruct((1, d), F32)],
        scratch_shapes=[pltpu.VMEM((s + 8, HEAD_DIM), F32), pltpu.VMEM((s, HEAD_DIM), F32),
                        pltpu.VMEM((s + 8, HEAD_DIM), F32)],
        input_output_aliases={11: 0},
        compiler_params=_params(VMEM_STREAM_MIB),
    )(proj_b, proj_b, h_lru, dy, conv_w, conv_b, wa, ba, wx, bx, lam, dproj_b)


def _place():
    x, y, c = (lax.axis_index(n) for n in AXES)
    return x, y, c


def _other_chips(x, y):
    return [(1 - x, y), (x, 1 - y), (1 - x, 1 - y)]


HBM = pl.BlockSpec(memory_space=pl.ANY)


def _cast_shard(w, chip_arr, name):
    r, cols = w.shape
    rh = r // 2
    tr = _row_tile(rh, cols * 4, STREAM_TILE)
    nt = rh // tr

    def body(chip_ref, w_ref, o_ref):
        del chip_ref
        o_ref[...] = w_ref[...].astype(BF16)

    return pl.pallas_call(
        body, name=name,
        grid_spec=pltpu.PrefetchScalarGridSpec(
            num_scalar_prefetch=1, grid=(2, nt),
            in_specs=[pl.BlockSpec((tr, cols), lambda h, i, chip_ref: (h * nt + i, 0))],
            out_specs=pl.BlockSpec((None, None, tr, cols), lambda h, i, chip_ref: (chip_ref[0], h, i, 0))),
        out_shape=pltpu.HBM((N_CHIPS, 2, rh, cols), BF16), compiler_params=_params(VMEM_STREAM_MIB),
    )(chip_arr, w)


HBM_SPEC = pl.BlockSpec(memory_space=pltpu.HBM)
SEM_SPEC = pl.BlockSpec(memory_space=pltpu.SEMAPHORE)
EFFECT = pltpu.SideEffectType.DATAFLOW_SIDE_EFFECTING
TOKEN = jax.ShapeDtypeStruct((8, 128), F32)
TOKEN_SPEC = pl.BlockSpec(memory_space=pltpu.VMEM)


def _in_hbm(arrays):
    return [pltpu.with_memory_space_constraint(a, pltpu.HBM) for a in arrays]


def _hbm_like(arrays):
    return [pltpu.HBM(a.shape, a.dtype) for a in arrays]


def _sems(n):
    return pltpu.SemaphoreType.DMA((n,))


def _remote(src, dst, send_sem, recv_sem, to):
    return pltpu.make_async_remote_copy(src_ref=src, dst_ref=dst, send_sem=send_sem, recv_sem=recv_sem,
                                        device_id=to, device_id_type=MESH)


ALL_FLIPS = (0, 1, 2)


def _gather_start(bufs, groups, after, name):
    n = len(bufs)
    ng = len(groups)

    def body(*refs):
        ins = refs[:n]
        sems = refs[n + len(after):n + len(after) + 2 * ng]
        x, y, c = _place()
        me = 2 * x + y
        chips = _other_chips(x, y)
        for g, (ws, flips) in enumerate(groups):
            for i, w in enumerate(ws):
                for jj, j in enumerate(flips):
                    k = len(flips) * i + jj
                    mine = ins[w].at[me, c]
                    _remote(mine, mine, sems[2 * g].at[k], sems[2 * g + 1].at[k], (*chips[j], c)).start()

    sem_shapes = []
    for ws, flips in groups:
        sem_shapes += [_sems(len(flips) * len(ws))] * 2
    res = pl.pallas_call(
        body, name=name, in_specs=[HBM_SPEC] * n + [HBM] * len(after),
        out_specs=[SEM_SPEC] * (2 * ng) + [HBM_SPEC] * n, out_shape=sem_shapes + _hbm_like(bufs),
        input_output_aliases={w: 2 * ng + w for w in range(n)},
        compiler_params=pltpu.CompilerParams(has_side_effects=EFFECT),
    )(*_in_hbm(bufs), *after)
    return [(res[2 * g], res[2 * g + 1]) for g in range(ng)], list(res[2 * ng:])


def _gather_forward(bufs, recv, after, name, flips=ALL_FLIPS):
    m = len(bufs)
    nf = len(flips)

    def body(*refs):
        ins, recv_in = refs[:m], refs[m]
        fsend, frecv = refs[m + 1 + len(after)], refs[m + 2 + len(after)]
        x, y, c = _place()
        chips = _other_chips(x, y)
        for jj, j in enumerate(flips):
            cx, cy = chips[j]
            for i in range(m):
                landed = ins[i].at[2 * cx + cy, c]
                k = nf * i + jj
                _remote(landed, landed, fsend.at[k], recv_in.at[k], (cx, cy, c)).wait_recv()
                _remote(landed, landed, fsend.at[k], frecv.at[k], (x, y, 1 - c)).start()

    res = pl.pallas_call(
        body, name=name, in_specs=[HBM_SPEC] * m + [SEM_SPEC] + [HBM] * len(after),
        out_specs=[SEM_SPEC, SEM_SPEC] + [HBM_SPEC] * m, out_shape=[_sems(nf * m), _sems(nf * m)] + _hbm_like(bufs),
        input_output_aliases={i: 2 + i for i in range(m)},
        compiler_params=pltpu.CompilerParams(has_side_effects=EFFECT),
    )(*bufs, recv, *after)
    return (res[0], res[1]), list(res[2:])


NEIGHBOURS = (0, 1)


def _relay_partner(x, y, c):
    return 2 * (x ^ (1 - c)) + (y ^ c), (x ^ c, y ^ (1 - c))


def _gather_forward_relay(bufs, recv, after, name):
    m = len(bufs)
    nf = len(NEIGHBOURS)

    def body(*refs):
        ins, recv_in = refs[:m], refs[m]
        fsend, frecv, rsend, rrecv = refs[m + 1 + len(after):m + 5 + len(after)]
        x, y, c = _place()
        chips = _other_chips(x, y)
        for jj, j in enumerate(NEIGHBOURS):
            cx, cy = chips[j]
            for i in range(m):
                landed = ins[i].at[2 * cx + cy, c]
                k = nf * i + jj
                _remote(landed, landed, fsend.at[k], recv_in.at[k], (cx, cy, c)).wait_recv()
        row, (px, py) = _relay_partner(x, y, c)
        for i in range(m):
            relayed = ins[i].at[row, c]
            _remote(relayed, relayed, rsend.at[i], rrecv.at[i], (px, py, c)).start()
        for jj, j in enumerate(NEIGHBOURS):
            cx, cy = chips[j]
            for i in range(m):
                landed = ins[i].at[2 * cx + cy, c]
                k = nf * i + jj
                _remote(landed, landed, fsend.at[k], frecv.at[k], (x, y, 1 - c)).start()

    res = pl.pallas_call(
        body, name=name, in_specs=[HBM_SPEC] * m + [SEM_SPEC] + [HBM] * len(after),
        out_specs=[SEM_SPEC] * 4 + [HBM_SPEC] * m,
        out_shape=[_sems(nf * m), _sems(nf * m), _sems(m), _sems(m)] + _hbm_like(bufs),
        input_output_aliases={i: 4 + i for i in range(m)},
        compiler_params=pltpu.CompilerParams(has_side_effects=EFFECT),
    )(*bufs, recv, *after)
    return tuple(res[:4]), list(res[4:])


def _gather_forward_diag(bufs, rrecv, after, name):
    m = len(bufs)

    def body(*refs):
        ins, rrecv_in = refs[:m], refs[m]
        dsend, drecv = refs[m + 1 + len(after)], refs[m + 2 + len(after)]
        x, y, c = _place()
        diag = (2 * x + y) ^ 3
        _, (px, py) = _relay_partner(x, y, c)
        for i in range(m):
            landed = ins[i].at[diag, c]
            _remote(landed, landed, dsend.at[i], rrecv_in.at[i], (px, py, c)).wait_recv()
            _remote(landed, landed, dsend.at[i], drecv.at[i], (x, y, 1 - c)).start()

    res = pl.pallas_call(
        body, name=name, in_specs=[HBM_SPEC] * m + [SEM_SPEC] + [HBM] * len(after),
        out_specs=[SEM_SPEC, SEM_SPEC] + [HBM_SPEC] * m, out_shape=[_sems(m), _sems(m)] + _hbm_like(bufs),
        input_output_aliases={i: 2 + i for i in range(m)},
        compiler_params=pltpu.CompilerParams(has_side_effects=EFFECT),
    )(*bufs, rrecv, *after)
    return (res[0], res[1]), list(res[2:])


def _gather_finish_diag(bufs, rsend, dsend, drecv, after, name):
    m = len(bufs)

    def body(*refs):
        ins = refs[:m]
        rsend_in, dsend_in, drecv_in = refs[m:m + 3]
        x, y, c = _place()
        diag = (2 * x + y) ^ 3
        row, (px, py) = _relay_partner(x, y, c)
        for i in range(m):
            relayed = ins[i].at[row, c]
            _remote(relayed, relayed, rsend_in.at[i], drecv_in.at[i], (px, py, c)).wait_send()
            landed = ins[i].at[diag, c]
            _remote(landed, landed, dsend_in.at[i], drecv_in.at[i], (x, y, 1 - c)).wait_send()
            theirs = ins[i].at[diag, 1 - c]
            _remote(theirs, theirs, dsend_in.at[i], drecv_in.at[i], (x, y, 1 - c)).wait_recv()

    return list(pl.pallas_call(
        body, name=name, in_specs=[HBM_SPEC] * m + [SEM_SPEC] * 3 + [HBM] * len(after),
        out_specs=[HBM_SPEC] * m, out_shape=_hbm_like(bufs),
        input_output_aliases={i: i for i in range(m)},
        compiler_params=pltpu.CompilerParams(has_side_effects=EFFECT),
    )(*bufs, rsend, dsend, drecv, *after))


def _gather_finish(bufs, send, fsend, frecv, after, name, flips=ALL_FLIPS):
    m = len(bufs)
    nf = len(flips)

    def body(*refs):
        ins = refs[:m]
        send_in, fsend_in, frecv_in = refs[m:m + 3]
        x, y, c = _place()
        me = 2 * x + y
        chips = _other_chips(x, y)
        for jj, j in enumerate(flips):
            cx, cy = chips[j]
            cj = 2 * cx + cy
            for i in range(m):
                k = nf * i + jj
                mine = ins[i].at[me, c]
                _remote(mine, mine, send_in.at[k], frecv_in.at[k], (cx, cy, c)).wait_send()
                landed = ins[i].at[cj, c]
                _remote(landed, landed, fsend_in.at[k], frecv_in.at[k], (x, y, 1 - c)).wait_send()
                theirs = ins[i].at[cj, 1 - c]
                _remote(theirs, theirs, fsend_in.at[k], frecv_in.at[k], (x, y, 1 - c)).wait_recv()

    return list(pl.pallas_call(
        body, name=name, in_specs=[HBM_SPEC] * m + [SEM_SPEC] * 3 + [HBM] * len(after),
        out_specs=[HBM_SPEC] * m, out_shape=_hbm_like(bufs),
        input_output_aliases={i: i for i in range(m)},
        compiler_params=pltpu.CompilerParams(has_side_effects=EFFECT),
    )(*bufs, send, fsend, frecv, *after))


def _pair_exchange(grads, name):
    n = len(grads)

    def body(*refs):
        ins, outs = refs[:n], refs[n:2 * n]
        send_sems, recv_sems = refs[2 * n:]
        x, y, c = _place()
        sibling = (x, y, 1 - c)
        cps = []
        for w in range(n):
            for j in range(N_CHIPS):
                cp = pltpu.make_async_remote_copy(
                    src_ref=ins[w].at[j, 1 - c], dst_ref=outs[w].at[j], send_sem=send_sems.at[N_CHIPS * w + j],
                    recv_sem=recv_sems.at[N_CHIPS * w + j], device_id=sibling, device_id_type=MESH)
                cp.start()
                cps.append(cp)
        for cp in cps:
            cp.wait()

    return pl.pallas_call(
        body, name=name, in_specs=[HBM] * n, out_specs=[HBM] * n,
        out_shape=[jax.ShapeDtypeStruct((N_CHIPS,) + a.shape[2:], a.dtype) for a in grads],
        scratch_shapes=[pltpu.SemaphoreType.DMA((N_CHIPS * n,)), pltpu.SemaphoreType.DMA((N_CHIPS * n,))],
    )(*grads)


def _pair_start(grads, name):
    n = len(grads)
    lands = [lax.empty((N_CHIPS,) + a.shape[2:], a.dtype) for a in grads]

    def body(*refs):
        ins, land_in = refs[:n], refs[n:2 * n]
        send, recv = refs[2 * n], refs[2 * n + 1]
        token = refs[4 * n + 2]
        x, y, c = _place()
        for w in range(n):
            for j in range(N_CHIPS):
                k = N_CHIPS * w + j
                _remote(ins[w].at[j, 1 - c], land_in[w].at[j], send.at[k], recv.at[k], (x, y, 1 - c)).start()
        token[...] = jnp.zeros_like(token)

    res = pl.pallas_call(
        body, name=name, in_specs=[HBM_SPEC] * (2 * n),
        out_specs=[SEM_SPEC, SEM_SPEC] + [HBM_SPEC] * (2 * n) + [TOKEN_SPEC],
        out_shape=[_sems(N_CHIPS * n), _sems(N_CHIPS * n)] + _hbm_like(grads) + _hbm_like(lands) + [TOKEN],
        input_output_aliases={i: 2 + i for i in range(2 * n)},
        compiler_params=pltpu.CompilerParams(has_side_effects=EFFECT),
    )(*_in_hbm(grads), *_in_hbm(lands))
    return (res[0], res[1]), list(res[2:2 + n]), list(res[2 + n:2 + 2 * n]), res[2 + 2 * n]


def _pair_wait(sems, grads, lands, after, name):
    n = len(grads)

    def body(*refs):
        ins, land_in = refs[:n], refs[n:2 * n]
        send_in, recv_in = refs[2 * n], refs[2 * n + 1]
        x, y, c = _place()
        for w in range(n):
            for j in range(N_CHIPS):
                k = N_CHIPS * w + j
                cp = _remote(ins[w].at[j, 1 - c], land_in[w].at[j], send_in.at[k], recv_in.at[k], (x, y, 1 - c))
                cp.wait_send()
                cp.wait_recv()

    res = pl.pallas_call(
        body, name=name, in_specs=[HBM_SPEC] * (2 * n) + [SEM_SPEC, SEM_SPEC] + [HBM] * len(after),
        out_specs=[HBM_SPEC] * (2 * n), out_shape=_hbm_like(grads) + _hbm_like(lands),
        input_output_aliases={i: i for i in range(2 * n)},
        compiler_params=pltpu.CompilerParams(has_side_effects=EFFECT),
    )(*grads, *lands, sems[0], sems[1], *after)
    return list(res[:n]), list(res[n:])


def _chip_start(sums, name, to_all=()):
    m = len(sums)
    lands = [lax.empty(((N_CHIPS,) if i in to_all else ()) + a.shape, a.dtype) for i, a in enumerate(sums)]

    def body(*refs):
        ins, land_in = refs[:m], refs[m:2 * m]
        send, recv = refs[2 * m], refs[2 * m + 1]
        token = refs[4 * m + 2]
        x, y, c = _place()
        me = 2 * x + y
        for i in sorted(range(m), key=lambda i: i not in to_all):
            for j, (cx, cy) in enumerate(_other_chips(x, y)):
                src = ins[i] if i in to_all else ins[i].at[2 * cx + cy]
                _remote(src, land_in[i].at[me], send.at[3 * i + j], recv.at[3 * i + j], (cx, cy, c)).start()
        token[...] = jnp.zeros_like(token)

    res = pl.pallas_call(
        body, name=name, in_specs=[HBM_SPEC] * (2 * m),
        out_specs=[SEM_SPEC, SEM_SPEC] + [HBM_SPEC] * (2 * m) + [TOKEN_SPEC],
        out_shape=[_sems(3 * m), _sems(3 * m)] + _hbm_like(sums) + _hbm_like(lands) + [TOKEN],
        input_output_aliases={i: 2 + i for i in range(2 * m)},
        compiler_params=pltpu.CompilerParams(has_side_effects=EFFECT),
    )(*_in_hbm(sums), *_in_hbm(lands))
    return (res[0], res[1]), list(res[2:2 + m]), list(res[2 + m:2 + 2 * m]), res[2 + 2 * m]


def _chip_wait(sems, sums, lands, after, name, to_all=()):
    m = len(sums)

    def body(*refs):
        ins, land_in = refs[:m], refs[m:2 * m]
        send_in, recv_in = refs[2 * m], refs[2 * m + 1]
        x, y, c = _place()
        for i in range(m):
            for j, (cx, cy) in enumerate(_other_chips(x, y)):
                cj = 2 * cx + cy
                src = ins[i] if i in to_all else ins[i].at[cj]
                cp = _remote(src, land_in[i].at[cj], send_in.at[3 * i + j], recv_in.at[3 * i + j], (cx, cy, c))
                cp.wait_send()
                cp.wait_recv()

    res = pl.pallas_call(
        body, name=name, in_specs=[HBM_SPEC] * (2 * m) + [SEM_SPEC, SEM_SPEC] + [HBM] * len(after),
        out_specs=[HBM_SPEC] * (2 * m), out_shape=_hbm_like(sums) + _hbm_like(lands),
        input_output_aliases={i: i for i in range(2 * m)},
        compiler_params=pltpu.CompilerParams(has_side_effects=EFFECT),
    )(*sums, *lands, sems[0], sems[1], *after)
    return list(res[:m]), list(res[m:])


def _half_parts(bufs):
    parts = []
    for w, a in enumerate(bufs):
        parts += [(w, None)] if a.ndim == 3 else [(w, j) for j in range(a.shape[0])]
    return parts


def _half_ref(refs, w, j, h):
    return refs[w].at[h] if j is None else refs[w].at[j, h]


def _half_start(bufs, name):
    n = len(bufs)
    parts = _half_parts(bufs)

    def body(*refs):
        ins = refs[:n]
        send, recv = refs[n], refs[n + 1]
        x, y, c = _place()
        for k, (w, j) in enumerate(parts):
            mine = _half_ref(ins, w, j, c)
            _remote(mine, mine, send.at[k], recv.at[k], (x, y, 1 - c)).start()

    res = pl.pallas_call(
        body, name=name, in_specs=[HBM_SPEC] * n, out_specs=[SEM_SPEC, SEM_SPEC] + [HBM_SPEC] * n,
        out_shape=[_sems(len(parts)), _sems(len(parts))] + _hbm_like(bufs),
        input_output_aliases={w: 2 + w for w in range(n)},
        compiler_params=pltpu.CompilerParams(has_side_effects=EFFECT),
    )(*_in_hbm(bufs))
    return (res[0], res[1]), list(res[2:])


def _half_wait(sems, bufs, after, name):
    n = len(bufs)
    parts = _half_parts(bufs)

    def body(*refs):
        ins = refs[:n]
        send_in, recv_in = refs[n], refs[n + 1]
        x, y, c = _place()
        for k, (w, j) in enumerate(parts):
            mine = _half_ref(ins, w, j, c)
            _remote(mine, mine, send_in.at[k], recv_in.at[k], (x, y, 1 - c)).wait_send()
            theirs = _half_ref(ins, w, j, 1 - c)
            _remote(theirs, theirs, send_in.at[k], recv_in.at[k], (x, y, 1 - c)).wait_recv()

    return list(pl.pallas_call(
        body, name=name, in_specs=[HBM_SPEC] * n + [SEM_SPEC, SEM_SPEC] + [HBM] * len(after),
        out_specs=[HBM_SPEC] * n, out_shape=_hbm_like(bufs),
        input_output_aliases={w: w for w in range(n)},
        compiler_params=pltpu.CompilerParams(has_side_effects=EFFECT),
    )(*bufs, sems[0], sems[1], *after))


def _all_gather8(block, name, after=()):
    def body(in_ref, *rest):
        out_ref, send_sems, recv_sems, local_sem = rest[len(after):]
        x, y, c = _place()
        me = 4 * x + 2 * y + c
        mine = pltpu.make_async_copy(in_ref, out_ref.at[me], local_sem)
        mine.start()
        flips = [(fx, fy, fc) for fx in (0, 1) for fy in (0, 1) for fc in (0, 1)][1:]
        cps = []
        for k, (fx, fy, fc) in enumerate(flips):
            cp = pltpu.make_async_remote_copy(
                src_ref=in_ref, dst_ref=out_ref.at[me], send_sem=send_sems.at[k], recv_sem=recv_sems.at[k],
                device_id=(x ^ fx, y ^ fy, c ^ fc), device_id_type=MESH)
            cp.start()
            cps.append(cp)
        for k, (fx, fy, fc) in enumerate(flips):
            px, py, pc = x ^ fx, y ^ fy, c ^ fc
            theirs = out_ref.at[4 * px + 2 * py + pc]
            pltpu.make_async_remote_copy(
                src_ref=theirs, dst_ref=theirs, send_sem=send_sems.at[k], recv_sem=recv_sems.at[k],
                device_id=(px, py, pc), device_id_type=MESH).wait_recv()
        for cp in cps:
            cp.wait_send()
        mine.wait()

    return pl.pallas_call(
        body, name=name, in_specs=[HBM] * (1 + len(after)), out_specs=HBM,
        out_shape=jax.ShapeDtypeStruct((N_DEV,) + block.shape, block.dtype),
        scratch_shapes=[pltpu.SemaphoreType.DMA((N_DEV - 1,)), pltpu.SemaphoreType.DMA((N_DEV - 1,)),
                        pltpu.SemaphoreType.DMA],
    )(block, *after)


def _pair_sum(grad, recv, c_arr, name):
    _, _, rh, cols = grad.shape
    tr = _row_tile(rh, cols * grad.dtype.itemsize, STREAM_TILE // 2)

    def body(c_ref, g_ref, r_ref, o_ref):
        del c_ref
        o_ref[...] = (g_ref[...].astype(F32) + r_ref[...].astype(F32)).astype(o_ref.dtype)

    spec = pl.BlockSpec((None, tr, cols), lambda j, i, c_ref: (j, i, 0))
    return pl.pallas_call(
        body, name=name,
        grid_spec=pltpu.PrefetchScalarGridSpec(
            num_scalar_prefetch=1, grid=(N_CHIPS, rh // tr),
            in_specs=[pl.BlockSpec((None, None, tr, cols), lambda j, i, c_ref: (j, c_ref[0], i, 0)), spec],
            out_specs=spec),
        out_shape=pltpu.HBM(recv.shape, grad.dtype), compiler_params=_params(VMEM_STREAM_MIB),
    )(c_arr, grad, recv)


def _sum_by_chip(chip, p_ref, own_ref, o_ref):
    o_ref[...] = jnp.zeros_like(o_ref)
    for k in range(N_CHIPS):
        @pl.when(chip == k)
        def _():
            o_ref[...] += own_ref[...].astype(F32)

        @pl.when(chip != k)
        def _(k=k):
            o_ref[...] += p_ref[k].astype(F32)


def _chip_sum_all(parts, own, place_arr, name):
    _, nj, rh, cols = parts.shape
    tr = _row_tile(rh, cols * 4, STREAM_TILE // 2)

    def body(place_ref, p_ref, own_ref, o_ref):
        _sum_by_chip(place_ref[0], p_ref, own_ref, o_ref)

    return pl.pallas_call(
        body, name=name,
        grid_spec=pltpu.PrefetchScalarGridSpec(
            num_scalar_prefetch=1, grid=(nj, rh // tr),
            in_specs=[pl.BlockSpec((N_CHIPS, None, tr, cols), lambda j, i, place_ref: (0, j, i, 0)),
                      pl.BlockSpec((None, tr, cols), lambda j, i, place_ref: (j, i, 0))],
            out_specs=pl.BlockSpec((None, None, tr, cols), lambda j, i, place_ref: (j, place_ref[1], i, 0))),
        out_shape=pltpu.HBM((nj, 2, rh, cols), F32), compiler_params=_params(VMEM_STREAM_MIB),
    )(place_arr, parts, own)


def _chip_sum(parts, own, place_arr, name):
    _, rh, cols = parts.shape
    tr = _row_tile(rh, cols * 4, STREAM_TILE // 2)

    def body(place_ref, p_ref, own_ref, o_ref):
        _sum_by_chip(place_ref[0], p_ref, own_ref, o_ref)

    return pl.pallas_call(
        body, name=name,
        grid_spec=pltpu.PrefetchScalarGridSpec(
            num_scalar_prefetch=1, grid=(rh // tr,),
            in_specs=[pl.BlockSpec((N_CHIPS, tr, cols), lambda i, place_ref: (0, i, 0)),
                      pl.BlockSpec((None, tr, cols), lambda i, place_ref: (place_ref[0], i, 0))],
            out_specs=pl.BlockSpec((None, tr, cols), lambda i, place_ref: (place_ref[1], i, 0))),
        out_shape=pltpu.HBM((2, rh, cols), F32), compiler_params=_params(VMEM_STREAM_MIB),
    )(place_arr, parts, own)


def _adamw_math(w, g, m, v):
    m = ADAM_B1 * m + (1.0 - ADAM_B1) * g
    v = ADAM_B2 * v + (1.0 - ADAM_B2) * (g * g)
    m_hat = m / (1.0 - ADAM_B1 ** ADAM_STEP)
    v_hat = v / (1.0 - ADAM_B2 ** ADAM_STEP)
    delta = -ADAM_LR * (m_hat / (jnp.sqrt(v_hat) + ADAM_EPS) + ADAM_WD * w)
    return delta, m, v


def _adamw(w, g, m, v, name):
    rows, cols = w.shape
    tr = _row_tile(rows, cols * 4)

    def body(w_ref, g_ref, m_ref, v_ref, go_ref, d_ref, nm_ref, nv_ref):
        gv = g_ref[...]
        go_ref[...] = gv
        d_ref[...], nm_ref[...], nv_ref[...] = _adamw_math(w_ref[...], gv, m_ref[...], v_ref[...])

    spec = pl.BlockSpec((tr, cols), lambda i: (i, 0))
    return pl.pallas_call(
        body, name=name, grid=(rows // tr,), in_specs=[spec] * 4, out_specs=[spec] * 4,
        out_shape=[jax.ShapeDtypeStruct(w.shape, F32)] * 4, compiler_params=_params(VMEM_STREAM_MIB),
    )(w, g, m, v)


def _sum8_adamw_row(parts, w, m, v, name):
    cols = parts.shape[2]

    def body(p_ref, w_ref, m_ref, v_ref, g_ref, d_ref, nm_ref, nv_ref):
        g = p_ref[0, 0:1, :]
        for k in range(1, N_DEV):
            g = g + p_ref[k, 0:1, :]
        g_ref[...] = g
        d_ref[...], nm_ref[...], nv_ref[...] = _adamw_math(w_ref[...], g, m_ref[...], v_ref[...])

    return pl.pallas_call(
        body, name=name, out_shape=[jax.ShapeDtypeStruct((1, cols), F32)] * 4, compiler_params=_params(VMEM_STREAM_MIB),
    )(parts, w, m, v)


def _pack_rows(pieces, rows, name):
    cols = pieces[0].shape[1]
    n = len(pieces)

    def body(*refs):
        o_ref = refs[n]
        o_ref[...] = jnp.zeros_like(o_ref)
        at = 0
        for p_ref in refs[:n]:
            r = p_ref.shape[0]
            o_ref[at:at + r, :] = p_ref[...]
            at += r

    return pl.pallas_call(
        body, name=name, out_shape=jax.ShapeDtypeStruct((rows, cols), F32), compiler_params=_params(VMEM_STREAM_MIB),
    )(*pieces)


def kernel(x, norm_mix_g, w_in, conv_w, conv_b, lru_wa, lru_ba, lru_wx, lru_bx, lru_lambda, w_proj_attn, w_proj_lru, w_out, norm_mlp_g, w_up, w_down, norm_final_g, loss_target, m_norm_mix_g, m_w_in, m_conv_w, m_conv_b, m_lru_wa, m_lru_ba, m_lru_wx, m_lru_bx, m_lru_lambda, m_w_proj_attn, m_w_proj_lru, m_w_out, m_norm_mlp_g, m_w_up, m_w_down, m_norm_final_g, v_norm_mix_g, v_w_in, v_conv_w, v_conv_b, v_lru_wa, v_lru_ba, v_lru_wx, v_lru_bx, v_lru_lambda, v_w_proj_attn, v_w_proj_lru, v_w_out, v_norm_mlp_g, v_w_up, v_w_down, v_norm_final_g):
    s, d = x.shape[1], x.shape[2]
    ff = w_up.shape[2] * N_CHIPS
    heads = d // HEAD_DIM
    u = d // 4
    assert s % (max(DILATIONS) * ATTN_BLK) == 0 and d % (4 * HEAD_DIM) == 0 and ff == 4 * d and DILATIONS[0] == 1
    xs, target = _in_hbm([x[0], loss_target[0]])
    gf = norm_final_g.reshape(1, d)
    wa, wx = lru_wa[0], lru_wx[0]
    core = lax.axis_index("c").astype(jnp.int32)
    chip = (2 * lax.axis_index("x") + lax.axis_index("y")).astype(jnp.int32)
    cidx = core.reshape(1)
    chip_arr = chip.reshape(1)
    place_arr = jnp.stack([chip, core])
    slopes = jnp.broadcast_to(
        (2.0 ** (-8.0 * jnp.arange(1, heads + 1, dtype=F32) / heads))[:, None, None], (heads, 1, HEAD_DIM))

    big = _in_hbm([w_in[0], w_proj_attn[0], w_proj_lru[0], w_out[0], w_up[0], w_down[0]])
    names = ["w_in", "w_proj_attn", "w_proj_lru", "w_out", "w_up", "w_down"]
    cw_pad = jnp.pad(conv_w[0], ((0, 8 - CONV_TAPS), (0, 0)))
    cw_all = _all_gather8(cw_pad, "gather_conv_w")
    conv_w_full = jnp.concatenate([cw_all[2 * j] for j in range(N_CHIPS)], axis=1)
    (sem_a,), buf_a = _gather_start([_cast_shard(big[0], chip_arr, "cast_w_in")], [([0], NEIGHBOURS)], [cw_all],
                                    "gather_start_w_in")

    xn = _rms_fwd(xs, norm_mix_g, "norm_mix")
    flip_bits = (2, 1, 3)

    def w_in_view():
        return buf_a[0].reshape(N_CHIPS, d, N_SLOTS * u)

    proj = _proj_in_shard(xn, w_in_view(), chip_arr, None, "proj_in_own")
    bufs = [_cast_shard(w, chip_arr, "cast_" + nm) for w, nm in zip(big[1:], names[1:])]
    (fs, fr, rs, rr), buf_a = _gather_forward_relay(buf_a, sem_a[1], [proj] + bufs, "gather_forward_w_in")
    buf_a = _gather_finish(buf_a, sem_a[0], fs, fr, [], "gather_finish_w_in", flips=NEIGHBOURS)
    for j in NEIGHBOURS:
        proj = _proj_in_shard(xn, w_in_view(), chip_arr ^ flip_bits[j], proj, "proj_in_from_%d" % j)
    (sem_b, sem_c, sem_d), bufs = _gather_start(
        bufs, [([0, 1, 2], ALL_FLIPS), ([3], ALL_FLIPS), ([4], ALL_FLIPS)], [proj], "gather_start_rest")
    (ds, dr), buf_a = _gather_forward_diag(buf_a, rr, [proj, bufs[0]], "gather_forward_w_in_diag")
    buf_a = _gather_finish_diag(buf_a, rs, ds, dr, [], "gather_finish_w_in_diag")
    proj = _proj_in_shard(xn, w_in_view(), chip_arr ^ flip_bits[2], proj, "proj_in_from_2")
    w_in_g = w_in_view()
    proj_a = proj_b = proj
    y_attn, lse = _attn_fwd(proj_a, slopes)
    y_lru, h_lru = _lru_fwd(proj_b, conv_w_full, conv_b, wa, lru_ba, wx, lru_bx, lru_lambda)
    fsem_b, buf_b = _gather_forward(bufs[:3], sem_b[1], [y_attn, y_lru], "gather_forward_proj")
    buf_b = _gather_finish(buf_b, sem_b[0], fsem_b[0], fsem_b[1], [], "gather_finish_proj")
    wpa_g = buf_b[0].reshape(d, d)
    wpl_g = buf_b[1].reshape(d, d)
    wout_g = buf_b[2].reshape(d, d)

    tn = u
    sd_f32 = jax.ShapeDtypeStruct((s, d), F32)
    sd_bf16 = jax.ShapeDtypeStruct((s, d), BF16)
    col = pl.BlockSpec((s, tn), lambda i, j, k: (0, j))

    def slot(n):
        return pl.BlockSpec((None, s, tn), lambda i, j, k: (n, 0, j))

    p_attn = _mm_nn("proj_attn", y_attn, wpa_g, [], [], [sd_bf16], [col], _store, tn)[0]
    fsem_c, buf_c = _gather_forward(bufs[3:4], sem_c[1], [p_attn], "gather_forward_w_up")

    def merge(acc, extras, outs):
        pa_ref, ga_ref, gl_ref = extras
        merged = _sigmoid(ga_ref[...]) * pa_ref[...].astype(F32) + _sigmoid(gl_ref[...]) * acc
        outs[0][...] = merged.astype(BF16)
        outs[1][...] = acc.astype(BF16)

    tn2 = max(HEAD_DIM, u // 2)
    col2 = pl.BlockSpec((s, tn2), lambda i, j, k: (0, j))

    def slot2(n):
        return pl.BlockSpec((None, s, tn2), lambda i, j, k: (n, 0, j))

    merged, p_lru = _mm_nn("proj_lru_merge", y_lru, wpl_g, [p_attn, proj, proj], [col2, slot2(5), slot2(6)],
                           [sd_bf16, sd_bf16], [col2, col2], merge, tn2)

    def add_resid(acc, extras, outs):
        outs[0][...] = extras[0][...] + acc

    h1 = _mm_nn("w_out_resid", merged, wout_g, [xs], [col], [sd_f32], [col], add_resid, tn)[0]
    hn = _rms_fwd(h1, norm_mlp_g, "norm_mlp")
    buf_c = _gather_finish(buf_c, sem_c[0], fsem_c[0], fsem_c[1], [hn], "gather_finish_w_up")
    wup_g = buf_c[0].reshape(N_CHIPS, d, d)

    def relu_sq(acc, extras, outs):
        r = jnp.maximum(acc, 0.0)
        outs[0][...] = (r * r).astype(BF16)
        outs[1][...] = r.astype(BF16)

    sf_bf16 = jax.ShapeDtypeStruct((s, ff), BF16)
    hid, relu_up = _mm(
        "w_up_relu2", [hn, wup_g],
        [pl.BlockSpec((s, d), lambda i, j, k: (0, 0)),
         pl.BlockSpec((None, d, tn), lambda i, j, k: (j // 4, 0, j % 4))],
        [sf_bf16, sf_bf16], [col, col], (1, ff // tn, 1), NN, relu_sq)
    fsem_d, buf_d = _gather_forward(bufs[4:], sem_d[1], [hid], "gather_forward_w_down")
    wdown_g = _gather_finish(buf_d, sem_d[0], fsem_d[0], fsem_d[1], [], "gather_finish_w_down")[0].reshape(ff, d)
    h2 = _mm(
        "w_down_resid", [hid, wdown_g, h1],
        [pl.BlockSpec((s, d), lambda i, j, k: (0, k)), pl.BlockSpec((d, tn), lambda i, j, k: (k, j)), col],
        [sd_f32], [col], (1, d // tn, ff // d), NN, add_resid, nk=ff // d, acc_shape=(s, tn))[0]
    loss_part, dh2, dh2_b, d_gf = _loss_head(h2, gf, target)
    loss = lax.psum(loss_part[0, 0], AXES)

    def relu_sq_bwd(acc, extras, outs):
        outs[0][...] = (acc * (2.0 * extras[0][...].astype(F32))).astype(BF16)

    dup = _mm_nt("d_hid", dh2_b, wdown_g, [relu_up], [col], [sf_bf16], [col], relu_sq_bwd, tn)[0]
    tok_d = pl.BlockSpec((s, d), lambda i, j: (0, 0))
    g_wdown = _mm_tn(
        "g_w_down", hid, dh2_b, pl.BlockSpec((s, d), lambda i, j: (0, i)),
        pl.BlockSpec((s, tn), lambda i, j: (0, j)), jax.ShapeDtypeStruct((ff, d), BF16),
        pl.BlockSpec((d, tn), lambda i, j: (i, j)), (ff // d, d // tn), d, tn, s)
    dhn = _mm(
        "d_hn", [dup, wup_g],
        [pl.BlockSpec((s, d), lambda i, j, k: (0, k)), pl.BlockSpec((None, tn, d), lambda i, j, k: (k, j, 0))],
        [sd_f32], [col], (1, d // tn, ff // d), NT, _store, nk=ff // d, acc_shape=(s, tn))[0]
    g_wup = _mm_tn(
        "g_w_up", hn, dup, tok_d, pl.BlockSpec((s, tn), lambda i, j: (0, j)),
        jax.ShapeDtypeStruct((N_CHIPS, d, d), BF16), pl.BlockSpec((None, d, tn), lambda i, j: (j // 4, 0, j % 4)),
        (1, ff // tn), d, tn, s)
    big_m = _in_hbm([m_w_in[0], m_w_proj_attn[0], m_w_proj_lru[0], m_w_out[0], m_w_up[0], m_w_down[0]])
    big_v = _in_hbm([v_w_in[0], v_w_proj_attn[0], v_w_proj_lru[0], v_w_out[0], v_w_up[0], v_w_down[0]])
    big_out = {}

    def reduce_begin(ids, gs, tag, everywhere=None):
        g4 = [g.reshape(N_CHIPS, 2, big[i].shape[0] // 2, big[i].shape[1]) for i, g in zip(ids, gs)]
        tags = [names[i] for i in ids]
        if everywhere is not None:
            g4.append(everywhere.reshape(N_CHIPS, 2, everywhere.shape[0] // (2 * N_CHIPS), everywhere.shape[1]))
            tags.append("small_" + tag)
        from_sibling = _pair_exchange(g4, "pair_exchange_" + tag)
        sums = [_pair_sum(g, r, cidx, "pair_sum_" + t) for t, g, r in zip(tags, g4, from_sibling)]
        to_all = () if everywhere is None else (len(ids),)
        return _chip_start(sums, "chip_start_" + tag, to_all), to_all

    def pair_begin(ids, gs, tag):
        g4 = [g.reshape(N_CHIPS, 2, big[i].shape[0] // 2, big[i].shape[1]) for i, g in zip(ids, gs)]
        return _pair_start(g4, "pair_start_" + tag)

    def reduce_begin_paired(ids, paired, after, tag):
        sems, g4, lands, _ = paired
        g4, from_sibling = _pair_wait(sems, g4, lands, after, "pair_wait_" + tag)
        sums = [_pair_sum(g, r, cidx, "pair_sum_" + names[i]) for i, g, r in zip(ids, g4, from_sibling)]
        return _chip_start(sums, "chip_start_" + tag), ()

    def reduce_mid(ids, begun, after, tag):
        (sems, sums, lands, _), to_all = begun
        sums, lands = _chip_wait(sems, sums, lands, after, "chip_wait_" + tag, to_all)
        halves = [_chip_sum(p, own, place_arr, "chip_sum_" + names[i]) for i, p, own in zip(ids, lands, sums)]
        if to_all:
            halves.append(_chip_sum_all(lands[-1], sums[-1], place_arr, "chip_sum_small_" + tag))
        return _half_start(halves, "half_start_" + tag), to_all

    def reduce_end(ids, mid, after, tag):
        (hsems, halves), to_all = mid
        full = _half_wait(hsems, halves, after, "half_wait_" + tag)
        done = []
        for i, g in zip(ids, full):
            res = _adamw(big[i], g.reshape(big[i].shape), big_m[i], big_v[i], "adamw_" + names[i])
            big_out[names[i]] = tuple(a[None] for a in res)
            done.append(res[1])
        everywhere = full[-1].reshape(-1, full[-1].shape[-1]) if to_all else None
        return everywhere, done

    def after_token(a, begun):
        return a + begun[0][3][:1, :1]

    pair_mlp = pair_begin([4, 5], [g_wup, g_wdown], "mlp")
    dh1, dh1_b, d_gmlp = _rms_bwd(h1, norm_mlp_g + pair_mlp[3][:1, :1], dhn, dh2, "norm_mlp_bwd")

    g_wout = _mm_tn(
        "g_w_out", merged, dh1_b, tok_d, pl.BlockSpec((s, tn), lambda i, j: (0, j)),
        jax.ShapeDtypeStruct((d, d), BF16), pl.BlockSpec((d, tn), lambda i, j: (0, j)), (1, d // tn), d, tn, s)

    def merge_bwd(acc, extras, outs):
        pa_ref, pl_ref, ga_ref, gl_ref = extras
        sa, sl = _sigmoid(ga_ref[...]), _sigmoid(gl_ref[...])
        outs[0][...] = (acc * sa).astype(BF16)
        outs[1][...] = (acc * sl).astype(BF16)
        outs[2][0] = (acc * pa_ref[...].astype(F32) * (sa * (1.0 - sa))).astype(BF16)
        outs[2][1] = (acc * pl_ref[...].astype(F32) * (sl * (1.0 - sl))).astype(BF16)

    nb = N_SLOTS - N_QKV
    d_pa, d_pl, dproj_b = _mm_nt(
        "d_merged", dh1_b, wout_g, [p_attn, p_lru, proj, proj], [col2, col2, slot2(5), slot2(6)],
        [sd_bf16, sd_bf16, jax.ShapeDtypeStruct((nb, s, d), BF16)],
        [col2, col2, pl.BlockSpec((2, s, tn2), lambda i, j, k: (1, 0, j))], merge_bwd, tn2)
    red_mlp = reduce_begin_paired([4, 5], pair_mlp, [d_pa], "mlp")
    dy_attn = _mm_nt("d_y_attn", d_pa, wpa_g, [], [], [sd_f32], [col], _store, tn)[0]
    dy_lru = _mm_nt("d_y_lru", d_pl, wpl_g, [], [], [sd_f32], [col], _store, tn)[0]
    g_wpa = _mm_tn(
        "g_w_proj_attn", y_attn, d_pa, tok_d, pl.BlockSpec((s, tn), lambda i, j: (0, j)),
        jax.ShapeDtypeStruct((d, d), BF16), pl.BlockSpec((d, tn), lambda i, j: (0, j)), (1, d // tn), d, tn, s)
    g_wpl = _mm_tn(
        "g_w_proj_lru", y_lru, d_pl, tok_d, pl.BlockSpec((s, tn), lambda i, j: (0, j)),
        jax.ShapeDtypeStruct((d, d), BF16), pl.BlockSpec((d, tn), lambda i, j: (0, j)), (1, d // tn), d, tn, s)

    pair_proj = pair_begin([1, 2, 3], [g_wpa, g_wpl, g_wout], "proj")

    dproj_b, d_cw, d_cb, d_wa, d_ba, d_wx, d_bx, d_lam = _lru_bwd(
        proj_b, h_lru, dy_lru, conv_w_full, conv_b, wa, lru_ba, wx, lru_bx, lru_lambda + pair_proj[3][:1, :1],
        dproj_b)
    red_proj = reduce_begin_paired([1, 2, 3], pair_proj, [dproj_b], "proj")
    dproj_a = _attn_bwd(proj_a, after_token(slopes, red_proj), y_attn, lse, dy_attn)
    per = N_SLOTS
    g_win_shape = jax.ShapeDtypeStruct((N_CHIPS, d, N_SLOTS * u), BF16)

    def g_win_part(name, dproj, first, prev):
        n_units = 4 * dproj.shape[0]
        return _mm_tn(
            name, xn, dproj, tok_d, pl.BlockSpec((None, s, u), lambda i, j: (j // 4, 0, j % 4)),
            g_win_shape, pl.BlockSpec((None, d, u), lambda i, j: ((j + first) // per, 0, (j + first) % per)),
            (1, n_units), d, u, s, aliases=None if prev is None else {2: 0}, extra=prev)

    mat_rows = heads * HEAD_DIM * HEAD_DIM // d
    vec_names = ["norm_mix_g", "conv_b", "lru_ba", "lru_bx", "lru_lambda", "norm_mlp_g", "norm_final_g"]

    def pack(wa_, wx_, cw_, vecs, name):
        rows = [wa_.reshape(mat_rows, d), wx_.reshape(mat_rows, d), cw_] + [a.reshape(1, d) for a in vecs]
        n = sum(a.shape[0] for a in rows)
        return _pack_rows(rows, n + (-n % 64), name)

    zero_cw = jnp.zeros((CONV_TAPS, d), F32)
    small_g = pack(d_wa, d_wx, d_cw, [jnp.zeros((1, d), F32), d_cb, d_ba, d_bx, d_lam, d_gmlp, d_gf], "pack_small_g")
    g_win = g_win_part("g_w_in_qkv", dproj_a, 0, None)
    g_win = g_win_part("g_w_in_rest", dproj_b, 4 * N_QKV, g_win)
    red_in = reduce_begin([0], [g_win], "w_in", everywhere=small_g)
    dxn = _dxn(dproj_a, dproj_b, w_in_g, d, [red_in[0][3]])
    grad_x, _, d_gmix = _rms_bwd(xs, norm_mix_g, dxn, dh1, "norm_mix_bwd")

    mid_mlp = reduce_mid([4, 5], red_mlp, [grad_x], "mlp")
    mid_proj = reduce_mid([1, 2, 3], red_proj, [mid_mlp[0][1][0]], "proj")
    _, done_mlp = reduce_end([4, 5], mid_mlp, [mid_proj[0][1][0]], "mlp")
    _, done_proj = reduce_end([1, 2, 3], mid_proj, done_mlp[-1:], "proj")
    done = done_mlp + done_proj
    small_w = pack(wa, wx, zero_cw, [norm_mix_g, conv_b, lru_ba, lru_bx, lru_lambda, norm_mlp_g, norm_final_g],
                   "pack_small_w")
    small_m = pack(m_lru_wa[0], m_lru_wx[0], zero_cw,
                   [m_norm_mix_g, m_conv_b, m_lru_ba, m_lru_bx, m_lru_lambda, m_norm_mlp_g, m_norm_final_g],
                   "pack_small_m")
    small_v = pack(v_lru_wa[0], v_lru_wx[0], zero_cw,
                   [v_norm_mix_g, v_conv_b, v_lru_ba, v_lru_bx, v_lru_lambda, v_norm_mlp_g, v_norm_final_g],
                   "pack_small_v")
    mid_in = reduce_mid([0], red_in, done + [small_w, small_m, small_v], "w_in")
    gmix_parts = _all_gather8(jnp.pad(d_gmix, ((0, 7), (0, 0))), "gather_gain_grad", [mid_in[0][1][0]])
    gmix_out = _sum8_adamw_row(gmix_parts, norm_mix_g, m_norm_mix_g, v_norm_mix_g, "sum_adamw_norm_mix_g")
    small_sum, _ = reduce_end([0], mid_in, [gmix_out[1]], "w_in")
    small = _adamw(small_w, small_sum, small_m, small_v, "adamw_small")
    g_cw = lax.dynamic_slice(small_sum[2 * mat_rows:2 * mat_rows + CONV_TAPS], (0, chip * u), (CONV_TAPS, u))
    cw_out = _adamw(conv_w[0], g_cw, m_conv_w[0], v_conv_w[0], "adamw_conv_w")

    def small_leaf(kind, name):
        a = small[kind]
        if name == "norm_mix_g":
            return gmix_out[kind]
        if name == "lru_wa":
            return a[0:mat_rows].reshape(lru_wa.shape)
        if name == "lru_wx":
            return a[mat_rows:2 * mat_rows].reshape(lru_wx.shape)
        if name == "conv_w":
            return cw_out[kind][None]
        row = a[2 * mat_rows + CONV_TAPS + vec_names.index(name)]
        return row if name == "norm_final_g" else row[None]

    order = ["norm_mix_g", "w_in", "conv_w", "conv_b", "lru_wa", "lru_ba", "lru_wx", "lru_bx", "lru_lambda",
             "w_proj_attn", "w_proj_lru", "w_out", "norm_mlp_g", "w_up", "w_down", "norm_final_g"]
    outs = [loss, grad_x[None]]
    for kind in range(4):
        for name in order:
            outs.append(big_out[name][kind] if name in big_out else small_leaf(kind, name))
    return tuple(outs)
```

```python
import functools

import jax
import jax.numpy as jnp
from jax import lax
from jax.experimental import pallas as pl
from jax.experimental.pallas import tpu as pltpu

F32 = jnp.float32
BF16 = jnp.bfloat16
MESH = pl.DeviceIdType.MESH
AXES = ("x", "y", "c")

N_CHIPS = 4
N_DEV = 8
HEAD_DIM = 128
ATTN_BLK = 128
DILATIONS = (1, 4, 16)
ATTN_UNROLL = 16
CONV_TAPS = 4
LRU_C = 8.0
EPS = 1e-6
N_SLOTS = 7
N_QKV = 3
VMEM_MIB = 2 ** 20
VMEM_V7X = 64 * VMEM_MIB
STREAM_TILE = 4 * VMEM_MIB
VMEM_STREAM_MIB = 32
VMEM_TILES_MIB = 56

ADAM_LR = 0.001
ADAM_B1 = 0.9
ADAM_B2 = 0.999
ADAM_EPS = 1e-08
ADAM_WD = 0.01
ADAM_STEP = 10

NN = (((1,), (0,)), ((), ()))
NT = (((1,), (1,)), ((), ()))
TN = (((0,), (0,)), ((), ()))


def _params(vmem_mib=None, **kw):
    limit = None if vmem_mib is None else min(vmem_mib * VMEM_MIB, VMEM_V7X - 8 * VMEM_MIB)
    return pltpu.CompilerParams(vmem_limit_bytes=limit, **kw)


def _row_tile(rows, row_bytes, budget=VMEM_MIB):
    t = rows
    while t % 16 == 0 and t * row_bytes > budget:
        t //= 2
    return t


def _dot(a, b, dims):
    return lax.dot_general(a.astype(BF16), b.astype(BF16), dims, preferred_element_type=F32)


def _sigmoid(x):
    return jax.nn.sigmoid(x)


def _rms_fwd(x, g, name):
    s, d = x.shape
    tm = _row_tile(s, d * 4, STREAM_TILE // 2)

    def body(x_ref, g_ref, o_ref):
        xf = x_ref[...]
        r = lax.rsqrt(jnp.mean(xf * xf, axis=-1, keepdims=True) + EPS)
        o_ref[...] = (xf * r * g_ref[...]).astype(o_ref.dtype)

    return pl.pallas_call(
        body, name=name, grid=(s // tm,),
        in_specs=[pl.BlockSpec((tm, d), lambda i: (i, 0)), pl.BlockSpec((1, d), lambda i: (0, 0))],
        out_specs=pl.BlockSpec((tm, d), lambda i: (i, 0)),
        out_shape=jax.ShapeDtypeStruct((s, d), BF16), compiler_params=_params(VMEM_STREAM_MIB),
    )(x, g)


def _rms_bwd(x, g, dy, resid, name):
    s, d = x.shape
    tm = _row_tile(s, d * 4, STREAM_TILE // 2)

    def body(x_ref, g_ref, dy_ref, res_ref, dx_ref, dxb_ref, dg_ref):
        xf = x_ref[...]
        r = lax.rsqrt(jnp.mean(xf * xf, axis=-1, keepdims=True) + EPS)
        xh = xf * r
        dyv = dy_ref[...]
        dxh = dyv * g_ref[...]
        dx = r * (dxh - xh * jnp.mean(dxh * xh, axis=-1, keepdims=True)) + res_ref[...]
        dx_ref[...] = dx
        dxb_ref[...] = dx.astype(BF16)
        part = jnp.sum(dyv * xh, axis=0, keepdims=True)

        @pl.when(pl.program_id(0) == 0)
        def _():
            dg_ref[...] = part

        @pl.when(pl.program_id(0) > 0)
        def _():
            dg_ref[...] += part

    row = pl.BlockSpec((tm, d), lambda i: (i, 0))
    vec = pl.BlockSpec((1, d), lambda i: (0, 0))
    return pl.pallas_call(
        body, name=name, grid=(s // tm,),
        in_specs=[row, vec, row, row], out_specs=[row, row, vec],
        out_shape=[jax.ShapeDtypeStruct((s, d), F32), jax.ShapeDtypeStruct((s, d), BF16),
                   jax.ShapeDtypeStruct((1, d), F32)],
        compiler_params=_params(VMEM_STREAM_MIB),
    )(x, g, dy, resid)


def _loss_head(h2, g, target):
    s, d = h2.shape
    tm = _row_tile(s, d * 4, STREAM_TILE // 2)

    def body(x_ref, g_ref, t_ref, loss_ref, dx_ref, dxb_ref, dg_ref):
        xf = x_ref[...]
        gv = g_ref[...]
        r = lax.rsqrt(jnp.mean(xf * xf, axis=-1, keepdims=True) + EPS)
        xh = xf * r
        err = xh * gv - t_ref[...]
        part = jnp.sum(jnp.sum(err * err, axis=1, keepdims=True), axis=0, keepdims=True) * (0.5 / d)
        dyv = err * (1.0 / d)
        dxh = dyv * gv
        dx = r * (dxh - xh * jnp.mean(dxh * xh, axis=-1, keepdims=True))
        dx_ref[...] = dx
        dxb_ref[...] = dx.astype(BF16)
        dgp = jnp.sum(dyv * xh, axis=0, keepdims=True)

        @pl.when(pl.program_id(0) == 0)
        def _():
            dg_ref[...] = dgp
            loss_ref[...] = jnp.broadcast_to(part, loss_ref.shape)

        @pl.when(pl.program_id(0) > 0)
        def _():
            dg_ref[...] += dgp
            loss_ref[...] += jnp.broadcast_to(part, loss_ref.shape)

    row = pl.BlockSpec((tm, d), lambda i: (i, 0))
    vec = pl.BlockSpec((1, d), lambda i: (0, 0))
    return pl.pallas_call(
        body, name="loss_head", grid=(s // tm,),
        in_specs=[row, vec, row],
        out_specs=[pl.BlockSpec((8, 128), lambda i: (0, 0)), row, row, vec],
        out_shape=[jax.ShapeDtypeStruct((8, 128), F32), jax.ShapeDtypeStruct((s, d), F32),
                   jax.ShapeDtypeStruct((s, d), BF16), jax.ShapeDtypeStruct((1, d), F32)],
        compiler_params=_params(VMEM_STREAM_MIB),
    )(h2, g, target)


def _mm(name, operands, in_specs, out_shape, out_specs, grid, dims, epilogue, nk=1, acc_shape=None,
        aliases=None):
    n_in = len(operands)
    n_out = len(out_shape)

    def body(*refs):
        a_ref, b_ref = refs[0], refs[1]
        extras = refs[2:n_in]
        outs = refs[n_in:n_in + n_out]

        def prod():
            return _dot(a_ref[...], b_ref[...], dims)

        if nk == 1:
            epilogue(prod(), extras, outs)
        else:
            acc = refs[n_in + n_out]
            k = pl.program_id(2)

            @pl.when(k == 0)
            def _():
                acc[...] = prod()

            @pl.when(k > 0)
            def _():
                acc[...] += prod()

            @pl.when(k == nk - 1)
            def _():
                epilogue(acc[...], extras, outs)

    scratch = [] if nk == 1 else [pltpu.VMEM(acc_shape, F32)]
    return pl.pallas_call(
        body, name=name, grid=grid, in_specs=in_specs, out_specs=out_specs, out_shape=out_shape,
        scratch_shapes=scratch, input_output_aliases=aliases or {},
        compiler_params=_params(VMEM_TILES_MIB),
    )(*operands)


def _store(acc, extras, outs):
    outs[0][...] = acc.astype(outs[0].dtype)


def _proj_in_shard(xn, w_in_g, shard_arr, prev, name):
    s, d = xn.shape
    u = d // 4
    per = N_SLOTS
    n_prev = 0 if prev is None else 1

    def body(sh_ref, x_ref, w_ref, *rest):
        del sh_ref
        rest[n_prev][...] = _dot(x_ref[...], w_ref[...], NN)

    def out_map(j, sh_ref):
        unit = per * sh_ref[0] + j
        return (unit // 4, 0, unit % 4)

    return pl.pallas_call(
        body, name=name,
        grid_spec=pltpu.PrefetchScalarGridSpec(
            num_scalar_prefetch=1, grid=(per,),
            in_specs=[pl.BlockSpec((s, d), lambda j, sh_ref: (0, 0)),
                      pl.BlockSpec((None, d, u), lambda j, sh_ref: (sh_ref[0], 0, j))] + [HBM] * n_prev,
            out_specs=pl.BlockSpec((None, s, u), out_map)),
        out_shape=jax.ShapeDtypeStruct((N_SLOTS, s, d), F32),
        input_output_aliases={3: 0} if n_prev else {},
        compiler_params=_params(VMEM_TILES_MIB),
    )(shard_arr, xn, w_in_g, *([] if prev is None else [prev]))


def _mm_nn(name, a, b, extras, extra_specs, out_shape, out_specs, epilogue, tn, aliases=None):
    s, kdim = a.shape
    n = b.shape[1]
    return _mm(
        name, [a, b] + list(extras),
        [pl.BlockSpec((s, kdim), lambda i, j, k: (0, 0)), pl.BlockSpec((kdim, tn), lambda i, j, k: (0, j))]
        + list(extra_specs),
        out_shape, out_specs, (1, n // tn, 1), NN, epilogue, aliases=aliases)


def _mm_nt(name, a, b, extras, extra_specs, out_shape, out_specs, epilogue, tn, aliases=None):
    s, kdim = a.shape
    n = b.shape[0]
    return _mm(
        name, [a, b] + list(extras),
        [pl.BlockSpec((s, kdim), lambda i, j, k: (0, 0)), pl.BlockSpec((tn, kdim), lambda i, j, k: (j, 0))]
        + list(extra_specs),
        out_shape, out_specs, (1, n // tn, 1), NT, epilogue, aliases=aliases)


def _mm_tn(name, a, b, a_spec, b_spec, out_shape, out_spec, grid, m, tn, s, aliases=None, extra=None):
    ch = ROW_CHUNK
    n_in = 2 if extra is None else 3

    def body(*refs):
        a_ref, b_ref = refs[0], refs[1]
        o_ref, at_ref = refs[n_in], refs[n_in + 1]

        @pl.when(pl.program_id(1) == 0)
        def _():
            for c0 in range(0, s, ch):
                at_ref[:, c0:c0 + ch] = a_ref[c0:c0 + ch, :].astype(F32).T.astype(BF16)

        o_ref[...] = _dot(at_ref[...], b_ref[...], NN).astype(o_ref.dtype)

    operands = [a, b] + ([] if extra is None else [extra])
    in_specs = [a_spec, b_spec] + ([] if extra is None else [pl.BlockSpec(memory_space=pl.ANY)])
    return pl.pallas_call(
        body, name=name, grid=grid, in_specs=in_specs, out_specs=out_spec, out_shape=out_shape,
        scratch_shapes=[pltpu.VMEM((m, s), BF16)], input_output_aliases=aliases or {},
        compiler_params=_params(VMEM_TILES_MIB),
    )(*operands)


def _dxn(dproj_a, dproj_b, w_in_g, tn, after):
    n_a, s, d = dproj_a.shape
    u = d // 4
    ua = 4 * n_a
    nk = 4 * N_SLOTS
    per = N_SLOTS

    def body(a_ref, b_ref, w_ref, *rest):
        o_ref = rest[len(after)]
        k = pl.program_id(2)

        @pl.when(k == 0)
        def _():
            o_ref[...] = jnp.zeros_like(o_ref)

        @pl.when(k < ua)
        def _():
            o_ref[...] += _dot(a_ref[...], w_ref[...], NT)

        @pl.when(k >= ua)
        def _():
            o_ref[...] += _dot(b_ref[...], w_ref[...], NT)

    def a_map(i, j, k):
        kk = jnp.minimum(k, ua - 1)
        return (kk // 4, 0, kk % 4)

    def b_map(i, j, k):
        kk = jnp.maximum(k - ua, 0)
        return (kk // 4, 0, kk % 4)

    return pl.pallas_call(
        body, name="dxn", grid=(1, d // tn, nk),
        in_specs=[pl.BlockSpec((None, s, u), a_map), pl.BlockSpec((None, s, u), b_map),
                  pl.BlockSpec((None, tn, u), lambda i, j, k: (k // per, j, k % per))] + [HBM] * len(after),
        out_specs=pl.BlockSpec((s, tn), lambda i, j, k: (0, j)),
        out_shape=jax.ShapeDtypeStruct((s, d), F32),
        compiler_params=_params(VMEM_TILES_MIB),
    )(dproj_a, dproj_b, w_in_g, *after)


def _attn_masks(slope, dil):
    ii = lax.broadcasted_iota(jnp.int32, (ATTN_BLK, 2 * ATTN_BLK), 0)
    jj = lax.broadcasted_iota(jnp.int32, (ATTN_BLK, 2 * ATTN_BLK), 1)
    diff = ATTN_BLK + ii - jj
    band = (diff >= 0) & (diff <= ATTN_BLK)
    bias = -(slope * float(dil)) * diff.astype(F32)
    return band, bias, jj


def _attn_window(t, nblk):
    cur = pl.ds(pl.multiple_of(t * ATTN_BLK, ATTN_BLK), ATTN_BLK)
    prev = pl.ds(pl.multiple_of(jnp.maximum(t - 1, 0) * ATTN_BLK, ATTN_BLK), ATTN_BLK)
    first = jnp.where(t % nblk == 0, ATTN_BLK, 0)
    return prev, cur, first


def _unrolled_loop(n, step, init, unroll=ATTN_UNROLL):
    def trip(i, carry):
        for k in range(unroll):
            carry = step(i * unroll + k, carry)
        return carry

    return lax.fori_loop(0, n // unroll, trip, init)


def _streams(pairs, dil, s):
    if dil == 1:
        return [src for _, src in pairs]
    seg = s // dil
    for dst, src in pairs:
        for r in range(dil):
            dst[r * seg:(r + 1) * seg, :] = src[pl.ds(r, seg, stride=dil), :].astype(dst.dtype)
    return [dst for dst, _ in pairs]


def _attn_fwd(proj_a, slopes):
    _, s, d = proj_a.shape
    heads = d // HEAD_DIM
    scale = HEAD_DIM ** -0.5
    n_t = s // ATTN_BLK
    ng = len(DILATIONS)

    def body(q_ref, k_ref, v_ref, sl_ref, o_ref, lse_ref, qd, kd, vd, od, ld, og, lg):
        slope = sl_ref[...][:, :1]
        for g, dil in enumerate(DILATIONS):
            nblk = s // dil // ATTN_BLK
            qs, ks, vs = _streams([(qd, q_ref), (kd, k_ref), (vd, v_ref)], dil, s)
            o_t, l_t = (og.at[g], lg.at[g]) if dil == 1 else (od, ld)
            band, bias, jj = _attn_masks(slope, dil)

            def blk(t, carry, nblk=nblk, band=band, bias=bias, jj=jj, qs=qs, ks=ks, vs=vs, o_t=o_t, l_t=l_t):
                prev, cur, first = _attn_window(t, nblk)
                kk = jnp.concatenate([ks[prev, :], ks[cur, :]], axis=0)
                vv = jnp.concatenate([vs[prev, :], vs[cur, :]], axis=0)
                sc = _dot(qs[cur, :], kk, NT) * scale + bias
                sc = jnp.where(band & (jj >= first), sc, -jnp.inf)
                m = jnp.max(sc, axis=1, keepdims=True)
                p = jnp.exp(sc - m)
                l = jnp.sum(p, axis=1, keepdims=True)
                o_t[cur, :] = _dot(p, vv, NN) / l
                l_t[cur, :] = jnp.broadcast_to(m + jnp.log(l), (ATTN_BLK, HEAD_DIM))
                return carry

            _unrolled_loop(n_t, blk, 0)
            seg = s // dil
            if dil > 1:
                for r in range(dil):
                    og[g, pl.ds(r, seg, stride=dil), :] = od[r * seg:(r + 1) * seg, :]
                    lg[g, pl.ds(r, seg, stride=dil), :] = ld[r * seg:(r + 1) * seg, :]

        ch = ROW_CHUNK

        def combine(c, carry):
            rows = pl.ds(pl.multiple_of(c * ch, ch), ch)
            ls = [lg[g, rows, :] for g in range(ng)]
            mx = functools.reduce(jnp.maximum, ls)
            es = [jnp.exp(x - mx) for x in ls]
            den = functools.reduce(jnp.add, es)
            num = functools.reduce(jnp.add, [es[g] * og[g, rows, :] for g in range(ng)])
            o_ref[rows, :] = (num / den).astype(o_ref.dtype)
            lse_ref[rows, :] = mx + jnp.log(den)
            return carry

        lax.fori_loop(0, s // ch, combine, 0)

    def col(slot):
        return pl.BlockSpec((None, s, HEAD_DIM), lambda h: (slot, 0, h))

    head = pl.BlockSpec((s, HEAD_DIM), lambda h: (0, h))
    return pl.pallas_call(
        body, name="attn_fwd", grid=(heads,),
        in_specs=[col(0), col(1), col(2), pl.BlockSpec((None, 1, HEAD_DIM), lambda h: (h, 0, 0))],
        out_specs=[head, head],
        out_shape=[jax.ShapeDtypeStruct((s, d), BF16), jax.ShapeDtypeStruct((s, d), F32)],
        scratch_shapes=[pltpu.VMEM((s, HEAD_DIM), BF16)] * 3 + [pltpu.VMEM((s, HEAD_DIM), F32)] * 2
        + [pltpu.VMEM((ng, s, HEAD_DIM), F32)] * 2,
        compiler_params=_params(VMEM_TILES_MIB),
    )(proj_a, proj_a, proj_a, slopes)


def _attn_bwd(proj_a, slopes, y_attn, lse, dy):
    _, s, d = proj_a.shape
    heads = d // HEAD_DIM
    scale = HEAD_DIM ** -0.5
    n_t = s // ATTN_BLK

    def body(q_ref, k_ref, v_ref, sl_ref, o_ref, lse_ref, dy_ref, out_ref,
             qd, kd, vd, dod, lsd, dld, delta, dqd, dkd, dvd, dqa, dka, dva):
        slope = sl_ref[...][:, :1]
        dyv = dy_ref[...]
        delta[...] = jnp.broadcast_to(
            jnp.sum(dyv * o_ref[...].astype(F32), axis=1, keepdims=True), (s, HEAD_DIM))
        for g, dil in enumerate(DILATIONS):
            nblk = s // dil // ATTN_BLK
            seg = s // dil
            qs, ks, vs, dos, lss, dls = _streams(
                [(qd, q_ref), (kd, k_ref), (vd, v_ref), (dod, dy_ref), (lsd, lse_ref), (dld, delta)], dil, s)
            dq_t, dk_t, dv_t = (dqa, dka, dva) if dil == 1 else (dqd, dkd, dvd)
            band, bias, jj = _attn_masks(slope, dil)

            def blk(t, carry, nblk=nblk, band=band, bias=bias, jj=jj, qs=qs, ks=ks, vs=vs, dos=dos, lss=lss,
                    dls=dls, dq_t=dq_t, dk_t=dk_t, dv_t=dv_t):
                ck, cv = carry
                prev, cur, first = _attn_window(t, nblk)
                q = qs[cur, :]
                do = dos[cur, :]
                lse_b = lss[cur, :]
                dl_b = dls[cur, :]
                kk = jnp.concatenate([ks[prev, :], ks[cur, :]], axis=0)
                vv = jnp.concatenate([vs[prev, :], vs[cur, :]], axis=0)
                sc = _dot(q, kk, NT) * scale + bias
                p = jnp.where(band & (jj >= first), jnp.exp(sc - jnp.concatenate([lse_b, lse_b], axis=1)), 0.0)
                dp = _dot(do, vv, NT)
                ds = p * (dp - jnp.concatenate([dl_b, dl_b], axis=1))
                dv_b = _dot(p, do, TN)
                dk_b = _dot(ds, q, TN) * scale
                dq_t[cur, :] = _dot(ds, kk, NN) * scale
                done = pl.ds(pl.multiple_of(jnp.where(t == 0, n_t, t - 1) * ATTN_BLK, ATTN_BLK), ATTN_BLK)
                dk_t[done, :] = ck + dk_b[:ATTN_BLK]
                dv_t[done, :] = cv + dv_b[:ATTN_BLK]
                return dk_b[ATTN_BLK:], dv_b[ATTN_BLK:]

            zero = jnp.zeros((ATTN_BLK, HEAD_DIM), F32)
            ck, cv = _unrolled_loop(n_t, blk, (zero, zero))
            dk_t[(n_t - 1) * ATTN_BLK:n_t * ATTN_BLK, :] = ck
            dv_t[(n_t - 1) * ATTN_BLK:n_t * ATTN_BLK, :] = cv
            if dil > 1:
                for acc, part in ((dqa, dqd), (dka, dkd), (dva, dvd)):
                    for r in range(dil):
                        acc[pl.ds(r, seg, stride=dil), :] += part[r * seg:(r + 1) * seg, :]
        out_ref[0] = dqa[...].astype(out_ref.dtype)
        out_ref[1] = dka[0:s, :].astype(out_ref.dtype)
        out_ref[2] = dva[0:s, :].astype(out_ref.dtype)

    def col(slot):
        return pl.BlockSpec((None, s, HEAD_DIM), lambda h: (slot, 0, h))

    head = pl.BlockSpec((s, HEAD_DIM), lambda h: (0, h))
    return pl.pallas_call(
        body, name="attn_bwd", grid=(heads,),
        in_specs=[col(0), col(1), col(2), pl.BlockSpec((None, 1, HEAD_DIM), lambda h: (h, 0, 0)),
                  head, head, head],
        out_specs=pl.BlockSpec((N_QKV, s, HEAD_DIM), lambda h: (0, 0, h)),
        out_shape=jax.ShapeDtypeStruct((N_QKV, s, d), BF16),
        scratch_shapes=[pltpu.VMEM((s, HEAD_DIM), BF16)] * 4 + [pltpu.VMEM((s, HEAD_DIM), F32)] * 4
        + [pltpu.VMEM((s + ATTN_BLK, HEAD_DIM), F32)] * 2 + [pltpu.VMEM((s, HEAD_DIM), F32)]
        + [pltpu.VMEM((s + ATTN_BLK, HEAD_DIM), F32)] * 2,
        compiler_params=_params(VMEM_TILES_MIB),
    )(proj_a, proj_a, proj_a, slopes, y_attn, lse, dy)


def _expm1(x):
    small = x * (1.0 + x * (0.5 + x * (1.0 / 6.0 + x * (1.0 / 24.0 + x * (1.0 / 120.0)))))
    return jnp.where(jnp.abs(x) < 0.1, small, jnp.exp(x) - 1.0)


def _softplus(x):
    return jnp.maximum(x, 0.0) + jnp.log1p(jnp.exp(-jnp.abs(x)))


GELU_K = 0.7978845608028654
GELU_C = 0.044715


def _gelu(x):
    t = jnp.tanh(GELU_K * (x + GELU_C * x * x * x))
    return 0.5 * x * (1.0 + t), t


def _gelu_grad(x, t):
    return 0.5 * (1.0 + t) + 0.5 * x * (1.0 - t * t) * GELU_K * (1.0 + 3.0 * GELU_C * x * x)


def _lru_gates(xc, wa, ba, wx, bx, sp):
    r = _sigmoid(_dot(xc, wa, NN) + ba)
    ig = _sigmoid(_dot(xc, wx, NN) + bx)
    log_a = -LRU_C * r * sp
    a = jnp.exp(log_a)
    mult = jnp.sqrt(-_expm1(2.0 * log_a))
    return r, ig, a, mult


def _scan_fwd(a, u, tt):
    row = lax.broadcasted_iota(jnp.int32, a.shape, 0)
    sh = 1
    while sh < tt:
        keep = row >= sh
        a_s = jnp.where(keep, pltpu.roll(a, sh, 0), 1.0)
        u_s = jnp.where(keep, pltpu.roll(u, sh, 0), 0.0)
        u = a * u_s + u
        a = a * a_s
        sh *= 2
    return a, u


def _scan_bwd(b, g, tt):
    row = lax.broadcasted_iota(jnp.int32, b.shape, 0)
    sh = 1
    while sh < tt:
        keep = row < tt - sh
        b_s = jnp.where(keep, pltpu.roll(b, tt - sh, 0), 1.0)
        g_s = jnp.where(keep, pltpu.roll(g, tt - sh, 0), 0.0)
        g = g + b * g_s
        b = b * b_s
        sh *= 2
    return b, g


ROW_CHUNK = 256


def _pad_copy(xpad_ref, x_ref, s):
    xpad_ref[0:8, :] = jnp.zeros((8, HEAD_DIM), F32)
    for c0 in range(0, s, ROW_CHUNK):
        xpad_ref[8 + c0:8 + c0 + ROW_CHUNK, :] = x_ref[c0:c0 + ROW_CHUNK, :]


def _conv_rows(dst_ref, xpad_ref, cw, cb, s):
    for c0 in range(0, s, ROW_CHUNK):
        acc = cb
        for j in range(CONV_TAPS):
            off = 8 - (CONV_TAPS - 1) + j + c0
            acc = acc + cw[j:j + 1, :] * xpad_ref[off:off + ROW_CHUNK, :]
        dst_ref[c0:c0 + ROW_CHUNK, :] = acc


LRU_TILE = 128
LRU_UNROLL = 16


def _lru_specs(s, d):
    heads = d // HEAD_DIM

    def col(slot):
        return pl.BlockSpec((None, s, HEAD_DIM), lambda h: (slot, 0, h))

    vec = pl.BlockSpec((1, HEAD_DIM), lambda h: (0, h))
    mat = pl.BlockSpec((None, HEAD_DIM, HEAD_DIM), lambda h: (h, 0, 0))
    cw = pl.BlockSpec((8, HEAD_DIM), lambda h: (0, h))
    head = pl.BlockSpec((s, HEAD_DIM), lambda h: (0, h))
    return heads, col, vec, mat, cw, head


def _lru_fwd(proj_b, conv_w, conv_b, wa, ba, wx, bx, lam):
    _, s, d = proj_b.shape
    heads, col, vec, mat, cws, head = _lru_specs(s, d)
    tt = LRU_TILE

    def body(xr_ref, xg_ref, cw_ref, cb_ref, wa_ref, ba_ref, wx_ref, bx_ref, lam_ref, y_ref, h_ref, xpad, xc_s):
        _pad_copy(xpad, xr_ref, s)
        _conv_rows(xc_s, xpad, cw_ref[...], cb_ref[...], s)
        sp = _softplus(-lam_ref[...])
        wav, wxv, bav, bxv = wa_ref[...], wx_ref[...], ba_ref[...], bx_ref[...]

        def tile(i, hc):
            rows = pl.ds(pl.multiple_of(i * tt, tt), tt)
            xc = xc_s[rows, :]
            _, ig, a, mult = _lru_gates(xc, wav, bav, wxv, bxv, sp)
            pa, hl = _scan_fwd(a, mult * (ig * xc), tt)
            h = hl + pa * hc
            h_ref[rows, :] = h
            gel, _ = _gelu(xg_ref[rows, :])
            y_ref[rows, :] = (h * gel).astype(y_ref.dtype)
            return h[tt - 1:tt, :]

        _unrolled_loop(s // tt, tile, jnp.zeros((1, HEAD_DIM), F32), LRU_UNROLL)

    return pl.pallas_call(
        body, name="lru_fwd", grid=(heads,),
        in_specs=[col(N_QKV), col(N_QKV + 1), cws, vec, mat, vec, mat, vec, vec],
        out_specs=[head, head],
        out_shape=[jax.ShapeDtypeStruct((s, d), BF16), jax.ShapeDtypeStruct((s, d), F32)],
        scratch_shapes=[pltpu.VMEM((s + 8, HEAD_DIM), F32), pltpu.VMEM((s, HEAD_DIM), F32)],
        compiler_params=_params(VMEM_STREAM_MIB),
    )(proj_b, proj_b, conv_w, conv_b, wa, ba, wx, bx, lam)


def _lru_bwd(proj_b, h_lru, dy, conv_w, conv_b, wa, ba, wx, bx, lam, dproj_b):
    _, s, d = proj_b.shape
    heads, col, vec, mat, cws, head = _lru_specs(s, d)
    tt = LRU_TILE
    n_t = s // tt

    def body(xr_ref, xg_ref, h_ref, dy_ref, cw_ref, cb_ref, wa_ref, ba_ref, wx_ref, bx_ref, lam_ref, alias_ref,
             out_ref, dcw_ref, dcb_ref, dwa_ref, dba_ref, dwx_ref, dbx_ref, dlam_ref, xpad, xc_s, dxc_s):
        del alias_ref
        _pad_copy(xpad, xr_ref, s)
        cwv = cw_ref[...]
        _conv_rows(xc_s, xpad, cwv, cb_ref[...], s)
        dxc_s[s:s + 8, :] = jnp.zeros((8, HEAD_DIM), F32)
        lamv = lam_ref[...]
        sp = _softplus(-lamv)
        wav, wxv, bav, bxv = wa_ref[...], wx_ref[...], ba_ref[...], bx_ref[...]
        dwa_ref[...] = jnp.zeros_like(dwa_ref)
        dwx_ref[...] = jnp.zeros_like(dwx_ref)
        zero = jnp.zeros((1, HEAD_DIM), F32)
        row = lax.broadcasted_iota(jnp.int32, (tt, HEAD_DIM), 0)

        def tile(it, carry):
            dh_next, a_next, dba, dbx, dsp, dcb = carry
            i = n_t - 1 - it
            t0 = pl.multiple_of(i * tt, tt)
            rows = pl.ds(t0, tt)
            xc = xc_s[rows, :]
            r, ig, a, mult = _lru_gates(xc, wav, bav, wxv, bxv, sp)
            h = h_ref[rows, :]
            before = h_ref[pl.ds(pl.multiple_of(jnp.maximum(t0 - 8, 0), 8), 8), :][7:8, :]
            before = before * (i > 0).astype(F32)
            h_prev = jnp.where(row == 0, before, pltpu.roll(h, 1, 0))
            xg = xg_ref[rows, :]
            dyv = dy_ref[rows, :]
            gel, th = _gelu(xg)
            out_ref[1, rows, :] = (dyv * h * _gelu_grad(xg, th)).astype(out_ref.dtype)
            b = jnp.where(row == tt - 1, a_next, pltpu.roll(a, tt - 1, 0))
            pb, z = _scan_bwd(b, dyv * gel, tt)
            dh = z + pb * dh_next
            da = dh * h_prev
            dmult = dh * (ig * xc)
            dig = dh * (mult * xc)
            dla = da * a - dmult * (a * a / mult)
            dzr = dla * (-LRU_C * sp) * (r * (1.0 - r))
            dzx = dig * (ig * (1.0 - ig))
            dxc = dh * (mult * ig) + _dot(dzr, wav, NT) + _dot(dzx, wxv, NT)
            dxc_s[rows, :] = dxc
            dwa_ref[...] += _dot(xc, dzr, TN)
            dwx_ref[...] += _dot(xc, dzx, TN)
            return (dh[0:1, :], a[0:1, :],
                    dba + jnp.sum(dzr, axis=0, keepdims=True),
                    dbx + jnp.sum(dzx, axis=0, keepdims=True),
                    dsp + jnp.sum(dla * (-LRU_C * r), axis=0, keepdims=True),
                    dcb + jnp.sum(dxc, axis=0, keepdims=True))

        _, _, dba, dbx, dsp, dcb = _unrolled_loop(n_t, tile, (zero, zero, zero, zero, zero, zero), LRU_UNROLL)
        dba_ref[...] = dba
        dbx_ref[...] = dbx
        dcb_ref[...] = dcb
        dlam_ref[...] = -dsp * _sigmoid(-lamv)
        dcw = [zero] * CONV_TAPS
        for c0 in range(0, s, ROW_CHUNK):
            dxc_c = dxc_s[c0:c0 + ROW_CHUNK, :]
            dxr = jnp.zeros((ROW_CHUNK, HEAD_DIM), F32)
            for j in range(CONV_TAPS):
                back = CONV_TAPS - 1 - j
                off = 8 - back + c0
                dcw[j] = dcw[j] + jnp.sum(dxc_c * xpad[off:off + ROW_CHUNK, :], axis=0, keepdims=True)
                dxr = dxr + cwv[j:j + 1, :] * dxc_s[back + c0:back + c0 + ROW_CHUNK, :]
            out_ref[0, c0:c0 + ROW_CHUNK, :] = dxr.astype(out_ref.dtype)
        for j in range(CONV_TAPS):
            dcw_ref[j:j + 1, :] = dcw[j]

    return pl.pallas_call(
        body, name="lru_bwd", grid=(heads,),
        in_specs=[col(N_QKV), col(N_QKV + 1), head, head, cws, vec, mat, vec, mat, vec, vec,
                  pl.BlockSpec(memory_space=pl.ANY)],
        out_specs=[pl.BlockSpec((2, s, HEAD_DIM), lambda h: (0, 0, h)),
                   pl.BlockSpec((CONV_TAPS, HEAD_DIM), lambda h: (0, h)), vec, mat, vec, mat, vec, vec],
        out_shape=[jax.ShapeDtypeStruct(dproj_b.shape, dproj_b.dtype),
                   jax.ShapeDtypeStruct((CONV_TAPS, d), F32), jax.ShapeDtypeStruct((1, d), F32),
                   jax.ShapeDtypeStruct(wa.shape, F32), jax.ShapeDtypeStruct((1, d), F32),
                   jax.ShapeDtypeStruct(wx.shape, F32), jax.ShapeDtypeStruct((1, d), F32),
                   jax.ShapeDtypeStruct((1, d), F32)],
        scratch_shapes=[pltpu.VMEM((s + 8, HEAD_DIM), F32), pltpu.VMEM((s, HEAD_DIM), F32),
                        pltpu.VMEM((s + 8, HEAD_DIM), F32)],
        input_output_aliases={11: 0},
        compiler_params=_params(VMEM_STREAM_MIB),
    )(proj_b, proj_b, h_lru, dy, conv_w, conv_b, wa, ba, wx, bx, lam, dproj_b)


def _place():
    x, y, c = (lax.axis_index(n) for n in AXES)
    return x, y, c


def _other_chips(x, y):
    return [(1 - x, y), (x, 1 - y), (1 - x, 1 - y)]


HBM = pl.BlockSpec(memory_space=pl.ANY)


def _cast_shard(w, chip_arr, name):
    r, cols = w.shape
    rh = r // 2
    tr = _row_tile(rh, cols * 4, STREAM_TILE)
    nt = rh // tr

    def body(chip_ref, w_ref, o_ref):
        del chip_ref
        o_ref[...] = w_ref[...].astype(BF16)

    return pl.pallas_call(
        body, name=name,
        grid_spec=pltpu.PrefetchScalarGridSpec(
            num_scalar_prefetch=1, grid=(2, nt),
            in_specs=[pl.BlockSpec((tr, cols), lambda h, i, chip_ref: (h * nt + i, 0))],
            out_specs=pl.BlockSpec((None, None, tr, cols), lambda h, i, chip_ref: (chip_ref[0], h, i, 0))),
        out_shape=pltpu.HBM((N_CHIPS, 2, rh, cols), BF16), compiler_params=_params(VMEM_STREAM_MIB),
    )(chip_arr, w)


HBM_SPEC = pl.BlockSpec(memory_space=pltpu.HBM)
SEM_SPEC = pl.BlockSpec(memory_space=pltpu.SEMAPHORE)
EFFECT = pltpu.SideEffectType.DATAFLOW_SIDE_EFFECTING
TOKEN = jax.ShapeDtypeStruct((8, 128), F32)
TOKEN_SPEC = pl.BlockSpec(memory_space=pltpu.VMEM)


def _in_hbm(arrays):
    return [pltpu.with_memory_space_constraint(a, pltpu.HBM) for a in arrays]


def _hbm_like(arrays):
    return [pltpu.HBM(a.shape, a.dtype) for a in arrays]


def _sems(n):
    return pltpu.SemaphoreType.DMA((n,))


def _remote(src, dst, send_sem, recv_sem, to):
    return pltpu.make_async_remote_copy(src_ref=src, dst_ref=dst, send_sem=send_sem, recv_sem=recv_sem,
                                        device_id=to, device_id_type=MESH)


ALL_FLIPS = (0, 1, 2)


def _gather_start(bufs, groups, after, name):
    n = len(bufs)
    ng = len(groups)

    def body(*refs):
        ins = refs[:n]
        sems = refs[n + len(after):n + len(after) + 2 * ng]
        x, y, c = _place()
        me = 2 * x + y
        chips = _other_chips(x, y)
        for g, (ws, flips) in enumerate(groups):
            for i, w in enumerate(ws):
                for jj, j in enumerate(flips):
                    k = len(flips) * i + jj
                    mine = ins[w].at[me, c]
                    _remote(mine, mine, sems[2 * g].at[k], sems[2 * g + 1].at[k], (*chips[j], c)).start()

    sem_shapes = []
    for ws, flips in groups:
        sem_shapes += [_sems(len(flips) * len(ws))] * 2
    res = pl.pallas_call(
        body, name=name, in_specs=[HBM_SPEC] * n + [HBM] * len(after),
        out_specs=[SEM_SPEC] * (2 * ng) + [HBM_SPEC] * n, out_shape=sem_shapes + _hbm_like(bufs),
        input_output_aliases={w: 2 * ng + w for w in range(n)},
        compiler_params=pltpu.CompilerParams(has_side_effects=EFFECT),
    )(*_in_hbm(bufs), *after)
    return [(res[2 * g], res[2 * g + 1]) for g in range(ng)], list(res[2 * ng:])


def _gather_forward(bufs, recv, after, name, flips=ALL_FLIPS):
    m = len(bufs)
    nf = len(flips)

    def body(*refs):
        ins, recv_in = refs[:m], refs[m]
        fsend, frecv = refs[m + 1 + len(after)], refs[m + 2 + len(after)]
        x, y, c = _place()
        chips = _other_chips(x, y)
        for jj, j in enumerate(flips):
            cx, cy = chips[j]
            for i in range(m):
                landed = ins[i].at[2 * cx + cy, c]
                k = nf * i + jj
                _remote(landed, landed, fsend.at[k], recv_in.at[k], (cx, cy, c)).wait_recv()
                _remote(landed, landed, fsend.at[k], frecv.at[k], (x, y, 1 - c)).start()

    res = pl.pallas_call(
        body, name=name, in_specs=[HBM_SPEC] * m + [SEM_SPEC] + [HBM] * len(after),
        out_specs=[SEM_SPEC, SEM_SPEC] + [HBM_SPEC] * m, out_shape=[_sems(nf * m), _sems(nf * m)] + _hbm_like(bufs),
        input_output_aliases={i: 2 + i for i in range(m)},
        compiler_params=pltpu.CompilerParams(has_side_effects=EFFECT),
    )(*bufs, recv, *after)
    return (res[0], res[1]), list(res[2:])


NEIGHBOURS = (0, 1)


def _relay_partner(x, y, c):
    return 2 * (x ^ (1 - c)) + (y ^ c), (x ^ c, y ^ (1 - c))


def _gather_forward_relay(bufs, recv, after, name):
    m = len(bufs)
    nf = len(NEIGHBOURS)

    def body(*refs):
        ins, recv_in = refs[:m], refs[m]
        fsend, frecv, rsend, rrecv = refs[m + 1 + len(after):m + 5 + len(after)]
        x, y, c = _place()
        chips = _other_chips(x, y)
        for jj, j in enumerate(NEIGHBOURS):
            cx, cy = chips[j]
            for i in range(m):
                landed = ins[i].at[2 * cx + cy, c]
                k = nf * i + jj
                _remote(landed, landed, fsend.at[k], recv_in.at[k], (cx, cy, c)).wait_recv()
        row, (px, py) = _relay_partner(x, y, c)
        for i in range(m):
            relayed = ins[i].at[row, c]
            _remote(relayed, relayed, rsend.at[i], rrecv.at[i], (px, py, c)).start()
        for jj, j in enumerate(NEIGHBOURS):
            cx, cy = chips[j]
            for i in range(m):
                landed = ins[i].at[2 * cx + cy, c]
                k = nf * i + jj
                _remote(landed, landed, fsend.at[k], frecv.at[k], (x, y, 1 - c)).start()

    res = pl.pallas_call(
        body, name=name, in_specs=[HBM_SPEC] * m + [SEM_SPEC] + [HBM] * len(after),
        out_specs=[SEM_SPEC] * 4 + [HBM_SPEC] * m,
        out_shape=[_sems(nf * m), _sems(nf * m), _sems(m), _sems(m)] + _hbm_like(bufs),
        input_output_aliases={i: 4 + i for i in range(m)},
        compiler_params=pltpu.CompilerParams(has_side_effects=EFFECT),
    )(*bufs, recv, *after)
    return tuple(res[:4]), list(res[4:])


def _gather_forward_diag(bufs, rrecv, after, name):
    m = len(bufs)

    def body(*refs):
        ins, rrecv_in = refs[:m], refs[m]
        dsend, drecv = refs[m + 1 + len(after)], refs[m + 2 + len(after)]
        x, y, c = _place()
        diag = (2 * x + y) ^ 3
        _, (px, py) = _relay_partner(x, y, c)
        for i in range(m):
            landed = ins[i].at[diag, c]
            _remote(landed, landed, dsend.at[i], rrecv_in.at[i], (px, py, c)).wait_recv()
            _remote(landed, landed, dsend.at[i], drecv.at[i], (x, y, 1 - c)).start()

    res = pl.pallas_call(
        body, name=name, in_specs=[HBM_SPEC] * m + [SEM_SPEC] + [HBM] * len(after),
        out_specs=[SEM_SPEC, SEM_SPEC] + [HBM_SPEC] * m, out_shape=[_sems(m), _sems(m)] + _hbm_like(bufs),
        input_output_aliases={i: 2 + i for i in range(m)},
        compiler_params=pltpu.CompilerParams(has_side_effects=EFFECT),
    )(*bufs, rrecv, *after)
    return (res[0], res[1]), list(res[2:])


def _gather_finish_diag(bufs, rsend, dsend, drecv, after, name):
    m = len(bufs)

    def body(*refs):
        ins = refs[:m]
        rsend_in, dsend_in, drecv_in = refs[m:m + 3]
        x, y, c = _place()
        diag = (2 * x + y) ^ 3
        row, (px, py) = _relay_partner(x, y, c)
        for i in range(m):
            relayed = ins[i].at[row, c]
            _remote(relayed, relayed, rsend_in.at[i], drecv_in.at[i], (px, py, c)).wait_send()
            landed = ins[i].at[diag, c]
            _remote(landed, landed, dsend_in.at[i], drecv_in.at[i], (x, y, 1 - c)).wait_send()
            theirs = ins[i].at[diag, 1 - c]
            _remote(theirs, theirs, dsend_in.at[i], drecv_in.at[i], (x, y, 1 - c)).wait_recv()

    return list(pl.pallas_call(
        body, name=name, in_specs=[HBM_SPEC] * m + [SEM_SPEC] * 3 + [HBM] * len(after),
        out_specs=[HBM_SPEC] * m, out_shape=_hbm_like(bufs),
        input_output_aliases={i: i for i in range(m)},
        compiler_params=pltpu.CompilerParams(has_side_effects=EFFECT),
    )(*bufs, rsend, dsend, drecv, *after))


def _gather_finish(bufs, send, fsend, frecv, after, name, flips=ALL_FLIPS):
    m = len(bufs)
    nf = len(flips)

    def body(*refs):
        ins = refs[:m]
        send_in, fsend_in, frecv_in = refs[m:m + 3]
        x, y, c = _place()
        me = 2 * x + y
        chips = _other_chips(x, y)
        for jj, j in enumerate(flips):
            cx, cy = chips[j]
            cj = 2 * cx + cy
            for i in range(m):
                k = nf * i + jj
                mine = ins[i].at[me, c]
                _remote(mine, mine, send_in.at[k], frecv_in.at[k], (cx, cy, c)).wait_send()
                landed = ins[i].at[cj, c]
                _remote(landed, landed, fsend_in.at[k], frecv_in.at[k], (x, y, 1 - c)).wait_send()
                theirs = ins[i].at[cj, 1 - c]
                _remote(theirs, theirs, fsend_in.at[k], frecv_in.at[k], (x, y, 1 - c)).wait_recv()

    return list(pl.pallas_call(
        body, name=name, in_specs=[HBM_SPEC] * m + [SEM_SPEC] * 3 + [HBM] * len(after),
        out_specs=[HBM_SPEC] * m, out_shape=_hbm_like(bufs),
        input_output_aliases={i: i for i in range(m)},
        compiler_params=pltpu.CompilerParams(has_side_effects=EFFECT),
    )(*bufs, send, fsend, frecv, *after))


def _pair_exchange(grads, name):
    n = len(grads)

    def body(*refs):
        ins, outs = refs[:n], refs[n:2 * n]
        send_sems, recv_sems = refs[2 * n:]
        x, y, c = _place()
        sibling = (x, y, 1 - c)
        cps = []
        for w in range(n):
            for j in range(N_CHIPS):
                cp = pltpu.make_async_remote_copy(
                    src_ref=ins[w].at[j, 1 - c], dst_ref=outs[w].at[j], send_sem=send_sems.at[N_CHIPS * w + j],
                    recv_sem=recv_sems.at[N_CHIPS * w + j], device_id=sibling, device_id_type=MESH)
                cp.start()
                cps.append(cp)
        for cp in cps:
            cp.wait()

    return pl.pallas_call(
        body, name=name, in_specs=[HBM] * n, out_specs=[HBM] * n,
        out_shape=[jax.ShapeDtypeStruct((N_CHIPS,) + a.shape[2:], a.dtype) for a in grads],
        scratch_shapes=[pltpu.SemaphoreType.DMA((N_CHIPS * n,)), pltpu.SemaphoreType.DMA((N_CHIPS * n,))],
    )(*grads)


def _pair_start(grads, name):
    n = len(grads)
    lands = [lax.empty((N_CHIPS,) + a.shape[2:], a.dtype) for a in grads]

    def body(*refs):
        ins, land_in = refs[:n], refs[n:2 * n]
        send, recv = refs[2 * n], refs[2 * n + 1]
        token = refs[4 * n + 2]
        x, y, c = _place()
        for w in range(n):
            for j in range(N_CHIPS):
                k = N_CHIPS * w + j
                _remote(ins[w].at[j, 1 - c], land_in[w].at[j], send.at[k], recv.at[k], (x, y, 1 - c)).start()
        token[...] = jnp.zeros_like(token)

    res = pl.pallas_call(
        body, name=name, in_specs=[HBM_SPEC] * (2 * n),
        out_specs=[SEM_SPEC, SEM_SPEC] + [HBM_SPEC] * (2 * n) + [TOKEN_SPEC],
        out_shape=[_sems(N_CHIPS * n), _sems(N_CHIPS * n)] + _hbm_like(grads) + _hbm_like(lands) + [TOKEN],
        input_output_aliases={i: 2 + i for i in range(2 * n)},
        compiler_params=pltpu.CompilerParams(has_side_effects=EFFECT),
    )(*_in_hbm(grads), *_in_hbm(lands))
    return (res[0], res[1]), list(res[2:2 + n]), list(res[2 + n:2 + 2 * n]), res[2 + 2 * n]


def _pair_wait(sems, grads, lands, after, name):
    n = len(grads)

    def body(*refs):
        ins, land_in = refs[:n], refs[n:2 * n]
        send_in, recv_in = refs[2 * n], refs[2 * n + 1]
        x, y, c = _place()
        for w in range(n):
            for j in range(N_CHIPS):
                k = N_CHIPS * w + j
                cp = _remote(ins[w].at[j, 1 - c], land_in[w].at[j], send_in.at[k], recv_in.at[k], (x, y, 1 - c))
                cp.wait_send()
                cp.wait_recv()

    res = pl.pallas_call(
        body, name=name, in_specs=[HBM_SPEC] * (2 * n) + [SEM_SPEC, SEM_SPEC] + [HBM] * len(after),
        out_specs=[HBM_SPEC] * (2 * n), out_shape=_hbm_like(grads) + _hbm_like(lands),
        input_output_aliases={i: i for i in range(2 * n)},
        compiler_params=pltpu.CompilerParams(has_side_effects=EFFECT),
    )(*grads, *lands, sems[0], sems[1], *after)
    return list(res[:n]), list(res[n:])


def _chip_start(sums, name, to_all=()):
    m = len(sums)
    lands = [lax.empty(((N_CHIPS,) if i in to_all else ()) + a.shape, a.dtype) for i, a in enumerate(sums)]

    def body(*refs):
        ins, land_in = refs[:m], refs[m:2 * m]
        send, recv = refs[2 * m], refs[2 * m + 1]
        token = refs[4 * m + 2]
        x, y, c = _place()
        me = 2 * x + y
        for i in sorted(range(m), key=lambda i: i not in to_all):
            for j, (cx, cy) in enumerate(_other_chips(x, y)):
                src = ins[i] if i in to_all else ins[i].at[2 * cx + cy]
                _remote(src, land_in[i].at[me], send.at[3 * i + j], recv.at[3 * i + j], (cx, cy, c)).start()
        token[...] = jnp.zeros_like(token)

    res = pl.pallas_call(
        body, name=name, in_specs=[HBM_SPEC] * (2 * m),
        out_specs=[SEM_SPEC, SEM_SPEC] + [HBM_SPEC] * (2 * m) + [TOKEN_SPEC],
        out_shape=[_sems(3 * m), _sems(3 * m)] + _hbm_like(sums) + _hbm_like(lands) + [TOKEN],
        input_output_aliases={i: 2 + i for i in range(2 * m)},
        compiler_params=pltpu.CompilerParams(has_side_effects=EFFECT),
    )(*_in_hbm(sums), *_in_hbm(lands))
    return (res[0], res[1]), list(res[2:2 + m]), list(res[2 + m:2 + 2 * m]), res[2 + 2 * m]


def _chip_wait(sems, sums, lands, after, name, to_all=()):
    m = len(sums)

    def body(*refs):
        ins, land_in = refs[:m], refs[m:2 * m]
        send_in, recv_in = refs[2 * m], refs[2 * m + 1]
        x, y, c = _place()
        for i in range(m):
            for j, (cx, cy) in enumerate(_other_chips(x, y)):
                cj = 2 * cx + cy
                src = ins[i] if i in to_all else ins[i].at[cj]
                cp = _remote(src, land_in[i].at[cj], send_in.at[3 * i + j], recv_in.at[3 * i + j], (cx, cy, c))
                cp.wait_send()
                cp.wait_recv()

    res = pl.pallas_call(
        body, name=name, in_specs=[HBM_SPEC] * (2 * m) + [SEM_SPEC, SEM_SPEC] + [HBM] * len(after),
        out_specs=[HBM_SPEC] * (2 * m), out_shape=_hbm_like(sums) + _hbm_like(lands),
        input_output_aliases={i: i for i in range(2 * m)},
        compiler_params=pltpu.CompilerParams(has_side_effects=EFFECT),
    )(*sums, *lands, sems[0], sems[1], *after)
    return list(res[:m]), list(res[m:])


def _half_parts(bufs):
    parts = []
    for w, a in enumerate(bufs):
        parts += [(w, None)] if a.ndim == 3 else [(w, j) for j in range(a.shape[0])]
    return parts


def _half_ref(refs, w, j, h):
    return refs[w].at[h] if j is None else refs[w].at[j, h]


def _half_start(bufs, name):
    n = len(bufs)
    parts = _half_parts(bufs)

    def body(*refs):
        ins = refs[:n]
        send, recv = refs[n], refs[n + 1]
        x, y, c = _place()
        for k, (w, j) in enumerate(parts):
            mine = _half_ref(ins, w, j, c)
            _remote(mine, mine, send.at[k], recv.at[k], (x, y, 1 - c)).start()

    res = pl.pallas_call(
        body, name=name, in_specs=[HBM_SPEC] * n, out_specs=[SEM_SPEC, SEM_SPEC] + [HBM_SPEC] * n,
        out_shape=[_sems(len(parts)), _sems(len(parts))] + _hbm_like(bufs),
        input_output_aliases={w: 2 + w for w in range(n)},
        compiler_params=pltpu.CompilerParams(has_side_effects=EFFECT),
    )(*_in_hbm(bufs))
    return (res[0], res[1]), list(res[2:])


def _half_wait(sems, bufs, after, name):
    n = len(bufs)
    parts = _half_parts(bufs)

    def body(*refs):
        ins = refs[:n]
        send_in, recv_in = refs[n], refs[n + 1]
        x, y, c = _place()
        for k, (w, j) in enumerate(parts):
            mine = _half_ref(ins, w, j, c)
            _remote(mine, mine, send_in.at[k], recv_in.at[k], (x, y, 1 - c)).wait_send()
            theirs = _half_ref(ins, w, j, 1 - c)
            _remote(theirs, theirs, send_in.at[k], recv_in.at[k], (x, y, 1 - c)).wait_recv()

    return list(pl.pallas_call(
        body, name=name, in_specs=[HBM_SPEC] * n + [SEM_SPEC, SEM_SPEC] + [HBM] * len(after),
        out_specs=[HBM_SPEC] * n, out_shape=_hbm_like(bufs),
        input_output_aliases={w: w for w in range(n)},
        compiler_params=pltpu.CompilerParams(has_side_effects=EFFECT),
    )(*bufs, sems[0], sems[1], *after))


def _all_gather8(block, name, after=()):
    def body(in_ref, *rest):
        out_ref, send_sems, recv_sems, local_sem = rest[len(after):]
        x, y, c = _place()
        me = 4 * x + 2 * y + c
        mine = pltpu.make_async_copy(in_ref, out_ref.at[me], local_sem)
        mine.start()
        flips = [(fx, fy, fc) for fx in (0, 1) for fy in (0, 1) for fc in (0, 1)][1:]
        cps = []
        for k, (fx, fy, fc) in enumerate(flips):
            cp = pltpu.make_async_remote_copy(
                src_ref=in_ref, dst_ref=out_ref.at[me], send_sem=send_sems.at[k], recv_sem=recv_sems.at[k],
                device_id=(x ^ fx, y ^ fy, c ^ fc), device_id_type=MESH)
            cp.start()
            cps.append(cp)
        for k, (fx, fy, fc) in enumerate(flips):
            px, py, pc = x ^ fx, y ^ fy, c ^ fc
            theirs = out_ref.at[4 * px + 2 * py + pc]
            pltpu.make_async_remote_copy(
                src_ref=theirs, dst_ref=theirs, send_sem=send_sems.at[k], recv_sem=recv_sems.at[k],
                device_id=(px, py, pc), device_id_type=MESH).wait_recv()
        for cp in cps:
            cp.wait_send()
        mine.wait()

    return pl.pallas_call(
        body, name=name, in_specs=[HBM] * (1 + len(after)), out_specs=HBM,
        out_shape=jax.ShapeDtypeStruct((N_DEV,) + block.shape, block.dtype),
        scratch_shapes=[pltpu.SemaphoreType.DMA((N_DEV - 1,)), pltpu.SemaphoreType.DMA((N_DEV - 1,)),
                        pltpu.SemaphoreType.DMA],
    )(block, *after)


def _pair_sum(grad, recv, c_arr, name):
    _, _, rh, cols = grad.shape
    tr = _row_tile(rh, cols * grad.dtype.itemsize, STREAM_TILE // 2)

    def body(c_ref, g_ref, r_ref, o_ref):
        del c_ref
        o_ref[...] = (g_ref[...].astype(F32) + r_ref[...].astype(F32)).astype(o_ref.dtype)

    spec = pl.BlockSpec((None, tr, cols), lambda j, i, c_ref: (j, i, 0))
    return pl.pallas_call(
        body, name=name,
        grid_spec=pltpu.PrefetchScalarGridSpec(
            num_scalar_prefetch=1, grid=(N_CHIPS, rh // tr),
            in_specs=[pl.BlockSpec((None, None, tr, cols), lambda j, i, c_ref: (j, c_ref[0], i, 0)), spec],
            out_specs=spec),
        out_shape=pltpu.HBM(recv.shape, grad.dtype), compiler_params=_params(VMEM_STREAM_MIB),
    )(c_arr, grad, recv)


def _sum_by_chip(chip, p_ref, own_ref, o_ref):
    o_ref[...] = jnp.zeros_like(o_ref)
    for k in range(N_CHIPS):
        @pl.when(chip == k)
        def _():
            o_ref[...] += own_ref[...].astype(F32)

        @pl.when(chip != k)
        def _(k=k):
            o_ref[...] += p_ref[k].astype(F32)


def _chip_sum_all(parts, own, place_arr, name):
    _, nj, rh, cols = parts.shape
    tr = _row_tile(rh, cols * 4, STREAM_TILE // 2)

    def body(place_ref, p_ref, own_ref, o_ref):
        _sum_by_chip(place_ref[0], p_ref, own_ref, o_ref)

    return pl.pallas_call(
        body, name=name,
        grid_spec=pltpu.PrefetchScalarGridSpec(
            num_scalar_prefetch=1, grid=(nj, rh // tr),
            in_specs=[pl.BlockSpec((N_CHIPS, None, tr, cols), lambda j, i, place_ref: (0, j, i, 0)),
                      pl.BlockSpec((None, tr, cols), lambda j, i, place_ref: (j, i, 0))],
            out_specs=pl.BlockSpec((None, None, tr, cols), lambda j, i, place_ref: (j, place_ref[1], i, 0))),
        out_shape=pltpu.HBM((nj, 2, rh, cols), F32), compiler_params=_params(VMEM_STREAM_MIB),
    )(place_arr, parts, own)


def _chip_sum(parts, own, place_arr, name):
    _, rh, cols = parts.shape
    tr = _row_tile(rh, cols * 4, STREAM_TILE // 2)

    def body(place_ref, p_ref, own_ref, o_ref):
        _sum_by_chip(place_ref[0], p_ref, own_ref, o_ref)

    return pl.pallas_call(
        body, name=name,
        grid_spec=pltpu.PrefetchScalarGridSpec(
            num_scalar_prefetch=1, grid=(rh // tr,),
            in_specs=[pl.BlockSpec((N_CHIPS, tr, cols), lambda i, place_ref: (0, i, 0)),
                      pl.BlockSpec((None, tr, cols), lambda i, place_ref: (place_ref[0], i, 0))],
            out_specs=pl.BlockSpec((None, tr, cols), lambda i, place_ref: (place_ref[1], i, 0))),
        out_shape=pltpu.HBM((2, rh, cols), F32), compiler_params=_params(VMEM_STREAM_MIB),
    )(place_arr, parts, own)


def _adamw_math(w, g, m, v):
    m = ADAM_B1 * m + (1.0 - ADAM_B1) * g
    v = ADAM_B2 * v + (1.0 - ADAM_B2) * (g * g)
    m_hat = m / (1.0 - ADAM_B1 ** ADAM_STEP)
    v_hat = v / (1.0 - ADAM_B2 ** ADAM_STEP)
    delta = -ADAM_LR * (m_hat / (jnp.sqrt(v_hat) + ADAM_EPS) + ADAM_WD * w)
    return delta, m, v


def _adamw(w, g, m, v, name):
    rows, cols = w.shape
    tr = _row_tile(rows, cols * 4)

    def body(w_ref, g_ref, m_ref, v_ref, go_ref, d_ref, nm_ref, nv_ref):
        gv = g_ref[...]
        go_ref[...] = gv
        d_ref[...], nm_ref[...], nv_ref[...] = _adamw_math(w_ref[...], gv, m_ref[...], v_ref[...])

    spec = pl.BlockSpec((tr, cols), lambda i: (i, 0))
    return pl.pallas_call(
        body, name=name, grid=(rows // tr,), in_specs=[spec] * 4, out_specs=[spec] * 4,
        out_shape=[jax.ShapeDtypeStruct(w.shape, F32)] * 4, compiler_params=_params(VMEM_STREAM_MIB),
    )(w, g, m, v)


def _sum8_adamw_row(parts, w, m, v, name):
    cols = parts.shape[2]

    def body(p_ref, w_ref, m_ref, v_ref, g_ref, d_ref, nm_ref, nv_ref):
        g = p_ref[0, 0:1, :]
        for k in range(1, N_DEV):
            g = g + p_ref[k, 0:1, :]
        g_ref[...] = g
        d_ref[...], nm_ref[...], nv_ref[...] = _adamw_math(w_ref[...], g, m_ref[...], v_ref[...])

    return pl.pallas_call(
        body, name=name, out_shape=[jax.ShapeDtypeStruct((1, cols), F32)] * 4, compiler_params=_params(VMEM_STREAM_MIB),
    )(parts, w, m, v)


def _pack_rows(pieces, rows, name):
    cols = pieces[0].shape[1]
    n = len(pieces)

    def body(*refs):
        o_ref = refs[n]
        o_ref[...] = jnp.zeros_like(o_ref)
        at = 0
        for p_ref in refs[:n]:
            r = p_ref.shape[0]
            o_ref[at:at + r, :] = p_ref[...]
            at += r

    return pl.pallas_call(
        body, name=name, out_shape=jax.ShapeDtypeStruct((rows, cols), F32), compiler_params=_params(VMEM_STREAM_MIB),
    )(*pieces)


def kernel(x, norm_mix_g, w_in, conv_w, conv_b, lru_wa, lru_ba, lru_wx, lru_bx, lru_lambda, w_proj_attn, w_proj_lru, w_out, norm_mlp_g, w_up, w_down, norm_final_g, loss_target, m_norm_mix_g, m_w_in, m_conv_w, m_conv_b, m_lru_wa, m_lru_ba, m_lru_wx, m_lru_bx, m_lru_lambda, m_w_proj_attn, m_w_proj_lru, m_w_out, m_norm_mlp_g, m_w_up, m_w_down, m_norm_final_g, v_norm_mix_g, v_w_in, v_conv_w, v_conv_b, v_lru_wa, v_lru_ba, v_lru_wx, v_lru_bx, v_lru_lambda, v_w_proj_attn, v_w_proj_lru, v_w_out, v_norm_mlp_g, v_w_up, v_w_down, v_norm_final_g):
    s, d = x.shape[1], x.shape[2]
    ff = w_up.shape[2] * N_CHIPS
    heads = d // HEAD_DIM
    u = d // 4
    assert s % (max(DILATIONS) * ATTN_BLK) == 0 and d % (4 * HEAD_DIM) == 0 and ff == 4 * d and DILATIONS[0] == 1
    xs, target = _in_hbm([x[0], loss_target[0]])
    gf = norm_final_g.reshape(1, d)
    wa, wx = lru_wa[0], lru_wx[0]
    core = lax.axis_index("c").astype(jnp.int32)
    chip = (2 * lax.axis_index("x") + lax.axis_index("y")).astype(jnp.int32)
    cidx = core.reshape(1)
    chip_arr = chip.reshape(1)
    place_arr = jnp.stack([chip, core])
    slopes = jnp.broadcast_to(
        (2.0 ** (-8.0 * jnp.arange(1, heads + 1, dtype=F32) / heads))[:, None, None], (heads, 1, HEAD_DIM))

    big = _in_hbm([w_in[0], w_proj_attn[0], w_proj_lru[0], w_out[0], w_up[0], w_down[0]])
    names = ["w_in", "w_proj_attn", "w_proj_lru", "w_out", "w_up", "w_down"]
    cw_pad = jnp.pad(conv_w[0], ((0, 8 - CONV_TAPS), (0, 0)))
    cw_all = _all_gather8(cw_pad, "gather_conv_w")
    conv_w_full = jnp.concatenate([cw_all[2 * j] for j in range(N_CHIPS)], axis=1)
    (sem_a,), buf_a = _gather_start([_cast_shard(big[0], chip_arr, "cast_w_in")], [([0], NEIGHBOURS)], [cw_all],
                                    "gather_start_w_in")

    xn = _rms_fwd(xs, norm_mix_g, "norm_mix")
    flip_bits = (2, 1, 3)

    def w_in_view():
        return buf_a[0].reshape(N_CHIPS, d, N_SLOTS * u)

    proj = _proj_in_shard(xn, w_in_view(), chip_arr, None, "proj_in_own")
    bufs = [_cast_shard(w, chip_arr, "cast_" + nm) for w, nm in zip(big[1:], names[1:])]
    (fs, fr, rs, rr), buf_a = _gather_forward_relay(buf_a, sem_a[1], [proj] + bufs, "gather_forward_w_in")
    buf_a = _gather_finish(buf_a, sem_a[0], fs, fr, [], "gather_finish_w_in", flips=NEIGHBOURS)
    for j in NEIGHBOURS:
        proj = _proj_in_shard(xn, w_in_view(), chip_arr ^ flip_bits[j], proj, "proj_in_from_%d" % j)
    (sem_b, sem_c, sem_d), bufs = _gather_start(
        bufs, [([0, 1, 2], ALL_FLIPS), ([3], ALL_FLIPS), ([4], ALL_FLIPS)], [proj], "gather_start_rest")
    (ds, dr), buf_a = _gather_forward_diag(buf_a, rr, [proj, bufs[0]], "gather_forward_w_in_diag")
    buf_a = _gather_finish_diag(buf_a, rs, ds, dr, [], "gather_finish_w_in_diag")
    proj = _proj_in_shard(xn, w_in_view(), chip_arr ^ flip_bits[2], proj, "proj_in_from_2")
    w_in_g = w_in_view()
    proj_a = proj_b = proj
    y_attn, lse = _attn_fwd(proj_a, slopes)
    y_lru, h_lru = _lru_fwd(proj_b, conv_w_full, conv_b, wa, lru_ba, wx, lru_bx, lru_lambda)
    fsem_b, buf_b = _gather_forward(bufs[:3], sem_b[1], [y_attn, y_lru], "gather_forward_proj")
    buf_b = _gather_finish(buf_b, sem_b[0], fsem_b[0], fsem_b[1], [], "gather_finish_proj")
    wpa_g = buf_b[0].reshape(d, d)
    wpl_g = buf_b[1].reshape(d, d)
    wout_g = buf_b[2].reshape(d, d)

    tn = u
    sd_f32 = jax.ShapeDtypeStruct((s, d), F32)
    sd_bf16 = jax.ShapeDtypeStruct((s, d), BF16)
    col = pl.BlockSpec((s, tn), lambda i, j, k: (0, j))

    def slot(n):
        return pl.BlockSpec((None, s, tn), lambda i, j, k: (n, 0, j))

    p_attn = _mm_nn("proj_attn", y_attn, wpa_g, [], [], [sd_bf16], [col], _store, tn)[0]
    fsem_c, buf_c = _gather_forward(bufs[3:4], sem_c[1], [p_attn], "gather_forward_w_up")

    def merge(acc, extras, outs):
        pa_ref, ga_ref, gl_ref = extras
        merged = _sigmoid(ga_ref[...]) * pa_ref[...].astype(F32) + _sigmoid(gl_ref[...]) * acc
        outs[0][...] = merged.astype(BF16)
        outs[1][...] = acc.astype(BF16)

    tn2 = max(HEAD_DIM, u // 2)
    col2 = pl.BlockSpec((s, tn2), lambda i, j, k: (0, j))

    def slot2(n):
        return pl.BlockSpec((None, s, tn2), lambda i, j, k: (n, 0, j))

    merged, p_lru = _mm_nn("proj_lru_merge", y_lru, wpl_g, [p_attn, proj, proj], [col2, slot2(5), slot2(6)],
                           [sd_bf16, sd_bf16], [col2, col2], merge, tn2)

    def add_resid(acc, extras, outs):
        outs[0][...] = extras[0][...] + acc

    h1 = _mm_nn("w_out_resid", merged, wout_g, [xs], [col], [sd_f32], [col], add_resid, tn)[0]
    hn = _rms_fwd(h1, norm_mlp_g, "norm_mlp")
    buf_c = _gather_finish(buf_c, sem_c[0], fsem_c[0], fsem_c[1], [hn], "gather_finish_w_up")
    wup_g = buf_c[0].reshape(N_CHIPS, d, d)

    def relu_sq(acc, extras, outs):
        r = jnp.maximum(acc, 0.0)
        outs[0][...] = (r * r).astype(BF16)
        outs[1][...] = r.astype(BF16)

    sf_bf16 = jax.ShapeDtypeStruct((s, ff), BF16)
    hid, relu_up = _mm(
        "w_up_relu2", [hn, wup_g],
        [pl.BlockSpec((s, d), lambda i, j, k: (0, 0)),
         pl.BlockSpec((None, d, tn), lambda i, j, k: (j // 4, 0, j % 4))],
        [sf_bf16, sf_bf16], [col, col], (1, ff // tn, 1), NN, relu_sq)
    fsem_d, buf_d = _gather_forward(bufs[4:], sem_d[1], [hid], "gather_forward_w_down")
    wdown_g = _gather_finish(buf_d, sem_d[0], fsem_d[0], fsem_d[1], [], "gather_finish_w_down")[0].reshape(ff, d)
    h2 = _mm(
        "w_down_resid", [hid, wdown_g, h1],
        [pl.BlockSpec((s, d), lambda i, j, k: (0, k)), pl.BlockSpec((d, tn), lambda i, j, k: (k, j)), col],
        [sd_f32], [col], (1, d // tn, ff // d), NN, add_resid, nk=ff // d, acc_shape=(s, tn))[0]
    loss_part, dh2, dh2_b, d_gf = _loss_head(h2, gf, target)
    loss = lax.psum(loss_part[0, 0], AXES)

    def relu_sq_bwd(acc, extras, outs):
        outs[0][...] = (acc * (2.0 * extras[0][...].astype(F32))).astype(BF16)

    dup = _mm_nt("d_hid", dh2_b, wdown_g, [relu_up], [col], [sf_bf16], [col], relu_sq_bwd, tn)[0]
    tok_d = pl.BlockSpec((s, d), lambda i, j: (0, 0))
    g_wdown = _mm_tn(
        "g_w_down", hid, dh2_b, pl.BlockSpec((s, d), lambda i, j: (0, i)),
        pl.BlockSpec((s, tn), lambda i, j: (0, j)), jax.ShapeDtypeStruct((ff, d), BF16),
        pl.BlockSpec((d, tn), lambda i, j: (i, j)), (ff // d, d // tn), d, tn, s)
    dhn = _mm(
        "d_hn", [dup, wup_g],
        [pl.BlockSpec((s, d), lambda i, j, k: (0, k)), pl.BlockSpec((None, tn, d), lambda i, j, k: (k, j, 0))],
        [sd_f32], [col], (1, d // tn, ff // d), NT, _store, nk=ff // d, acc_shape=(s, tn))[0]
    g_wup = _mm_tn(
        "g_w_up", hn, dup, tok_d, pl.BlockSpec((s, tn), lambda i, j: (0, j)),
        jax.ShapeDtypeStruct((N_CHIPS, d, d), BF16), pl.BlockSpec((None, d, tn), lambda i, j: (j // 4, 0, j % 4)),
        (1, ff // tn), d, tn, s)
    big_m = _in_hbm([m_w_in[0], m_w_proj_attn[0], m_w_proj_lru[0], m_w_out[0], m_w_up[0], m_w_down[0]])
    big_v = _in_hbm([v_w_in[0], v_w_proj_attn[0], v_w_proj_lru[0], v_w_out[0], v_w_up[0], v_w_down[0]])
    big_out = {}

    def reduce_begin(ids, gs, tag, everywhere=None):
        g4 = [g.reshape(N_CHIPS, 2, big[i].shape[0] // 2, big[i].shape[1]) for i, g in zip(ids, gs)]
        tags = [names[i] for i in ids]
        if everywhere is not None:
            g4.append(everywhere.reshape(N_CHIPS, 2, everywhere.shape[0] // (2 * N_CHIPS), everywhere.shape[1]))
            tags.append("small_" + tag)
        from_sibling = _pair_exchange(g4, "pair_exchange_" + tag)
        sums = [_pair_sum(g, r, cidx, "pair_sum_" + t) for t, g, r in zip(tags, g4, from_sibling)]
        to_all = () if everywhere is None else (len(ids),)
        return _chip_start(sums, "chip_start_" + tag, to_all), to_all

    def pair_begin(ids, gs, tag):
        g4 = [g.reshape(N_CHIPS, 2, big[i].shape[0] // 2, big[i].shape[1]) for i, g in zip(ids, gs)]
        return _pair_start(g4, "pair_start_" + tag)

    def reduce_begin_paired(ids, paired, after, tag):
        sems, g4, lands, _ = paired
        g4, from_sibling = _pair_wait(sems, g4, lands, after, "pair_wait_" + tag)
        sums = [_pair_sum(g, r, cidx, "pair_sum_" + names[i]) for i, g, r in zip(ids, g4, from_sibling)]
        return _chip_start(sums, "chip_start_" + tag), ()

    def reduce_mid(ids, begun, after, tag):
        (sems, sums, lands, _), to_all = begun
        sums, lands = _chip_wait(sems, sums, lands, after, "chip_wait_" + tag, to_all)
        halves = [_chip_sum(p, own, place_arr, "chip_sum_" + names[i]) for i, p, own in zip(ids, lands, sums)]
        if to_all:
            halves.append(_chip_sum_all(lands[-1], sums[-1], place_arr, "chip_sum_small_" + tag))
        return _half_start(halves, "half_start_" + tag), to_all

    def reduce_end(ids, mid, after, tag):
        (hsems, halves), to_all = mid
        full = _half_wait(hsems, halves, after, "half_wait_" + tag)
        done = []
        for i, g in zip(ids, full):
            res = _adamw(big[i], g.reshape(big[i].shape), big_m[i], big_v[i], "adamw_" + names[i])
            big_out[names[i]] = tuple(a[None] for a in res)
            done.append(res[1])
        everywhere = full[-1].reshape(-1, full[-1].shape[-1]) if to_all else None
        return everywhere, done

    def after_token(a, begun):
        return a + begun[0][3][:1, :1]

    pair_mlp = pair_begin([4, 5], [g_wup, g_wdown], "mlp")
    dh1, dh1_b, d_gmlp = _rms_bwd(h1, norm_mlp_g + pair_mlp[3][:1, :1], dhn, dh2, "norm_mlp_bwd")

    g_wout = _mm_tn(
        "g_w_out", merged, dh1_b, tok_d, pl.BlockSpec((s, tn), lambda i, j: (0, j)),
        jax.ShapeDtypeStruct((d, d), BF16), pl.BlockSpec((d, tn), lambda i, j: (0, j)), (1, d // tn), d, tn, s)

    def merge_bwd(acc, extras, outs):
        pa_ref, pl_ref, ga_ref, gl_ref = extras
        sa, sl = _sigmoid(ga_ref[...]), _sigmoid(gl_ref[...])
        outs[0][...] = (acc * sa).astype(BF16)
        outs[1][...] = (acc * sl).astype(BF16)
        outs[2][0] = (acc * pa_ref[...].astype(F32) * (sa * (1.0 - sa))).astype(BF16)
        outs[2][1] = (acc * pl_ref[...].astype(F32) * (sl * (1.0 - sl))).astype(BF16)

    nb = N_SLOTS - N_QKV
    d_pa, d_pl, dproj_b = _mm_nt(
        "d_merged", dh1_b, wout_g, [p_attn, p_lru, proj, proj], [col2, col2, slot2(5), slot2(6)],
        [sd_bf16, sd_bf16, jax.ShapeDtypeStruct((nb, s, d), BF16)],
        [col2, col2, pl.BlockSpec((2, s, tn2), lambda i, j, k: (1, 0, j))], merge_bwd, tn2)
    red_mlp = reduce_begin_paired([4, 5], pair_mlp, [d_pa], "mlp")
    dy_attn = _mm_nt("d_y_attn", d_pa, wpa_g, [], [], [sd_f32], [col], _store, tn)[0]
    dy_lru = _mm_nt("d_y_lru", d_pl, wpl_g, [], [], [sd_f32], [col], _store, tn)[0]
    g_wpa = _mm_tn(
        "g_w_proj_attn", y_attn, d_pa, tok_d, pl.BlockSpec((s, tn), lambda i, j: (0, j)),
        jax.ShapeDtypeStruct((d, d), BF16), pl.BlockSpec((d, tn), lambda i, j: (0, j)), (1, d // tn), d, tn, s)
    g_wpl = _mm_tn(
        "g_w_proj_lru", y_lru, d_pl, tok_d, pl.BlockSpec((s, tn), lambda i, j: (0, j)),
        jax.ShapeDtypeStruct((d, d), BF16), pl.BlockSpec((d, tn), lambda i, j: (0, j)), (1, d // tn), d, tn, s)

    pair_proj = pair_begin([1, 2, 3], [g_wpa, g_wpl, g_wout], "proj")

    dproj_b, d_cw, d_cb, d_wa, d_ba, d_wx, d_bx, d_lam = _lru_bwd(
        proj_b, h_lru, dy_lru, conv_w_full, conv_b, wa, lru_ba, wx, lru_bx, lru_lambda + pair_proj[3][:1, :1],
        dproj_b)
    red_proj = reduce_begin_paired([1, 2, 3], pair_proj, [dproj_b], "proj")
    dproj_a = _attn_bwd(proj_a, after_token(slopes, red_proj), y_attn, lse, dy_attn)
    per = N_SLOTS
    g_win_shape = jax.ShapeDtypeStruct((N_CHIPS, d, N_SLOTS * u), BF16)

    def g_win_part(name, dproj, first, prev):
        n_units = 4 * dproj.shape[0]
        return _mm_tn(
            name, xn, dproj, tok_d, pl.BlockSpec((None, s, u), lambda i, j: (j // 4, 0, j % 4)),
            g_win_shape, pl.BlockSpec((None, d, u), lambda i, j: ((j + first) // per, 0, (j + first) % per)),
            (1, n_units), d, u, s, aliases=None if prev is None else {2: 0}, extra=prev)

    mat_rows = heads * HEAD_DIM * HEAD_DIM // d
    vec_names = ["norm_mix_g", "conv_b", "lru_ba", "lru_bx", "lru_lambda", "norm_mlp_g", "norm_final_g"]

    def pack(wa_, wx_, cw_, vecs, name):
        rows = [wa_.reshape(mat_rows, d), wx_.reshape(mat_rows, d), cw_] + [a.reshape(1, d) for a in vecs]
        n = sum(a.shape[0] for a in rows)
        return _pack_rows(rows, n + (-n % 64), name)

    zero_cw = jnp.zeros((CONV_TAPS, d), F32)
    small_g = pack(d_wa, d_wx, d_cw, [jnp.zeros((1, d), F32), d_cb, d_ba, d_bx, d_lam, d_gmlp, d_gf], "pack_small_g")
    g_win = g_win_part("g_w_in_qkv", dproj_a, 0, None)
    g_win = g_win_part("g_w_in_rest", dproj_b, 4 * N_QKV, g_win)
    red_in = reduce_begin([0], [g_win], "w_in", everywhere=small_g)
    dxn = _dxn(dproj_a, dproj_b, w_in_g, d, [red_in[0][3]])
    grad_x, _, d_gmix = _rms_bwd(xs, norm_mix_g, dxn, dh1, "norm_mix_bwd")

    mid_mlp = reduce_mid([4, 5], red_mlp, [grad_x], "mlp")
    mid_proj = reduce_mid([1, 2, 3], red_proj, [mid_mlp[0][1][0]], "proj")
    _, done_mlp = reduce_end([4, 5], mid_mlp, [mid_proj[0][1][0]], "mlp")
    _, done_proj = reduce_end([1, 2, 3], mid_proj, done_mlp[-1:], "proj")
    done = done_mlp + done_proj
    small_w = pack(wa, wx, zero_cw, [norm_mix_g, conv_b, lru_ba, lru_bx, lru_lambda, norm_mlp_g, norm_final_g],
                   "pack_small_w")
    small_m = pack(m_lru_wa[0], m_lru_wx[0], zero_cw,
                   [m_norm_mix_g, m_conv_b, m_lru_ba, m_lru_bx, m_lru_lambda, m_norm_mlp_g, m_norm_final_g],
                   "pack_small_m")
    small_v = pack(v_lru_wa[0], v_lru_wx[0], zero_cw,
                   [v_norm_mix_g, v_conv_b, v_lru_ba, v_lru_bx, v_lru_lambda, v_norm_mlp_g, v_norm_final_g],
                   "pack_small_v")
    mid_in = reduce_mid([0], red_in, done + [small_w, small_m, small_v], "w_in")
    gmix_parts = _all_gather8(jnp.pad(d_gmix, ((0, 7), (0, 0))), "gather_gain_grad", [mid_in[0][1][0]])
    gmix_out = _sum8_adamw_row(gmix_parts, norm_mix_g, m_norm_mix_g, v_norm_mix_g, "sum_adamw_norm_mix_g")
    small_sum, _ = reduce_end([0], mid_in, [gmix_out[1]], "w_in")
    small = _adamw(small_w, small_sum, small_m, small_v, "adamw_small")
    g_cw = lax.dynamic_slice(small_sum[2 * mat_rows:2 * mat_rows + CONV_TAPS], (0, chip * u), (CONV_TAPS, u))
    cw_out = _adamw(conv_w[0], g_cw, m_conv_w[0], v_conv_w[0], "adamw_conv_w")

    def small_leaf(kind, name):
        a = small[kind]
        if name == "norm_mix_g":
            return gmix_out[kind]
        if name == "lru_wa":
            return a[0:mat_rows].reshape(lru_wa.shape)
        if name == "lru_wx":
            return a[mat_rows:2 * mat_rows].reshape(lru_wx.shape)
        if name == "conv_w":
            return cw_out[kind][None]
        row = a[2 * mat_rows + CONV_TAPS + vec_names.index(name)]
        return row if name == "norm_final_g" else row[None]

    order = ["norm_mix_g", "w_in", "conv_w", "conv_b", "lru_wa", "lru_ba", "lru_wx", "lru_bx", "lru_lambda",
             "w_proj_attn", "w_proj_lru", "w_out", "norm_mlp_g", "w_up", "w_down", "norm_final_g"]
    outs = [loss, grad_x[None]]
    for kind in range(4):
        for name in order:
            outs.append(big_out[name][kind] if name in big_out else small_leaf(kind, name))
    return tuple(outs)
```
